```python
import math
import jax, jax.numpy as jnp
from jax import lax
import numpy as np

D_MODEL = 1024
BATCH = 4
SEQ = 8192
DEPTH = 4

CTX_LEN = 256
GRID_W = 64
D_MIX = D_MODEL
SHORT_CONV = 3
A_HEADS = 8
A_HEAD_DIM = 64
A_INNER = A_HEADS * A_HEAD_DIM
A_GROUPS = 2
A_STATE = 64
A_XBC = A_INNER + 2 * A_GROUPS * A_STATE
A_CHUNK = 64
B_HEADS = 4
B_QK_DIM = 64
B_V_DIM = 128
B_QK = B_HEADS * B_QK_DIM
B_INNER = B_HEADS * B_V_DIM
B_CHUNK = 64
MLSTM_EPS = 1e-6
C_HEADS = 4
C_HEAD_DIM = 128
C_INNER = C_HEADS * C_HEAD_DIM
C_CHUNK = 16
D_HEADS = 8
D_HEAD_DIM = 64
D_INNER = D_HEADS * D_HEAD_DIM
D_W_LORA = 64
D_A_LORA = 64
D_G_LORA = 128
RWKV_EPS = 64e-5
P_A = A_INNER + A_XBC + 2 * A_HEADS
P_B = 2 * B_QK + 2 * B_INNER + 4 * B_HEADS
P_EVEN = P_A + P_B
P_C = 5 * C_INNER
P_D = 3 * D_INNER + 2 * D_W_LORA + D_A_LORA + D_G_LORA
P_ODD = P_C + P_D
N_EVEN = (DEPTH + 1) // 2
N_ODD = DEPTH // 2
N_EXPERTS = 32
N_EXPERT_GROUPS = 8
EXPERTS_PER_GROUP = N_EXPERTS // N_EXPERT_GROUPS
TOP_K = 2
D_EXPERT = 512
MOE_BLOCK = 128
DEEPNORM_ALPHA = (2 * DEPTH) ** 0.25
DEEPNORM_BETA = (8 * DEPTH) ** -0.25
LN_EPS = 1e-5
M_INIT = -1e30

kernel_name = 'hybrid_ssd_mlstm_hgrn2_rwkv7_moe_flow_trunk'


def layer_norm(x, g, b):
    xf = x.astype(jnp.float32)
    mu = xf.mean(-1, keepdims=True)
    var = jnp.mean(jnp.square(xf - mu), -1, keepdims=True)
    return ((xf - mu) * lax.rsqrt(var + LN_EPS)).astype(x.dtype) * g + b


def rms_norm(x, w, eps=1e-6):
    xf = x.astype(jnp.float32)
    return (xf * lax.rsqrt(jnp.mean(xf * xf, -1, keepdims=True) + eps)).astype(x.dtype) * w


def head_norm(y, eps, center):
    yf = y.astype(jnp.float32)
    if center:
        yf = yf - yf.mean(-1, keepdims=True)
    yf = yf * lax.rsqrt(jnp.mean(yf * yf, -1, keepdims=True) + eps)
    return yf.reshape(y.shape[0], y.shape[1], -1).astype(y.dtype)


def centred_dwconv(x, w, b):
    y = lax.conv_general_dilated(x, w[:, None, :], window_strides=(1,), padding='SAME',
                                 dimension_numbers=('NWC', 'WIO', 'NWC'), feature_group_count=x.shape[-1])
    return y + b


def centred_token_shift(p):
    prev = jnp.pad(p[:, :-1], ((0, 0), (1, 0), (0, 0)))
    nxt = jnp.pad(p[:, 1:], ((0, 0), (0, 1), (0, 0)))
    return 0.5 * (prev + nxt)


def to_col_major(u, rows):
    b, s, d = u.shape
    return u.reshape(b, rows, GRID_W, d).transpose(0, 2, 1, 3).reshape(b, s, d)


def from_col_major(u, rows):
    b, s, d = u.shape
    return u.reshape(b, GRID_W, rows, d).transpose(0, 2, 1, 3).reshape(b, s, d)


def _chunk(a, L):
    b, t, h, d = a.shape
    return a.astype(jnp.float32).reshape(b, t // L, L, h, d).transpose(0, 3, 1, 2, 4)


def _unchunk(a, dtype):
    b, h, n, L, d = a.shape
    return a.transpose(0, 2, 3, 1, 4).reshape(b, n * L, h, d).astype(dtype)


def _flip_t(a):
    return jnp.flip(a, axis=1)


def prefix_bidirectional(scan_fn, ctx_fwd, lat_fwd, ctx_bwd, lat_bwd):
    yc_f, s_f = scan_fn(*ctx_fwd, None)
    yl_f, _ = scan_fn(*lat_fwd, s_f)
    yc_b, s_b = scan_fn(*[_flip_t(a) for a in ctx_bwd], None)
    yl_b, _ = scan_fn(*[_flip_t(a) for a in lat_bwd], s_b)
    return yc_f + _flip_t(yc_b), yl_f + _flip_t(yl_b)


def segsum(a):
    L = a.shape[-1]
    cs = jnp.cumsum(a, axis=-1)
    return jnp.where(jnp.tril(jnp.ones((L, L), bool)), cs[..., :, None] - cs[..., None, :], -jnp.inf)


def ssd_scan(x, log_a, bm, cm, state):
    b, t, h, p = x.shape
    g, n = bm.shape[2], bm.shape[3]
    r = h // g
    L = A_CHUNK
    nc = t // L
    xf = x.astype(jnp.float32).reshape(b, nc, L, g, r, p)
    bf = bm.astype(jnp.float32).reshape(b, nc, L, g, n)
    cf = cm.astype(jnp.float32).reshape(b, nc, L, g, n)
    la = log_a.astype(jnp.float32).reshape(b, nc, L, g, r).transpose(0, 3, 4, 1, 2)
    la_cum = jnp.cumsum(la, axis=-1)
    cb = jnp.einsum('bclgn,bcsgn->bgcls', cf, bf)
    attn = cb[:, :, None] * jnp.exp(segsum(la))
    y_diag = jnp.einsum('bgrcls,bcsgrp->bclgrp', attn, xf)
    decay_end = jnp.exp(la_cum[..., -1:] - la_cum).transpose(0, 3, 4, 1, 2)
    chunk_states = jnp.einsum('bcsgn,bcsgrp->bcgrpn', bf, xf * decay_end[..., None])
    if state is None:
        state = jnp.zeros((b, h, p, n), jnp.float32)
    states = jnp.concatenate([state.reshape(b, 1, g, r, p, n), chunk_states], axis=1)
    chunk_decay = jnp.exp(segsum(jnp.pad(la_cum[..., -1], ((0, 0), (0, 0), (0, 0), (1, 0)))))
    new_states = jnp.einsum('bgrzc,bcgrpn->bzgrpn', chunk_decay, states)
    prev_states, final = new_states[:, :-1], new_states[:, -1]
    y_off = jnp.einsum('bclgn,bcgrpn->bclgrp', cf, prev_states) * jnp.exp(la_cum).transpose(0, 3, 4, 1, 2)[..., None]
    return (y_diag + y_off).reshape(b, t, h, p).astype(x.dtype), final.reshape(b, h, p, n)


def mlstm_scan(q, k, v, log_i, log_f, state):
    b, t, h, dk = q.shape
    dv = v.shape[-1]
    L = B_CHUNK
    qc = _chunk(q, L) * dk ** -0.5
    kc = _chunk(k, L)
    vc = _chunk(v, L)
    li = _chunk(log_i[..., None], L)[..., 0]
    fcum = jnp.cumsum(_chunk(log_f[..., None], L)[..., 0], axis=-1)
    ftot = fcum[..., -1]
    w_end = ftot[..., None] - fcum + li
    m_loc = w_end.max(-1)
    e_end = jnp.exp(w_end - m_loc[..., None])
    c_loc = jnp.einsum('bhnsk,bhnsv->bhnkv', kc * e_end[..., None], vc)
    n_loc = jnp.einsum('bhns,bhnsk->bhnk', e_end, kc)
    if state is None:
        state = (jnp.zeros((b, h, dk, dv), jnp.float32), jnp.zeros((b, h, dk), jnp.float32),
                 jnp.full((b, h), M_INIT, jnp.float32))

    def step(carry, inp):
        cmat, nvec, mst = carry
        ft, ml, cl, nl = inp
        m_new = jnp.maximum(ft + mst, ml)
        sp = jnp.exp(ft + mst - m_new)
        sl = jnp.exp(ml - m_new)
        c_new = sp[..., None, None] * cmat + sl[..., None, None] * cl
        n_new = sp[..., None] * nvec + sl[..., None] * nl
        return (c_new, n_new, m_new), (cmat, nvec, mst)

    xs = tuple(jnp.moveaxis(a, 2, 0) for a in (ftot, m_loc, c_loc, n_loc))
    final, (c_prev, n_prev, m_prev) = lax.scan(step, state, xs)
    c_prev = jnp.moveaxis(c_prev, 0, 2)
    n_prev = jnp.moveaxis(n_prev, 0, 2)
    m_prev = jnp.moveaxis(m_prev, 0, 2)
    causal = jnp.tril(jnp.ones((L, L), bool))
    log_d = jnp.where(causal, fcum[..., :, None] - fcum[..., None, :] + li[..., None, :], -jnp.inf)
    log_inter = fcum + m_prev[..., None]
    m_row = jnp.maximum(log_d.max(-1), log_inter)
    s = jnp.einsum('bhnld,bhnsd->bhnls', qc, kc) * jnp.exp(log_d - m_row[..., None])
    inter = jnp.exp(log_inter - m_row)
    num = jnp.einsum('bhnls,bhnsv->bhnlv', s, vc) + inter[..., None] * jnp.einsum('bhnld,bhndv->bhnlv', qc, c_prev)
    den = s.sum(-1) + inter * jnp.einsum('bhnld,bhnd->bhnl', qc, n_prev)
    hout = num / jnp.maximum(jnp.abs(den), jnp.exp(-m_row))[..., None]
    return _unchunk(hout, v.dtype), final


def gla_scan(q, k, v, log_f, state):
    b, t, h, dk = q.shape
    dv = v.shape[-1]
    L = C_CHUNK
    qc, kc, vc = _chunk(q, L), _chunk(k, L), _chunk(v, L)
    lam = jnp.cumsum(_chunk(log_f, L), axis=3)
    lam_end = lam[:, :, :, -1:]
    q_in = qc * jnp.exp(lam)
    att = jnp.einsum('bhnld,bhnsd->bhnls', q_in, kc * jnp.exp(-lam))
    att = jnp.where(jnp.tril(jnp.ones((L, L), bool)), att, 0.0)
    y_intra = jnp.einsum('bhnls,bhnsv->bhnlv', att, vc)
    k_end = kc * jnp.exp(lam_end - lam)
    if state is None:
        state = jnp.zeros((b, h, dk, dv), jnp.float32)

    def step(s, inp):
        qi, ke, vv, dec = inp
        y = jnp.einsum('bhld,bhdv->bhlv', qi, s)
        s = dec[..., None] * s + jnp.einsum('bhld,bhlv->bhdv', ke, vv)
        return s, y

    xs = tuple(jnp.moveaxis(a, 2, 0) for a in (q_in, k_end, vc, jnp.exp(lam_end[:, :, :, 0])))
    final, y_inter = lax.scan(step, state, xs)
    return _unchunk(y_intra + jnp.moveaxis(y_inter, 0, 2), v.dtype), final


def rwkv7_scan(r, w, k, v, a, bvec, state):
    b, t, h, d = r.shape
    if state is None:
        state = jnp.zeros((b, h, d, d), jnp.float32)

    def step(s, inp):
        rt, wt, kt, vt, at, bt = inp
        sa = jnp.einsum('bhvk,bhk->bhv', s, at)
        s = s * wt[:, :, None, :] + sa[..., None] * bt[:, :, None, :] + vt[..., None] * kt[:, :, None, :]
        return s, jnp.einsum('bhvk,bhk->bhv', s, rt)

    xs = tuple(jnp.moveaxis(u.astype(jnp.float32), 1, 0) for u in (r, w, k, v, a, bvec))
    final, y = lax.scan(step, state, xs)
    return jnp.moveaxis(y, 0, 1).astype(r.dtype), final


def ssd_prepare(p, conv_w, conv_b, dt_bias, a_log):
    b, t, _ = p.shape
    z = p[..., :A_INNER]
    xbc = jax.nn.silu(centred_dwconv(p[..., A_INNER:A_INNER + A_XBC], conv_w, conv_b))
    xs = xbc[..., :A_INNER].reshape(b, t, A_HEADS, A_HEAD_DIM)
    bm = xbc[..., A_INNER:A_INNER + A_GROUPS * A_STATE].reshape(b, t, A_GROUPS, A_STATE)
    cm = xbc[..., A_INNER + A_GROUPS * A_STATE:].reshape(b, t, A_GROUPS, A_STATE)
    dt = jax.nn.softplus(p[..., A_INNER + A_XBC:].reshape(b, t, 2, A_HEADS) + dt_bias)
    a = -jnp.exp(a_log)
    fwd = (xs * dt[:, :, 0, :, None], dt[:, :, 0] * a[0], bm, cm)
    bwd = (xs * dt[:, :, 1, :, None], dt[:, :, 1] * a[1], bm, cm)
    return (z, xs), fwd, bwd


def ssd_output(y, z, xs, d_skip, norm_w):
    b, t = y.shape[:2]
    y = (y + d_skip[:, None] * xs).reshape(b, t, A_INNER)
    return rms_norm(y * jax.nn.silu(z), norm_w)


def mlstm_prepare(p, conv_w, conv_b, i_bias, f_bias):
    b, t, _ = p.shape
    qk = jax.nn.silu(centred_dwconv(p[..., :2 * B_QK], conv_w, conv_b))
    q = qk[..., :B_QK].reshape(b, t, B_HEADS, B_QK_DIM)
    k = qk[..., B_QK:].reshape(b, t, B_HEADS, B_QK_DIM)
    v = p[..., 2 * B_QK:2 * B_QK + B_INNER].reshape(b, t, B_HEADS, B_V_DIM)
    o = p[..., 2 * B_QK + B_INNER:2 * B_QK + 2 * B_INNER]
    gates = p[..., 2 * B_QK + 2 * B_INNER:].reshape(b, t, 2, 2, B_HEADS)
    log_i = gates[:, :, 0] + i_bias
    log_f = jax.nn.log_sigmoid(gates[:, :, 1] + f_bias)
    fwd = (q, k, v, log_i[:, :, 0], log_f[:, :, 0])
    bwd = (q, k, v, log_i[:, :, 1], log_f[:, :, 1])
    return o, fwd, bwd


def mlstm_output(h, o, norm_w):
    return jax.nn.sigmoid(o) * head_norm(h, MLSTM_EPS, True) * norm_w


def hgrn2_prepare(p, lb, f_bias):
    b, t, _ = p.shape
    hd = lambda u: u.reshape(b, t, C_HEADS, C_HEAD_DIM)
    q = jax.nn.silu(p[..., :C_INNER])
    f_pre = p[..., C_INNER:3 * C_INNER].reshape(b, t, 2, C_INNER) + f_bias
    log_f = jnp.log(lb + (1 - lb) * jax.nn.sigmoid(f_pre))
    k = (1 - lb) * jax.nn.sigmoid(-f_pre)
    i = p[..., 3 * C_INNER:4 * C_INNER]
    g = p[..., 4 * C_INNER:]
    fwd = (hd(q), hd(k[:, :, 0]), hd(i), hd(log_f[:, :, 0]))
    bwd = (hd(q), hd(k[:, :, 1]), hd(i), hd(log_f[:, :, 1]))
    return g, fwd, bwd


def hgrn2_output(o, g, norm_w):
    return head_norm(o, 1e-6, False) * norm_w * jax.nn.silu(g)


def rwkv7_prepare(p, mu, w0, w2, a0, a2, g2, k_k, k_a):
    b, t, _ = p.shape
    hd = lambda u: u.reshape(b, t, D_HEADS, D_HEAD_DIM)
    p = p + mu * (centred_token_shift(p) - p)
    r = p[..., :D_INNER]
    k = p[..., D_INNER:2 * D_INNER]
    v = p[..., 2 * D_INNER:3 * D_INNER]
    o = 3 * D_INNER
    wl = p[..., o:o + 2 * D_W_LORA].reshape(b, t, 2, D_W_LORA)
    al = p[..., o + 2 * D_W_LORA:o + 2 * D_W_LORA + D_A_LORA]
    gl = p[..., o + 2 * D_W_LORA + D_A_LORA:]
    w = w0 + jnp.einsum('btdr,drc->btdc', jnp.tanh(wl), w2)
    decay = jnp.exp(-jnp.exp(-jax.nn.softplus(-w) - 0.5))
    a = jax.nn.sigmoid(a0 + al @ a2)
    g = jax.nn.sigmoid(gl) @ g2
    kk = hd(k * k_k).astype(jnp.float32)
    kk = (kk * lax.rsqrt(jnp.maximum(jnp.sum(kk * kk, -1, keepdims=True), 1e-12))).astype(p.dtype)
    k = hd(k * (1 + (a - 1) * k_a))
    r, v, a = hd(r), hd(v), hd(a)
    fwd = (r, hd(decay[:, :, 0]), k, v, -kk, kk * a)
    bwd = (r, hd(decay[:, :, 1]), k, v, -kk, kk * a)
    return (r, k, v, g), fwd, bwd


def rwkv7_output(y, r, k, v, g, r_k, ln_w, ln_b):
    b, t = y.shape[:2]
    bonus = (jnp.sum(r * k * r_k, axis=-1, keepdims=True) * v).reshape(b, t, D_INNER)
    return (head_norm(y, RWKV_EPS, True) * ln_w + ln_b + bonus) * g


def even_mixer(pc, pl, ssd_p, mlstm_p, ctx_out):
    conv_w, conv_b, dt_bias, a_log, d_skip, norm_a = ssd_p
    conv_bw, conv_bb, i_bias, f_bias, norm_b = mlstm_p
    ac = ssd_prepare(pc[..., :P_A], conv_w, conv_b, dt_bias, a_log)
    al = ssd_prepare(pl[..., :P_A], conv_w, conv_b, dt_bias, a_log)
    ya_c, ya_l = prefix_bidirectional(ssd_scan, ac[1], al[1], ac[2], al[2])
    bc = mlstm_prepare(pc[..., P_A:], conv_bw, conv_bb, i_bias, f_bias)
    bl = mlstm_prepare(pl[..., P_A:], conv_bw, conv_bb, i_bias, f_bias)
    yb_c, yb_l = prefix_bidirectional(mlstm_scan, bc[1], bl[1], bc[2], bl[2])
    feat_l = jnp.concatenate([ssd_output(ya_l, *al[0], d_skip, norm_a), mlstm_output(yb_l, bl[0], norm_b)], axis=-1)
    feat_c = None
    if ctx_out:
        feat_c = jnp.concatenate([ssd_output(ya_c, *ac[0], d_skip, norm_a), mlstm_output(yb_c, bc[0], norm_b)], axis=-1)
    return feat_c, feat_l


def odd_mixer(pc, pl, lb, hgrn_p, rwkv_p, ctx_out):
    f_bias, norm_c = hgrn_p
    mu, w0, w2, a0, a2, g2, k_k, k_a, r_k, ln_w, ln_b = rwkv_p
    hc = hgrn2_prepare(pc[..., :P_C], lb, f_bias)
    hl = hgrn2_prepare(pl[..., :P_C], lb, f_bias)
    yc_c, yc_l = prefix_bidirectional(gla_scan, hc[1], hl[1], hc[2], hl[2])
    rc = rwkv7_prepare(pc[..., P_C:], mu, w0, w2, a0, a2, g2, k_k, k_a)
    rl = rwkv7_prepare(pl[..., P_C:], mu, w0, w2, a0, a2, g2, k_k, k_a)
    yd_c, yd_l = prefix_bidirectional(rwkv7_scan, rc[1], rl[1], rc[2], rl[2])
    feat_l = jnp.concatenate([hgrn2_output(yc_l, hl[0], norm_c), rwkv7_output(yd_l, *rl[0], r_k, ln_w, ln_b)], axis=-1)
    feat_c = None
    if ctx_out:
        feat_c = jnp.concatenate([hgrn2_output(yc_c, hc[0], norm_c), rwkv7_output(yd_c, *rc[0], r_k, ln_w, ln_b)], axis=-1)
    return feat_c, feat_l


def moe_ffn(h, router_w, router_bias, w_gate, w_up, w_down):
    t, d = h.shape
    scores = jax.nn.sigmoid((h @ router_w).astype(jnp.float32))
    grp = (scores + router_bias.astype(jnp.float32)).reshape(t, N_EXPERT_GROUPS, EXPERTS_PER_GROUP)
    g_idx = jnp.argmax(lax.top_k(grp, TOP_K)[0].sum(-1), axis=-1)
    in_grp = jnp.take_along_axis(grp, g_idx[:, None, None], axis=1)[:, 0]
    _, local = lax.top_k(in_grp, TOP_K)
    expert = g_idx[:, None] * EXPERTS_PER_GROUP + local
    wts = jnp.take_along_axis(scores, expert, axis=1)
    wts = (wts / wts.sum(-1, keepdims=True)).astype(h.dtype)
    n_assign = t * TOP_K
    flat_e = expert.reshape(n_assign)
    order = jnp.argsort(flat_e)
    e_s = flat_e[order]
    tok_s = order // TOP_K
    w_s = wts.reshape(n_assign)[order]
    counts = jnp.bincount(flat_e, length=N_EXPERTS)
    starts = jnp.cumsum(counts) - counts
    padded = (counts + MOE_BLOCK - 1) // MOE_BLOCK * MOE_BLOCK
    pends = jnp.cumsum(padded)
    pstarts = pends - padded
    dest = pstarts[e_s] + (jnp.arange(n_assign) - starts[e_s])
    n_blocks = -(-n_assign // MOE_BLOCK) + N_EXPERTS
    xp = jnp.zeros((n_blocks * MOE_BLOCK, d), h.dtype).at[dest].set(h[tok_s])
    blk_e = jnp.minimum(jnp.searchsorted(pends, jnp.arange(n_blocks) * MOE_BLOCK, side='right'), N_EXPERTS - 1)

    def expert_block(args):
        xb, e = args
        return (jax.nn.silu(xb @ w_gate[e]) * (xb @ w_up[e])) @ w_down[e]

    yp = lax.map(expert_block, (xp.reshape(n_blocks, MOE_BLOCK, d), blk_e)).reshape(-1, d)
    return jnp.zeros_like(h).at[tok_s].add(yp[dest] * w_s[:, None])


def setup_inputs(seed: int = 0) -> dict:
    key = jax.random.key(seed)
    ks = iter(jax.random.split(key, 48))
    nrm = lambda shape, scale: jax.random.normal(next(ks), shape, jnp.float32) * scale
    unif = lambda shape, lo, hi: jax.random.uniform(next(ks), shape, jnp.float32, lo, hi)
    D = D_MODEL
    inp = {}
    inp['x'] = nrm((BATCH, SEQ, D), 1.0)
    inp['c'] = nrm((BATCH, D), 1.0)
    inp['ctx'] = nrm((BATCH, CTX_LEN, D), 1.0)
    inp['c_ctx'] = nrm((D,), 1.0)
    inp['mod_w'] = nrm((DEPTH, D, 6 * D), 0.5 * D ** -0.5)
    inp['mod_b'] = nrm((DEPTH, 6 * D), 0.02)
    inp['ln_g'] = 1.0 + nrm((DEPTH, 2, D), 0.02)
    inp['ln_b'] = nrm((DEPTH, 2, D), 0.02)
    inp['ev_w_in'] = nrm((N_EVEN, D, P_EVEN), D ** -0.5)
    inp['ev_w_out'] = nrm((N_EVEN, D_MIX, D), DEEPNORM_BETA * D_MIX ** -0.5)
    inp['ssd_conv_w'] = nrm((N_EVEN, SHORT_CONV, A_XBC), SHORT_CONV ** -0.5)
    inp['ssd_conv_b'] = nrm((N_EVEN, A_XBC), 0.02)
    dt0 = jnp.exp(unif((N_EVEN, 2, A_HEADS), math.log(1e-3), math.log(1e-1)))
    inp['ssd_dt_bias'] = dt0 + jnp.log(-jnp.expm1(-dt0))
    inp['ssd_a_log'] = jnp.log(unif((N_EVEN, 2, A_HEADS), 1.0, 16.0))
    inp['ssd_d'] = 1.0 + nrm((N_EVEN, A_HEADS), 0.02)
    inp['ssd_norm_w'] = 1.0 + nrm((N_EVEN, A_INNER), 0.02)
    inp['mlstm_conv_w'] = nrm((N_EVEN, SHORT_CONV, 2 * B_QK), SHORT_CONV ** -0.5)
    inp['mlstm_conv_b'] = nrm((N_EVEN, 2 * B_QK), 0.02)
    inp['mlstm_i_bias'] = nrm((N_EVEN, 2, B_HEADS), 0.1)
    inp['mlstm_f_bias'] = jnp.linspace(3.0, 6.0, B_HEADS) + nrm((N_EVEN, 2, B_HEADS), 0.1)
    inp['mlstm_norm_w'] = 1.0 + nrm((N_EVEN, B_INNER), 0.02)
    inp['od_w_in'] = nrm((N_ODD, D, P_ODD), D ** -0.5)
    inp['od_w_out'] = nrm((N_ODD, D_MIX, D), DEEPNORM_BETA * D_MIX ** -0.5)
    inp['hgrn_lb_logits'] = nrm((DEPTH, C_INNER), 0.1)
    inp['hgrn_f_bias'] = nrm((N_ODD, 2, C_INNER), 0.1)
    inp['hgrn_norm_w'] = 1.0 + nrm((N_ODD, C_INNER), 0.02)
    inp['rwkv_mu'] = unif((N_ODD, P_D), 0.0, 1.0)
    inp['rwkv_w0'] = unif((N_ODD, 2, D_INNER), -4.0, 0.0)
    inp['rwkv_w2'] = nrm((N_ODD, 2, D_W_LORA, D_INNER), 0.5 * D_W_LORA ** -0.5)
    inp['rwkv_a0'] = nrm((N_ODD, D_INNER), 0.1)
    inp['rwkv_a2'] = nrm((N_ODD, D_A_LORA, D_INNER), 0.5 * D_A_LORA ** -0.5)
    inp['rwkv_g2'] = nrm((N_ODD, D_G_LORA, D_INNER), D_G_LORA ** -0.5)
    inp['rwkv_k_k'] = 0.85 + nrm((N_ODD, D_INNER), 0.02)
    inp['rwkv_k_a'] = 1.0 + nrm((N_ODD, D_INNER), 0.02)
    inp['rwkv_r_k'] = nrm((N_ODD, D_HEADS, D_HEAD_DIM), 0.1)
    inp['rwkv_ln_w'] = 1.0 + nrm((N_ODD, D_INNER), 0.02)
    inp['rwkv_ln_b'] = nrm((N_ODD, D_INNER), 0.02)
    inp['router_w'] = nrm((D, N_EXPERTS), D ** -0.5)
    inp['router_bias'] = nrm((N_EXPERTS,), 0.01)
    inp['exp_w_gate'] = nrm((DEPTH, N_EXPERTS, D, D_EXPERT), D ** -0.5)
    inp['exp_w_up'] = nrm((DEPTH, N_EXPERTS, D, D_EXPERT), D ** -0.5)
    inp['exp_w_down'] = nrm((DEPTH, N_EXPERTS, D_EXPERT, D), DEEPNORM_BETA * D_EXPERT ** -0.5)
    return inp


def reference(x, c, ctx, c_ctx, mod_w, mod_b, ln_g, ln_b,
              ev_w_in, ev_w_out, ssd_conv_w, ssd_conv_b, ssd_dt_bias, ssd_a_log, ssd_d, ssd_norm_w,
              mlstm_conv_w, mlstm_conv_b, mlstm_i_bias, mlstm_f_bias, mlstm_norm_w,
              od_w_in, od_w_out, hgrn_lb_logits, hgrn_f_bias, hgrn_norm_w,
              rwkv_mu, rwkv_w0, rwkv_w2, rwkv_a0, rwkv_a2, rwkv_g2, rwkv_k_k, rwkv_k_a, rwkv_r_k,
              rwkv_ln_w, rwkv_ln_b, router_w, router_bias, exp_w_gate, exp_w_up, exp_w_down):
    bsz, seq, _ = x.shape
    rows = seq // GRID_W
    lb_all = jnp.cumsum(jax.nn.softmax(hgrn_lb_logits.astype(jnp.float32), axis=0), axis=0)
    lb_all = (lb_all - lb_all[0]).astype(x.dtype)
    s_c = jax.nn.silu(c)
    s_cc = jax.nn.silu(c_ctx)
    xl, xc = x, ctx
    for layer in range(DEPTH):
        i = layer // 2
        ctx_out = layer < DEPTH - 1
        sh1, sc1, g1, sh2, sc2, g2 = jnp.split(s_c @ mod_w[layer] + mod_b[layer], 6, axis=-1)
        csh1, csc1, cg1, csh2, csc2, cg2 = jnp.split(s_cc @ mod_w[layer] + mod_b[layer], 6, axis=-1)
        ul = xl * (1 + sc1[:, None]) + sh1[:, None]
        uc = xc * (1 + csc1) + csh1
        if layer % 2 == 0:
            fc, fl = even_mixer(uc @ ev_w_in[i], ul @ ev_w_in[i],
                                (ssd_conv_w[i], ssd_conv_b[i], ssd_dt_bias[i], ssd_a_log[i], ssd_d[i], ssd_norm_w[i]),
                                (mlstm_conv_w[i], mlstm_conv_b[i], mlstm_i_bias[i], mlstm_f_bias[i], mlstm_norm_w[i]),
                                ctx_out)
            w_out = ev_w_out[i]
        else:
            fc, fl = odd_mixer(uc @ od_w_in[i], to_col_major(ul, rows) @ od_w_in[i], lb_all[layer],
                               (hgrn_f_bias[i], hgrn_norm_w[i]),
                               (rwkv_mu[i], rwkv_w0[i], rwkv_w2[i], rwkv_a0[i], rwkv_a2[i], rwkv_g2[i],
                                rwkv_k_k[i], rwkv_k_a[i], rwkv_r_k[i], rwkv_ln_w[i], rwkv_ln_b[i]),
                               ctx_out)
            fl = from_col_major(fl, rows)
            w_out = od_w_out[i]
        xl = layer_norm(DEEPNORM_ALPHA * xl + g1[:, None] * (fl @ w_out), ln_g[layer, 0], ln_b[layer, 0])
        hl = xl * (1 + sc2[:, None]) + sh2[:, None]
        if ctx_out:
            xc = layer_norm(DEEPNORM_ALPHA * xc + cg1 * (fc @ w_out), ln_g[layer, 0], ln_b[layer, 0])
            hc = xc * (1 + csc2) + csh2
            n_c = hc.shape[0] * hc.shape[1]
            f = moe_ffn(jnp.concatenate([hc.reshape(n_c, D_MODEL), hl.reshape(-1, D_MODEL)], axis=0),
                        router_w, router_bias, exp_w_gate[layer], exp_w_up[layer], exp_w_down[layer])
            xc = layer_norm(DEEPNORM_ALPHA * xc + cg2 * f[:n_c].reshape(xc.shape), ln_g[layer, 1], ln_b[layer, 1])
            f_l = f[n_c:]
        else:
            f_l = moe_ffn(hl.reshape(-1, D_MODEL), router_w, router_bias,
                          exp_w_gate[layer], exp_w_up[layer], exp_w_down[layer])
        xl = layer_norm(DEEPNORM_ALPHA * xl + g2[:, None] * f_l.reshape(xl.shape), ln_g[layer, 1], ln_b[layer, 1])
    return xl
```

```python
import functools
import math

import jax
import jax.numpy as jnp
from jax import lax
from jax.experimental import pallas as pl
from jax.experimental.pallas import tpu as pltpu

F32 = jnp.float32
BF16 = jnp.bfloat16

D_MODEL = 1024
DEPTH = 4
GRID_W = 64
A_HEADS = 8
A_HEAD_DIM = 64
A_INNER = A_HEADS * A_HEAD_DIM
A_GROUPS = 2
A_STATE = 64
A_XBC = A_INNER + 2 * A_GROUPS * A_STATE
B_HEADS = 4
B_QK_DIM = 64
B_V_DIM = 128
B_QK = B_HEADS * B_QK_DIM
B_INNER = B_HEADS * B_V_DIM
MLSTM_EPS = 1e-6
C_HEADS = 4
C_HEAD_DIM = 128
C_INNER = C_HEADS * C_HEAD_DIM
D_HEADS = 8
D_HEAD_DIM = 64
D_INNER = D_HEADS * D_HEAD_DIM
D_W_LORA = 64
D_A_LORA = 64
D_G_LORA = 128
RWKV_EPS = 64e-5
P_A = A_INNER + A_XBC + 2 * A_HEADS
P_B = 2 * B_QK + 2 * B_INNER + 4 * B_HEADS
P_C = 5 * C_INNER
P_D = 3 * D_INNER + 2 * D_W_LORA + D_A_LORA + D_G_LORA
N_EXPERTS = 32
N_EXPERT_GROUPS = 8
EXPERTS_PER_GROUP = N_EXPERTS // N_EXPERT_GROUPS
TOP_K = 2
D_EXPERT = 512
MOE_BLOCK = 128
DEEPNORM_ALPHA = (2 * DEPTH) ** 0.25
LN_EPS = 1e-5
M_INIT = -1e30
NEG_BIG = -1e30

SSD_CHUNK = 128
MLSTM_CHUNK = 64
GLA_CHUNK = 16
RWKV_CHUNK = 32

VMEM_LIMIT_BYTES = 48 * 1024 * 1024
HI = lax.Precision.HIGHEST


def _dot(a, b, dims, exact):
    if exact:
        return lax.dot_general(a.astype(F32), b.astype(F32), (dims, ((), ())),
                               precision=HI, preferred_element_type=F32)
    return lax.dot_general(a.astype(BF16), b.astype(BF16), (dims, ((), ())),
                           preferred_element_type=F32)


def _nn(a, b, exact=True):
    return _dot(a, b, ((1,), (0,)), exact)


def _nt(a, b, exact=True):
    return _dot(a, b, ((1,), (1,)), exact)


def _tn(a, b, exact=True):
    return _dot(a, b, ((0,), (0,)), exact)


def _iota2(n, m):
    return (lax.broadcasted_iota(jnp.int32, (n, m), 0),
            lax.broadcasted_iota(jnp.int32, (n, m), 1))


def _cum_mats(col, row, L):
    ri, ci = _iota2(L, L)
    tri = (ci <= ri).astype(F32)
    ccol = _nn(tri, jnp.broadcast_to(col, (L, L)))
    crow = _nn(jnp.broadcast_to(row, (L, L)), (ri <= ci).astype(F32))
    return ccol, crow


def _mm_kernel(x_ref, w_ref, o_ref, *, exact):
    o_ref[...] = _nn(x_ref[...], w_ref[...], exact)


def _matmul(x, w, tm=512, tn=512, exact=False):
    m, k = x.shape
    n = w.shape[1]
    n_pad = -(-n // tn) * tn
    m_pad = -(-m // tm) * tm
    xb = x if exact else x.astype(BF16)
    wb = w if exact else w.astype(BF16)
    if n_pad != n:
        wb = jnp.pad(wb, ((0, 0), (0, n_pad - n)))
    if m_pad != m:
        xb = jnp.pad(xb, ((0, m_pad - m), (0, 0)))
    out = pl.pallas_call(
        functools.partial(_mm_kernel, exact=exact),
        grid=(n_pad // tn, m_pad // tm),
        in_specs=[pl.BlockSpec((tm, k), lambda j, i: (i, 0)),
                  pl.BlockSpec((k, tn), lambda j, i: (0, j))],
        out_specs=pl.BlockSpec((tm, tn), lambda j, i: (i, j)),
        out_shape=jax.ShapeDtypeStruct((m_pad, n_pad), F32),
        compiler_params=pltpu.CompilerParams(
            dimension_semantics=("arbitrary", "arbitrary"),
            vmem_limit_bytes=VMEM_LIMIT_BYTES),
        name="dense_matmul",
    )(xb, wb)
    return out[:m, :n]


def _moe_kernel(blk_e_ref, n_used_ref, x_ref, wg_ref, wu_ref, wd_ref, o_ref):
    i = pl.program_id(0)

    @pl.when(i < n_used_ref[0])
    def _():
        x = x_ref[...]
        g = jnp.dot(x, wg_ref[...], preferred_element_type=F32)
        u = jnp.dot(x, wu_ref[...], preferred_element_type=F32)
        hmid = (g * jax.nn.sigmoid(g) * u).astype(BF16)
        o_ref[...] = jnp.dot(hmid, wd_ref[...], preferred_element_type=F32)

    @pl.when(i >= n_used_ref[0])
    def _():
        o_ref[...] = jnp.zeros_like(o_ref)


def _moe_experts(xp, blk_e, n_used, w_gate, w_up, w_down):
    n_rows, d = xp.shape
    n_blocks = n_rows // MOE_BLOCK
    grid_spec = pltpu.PrefetchScalarGridSpec(
        num_scalar_prefetch=2,
        grid=(n_blocks,),
        in_specs=[
            pl.BlockSpec((MOE_BLOCK, d), lambda i, be, nu: (i, 0)),
            pl.BlockSpec((None, d, D_EXPERT), lambda i, be, nu: (be[i], 0, 0)),
            pl.BlockSpec((None, d, D_EXPERT), lambda i, be, nu: (be[i], 0, 0)),
            pl.BlockSpec((None, D_EXPERT, d), lambda i, be, nu: (be[i], 0, 0)),
        ],
        out_specs=pl.BlockSpec((MOE_BLOCK, d), lambda i, be, nu: (i, 0)),
    )
    return pl.pallas_call(
        _moe_kernel,
        grid_spec=grid_spec,
        out_shape=jax.ShapeDtypeStruct((n_rows, d), F32),
        compiler_params=pltpu.CompilerParams(
            dimension_semantics=("arbitrary",),
            vmem_limit_bytes=VMEM_LIMIT_BYTES),
        name="moe_experts",
    )(blk_e, n_used, xp, w_gate, w_up, w_down)


def _moe_ffn(h, router_w, router_bias, w_gate, w_up, w_down):
    t, d = h.shape
    logits = _matmul(h, router_w, tm=512, tn=128, exact=True)
    scores = jax.nn.sigmoid(logits)
    grp = (scores + router_bias.astype(F32)).reshape(t, N_EXPERT_GROUPS, EXPERTS_PER_GROUP)
    g_idx = jnp.argmax(lax.top_k(grp, TOP_K)[0].sum(-1), axis=-1)
    in_grp = jnp.take_along_axis(grp, g_idx[:, None, None], axis=1)[:, 0]
    _, local = lax.top_k(in_grp, TOP_K)
    expert = g_idx[:, None] * EXPERTS_PER_GROUP + local
    wts = jnp.take_along_axis(scores, expert, axis=1)
    wts = wts / wts.sum(-1, keepdims=True)
    n_assign = t * TOP_K
    flat_e = expert.reshape(n_assign).astype(jnp.int32)
    order = jnp.argsort(flat_e)
    e_s = flat_e[order]
    tok_s = order // TOP_K
    counts = jnp.bincount(flat_e, length=N_EXPERTS)
    starts = jnp.cumsum(counts) - counts
    padded = (counts + MOE_BLOCK - 1) // MOE_BLOCK * MOE_BLOCK
    pends = jnp.cumsum(padded)
    pstarts = pends - padded
    dest = pstarts[e_s] + (jnp.arange(n_assign) - starts[e_s])
    n_blocks = -(-n_assign // MOE_BLOCK) + N_EXPERTS
    hb = h.astype(BF16)
    xp = jnp.zeros((n_blocks * MOE_BLOCK, d), BF16).at[dest].set(hb[tok_s])
    blk_e = jnp.minimum(jnp.searchsorted(pends, jnp.arange(n_blocks) * MOE_BLOCK, side='right'),
                        N_EXPERTS - 1).astype(jnp.int32)
    n_used = (pends[-1] // MOE_BLOCK).astype(jnp.int32).reshape(1)
    yp = _moe_experts(xp, blk_e, n_used, w_gate.astype(BF16), w_up.astype(BF16), w_down.astype(BF16))
    dest_of_assign = jnp.zeros((n_assign,), jnp.int32).at[order].set(dest.astype(jnp.int32))
    dest_of_assign = dest_of_assign.reshape(t, TOP_K)
    out = yp[dest_of_assign[:, 0]] * wts[:, 0:1]
    for kk in range(1, TOP_K):
        out = out + yp[dest_of_assign[:, kk]] * wts[:, kk:kk + 1]
    return out


def _time_block(t, chunk, target):
    n = t // chunk
    best = 1
    for c in range(1, n + 1):
        if n % c == 0 and c * chunk <= target:
            best = c
    return best * chunk


def _ssd_kernel(x_ref, lac_ref, lar_ref, b_ref, c_ref, y_ref, st_ref, *, L, nck):
    P = x_ref.shape[-1]

    @pl.when(pl.program_id(1) == 0)
    def _():
        st_ref[...] = jnp.zeros_like(st_ref)

    ri, ci = _iota2(L, L)
    causal = ci <= ri

    def body(i, carry):
        sl = pl.ds(pl.multiple_of(i * L, L), L)
        x = x_ref[sl, :]
        bm = b_ref[sl, :]
        cm = c_ref[sl, :]
        ccol, crow = _cum_mats(lac_ref[sl, :], lar_ref[i], L)
        decay = jnp.exp(jnp.where(causal, ccol - crow, NEG_BIG))
        att = _nt(cm, bm) * decay
        st = st_ref[...]
        cum_p = ccol[:, :P]
        end_p = ccol[L - 1:L, :P]
        y = _nn(att, x) + jnp.exp(cum_p) * _nn(cm, st)
        y_ref[sl, :] = y
        st_ref[...] = jnp.exp(end_p) * st + _tn(bm, x * jnp.exp(end_p - cum_p))
        return carry

    lax.fori_loop(0, nck, body, 0)


def _ssd_scan(x, la, bm, cm):
    g, t, p = x.shape
    n = bm.shape[-1]
    hpg = g // bm.shape[0]
    L = SSD_CHUNK
    tb = _time_block(t, L, 1024)
    nck = tb // L
    lac = la[..., None]
    lar = la.reshape(g, t // L, 1, L)
    return pl.pallas_call(
        functools.partial(_ssd_kernel, L=L, nck=nck),
        grid=(g, t // tb),
        in_specs=[
            pl.BlockSpec((None, tb, p), lambda i, j: (i, j, 0)),
            pl.BlockSpec((None, tb, 1), lambda i, j: (i, j, 0)),
            pl.BlockSpec((None, nck, 1, L), lambda i, j: (i, j, 0, 0)),
            pl.BlockSpec((None, tb, n), lambda i, j: (i // hpg, j, 0)),
            pl.BlockSpec((None, tb, n), lambda i, j: (i // hpg, j, 0)),
        ],
        out_specs=pl.BlockSpec((None, tb, p), lambda i, j: (i, j, 0)),
        out_shape=jax.ShapeDtypeStruct((g, t, p), F32),
        scratch_shapes=[pltpu.VMEM((n, p), F32)],
        compiler_params=pltpu.CompilerParams(
            dimension_semantics=("arbitrary", "arbitrary"),
            vmem_limit_bytes=VMEM_LIMIT_BYTES),
        name="ssd_scan",
    )(x, lac, lar, bm, cm)


def _mlstm_kernel(q_ref, k_ref, v_ref, lic_ref, lfc_ref, lir_ref, lfr_ref, h_ref,
                  c_ref, n_ref, m_ref, *, L, nck):
    dk = q_ref.shape[-1]

    @pl.when(pl.program_id(1) == 0)
    def _():
        c_ref[...] = jnp.zeros_like(c_ref)
        n_ref[...] = jnp.zeros_like(n_ref)
        m_ref[...] = jnp.full_like(m_ref, M_INIT)

    ri, ci = _iota2(L, L)
    causal = ci <= ri

    def body(i, carry):
        sl = pl.ds(pl.multiple_of(i * L, L), L)
        q = q_ref[sl, :] * (dk ** -0.5)
        k = k_ref[sl, :]
        v = v_ref[sl, :]
        li_c = lic_ref[sl, :]
        li_r = lir_ref[i]
        fcol, frow = _cum_mats(lfc_ref[sl, :], lfr_ref[i], L)
        f_c = fcol[:, :1]
        ftot = fcol[L - 1:L, :1]
        c_prev = c_ref[...]
        n_prev = n_ref[...]
        m_prev = m_ref[...]
        w_end = ftot - f_c + li_c
        m_loc = jnp.max(w_end, axis=0, keepdims=True)
        ke = k * jnp.exp(w_end - m_loc)
        c_loc = _tn(ke, v)
        n_loc = jnp.sum(ke, axis=0, keepdims=True)
        log_d = jnp.where(causal, fcol - frow + li_r, NEG_BIG)
        log_inter = f_c + m_prev
        m_row = jnp.maximum(jnp.max(log_d, axis=-1, keepdims=True), log_inter)
        s = _nt(q, k) * jnp.exp(log_d - m_row)
        inter = jnp.exp(log_inter - m_row)
        num = _nn(s, v) + inter * _nn(q, c_prev)
        den = jnp.sum(s, axis=-1, keepdims=True) + inter * jnp.sum(q * n_prev, axis=-1, keepdims=True)
        h_ref[sl, :] = num / jnp.maximum(jnp.abs(den), jnp.exp(-m_row))
        m_new = jnp.maximum(ftot + m_prev, m_loc)
        sp = jnp.exp(ftot + m_prev - m_new)
        sc = jnp.exp(m_loc - m_new)
        c_ref[...] = sp * c_prev + sc * c_loc
        n_ref[...] = sp * n_prev + sc * n_loc
        m_ref[...] = m_new
        return carry

    lax.fori_loop(0, nck, body, 0)


def _mlstm_scan(q, k, v, li, lf):
    g, t, dk = q.shape
    dv = v.shape[-1]
    L = MLSTM_CHUNK
    tb = _time_block(t, L, 1024)
    nck = tb // L
    col = lambda a: a[..., None]
    row = lambda a: a.reshape(g, t // L, 1, L)
    seq = lambda d: pl.BlockSpec((None, tb, d), lambda i, j: (i, j, 0))
    rows = pl.BlockSpec((None, nck, 1, L), lambda i, j: (i, j, 0, 0))
    return pl.pallas_call(
        functools.partial(_mlstm_kernel, L=L, nck=nck),
        grid=(g, t // tb),
        in_specs=[seq(dk), seq(dk), seq(dv), seq(1), seq(1), rows, rows],
        out_specs=seq(dv),
        out_shape=jax.ShapeDtypeStruct((g, t, dv), F32),
        scratch_shapes=[pltpu.VMEM((dk, dv), F32), pltpu.VMEM((1, dk), F32), pltpu.VMEM((1, 1), F32)],
        compiler_params=pltpu.CompilerParams(
            dimension_semantics=("arbitrary", "arbitrary"),
            vmem_limit_bytes=VMEM_LIMIT_BYTES),
        name="mlstm_scan",
    )(q, k, v, col(li), col(lf), row(li), row(lf))


def _gla_kernel(q_ref, k_ref, v_ref, lf_ref, y_ref, st_ref, *, L, nck):
    @pl.when(pl.program_id(1) == 0)
    def _():
        st_ref[...] = jnp.zeros_like(st_ref)

    ri, ci = _iota2(L, L)
    causal = ci <= ri
    tri = causal.astype(F32)

    def body(i, carry):
        sl = pl.ds(pl.multiple_of(i * L, L), L)
        q = q_ref[sl, :]
        k = k_ref[sl, :]
        v = v_ref[sl, :]
        lam = _nn(tri, lf_ref[sl, :])
        lam_end = lam[L - 1:L, :]
        st = st_ref[...]
        q_in = q * jnp.exp(lam)
        att = jnp.where(causal, _nt(q_in, k * jnp.exp(-lam)), 0.0)
        y_ref[sl, :] = _nn(att, v) + _nt(q_in, st)
        st_ref[...] = st * jnp.exp(lam_end) + _tn(v, k * jnp.exp(lam_end - lam))
        return carry

    lax.fori_loop(0, nck, body, 0)


def _gla_scan(q, k, v, lf):
    g, t, dk = q.shape
    dv = v.shape[-1]
    L = GLA_CHUNK
    tb = _time_block(t, L, 1024)
    nck = tb // L
    seq = lambda d: pl.BlockSpec((None, tb, d), lambda i, j: (i, j, 0))
    return pl.pallas_call(
        functools.partial(_gla_kernel, L=L, nck=nck),
        grid=(g, t // tb),
        in_specs=[seq(dk), seq(dk), seq(dv), seq(dk)],
        out_specs=seq(dv),
        out_shape=jax.ShapeDtypeStruct((g, t, dv), F32),
        scratch_shapes=[pltpu.VMEM((dv, dk), F32)],
        compiler_params=pltpu.CompilerParams(
            dimension_semantics=("arbitrary", "arbitrary"),
            vmem_limit_bytes=VMEM_LIMIT_BYTES),
        name="gla_scan",
    )(q, k, v, lf)


def _rwkv_kernel(r_ref, lw_ref, k_ref, v_ref, a_ref, b_ref, y_ref, st_ref, *, L, nck):
    @pl.when(pl.program_id(1) == 0)
    def _():
        st_ref[...] = jnp.zeros_like(st_ref)

    ri, ci = _iota2(L, L)
    incl = ci <= ri
    strict = ci < ri
    tri = incl.astype(F32)
    eye = (ci == ri).astype(F32)
    n_double = int(math.log2(L)) - 1

    def body(i, carry):
        sl = pl.ds(pl.multiple_of(i * L, L), L)
        r = r_ref[sl, :]
        lw = lw_ref[sl, :]
        k = k_ref[sl, :]
        v = v_ref[sl, :]
        a = a_ref[sl, :]
        b = b_ref[sl, :]
        cum = _nn(tri, lw)
        cum_end = cum[L - 1:L, :]
        e_neg = jnp.exp(-cum)
        at = a * jnp.exp(cum - lw)
        bt = b * e_neg
        kt = k * e_neg
        rt = r * jnp.exp(cum)
        st = st_ref[...]
        nmat = jnp.where(strict, _nt(at, bt), 0.0)
        tinv = eye + nmat
        pw = nmat
        for _ in range(n_double):
            pw = _nn(pw, pw)
            tinv = tinv + _nn(tinv, pw)
        rhs = _nt(at, st) + _nn(jnp.where(strict, _nt(at, kt), 0.0), v)
        u = _nn(tinv, rhs)
        y = (_nt(rt, st) + _nn(jnp.where(incl, _nt(rt, bt), 0.0), u)
             + _nn(jnp.where(incl, _nt(rt, kt), 0.0), v))
        y_ref[sl, :] = y
        e_end = jnp.exp(cum_end - cum)
        st_ref[...] = st * jnp.exp(cum_end) + _tn(u, b * e_end) + _tn(v, k * e_end)
        return carry

    lax.fori_loop(0, nck, body, 0)


def _rwkv_scan(r, lw, k, v, a, b):
    g, t, d = r.shape
    L = RWKV_CHUNK
    tb = _time_block(t, L, 1024)
    nck = tb // L
    seq = pl.BlockSpec((None, tb, d), lambda i, j: (i, j, 0))
    return pl.pallas_call(
        functools.partial(_rwkv_kernel, L=L, nck=nck),
        grid=(g, t // tb),
        in_specs=[seq] * 6,
        out_specs=seq,
        out_shape=jax.ShapeDtypeStruct((g, t, d), F32),
        scratch_shapes=[pltpu.VMEM((d, d), F32)],
        compiler_params=pltpu.CompilerParams(
            dimension_semantics=("arbitrary", "arbitrary"),
            vmem_limit_bytes=VMEM_LIMIT_BYTES),
        name="rwkv7_scan",
    )(r, lw, k, v, a, b)


def _bidir_seq(ac, al, bc, bl):
    fwd = jnp.concatenate([ac, al], axis=1)
    bwd = jnp.concatenate([jnp.flip(bc, axis=1), jnp.flip(bl, axis=1)], axis=1)
    return jnp.stack([fwd, bwd])


def _heads(u, h):
    two, b, t, c = u.shape
    return u.reshape(two, b, t, h, c // h).transpose(0, 1, 3, 2, 4).reshape(two * b * h, t, c // h)


def _head_scalars(u):
    two, b, t, h = u.shape
    return u.transpose(0, 1, 3, 2).reshape(two * b * h, t)


def _unheads(y, b, h, n_ctx):
    g, t, d = y.shape
    y = y.reshape(2, b, h, t, d).transpose(0, 1, 3, 2, 4)
    yc = y[0, :, :n_ctx] + jnp.flip(y[1, :, :n_ctx], axis=1)
    yl = y[0, :, n_ctx:] + jnp.flip(y[1, :, n_ctx:], axis=1)
    return yc, yl


def _dwconv3(x, w, b):
    prev = jnp.pad(x[:, :-1], ((0, 0), (1, 0), (0, 0)))
    nxt = jnp.pad(x[:, 1:], ((0, 0), (0, 1), (0, 0)))
    return prev * w[0] + x * w[1] + nxt * w[2] + b


def _layer_norm(x, g, b):
    mu = x.mean(-1, keepdims=True)
    var = jnp.mean(jnp.square(x - mu), -1, keepdims=True)
    return (x - mu) * lax.rsqrt(var + LN_EPS) * g + b


def _head_norm(y, eps, center):
    if center:
        y = y - y.mean(-1, keepdims=True)
    y = y * lax.rsqrt(jnp.mean(y * y, -1, keepdims=True) + eps)
    return y.reshape(y.shape[0], y.shape[1], -1)


def _ssd_mixer(pc, pl_, params, ctx_out):
    conv_w, conv_b, dt_bias, a_log, d_skip, norm_w = params
    bsz, n_ctx = pc.shape[0], pc.shape[1]
    a = -jnp.exp(a_log)

    def prep(p):
        b, t, _ = p.shape
        z = p[..., :A_INNER]
        xbc = jax.nn.silu(_dwconv3(p[..., A_INNER:A_INNER + A_XBC], conv_w, conv_b))
        xs = xbc[..., :A_INNER].reshape(b, t, A_HEADS, A_HEAD_DIM)
        bm = xbc[..., A_INNER:A_INNER + A_GROUPS * A_STATE]
        cm = xbc[..., A_INNER + A_GROUPS * A_STATE:]
        dt = jax.nn.softplus(p[..., A_INNER + A_XBC:].reshape(b, t, 2, A_HEADS) + dt_bias)
        return z, xs, bm, cm, dt

    zc, xsc, bmc, cmc, dtc = prep(pc)
    zl, xsl, bml, cml, dtl = prep(pl_)
    xd = lambda xs, dt, d: (xs * dt[:, :, d, :, None]).reshape(xs.shape[0], xs.shape[1], A_INNER)
    x = _heads(_bidir_seq(xd(xsc, dtc, 0), xd(xsl, dtl, 0), xd(xsc, dtc, 1), xd(xsl, dtl, 1)), A_HEADS)
    la = _head_scalars(_bidir_seq(dtc[:, :, 0] * a[0], dtl[:, :, 0] * a[0], dtc[:, :, 1] * a[1], dtl[:, :, 1] * a[1]))
    bm = _heads(_bidir_seq(bmc, bml, bmc, bml), A_GROUPS)
    cm = _heads(_bidir_seq(cmc, cml, cmc, cml), A_GROUPS)
    y = _ssd_scan(x, la, bm, cm)
    yc, yl = _unheads(y, bsz, A_HEADS, n_ctx)

    def out(y, z, xs):
        b, t = y.shape[:2]
        y = (y + d_skip[:, None] * xs).reshape(b, t, A_INNER) * jax.nn.silu(z)
        return y * lax.rsqrt(jnp.mean(y * y, -1, keepdims=True) + 1e-6) * norm_w

    return (out(yc, zc, xsc) if ctx_out else None), out(yl, zl, xsl)


def _mlstm_mixer(pc, pl_, params, ctx_out):
    conv_w, conv_b, i_bias, f_bias, norm_w = params
    bsz, n_ctx = pc.shape[0], pc.shape[1]

    def prep(p):
        b, t, _ = p.shape
        qk = jax.nn.silu(_dwconv3(p[..., :2 * B_QK], conv_w, conv_b))
        q, k = qk[..., :B_QK], qk[..., B_QK:]
        v = p[..., 2 * B_QK:2 * B_QK + B_INNER]
        o = p[..., 2 * B_QK + B_INNER:2 * B_QK + 2 * B_INNER]
        gates = p[..., 2 * B_QK + 2 * B_INNER:].reshape(b, t, 2, 2, B_HEADS)
        log_i = gates[:, :, 0] + i_bias
        log_f = jax.nn.log_sigmoid(gates[:, :, 1] + f_bias)
        return q, k, v, o, log_i, log_f

    qc, kc, vc, oc, lic, lfc = prep(pc)
    ql, kl, vl, ol, lil, lfl = prep(pl_)
    q = _heads(_bidir_seq(qc, ql, qc, ql), B_HEADS)
    k = _heads(_bidir_seq(kc, kl, kc, kl), B_HEADS)
    v = _heads(_bidir_seq(vc, vl, vc, vl), B_HEADS)
    li = _head_scalars(_bidir_seq(lic[:, :, 0], lil[:, :, 0], lic[:, :, 1], lil[:, :, 1]))
    lf = _head_scalars(_bidir_seq(lfc[:, :, 0], lfl[:, :, 0], lfc[:, :, 1], lfl[:, :, 1]))
    h = _mlstm_scan(q, k, v, li, lf)
    hc, hl = _unheads(h, bsz, B_HEADS, n_ctx)
    out = lambda h, o: jax.nn.sigmoid(o) * _head_norm(h, MLSTM_EPS, True) * norm_w
    return (out(hc, oc) if ctx_out else None), out(hl, ol)


def _hgrn2_mixer(pc, pl_, lb, params, ctx_out):
    f_bias, norm_w = params
    bsz, n_ctx = pc.shape[0], pc.shape[1]

    def prep(p):
        b, t, _ = p.shape
        q = jax.nn.silu(p[..., :C_INNER])
        f_pre = p[..., C_INNER:3 * C_INNER].reshape(b, t, 2, C_INNER) + f_bias
        log_f = jnp.log(lb + (1 - lb) * jax.nn.sigmoid(f_pre))
        k = (1 - lb) * jax.nn.sigmoid(-f_pre)
        return q, k, p[..., 3 * C_INNER:4 * C_INNER], p[..., 4 * C_INNER:], log_f

    qc, kc, ic, gc, lfc = prep(pc)
    ql, kl, il, gl, lfl = prep(pl_)
    q = _heads(_bidir_seq(qc, ql, qc, ql), C_HEADS)
    k = _heads(_bidir_seq(kc[:, :, 0], kl[:, :, 0], kc[:, :, 1], kl[:, :, 1]), C_HEADS)
    v = _heads(_bidir_seq(ic, il, ic, il), C_HEADS)
    lf = _heads(_bidir_seq(lfc[:, :, 0], lfl[:, :, 0], lfc[:, :, 1], lfl[:, :, 1]), C_HEADS)
    o = _gla_scan(q, k, v, lf)
    oc, ol = _unheads(o, bsz, C_HEADS, n_ctx)
    out = lambda o, g: _head_norm(o, 1e-6, False) * norm_w * jax.nn.silu(g)
    return (out(oc, gc) if ctx_out else None), out(ol, gl)


def _rwkv7_mixer(pc, pl_, params, ctx_out):
    mu, w0, w2, a0, a2, g2, k_k, k_a, r_k, ln_w, ln_b = params
    bsz, n_ctx = pc.shape[0], pc.shape[1]

    def prep(p):
        b, t, _ = p.shape
        hd = lambda u: u.reshape(b, t, D_HEADS, D_HEAD_DIM)
        prev = jnp.pad(p[:, :-1], ((0, 0), (1, 0), (0, 0)))
        nxt = jnp.pad(p[:, 1:], ((0, 0), (0, 1), (0, 0)))
        p = p + mu * (0.5 * (prev + nxt) - p)
        r = p[..., :D_INNER]
        k = p[..., D_INNER:2 * D_INNER]
        v = p[..., 2 * D_INNER:3 * D_INNER]
        o = 3 * D_INNER
        wl = jnp.tanh(p[..., o:o + 2 * D_W_LORA]).reshape(b * t, 2, D_W_LORA)
        al = p[..., o + 2 * D_W_LORA:o + 2 * D_W_LORA + D_A_LORA]
        gl = p[..., o + 2 * D_W_LORA + D_A_LORA:]
        w = jnp.stack([_matmul(wl[:, d], w2[d]) for d in range(2)], axis=1).reshape(b, t, 2, D_INNER) + w0
        log_decay = -jnp.exp(-jax.nn.softplus(-w) - 0.5)
        a = jax.nn.sigmoid(a0 + _matmul(al.reshape(b * t, D_A_LORA), a2).reshape(b, t, D_INNER))
        g = _matmul(jax.nn.sigmoid(gl).reshape(b * t, D_G_LORA), g2).reshape(b, t, D_INNER)
        kk = hd(k * k_k)
        kk = (kk * lax.rsqrt(jnp.maximum(jnp.sum(kk * kk, -1, keepdims=True), 1e-12))).reshape(b, t, D_INNER)
        k = k * (1 + (a - 1) * k_a)
        return r, k, v, g, log_decay, -kk, kk * a

    rc, kc, vc, gc, ldc, ac, bc = prep(pc)
    rl, kl, vl, gl, ldl, al, bl = prep(pl_)
    same = lambda c, l: _heads(_bidir_seq(c, l, c, l), D_HEADS)
    lw = _heads(_bidir_seq(ldc[:, :, 0], ldl[:, :, 0], ldc[:, :, 1], ldl[:, :, 1]), D_HEADS)
    y = _rwkv_scan(same(rc, rl), lw, same(kc, kl), same(vc, vl), same(ac, al), same(bc, bl))
    yc, yl = _unheads(y, bsz, D_HEADS, n_ctx)

    def out(y, r, k, v, g):
        b, t = y.shape[:2]
        hd = lambda u: u.reshape(b, t, D_HEADS, D_HEAD_DIM)
        bonus = (jnp.sum(hd(r) * hd(k) * r_k, axis=-1, keepdims=True) * hd(v)).reshape(b, t, D_INNER)
        return (_head_norm(y, RWKV_EPS, True) * ln_w + ln_b + bonus) * g

    return (out(yc, rc, kc, vc, gc) if ctx_out else None), out(yl, rl, kl, vl, gl)


def _to_col_major(u, rows):
    b, s, d = u.shape
    return u.reshape(b, rows, GRID_W, d).transpose(0, 2, 1, 3).reshape(b, s, d)


def _from_col_major(u, rows):
    b, s, d = u.shape
    return u.reshape(b, GRID_W, rows, d).transpose(0, 2, 1, 3).reshape(b, s, d)


def _proj(u, w):
    b, t, d = u.shape
    return _matmul(u.reshape(b * t, d), w).reshape(b, t, w.shape[1])


def kernel(x, c, ctx, c_ctx, mod_w, mod_b, ln_g, ln_b, ev_w_in, ev_w_out, ssd_conv_w, ssd_conv_b, ssd_dt_bias, ssd_a_log, ssd_d, ssd_norm_w, mlstm_conv_w, mlstm_conv_b, mlstm_i_bias, mlstm_f_bias, mlstm_norm_w, od_w_in, od_w_out, hgrn_lb_logits, hgrn_f_bias, hgrn_norm_w, rwkv_mu, rwkv_w0, rwkv_w2, rwkv_a0, rwkv_a2, rwkv_g2, rwkv_k_k, rwkv_k_a, rwkv_r_k, rwkv_ln_w, rwkv_ln_b, router_w, router_bias, exp_w_gate, exp_w_up, exp_w_down):
    bsz, seq, _ = x.shape
    rows = seq // GRID_W
    lb_all = jnp.cumsum(jax.nn.softmax(hgrn_lb_logits.astype(F32), axis=0), axis=0)
    lb_all = lb_all - lb_all[0]
    s_c = jax.nn.silu(c)
    s_cc = jax.nn.silu(c_ctx)
    xl, xc = x, ctx
    for layer in range(DEPTH):
        i = layer // 2
        ctx_out = layer < DEPTH - 1
        mods = _matmul(jnp.concatenate([s_c, s_cc[None]], axis=0), mod_w[layer], tm=8, tn=512) + mod_b[layer]
        sh1, sc1, g1, sh2, sc2, g2 = jnp.split(mods[:bsz], 6, axis=-1)
        csh1, csc1, cg1, csh2, csc2, cg2 = jnp.split(mods[bsz], 6, axis=-1)
        ul = xl * (1 + sc1[:, None]) + sh1[:, None]
        uc = xc * (1 + csc1) + csh1
        if layer % 2 == 0:
            pc, pl_ = _proj(uc, ev_w_in[i]), _proj(ul, ev_w_in[i])
            fca, fla = _ssd_mixer(pc[..., :P_A], pl_[..., :P_A],
                                  (ssd_conv_w[i], ssd_conv_b[i], ssd_dt_bias[i], ssd_a_log[i], ssd_d[i], ssd_norm_w[i]),
                                  ctx_out)
            fcb, flb = _mlstm_mixer(pc[..., P_A:], pl_[..., P_A:],
                                    (mlstm_conv_w[i], mlstm_conv_b[i], mlstm_i_bias[i], mlstm_f_bias[i], mlstm_norm_w[i]),
                                    ctx_out)
            w_out = ev_w_out[i]
        else:
            pc, pl_ = _proj(uc, od_w_in[i]), _proj(_to_col_major(ul, rows), od_w_in[i])
            fca, fla = _hgrn2_mixer(pc[..., :P_C], pl_[..., :P_C], lb_all[layer],
                                    (hgrn_f_bias[i], hgrn_norm_w[i]), ctx_out)
            fcb, flb = _rwkv7_mixer(pc[..., P_C:], pl_[..., P_C:],
                                    (rwkv_mu[i], rwkv_w0[i], rwkv_w2[i], rwkv_a0[i], rwkv_a2[i], rwkv_g2[i],
                                     rwkv_k_k[i], rwkv_k_a[i], rwkv_r_k[i], rwkv_ln_w[i], rwkv_ln_b[i]),
                                    ctx_out)
            w_out = od_w_out[i]
        fl = jnp.concatenate([fla, flb], axis=-1)
        if layer % 2 == 1:
            fl = _from_col_major(fl, rows)
        xl = _layer_norm(DEEPNORM_ALPHA * xl + g1[:, None] * _proj(fl, w_out), ln_g[layer, 0], ln_b[layer, 0])
        hl = xl * (1 + sc2[:, None]) + sh2[:, None]
        if ctx_out:
            fc = jnp.concatenate([fca, fcb], axis=-1)
            xc = _layer_norm(DEEPNORM_ALPHA * xc + cg1 * _proj(fc, w_out), ln_g[layer, 0], ln_b[layer, 0])
            hc = xc * (1 + csc2) + csh2
            n_c = hc.shape[0] * hc.shape[1]
            f = _moe_ffn(jnp.concatenate([hc.reshape(n_c, D_MODEL), hl.reshape(-1, D_MODEL)], axis=0),
                         router_w, router_bias, exp_w_gate[layer], exp_w_up[layer], exp_w_down[layer])
            xc = _layer_norm(DEEPNORM_ALPHA * xc + cg2 * f[:n_c].reshape(xc.shape), ln_g[layer, 1], ln_b[layer, 1])
            f_l = f[n_c:]
        else:
            f_l = _moe_ffn(hl.reshape(-1, D_MODEL), router_w, router_bias,
                           exp_w_gate[layer], exp_w_up[layer], exp_w_down[layer])
        xl = _layer_norm(DEEPNORM_ALPHA * xl + g2[:, None] * f_l.reshape(xl.shape), ln_g[layer, 1], ln_b[layer, 1])
    return xl
```

```python
import functools
import math

import jax
import jax.numpy as jnp
from jax import lax
from jax.experimental import pallas as pl
from jax.experimental.pallas import tpu as pltpu

F32 = jnp.float32
BF16 = jnp.bfloat16

D_MODEL = 1024
DEPTH = 4
GRID_W = 64
A_HEADS = 8
A_HEAD_DIM = 64
A_INNER = A_HEADS * A_HEAD_DIM
A_GROUPS = 2
A_STATE = 64
A_XBC = A_INNER + 2 * A_GROUPS * A_STATE
B_HEADS = 4
B_QK_DIM = 64
B_V_DIM = 128
B_QK = B_HEADS * B_QK_DIM
B_INNER = B_HEADS * B_V_DIM
MLSTM_EPS = 1e-6
C_HEADS = 4
C_HEAD_DIM = 128
C_INNER = C_HEADS * C_HEAD_DIM
D_HEADS = 8
D_HEAD_DIM = 64
D_INNER = D_HEADS * D_HEAD_DIM
D_W_LORA = 64
D_A_LORA = 64
D_G_LORA = 128
RWKV_EPS = 64e-5
P_A = A_INNER + A_XBC + 2 * A_HEADS
P_B = 2 * B_QK + 2 * B_INNER + 4 * B_HEADS
P_C = 5 * C_INNER
P_D = 3 * D_INNER + 2 * D_W_LORA + D_A_LORA + D_G_LORA
N_EXPERTS = 32
N_EXPERT_GROUPS = 8
EXPERTS_PER_GROUP = N_EXPERTS // N_EXPERT_GROUPS
TOP_K = 2
D_EXPERT = 512
MOE_BLOCK = 128
RANK_BLOCK = 512
DEEPNORM_ALPHA = (2 * DEPTH) ** 0.25
LN_EPS = 1e-5
M_INIT = -1e30
NEG_BIG = -1e30

SSD_CHUNK = 128
MLSTM_CHUNK = 64
GLA_CHUNK = 16
RWKV_CHUNK = 32

VMEM_LIMIT_BYTES = 48 * 1024 * 1024
HI = lax.Precision.HIGHEST


def _dot(a, b, dims, exact):
    if exact:
        return lax.dot_general(a.astype(F32), b.astype(F32), (dims, ((), ())),
                               precision=HI, preferred_element_type=F32)
    return lax.dot_general(a.astype(BF16), b.astype(BF16), (dims, ((), ())),
                           preferred_element_type=F32)


def _nn(a, b, exact=False):
    return _dot(a, b, ((1,), (0,)), exact)


def _nt(a, b, exact=False):
    return _dot(a, b, ((1,), (1,)), exact)


def _tn(a, b, exact=False):
    return _dot(a, b, ((0,), (0,)), exact)


def _iota2(n, m):
    return (lax.broadcasted_iota(jnp.int32, (n, m), 0),
            lax.broadcasted_iota(jnp.int32, (n, m), 1))


def _split3(x):
    x1 = x.astype(BF16)
    r1 = x - x1.astype(F32)
    x2 = r1.astype(BF16)
    x3 = (r1 - x2.astype(F32)).astype(BF16)
    return x1, x2, x3


def _mask_nn(mask, x):
    mb = mask.astype(BF16)
    x1, x2, x3 = _split3(x)
    return _nn(mb, x1, False) + _nn(mb, x2, False) + _nn(mb, x3, False)


def _nn_mask(x, mask):
    mb = mask.astype(BF16)
    x1, x2, x3 = _split3(x)
    return _nn(x1, mb, False) + _nn(x2, mb, False) + _nn(x3, mb, False)


def _cum_mats(col, row, L):
    ri, ci = _iota2(L, L)
    ccol = _mask_nn(ci <= ri, jnp.broadcast_to(col, (L, L)))
    crow = _nn_mask(jnp.broadcast_to(row, (L, L)), ri <= ci)
    return ccol, crow


def _mm_kernel(x_ref, w_ref, o_ref, *, exact):
    o_ref[...] = _nn(x_ref[...], w_ref[...], exact)


def _matmul(x, w, tm=512, tn=512, exact=False):
    m, k = x.shape
    n = w.shape[1]
    n_pad = -(-n // tn) * tn
    m_pad = -(-m // tm) * tm
    xb = x if exact else x.astype(BF16)
    wb = w if exact else w.astype(BF16)
    if n_pad != n:
        wb = jnp.pad(wb, ((0, 0), (0, n_pad - n)))
    if m_pad != m:
        xb = jnp.pad(xb, ((0, m_pad - m), (0, 0)))
    out = pl.pallas_call(
        functools.partial(_mm_kernel, exact=exact),
        grid=(n_pad // tn, m_pad // tm),
        in_specs=[pl.BlockSpec((tm, k), lambda j, i: (i, 0)),
                  pl.BlockSpec((k, tn), lambda j, i: (0, j))],
        out_specs=pl.BlockSpec((tm, tn), lambda j, i: (i, j)),
        out_shape=jax.ShapeDtypeStruct((m_pad, n_pad), F32),
        compiler_params=pltpu.CompilerParams(
            dimension_semantics=("arbitrary", "arbitrary"),
            vmem_limit_bytes=VMEM_LIMIT_BYTES),
        name="dense_matmul",
    )(xb, wb)
    return out[:m, :n]


def _moe_kernel(blk_e_ref, n_used_ref, x_ref, wg_ref, wu_ref, wd_ref, o_ref):
    i = pl.program_id(0)

    @pl.when(i < n_used_ref[0])
    def _():
        x = x_ref[...]
        g = jnp.dot(x, wg_ref[...], preferred_element_type=F32)
        u = jnp.dot(x, wu_ref[...], preferred_element_type=F32)
        hmid = (g * jax.nn.sigmoid(g) * u).astype(BF16)
        o_ref[...] = jnp.dot(hmid, wd_ref[...], preferred_element_type=F32)

    @pl.when(i >= n_used_ref[0])
    def _():
        o_ref[...] = jnp.zeros_like(o_ref)


def _moe_experts(xp, blk_e, n_used, w_gate, w_up, w_down):
    n_rows, d = xp.shape
    n_blocks = n_rows // MOE_BLOCK
    grid_spec = pltpu.PrefetchScalarGridSpec(
        num_scalar_prefetch=2,
        grid=(n_blocks,),
        in_specs=[
            pl.BlockSpec((MOE_BLOCK, d), lambda i, be, nu: (i, 0)),
            pl.BlockSpec((None, d, D_EXPERT), lambda i, be, nu: (be[i], 0, 0)),
            pl.BlockSpec((None, d, D_EXPERT), lambda i, be, nu: (be[i], 0, 0)),
            pl.BlockSpec((None, D_EXPERT, d), lambda i, be, nu: (be[i], 0, 0)),
        ],
        out_specs=pl.BlockSpec((MOE_BLOCK, d), lambda i, be, nu: (i, 0)),
    )
    return pl.pallas_call(
        _moe_kernel,
        grid_spec=grid_spec,
        out_shape=jax.ShapeDtypeStruct((n_rows, d), F32),
        compiler_params=pltpu.CompilerParams(
            dimension_semantics=("arbitrary",),
            vmem_limit_bytes=VMEM_LIMIT_BYTES),
        name="moe_experts",
    )(blk_e, n_used, xp, w_gate, w_up, w_down)


def _rank_kernel(e_ref, rank_ref, cnt_ref, carry_ref):
    rb = e_ref.shape[-1]

    @pl.when(pl.program_id(0) == 0)
    def _():
        carry_ref[...] = jnp.zeros_like(carry_ref)

    experts = lax.broadcasted_iota(jnp.int32, (N_EXPERTS, rb), 0)
    onehot = jnp.where(experts == e_ref[...], 1.0, 0.0)
    ri, ci = _iota2(rb, rb)
    earlier = jnp.where(ri < ci, 1.0, 0.0)
    prefix = _nn(onehot, earlier)
    carry = carry_ref[...]
    rank_ref[...] = jnp.sum(onehot * (prefix + carry), axis=0, keepdims=True).astype(jnp.int32)
    carry = carry + jnp.sum(onehot, axis=1, keepdims=True)
    carry_ref[...] = carry
    cnt_ref[...] = carry.astype(jnp.int32)


def _expert_ranks(flat_e):
    n = flat_e.shape[0]
    nblk = n // RANK_BLOCK
    rank, counts = pl.pallas_call(
        _rank_kernel,
        grid=(nblk,),
        in_specs=[pl.BlockSpec((None, 1, RANK_BLOCK), lambda i: (i, 0, 0))],
        out_specs=[pl.BlockSpec((None, 1, RANK_BLOCK), lambda i: (i, 0, 0)),
                   pl.BlockSpec((N_EXPERTS, 1), lambda i: (0, 0))],
        out_shape=[jax.ShapeDtypeStruct((nblk, 1, RANK_BLOCK), jnp.int32),
                   jax.ShapeDtypeStruct((N_EXPERTS, 1), jnp.int32)],
        scratch_shapes=[pltpu.VMEM((N_EXPERTS, 1), F32)],
        compiler_params=pltpu.CompilerParams(dimension_semantics=("arbitrary",)),
        name="moe_expert_ranks",
    )(flat_e.reshape(nblk, 1, RANK_BLOCK))
    return rank.reshape(n), counts[:, 0]


def _moe_ffn(h, router_w, router_bias, w_gate, w_up, w_down):
    t, d = h.shape
    logits = _matmul(h, router_w, tm=512, tn=128, exact=True)
    scores = jax.nn.sigmoid(logits)
    grp = (scores + router_bias.astype(F32)).reshape(t, N_EXPERT_GROUPS, EXPERTS_PER_GROUP)
    g_idx = jnp.argmax(lax.top_k(grp, TOP_K)[0].sum(-1), axis=-1)
    in_grp = jnp.take_along_axis(grp, g_idx[:, None, None], axis=1)[:, 0]
    _, local = lax.top_k(in_grp, TOP_K)
    expert = (g_idx[:, None] * EXPERTS_PER_GROUP + local).astype(jnp.int32)
    wts = jnp.take_along_axis(scores, expert, axis=1)
    wts = wts / wts.sum(-1, keepdims=True)
    n_assign = t * TOP_K
    flat_e = expert.reshape(n_assign)
    rank, counts = _expert_ranks(flat_e)
    padded = (counts + MOE_BLOCK - 1) // MOE_BLOCK * MOE_BLOCK
    pends = jnp.cumsum(padded)
    pstarts = pends - padded
    dest = (pstarts[flat_e] + rank).reshape(t, TOP_K)
    n_blocks = -(-n_assign // MOE_BLOCK) + N_EXPERTS
    hb = h.astype(BF16)
    xp = jnp.zeros((n_blocks * MOE_BLOCK, d), BF16)
    for kk in range(TOP_K):
        xp = xp.at[dest[:, kk]].set(hb, unique_indices=True)
    blk_e = jnp.minimum(jnp.searchsorted(pends, jnp.arange(n_blocks) * MOE_BLOCK, side='right'),
                        N_EXPERTS - 1).astype(jnp.int32)
    n_used = (pends[-1] // MOE_BLOCK).astype(jnp.int32).reshape(1)
    yp = _moe_experts(xp, blk_e, n_used, w_gate.astype(BF16), w_up.astype(BF16), w_down.astype(BF16))
    out = yp[dest[:, 0]] * wts[:, 0:1]
    for kk in range(1, TOP_K):
        out = out + yp[dest[:, kk]] * wts[:, kk:kk + 1]
    return out


def _time_block(t, chunk, target):
    n = t // chunk
    best = 1
    for c in range(1, n + 1):
        if n % c == 0 and c * chunk <= target:
            best = c
    return best * chunk


def _ssd_kernel(x_ref, lac_ref, lar_ref, b_ref, c_ref, y_ref, st_ref, *, L, nck):
    P = x_ref.shape[-1]

    @pl.when(pl.program_id(1) == 0)
    def _():
        st_ref[...] = jnp.zeros_like(st_ref)

    ri, ci = _iota2(L, L)
    causal = ci <= ri

    def body(i, carry):
        sl = pl.ds(pl.multiple_of(i * L, L), L)
        x = x_ref[sl, :]
        bm = b_ref[sl, :]
        cm = c_ref[sl, :]
        ccol, crow = _cum_mats(lac_ref[sl, :], lar_ref[i], L)
        decay = jnp.exp(jnp.where(causal, ccol - crow, NEG_BIG))
        att = _nt(cm, bm) * decay
        st = st_ref[...]
        cum_p = ccol[:, :P]
        end_p = ccol[L - 1:L, :P]
        y = _nn(att, x) + jnp.exp(cum_p) * _nn(cm, st)
        y_ref[sl, :] = y
        st_ref[...] = jnp.exp(end_p) * st + _tn(bm, x * jnp.exp(end_p - cum_p))
        return carry

    lax.fori_loop(0, nck, body, 0)


def _ssd_scan(x, la, bm, cm):
    g, t, p = x.shape
    n = bm.shape[-1]
    hpg = g // bm.shape[0]
    L = SSD_CHUNK
    tb = _time_block(t, L, 1024)
    nck = tb // L
    lac = la[..., None]
    lar = la.reshape(g, t // L, 1, L)
    return pl.pallas_call(
        functools.partial(_ssd_kernel, L=L, nck=nck),
        grid=(g, t // tb),
        in_specs=[
            pl.BlockSpec((None, tb, p), lambda i, j: (i, j, 0)),
            pl.BlockSpec((None, tb, 1), lambda i, j: (i, j, 0)),
            pl.BlockSpec((None, nck, 1, L), lambda i, j: (i, j, 0, 0)),
            pl.BlockSpec((None, tb, n), lambda i, j: (i // hpg, j, 0)),
            pl.BlockSpec((None, tb, n), lambda i, j: (i // hpg, j, 0)),
        ],
        out_specs=pl.BlockSpec((None, tb, p), lambda i, j: (i, j, 0)),
        out_shape=jax.ShapeDtypeStruct((g, t, p), F32),
        scratch_shapes=[pltpu.VMEM((n, p), F32)],
        compiler_params=pltpu.CompilerParams(
            dimension_semantics=("arbitrary", "arbitrary"),
            vmem_limit_bytes=VMEM_LIMIT_BYTES),
        name="ssd_scan",
    )(x, lac, lar, bm, cm)


def _mlstm_kernel(q_ref, k_ref, v_ref, lic_ref, lfc_ref, lir_ref, lfr_ref, h_ref,
                  c_ref, n_ref, m_ref, *, L, nck):
    dk = q_ref.shape[-1]

    @pl.when(pl.program_id(1) == 0)
    def _():
        c_ref[...] = jnp.zeros_like(c_ref)
        n_ref[...] = jnp.zeros_like(n_ref)
        m_ref[...] = jnp.full_like(m_ref, M_INIT)

    ri, ci = _iota2(L, L)
    causal = ci <= ri

    def body(i, carry):
        sl = pl.ds(pl.multiple_of(i * L, L), L)
        q = q_ref[sl, :] * (dk ** -0.5)
        k = k_ref[sl, :]
        v = v_ref[sl, :]
        li_c = lic_ref[sl, :]
        li_r = lir_ref[i]
        fcol, frow = _cum_mats(lfc_ref[sl, :], lfr_ref[i], L)
        f_c = fcol[:, :1]
        ftot = fcol[L - 1:L, :1]
        c_prev = c_ref[...]
        n_prev = n_ref[...]
        m_prev = m_ref[...]
        w_end = ftot - f_c + li_c
        m_loc = jnp.max(w_end, axis=0, keepdims=True)
        ke = k * jnp.exp(w_end - m_loc)
        c_loc = _tn(ke, v)
        n_loc = jnp.sum(ke, axis=0, keepdims=True)
        log_d = jnp.where(causal, fcol - frow + li_r, NEG_BIG)
        log_inter = f_c + m_prev
        m_row = jnp.maximum(jnp.max(log_d, axis=-1, keepdims=True), log_inter)
        s = _nt(q, k) * jnp.exp(log_d - m_row)
        inter = jnp.exp(log_inter - m_row)
        num = _nn(s, v) + inter * _nn(q, c_prev)
        den = jnp.sum(s, axis=-1, keepdims=True) + inter * jnp.sum(q * n_prev, axis=-1, keepdims=True)
        h_ref[sl, :] = num / jnp.maximum(jnp.abs(den), jnp.exp(-m_row))
        m_new = jnp.maximum(ftot + m_prev, m_loc)
        sp = jnp.exp(ftot + m_prev - m_new)
        sc = jnp.exp(m_loc - m_new)
        c_ref[...] = sp * c_prev + sc * c_loc
        n_ref[...] = sp * n_prev + sc * n_loc
        m_ref[...] = m_new
        return carry

    lax.fori_loop(0, nck, body, 0)


def _mlstm_scan(q, k, v, li, lf):
    g, t, dk = q.shape
    dv = v.shape[-1]
    L = MLSTM_CHUNK
    tb = _time_block(t, L, 1024)
    nck = tb // L
    col = lambda a: a[..., None]
    row = lambda a: a.reshape(g, t // L, 1, L)
    seq = lambda d: pl.BlockSpec((None, tb, d), lambda i, j: (i, j, 0))
    rows = pl.BlockSpec((None, nck, 1, L), lambda i, j: (i, j, 0, 0))
    return pl.pallas_call(
        functools.partial(_mlstm_kernel, L=L, nck=nck),
        grid=(g, t // tb),
        in_specs=[seq(dk), seq(dk), seq(dv), seq(1), seq(1), rows, rows],
        out_specs=seq(dv),
        out_shape=jax.ShapeDtypeStruct((g, t, dv), F32),
        scratch_shapes=[pltpu.VMEM((dk, dv), F32), pltpu.VMEM((1, dk), F32), pltpu.VMEM((1, 1), F32)],
        compiler_params=pltpu.CompilerParams(
            dimension_semantics=("arbitrary", "arbitrary"),
            vmem_limit_bytes=VMEM_LIMIT_BYTES),
        name="mlstm_scan",
    )(q, k, v, col(li), col(lf), row(li), row(lf))


def _gla_kernel(q_ref, k_ref, v_ref, lf_ref, y_ref, st_ref, *, L, nck):
    @pl.when(pl.program_id(1) == 0)
    def _():
        st_ref[...] = jnp.zeros_like(st_ref)

    ri, ci = _iota2(L, L)
    causal = ci <= ri

    def body(i, carry):
        sl = pl.ds(pl.multiple_of(i * L, L), L)
        q = q_ref[sl, :]
        k = k_ref[sl, :]
        v = v_ref[sl, :]
        lam = _mask_nn(causal, lf_ref[sl, :])
        lam_end = lam[L - 1:L, :]
        st = st_ref[...]
        q_in = q * jnp.exp(lam)
        att = jnp.where(causal, _nt(q_in, k * jnp.exp(-lam)), 0.0)
        y_ref[sl, :] = _nn(att, v) + _nt(q_in, st)
        st_ref[...] = st * jnp.exp(lam_end) + _tn(v, k * jnp.exp(lam_end - lam))
        return carry

    lax.fori_loop(0, nck, body, 0)


def _gla_scan(q, k, v, lf):
    g, t, dk = q.shape
    dv = v.shape[-1]
    L = GLA_CHUNK
    tb = _time_block(t, L, 1024)
    nck = tb // L
    seq = lambda d: pl.BlockSpec((None, tb, d), lambda i, j: (i, j, 0))
    return pl.pallas_call(
        functools.partial(_gla_kernel, L=L, nck=nck),
        grid=(g, t // tb),
        in_specs=[seq(dk), seq(dk), seq(dv), seq(dk)],
        out_specs=seq(dv),
        out_shape=jax.ShapeDtypeStruct((g, t, dv), F32),
        scratch_shapes=[pltpu.VMEM((dv, dk), F32)],
        compiler_params=pltpu.CompilerParams(
            dimension_semantics=("arbitrary", "arbitrary"),
            vmem_limit_bytes=VMEM_LIMIT_BYTES),
        name="gla_scan",
    )(q, k, v, lf)


def _rwkv_kernel(r_ref, lw_ref, k_ref, v_ref, a_ref, b_ref, y_ref, st_ref, *, L, nck):
    @pl.when(pl.program_id(1) == 0)
    def _():
        st_ref[...] = jnp.zeros_like(st_ref)

    ri, ci = _iota2(L, L)
    incl = ci <= ri
    strict = ci < ri
    eye = (ci == ri).astype(F32)
    n_double = int(math.log2(L)) - 1

    def body(i, carry):
        sl = pl.ds(pl.multiple_of(i * L, L), L)
        r = r_ref[sl, :]
        lw = lw_ref[sl, :]
        k = k_ref[sl, :]
        v = v_ref[sl, :]
        a = a_ref[sl, :]
        b = b_ref[sl, :]
        cum = _mask_nn(incl, lw)
        cum_end = cum[L - 1:L, :]
        e_neg = jnp.exp(-cum)
        at = a * jnp.exp(cum - lw)
        bt = b * e_neg
        kt = k * e_neg
        rt = r * jnp.exp(cum)
        st = st_ref[...]
        nmat = jnp.where(strict, _nt(at, bt), 0.0)
        tinv = eye + nmat
        pw = nmat
        for _ in range(n_double):
            pw = _nn(pw, pw)
            tinv = tinv + _nn(tinv, pw)
        rhs = _nt(at, st) + _nn(jnp.where(strict, _nt(at, kt), 0.0), v)
        u = _nn(tinv, rhs)
        y = (_nt(rt, st) + _nn(jnp.where(incl, _nt(rt, bt), 0.0), u)
             + _nn(jnp.where(incl, _nt(rt, kt), 0.0), v))
        y_ref[sl, :] = y
        e_end = jnp.exp(cum_end - cum)
        st_ref[...] = st * jnp.exp(cum_end) + _tn(u, b * e_end) + _tn(v, k * e_end)
        return carry

    lax.fori_loop(0, nck, body, 0)


def _rwkv_scan(r, lw, k, v, a, b):
    g, t, d = r.shape
    L = RWKV_CHUNK
    tb = _time_block(t, L, 1024)
    nck = tb // L
    seq = pl.BlockSpec((None, tb, d), lambda i, j: (i, j, 0))
    return pl.pallas_call(
        functools.partial(_rwkv_kernel, L=L, nck=nck),
        grid=(g, t // tb),
        in_specs=[seq] * 6,
        out_specs=seq,
        out_shape=jax.ShapeDtypeStruct((g, t, d), F32),
        scratch_shapes=[pltpu.VMEM((d, d), F32)],
        compiler_params=pltpu.CompilerParams(
            dimension_semantics=("arbitrary", "arbitrary"),
            vmem_limit_bytes=VMEM_LIMIT_BYTES),
        name="rwkv7_scan",
    )(r, lw, k, v, a, b)


def _bidir_seq(ac, al, bc, bl):
    fwd = jnp.concatenate([ac, al], axis=1)
    bwd = jnp.concatenate([jnp.flip(bc, axis=1), jnp.flip(bl, axis=1)], axis=1)
    return jnp.stack([fwd, bwd])


def _heads(u, h):
    two, b, t, c = u.shape
    return u.reshape(two, b, t, h, c // h).transpose(0, 1, 3, 2, 4).reshape(two * b * h, t, c // h)


def _head_scalars(u):
    two, b, t, h = u.shape
    return u.transpose(0, 1, 3, 2).reshape(two * b * h, t)


def _unheads(y, b, h, n_ctx):
    g, t, d = y.shape
    y = y.reshape(2, b, h, t, d).transpose(0, 1, 3, 2, 4)
    yc = y[0, :, :n_ctx] + jnp.flip(y[1, :, :n_ctx], axis=1)
    yl = y[0, :, n_ctx:] + jnp.flip(y[1, :, n_ctx:], axis=1)
    return yc, yl


def _dwconv3(x, w, b):
    prev = jnp.pad(x[:, :-1], ((0, 0), (1, 0), (0, 0)))
    nxt = jnp.pad(x[:, 1:], ((0, 0), (0, 1), (0, 0)))
    return prev * w[0] + x * w[1] + nxt * w[2] + b


def _layer_norm(x, g, b):
    mu = x.mean(-1, keepdims=True)
    var = jnp.mean(jnp.square(x - mu), -1, keepdims=True)
    return (x - mu) * lax.rsqrt(var + LN_EPS) * g + b


def _head_norm(y, eps, center):
    if center:
        y = y - y.mean(-1, keepdims=True)
    y = y * lax.rsqrt(jnp.mean(y * y, -1, keepdims=True) + eps)
    return y.reshape(y.shape[0], y.shape[1], -1)


def _ssd_mixer(pc, pl_, params, ctx_out):
    conv_w, conv_b, dt_bias, a_log, d_skip, norm_w = params
    bsz, n_ctx = pc.shape[0], pc.shape[1]
    a = -jnp.exp(a_log)

    def prep(p):
        b, t, _ = p.shape
        z = p[..., :A_INNER]
        xbc = jax.nn.silu(_dwconv3(p[..., A_INNER:A_INNER + A_XBC], conv_w, conv_b))
        xs = xbc[..., :A_INNER].reshape(b, t, A_HEADS, A_HEAD_DIM)
        bm = xbc[..., A_INNER:A_INNER + A_GROUPS * A_STATE]
        cm = xbc[..., A_INNER + A_GROUPS * A_STATE:]
        dt = jax.nn.softplus(p[..., A_INNER + A_XBC:].reshape(b, t, 2, A_HEADS) + dt_bias)
        return z, xs, bm, cm, dt

    zc, xsc, bmc, cmc, dtc = prep(pc)
    zl, xsl, bml, cml, dtl = prep(pl_)
    xd = lambda xs, dt, d: (xs * dt[:, :, d, :, None]).reshape(xs.shape[0], xs.shape[1], A_INNER)
    x = _heads(_bidir_seq(xd(xsc, dtc, 0), xd(xsl, dtl, 0), xd(xsc, dtc, 1), xd(xsl, dtl, 1)), A_HEADS)
    la = _head_scalars(_bidir_seq(dtc[:, :, 0] * a[0], dtl[:, :, 0] * a[0], dtc[:, :, 1] * a[1], dtl[:, :, 1] * a[1]))
    bm = _heads(_bidir_seq(bmc, bml, bmc, bml), A_GROUPS)
    cm = _heads(_bidir_seq(cmc, cml, cmc, cml), A_GROUPS)
    y = _ssd_scan(x, la, bm, cm)
    yc, yl = _unheads(y, bsz, A_HEADS, n_ctx)

    def out(y, z, xs):
        b, t = y.shape[:2]
        y = (y + d_skip[:, None] * xs).reshape(b, t, A_INNER) * jax.nn.silu(z)
        return y * lax.rsqrt(jnp.mean(y * y, -1, keepdims=True) + 1e-6) * norm_w

    return (out(yc, zc, xsc) if ctx_out else None), out(yl, zl, xsl)


def _mlstm_mixer(pc, pl_, params, ctx_out):
    conv_w, conv_b, i_bias, f_bias, norm_w = params
    bsz, n_ctx = pc.shape[0], pc.shape[1]

    def prep(p):
        b, t, _ = p.shape
        qk = jax.nn.silu(_dwconv3(p[..., :2 * B_QK], conv_w, conv_b))
        q, k = qk[..., :B_QK], qk[..., B_QK:]
        v = p[..., 2 * B_QK:2 * B_QK + B_INNER]
        o = p[..., 2 * B_QK + B_INNER:2 * B_QK + 2 * B_INNER]
        gates = p[..., 2 * B_QK + 2 * B_INNER:].reshape(b, t, 2, 2, B_HEADS)
        log_i = gates[:, :, 0] + i_bias
        log_f = jax.nn.log_sigmoid(gates[:, :, 1] + f_bias)
        return q, k, v, o, log_i, log_f

    qc, kc, vc, oc, lic, lfc = prep(pc)
    ql, kl, vl, ol, lil, lfl = prep(pl_)
    q = _heads(_bidir_seq(qc, ql, qc, ql), B_HEADS)
    k = _heads(_bidir_seq(kc, kl, kc, kl), B_HEADS)
    v = _heads(_bidir_seq(vc, vl, vc, vl), B_HEADS)
    li = _head_scalars(_bidir_seq(lic[:, :, 0], lil[:, :, 0], lic[:, :, 1], lil[:, :, 1]))
    lf = _head_scalars(_bidir_seq(lfc[:, :, 0], lfl[:, :, 0], lfc[:, :, 1], lfl[:, :, 1]))
    h = _mlstm_scan(q, k, v, li, lf)
    hc, hl = _unheads(h, bsz, B_HEADS, n_ctx)
    out = lambda h, o: jax.nn.sigmoid(o) * _head_norm(h, MLSTM_EPS, True) * norm_w
    return (out(hc, oc) if ctx_out else None), out(hl, ol)


def _hgrn2_mixer(pc, pl_, lb, params, ctx_out):
    f_bias, norm_w = params
    bsz, n_ctx = pc.shape[0], pc.shape[1]

    def prep(p):
        b, t, _ = p.shape
        q = jax.nn.silu(p[..., :C_INNER])
        f_pre = p[..., C_INNER:3 * C_INNER].reshape(b, t, 2, C_INNER) + f_bias
        log_f = jnp.log(lb + (1 - lb) * jax.nn.sigmoid(f_pre))
        k = (1 - lb) * jax.nn.sigmoid(-f_pre)
        return q, k, p[..., 3 * C_INNER:4 * C_INNER], p[..., 4 * C_INNER:], log_f

    qc, kc, ic, gc, lfc = prep(pc)
    ql, kl, il, gl, lfl = prep(pl_)
    q = _heads(_bidir_seq(qc, ql, qc, ql), C_HEADS)
    k = _heads(_bidir_seq(kc[:, :, 0], kl[:, :, 0], kc[:, :, 1], kl[:, :, 1]), C_HEADS)
    v = _heads(_bidir_seq(ic, il, ic, il), C_HEADS)
    lf = _heads(_bidir_seq(lfc[:, :, 0], lfl[:, :, 0], lfc[:, :, 1], lfl[:, :, 1]), C_HEADS)
    o = _gla_scan(q, k, v, lf)
    oc, ol = _unheads(o, bsz, C_HEADS, n_ctx)
    out = lambda o, g: _head_norm(o, 1e-6, False) * norm_w * jax.nn.silu(g)
    return (out(oc, gc) if ctx_out else None), out(ol, gl)


def _rwkv7_mixer(pc, pl_, params, ctx_out):
    mu, w0, w2, a0, a2, g2, k_k, k_a, r_k, ln_w, ln_b = params
    bsz, n_ctx = pc.shape[0], pc.shape[1]

    def prep(p):
        b, t, _ = p.shape
        hd = lambda u: u.reshape(b, t, D_HEADS, D_HEAD_DIM)
        prev = jnp.pad(p[:, :-1], ((0, 0), (1, 0), (0, 0)))
        nxt = jnp.pad(p[:, 1:], ((0, 0), (0, 1), (0, 0)))
        p = p + mu * (0.5 * (prev + nxt) - p)
        r = p[..., :D_INNER]
        k = p[..., D_INNER:2 * D_INNER]
        v = p[..., 2 * D_INNER:3 * D_INNER]
        o = 3 * D_INNER
        wl = jnp.tanh(p[..., o:o + 2 * D_W_LORA]).reshape(b * t, 2, D_W_LORA)
        al = p[..., o + 2 * D_W_LORA:o + 2 * D_W_LORA + D_A_LORA]
        gl = p[..., o + 2 * D_W_LORA + D_A_LORA:]
        w = jnp.stack([_matmul(wl[:, d], w2[d]) for d in range(2)], axis=1).reshape(b, t, 2, D_INNER) + w0
        log_decay = -jnp.exp(-jax.nn.softplus(-w) - 0.5)
        a = jax.nn.sigmoid(a0 + _matmul(al.reshape(b * t, D_A_LORA), a2).reshape(b, t, D_INNER))
        g = _matmul(jax.nn.sigmoid(gl).reshape(b * t, D_G_LORA), g2).reshape(b, t, D_INNER)
        kk = hd(k * k_k)
        kk = (kk * lax.rsqrt(jnp.maximum(jnp.sum(kk * kk, -1, keepdims=True), 1e-12))).reshape(b, t, D_INNER)
        k = k * (1 + (a - 1) * k_a)
        return r, k, v, g, log_decay, -kk, kk * a

    rc, kc, vc, gc, ldc, ac, bc = prep(pc)
    rl, kl, vl, gl, ldl, al, bl = prep(pl_)
    same = lambda c, l: _heads(_bidir_seq(c, l, c, l), D_HEADS)
    lw = _heads(_bidir_seq(ldc[:, :, 0], ldl[:, :, 0], ldc[:, :, 1], ldl[:, :, 1]), D_HEADS)
    y = _rwkv_scan(same(rc, rl), lw, same(kc, kl), same(vc, vl), same(ac, al), same(bc, bl))
    yc, yl = _unheads(y, bsz, D_HEADS, n_ctx)

    def out(y, r, k, v, g):
        b, t = y.shape[:2]
        hd = lambda u: u.reshape(b, t, D_HEADS, D_HEAD_DIM)
        bonus = (jnp.sum(hd(r) * hd(k) * r_k, axis=-1, keepdims=True) * hd(v)).reshape(b, t, D_INNER)
        return (_head_norm(y, RWKV_EPS, True) * ln_w + ln_b + bonus) * g

    return (out(yc, rc, kc, vc, gc) if ctx_out else None), out(yl, rl, kl, vl, gl)


def _to_col_major(u, rows):
    b, s, d = u.shape
    return u.reshape(b, rows, GRID_W, d).transpose(0, 2, 1, 3).reshape(b, s, d)


def _from_col_major(u, rows):
    b, s, d = u.shape
    return u.reshape(b, GRID_W, rows, d).transpose(0, 2, 1, 3).reshape(b, s, d)


def _proj(u, w):
    b, t, d = u.shape
    return _matmul(u.reshape(b * t, d), w).reshape(b, t, w.shape[1])


def kernel(x, c, ctx, c_ctx, mod_w, mod_b, ln_g, ln_b, ev_w_in, ev_w_out, ssd_conv_w, ssd_conv_b, ssd_dt_bias, ssd_a_log, ssd_d, ssd_norm_w, mlstm_conv_w, mlstm_conv_b, mlstm_i_bias, mlstm_f_bias, mlstm_norm_w, od_w_in, od_w_out, hgrn_lb_logits, hgrn_f_bias, hgrn_norm_w, rwkv_mu, rwkv_w0, rwkv_w2, rwkv_a0, rwkv_a2, rwkv_g2, rwkv_k_k, rwkv_k_a, rwkv_r_k, rwkv_ln_w, rwkv_ln_b, router_w, router_bias, exp_w_gate, exp_w_up, exp_w_down):
    bsz, seq, _ = x.shape
    rows = seq // GRID_W
    lb_all = jnp.cumsum(jax.nn.softmax(hgrn_lb_logits.astype(F32), axis=0), axis=0)
    lb_all = lb_all - lb_all[0]
    s_c = jax.nn.silu(c)
    s_cc = jax.nn.silu(c_ctx)
    xl, xc = x, ctx
    for layer in range(DEPTH):
        i = layer // 2
        ctx_out = layer < DEPTH - 1
        mods = _matmul(jnp.concatenate([s_c, s_cc[None]], axis=0), mod_w[layer], tm=8, tn=512) + mod_b[layer]
        sh1, sc1, g1, sh2, sc2, g2 = jnp.split(mods[:bsz], 6, axis=-1)
        csh1, csc1, cg1, csh2, csc2, cg2 = jnp.split(mods[bsz], 6, axis=-1)
        ul = xl * (1 + sc1[:, None]) + sh1[:, None]
        uc = xc * (1 + csc1) + csh1
        if layer % 2 == 0:
            pc, pl_ = _proj(uc, ev_w_in[i]), _proj(ul, ev_w_in[i])
            fca, fla = _ssd_mixer(pc[..., :P_A], pl_[..., :P_A],
                                  (ssd_conv_w[i], ssd_conv_b[i], ssd_dt_bias[i], ssd_a_log[i], ssd_d[i], ssd_norm_w[i]),
                                  ctx_out)
            fcb, flb = _mlstm_mixer(pc[..., P_A:], pl_[..., P_A:],
                                    (mlstm_conv_w[i], mlstm_conv_b[i], mlstm_i_bias[i], mlstm_f_bias[i], mlstm_norm_w[i]),
                                    ctx_out)
            w_out = ev_w_out[i]
        else:
            pc, pl_ = _proj(uc, od_w_in[i]), _proj(_to_col_major(ul, rows), od_w_in[i])
            fca, fla = _hgrn2_mixer(pc[..., :P_C], pl_[..., :P_C], lb_all[layer],
                                    (hgrn_f_bias[i], hgrn_norm_w[i]), ctx_out)
            fcb, flb = _rwkv7_mixer(pc[..., P_C:], pl_[..., P_C:],
                                    (rwkv_mu[i], rwkv_w0[i], rwkv_w2[i], rwkv_a0[i], rwkv_a2[i], rwkv_g2[i],
                                     rwkv_k_k[i], rwkv_k_a[i], rwkv_r_k[i], rwkv_ln_w[i], rwkv_ln_b[i]),
                                    ctx_out)
            w_out = od_w_out[i]
        fl = jnp.concatenate([fla, flb], axis=-1)
        if layer % 2 == 1:
            fl = _from_col_major(fl, rows)
        xl = _layer_norm(DEEPNORM_ALPHA * xl + g1[:, None] * _proj(fl, w_out), ln_g[layer, 0], ln_b[layer, 0])
        hl = xl * (1 + sc2[:, None]) + sh2[:, None]
        if ctx_out:
            fc = jnp.concatenate([fca, fcb], axis=-1)
            xc = _layer_norm(DEEPNORM_ALPHA * xc + cg1 * _proj(fc, w_out), ln_g[layer, 0], ln_b[layer, 0])
            hc = xc * (1 + csc2) + csh2
            n_c = hc.shape[0] * hc.shape[1]
            f = _moe_ffn(jnp.concatenate([hc.reshape(n_c, D_MODEL), hl.reshape(-1, D_MODEL)], axis=0),
                         router_w, router_bias, exp_w_gate[layer], exp_w_up[layer], exp_w_down[layer])
            xc = _layer_norm(DEEPNORM_ALPHA * xc + cg2 * f[:n_c].reshape(xc.shape), ln_g[layer, 1], ln_b[layer, 1])
            f_l = f[n_c:]
        else:
            f_l = _moe_ffn(hl.reshape(-1, D_MODEL), router_w, router_bias,
                           exp_w_gate[layer], exp_w_up[layer], exp_w_down[layer])
        xl = _layer_norm(DEEPNORM_ALPHA * xl + g2[:, None] * f_l.reshape(xl.shape), ln_g[layer, 1], ln_b[layer, 1])
    return xl
```

```python
import functools
import math

import jax
import jax.numpy as jnp
from jax import lax
from jax.experimental import pallas as pl
from jax.experimental.pallas import tpu as pltpu

F32 = jnp.float32
BF16 = jnp.bfloat16

D_MODEL = 1024
DEPTH = 4
GRID_W = 64
A_HEADS = 8
A_HEAD_DIM = 64
A_INNER = A_HEADS * A_HEAD_DIM
A_GROUPS = 2
A_STATE = 64
A_XBC = A_INNER + 2 * A_GROUPS * A_STATE
B_HEADS = 4
B_QK_DIM = 64
B_V_DIM = 128
B_QK = B_HEADS * B_QK_DIM
B_INNER = B_HEADS * B_V_DIM
MLSTM_EPS = 1e-6
C_HEADS = 4
C_HEAD_DIM = 128
C_INNER = C_HEADS * C_HEAD_DIM
D_HEADS = 8
D_HEAD_DIM = 64
D_INNER = D_HEADS * D_HEAD_DIM
D_W_LORA = 64
D_A_LORA = 64
D_G_LORA = 128
RWKV_EPS = 64e-5
P_A = A_INNER + A_XBC + 2 * A_HEADS
P_B = 2 * B_QK + 2 * B_INNER + 4 * B_HEADS
P_C = 5 * C_INNER
P_D = 3 * D_INNER + 2 * D_W_LORA + D_A_LORA + D_G_LORA
N_EXPERTS = 32
N_EXPERT_GROUPS = 8
EXPERTS_PER_GROUP = N_EXPERTS // N_EXPERT_GROUPS
TOP_K = 2
D_EXPERT = 512
MOE_BLOCK = 128
RANK_BLOCK = 512
DEEPNORM_ALPHA = (2 * DEPTH) ** 0.25
LN_EPS = 1e-5
M_INIT = -1e30
NEG_BIG = -1e30

SSD_CHUNK = 128
MLSTM_CHUNK = 64
GLA_CHUNK = 16
RWKV_CHUNK = 32

VMEM_LIMIT_BYTES = 48 * 1024 * 1024
HI = lax.Precision.HIGHEST


def _dot(a, b, dims, exact):
    if exact:
        return lax.dot_general(a.astype(F32), b.astype(F32), (dims, ((), ())),
                               precision=HI, preferred_element_type=F32)
    return lax.dot_general(a.astype(BF16), b.astype(BF16), (dims, ((), ())),
                           preferred_element_type=F32)


def _nn(a, b, exact=False):
    return _dot(a, b, ((1,), (0,)), exact)


def _nt(a, b, exact=False):
    return _dot(a, b, ((1,), (1,)), exact)


def _tn(a, b, exact=False):
    return _dot(a, b, ((0,), (0,)), exact)


def _iota2(n, m):
    return (lax.broadcasted_iota(jnp.int32, (n, m), 0),
            lax.broadcasted_iota(jnp.int32, (n, m), 1))


def _split3(x):
    x1 = x.astype(BF16)
    r1 = x - x1.astype(F32)
    x2 = r1.astype(BF16)
    x3 = (r1 - x2.astype(F32)).astype(BF16)
    return x1, x2, x3


def _mask_nn(mask, x):
    mb = mask.astype(BF16)
    x1, x2, x3 = _split3(x)
    return _nn(mb, x1, False) + _nn(mb, x2, False) + _nn(mb, x3, False)


def _nn_mask(x, mask):
    mb = mask.astype(BF16)
    x1, x2, x3 = _split3(x)
    return _nn(x1, mb, False) + _nn(x2, mb, False) + _nn(x3, mb, False)


def _cum_mats(col, row, L):
    ri, ci = _iota2(L, L)
    ccol = _mask_nn(ci <= ri, jnp.broadcast_to(col, (L, L)))
    crow = _nn_mask(jnp.broadcast_to(row, (L, L)), ri <= ci)
    return ccol, crow


def _mm_kernel(x_ref, w_ref, o_ref, *, exact):
    o_ref[...] = _nn(x_ref[...], w_ref[...], exact)


def _matmul(x, w, tm=512, tn=512, exact=False):
    m, k = x.shape
    n = w.shape[1]
    n_pad = -(-n // tn) * tn
    m_pad = -(-m // tm) * tm
    xb = x if exact else x.astype(BF16)
    wb = w if exact else w.astype(BF16)
    if n_pad != n:
        wb = jnp.pad(wb, ((0, 0), (0, n_pad - n)))
    if m_pad != m:
        xb = jnp.pad(xb, ((0, m_pad - m), (0, 0)))
    out = pl.pallas_call(
        functools.partial(_mm_kernel, exact=exact),
        grid=(n_pad // tn, m_pad // tm),
        in_specs=[pl.BlockSpec((tm, k), lambda j, i: (i, 0)),
                  pl.BlockSpec((k, tn), lambda j, i: (0, j))],
        out_specs=pl.BlockSpec((tm, tn), lambda j, i: (i, j)),
        out_shape=jax.ShapeDtypeStruct((m_pad, n_pad), F32),
        compiler_params=pltpu.CompilerParams(
            dimension_semantics=("arbitrary", "arbitrary"),
            vmem_limit_bytes=VMEM_LIMIT_BYTES),
        name="dense_matmul",
    )(xb, wb)
    return out[:m, :n]


def _moe_kernel(blk_e_ref, n_used_ref, x_ref, wg_ref, wu_ref, wd_ref, o_ref):
    i = pl.program_id(0)

    @pl.when(i < n_used_ref[0])
    def _():
        x = x_ref[...]
        g = jnp.dot(x, wg_ref[...], preferred_element_type=F32)
        u = jnp.dot(x, wu_ref[...], preferred_element_type=F32)
        hmid = (g * jax.nn.sigmoid(g) * u).astype(BF16)
        o_ref[...] = jnp.dot(hmid, wd_ref[...], preferred_element_type=F32)

    @pl.when(i >= n_used_ref[0])
    def _():
        o_ref[...] = jnp.zeros_like(o_ref)


def _moe_experts(xp, blk_e, n_used, w_gate, w_up, w_down):
    n_rows, d = xp.shape
    n_blocks = n_rows // MOE_BLOCK
    grid_spec = pltpu.PrefetchScalarGridSpec(
        num_scalar_prefetch=2,
        grid=(n_blocks,),
        in_specs=[
            pl.BlockSpec((MOE_BLOCK, d), lambda i, be, nu: (i, 0)),
            pl.BlockSpec((None, d, D_EXPERT), lambda i, be, nu: (be[i], 0, 0)),
            pl.BlockSpec((None, d, D_EXPERT), lambda i, be, nu: (be[i], 0, 0)),
            pl.BlockSpec((None, D_EXPERT, d), lambda i, be, nu: (be[i], 0, 0)),
        ],
        out_specs=pl.BlockSpec((MOE_BLOCK, d), lambda i, be, nu: (i, 0)),
    )
    return pl.pallas_call(
        _moe_kernel,
        grid_spec=grid_spec,
        out_shape=jax.ShapeDtypeStruct((n_rows, d), F32),
        compiler_params=pltpu.CompilerParams(
            dimension_semantics=("arbitrary",),
            vmem_limit_bytes=VMEM_LIMIT_BYTES),
        name="moe_experts",
    )(blk_e, n_used, xp, w_gate, w_up, w_down)


def _top2(vals):
    m = len(vals)
    m1 = functools.reduce(jnp.maximum, vals)
    i1 = jnp.full_like(m1, float(m - 1))
    for j in reversed(range(m - 1)):
        i1 = jnp.where(vals[j] == m1, float(j), i1)
    rest = [jnp.where(i1 == float(j), -jnp.inf, vals[j]) for j in range(m)]
    m2 = functools.reduce(jnp.maximum, rest)
    i2 = jnp.full_like(m1, float(m - 1))
    for j in reversed(range(m - 1)):
        i2 = jnp.where(rest[j] == m2, float(j), i2)
    return m1, i1, m2, i2


def _router_kernel(h_ref, wt_ref, bias_ref, e_ref, w_ref, rank_ref, cnt_ref, carry_ref):
    tm = h_ref.shape[0]
    ng, per = N_EXPERT_GROUPS, EXPERTS_PER_GROUP

    @pl.when(pl.program_id(0) == 0)
    def _():
        carry_ref[...] = jnp.zeros_like(carry_ref)

    s = jax.nn.sigmoid(_nt(wt_ref[...], h_ref[...], True))
    sb = s + bias_ref[...]
    biased = [sb[j * ng:(j + 1) * ng, :] for j in range(per)]
    plain = [s[j * ng:(j + 1) * ng, :] for j in range(per)]
    m1, _, m2, _ = _top2(biased)
    gsum = m1 + m2
    rows = lax.broadcasted_iota(jnp.int32, (ng, tm), 0).astype(F32)
    gmax = jnp.max(gsum, axis=0, keepdims=True)
    gi = jnp.min(jnp.where(gsum == gmax, rows, float(ng)), axis=0, keepdims=True)
    sel = rows == gi
    pick = lambda v: jnp.sum(jnp.where(sel, v, 0.0), axis=0, keepdims=True)
    in_b = [pick(v) for v in biased]
    in_s = [pick(v) for v in plain]
    _, l1, _, l2 = _top2(in_b)
    w1 = functools.reduce(jnp.add, [jnp.where(l1 == float(j), in_s[j], 0.0) for j in range(per)])
    w2 = functools.reduce(jnp.add, [jnp.where(l2 == float(j), in_s[j], 0.0) for j in range(per)])
    e1 = gi * float(per) + l1
    e2 = gi * float(per) + l2
    wsum = w1 + w2
    e_ref[0:1, :] = e1.astype(jnp.int32)
    e_ref[1:2, :] = e2.astype(jnp.int32)
    w_ref[0:1, :] = w1 / wsum
    w_ref[1:2, :] = w2 / wsum
    row = lax.broadcasted_iota(jnp.int32, (N_EXPERTS, tm), 0)
    experts = ((row % ng) * per + row // ng).astype(F32)
    oh1 = jnp.where(experts == e1, 1.0, 0.0)
    oh2 = jnp.where(experts == e2, 1.0, 0.0)
    oh = oh1 + oh2
    ri, ci = _iota2(tm, tm)
    seen = _nn(oh, jnp.where(ri < ci, 1.0, 0.0)) + carry_ref[...]
    rank_ref[0:1, :] = jnp.sum(oh1 * seen, axis=0, keepdims=True).astype(jnp.int32)
    rank_ref[1:2, :] = jnp.sum(oh2 * seen, axis=0, keepdims=True).astype(jnp.int32)
    carry = carry_ref[...] + jnp.sum(oh, axis=1, keepdims=True)
    carry_ref[...] = carry
    cnt_ref[...] = carry.astype(jnp.int32)


def _route(h, router_w, router_bias):
    t, d = h.shape
    tm = RANK_BLOCK
    kt = lambda dt: jax.ShapeDtypeStruct((TOP_K, t), dt)
    blk = pl.BlockSpec((TOP_K, tm), lambda i: (0, i))
    member_major = lambda a: a.reshape(N_EXPERT_GROUPS, EXPERTS_PER_GROUP, -1).transpose(1, 0, 2).reshape(N_EXPERTS, -1)
    e, w, rank, counts = pl.pallas_call(
        _router_kernel,
        grid=(t // tm,),
        in_specs=[pl.BlockSpec((tm, d), lambda i: (i, 0)),
                  pl.BlockSpec((N_EXPERTS, d), lambda i: (0, 0)),
                  pl.BlockSpec((N_EXPERTS, 1), lambda i: (0, 0))],
        out_specs=[blk, blk, blk, pl.BlockSpec((N_EXPERTS, 1), lambda i: (0, 0))],
        out_shape=[kt(jnp.int32), kt(F32), kt(jnp.int32), jax.ShapeDtypeStruct((N_EXPERTS, 1), jnp.int32)],
        scratch_shapes=[pltpu.VMEM((N_EXPERTS, 1), F32)],
        compiler_params=pltpu.CompilerParams(dimension_semantics=("arbitrary",),
                                             vmem_limit_bytes=VMEM_LIMIT_BYTES),
        name="moe_router",
    )(h, member_major(router_w.T), member_major(router_bias.astype(F32).reshape(N_EXPERTS, 1)))
    counts = counts.reshape(EXPERTS_PER_GROUP, N_EXPERT_GROUPS).T.reshape(N_EXPERTS)
    return e, w, rank, counts


def _moe_ffn(h, router_w, router_bias, w_gate, w_up, w_down):
    t, d = h.shape
    expert, wts, rank, counts = _route(h, router_w, router_bias)
    n_assign = t * TOP_K
    padded = (counts + MOE_BLOCK - 1) // MOE_BLOCK * MOE_BLOCK
    pends = jnp.cumsum(padded)
    pstarts = pends - padded
    dest = pstarts[expert] + rank
    n_blocks = -(-n_assign // MOE_BLOCK) + N_EXPERTS
    hb = h.astype(BF16)
    xp = jnp.zeros((n_blocks * MOE_BLOCK, d), BF16)
    for kk in range(TOP_K):
        xp = xp.at[dest[kk]].set(hb, unique_indices=True)
    blk_e = jnp.minimum(jnp.searchsorted(pends, jnp.arange(n_blocks) * MOE_BLOCK, side='right'),
                        N_EXPERTS - 1).astype(jnp.int32)
    n_used = (pends[-1] // MOE_BLOCK).astype(jnp.int32).reshape(1)
    yp = _moe_experts(xp, blk_e, n_used, w_gate.astype(BF16), w_up.astype(BF16), w_down.astype(BF16))
    out = yp[dest[0]] * wts[0][:, None]
    for kk in range(1, TOP_K):
        out = out + yp[dest[kk]] * wts[kk][:, None]
    return out


def _time_block(t, chunk, target):
    n = t // chunk
    best = 1
    for c in range(1, n + 1):
        if n % c == 0 and c * chunk <= target:
            best = c
    return best * chunk


def _ssd_kernel(x_ref, lac_ref, lar_ref, b_ref, c_ref, y_ref, st_ref, *, L, nck):
    P = x_ref.shape[-1]

    @pl.when(pl.program_id(1) == 0)
    def _():
        st_ref[...] = jnp.zeros_like(st_ref)

    ri, ci = _iota2(L, L)
    causal = ci <= ri

    def body(i, carry):
        sl = pl.ds(pl.multiple_of(i * L, L), L)
        x = x_ref[sl, :]
        bm = b_ref[sl, :]
        cm = c_ref[sl, :]
        ccol, crow = _cum_mats(lac_ref[sl, :], lar_ref[i], L)
        decay = jnp.exp(jnp.where(causal, ccol - crow, NEG_BIG))
        att = _nt(cm, bm) * decay
        st = st_ref[...]
        cum_p = ccol[:, :P]
        end_p = ccol[L - 1:L, :P]
        y = _nn(att, x) + jnp.exp(cum_p) * _nn(cm, st)
        y_ref[sl, :] = y
        st_ref[...] = jnp.exp(end_p) * st + _tn(bm, x * jnp.exp(end_p - cum_p))
        return carry

    lax.fori_loop(0, nck, body, 0)


def _ssd_scan(x, la, bm, cm):
    g, t, p = x.shape
    n = bm.shape[-1]
    hpg = g // bm.shape[0]
    L = SSD_CHUNK
    tb = _time_block(t, L, 1024)
    nck = tb // L
    lac = la[..., None]
    lar = la.reshape(g, t // L, 1, L)
    return pl.pallas_call(
        functools.partial(_ssd_kernel, L=L, nck=nck),
        grid=(g, t // tb),
        in_specs=[
            pl.BlockSpec((None, tb, p), lambda i, j: (i, j, 0)),
            pl.BlockSpec((None, tb, 1), lambda i, j: (i, j, 0)),
            pl.BlockSpec((None, nck, 1, L), lambda i, j: (i, j, 0, 0)),
            pl.BlockSpec((None, tb, n), lambda i, j: (i // hpg, j, 0)),
            pl.BlockSpec((None, tb, n), lambda i, j: (i // hpg, j, 0)),
        ],
        out_specs=pl.BlockSpec((None, tb, p), lambda i, j: (i, j, 0)),
        out_shape=jax.ShapeDtypeStruct((g, t, p), F32),
        scratch_shapes=[pltpu.VMEM((n, p), F32)],
        compiler_params=pltpu.CompilerParams(
            dimension_semantics=("arbitrary", "arbitrary"),
            vmem_limit_bytes=VMEM_LIMIT_BYTES),
        name="ssd_scan",
    )(x, lac, lar, bm, cm)


def _mlstm_kernel(q_ref, k_ref, v_ref, lic_ref, lfc_ref, lir_ref, lfr_ref, h_ref,
                  c_ref, n_ref, m_ref, *, L, nck):
    dk = q_ref.shape[-1]

    @pl.when(pl.program_id(1) == 0)
    def _():
        c_ref[...] = jnp.zeros_like(c_ref)
        n_ref[...] = jnp.zeros_like(n_ref)
        m_ref[...] = jnp.full_like(m_ref, M_INIT)

    ri, ci = _iota2(L, L)
    causal = ci <= ri

    def body(i, carry):
        sl = pl.ds(pl.multiple_of(i * L, L), L)
        q = q_ref[sl, :] * (dk ** -0.5)
        k = k_ref[sl, :]
        v = v_ref[sl, :]
        li_c = lic_ref[sl, :]
        li_r = lir_ref[i]
        fcol, frow = _cum_mats(lfc_ref[sl, :], lfr_ref[i], L)
        f_c = fcol[:, :1]
        ftot = fcol[L - 1:L, :1]
        c_prev = c_ref[...]
        n_prev = n_ref[...]
        m_prev = m_ref[...]
        w_end = ftot - f_c + li_c
        m_loc = jnp.max(w_end, axis=0, keepdims=True)
        ke = k * jnp.exp(w_end - m_loc)
        c_loc = _tn(ke, v)
        n_loc = jnp.sum(ke, axis=0, keepdims=True)
        log_d = jnp.where(causal, fcol - frow + li_r, NEG_BIG)
        log_inter = f_c + m_prev
        m_row = jnp.maximum(jnp.max(log_d, axis=-1, keepdims=True), log_inter)
        s = _nt(q, k) * jnp.exp(log_d - m_row)
        inter = jnp.exp(log_inter - m_row)
        num = _nn(s, v) + inter * _nn(q, c_prev)
        den = jnp.sum(s, axis=-1, keepdims=True) + inter * jnp.sum(q * n_prev, axis=-1, keepdims=True)
        h_ref[sl, :] = num / jnp.maximum(jnp.abs(den), jnp.exp(-m_row))
        m_new = jnp.maximum(ftot + m_prev, m_loc)
        sp = jnp.exp(ftot + m_prev - m_new)
        sc = jnp.exp(m_loc - m_new)
        c_ref[...] = sp * c_prev + sc * c_loc
        n_ref[...] = sp * n_prev + sc * n_loc
        m_ref[...] = m_new
        return carry

    lax.fori_loop(0, nck, body, 0)


def _mlstm_scan(q, k, v, li, lf):
    g, t, dk = q.shape
    dv = v.shape[-1]
    L = MLSTM_CHUNK
    tb = _time_block(t, L, 1024)
    nck = tb // L
    col = lambda a: a[..., None]
    row = lambda a: a.reshape(g, t // L, 1, L)
    seq = lambda d: pl.BlockSpec((None, tb, d), lambda i, j: (i, j, 0))
    rows = pl.BlockSpec((None, nck, 1, L), lambda i, j: (i, j, 0, 0))
    return pl.pallas_call(
        functools.partial(_mlstm_kernel, L=L, nck=nck),
        grid=(g, t // tb),
        in_specs=[seq(dk), seq(dk), seq(dv), seq(1), seq(1), rows, rows],
        out_specs=seq(dv),
        out_shape=jax.ShapeDtypeStruct((g, t, dv), F32),
        scratch_shapes=[pltpu.VMEM((dk, dv), F32), pltpu.VMEM((1, dk), F32), pltpu.VMEM((1, 1), F32)],
        compiler_params=pltpu.CompilerParams(
            dimension_semantics=("arbitrary", "arbitrary"),
            vmem_limit_bytes=VMEM_LIMIT_BYTES),
        name="mlstm_scan",
    )(q, k, v, col(li), col(lf), row(li), row(lf))


def _gla_kernel(q_ref, k_ref, v_ref, lf_ref, y_ref, st_ref, *, L, nck):
    @pl.when(pl.program_id(1) == 0)
    def _():
        st_ref[...] = jnp.zeros_like(st_ref)

    ri, ci = _iota2(L, L)
    causal = ci <= ri

    def body(i, carry):
        sl = pl.ds(pl.multiple_of(i * L, L), L)
        q = q_ref[sl, :]
        k = k_ref[sl, :]
        v = v_ref[sl, :]
        lam = _mask_nn(causal, lf_ref[sl, :])
        lam_end = lam[L - 1:L, :]
        st = st_ref[...]
        q_in = q * jnp.exp(lam)
        att = jnp.where(causal, _nt(q_in, k * jnp.exp(-lam)), 0.0)
        y_ref[sl, :] = _nn(att, v) + _nt(q_in, st)
        st_ref[...] = st * jnp.exp(lam_end) + _tn(v, k * jnp.exp(lam_end - lam))
        return carry

    lax.fori_loop(0, nck, body, 0)


def _gla_scan(q, k, v, lf):
    g, t, dk = q.shape
    dv = v.shape[-1]
    L = GLA_CHUNK
    tb = _time_block(t, L, 1024)
    nck = tb // L
    seq = lambda d: pl.BlockSpec((None, tb, d), lambda i, j: (i, j, 0))
    return pl.pallas_call(
        functools.partial(_gla_kernel, L=L, nck=nck),
        grid=(g, t // tb),
        in_specs=[seq(dk), seq(dk), seq(dv), seq(dk)],
        out_specs=seq(dv),
        out_shape=jax.ShapeDtypeStruct((g, t, dv), F32),
        scratch_shapes=[pltpu.VMEM((dv, dk), F32)],
        compiler_params=pltpu.CompilerParams(
            dimension_semantics=("arbitrary", "arbitrary"),
            vmem_limit_bytes=VMEM_LIMIT_BYTES),
        name="gla_scan",
    )(q, k, v, lf)


def _rwkv_kernel(r_ref, lw_ref, k_ref, v_ref, a_ref, b_ref, y_ref, st_ref, *, L, nck):
    @pl.when(pl.program_id(1) == 0)
    def _():
        st_ref[...] = jnp.zeros_like(st_ref)

    ri, ci = _iota2(L, L)
    incl = ci <= ri
    strict = ci < ri
    eye = (ci == ri).astype(F32)
    n_double = int(math.log2(L)) - 1

    def body(i, carry):
        sl = pl.ds(pl.multiple_of(i * L, L), L)
        r = r_ref[sl, :]
        lw = lw_ref[sl, :]
        k = k_ref[sl, :]
        v = v_ref[sl, :]
        a = a_ref[sl, :]
        b = b_ref[sl, :]
        cum = _mask_nn(incl, lw)
        cum_end = cum[L - 1:L, :]
        e_neg = jnp.exp(-cum)
        at = a * jnp.exp(cum - lw)
        bt = b * e_neg
        kt = k * e_neg
        rt = r * jnp.exp(cum)
        st = st_ref[...]
        nmat = jnp.where(strict, _nt(at, bt), 0.0)
        tinv = eye + nmat
        pw = nmat
        for _ in range(n_double):
            pw = _nn(pw, pw)
            tinv = tinv + _nn(tinv, pw)
        rhs = _nt(at, st) + _nn(jnp.where(strict, _nt(at, kt), 0.0), v)
        u = _nn(tinv, rhs)
        y = (_nt(rt, st) + _nn(jnp.where(incl, _nt(rt, bt), 0.0), u)
             + _nn(jnp.where(incl, _nt(rt, kt), 0.0), v))
        y_ref[sl, :] = y
        e_end = jnp.exp(cum_end - cum)
        st_ref[...] = st * jnp.exp(cum_end) + _tn(u, b * e_end) + _tn(v, k * e_end)
        return carry

    lax.fori_loop(0, nck, body, 0)


def _rwkv_scan(r, lw, k, v, a, b):
    g, t, d = r.shape
    L = RWKV_CHUNK
    tb = _time_block(t, L, 1024)
    nck = tb // L
    seq = pl.BlockSpec((None, tb, d), lambda i, j: (i, j, 0))
    return pl.pallas_call(
        functools.partial(_rwkv_kernel, L=L, nck=nck),
        grid=(g, t // tb),
        in_specs=[seq] * 6,
        out_specs=seq,
        out_shape=jax.ShapeDtypeStruct((g, t, d), F32),
        scratch_shapes=[pltpu.VMEM((d, d), F32)],
        compiler_params=pltpu.CompilerParams(
            dimension_semantics=("arbitrary", "arbitrary"),
            vmem_limit_bytes=VMEM_LIMIT_BYTES),
        name="rwkv7_scan",
    )(r, lw, k, v, a, b)


def _bidir_seq(ac, al, bc, bl):
    fwd = jnp.concatenate([ac, al], axis=1)
    bwd = jnp.concatenate([jnp.flip(bc, axis=1), jnp.flip(bl, axis=1)], axis=1)
    return jnp.stack([fwd, bwd])


def _heads(u, h):
    two, b, t, c = u.shape
    return u.reshape(two, b, t, h, c // h).transpose(0, 1, 3, 2, 4).reshape(two * b * h, t, c // h)


def _head_scalars(u):
    two, b, t, h = u.shape
    return u.transpose(0, 1, 3, 2).reshape(two * b * h, t)


def _unheads(y, b, h, n_ctx):
    g, t, d = y.shape
    y = y.reshape(2, b, h, t, d).transpose(0, 1, 3, 2, 4)
    yc = y[0, :, :n_ctx] + jnp.flip(y[1, :, :n_ctx], axis=1)
    yl = y[0, :, n_ctx:] + jnp.flip(y[1, :, n_ctx:], axis=1)
    return yc, yl


def _dwconv3(x, w, b):
    prev = jnp.pad(x[:, :-1], ((0, 0), (1, 0), (0, 0)))
    nxt = jnp.pad(x[:, 1:], ((0, 0), (0, 1), (0, 0)))
    return prev * w[0] + x * w[1] + nxt * w[2] + b


def _layer_norm(x, g, b):
    mu = x.mean(-1, keepdims=True)
    var = jnp.mean(jnp.square(x - mu), -1, keepdims=True)
    return (x - mu) * lax.rsqrt(var + LN_EPS) * g + b


def _head_norm(y, eps, center):
    if center:
        y = y - y.mean(-1, keepdims=True)
    y = y * lax.rsqrt(jnp.mean(y * y, -1, keepdims=True) + eps)
    return y.reshape(y.shape[0], y.shape[1], -1)


def _ssd_mixer(pc, pl_, params, ctx_out):
    conv_w, conv_b, dt_bias, a_log, d_skip, norm_w = params
    bsz, n_ctx = pc.shape[0], pc.shape[1]
    a = -jnp.exp(a_log)

    def prep(p):
        b, t, _ = p.shape
        z = p[..., :A_INNER]
        xbc = jax.nn.silu(_dwconv3(p[..., A_INNER:A_INNER + A_XBC], conv_w, conv_b))
        xs = xbc[..., :A_INNER].reshape(b, t, A_HEADS, A_HEAD_DIM)
        bm = xbc[..., A_INNER:A_INNER + A_GROUPS * A_STATE]
        cm = xbc[..., A_INNER + A_GROUPS * A_STATE:]
        dt = jax.nn.softplus(p[..., A_INNER + A_XBC:].reshape(b, t, 2, A_HEADS) + dt_bias)
        return z, xs, bm, cm, dt

    zc, xsc, bmc, cmc, dtc = prep(pc)
    zl, xsl, bml, cml, dtl = prep(pl_)
    xd = lambda xs, dt, d: (xs * dt[:, :, d, :, None]).reshape(xs.shape[0], xs.shape[1], A_INNER)
    x = _heads(_bidir_seq(xd(xsc, dtc, 0), xd(xsl, dtl, 0), xd(xsc, dtc, 1), xd(xsl, dtl, 1)), A_HEADS)
    la = _head_scalars(_bidir_seq(dtc[:, :, 0] * a[0], dtl[:, :, 0] * a[0], dtc[:, :, 1] * a[1], dtl[:, :, 1] * a[1]))
    bm = _heads(_bidir_seq(bmc, bml, bmc, bml), A_GROUPS)
    cm = _heads(_bidir_seq(cmc, cml, cmc, cml), A_GROUPS)
    y = _ssd_scan(x, la, bm, cm)
    yc, yl = _unheads(y, bsz, A_HEADS, n_ctx)

    def out(y, z, xs):
        b, t = y.shape[:2]
        y = (y + d_skip[:, None] * xs).reshape(b, t, A_INNER) * jax.nn.silu(z)
        return y * lax.rsqrt(jnp.mean(y * y, -1, keepdims=True) + 1e-6) * norm_w

    return (out(yc, zc, xsc) if ctx_out else None), out(yl, zl, xsl)


def _mlstm_mixer(pc, pl_, params, ctx_out):
    conv_w, conv_b, i_bias, f_bias, norm_w = params
    bsz, n_ctx = pc.shape[0], pc.shape[1]

    def prep(p):
        b, t, _ = p.shape
        qk = jax.nn.silu(_dwconv3(p[..., :2 * B_QK], conv_w, conv_b))
        q, k = qk[..., :B_QK], qk[..., B_QK:]
        v = p[..., 2 * B_QK:2 * B_QK + B_INNER]
        o = p[..., 2 * B_QK + B_INNER:2 * B_QK + 2 * B_INNER]
        gates = p[..., 2 * B_QK + 2 * B_INNER:].reshape(b, t, 2, 2, B_HEADS)
        log_i = gates[:, :, 0] + i_bias
        log_f = jax.nn.log_sigmoid(gates[:, :, 1] + f_bias)
        return q, k, v, o, log_i, log_f

    qc, kc, vc, oc, lic, lfc = prep(pc)
    ql, kl, vl, ol, lil, lfl = prep(pl_)
    q = _heads(_bidir_seq(qc, ql, qc, ql), B_HEADS)
    k = _heads(_bidir_seq(kc, kl, kc, kl), B_HEADS)
    v = _heads(_bidir_seq(vc, vl, vc, vl), B_HEADS)
    li = _head_scalars(_bidir_seq(lic[:, :, 0], lil[:, :, 0], lic[:, :, 1], lil[:, :, 1]))
    lf = _head_scalars(_bidir_seq(lfc[:, :, 0], lfl[:, :, 0], lfc[:, :, 1], lfl[:, :, 1]))
    h = _mlstm_scan(q, k, v, li, lf)
    hc, hl = _unheads(h, bsz, B_HEADS, n_ctx)
    out = lambda h, o: jax.nn.sigmoid(o) * _head_norm(h, MLSTM_EPS, True) * norm_w
    return (out(hc, oc) if ctx_out else None), out(hl, ol)


def _hgrn2_mixer(pc, pl_, lb, params, ctx_out):
    f_bias, norm_w = params
    bsz, n_ctx = pc.shape[0], pc.shape[1]

    def prep(p):
        b, t, _ = p.shape
        q = jax.nn.silu(p[..., :C_INNER])
        f_pre = p[..., C_INNER:3 * C_INNER].reshape(b, t, 2, C_INNER) + f_bias
        log_f = jnp.log(lb + (1 - lb) * jax.nn.sigmoid(f_pre))
        k = (1 - lb) * jax.nn.sigmoid(-f_pre)
        return q, k, p[..., 3 * C_INNER:4 * C_INNER], p[..., 4 * C_INNER:], log_f

    qc, kc, ic, gc, lfc = prep(pc)
    ql, kl, il, gl, lfl = prep(pl_)
    q = _heads(_bidir_seq(qc, ql, qc, ql), C_HEADS)
    k = _heads(_bidir_seq(kc[:, :, 0], kl[:, :, 0], kc[:, :, 1], kl[:, :, 1]), C_HEADS)
    v = _heads(_bidir_seq(ic, il, ic, il), C_HEADS)
    lf = _heads(_bidir_seq(lfc[:, :, 0], lfl[:, :, 0], lfc[:, :, 1], lfl[:, :, 1]), C_HEADS)
    o = _gla_scan(q, k, v, lf)
    oc, ol = _unheads(o, bsz, C_HEADS, n_ctx)
    out = lambda o, g: _head_norm(o, 1e-6, False) * norm_w * jax.nn.silu(g)
    return (out(oc, gc) if ctx_out else None), out(ol, gl)


def _rwkv7_mixer(pc, pl_, params, ctx_out):
    mu, w0, w2, a0, a2, g2, k_k, k_a, r_k, ln_w, ln_b = params
    bsz, n_ctx = pc.shape[0], pc.shape[1]

    def prep(p):
        b, t, _ = p.shape
        hd = lambda u: u.reshape(b, t, D_HEADS, D_HEAD_DIM)
        prev = jnp.pad(p[:, :-1], ((0, 0), (1, 0), (0, 0)))
        nxt = jnp.pad(p[:, 1:], ((0, 0), (0, 1), (0, 0)))
        p = p + mu * (0.5 * (prev + nxt) - p)
        r = p[..., :D_INNER]
        k = p[..., D_INNER:2 * D_INNER]
        v = p[..., 2 * D_INNER:3 * D_INNER]
        o = 3 * D_INNER
        wl = jnp.tanh(p[..., o:o + 2 * D_W_LORA]).reshape(b * t, 2, D_W_LORA)
        al = p[..., o + 2 * D_W_LORA:o + 2 * D_W_LORA + D_A_LORA]
        gl = p[..., o + 2 * D_W_LORA + D_A_LORA:]
        w = jnp.stack([_matmul(wl[:, d], w2[d]) for d in range(2)], axis=1).reshape(b, t, 2, D_INNER) + w0
        log_decay = -jnp.exp(-jax.nn.softplus(-w) - 0.5)
        a = jax.nn.sigmoid(a0 + _matmul(al.reshape(b * t, D_A_LORA), a2).reshape(b, t, D_INNER))
        g = _matmul(jax.nn.sigmoid(gl).reshape(b * t, D_G_LORA), g2).reshape(b, t, D_INNER)
        kk = hd(k * k_k)
        kk = (kk * lax.rsqrt(jnp.maximum(jnp.sum(kk * kk, -1, keepdims=True), 1e-12))).reshape(b, t, D_INNER)
        k = k * (1 + (a - 1) * k_a)
        return r, k, v, g, log_decay, -kk, kk * a

    rc, kc, vc, gc, ldc, ac, bc = prep(pc)
    rl, kl, vl, gl, ldl, al, bl = prep(pl_)
    same = lambda c, l: _heads(_bidir_seq(c, l, c, l), D_HEADS)
    lw = _heads(_bidir_seq(ldc[:, :, 0], ldl[:, :, 0], ldc[:, :, 1], ldl[:, :, 1]), D_HEADS)
    y = _rwkv_scan(same(rc, rl), lw, same(kc, kl), same(vc, vl), same(ac, al), same(bc, bl))
    yc, yl = _unheads(y, bsz, D_HEADS, n_ctx)

    def out(y, r, k, v, g):
        b, t = y.shape[:2]
        hd = lambda u: u.reshape(b, t, D_HEADS, D_HEAD_DIM)
        bonus = (jnp.sum(hd(r) * hd(k) * r_k, axis=-1, keepdims=True) * hd(v)).reshape(b, t, D_INNER)
        return (_head_norm(y, RWKV_EPS, True) * ln_w + ln_b + bonus) * g

    return (out(yc, rc, kc, vc, gc) if ctx_out else None), out(yl, rl, kl, vl, gl)


def _to_col_major(u, rows):
    b, s, d = u.shape
    return u.reshape(b, rows, GRID_W, d).transpose(0, 2, 1, 3).reshape(b, s, d)


def _from_col_major(u, rows):
    b, s, d = u.shape
    return u.reshape(b, GRID_W, rows, d).transpose(0, 2, 1, 3).reshape(b, s, d)


def _proj(u, w):
    b, t, d = u.shape
    return _matmul(u.reshape(b * t, d), w).reshape(b, t, w.shape[1])


def kernel(x, c, ctx, c_ctx, mod_w, mod_b, ln_g, ln_b, ev_w_in, ev_w_out, ssd_conv_w, ssd_conv_b, ssd_dt_bias, ssd_a_log, ssd_d, ssd_norm_w, mlstm_conv_w, mlstm_conv_b, mlstm_i_bias, mlstm_f_bias, mlstm_norm_w, od_w_in, od_w_out, hgrn_lb_logits, hgrn_f_bias, hgrn_norm_w, rwkv_mu, rwkv_w0, rwkv_w2, rwkv_a0, rwkv_a2, rwkv_g2, rwkv_k_k, rwkv_k_a, rwkv_r_k, rwkv_ln_w, rwkv_ln_b, router_w, router_bias, exp_w_gate, exp_w_up, exp_w_down):
    bsz, seq, _ = x.shape
    rows = seq // GRID_W
    lb_all = jnp.cumsum(jax.nn.softmax(hgrn_lb_logits.astype(F32), axis=0), axis=0)
    lb_all = lb_all - lb_all[0]
    s_c = jax.nn.silu(c)
    s_cc = jax.nn.silu(c_ctx)
    xl, xc = x, ctx
    for layer in range(DEPTH):
        i = layer // 2
        ctx_out = layer < DEPTH - 1
        mods = _matmul(jnp.concatenate([s_c, s_cc[None]], axis=0), mod_w[layer], tm=8, tn=512) + mod_b[layer]
        sh1, sc1, g1, sh2, sc2, g2 = jnp.split(mods[:bsz], 6, axis=-1)
        csh1, csc1, cg1, csh2, csc2, cg2 = jnp.split(mods[bsz], 6, axis=-1)
        ul = xl * (1 + sc1[:, None]) + sh1[:, None]
        uc = xc * (1 + csc1) + csh1
        if layer % 2 == 0:
            pc, pl_ = _proj(uc, ev_w_in[i]), _proj(ul, ev_w_in[i])
            fca, fla = _ssd_mixer(pc[..., :P_A], pl_[..., :P_A],
                                  (ssd_conv_w[i], ssd_conv_b[i], ssd_dt_bias[i], ssd_a_log[i], ssd_d[i], ssd_norm_w[i]),
                                  ctx_out)
            fcb, flb = _mlstm_mixer(pc[..., P_A:], pl_[..., P_A:],
                                    (mlstm_conv_w[i], mlstm_conv_b[i], mlstm_i_bias[i], mlstm_f_bias[i], mlstm_norm_w[i]),
                                    ctx_out)
            w_out = ev_w_out[i]
        else:
            pc, pl_ = _proj(uc, od_w_in[i]), _proj(_to_col_major(ul, rows), od_w_in[i])
            fca, fla = _hgrn2_mixer(pc[..., :P_C], pl_[..., :P_C], lb_all[layer],
                                    (hgrn_f_bias[i], hgrn_norm_w[i]), ctx_out)
            fcb, flb = _rwkv7_mixer(pc[..., P_C:], pl_[..., P_C:],
                                    (rwkv_mu[i], rwkv_w0[i], rwkv_w2[i], rwkv_a0[i], rwkv_a2[i], rwkv_g2[i],
                                     rwkv_k_k[i], rwkv_k_a[i], rwkv_r_k[i], rwkv_ln_w[i], rwkv_ln_b[i]),
                                    ctx_out)
            w_out = od_w_out[i]
        fl = jnp.concatenate([fla, flb], axis=-1)
        if layer % 2 == 1:
            fl = _from_col_major(fl, rows)
        xl = _layer_norm(DEEPNORM_ALPHA * xl + g1[:, None] * _proj(fl, w_out), ln_g[layer, 0], ln_b[layer, 0])
        hl = xl * (1 + sc2[:, None]) + sh2[:, None]
        if ctx_out:
            fc = jnp.concatenate([fca, fcb], axis=-1)
            xc = _layer_norm(DEEPNORM_ALPHA * xc + cg1 * _proj(fc, w_out), ln_g[layer, 0], ln_b[layer, 0])
            hc = xc * (1 + csc2) + csh2
            n_c = hc.shape[0] * hc.shape[1]
            f = _moe_ffn(jnp.concatenate([hc.reshape(n_c, D_MODEL), hl.reshape(-1, D_MODEL)], axis=0),
                         router_w, router_bias, exp_w_gate[layer], exp_w_up[layer], exp_w_down[layer])
            xc = _layer_norm(DEEPNORM_ALPHA * xc + cg2 * f[:n_c].reshape(xc.shape), ln_g[layer, 1], ln_b[layer, 1])
            f_l = f[n_c:]
        else:
            f_l = _moe_ffn(hl.reshape(-1, D_MODEL), router_w, router_bias,
                           exp_w_gate[layer], exp_w_up[layer], exp_w_down[layer])
        xl = _layer_norm(DEEPNORM_ALPHA * xl + g2[:, None] * f_l.reshape(xl.shape), ln_g[layer, 1], ln_b[layer, 1])
    return xl
```

```python
import functools
import math

import jax
import jax.numpy as jnp
from jax import lax
from jax.experimental import pallas as pl
from jax.experimental.pallas import tpu as pltpu

F32 = jnp.float32
BF16 = jnp.bfloat16

D_MODEL = 1024
DEPTH = 4
GRID_W = 64
A_HEADS = 8
A_HEAD_DIM = 64
A_INNER = A_HEADS * A_HEAD_DIM
A_GROUPS = 2
A_STATE = 64
A_XBC = A_INNER + 2 * A_GROUPS * A_STATE
B_HEADS = 4
B_QK_DIM = 64
B_V_DIM = 128
B_QK = B_HEADS * B_QK_DIM
B_INNER = B_HEADS * B_V_DIM
MLSTM_EPS = 1e-6
C_HEADS = 4
C_HEAD_DIM = 128
C_INNER = C_HEADS * C_HEAD_DIM
D_HEADS = 8
D_HEAD_DIM = 64
D_INNER = D_HEADS * D_HEAD_DIM
D_W_LORA = 64
D_A_LORA = 64
D_G_LORA = 128
RWKV_EPS = 64e-5
P_A = A_INNER + A_XBC + 2 * A_HEADS
P_B = 2 * B_QK + 2 * B_INNER + 4 * B_HEADS
P_C = 5 * C_INNER
P_D = 3 * D_INNER + 2 * D_W_LORA + D_A_LORA + D_G_LORA
N_EXPERTS = 32
N_EXPERT_GROUPS = 8
EXPERTS_PER_GROUP = N_EXPERTS // N_EXPERT_GROUPS
TOP_K = 2
D_EXPERT = 512
MOE_BLOCK = 128
RANK_BLOCK = 512
DEEPNORM_ALPHA = (2 * DEPTH) ** 0.25
LN_EPS = 1e-5
M_INIT = -1e30
NEG_BIG = -1e30

SSD_CHUNK = 128
MLSTM_CHUNK = 64
GLA_CHUNK = 16
GLA_BLOCK = 64
RWKV_CHUNK = 64
SCAN_TIME_BLOCK = 256

VMEM_LIMIT_BYTES = 48 * 1024 * 1024
HI = lax.Precision.HIGHEST


def _dot(a, b, dims, exact):
    if exact:
        return lax.dot_general(a.astype(F32), b.astype(F32), (dims, ((), ())),
                               precision=HI, preferred_element_type=F32)
    return lax.dot_general(a.astype(BF16), b.astype(BF16), (dims, ((), ())),
                           preferred_element_type=F32)


def _nn(a, b, exact=False):
    return _dot(a, b, ((1,), (0,)), exact)


def _nt(a, b, exact=False):
    return _dot(a, b, ((1,), (1,)), exact)


def _tn(a, b, exact=False):
    return _dot(a, b, ((0,), (0,)), exact)


def _iota2(n, m):
    return (lax.broadcasted_iota(jnp.int32, (n, m), 0),
            lax.broadcasted_iota(jnp.int32, (n, m), 1))


def _split3(x):
    x1 = x.astype(BF16)
    r1 = x - x1.astype(F32)
    x2 = r1.astype(BF16)
    x3 = (r1 - x2.astype(F32)).astype(BF16)
    return x1, x2, x3


def _mask_nn(mask, x):
    mb = mask.astype(BF16)
    x1, x2, x3 = _split3(x)
    return _nn(mb, x1, False) + _nn(mb, x2, False) + _nn(mb, x3, False)


def _nn_mask(x, mask):
    mb = mask.astype(BF16)
    x1, x2, x3 = _split3(x)
    return _nn(x1, mb, False) + _nn(x2, mb, False) + _nn(x3, mb, False)


def _cum_mats(col, row, L):
    ri, ci = _iota2(L, L)
    ccol = _mask_nn(ci <= ri, jnp.broadcast_to(col, (L, L)))
    crow = _nn_mask(jnp.broadcast_to(row, (L, L)), ri <= ci)
    return ccol, crow


def _mm_kernel(x_ref, w_ref, o_ref, *, exact):
    o_ref[...] = _nn(x_ref[...], w_ref[...], exact)


def _matmul(x, w, tm=512, tn=512, exact=False):
    m, k = x.shape
    n = w.shape[1]
    n_pad = -(-n // tn) * tn
    m_pad = -(-m // tm) * tm
    xb = x if exact else x.astype(BF16)
    wb = w if exact else w.astype(BF16)
    if n_pad != n:
        wb = jnp.pad(wb, ((0, 0), (0, n_pad - n)))
    if m_pad != m:
        xb = jnp.pad(xb, ((0, m_pad - m), (0, 0)))
    out = pl.pallas_call(
        functools.partial(_mm_kernel, exact=exact),
        grid=(n_pad // tn, m_pad // tm),
        in_specs=[pl.BlockSpec((tm, k), lambda j, i: (i, 0)),
                  pl.BlockSpec((k, tn), lambda j, i: (0, j))],
        out_specs=pl.BlockSpec((tm, tn), lambda j, i: (i, j)),
        out_shape=jax.ShapeDtypeStruct((m_pad, n_pad), F32),
        compiler_params=pltpu.CompilerParams(
            dimension_semantics=("arbitrary", "arbitrary"),
            vmem_limit_bytes=VMEM_LIMIT_BYTES),
        name="dense_matmul",
    )(xb, wb)
    return out[:m, :n]


def _moe_kernel(blk_e_ref, n_used_ref, x_ref, wg_ref, wu_ref, wd_ref, o_ref):
    i = pl.program_id(0)

    @pl.when(i < n_used_ref[0])
    def _():
        x = x_ref[...]
        g = jnp.dot(x, wg_ref[...], preferred_element_type=F32)
        u = jnp.dot(x, wu_ref[...], preferred_element_type=F32)
        hmid = (g * jax.nn.sigmoid(g) * u).astype(BF16)
        o_ref[...] = jnp.dot(hmid, wd_ref[...], preferred_element_type=F32)

    @pl.when(i >= n_used_ref[0])
    def _():
        o_ref[...] = jnp.zeros_like(o_ref)


def _moe_experts(xp, blk_e, n_used, w_gate, w_up, w_down):
    n_rows, d = xp.shape
    n_blocks = n_rows // MOE_BLOCK
    grid_spec = pltpu.PrefetchScalarGridSpec(
        num_scalar_prefetch=2,
        grid=(n_blocks,),
        in_specs=[
            pl.BlockSpec((MOE_BLOCK, d), lambda i, be, nu: (i, 0)),
            pl.BlockSpec((None, d, D_EXPERT), lambda i, be, nu: (be[i], 0, 0)),
            pl.BlockSpec((None, d, D_EXPERT), lambda i, be, nu: (be[i], 0, 0)),
            pl.BlockSpec((None, D_EXPERT, d), lambda i, be, nu: (be[i], 0, 0)),
        ],
        out_specs=pl.BlockSpec((MOE_BLOCK, d), lambda i, be, nu: (i, 0)),
    )
    return pl.pallas_call(
        _moe_kernel,
        grid_spec=grid_spec,
        out_shape=jax.ShapeDtypeStruct((n_rows, d), F32),
        compiler_params=pltpu.CompilerParams(
            dimension_semantics=("arbitrary",),
            vmem_limit_bytes=VMEM_LIMIT_BYTES),
        name="moe_experts",
    )(blk_e, n_used, xp, w_gate, w_up, w_down)


def _top2(vals):
    m = len(vals)
    m1 = functools.reduce(jnp.maximum, vals)
    i1 = jnp.full_like(m1, float(m - 1))
    for j in reversed(range(m - 1)):
        i1 = jnp.where(vals[j] == m1, float(j), i1)
    rest = [jnp.where(i1 == float(j), -jnp.inf, vals[j]) for j in range(m)]
    m2 = functools.reduce(jnp.maximum, rest)
    i2 = jnp.full_like(m1, float(m - 1))
    for j in reversed(range(m - 1)):
        i2 = jnp.where(rest[j] == m2, float(j), i2)
    return m1, i1, m2, i2


def _router_kernel(h_ref, wt_ref, bias_ref, e_ref, w_ref, rank_ref, cnt_ref, carry_ref):
    tm = h_ref.shape[0]
    ng, per = N_EXPERT_GROUPS, EXPERTS_PER_GROUP

    @pl.when(pl.program_id(0) == 0)
    def _():
        carry_ref[...] = jnp.zeros_like(carry_ref)

    s = jax.nn.sigmoid(_nt(wt_ref[...], h_ref[...], True))
    sb = s + bias_ref[...]
    biased = [sb[j * ng:(j + 1) * ng, :] for j in range(per)]
    plain = [s[j * ng:(j + 1) * ng, :] for j in range(per)]
    m1, _, m2, _ = _top2(biased)
    gsum = m1 + m2
    rows = lax.broadcasted_iota(jnp.int32, (ng, tm), 0).astype(F32)
    gmax = jnp.max(gsum, axis=0, keepdims=True)
    gi = jnp.min(jnp.where(gsum == gmax, rows, float(ng)), axis=0, keepdims=True)
    sel = rows == gi
    pick = lambda v: jnp.sum(jnp.where(sel, v, 0.0), axis=0, keepdims=True)
    in_b = [pick(v) for v in biased]
    in_s = [pick(v) for v in plain]
    _, l1, _, l2 = _top2(in_b)
    w1 = functools.reduce(jnp.add, [jnp.where(l1 == float(j), in_s[j], 0.0) for j in range(per)])
    w2 = functools.reduce(jnp.add, [jnp.where(l2 == float(j), in_s[j], 0.0) for j in range(per)])
    e1 = gi * float(per) + l1
    e2 = gi * float(per) + l2
    wsum = w1 + w2
    e_ref[0:1, :] = e1.astype(jnp.int32)
    e_ref[1:2, :] = e2.astype(jnp.int32)
    w_ref[0:1, :] = w1 / wsum
    w_ref[1:2, :] = w2 / wsum
    row = lax.broadcasted_iota(jnp.int32, (N_EXPERTS, tm), 0)
    experts = ((row % ng) * per + row // ng).astype(F32)
    oh1 = jnp.where(experts == e1, 1.0, 0.0)
    oh2 = jnp.where(experts == e2, 1.0, 0.0)
    oh = oh1 + oh2
    ri, ci = _iota2(tm, tm)
    seen = _nn(oh, jnp.where(ri < ci, 1.0, 0.0)) + carry_ref[...]
    rank_ref[0:1, :] = jnp.sum(oh1 * seen, axis=0, keepdims=True).astype(jnp.int32)
    rank_ref[1:2, :] = jnp.sum(oh2 * seen, axis=0, keepdims=True).astype(jnp.int32)
    carry = carry_ref[...] + jnp.sum(oh, axis=1, keepdims=True)
    carry_ref[...] = carry
    cnt_ref[...] = carry.astype(jnp.int32)


def _route(h, router_w, router_bias):
    t, d = h.shape
    tm = RANK_BLOCK
    kt = lambda dt: jax.ShapeDtypeStruct((TOP_K, t), dt)
    blk = pl.BlockSpec((TOP_K, tm), lambda i: (0, i))
    member_major = lambda a: a.reshape(N_EXPERT_GROUPS, EXPERTS_PER_GROUP, -1).transpose(1, 0, 2).reshape(N_EXPERTS, -1)
    e, w, rank, counts = pl.pallas_call(
        _router_kernel,
        grid=(t // tm,),
        in_specs=[pl.BlockSpec((tm, d), lambda i: (i, 0)),
                  pl.BlockSpec((N_EXPERTS, d), lambda i: (0, 0)),
                  pl.BlockSpec((N_EXPERTS, 1), lambda i: (0, 0))],
        out_specs=[blk, blk, blk, pl.BlockSpec((N_EXPERTS, 1), lambda i: (0, 0))],
        out_shape=[kt(jnp.int32), kt(F32), kt(jnp.int32), jax.ShapeDtypeStruct((N_EXPERTS, 1), jnp.int32)],
        scratch_shapes=[pltpu.VMEM((N_EXPERTS, 1), F32)],
        compiler_params=pltpu.CompilerParams(dimension_semantics=("arbitrary",),
                                             vmem_limit_bytes=VMEM_LIMIT_BYTES),
        name="moe_router",
    )(h, member_major(router_w.T), member_major(router_bias.astype(F32).reshape(N_EXPERTS, 1)))
    counts = counts.reshape(EXPERTS_PER_GROUP, N_EXPERT_GROUPS).T.reshape(N_EXPERTS)
    return e, w, rank, counts


def _moe_ffn(h, router_w, router_bias, w_gate, w_up, w_down):
    t, d = h.shape
    expert, wts, rank, counts = _route(h, router_w, router_bias)
    n_assign = t * TOP_K
    padded = (counts + MOE_BLOCK - 1) // MOE_BLOCK * MOE_BLOCK
    pends = jnp.cumsum(padded)
    pstarts = pends - padded
    dest = pstarts[expert] + rank
    n_blocks = -(-n_assign // MOE_BLOCK) + N_EXPERTS
    hb = h.astype(BF16)
    xp = jnp.zeros((n_blocks * MOE_BLOCK, d), BF16)
    for kk in range(TOP_K):
        xp = xp.at[dest[kk]].set(hb, unique_indices=True)
    blk_start = jnp.arange(n_blocks, dtype=jnp.int32) * MOE_BLOCK
    blk_e = jnp.minimum(jnp.sum(pends[None, :] <= blk_start[:, None], axis=1), N_EXPERTS - 1).astype(jnp.int32)
    n_used = (pends[-1] // MOE_BLOCK).astype(jnp.int32).reshape(1)
    yp = _moe_experts(xp, blk_e, n_used, w_gate.astype(BF16), w_up.astype(BF16), w_down.astype(BF16))
    out = yp[dest[0]] * wts[0][:, None]
    for kk in range(1, TOP_K):
        out = out + yp[dest[kk]] * wts[kk][:, None]
    return out


def _time_block(t, chunk, target):
    n = t // chunk
    best = 1
    for c in range(1, n + 1):
        if n % c == 0 and c * chunk <= target:
            best = c
    return best * chunk


def _ssd_kernel(x_ref, lac_ref, lar_ref, b_ref, c_ref, y_ref, st_ref, *, L, nck):
    P = x_ref.shape[-1]

    @pl.when(pl.program_id(1) == 0)
    def _():
        st_ref[...] = jnp.zeros_like(st_ref)

    ri, ci = _iota2(L, L)
    causal = ci <= ri

    def body(i, carry):
        sl = pl.ds(pl.multiple_of(i * L, L), L)
        x = x_ref[sl, :]
        bm = b_ref[sl, :]
        cm = c_ref[sl, :]
        ccol, crow = _cum_mats(lac_ref[sl, :], lar_ref[i], L)
        decay = jnp.exp(jnp.where(causal, ccol - crow, NEG_BIG))
        att = _nt(cm, bm) * decay
        st = st_ref[...]
        cum_p = ccol[:, :P]
        end_p = ccol[L - 1:L, :P]
        y = _nn(att, x) + jnp.exp(cum_p) * _nn(cm, st)
        y_ref[sl, :] = y
        st_ref[...] = jnp.exp(end_p) * st + _tn(bm, x * jnp.exp(end_p - cum_p))
        return carry

    lax.fori_loop(0, nck, body, 0)


def _ssd_scan(x, la, bm, cm):
    g, t, p = x.shape
    n = bm.shape[-1]
    hpg = g // bm.shape[0]
    L = SSD_CHUNK
    tb = _time_block(t, L, 1024)
    nck = tb // L
    lac = la[..., None]
    lar = la.reshape(g, t // L, 1, L)
    return pl.pallas_call(
        functools.partial(_ssd_kernel, L=L, nck=nck),
        grid=(g, t // tb),
        in_specs=[
            pl.BlockSpec((None, tb, p), lambda i, j: (i, j, 0)),
            pl.BlockSpec((None, tb, 1), lambda i, j: (i, j, 0)),
            pl.BlockSpec((None, nck, 1, L), lambda i, j: (i, j, 0, 0)),
            pl.BlockSpec((None, tb, n), lambda i, j: (i // hpg, j, 0)),
            pl.BlockSpec((None, tb, n), lambda i, j: (i // hpg, j, 0)),
        ],
        out_specs=pl.BlockSpec((None, tb, p), lambda i, j: (i, j, 0)),
        out_shape=jax.ShapeDtypeStruct((g, t, p), F32),
        scratch_shapes=[pltpu.VMEM((n, p), F32)],
        compiler_params=pltpu.CompilerParams(
            dimension_semantics=("arbitrary", "arbitrary"),
            vmem_limit_bytes=VMEM_LIMIT_BYTES),
        name="ssd_scan",
    )(x, lac, lar, bm, cm)


def _mlstm_kernel(q_ref, k_ref, v_ref, lic_ref, lfc_ref, lir_ref, lfr_ref, h_ref,
                  c_ref, n_ref, m_ref, *, L, nck):
    dk = q_ref.shape[-1]

    @pl.when(pl.program_id(1) == 0)
    def _():
        c_ref[...] = jnp.zeros_like(c_ref)
        n_ref[...] = jnp.zeros_like(n_ref)
        m_ref[...] = jnp.full_like(m_ref, M_INIT)

    ri, ci = _iota2(L, L)
    causal = ci <= ri

    def body(i, carry):
        sl = pl.ds(pl.multiple_of(i * L, L), L)
        q = q_ref[sl, :] * (dk ** -0.5)
        k = k_ref[sl, :]
        v = v_ref[sl, :]
        li_c = lic_ref[sl, :]
        li_r = lir_ref[i]
        fcol, frow = _cum_mats(lfc_ref[sl, :], lfr_ref[i], L)
        f_c = fcol[:, :1]
        ftot = fcol[L - 1:L, :1]
        c_prev = c_ref[...]
        n_prev = n_ref[...]
        m_prev = m_ref[...]
        w_end = ftot - f_c + li_c
        m_loc = jnp.max(w_end, axis=0, keepdims=True)
        ke = k * jnp.exp(w_end - m_loc)
        c_loc = _tn(ke, v)
        n_loc = jnp.sum(ke, axis=0, keepdims=True)
        log_d = jnp.where(causal, fcol - frow + li_r, NEG_BIG)
        log_inter = f_c + m_prev
        m_row = jnp.maximum(jnp.max(log_d, axis=-1, keepdims=True), log_inter)
        s = _nt(q, k) * jnp.exp(log_d - m_row)
        inter = jnp.exp(log_inter - m_row)
        num = _nn(s, v) + inter * _nn(q, c_prev)
        den = jnp.sum(s, axis=-1, keepdims=True) + inter * jnp.sum(q * n_prev, axis=-1, keepdims=True)
        h_ref[sl, :] = num / jnp.maximum(jnp.abs(den), jnp.exp(-m_row))
        m_new = jnp.maximum(ftot + m_prev, m_loc)
        sp = jnp.exp(ftot + m_prev - m_new)
        sc = jnp.exp(m_loc - m_new)
        c_ref[...] = sp * c_prev + sc * c_loc
        n_ref[...] = sp * n_prev + sc * n_loc
        m_ref[...] = m_new
        return carry

    lax.fori_loop(0, nck, body, 0)


def _mlstm_scan(q, k, v, li, lf):
    g, t, dk = q.shape
    dv = v.shape[-1]
    L = MLSTM_CHUNK
    tb = _time_block(t, L, 1024)
    nck = tb // L
    col = lambda a: a[..., None]
    row = lambda a: a.reshape(g, t // L, 1, L)
    seq = lambda d: pl.BlockSpec((None, tb, d), lambda i, j: (i, j, 0))
    rows = pl.BlockSpec((None, nck, 1, L), lambda i, j: (i, j, 0, 0))
    return pl.pallas_call(
        functools.partial(_mlstm_kernel, L=L, nck=nck),
        grid=(g, t // tb),
        in_specs=[seq(dk), seq(dk), seq(dv), seq(1), seq(1), rows, rows],
        out_specs=seq(dv),
        out_shape=jax.ShapeDtypeStruct((g, t, dv), F32),
        scratch_shapes=[pltpu.VMEM((dk, dv), F32), pltpu.VMEM((1, dk), F32), pltpu.VMEM((1, 1), F32)],
        compiler_params=pltpu.CompilerParams(
            dimension_semantics=("arbitrary", "arbitrary"),
            vmem_limit_bytes=VMEM_LIMIT_BYTES),
        name="mlstm_scan",
    )(q, k, v, col(li), col(lf), row(li), row(lf))


def _time_index(d, j, n_ctx_blocks, n_blocks):
    rev = jnp.where(j < n_ctx_blocks, n_ctx_blocks - 1 - j, n_blocks - 1 - j + n_ctx_blocks)
    return jnp.where(d == 1, rev, j)


def _end_row(x, d):
    n = x.shape[0]
    return jnp.where(d == 1, x[0:1, :], x[n - 1:n, :])


def _gla_kernel(q_ref, v_ref, k_ref, lf_ref, y_ref, st_ref, *, L, sub, nck):
    d = pl.program_id(0)
    sgn = 1 - 2 * d

    @pl.when(pl.program_id(3) == 0)
    def _():
        st_ref[...] = jnp.zeros_like(st_ref)

    ri, ci = _iota2(L, L)
    before = (ci - ri) * sgn <= 0
    rows = lax.broadcasted_iota(jnp.int32, (L, 1), 0)

    def body(i, carry):
        ii = i + d * (nck - 1 - 2 * i)
        sl = pl.ds(pl.multiple_of(ii * L, L), L)
        q = q_ref[sl, :]
        k = k_ref[sl, :]
        v = v_ref[sl, :]
        lam = _mask_nn(before, lf_ref[sl, :])
        lam_end = _end_row(lam, d)
        st = st_ref[...]
        blocks = []
        for c in range(L // sub):
            lo, hi = c * sub, (c + 1) * sub
            zero = jnp.zeros_like(lam_end)
            ref_f = lam[lo - 1:lo, :] if lo > 0 else zero
            ref_b = lam[hi:hi + 1, :] if hi < L else zero
            ref = jnp.where(d == 1, ref_b, ref_f)
            upto = jnp.where(d == 1, lo - 1 - rows, rows - hi) < 0
            qc = q[lo:hi, :] * jnp.exp(lam[lo:hi, :] - ref)
            kc = k * jnp.exp(jnp.where(upto, ref - lam, NEG_BIG))
            blocks.append(_nt(qc, kc))
        att = jnp.where(before, jnp.concatenate(blocks, axis=0), 0.0)
        y_ref[sl, :] = _nn(att, v) + _nt(q * jnp.exp(lam), st)
        st_ref[...] = st * jnp.exp(lam_end) + _tn(v, k * jnp.exp(lam_end - lam))
        return carry

    lax.fori_loop(0, nck, body, 0)


def _gla_scan(q, v, k, lf, n_ctx):
    b, t, c = q.shape
    hd = C_HEAD_DIM
    L = GLA_BLOCK
    tb = SCAN_TIME_BLOCK
    nck, nb, ncb = tb // L, t // tb, n_ctx // tb
    shared = pl.BlockSpec((None, tb, hd), lambda d, i, h, j: (i, _time_index(d, j, ncb, nb), h))
    per_dir = pl.BlockSpec((None, None, tb, hd), lambda d, i, h, j: (d, i, _time_index(d, j, ncb, nb), h))
    return pl.pallas_call(
        functools.partial(_gla_kernel, L=L, sub=GLA_CHUNK, nck=nck),
        grid=(2, b, c // hd, nb),
        in_specs=[shared, shared, per_dir, per_dir],
        out_specs=per_dir,
        out_shape=jax.ShapeDtypeStruct((2, b, t, c), F32),
        scratch_shapes=[pltpu.VMEM((hd, hd), F32)],
        compiler_params=pltpu.CompilerParams(
            dimension_semantics=("arbitrary",) * 4,
            vmem_limit_bytes=VMEM_LIMIT_BYTES),
        name="gla_scan",
    )(q, v, k, lf)


def _rwkv_kernel(r_ref, k_ref, v_ref, a_ref, b_ref, lw_ref, y_ref, h_ref, *, L, nck):
    d = pl.program_id(0)
    sgn = 1 - 2 * d
    L2 = 2 * L
    W = 2 * D_HEAD_DIM

    @pl.when(pl.program_id(3) == 0)
    def _():
        h_ref[...] = jnp.zeros_like(h_ref)

    ri, ci = _iota2(L, L)
    before = (ci - ri) * sgn <= 0
    r2, c2 = _iota2(L2, L2)
    order2 = ((c2 & (L - 1)) - (r2 & (L - 1))) * sgn
    strict2 = order2 < 0
    incl2 = order2 <= 0
    eye2 = jnp.where(r2 == c2, 1.0, 0.0)
    rw, cw = _iota2(W, W)
    eye_w = rw == cw
    head0 = lax.broadcasted_iota(jnp.int32, (L, W), 1) < D_HEAD_DIM
    stack = lambda x: jnp.concatenate([jnp.where(head0, x, 0.0), jnp.where(head0, 0.0, x)], axis=0)
    n_levels = int(math.log2(L))

    def body(i, carry):
        ii = i + d * (nck - 1 - 2 * i)
        sl = pl.ds(pl.multiple_of(ii * L, L), L)
        r, k, v, a, b, lw = (ref[sl, :] for ref in (r_ref, k_ref, v_ref, a_ref, b_ref, lw_ref))
        cum = _mask_nn(before, lw)
        cum_end = _end_row(cum, d)
        e_neg = jnp.exp(-cum)
        e_end = jnp.exp(cum_end - cum)
        at, rt = stack(a * jnp.exp(cum - lw)), stack(r * jnp.exp(cum))
        bt, kt = stack(b * e_neg), stack(k * e_neg)
        vs = stack(v)
        gram = _nt(jnp.concatenate([at, rt], axis=0), jnp.concatenate([bt, kt], axis=0))
        nmat = jnp.where(strict2, gram[:L2, :L2], 0.0)
        a_k = jnp.where(strict2, gram[:L2, L2:], 0.0)
        r_bk = jnp.where(jnp.concatenate([incl2, incl2], axis=1), gram[L2:, :], 0.0)
        tinv = eye2 + nmat
        pw = _nn(nmat, nmat)
        for lev in range(1, n_levels):
            if lev < n_levels - 1:
                both = _nn(pw, jnp.concatenate([pw, tinv], axis=1))
                pw, tinv = both[:, :L2], tinv + both[:, L2:]
            else:
                tinv = tinv + _nn(pw, tinv)
        wu = _nn(tinv, jnp.concatenate([at, _nn(a_k, vs)], axis=1))
        zs = jnp.concatenate([wu, jnp.concatenate([jnp.zeros_like(vs), vs], axis=1)], axis=0)
        qy = _nn(r_bk, zs)
        md = _tn(jnp.concatenate([stack(b * e_end), stack(k * e_end)], axis=0), zs)
        h = h_ref[...]
        ys = _nn(rt + qy[:, :W], h) + qy[:, W:]
        y_ref[sl, :] = ys[:L, :] + ys[L:, :]
        dec = jnp.sum(jnp.where(eye_w, jnp.broadcast_to(jnp.exp(cum_end), (W, W)), 0.0), axis=1, keepdims=True)
        h_ref[...] = dec * h + _nn(md[:, :W], h) + md[:, W:]
        return carry

    lax.fori_loop(0, nck, body, 0)


def _rwkv_scan(r, k, v, a, b, lw, n_ctx):
    bsz, t, c = r.shape
    w = 2 * D_HEAD_DIM
    L = RWKV_CHUNK
    tb = SCAN_TIME_BLOCK
    nck, nb, ncb = tb // L, t // tb, n_ctx // tb
    shared = pl.BlockSpec((None, tb, w), lambda d, i, p, j: (i, _time_index(d, j, ncb, nb), p))
    per_dir = pl.BlockSpec((None, None, tb, w), lambda d, i, p, j: (d, i, _time_index(d, j, ncb, nb), p))
    return pl.pallas_call(
        functools.partial(_rwkv_kernel, L=L, nck=nck),
        grid=(2, bsz, c // w, nb),
        in_specs=[shared] * 5 + [per_dir],
        out_specs=per_dir,
        out_shape=jax.ShapeDtypeStruct((2, bsz, t, c), F32),
        scratch_shapes=[pltpu.VMEM((w, w), F32)],
        compiler_params=pltpu.CompilerParams(
            dimension_semantics=("arbitrary",) * 4,
            vmem_limit_bytes=VMEM_LIMIT_BYTES),
        name="rwkv7_scan",
    )(r, k, v, a, b, lw)


def _bidir_seq(ac, al, bc, bl):
    fwd = jnp.concatenate([ac, al], axis=1)
    bwd = jnp.concatenate([jnp.flip(bc, axis=1), jnp.flip(bl, axis=1)], axis=1)
    return jnp.stack([fwd, bwd])


def _heads(u, h):
    two, b, t, c = u.shape
    return u.reshape(two, b, t, h, c // h).transpose(0, 1, 3, 2, 4).reshape(two * b * h, t, c // h)


def _head_scalars(u):
    two, b, t, h = u.shape
    return u.transpose(0, 1, 3, 2).reshape(two * b * h, t)


def _unheads(y, b, h, n_ctx):
    g, t, d = y.shape
    y = y.reshape(2, b, h, t, d).transpose(0, 1, 3, 2, 4)
    yc = y[0, :, :n_ctx] + jnp.flip(y[1, :, :n_ctx], axis=1)
    yl = y[0, :, n_ctx:] + jnp.flip(y[1, :, n_ctx:], axis=1)
    return yc, yl


def _dwconv3(x, w, b):
    prev = jnp.pad(x[:, :-1], ((0, 0), (1, 0), (0, 0)))
    nxt = jnp.pad(x[:, 1:], ((0, 0), (0, 1), (0, 0)))
    return prev * w[0] + x * w[1] + nxt * w[2] + b


def _layer_norm(x, g, b):
    mu = x.mean(-1, keepdims=True)
    var = jnp.mean(jnp.square(x - mu), -1, keepdims=True)
    return (x - mu) * lax.rsqrt(var + LN_EPS) * g + b


def _head_norm(y, eps, center):
    if center:
        y = y - y.mean(-1, keepdims=True)
    y = y * lax.rsqrt(jnp.mean(y * y, -1, keepdims=True) + eps)
    return y.reshape(y.shape[0], y.shape[1], -1)


def _ssd_mixer(pc, pl_, params, ctx_out):
    conv_w, conv_b, dt_bias, a_log, d_skip, norm_w = params
    bsz, n_ctx = pc.shape[0], pc.shape[1]
    a = -jnp.exp(a_log)

    def prep(p):
        b, t, _ = p.shape
        z = p[..., :A_INNER]
        xbc = jax.nn.silu(_dwconv3(p[..., A_INNER:A_INNER + A_XBC], conv_w, conv_b))
        xs = xbc[..., :A_INNER].reshape(b, t, A_HEADS, A_HEAD_DIM)
        bm = xbc[..., A_INNER:A_INNER + A_GROUPS * A_STATE]
        cm = xbc[..., A_INNER + A_GROUPS * A_STATE:]
        dt = jax.nn.softplus(p[..., A_INNER + A_XBC:].reshape(b, t, 2, A_HEADS) + dt_bias)
        return z, xs, bm, cm, dt

    zc, xsc, bmc, cmc, dtc = prep(pc)
    zl, xsl, bml, cml, dtl = prep(pl_)
    xd = lambda xs, dt, d: (xs * dt[:, :, d, :, None]).reshape(xs.shape[0], xs.shape[1], A_INNER)
    x = _heads(_bidir_seq(xd(xsc, dtc, 0), xd(xsl, dtl, 0), xd(xsc, dtc, 1), xd(xsl, dtl, 1)), A_HEADS)
    la = _head_scalars(_bidir_seq(dtc[:, :, 0] * a[0], dtl[:, :, 0] * a[0], dtc[:, :, 1] * a[1], dtl[:, :, 1] * a[1]))
    bm = _heads(_bidir_seq(bmc, bml, bmc, bml), A_GROUPS)
    cm = _heads(_bidir_seq(cmc, cml, cmc, cml), A_GROUPS)
    y = _ssd_scan(x, la, bm, cm)
    yc, yl = _unheads(y, bsz, A_HEADS, n_ctx)

    def out(y, z, xs):
        b, t = y.shape[:2]
        y = (y + d_skip[:, None] * xs).reshape(b, t, A_INNER) * jax.nn.silu(z)
        return y * lax.rsqrt(jnp.mean(y * y, -1, keepdims=True) + 1e-6) * norm_w

    return (out(yc, zc, xsc) if ctx_out else None), out(yl, zl, xsl)


def _mlstm_mixer(pc, pl_, params, ctx_out):
    conv_w, conv_b, i_bias, f_bias, norm_w = params
    bsz, n_ctx = pc.shape[0], pc.shape[1]

    def prep(p):
        b, t, _ = p.shape
        qk = jax.nn.silu(_dwconv3(p[..., :2 * B_QK], conv_w, conv_b))
        q, k = qk[..., :B_QK], qk[..., B_QK:]
        v = p[..., 2 * B_QK:2 * B_QK + B_INNER]
        o = p[..., 2 * B_QK + B_INNER:2 * B_QK + 2 * B_INNER]
        gates = p[..., 2 * B_QK + 2 * B_INNER:].reshape(b, t, 2, 2, B_HEADS)
        log_i = gates[:, :, 0] + i_bias
        log_f = jax.nn.log_sigmoid(gates[:, :, 1] + f_bias)
        return q, k, v, o, log_i, log_f

    qc, kc, vc, oc, lic, lfc = prep(pc)
    ql, kl, vl, ol, lil, lfl = prep(pl_)
    q = _heads(_bidir_seq(qc, ql, qc, ql), B_HEADS)
    k = _heads(_bidir_seq(kc, kl, kc, kl), B_HEADS)
    v = _heads(_bidir_seq(vc, vl, vc, vl), B_HEADS)
    li = _head_scalars(_bidir_seq(lic[:, :, 0], lil[:, :, 0], lic[:, :, 1], lil[:, :, 1]))
    lf = _head_scalars(_bidir_seq(lfc[:, :, 0], lfl[:, :, 0], lfc[:, :, 1], lfl[:, :, 1]))
    h = _mlstm_scan(q, k, v, li, lf)
    hc, hl = _unheads(h, bsz, B_HEADS, n_ctx)
    out = lambda h, o: jax.nn.sigmoid(o) * _head_norm(h, MLSTM_EPS, True) * norm_w
    return (out(hc, oc) if ctx_out else None), out(hl, ol)


def _neighbours(x, n_ctx):
    pos = jnp.arange(x.shape[1])[None, :, None]
    prev = jnp.pad(x[:, :-1], ((0, 0), (1, 0), (0, 0)))
    nxt = jnp.pad(x[:, 1:], ((0, 0), (0, 1), (0, 0)))
    return jnp.where(pos == n_ctx, 0.0, prev), jnp.where(pos == n_ctx - 1, 0.0, nxt)


def _hgrn2_mixer(p, n_ctx, lb, params):
    f_bias, norm_w = params
    b, t, _ = p.shape
    q = jax.nn.silu(p[..., :C_INNER])
    f_pre = jnp.moveaxis(p[..., C_INNER:3 * C_INNER].reshape(b, t, 2, C_INNER), 2, 0) + f_bias[:, None, None, :]
    log_f = jnp.log(lb + (1 - lb) * jax.nn.sigmoid(f_pre))
    k = (1 - lb) * jax.nn.sigmoid(-f_pre)
    o = _gla_scan(q, p[..., 3 * C_INNER:4 * C_INNER], k, log_f, n_ctx)
    o = (o[0] + o[1]).reshape(b, t, C_HEADS, C_HEAD_DIM)
    return _head_norm(o, 1e-6, False) * norm_w * jax.nn.silu(p[..., 4 * C_INNER:])


def _rwkv7_mixer(p, n_ctx, params):
    mu, w0, w2, a0, a2, g2, k_k, k_a, r_k, ln_w, ln_b = params
    b, t, _ = p.shape
    hd = lambda u: u.reshape(b, t, D_HEADS, D_HEAD_DIM)
    prev, nxt = _neighbours(p, n_ctx)
    p = p + mu * (0.5 * (prev + nxt) - p)
    r = p[..., :D_INNER]
    k = p[..., D_INNER:2 * D_INNER]
    v = p[..., 2 * D_INNER:3 * D_INNER]
    o = 3 * D_INNER
    wl = jnp.tanh(p[..., o:o + 2 * D_W_LORA]).reshape(b * t, 2, D_W_LORA)
    al = p[..., o + 2 * D_W_LORA:o + 2 * D_W_LORA + D_A_LORA]
    gl = p[..., o + 2 * D_W_LORA + D_A_LORA:]
    w = jnp.stack([_matmul(wl[:, d], w2[d]).reshape(b, t, D_INNER) + w0[d] for d in range(2)])
    log_decay = -jnp.exp(-jax.nn.softplus(-w) - 0.5)
    a = jax.nn.sigmoid(a0 + _matmul(al.reshape(b * t, D_A_LORA), a2).reshape(b, t, D_INNER))
    g = _matmul(jax.nn.sigmoid(gl).reshape(b * t, D_G_LORA), g2).reshape(b, t, D_INNER)
    kk = hd(k * k_k)
    kk = (kk * lax.rsqrt(jnp.maximum(jnp.sum(kk * kk, -1, keepdims=True), 1e-12))).reshape(b, t, D_INNER)
    k = k * (1 + (a - 1) * k_a)
    y = _rwkv_scan(r, k, v, -kk, kk * a, log_decay, n_ctx)
    y = hd(y[0] + y[1])
    bonus = (jnp.sum(hd(r) * hd(k) * r_k, axis=-1, keepdims=True) * hd(v)).reshape(b, t, D_INNER)
    return (_head_norm(y, RWKV_EPS, True) * ln_w + ln_b + bonus) * g


def _to_col_major(u, rows):
    b, s, d = u.shape
    return u.reshape(b, rows, GRID_W, d).transpose(0, 2, 1, 3).reshape(b, s, d)


def _from_col_major(u, rows):
    b, s, d = u.shape
    return u.reshape(b, GRID_W, rows, d).transpose(0, 2, 1, 3).reshape(b, s, d)


def _proj(u, w):
    b, t, d = u.shape
    return _matmul(u.reshape(b * t, d), w).reshape(b, t, w.shape[1])


def kernel(x, c, ctx, c_ctx, mod_w, mod_b, ln_g, ln_b, ev_w_in, ev_w_out, ssd_conv_w, ssd_conv_b, ssd_dt_bias, ssd_a_log, ssd_d, ssd_norm_w, mlstm_conv_w, mlstm_conv_b, mlstm_i_bias, mlstm_f_bias, mlstm_norm_w, od_w_in, od_w_out, hgrn_lb_logits, hgrn_f_bias, hgrn_norm_w, rwkv_mu, rwkv_w0, rwkv_w2, rwkv_a0, rwkv_a2, rwkv_g2, rwkv_k_k, rwkv_k_a, rwkv_r_k, rwkv_ln_w, rwkv_ln_b, router_w, router_bias, exp_w_gate, exp_w_up, exp_w_down):
    bsz, seq, _ = x.shape
    n_ctx = ctx.shape[1]
    rows = seq // GRID_W
    lb_all = jnp.cumsum(jax.nn.softmax(hgrn_lb_logits.astype(F32), axis=0), axis=0)
    lb_all = lb_all - lb_all[0]
    s_c = jax.nn.silu(c)
    s_cc = jax.nn.silu(c_ctx)
    xa = jnp.concatenate([ctx, x], axis=1)
    is_ctx = (jnp.arange(n_ctx + seq) < n_ctx)[None, :, None]
    for layer in range(DEPTH):
        i = layer // 2
        mods = _matmul(jnp.concatenate([s_c, s_cc[None]], axis=0), mod_w[layer], tm=8, tn=512) + mod_b[layer]
        sh1, sc1, g1, sh2, sc2, g2 = [jnp.where(is_ctx, mc[None, None, :], ml[:, None, :]) for ml, mc in
                                      zip(jnp.split(mods[:bsz], 6, axis=-1), jnp.split(mods[bsz], 6, axis=-1))]
        u = xa * (1 + sc1) + sh1
        if layer % 2 == 0:
            p = _proj(u, ev_w_in[i])
            pc, pl_ = p[:, :n_ctx], p[:, n_ctx:]
            fca, fla = _ssd_mixer(pc[..., :P_A], pl_[..., :P_A],
                                  (ssd_conv_w[i], ssd_conv_b[i], ssd_dt_bias[i], ssd_a_log[i], ssd_d[i], ssd_norm_w[i]),
                                  True)
            fcb, flb = _mlstm_mixer(pc[..., P_A:], pl_[..., P_A:],
                                    (mlstm_conv_w[i], mlstm_conv_b[i], mlstm_i_bias[i], mlstm_f_bias[i], mlstm_norm_w[i]),
                                    True)
            feat = jnp.concatenate([jnp.concatenate([fca, fcb], axis=-1), jnp.concatenate([fla, flb], axis=-1)], axis=1)
            w_out = ev_w_out[i]
        else:
            u = jnp.concatenate([u[:, :n_ctx], _to_col_major(u[:, n_ctx:], rows)], axis=1)
            p = _proj(u, od_w_in[i])
            feat = jnp.concatenate([
                _hgrn2_mixer(p[..., :P_C], n_ctx, lb_all[layer], (hgrn_f_bias[i], hgrn_norm_w[i])),
                _rwkv7_mixer(p[..., P_C:], n_ctx,
                             (rwkv_mu[i], rwkv_w0[i], rwkv_w2[i], rwkv_a0[i], rwkv_a2[i], rwkv_g2[i],
                              rwkv_k_k[i], rwkv_k_a[i], rwkv_r_k[i], rwkv_ln_w[i], rwkv_ln_b[i]))], axis=-1)
            feat = jnp.concatenate([feat[:, :n_ctx], _from_col_major(feat[:, n_ctx:], rows)], axis=1)
            w_out = od_w_out[i]
        xa = _layer_norm(DEEPNORM_ALPHA * xa + g1 * _proj(feat, w_out), ln_g[layer, 0], ln_b[layer, 0])
        h = xa * (1 + sc2) + sh2
        f = _moe_ffn(h.reshape(-1, D_MODEL), router_w, router_bias,
                     exp_w_gate[layer], exp_w_up[layer], exp_w_down[layer])
        xa = _layer_norm(DEEPNORM_ALPHA * xa + g2 * f.reshape(xa.shape), ln_g[layer, 1], ln_b[layer, 1])
    return xa[:, n_ctx:]
```

```python
import functools
import math

import jax
import jax.numpy as jnp
from jax import lax
from jax.experimental import pallas as pl
from jax.experimental.pallas import tpu as pltpu

F32 = jnp.float32
BF16 = jnp.bfloat16

D_MODEL = 1024
DEPTH = 4
GRID_W = 64
A_HEADS = 8
A_HEAD_DIM = 64
A_INNER = A_HEADS * A_HEAD_DIM
A_GROUPS = 2
A_STATE = 64
A_XBC = A_INNER + 2 * A_GROUPS * A_STATE
B_HEADS = 4
B_QK_DIM = 64
B_V_DIM = 128
B_QK = B_HEADS * B_QK_DIM
B_INNER = B_HEADS * B_V_DIM
MLSTM_EPS = 1e-6
C_HEADS = 4
C_HEAD_DIM = 128
C_INNER = C_HEADS * C_HEAD_DIM
D_HEADS = 8
D_HEAD_DIM = 64
D_INNER = D_HEADS * D_HEAD_DIM
D_W_LORA = 64
D_A_LORA = 64
D_G_LORA = 128
RWKV_EPS = 64e-5
P_A = A_INNER + A_XBC + 2 * A_HEADS
P_B = 2 * B_QK + 2 * B_INNER + 4 * B_HEADS
P_C = 5 * C_INNER
P_D = 3 * D_INNER + 2 * D_W_LORA + D_A_LORA + D_G_LORA
N_EXPERTS = 32
N_EXPERT_GROUPS = 8
EXPERTS_PER_GROUP = N_EXPERTS // N_EXPERT_GROUPS
TOP_K = 2
D_EXPERT = 512
MOE_BLOCK = 128
RANK_BLOCK = 512
DEEPNORM_ALPHA = (2 * DEPTH) ** 0.25
LN_EPS = 1e-5
M_INIT = -1e30
NEG_BIG = -1e30

SSD_CHUNK = 128
MLSTM_CHUNK = 128
GLA_CHUNK = 16
GLA_BLOCK = 64
RWKV_CHUNK = 64
SCAN_TIME_BLOCK = 256

VMEM_LIMIT_BYTES = 48 * 1024 * 1024
HI = lax.Precision.HIGHEST


def _dot(a, b, dims, exact):
    if exact:
        return lax.dot_general(a.astype(F32), b.astype(F32), (dims, ((), ())),
                               precision=HI, preferred_element_type=F32)
    return lax.dot_general(a.astype(BF16), b.astype(BF16), (dims, ((), ())),
                           preferred_element_type=F32)


def _nn(a, b, exact=False):
    return _dot(a, b, ((1,), (0,)), exact)


def _nt(a, b, exact=False):
    return _dot(a, b, ((1,), (1,)), exact)


def _tn(a, b, exact=False):
    return _dot(a, b, ((0,), (0,)), exact)


def _iota2(n, m):
    return (lax.broadcasted_iota(jnp.int32, (n, m), 0),
            lax.broadcasted_iota(jnp.int32, (n, m), 1))


def _split3(x):
    x1 = x.astype(BF16)
    r1 = x - x1.astype(F32)
    x2 = r1.astype(BF16)
    x3 = (r1 - x2.astype(F32)).astype(BF16)
    return x1, x2, x3


def _mask_nn(mask, x):
    mb = mask.astype(BF16)
    x1, x2, x3 = _split3(x)
    return _nn(mb, x1, False) + _nn(mb, x2, False) + _nn(mb, x3, False)


def _nn_mask(x, mask):
    mb = mask.astype(BF16)
    x1, x2, x3 = _split3(x)
    return _nn(x1, mb, False) + _nn(x2, mb, False) + _nn(x3, mb, False)


def _cum_mats(col, row, L):
    ri, ci = _iota2(L, L)
    ccol = _mask_nn(ci <= ri, jnp.broadcast_to(col, (L, L)))
    crow = _nn_mask(jnp.broadcast_to(row, (L, L)), ri <= ci)
    return ccol, crow


def _mm_kernel(x_ref, w_ref, o_ref, *, exact):
    o_ref[...] = _nn(x_ref[...], w_ref[...], exact)


def _matmul(x, w, tm=512, tn=512, exact=False):
    m, k = x.shape
    n = w.shape[1]
    n_pad = -(-n // tn) * tn
    m_pad = -(-m // tm) * tm
    xb = x if exact else x.astype(BF16)
    wb = w if exact else w.astype(BF16)
    if n_pad != n:
        wb = jnp.pad(wb, ((0, 0), (0, n_pad - n)))
    if m_pad != m:
        xb = jnp.pad(xb, ((0, m_pad - m), (0, 0)))
    out = pl.pallas_call(
        functools.partial(_mm_kernel, exact=exact),
        grid=(n_pad // tn, m_pad // tm),
        in_specs=[pl.BlockSpec((tm, k), lambda j, i: (i, 0)),
                  pl.BlockSpec((k, tn), lambda j, i: (0, j))],
        out_specs=pl.BlockSpec((tm, tn), lambda j, i: (i, j)),
        out_shape=jax.ShapeDtypeStruct((m_pad, n_pad), F32),
        compiler_params=pltpu.CompilerParams(
            dimension_semantics=("arbitrary", "arbitrary"),
            vmem_limit_bytes=VMEM_LIMIT_BYTES),
        name="dense_matmul",
    )(xb, wb)
    return out[:m, :n]


def _moe_kernel(blk_e_ref, n_used_ref, x_ref, wg_ref, wu_ref, wd_ref, o_ref):
    i = pl.program_id(0)

    @pl.when(i < n_used_ref[0])
    def _():
        x = x_ref[...]
        g = jnp.dot(x, wg_ref[...], preferred_element_type=F32)
        u = jnp.dot(x, wu_ref[...], preferred_element_type=F32)
        hmid = (g * jax.nn.sigmoid(g) * u).astype(BF16)
        o_ref[...] = jnp.dot(hmid, wd_ref[...], preferred_element_type=F32)

    @pl.when(i >= n_used_ref[0])
    def _():
        o_ref[...] = jnp.zeros_like(o_ref)


def _moe_experts(xp, blk_e, n_used, w_gate, w_up, w_down):
    n_rows, d = xp.shape
    n_blocks = n_rows // MOE_BLOCK
    grid_spec = pltpu.PrefetchScalarGridSpec(
        num_scalar_prefetch=2,
        grid=(n_blocks,),
        in_specs=[
            pl.BlockSpec((MOE_BLOCK, d), lambda i, be, nu: (i, 0)),
            pl.BlockSpec((None, d, D_EXPERT), lambda i, be, nu: (be[i], 0, 0)),
            pl.BlockSpec((None, d, D_EXPERT), lambda i, be, nu: (be[i], 0, 0)),
            pl.BlockSpec((None, D_EXPERT, d), lambda i, be, nu: (be[i], 0, 0)),
        ],
        out_specs=pl.BlockSpec((MOE_BLOCK, d), lambda i, be, nu: (i, 0)),
    )
    return pl.pallas_call(
        _moe_kernel,
        grid_spec=grid_spec,
        out_shape=jax.ShapeDtypeStruct((n_rows, d), F32),
        compiler_params=pltpu.CompilerParams(
            dimension_semantics=("arbitrary",),
            vmem_limit_bytes=VMEM_LIMIT_BYTES),
        name="moe_experts",
    )(blk_e, n_used, xp, w_gate, w_up, w_down)


def _top2(vals):
    m = len(vals)
    m1 = functools.reduce(jnp.maximum, vals)
    i1 = jnp.full_like(m1, float(m - 1))
    for j in reversed(range(m - 1)):
        i1 = jnp.where(vals[j] == m1, float(j), i1)
    rest = [jnp.where(i1 == float(j), -jnp.inf, vals[j]) for j in range(m)]
    m2 = functools.reduce(jnp.maximum, rest)
    i2 = jnp.full_like(m1, float(m - 1))
    for j in reversed(range(m - 1)):
        i2 = jnp.where(rest[j] == m2, float(j), i2)
    return m1, i1, m2, i2


def _router_kernel(h_ref, wt_ref, bias_ref, e_ref, w_ref, rank_ref, cnt_ref, carry_ref):
    tm = h_ref.shape[0]
    ng, per = N_EXPERT_GROUPS, EXPERTS_PER_GROUP

    @pl.when(pl.program_id(0) == 0)
    def _():
        carry_ref[...] = jnp.zeros_like(carry_ref)

    s = jax.nn.sigmoid(_nt(wt_ref[...], h_ref[...], True))
    sb = s + bias_ref[...]
    biased = [sb[j * ng:(j + 1) * ng, :] for j in range(per)]
    plain = [s[j * ng:(j + 1) * ng, :] for j in range(per)]
    m1, _, m2, _ = _top2(biased)
    gsum = m1 + m2
    rows = lax.broadcasted_iota(jnp.int32, (ng, tm), 0).astype(F32)
    gmax = jnp.max(gsum, axis=0, keepdims=True)
    gi = jnp.min(jnp.where(gsum == gmax, rows, float(ng)), axis=0, keepdims=True)
    sel = rows == gi
    pick = lambda v: jnp.sum(jnp.where(sel, v, 0.0), axis=0, keepdims=True)
    in_b = [pick(v) for v in biased]
    in_s = [pick(v) for v in plain]
    _, l1, _, l2 = _top2(in_b)
    w1 = functools.reduce(jnp.add, [jnp.where(l1 == float(j), in_s[j], 0.0) for j in range(per)])
    w2 = functools.reduce(jnp.add, [jnp.where(l2 == float(j), in_s[j], 0.0) for j in range(per)])
    e1 = gi * float(per) + l1
    e2 = gi * float(per) + l2
    wsum = w1 + w2
    e_ref[0:1, :] = e1.astype(jnp.int32)
    e_ref[1:2, :] = e2.astype(jnp.int32)
    w_ref[0:1, :] = w1 / wsum
    w_ref[1:2, :] = w2 / wsum
    row = lax.broadcasted_iota(jnp.int32, (N_EXPERTS, tm), 0)
    experts = ((row % ng) * per + row // ng).astype(F32)
    oh1 = jnp.where(experts == e1, 1.0, 0.0)
    oh2 = jnp.where(experts == e2, 1.0, 0.0)
    oh = oh1 + oh2
    ri, ci = _iota2(tm, tm)
    seen = _nn(oh, jnp.where(ri < ci, 1.0, 0.0)) + carry_ref[...]
    rank_ref[0:1, :] = jnp.sum(oh1 * seen, axis=0, keepdims=True).astype(jnp.int32)
    rank_ref[1:2, :] = jnp.sum(oh2 * seen, axis=0, keepdims=True).astype(jnp.int32)
    carry = carry_ref[...] + jnp.sum(oh, axis=1, keepdims=True)
    carry_ref[...] = carry
    cnt_ref[...] = carry.astype(jnp.int32)


def _route(h, router_w, router_bias):
    t, d = h.shape
    tm = RANK_BLOCK
    kt = lambda dt: jax.ShapeDtypeStruct((TOP_K, t), dt)
    blk = pl.BlockSpec((TOP_K, tm), lambda i: (0, i))
    member_major = lambda a: a.reshape(N_EXPERT_GROUPS, EXPERTS_PER_GROUP, -1).transpose(1, 0, 2).reshape(N_EXPERTS, -1)
    e, w, rank, counts = pl.pallas_call(
        _router_kernel,
        grid=(t // tm,),
        in_specs=[pl.BlockSpec((tm, d), lambda i: (i, 0)),
                  pl.BlockSpec((N_EXPERTS, d), lambda i: (0, 0)),
                  pl.BlockSpec((N_EXPERTS, 1), lambda i: (0, 0))],
        out_specs=[blk, blk, blk, pl.BlockSpec((N_EXPERTS, 1), lambda i: (0, 0))],
        out_shape=[kt(jnp.int32), kt(F32), kt(jnp.int32), jax.ShapeDtypeStruct((N_EXPERTS, 1), jnp.int32)],
        scratch_shapes=[pltpu.VMEM((N_EXPERTS, 1), F32)],
        compiler_params=pltpu.CompilerParams(dimension_semantics=("arbitrary",),
                                             vmem_limit_bytes=VMEM_LIMIT_BYTES),
        name="moe_router",
    )(h, member_major(router_w.T), member_major(router_bias.astype(F32).reshape(N_EXPERTS, 1)))
    counts = counts.reshape(EXPERTS_PER_GROUP, N_EXPERT_GROUPS).T.reshape(N_EXPERTS)
    return e, w, rank, counts


def _moe_ffn(h, router_w, router_bias, w_gate, w_up, w_down):
    t, d = h.shape
    expert, wts, rank, counts = _route(h, router_w, router_bias)
    n_assign = t * TOP_K
    padded = (counts + MOE_BLOCK - 1) // MOE_BLOCK * MOE_BLOCK
    pends = jnp.cumsum(padded)
    pstarts = pends - padded
    dest = pstarts[expert] + rank
    n_blocks = -(-n_assign // MOE_BLOCK) + N_EXPERTS
    hb = h.astype(BF16)
    xp = jnp.zeros((n_blocks * MOE_BLOCK, d), BF16)
    for kk in range(TOP_K):
        xp = xp.at[dest[kk]].set(hb, unique_indices=True)
    blk_start = jnp.arange(n_blocks, dtype=jnp.int32) * MOE_BLOCK
    blk_e = jnp.minimum(jnp.sum(pends[None, :] <= blk_start[:, None], axis=1), N_EXPERTS - 1).astype(jnp.int32)
    n_used = (pends[-1] // MOE_BLOCK).astype(jnp.int32).reshape(1)
    yp = _moe_experts(xp, blk_e, n_used, w_gate.astype(BF16), w_up.astype(BF16), w_down.astype(BF16))
    out = yp[dest[0]] * wts[0][:, None]
    for kk in range(1, TOP_K):
        out = out + yp[dest[kk]] * wts[kk][:, None]
    return out


def _ssd_kernel(x_ref, lac_ref, lar_ref, b_ref, c_ref, y_ref, st_ref, *, L):
    d = pl.program_id(0)
    sgn = 1 - 2 * d
    hp, gn = A_INNER, A_GROUPS * A_STATE
    hpg = A_HEADS // A_GROUPS

    @pl.when(pl.program_id(2) == 0)
    def _():
        st_ref[...] = jnp.zeros_like(st_ref)

    ri, ci = _iota2(L, L)
    before = (ci - ri) * sgn <= 0
    before_t = (ri - ci) * sgn <= 0
    x = x_ref[...]
    bm = b_ref[...]
    cm = c_ref[...]
    ccol = _mask_nn(before, lac_ref[...])
    crow = _nn_mask(lar_ref[...], before_t)
    head_of_lane = lax.broadcasted_iota(jnp.int32, (A_HEADS, hp), 1) // A_HEAD_DIM
    expand = head_of_lane == lax.broadcasted_iota(jnp.int32, (A_HEADS, hp), 0)
    cum = _nn_mask(ccol, expand)
    end = _end_row(cum, d)
    group_of_lane = lax.broadcasted_iota(jnp.int32, (L, gn), 1) // A_STATE
    first_of_pair = (lax.broadcasted_iota(jnp.int32, (L, 2 * A_HEAD_DIM), 1) < A_HEAD_DIM)
    cbs = [_nt(jnp.where(group_of_lane == g, cm, 0.0), bm) for g in range(A_GROUPS)]
    pairs = []
    for p in range(A_HEADS // 2):
        xp = x[:, 2 * p * A_HEAD_DIM:(2 * p + 2) * A_HEAD_DIM]
        ys = []
        for h in (2 * p, 2 * p + 1):
            decay = jnp.exp(jnp.where(before, ccol[:, h:h + 1] - crow[h:h + 1, :], NEG_BIG))
            ys.append(_nn(cbs[h // hpg] * decay, xp))
        pairs.append(jnp.where(first_of_pair, ys[0], ys[1]))
    st = st_ref[...]
    y_ref[...] = jnp.concatenate(pairs, axis=1) + jnp.exp(cum) * _nn(cm, st)
    own_group = (lax.broadcasted_iota(jnp.int32, (gn, hp), 0) // A_STATE
                 == lax.broadcasted_iota(jnp.int32, (gn, hp), 1) // (A_HEAD_DIM * hpg))
    st_ref[...] = jnp.exp(end) * st + jnp.where(own_group, _tn(bm, x * jnp.exp(end - cum)), 0.0)


def _ssd_scan(x, la, bm, cm, n_ctx):
    _, b, t, hp = x.shape
    L = SSD_CHUNK
    nb, ncb = t // L, n_ctx // L
    tix = lambda d, j: _time_index(d, j, ncb, nb)
    shared = pl.BlockSpec((None, L, bm.shape[-1]), lambda d, i, j: (i, tix(d, j), 0))
    return pl.pallas_call(
        functools.partial(_ssd_kernel, L=L),
        grid=(2, b, nb),
        in_specs=[pl.BlockSpec((None, None, L, hp), lambda d, i, j: (d, i, tix(d, j), 0)),
                  pl.BlockSpec((None, None, L, A_HEADS), lambda d, i, j: (d, i, tix(d, j), 0)),
                  pl.BlockSpec((None, None, A_HEADS, L), lambda d, i, j: (d, i, 0, tix(d, j))),
                  shared, shared],
        out_specs=pl.BlockSpec((None, None, L, hp), lambda d, i, j: (d, i, tix(d, j), 0)),
        out_shape=jax.ShapeDtypeStruct((2, b, t, hp), F32),
        scratch_shapes=[pltpu.VMEM((bm.shape[-1], hp), F32)],
        compiler_params=pltpu.CompilerParams(
            dimension_semantics=("arbitrary",) * 3,
            vmem_limit_bytes=VMEM_LIMIT_BYTES),
        name="ssd_scan",
    )(x, la, jnp.swapaxes(la, 2, 3), bm, cm)


def _mlstm_kernel(q_ref, k_ref, v_ref, gc_ref, gr_ref, h_ref, c_ref, n_ref, m_ref, *, L):
    d = pl.program_id(0)
    sgn = 1 - 2 * d
    nh, dk, dv = B_HEADS, B_QK_DIM, B_V_DIM

    @pl.when(pl.program_id(2) == 0)
    def _():
        c_ref[...] = jnp.zeros_like(c_ref)
        n_ref[...] = jnp.zeros_like(n_ref)
        m_ref[...] = jnp.full_like(m_ref, M_INIT)

    ri, ci = _iota2(L, L)
    before = (ci - ri) * sgn <= 0
    before_t = (ri - ci) * sgn <= 0
    gc = gc_ref[...]
    gr = gr_ref[...]
    fcol = _mask_nn(before, gc[:, nh:])
    frow = _nn_mask(gr[nh:, :], before_t)
    lane_head = lax.broadcasted_iota(jnp.int32, (L, 2 * dk), 1) // dk
    for h in range(nh):
        slab = slice((h // 2) * 2 * dk, (h // 2 + 1) * 2 * dk)
        q = jnp.where(lane_head == h % 2, q_ref[:, slab], 0.0) * (dk ** -0.5)
        k = k_ref[:, slab]
        v = v_ref[:, h * dv:(h + 1) * dv]
        li_c = gc[:, h:h + 1]
        li_r = gr[h:h + 1, :]
        f_c = fcol[:, h:h + 1]
        ftot = _end_row(f_c, d)
        c_prev = c_ref[h]
        n_prev = n_ref[h]
        m_prev = m_ref[h]
        w_end = ftot - f_c + li_c
        m_loc = jnp.max(w_end, axis=0, keepdims=True)
        ke = k * jnp.exp(w_end - m_loc)
        c_loc = _tn(ke, v)
        n_loc = jnp.sum(ke, axis=0, keepdims=True)
        log_d = jnp.where(before, f_c - frow[h:h + 1, :] + li_r, NEG_BIG)
        log_inter = f_c + m_prev
        m_row = jnp.maximum(jnp.max(log_d, axis=-1, keepdims=True), log_inter)
        s = _nt(q, k) * jnp.exp(log_d - m_row)
        inter = jnp.exp(log_inter - m_row)
        num = _nn(s, v) + inter * _nn(q, c_prev)
        den = jnp.sum(s, axis=-1, keepdims=True) + inter * jnp.sum(q * n_prev, axis=-1, keepdims=True)
        h_ref[:, h * dv:(h + 1) * dv] = num / jnp.maximum(jnp.abs(den), jnp.exp(-m_row))
        m_new = jnp.maximum(ftot + m_prev, m_loc)
        sp = jnp.exp(ftot + m_prev - m_new)
        sc = jnp.exp(m_loc - m_new)
        c_ref[h] = sp * c_prev + sc * c_loc
        n_ref[h] = sp * n_prev + sc * n_loc
        m_ref[h] = m_new


def _mlstm_scan(q, k, v, gates, n_ctx):
    b, t, _ = q.shape
    L = MLSTM_CHUNK
    nb, ncb = t // L, n_ctx // L
    tix = lambda d, j: _time_index(d, j, ncb, nb)
    shared = lambda c: pl.BlockSpec((None, L, c), lambda d, i, j: (i, tix(d, j), 0))
    return pl.pallas_call(
        functools.partial(_mlstm_kernel, L=L),
        grid=(2, b, nb),
        in_specs=[shared(B_QK), shared(B_QK), shared(B_INNER),
                  pl.BlockSpec((None, None, L, 2 * B_HEADS), lambda d, i, j: (d, i, tix(d, j), 0)),
                  pl.BlockSpec((None, None, 2 * B_HEADS, L), lambda d, i, j: (d, i, 0, tix(d, j)))],
        out_specs=pl.BlockSpec((None, None, L, B_INNER), lambda d, i, j: (d, i, tix(d, j), 0)),
        out_shape=jax.ShapeDtypeStruct((2, b, t, B_INNER), F32),
        scratch_shapes=[pltpu.VMEM((B_HEADS, 2 * B_QK_DIM, B_V_DIM), F32),
                        pltpu.VMEM((B_HEADS, 1, 2 * B_QK_DIM), F32),
                        pltpu.VMEM((B_HEADS, 1, 1), F32)],
        compiler_params=pltpu.CompilerParams(
            dimension_semantics=("arbitrary",) * 3,
            vmem_limit_bytes=VMEM_LIMIT_BYTES),
        name="mlstm_scan",
    )(q, k, v, gates, jnp.swapaxes(gates, 2, 3))


def _time_index(d, j, n_ctx_blocks, n_blocks):
    rev = jnp.where(j < n_ctx_blocks, n_ctx_blocks - 1 - j, n_blocks - 1 - j + n_ctx_blocks)
    return jnp.where(d == 1, rev, j)


def _end_row(x, d):
    n = x.shape[0]
    return jnp.where(d == 1, x[0:1, :], x[n - 1:n, :])


def _gla_kernel(q_ref, v_ref, k_ref, lf_ref, y_ref, st_ref, *, L, sub, nck):
    d = pl.program_id(0)
    sgn = 1 - 2 * d

    @pl.when(pl.program_id(3) == 0)
    def _():
        st_ref[...] = jnp.zeros_like(st_ref)

    ri, ci = _iota2(L, L)
    before = (ci - ri) * sgn <= 0
    rows = lax.broadcasted_iota(jnp.int32, (L, 1), 0)

    def body(i, carry):
        ii = i + d * (nck - 1 - 2 * i)
        sl = pl.ds(pl.multiple_of(ii * L, L), L)
        q = q_ref[sl, :]
        k = k_ref[sl, :]
        v = v_ref[sl, :]
        lam = _mask_nn(before, lf_ref[sl, :])
        lam_end = _end_row(lam, d)
        st = st_ref[...]
        blocks = []
        for c in range(L // sub):
            lo, hi = c * sub, (c + 1) * sub
            zero = jnp.zeros_like(lam_end)
            ref_f = lam[lo - 1:lo, :] if lo > 0 else zero
            ref_b = lam[hi:hi + 1, :] if hi < L else zero
            ref = jnp.where(d == 1, ref_b, ref_f)
            upto = jnp.where(d == 1, lo - 1 - rows, rows - hi) < 0
            qc = q[lo:hi, :] * jnp.exp(lam[lo:hi, :] - ref)
            kc = k * jnp.exp(jnp.where(upto, ref - lam, NEG_BIG))
            blocks.append(_nt(qc, kc))
        att = jnp.where(before, jnp.concatenate(blocks, axis=0), 0.0)
        y_ref[sl, :] = _nn(att, v) + _nt(q * jnp.exp(lam), st)
        st_ref[...] = st * jnp.exp(lam_end) + _tn(v, k * jnp.exp(lam_end - lam))
        return carry

    for i in range(nck):
        body(i, 0)


def _gla_scan(q, v, k, lf, n_ctx):
    b, t, c = q.shape
    hd = C_HEAD_DIM
    L = GLA_BLOCK
    tb = SCAN_TIME_BLOCK
    nck, nb, ncb = tb // L, t // tb, n_ctx // tb
    shared = pl.BlockSpec((None, tb, hd), lambda d, i, h, j: (i, _time_index(d, j, ncb, nb), h))
    per_dir = pl.BlockSpec((None, None, tb, hd), lambda d, i, h, j: (d, i, _time_index(d, j, ncb, nb), h))
    return pl.pallas_call(
        functools.partial(_gla_kernel, L=L, sub=GLA_CHUNK, nck=nck),
        grid=(2, b, c // hd, nb),
        in_specs=[shared, shared, per_dir, per_dir],
        out_specs=per_dir,
        out_shape=jax.ShapeDtypeStruct((2, b, t, c), F32),
        scratch_shapes=[pltpu.VMEM((hd, hd), F32)],
        compiler_params=pltpu.CompilerParams(
            dimension_semantics=("arbitrary",) * 4,
            vmem_limit_bytes=VMEM_LIMIT_BYTES),
        name="gla_scan",
    )(q, v, k, lf)


def _rwkv_kernel(r_ref, k_ref, v_ref, a_ref, b_ref, lw_ref, y_ref, h_ref, *, L, nck):
    d = pl.program_id(0)
    sgn = 1 - 2 * d
    L2 = 2 * L
    W = 2 * D_HEAD_DIM

    @pl.when(pl.program_id(3) == 0)
    def _():
        h_ref[...] = jnp.zeros_like(h_ref)

    ri, ci = _iota2(L, L)
    before = (ci - ri) * sgn <= 0
    r2, c2 = _iota2(L2, L2)
    order2 = ((c2 & (L - 1)) - (r2 & (L - 1))) * sgn
    strict2 = order2 < 0
    incl2 = order2 <= 0
    eye2 = jnp.where(r2 == c2, 1.0, 0.0)
    rw, cw = _iota2(W, W)
    eye_w = rw == cw
    head0 = lax.broadcasted_iota(jnp.int32, (L, W), 1) < D_HEAD_DIM
    stack = lambda x: jnp.concatenate([jnp.where(head0, x, 0.0), jnp.where(head0, 0.0, x)], axis=0)
    n_levels = int(math.log2(L))

    def body(i, carry):
        ii = i + d * (nck - 1 - 2 * i)
        sl = pl.ds(pl.multiple_of(ii * L, L), L)
        r, k, v, a, b, lw = (ref[sl, :] for ref in (r_ref, k_ref, v_ref, a_ref, b_ref, lw_ref))
        cum = _mask_nn(before, lw)
        cum_end = _end_row(cum, d)
        e_neg = jnp.exp(-cum)
        e_end = jnp.exp(cum_end - cum)
        at, rt = stack(a * jnp.exp(cum - lw)), stack(r * jnp.exp(cum))
        bt, kt = stack(b * e_neg), stack(k * e_neg)
        vs = stack(v)
        gram = _nt(jnp.concatenate([at, rt], axis=0), jnp.concatenate([bt, kt], axis=0))
        nmat = jnp.where(strict2, gram[:L2, :L2], 0.0)
        a_k = jnp.where(strict2, gram[:L2, L2:], 0.0)
        r_bk = jnp.where(jnp.concatenate([incl2, incl2], axis=1), gram[L2:, :], 0.0)
        tinv = eye2 + nmat
        pw = _nn(nmat, nmat)
        for lev in range(1, n_levels):
            if lev < n_levels - 1:
                both = _nn(pw, jnp.concatenate([pw, tinv], axis=1))
                pw, tinv = both[:, :L2], tinv + both[:, L2:]
            else:
                tinv = tinv + _nn(pw, tinv)
        wu = _nn(tinv, jnp.concatenate([at, _nn(a_k, vs)], axis=1))
        zs = jnp.concatenate([wu, jnp.concatenate([jnp.zeros_like(vs), vs], axis=1)], axis=0)
        qy = _nn(r_bk, zs)
        md = _tn(jnp.concatenate([stack(b * e_end), stack(k * e_end)], axis=0), zs)
        h = h_ref[...]
        ys = _nn(rt + qy[:, :W], h) + qy[:, W:]
        y_ref[sl, :] = ys[:L, :] + ys[L:, :]
        dec = jnp.sum(jnp.where(eye_w, jnp.broadcast_to(jnp.exp(cum_end), (W, W)), 0.0), axis=1, keepdims=True)
        h_ref[...] = dec * h + _nn(md[:, :W], h) + md[:, W:]
        return carry

    for i in range(nck):
        body(i, 0)


def _rwkv_scan(r, k, v, a, b, lw, n_ctx):
    bsz, t, c = r.shape
    w = 2 * D_HEAD_DIM
    L = RWKV_CHUNK
    tb = SCAN_TIME_BLOCK
    nck, nb, ncb = tb // L, t // tb, n_ctx // tb
    shared = pl.BlockSpec((None, tb, w), lambda d, i, p, j: (i, _time_index(d, j, ncb, nb), p))
    per_dir = pl.BlockSpec((None, None, tb, w), lambda d, i, p, j: (d, i, _time_index(d, j, ncb, nb), p))
    return pl.pallas_call(
        functools.partial(_rwkv_kernel, L=L, nck=nck),
        grid=(2, bsz, c // w, nb),
        in_specs=[shared] * 5 + [per_dir],
        out_specs=per_dir,
        out_shape=jax.ShapeDtypeStruct((2, bsz, t, c), F32),
        scratch_shapes=[pltpu.VMEM((w, w), F32)],
        compiler_params=pltpu.CompilerParams(
            dimension_semantics=("arbitrary",) * 4,
            vmem_limit_bytes=VMEM_LIMIT_BYTES),
        name="rwkv7_scan",
    )(r, k, v, a, b, lw)


def _neighbours(x, n_ctx):
    pos = jnp.arange(x.shape[1])[None, :, None]
    prev = jnp.pad(x[:, :-1], ((0, 0), (1, 0), (0, 0)))
    nxt = jnp.pad(x[:, 1:], ((0, 0), (0, 1), (0, 0)))
    return jnp.where(pos == n_ctx, 0.0, prev), jnp.where(pos == n_ctx - 1, 0.0, nxt)


def _dwconv3(x, w, b, n_ctx):
    prev, nxt = _neighbours(x, n_ctx)
    return prev * w[0] + x * w[1] + nxt * w[2] + b


def _layer_norm(x, g, b):
    mu = x.mean(-1, keepdims=True)
    var = jnp.mean(jnp.square(x - mu), -1, keepdims=True)
    return (x - mu) * lax.rsqrt(var + LN_EPS) * g + b


def _head_norm(y, eps, center):
    if center:
        y = y - y.mean(-1, keepdims=True)
    y = y * lax.rsqrt(jnp.mean(y * y, -1, keepdims=True) + eps)
    return y.reshape(y.shape[0], y.shape[1], -1)


def _ssd_mixer(p, n_ctx, params):
    conv_w, conv_b, dt_bias, a_log, d_skip, norm_w = params
    b, t, _ = p.shape
    a = -jnp.exp(a_log)
    z = p[..., :A_INNER]
    xbc = jax.nn.silu(_dwconv3(p[..., A_INNER:A_INNER + A_XBC], conv_w, conv_b, n_ctx))
    xs = xbc[..., :A_INNER].reshape(b, t, A_HEADS, A_HEAD_DIM)
    bm = xbc[..., A_INNER:A_INNER + A_GROUPS * A_STATE]
    cm = xbc[..., A_INNER + A_GROUPS * A_STATE:]
    dt = jnp.moveaxis(jax.nn.softplus(p[..., A_INNER + A_XBC:].reshape(b, t, 2, A_HEADS) + dt_bias), 2, 0)
    x = (xs[None] * dt[..., None]).reshape(2, b, t, A_INNER)
    y = _ssd_scan(x, dt * a[:, None, None, :], bm, cm, n_ctx)
    y = ((y[0] + y[1]).reshape(b, t, A_HEADS, A_HEAD_DIM) + d_skip[:, None] * xs).reshape(b, t, A_INNER) * jax.nn.silu(z)
    return y * lax.rsqrt(jnp.mean(y * y, -1, keepdims=True) + 1e-6) * norm_w


def _mlstm_mixer(p, n_ctx, params):
    conv_w, conv_b, i_bias, f_bias, norm_w = params
    b, t, _ = p.shape
    qk = jax.nn.silu(_dwconv3(p[..., :2 * B_QK], conv_w, conv_b, n_ctx))
    v = p[..., 2 * B_QK:2 * B_QK + B_INNER]
    o = p[..., 2 * B_QK + B_INNER:2 * B_QK + 2 * B_INNER]
    gates = p[..., 2 * B_QK + 2 * B_INNER:].reshape(b, t, 2, 2, B_HEADS)
    log_i = gates[:, :, 0] + i_bias
    log_f = jax.nn.log_sigmoid(gates[:, :, 1] + f_bias)
    g = jnp.moveaxis(jnp.concatenate([log_i, log_f], axis=-1), 2, 0)
    h = _mlstm_scan(qk[..., :B_QK], qk[..., B_QK:], v, g, n_ctx)
    h = (h[0] + h[1]).reshape(b, t, B_HEADS, B_V_DIM)
    return jax.nn.sigmoid(o) * _head_norm(h, MLSTM_EPS, True) * norm_w


def _hgrn2_mixer(p, n_ctx, lb, params):
    f_bias, norm_w = params
    b, t, _ = p.shape
    q = jax.nn.silu(p[..., :C_INNER])
    f_pre = jnp.moveaxis(p[..., C_INNER:3 * C_INNER].reshape(b, t, 2, C_INNER), 2, 0) + f_bias[:, None, None, :]
    log_f = jnp.log(lb + (1 - lb) * jax.nn.sigmoid(f_pre))
    k = (1 - lb) * jax.nn.sigmoid(-f_pre)
    o = _gla_scan(q, p[..., 3 * C_INNER:4 * C_INNER], k, log_f, n_ctx)
    o = (o[0] + o[1]).reshape(b, t, C_HEADS, C_HEAD_DIM)
    return _head_norm(o, 1e-6, False) * norm_w * jax.nn.silu(p[..., 4 * C_INNER:])


def _rwkv7_mixer(p, n_ctx, params):
    mu, w0, w2, a0, a2, g2, k_k, k_a, r_k, ln_w, ln_b = params
    b, t, _ = p.shape
    hd = lambda u: u.reshape(b, t, D_HEADS, D_HEAD_DIM)
    prev, nxt = _neighbours(p, n_ctx)
    p = p + mu * (0.5 * (prev + nxt) - p)
    r = p[..., :D_INNER]
    k = p[..., D_INNER:2 * D_INNER]
    v = p[..., 2 * D_INNER:3 * D_INNER]
    o = 3 * D_INNER
    wl = jnp.tanh(p[..., o:o + 2 * D_W_LORA]).reshape(b * t, 2, D_W_LORA)
    al = p[..., o + 2 * D_W_LORA:o + 2 * D_W_LORA + D_A_LORA]
    gl = p[..., o + 2 * D_W_LORA + D_A_LORA:]
    w = jnp.stack([_matmul(wl[:, d], w2[d]).reshape(b, t, D_INNER) + w0[d] for d in range(2)])
    log_decay = -jnp.exp(-jax.nn.softplus(-w) - 0.5)
    a = jax.nn.sigmoid(a0 + _matmul(al.reshape(b * t, D_A_LORA), a2).reshape(b, t, D_INNER))
    g = _matmul(jax.nn.sigmoid(gl).reshape(b * t, D_G_LORA), g2).reshape(b, t, D_INNER)
    kk = hd(k * k_k)
    kk = (kk * lax.rsqrt(jnp.maximum(jnp.sum(kk * kk, -1, keepdims=True), 1e-12))).reshape(b, t, D_INNER)
    k = k * (1 + (a - 1) * k_a)
    y = _rwkv_scan(r, k, v, -kk, kk * a, log_decay, n_ctx)
    y = hd(y[0] + y[1])
    bonus = (jnp.sum(hd(r) * hd(k) * r_k, axis=-1, keepdims=True) * hd(v)).reshape(b, t, D_INNER)
    return (_head_norm(y, RWKV_EPS, True) * ln_w + ln_b + bonus) * g


def _to_col_major(u, rows):
    b, s, d = u.shape
    return u.reshape(b, rows, GRID_W, d).transpose(0, 2, 1, 3).reshape(b, s, d)


def _from_col_major(u, rows):
    b, s, d = u.shape
    return u.reshape(b, GRID_W, rows, d).transpose(0, 2, 1, 3).reshape(b, s, d)


def _proj(u, w):
    b, t, d = u.shape
    return _matmul(u.reshape(b * t, d), w).reshape(b, t, w.shape[1])


def kernel(x, c, ctx, c_ctx, mod_w, mod_b, ln_g, ln_b, ev_w_in, ev_w_out, ssd_conv_w, ssd_conv_b, ssd_dt_bias, ssd_a_log, ssd_d, ssd_norm_w, mlstm_conv_w, mlstm_conv_b, mlstm_i_bias, mlstm_f_bias, mlstm_norm_w, od_w_in, od_w_out, hgrn_lb_logits, hgrn_f_bias, hgrn_norm_w, rwkv_mu, rwkv_w0, rwkv_w2, rwkv_a0, rwkv_a2, rwkv_g2, rwkv_k_k, rwkv_k_a, rwkv_r_k, rwkv_ln_w, rwkv_ln_b, router_w, router_bias, exp_w_gate, exp_w_up, exp_w_down):
    bsz, seq, _ = x.shape
    n_ctx = ctx.shape[1]
    rows = seq // GRID_W
    lb_all = jnp.cumsum(jax.nn.softmax(hgrn_lb_logits.astype(F32), axis=0), axis=0)
    lb_all = lb_all - lb_all[0]
    s_c = jax.nn.silu(c)
    s_cc = jax.nn.silu(c_ctx)
    xa = jnp.concatenate([ctx, x], axis=1)
    is_ctx = (jnp.arange(n_ctx + seq) < n_ctx)[None, :, None]
    for layer in range(DEPTH):
        i = layer // 2
        mods = _matmul(jnp.concatenate([s_c, s_cc[None]], axis=0), mod_w[layer], tm=8, tn=512) + mod_b[layer]
        sh1, sc1, g1, sh2, sc2, g2 = [jnp.where(is_ctx, mc[None, None, :], ml[:, None, :]) for ml, mc in
                                      zip(jnp.split(mods[:bsz], 6, axis=-1), jnp.split(mods[bsz], 6, axis=-1))]
        u = xa * (1 + sc1) + sh1
        if layer % 2 == 0:
            p = _proj(u, ev_w_in[i])
            feat = jnp.concatenate([
                _ssd_mixer(p[..., :P_A], n_ctx,
                           (ssd_conv_w[i], ssd_conv_b[i], ssd_dt_bias[i], ssd_a_log[i], ssd_d[i], ssd_norm_w[i])),
                _mlstm_mixer(p[..., P_A:], n_ctx,
                             (mlstm_conv_w[i], mlstm_conv_b[i], mlstm_i_bias[i], mlstm_f_bias[i], mlstm_norm_w[i]))],
                axis=-1)
            w_out = ev_w_out[i]
        else:
            u = jnp.concatenate([u[:, :n_ctx], _to_col_major(u[:, n_ctx:], rows)], axis=1)
            p = _proj(u, od_w_in[i])
            feat = jnp.concatenate([
                _hgrn2_mixer(p[..., :P_C], n_ctx, lb_all[layer], (hgrn_f_bias[i], hgrn_norm_w[i])),
                _rwkv7_mixer(p[..., P_C:], n_ctx,
                             (rwkv_mu[i], rwkv_w0[i], rwkv_w2[i], rwkv_a0[i], rwkv_a2[i], rwkv_g2[i],
                              rwkv_k_k[i], rwkv_k_a[i], rwkv_r_k[i], rwkv_ln_w[i], rwkv_ln_b[i]))], axis=-1)
            feat = jnp.concatenate([feat[:, :n_ctx], _from_col_major(feat[:, n_ctx:], rows)], axis=1)
            w_out = od_w_out[i]
        xa = _layer_norm(DEEPNORM_ALPHA * xa + g1 * _proj(feat, w_out), ln_g[layer, 0], ln_b[layer, 0])
        h = xa * (1 + sc2) + sh2
        f = _moe_ffn(h.reshape(-1, D_MODEL), router_w, router_bias,
                     exp_w_gate[layer], exp_w_up[layer], exp_w_down[layer])
        xa = _layer_norm(DEEPNORM_ALPHA * xa + g2 * f.reshape(xa.shape), ln_g[layer, 1], ln_b[layer, 1])
    return xa[:, n_ctx:]
```

```python
import functools
import math

import jax
import jax.numpy as jnp
from jax import lax
from jax.experimental import pallas as pl
from jax.experimental.pallas import tpu as pltpu

F32 = jnp.float32
BF16 = jnp.bfloat16

D_MODEL = 1024
DEPTH = 4
GRID_W = 64
A_HEADS = 8
A_HEAD_DIM = 64
A_INNER = A_HEADS * A_HEAD_DIM
A_GROUPS = 2
A_STATE = 64
A_XBC = A_INNER + 2 * A_GROUPS * A_STATE
B_HEADS = 4
B_QK_DIM = 64
B_V_DIM = 128
B_QK = B_HEADS * B_QK_DIM
B_INNER = B_HEADS * B_V_DIM
MLSTM_EPS = 1e-6
C_HEADS = 4
C_HEAD_DIM = 128
C_INNER = C_HEADS * C_HEAD_DIM
D_HEADS = 8
D_HEAD_DIM = 64
D_INNER = D_HEADS * D_HEAD_DIM
D_W_LORA = 64
D_A_LORA = 64
D_G_LORA = 128
RWKV_EPS = 64e-5
P_A = A_INNER + A_XBC + 2 * A_HEADS
P_B = 2 * B_QK + 2 * B_INNER + 4 * B_HEADS
P_C = 5 * C_INNER
P_D = 3 * D_INNER + 2 * D_W_LORA + D_A_LORA + D_G_LORA
N_EXPERTS = 32
N_EXPERT_GROUPS = 8
EXPERTS_PER_GROUP = N_EXPERTS // N_EXPERT_GROUPS
TOP_K = 2
D_EXPERT = 512
MOE_BLOCK = 128
RANK_BLOCK = 512
DEEPNORM_ALPHA = (2 * DEPTH) ** 0.25
LN_EPS = 1e-5
M_INIT = -1e30
NEG_BIG = -1e30

SSD_CHUNK = 128
MLSTM_CHUNK = 128
GLA_CHUNK = 16
GLA_BLOCK = 64
GLA_HEADS_PER_STEP = 4
RWKV_CHUNK = 64
RWKV_PAIRS_PER_STEP = 4
SCAN_TIME_BLOCK = 256

VMEM_LIMIT_BYTES = 48 * 1024 * 1024
HI = lax.Precision.HIGHEST


def _dot(a, b, dims, exact):
    if exact:
        return lax.dot_general(a.astype(F32), b.astype(F32), (dims, ((), ())),
                               precision=HI, preferred_element_type=F32)
    return lax.dot_general(a.astype(BF16), b.astype(BF16), (dims, ((), ())),
                           preferred_element_type=F32)


def _nn(a, b, exact=False):
    return _dot(a, b, ((1,), (0,)), exact)


def _nt(a, b, exact=False):
    return _dot(a, b, ((1,), (1,)), exact)


def _tn(a, b, exact=False):
    return _dot(a, b, ((0,), (0,)), exact)


def _iota2(n, m):
    return (lax.broadcasted_iota(jnp.int32, (n, m), 0),
            lax.broadcasted_iota(jnp.int32, (n, m), 1))


def _split3(x):
    x1 = x.astype(BF16)
    r1 = x - x1.astype(F32)
    x2 = r1.astype(BF16)
    x3 = (r1 - x2.astype(F32)).astype(BF16)
    return x1, x2, x3


def _mask_nn(mask, x):
    mb = mask.astype(BF16)
    x1, x2, x3 = _split3(x)
    return _nn(mb, x1, False) + _nn(mb, x2, False) + _nn(mb, x3, False)


def _nn_mask(x, mask):
    mb = mask.astype(BF16)
    x1, x2, x3 = _split3(x)
    return _nn(x1, mb, False) + _nn(x2, mb, False) + _nn(x3, mb, False)


def _cum_mats(col, row, L):
    ri, ci = _iota2(L, L)
    ccol = _mask_nn(ci <= ri, jnp.broadcast_to(col, (L, L)))
    crow = _nn_mask(jnp.broadcast_to(row, (L, L)), ri <= ci)
    return ccol, crow


def _mm_kernel(x_ref, w_ref, o_ref, *, exact):
    o_ref[...] = _nn(x_ref[...], w_ref[...], exact)


def _matmul(x, w, tm=512, tn=512, exact=False):
    m, k = x.shape
    n = w.shape[1]
    n_pad = -(-n // tn) * tn
    m_pad = -(-m // tm) * tm
    xb = x if exact else x.astype(BF16)
    wb = w if exact else w.astype(BF16)
    if n_pad != n:
        wb = jnp.pad(wb, ((0, 0), (0, n_pad - n)))
    if m_pad != m:
        xb = jnp.pad(xb, ((0, m_pad - m), (0, 0)))
    out = pl.pallas_call(
        functools.partial(_mm_kernel, exact=exact),
        grid=(n_pad // tn, m_pad // tm),
        in_specs=[pl.BlockSpec((tm, k), lambda j, i: (i, 0)),
                  pl.BlockSpec((k, tn), lambda j, i: (0, j))],
        out_specs=pl.BlockSpec((tm, tn), lambda j, i: (i, j)),
        out_shape=jax.ShapeDtypeStruct((m_pad, n_pad), F32),
        compiler_params=pltpu.CompilerParams(
            dimension_semantics=("arbitrary", "arbitrary"),
            vmem_limit_bytes=VMEM_LIMIT_BYTES),
        name="dense_matmul",
    )(xb, wb)
    return out[:m, :n]


def _moe_kernel(blk_e_ref, n_used_ref, x_ref, wg_ref, wu_ref, wd_ref, o_ref):
    i = pl.program_id(0)

    @pl.when(i < n_used_ref[0])
    def _():
        x = x_ref[...]
        g = jnp.dot(x, wg_ref[...], preferred_element_type=F32)
        u = jnp.dot(x, wu_ref[...], preferred_element_type=F32)
        hmid = (g * jax.nn.sigmoid(g) * u).astype(BF16)
        o_ref[...] = jnp.dot(hmid, wd_ref[...], preferred_element_type=F32)

    @pl.when(i >= n_used_ref[0])
    def _():
        o_ref[...] = jnp.zeros_like(o_ref)


def _moe_experts(xp, blk_e, n_used, w_gate, w_up, w_down):
    n_rows, d = xp.shape
    n_blocks = n_rows // MOE_BLOCK
    grid_spec = pltpu.PrefetchScalarGridSpec(
        num_scalar_prefetch=2,
        grid=(n_blocks,),
        in_specs=[
            pl.BlockSpec((MOE_BLOCK, d), lambda i, be, nu: (i, 0)),
            pl.BlockSpec((None, d, D_EXPERT), lambda i, be, nu: (be[i], 0, 0)),
            pl.BlockSpec((None, d, D_EXPERT), lambda i, be, nu: (be[i], 0, 0)),
            pl.BlockSpec((None, D_EXPERT, d), lambda i, be, nu: (be[i], 0, 0)),
        ],
        out_specs=pl.BlockSpec((MOE_BLOCK, d), lambda i, be, nu: (i, 0)),
    )
    return pl.pallas_call(
        _moe_kernel,
        grid_spec=grid_spec,
        out_shape=jax.ShapeDtypeStruct((n_rows, d), F32),
        compiler_params=pltpu.CompilerParams(
            dimension_semantics=("arbitrary",),
            vmem_limit_bytes=VMEM_LIMIT_BYTES),
        name="moe_experts",
    )(blk_e, n_used, xp, w_gate, w_up, w_down)


def _top2(vals):
    m = len(vals)
    m1 = functools.reduce(jnp.maximum, vals)
    i1 = jnp.full_like(m1, float(m - 1))
    for j in reversed(range(m - 1)):
        i1 = jnp.where(vals[j] == m1, float(j), i1)
    rest = [jnp.where(i1 == float(j), -jnp.inf, vals[j]) for j in range(m)]
    m2 = functools.reduce(jnp.maximum, rest)
    i2 = jnp.full_like(m1, float(m - 1))
    for j in reversed(range(m - 1)):
        i2 = jnp.where(rest[j] == m2, float(j), i2)
    return m1, i1, m2, i2


def _router_kernel(h_ref, wt_ref, bias_ref, e_ref, w_ref, rank_ref, cnt_ref, carry_ref):
    tm = h_ref.shape[0]
    ng, per = N_EXPERT_GROUPS, EXPERTS_PER_GROUP

    @pl.when(pl.program_id(0) == 0)
    def _():
        carry_ref[...] = jnp.zeros_like(carry_ref)

    s = jax.nn.sigmoid(_nt(wt_ref[...], h_ref[...], True))
    sb = s + bias_ref[...]
    biased = [sb[j * ng:(j + 1) * ng, :] for j in range(per)]
    plain = [s[j * ng:(j + 1) * ng, :] for j in range(per)]
    m1, _, m2, _ = _top2(biased)
    gsum = m1 + m2
    rows = lax.broadcasted_iota(jnp.int32, (ng, tm), 0).astype(F32)
    gmax = jnp.max(gsum, axis=0, keepdims=True)
    gi = jnp.min(jnp.where(gsum == gmax, rows, float(ng)), axis=0, keepdims=True)
    sel = rows == gi
    pick = lambda v: jnp.sum(jnp.where(sel, v, 0.0), axis=0, keepdims=True)
    in_b = [pick(v) for v in biased]
    in_s = [pick(v) for v in plain]
    _, l1, _, l2 = _top2(in_b)
    w1 = functools.reduce(jnp.add, [jnp.where(l1 == float(j), in_s[j], 0.0) for j in range(per)])
    w2 = functools.reduce(jnp.add, [jnp.where(l2 == float(j), in_s[j], 0.0) for j in range(per)])
    e1 = gi * float(per) + l1
    e2 = gi * float(per) + l2
    wsum = w1 + w2
    e_ref[0:1, :] = e1.astype(jnp.int32)
    e_ref[1:2, :] = e2.astype(jnp.int32)
    w_ref[0:1, :] = w1 / wsum
    w_ref[1:2, :] = w2 / wsum
    row = lax.broadcasted_iota(jnp.int32, (N_EXPERTS, tm), 0)
    experts = ((row % ng) * per + row // ng).astype(F32)
    oh1 = jnp.where(experts == e1, 1.0, 0.0)
    oh2 = jnp.where(experts == e2, 1.0, 0.0)
    oh = oh1 + oh2
    ri, ci = _iota2(tm, tm)
    seen = _nn(oh, jnp.where(ri < ci, 1.0, 0.0)) + carry_ref[...]
    rank_ref[0:1, :] = jnp.sum(oh1 * seen, axis=0, keepdims=True).astype(jnp.int32)
    rank_ref[1:2, :] = jnp.sum(oh2 * seen, axis=0, keepdims=True).astype(jnp.int32)
    carry = carry_ref[...] + jnp.sum(oh, axis=1, keepdims=True)
    carry_ref[...] = carry
    cnt_ref[...] = carry.astype(jnp.int32)


def _route(h, router_w, router_bias):
    t, d = h.shape
    tm = RANK_BLOCK
    kt = lambda dt: jax.ShapeDtypeStruct((TOP_K, t), dt)
    blk = pl.BlockSpec((TOP_K, tm), lambda i: (0, i))
    member_major = lambda a: a.reshape(N_EXPERT_GROUPS, EXPERTS_PER_GROUP, -1).transpose(1, 0, 2).reshape(N_EXPERTS, -1)
    e, w, rank, counts = pl.pallas_call(
        _router_kernel,
        grid=(t // tm,),
        in_specs=[pl.BlockSpec((tm, d), lambda i: (i, 0)),
                  pl.BlockSpec((N_EXPERTS, d), lambda i: (0, 0)),
                  pl.BlockSpec((N_EXPERTS, 1), lambda i: (0, 0))],
        out_specs=[blk, blk, blk, pl.BlockSpec((N_EXPERTS, 1), lambda i: (0, 0))],
        out_shape=[kt(jnp.int32), kt(F32), kt(jnp.int32), jax.ShapeDtypeStruct((N_EXPERTS, 1), jnp.int32)],
        scratch_shapes=[pltpu.VMEM((N_EXPERTS, 1), F32)],
        compiler_params=pltpu.CompilerParams(dimension_semantics=("arbitrary",),
                                             vmem_limit_bytes=VMEM_LIMIT_BYTES),
        name="moe_router",
    )(h, member_major(router_w.T), member_major(router_bias.astype(F32).reshape(N_EXPERTS, 1)))
    counts = counts.reshape(EXPERTS_PER_GROUP, N_EXPERT_GROUPS).T.reshape(N_EXPERTS)
    return e, w, rank, counts


def _moe_ffn(h, router_w, router_bias, w_gate, w_up, w_down):
    t, d = h.shape
    expert, wts, rank, counts = _route(h, router_w, router_bias)
    n_assign = t * TOP_K
    padded = (counts + MOE_BLOCK - 1) // MOE_BLOCK * MOE_BLOCK
    pends = jnp.cumsum(padded)
    pstarts = pends - padded
    dest = pstarts[expert] + rank
    n_blocks = -(-n_assign // MOE_BLOCK) + N_EXPERTS
    slot_token = jnp.zeros((n_blocks * MOE_BLOCK,), jnp.int32)
    for kk in range(TOP_K):
        slot_token = slot_token.at[dest[kk]].set(jnp.arange(t, dtype=jnp.int32), unique_indices=True)
    xp = h.astype(BF16)[slot_token]
    blk_start = jnp.arange(n_blocks, dtype=jnp.int32) * MOE_BLOCK
    blk_e = jnp.minimum(jnp.sum(pends[None, :] <= blk_start[:, None], axis=1), N_EXPERTS - 1).astype(jnp.int32)
    n_used = (pends[-1] // MOE_BLOCK).astype(jnp.int32).reshape(1)
    yp = _moe_experts(xp, blk_e, n_used, w_gate.astype(BF16), w_up.astype(BF16), w_down.astype(BF16))
    out = yp[dest[0]] * wts[0][:, None]
    for kk in range(1, TOP_K):
        out = out + yp[dest[kk]] * wts[kk][:, None]
    return out


def _ssd_kernel(x_ref, lac_ref, lar_ref, b_ref, c_ref, y_ref, st_ref, *, L):
    d = pl.program_id(0)
    sgn = 1 - 2 * d
    hp, gn = A_INNER, A_GROUPS * A_STATE
    hpg = A_HEADS // A_GROUPS

    @pl.when(pl.program_id(2) == 0)
    def _():
        st_ref[...] = jnp.zeros_like(st_ref)

    ri, ci = _iota2(L, L)
    before = (ci - ri) * sgn <= 0
    before_t = (ri - ci) * sgn <= 0
    x = x_ref[...]
    bm = b_ref[...]
    cm = c_ref[...]
    ccol = _mask_nn(before, lac_ref[...])
    crow = _nn_mask(lar_ref[...], before_t)
    head_of_lane = lax.broadcasted_iota(jnp.int32, (A_HEADS, hp), 1) // A_HEAD_DIM
    expand = head_of_lane == lax.broadcasted_iota(jnp.int32, (A_HEADS, hp), 0)
    cum = _nn_mask(ccol, expand)
    end = _end_row(cum, d)
    group_of_lane = lax.broadcasted_iota(jnp.int32, (L, gn), 1) // A_STATE
    first_of_pair = (lax.broadcasted_iota(jnp.int32, (L, 2 * A_HEAD_DIM), 1) < A_HEAD_DIM)
    cbs = [_nt(jnp.where(group_of_lane == g, cm, 0.0), bm) for g in range(A_GROUPS)]
    pairs = []
    for p in range(A_HEADS // 2):
        xp = x[:, 2 * p * A_HEAD_DIM:(2 * p + 2) * A_HEAD_DIM]
        ys = []
        for h in (2 * p, 2 * p + 1):
            decay = jnp.exp(jnp.where(before, ccol[:, h:h + 1] - crow[h:h + 1, :], NEG_BIG))
            ys.append(_nn(cbs[h // hpg] * decay, xp))
        pairs.append(jnp.where(first_of_pair, ys[0], ys[1]))
    st = st_ref[...]
    y_ref[...] = jnp.concatenate(pairs, axis=1) + jnp.exp(cum) * _nn(cm, st)
    own_group = (lax.broadcasted_iota(jnp.int32, (gn, hp), 0) // A_STATE
                 == lax.broadcasted_iota(jnp.int32, (gn, hp), 1) // (A_HEAD_DIM * hpg))
    st_ref[...] = jnp.exp(end) * st + jnp.where(own_group, _tn(bm, x * jnp.exp(end - cum)), 0.0)


def _ssd_scan(x, la, bm, cm, n_ctx):
    _, b, t, hp = x.shape
    L = SSD_CHUNK
    nb, ncb = t // L, n_ctx // L
    tix = lambda d, j: _time_index(d, j, ncb, nb)
    shared = pl.BlockSpec((None, L, bm.shape[-1]), lambda d, i, j: (i, tix(d, j), 0))
    return pl.pallas_call(
        functools.partial(_ssd_kernel, L=L),
        grid=(2, b, nb),
        in_specs=[pl.BlockSpec((None, None, L, hp), lambda d, i, j: (d, i, tix(d, j), 0)),
                  pl.BlockSpec((None, None, L, A_HEADS), lambda d, i, j: (d, i, tix(d, j), 0)),
                  pl.BlockSpec((None, None, A_HEADS, L), lambda d, i, j: (d, i, 0, tix(d, j))),
                  shared, shared],
        out_specs=pl.BlockSpec((None, None, L, hp), lambda d, i, j: (d, i, tix(d, j), 0)),
        out_shape=jax.ShapeDtypeStruct((2, b, t, hp), F32),
        scratch_shapes=[pltpu.VMEM((bm.shape[-1], hp), F32)],
        compiler_params=pltpu.CompilerParams(
            dimension_semantics=("arbitrary",) * 3,
            vmem_limit_bytes=VMEM_LIMIT_BYTES),
        name="ssd_scan",
    )(x, la, jnp.swapaxes(la, 2, 3), bm, cm)


def _mlstm_kernel(q_ref, k_ref, v_ref, gc_ref, gr_ref, h_ref, c_ref, n_ref, m_ref, *, L):
    d = pl.program_id(0)
    sgn = 1 - 2 * d
    nh, dk, dv = B_HEADS, B_QK_DIM, B_V_DIM

    @pl.when(pl.program_id(2) == 0)
    def _():
        c_ref[...] = jnp.zeros_like(c_ref)
        n_ref[...] = jnp.zeros_like(n_ref)
        m_ref[...] = jnp.full_like(m_ref, M_INIT)

    ri, ci = _iota2(L, L)
    before = (ci - ri) * sgn <= 0
    before_t = (ri - ci) * sgn <= 0
    gc = gc_ref[...]
    gr = gr_ref[...]
    fcol = _mask_nn(before, gc[:, nh:])
    frow = _nn_mask(gr[nh:, :], before_t)
    lane_head = lax.broadcasted_iota(jnp.int32, (L, 2 * dk), 1) // dk
    for h in range(nh):
        slab = slice((h // 2) * 2 * dk, (h // 2 + 1) * 2 * dk)
        q = jnp.where(lane_head == h % 2, q_ref[:, slab], 0.0) * (dk ** -0.5)
        k = k_ref[:, slab]
        v = v_ref[:, h * dv:(h + 1) * dv]
        li_c = gc[:, h:h + 1]
        li_r = gr[h:h + 1, :]
        f_c = fcol[:, h:h + 1]
        ftot = _end_row(f_c, d)
        c_prev = c_ref[h]
        n_prev = n_ref[h]
        m_prev = m_ref[h]
        w_end = ftot - f_c + li_c
        m_loc = jnp.max(w_end, axis=0, keepdims=True)
        ke = k * jnp.exp(w_end - m_loc)
        c_loc = _tn(ke, v)
        n_loc = jnp.sum(ke, axis=0, keepdims=True)
        log_d = jnp.where(before, f_c - frow[h:h + 1, :] + li_r, NEG_BIG)
        log_inter = f_c + m_prev
        m_row = jnp.maximum(jnp.max(log_d, axis=-1, keepdims=True), log_inter)
        s = _nt(q, k) * jnp.exp(log_d - m_row)
        inter = jnp.exp(log_inter - m_row)
        num = _nn(s, v) + inter * _nn(q, c_prev)
        den = jnp.sum(s, axis=-1, keepdims=True) + inter * jnp.sum(q * n_prev, axis=-1, keepdims=True)
        h_ref[:, h * dv:(h + 1) * dv] = num / jnp.maximum(jnp.abs(den), jnp.exp(-m_row))
        m_new = jnp.maximum(ftot + m_prev, m_loc)
        sp = jnp.exp(ftot + m_prev - m_new)
        sc = jnp.exp(m_loc - m_new)
        c_ref[h] = sp * c_prev + sc * c_loc
        n_ref[h] = sp * n_prev + sc * n_loc
        m_ref[h] = m_new


def _mlstm_scan(q, k, v, gates, n_ctx):
    b, t, _ = q.shape
    L = MLSTM_CHUNK
    nb, ncb = t // L, n_ctx // L
    tix = lambda d, j: _time_index(d, j, ncb, nb)
    shared = lambda c: pl.BlockSpec((None, L, c), lambda d, i, j: (i, tix(d, j), 0))
    return pl.pallas_call(
        functools.partial(_mlstm_kernel, L=L),
        grid=(2, b, nb),
        in_specs=[shared(B_QK), shared(B_QK), shared(B_INNER),
                  pl.BlockSpec((None, None, L, 2 * B_HEADS), lambda d, i, j: (d, i, tix(d, j), 0)),
                  pl.BlockSpec((None, None, 2 * B_HEADS, L), lambda d, i, j: (d, i, 0, tix(d, j)))],
        out_specs=pl.BlockSpec((None, None, L, B_INNER), lambda d, i, j: (d, i, tix(d, j), 0)),
        out_shape=jax.ShapeDtypeStruct((2, b, t, B_INNER), F32),
        scratch_shapes=[pltpu.VMEM((B_HEADS, 2 * B_QK_DIM, B_V_DIM), F32),
                        pltpu.VMEM((B_HEADS, 1, 2 * B_QK_DIM), F32),
                        pltpu.VMEM((B_HEADS, 1, 1), F32)],
        compiler_params=pltpu.CompilerParams(
            dimension_semantics=("arbitrary",) * 3,
            vmem_limit_bytes=VMEM_LIMIT_BYTES),
        name="mlstm_scan",
    )(q, k, v, gates, jnp.swapaxes(gates, 2, 3))


def _time_index(d, j, n_ctx_blocks, n_blocks):
    rev = jnp.where(j < n_ctx_blocks, n_ctx_blocks - 1 - j, n_blocks - 1 - j + n_ctx_blocks)
    return jnp.where(d == 1, rev, j)


def _end_row(x, d):
    n = x.shape[0]
    return jnp.where(d == 1, x[0:1, :], x[n - 1:n, :])


def _gla_kernel(q_ref, v_ref, k_ref, lf_ref, y_ref, st_ref, *, L, sub, nck):
    d = pl.program_id(0)
    sgn = 1 - 2 * d

    @pl.when(pl.program_id(3) == 0)
    def _():
        st_ref[...] = jnp.zeros_like(st_ref)

    ri, ci = _iota2(L, L)
    before = (ci - ri) * sgn <= 0
    rows = lax.broadcasted_iota(jnp.int32, (L, 1), 0)

    each = lambda f, *cols: [f(*xs) for xs in zip(*cols)]
    hd = C_HEAD_DIM
    nhead = st_ref.shape[0]
    slices = [pl.ds(pl.multiple_of((i + d * (nck - 1 - 2 * i)) * L, L), L) for i in range(nck)]
    where = [(sl, slice(h * hd, (h + 1) * hd)) for h in range(nhead) for sl in slices]
    q, k, v, lf = ([ref[sl, lanes] for sl, lanes in where] for ref in (q_ref, k_ref, v_ref, lf_ref))
    lam = each(lambda x: _mask_nn(before, x), lf)
    lam_end = each(lambda x: _end_row(x, d), lam)
    blocks = [[] for _ in where]
    for c in range(L // sub):
        lo, hi = c * sub, (c + 1) * sub
        upto = jnp.where(d == 1, lo - 1 - rows, rows - hi) < 0
        for s in range(len(where)):
            zero = jnp.zeros_like(lam_end[s])
            ref_f = lam[s][lo - 1:lo, :] if lo > 0 else zero
            ref_b = lam[s][hi:hi + 1, :] if hi < L else zero
            ref = jnp.where(d == 1, ref_b, ref_f)
            qc = q[s][lo:hi, :] * jnp.exp(lam[s][lo:hi, :] - ref)
            kc = k[s] * jnp.exp(jnp.where(upto, ref - lam[s], NEG_BIG))
            blocks[s].append(_nt(qc, kc))
    att = each(lambda bl: jnp.where(before, jnp.concatenate(bl, axis=0), 0.0), blocks)
    y_intra = each(_nn, att, v)
    q_in = each(lambda x, l: x * jnp.exp(l), q, lam)
    kv = each(lambda x, y, l, le: _tn(x, y * jnp.exp(le - l)), v, k, lam, lam_end)
    dec = each(jnp.exp, lam_end)
    for h in range(nhead):
        st = st_ref[h]
        for i in range(nck):
            s = h * nck + i
            y_ref[where[s][0], where[s][1]] = y_intra[s] + _nt(q_in[s], st)
            st = st * dec[s] + kv[s]
        st_ref[h] = st


def _gla_scan(q, v, k, lf, n_ctx):
    b, t, c = q.shape
    hd = C_HEAD_DIM
    L = GLA_BLOCK
    tb = SCAN_TIME_BLOCK
    nck, nb, ncb = tb // L, t // tb, n_ctx // tb
    nhead = GLA_HEADS_PER_STEP
    shared = pl.BlockSpec((None, tb, nhead * hd), lambda d, i, h, j: (i, _time_index(d, j, ncb, nb), h))
    per_dir = pl.BlockSpec((None, None, tb, nhead * hd), lambda d, i, h, j: (d, i, _time_index(d, j, ncb, nb), h))
    return pl.pallas_call(
        functools.partial(_gla_kernel, L=L, sub=GLA_CHUNK, nck=nck),
        grid=(2, b, c // (nhead * hd), nb),
        in_specs=[shared, shared, per_dir, per_dir],
        out_specs=per_dir,
        out_shape=jax.ShapeDtypeStruct((2, b, t, c), F32),
        scratch_shapes=[pltpu.VMEM((nhead, hd, hd), F32)],
        compiler_params=pltpu.CompilerParams(
            dimension_semantics=("arbitrary",) * 4,
            vmem_limit_bytes=VMEM_LIMIT_BYTES),
        name="gla_scan",
    )(q, v, k, lf)


def _rwkv_kernel(r_ref, k_ref, v_ref, a_ref, b_ref, lw_ref, y_ref, h_ref, *, L, nck):
    d = pl.program_id(0)
    sgn = 1 - 2 * d
    L2 = 2 * L
    W = 2 * D_HEAD_DIM

    @pl.when(pl.program_id(3) == 0)
    def _():
        h_ref[...] = jnp.zeros_like(h_ref)

    ri, ci = _iota2(L, L)
    before = (ci - ri) * sgn <= 0
    r2, c2 = _iota2(L2, L2)
    order2 = ((c2 & (L - 1)) - (r2 & (L - 1))) * sgn
    strict2 = order2 < 0
    incl2 = order2 <= 0
    eye2 = jnp.where(r2 == c2, 1.0, 0.0)
    rw, cw = _iota2(W, W)
    eye_w = rw == cw
    head0 = lax.broadcasted_iota(jnp.int32, (L, W), 1) < D_HEAD_DIM
    stack = lambda x: jnp.concatenate([jnp.where(head0, x, 0.0), jnp.where(head0, 0.0, x)], axis=0)
    n_levels = int(math.log2(L))

    each = lambda f, *cols: [f(*xs) for xs in zip(*cols)]
    npair = h_ref.shape[0]
    slices = [pl.ds(pl.multiple_of((i + d * (nck - 1 - 2 * i)) * L, L), L) for i in range(nck)]
    where = [(sl, slice(p * W, (p + 1) * W)) for p in range(npair) for sl in slices]
    r, k, v, a, b, lw = ([ref[sl, lanes] for sl, lanes in where]
                         for ref in (r_ref, k_ref, v_ref, a_ref, b_ref, lw_ref))
    cum = each(lambda x: _mask_nn(before, x), lw)
    cum_end = each(lambda c: _end_row(c, d), cum)
    e_neg = each(lambda c: jnp.exp(-c), cum)
    e_end = each(lambda ce, c: jnp.exp(ce - c), cum_end, cum)
    at = each(lambda x, c, w: stack(x * jnp.exp(c - w)), a, cum, lw)
    rt = each(lambda x, c: stack(x * jnp.exp(c)), r, cum)
    bt = each(lambda x, e: stack(x * e), b, e_neg)
    kt = each(lambda x, e: stack(x * e), k, e_neg)
    vs = each(stack, v)
    gram = each(lambda p, q, s, t: _nt(jnp.concatenate([p, q], axis=0), jnp.concatenate([s, t], axis=0)),
                at, rt, bt, kt)
    nmat = each(lambda g: jnp.where(strict2, g[:L2, :L2], 0.0), gram)
    a_k = each(lambda g: jnp.where(strict2, g[:L2, L2:], 0.0), gram)
    r_bk = each(lambda g: jnp.where(jnp.concatenate([incl2, incl2], axis=1), g[L2:, :], 0.0), gram)
    tinv = each(lambda n: eye2 + n, nmat)
    pw = each(lambda n: _nn(n, n), nmat)
    for lev in range(1, n_levels):
        if lev < n_levels - 1:
            both = each(lambda p, t: _nn(p, jnp.concatenate([p, t], axis=1)), pw, tinv)
            pw = each(lambda x: x[:, :L2], both)
            tinv = each(lambda t, x: t + x[:, L2:], tinv, both)
        else:
            tinv = each(lambda p, t: t + _nn(p, t), pw, tinv)
    akv = each(_nn, a_k, vs)
    wu = each(lambda t, p, q: _nn(t, jnp.concatenate([p, q], axis=1)), tinv, at, akv)
    zs = each(lambda x, y: jnp.concatenate([x, jnp.concatenate([jnp.zeros_like(y), y], axis=1)], axis=0), wu, vs)
    qy = each(_nn, r_bk, zs)
    md = each(lambda x, y, e, z: _tn(jnp.concatenate([stack(x * e), stack(y * e)], axis=0), z),
              b, k, e_end, zs)
    dec = each(lambda ce: jnp.sum(jnp.where(eye_w, jnp.broadcast_to(jnp.exp(ce), (W, W)), 0.0),
                                  axis=1, keepdims=True), cum_end)
    hs = [h_ref[p] for p in range(npair)]
    for i in range(nck):
        for p in range(npair):
            s = p * nck + i
            ys = _nn(rt[s] + qy[s][:, :W], hs[p]) + qy[s][:, W:]
            y_ref[where[s][0], where[s][1]] = ys[:L, :] + ys[L:, :]
            hs[p] = dec[s] * hs[p] + _nn(md[s][:, :W], hs[p]) + md[s][:, W:]
    for p in range(npair):
        h_ref[p] = hs[p]


def _rwkv_scan(r, k, v, a, b, lw, n_ctx):
    bsz, t, c = r.shape
    w = 2 * D_HEAD_DIM
    L = RWKV_CHUNK
    tb = SCAN_TIME_BLOCK
    nck, nb, ncb = tb // L, t // tb, n_ctx // tb
    npair = RWKV_PAIRS_PER_STEP
    shared = pl.BlockSpec((None, tb, npair * w), lambda d, i, p, j: (i, _time_index(d, j, ncb, nb), p))
    per_dir = pl.BlockSpec((None, None, tb, npair * w), lambda d, i, p, j: (d, i, _time_index(d, j, ncb, nb), p))
    return pl.pallas_call(
        functools.partial(_rwkv_kernel, L=L, nck=nck),
        grid=(2, bsz, c // (npair * w), nb),
        in_specs=[shared] * 5 + [per_dir],
        out_specs=per_dir,
        out_shape=jax.ShapeDtypeStruct((2, bsz, t, c), F32),
        scratch_shapes=[pltpu.VMEM((npair, w, w), F32)],
        compiler_params=pltpu.CompilerParams(
            dimension_semantics=("arbitrary",) * 4,
            vmem_limit_bytes=VMEM_LIMIT_BYTES),
        name="rwkv7_scan",
    )(r, k, v, a, b, lw)


def _neighbours(x, n_ctx):
    pos = jnp.arange(x.shape[1])[None, :, None]
    prev = jnp.pad(x[:, :-1], ((0, 0), (1, 0), (0, 0)))
    nxt = jnp.pad(x[:, 1:], ((0, 0), (0, 1), (0, 0)))
    return jnp.where(pos == n_ctx, 0.0, prev), jnp.where(pos == n_ctx - 1, 0.0, nxt)


def _dwconv3(x, w, b, n_ctx):
    prev, nxt = _neighbours(x, n_ctx)
    return prev * w[0] + x * w[1] + nxt * w[2] + b


def _layer_norm(x, g, b):
    mu = x.mean(-1, keepdims=True)
    var = jnp.mean(jnp.square(x - mu), -1, keepdims=True)
    return (x - mu) * lax.rsqrt(var + LN_EPS) * g + b


def _head_norm(y, eps, center):
    if center:
        y = y - y.mean(-1, keepdims=True)
    y = y * lax.rsqrt(jnp.mean(y * y, -1, keepdims=True) + eps)
    return y.reshape(y.shape[0], y.shape[1], -1)


def _ssd_mixer(p, n_ctx, params):
    conv_w, conv_b, dt_bias, a_log, d_skip, norm_w = params
    b, t, _ = p.shape
    a = -jnp.exp(a_log)
    z = p[..., :A_INNER]
    xbc = jax.nn.silu(_dwconv3(p[..., A_INNER:A_INNER + A_XBC], conv_w, conv_b, n_ctx))
    xs = xbc[..., :A_INNER].reshape(b, t, A_HEADS, A_HEAD_DIM)
    bm = xbc[..., A_INNER:A_INNER + A_GROUPS * A_STATE]
    cm = xbc[..., A_INNER + A_GROUPS * A_STATE:]
    dt = jnp.moveaxis(jax.nn.softplus(p[..., A_INNER + A_XBC:].reshape(b, t, 2, A_HEADS) + dt_bias), 2, 0)
    x = (xs[None] * dt[..., None]).reshape(2, b, t, A_INNER)
    y = _ssd_scan(x, dt * a[:, None, None, :], bm, cm, n_ctx)
    y = ((y[0] + y[1]).reshape(b, t, A_HEADS, A_HEAD_DIM) + d_skip[:, None] * xs).reshape(b, t, A_INNER) * jax.nn.silu(z)
    return y * lax.rsqrt(jnp.mean(y * y, -1, keepdims=True) + 1e-6) * norm_w


def _mlstm_mixer(p, n_ctx, params):
    conv_w, conv_b, i_bias, f_bias, norm_w = params
    b, t, _ = p.shape
    qk = jax.nn.silu(_dwconv3(p[..., :2 * B_QK], conv_w, conv_b, n_ctx))
    v = p[..., 2 * B_QK:2 * B_QK + B_INNER]
    o = p[..., 2 * B_QK + B_INNER:2 * B_QK + 2 * B_INNER]
    gates = p[..., 2 * B_QK + 2 * B_INNER:].reshape(b, t, 2, 2, B_HEADS)
    log_i = gates[:, :, 0] + i_bias
    log_f = jax.nn.log_sigmoid(gates[:, :, 1] + f_bias)
    g = jnp.moveaxis(jnp.concatenate([log_i, log_f], axis=-1), 2, 0)
    h = _mlstm_scan(qk[..., :B_QK], qk[..., B_QK:], v, g, n_ctx)
    h = (h[0] + h[1]).reshape(b, t, B_HEADS, B_V_DIM)
    return jax.nn.sigmoid(o) * _head_norm(h, MLSTM_EPS, True) * norm_w


def _hgrn2_mixer(p, n_ctx, lb, params):
    f_bias, norm_w = params
    b, t, _ = p.shape
    q = jax.nn.silu(p[..., :C_INNER])
    f_pre = jnp.moveaxis(p[..., C_INNER:3 * C_INNER].reshape(b, t, 2, C_INNER), 2, 0) + f_bias[:, None, None, :]
    log_f = jnp.log(lb + (1 - lb) * jax.nn.sigmoid(f_pre))
    k = (1 - lb) * jax.nn.sigmoid(-f_pre)
    o = _gla_scan(q, p[..., 3 * C_INNER:4 * C_INNER], k, log_f, n_ctx)
    o = (o[0] + o[1]).reshape(b, t, C_HEADS, C_HEAD_DIM)
    return _head_norm(o, 1e-6, False) * norm_w * jax.nn.silu(p[..., 4 * C_INNER:])


def _rwkv7_mixer(p, n_ctx, params):
    mu, w0, w2, a0, a2, g2, k_k, k_a, r_k, ln_w, ln_b = params
    b, t, _ = p.shape
    hd = lambda u: u.reshape(b, t, D_HEADS, D_HEAD_DIM)
    prev, nxt = _neighbours(p, n_ctx)
    p = p + mu * (0.5 * (prev + nxt) - p)
    r = p[..., :D_INNER]
    k = p[..., D_INNER:2 * D_INNER]
    v = p[..., 2 * D_INNER:3 * D_INNER]
    o = 3 * D_INNER
    wl = jnp.tanh(p[..., o:o + 2 * D_W_LORA]).reshape(b * t, 2, D_W_LORA)
    al = p[..., o + 2 * D_W_LORA:o + 2 * D_W_LORA + D_A_LORA]
    gl = p[..., o + 2 * D_W_LORA + D_A_LORA:]
    w = jnp.stack([_matmul(wl[:, d], w2[d]).reshape(b, t, D_INNER) + w0[d] for d in range(2)])
    log_decay = -jnp.exp(-jax.nn.softplus(-w) - 0.5)
    a = jax.nn.sigmoid(a0 + _matmul(al.reshape(b * t, D_A_LORA), a2).reshape(b, t, D_INNER))
    g = _matmul(jax.nn.sigmoid(gl).reshape(b * t, D_G_LORA), g2).reshape(b, t, D_INNER)
    kk = hd(k * k_k)
    kk = (kk * lax.rsqrt(jnp.maximum(jnp.sum(kk * kk, -1, keepdims=True), 1e-12))).reshape(b, t, D_INNER)
    k = k * (1 + (a - 1) * k_a)
    y = _rwkv_scan(r, k, v, -kk, kk * a, log_decay, n_ctx)
    y = hd(y[0] + y[1])
    bonus = (jnp.sum(hd(r) * hd(k) * r_k, axis=-1, keepdims=True) * hd(v)).reshape(b, t, D_INNER)
    return (_head_norm(y, RWKV_EPS, True) * ln_w + ln_b + bonus) * g


def _to_col_major(u, rows):
    b, s, d = u.shape
    return u.reshape(b, rows, GRID_W, d).transpose(0, 2, 1, 3).reshape(b, s, d)


def _from_col_major(u, rows):
    b, s, d = u.shape
    return u.reshape(b, GRID_W, rows, d).transpose(0, 2, 1, 3).reshape(b, s, d)


def _proj(u, w):
    b, t, d = u.shape
    return _matmul(u.reshape(b * t, d), w).reshape(b, t, w.shape[1])


def kernel(x, c, ctx, c_ctx, mod_w, mod_b, ln_g, ln_b, ev_w_in, ev_w_out, ssd_conv_w, ssd_conv_b, ssd_dt_bias, ssd_a_log, ssd_d, ssd_norm_w, mlstm_conv_w, mlstm_conv_b, mlstm_i_bias, mlstm_f_bias, mlstm_norm_w, od_w_in, od_w_out, hgrn_lb_logits, hgrn_f_bias, hgrn_norm_w, rwkv_mu, rwkv_w0, rwkv_w2, rwkv_a0, rwkv_a2, rwkv_g2, rwkv_k_k, rwkv_k_a, rwkv_r_k, rwkv_ln_w, rwkv_ln_b, router_w, router_bias, exp_w_gate, exp_w_up, exp_w_down):
    bsz, seq, _ = x.shape
    n_ctx = ctx.shape[1]
    rows = seq // GRID_W
    lb_all = jnp.cumsum(jax.nn.softmax(hgrn_lb_logits.astype(F32), axis=0), axis=0)
    lb_all = lb_all - lb_all[0]
    s_c = jax.nn.silu(c)
    s_cc = jax.nn.silu(c_ctx)
    xa = jnp.concatenate([ctx, x], axis=1)
    is_ctx = (jnp.arange(n_ctx + seq) < n_ctx)[None, :, None]
    for layer in range(DEPTH):
        i = layer // 2
        mods = _matmul(jnp.concatenate([s_c, s_cc[None]], axis=0), mod_w[layer], tm=8, tn=512) + mod_b[layer]
        sh1, sc1, g1, sh2, sc2, g2 = [jnp.where(is_ctx, mc[None, None, :], ml[:, None, :]) for ml, mc in
                                      zip(jnp.split(mods[:bsz], 6, axis=-1), jnp.split(mods[bsz], 6, axis=-1))]
        u = xa * (1 + sc1) + sh1
        if layer % 2 == 0:
            p = _proj(u, ev_w_in[i])
            feat = jnp.concatenate([
                _ssd_mixer(p[..., :P_A], n_ctx,
                           (ssd_conv_w[i], ssd_conv_b[i], ssd_dt_bias[i], ssd_a_log[i], ssd_d[i], ssd_norm_w[i])),
                _mlstm_mixer(p[..., P_A:], n_ctx,
                             (mlstm_conv_w[i], mlstm_conv_b[i], mlstm_i_bias[i], mlstm_f_bias[i], mlstm_norm_w[i]))],
                axis=-1)
            w_out = ev_w_out[i]
        else:
            u = jnp.concatenate([u[:, :n_ctx], _to_col_major(u[:, n_ctx:], rows)], axis=1)
            p = _proj(u, od_w_in[i])
            feat = jnp.concatenate([
                _hgrn2_mixer(p[..., :P_C], n_ctx, lb_all[layer], (hgrn_f_bias[i], hgrn_norm_w[i])),
                _rwkv7_mixer(p[..., P_C:], n_ctx,
                             (rwkv_mu[i], rwkv_w0[i], rwkv_w2[i], rwkv_a0[i], rwkv_a2[i], rwkv_g2[i],
                              rwkv_k_k[i], rwkv_k_a[i], rwkv_r_k[i], rwkv_ln_w[i], rwkv_ln_b[i]))], axis=-1)
            feat = jnp.concatenate([feat[:, :n_ctx], _from_col_major(feat[:, n_ctx:], rows)], axis=1)
            w_out = od_w_out[i]
        xa = _layer_norm(DEEPNORM_ALPHA * xa + g1 * _proj(feat, w_out), ln_g[layer, 0], ln_b[layer, 0])
        h = xa * (1 + sc2) + sh2
        f = _moe_ffn(h.reshape(-1, D_MODEL), router_w, router_bias,
                     exp_w_gate[layer], exp_w_up[layer], exp_w_down[layer])
        xa = _layer_norm(DEEPNORM_ALPHA * xa + g2 * f.reshape(xa.shape), ln_g[layer, 1], ln_b[layer, 1])
    return xa[:, n_ctx:]
```

```python
import functools
import math

import jax
import jax.numpy as jnp
from jax import lax
from jax.experimental import pallas as pl
from jax.experimental.pallas import tpu as pltpu

F32 = jnp.float32
BF16 = jnp.bfloat16

D_MODEL = 1024
DEPTH = 4
GRID_W = 64
A_HEADS = 8
A_HEAD_DIM = 64
A_INNER = A_HEADS * A_HEAD_DIM
A_GROUPS = 2
A_STATE = 64
A_XBC = A_INNER + 2 * A_GROUPS * A_STATE
B_HEADS = 4
B_QK_DIM = 64
B_V_DIM = 128
B_QK = B_HEADS * B_QK_DIM
B_INNER = B_HEADS * B_V_DIM
MLSTM_EPS = 1e-6
C_HEADS = 4
C_HEAD_DIM = 128
C_INNER = C_HEADS * C_HEAD_DIM
D_HEADS = 8
D_HEAD_DIM = 64
D_INNER = D_HEADS * D_HEAD_DIM
D_W_LORA = 64
D_A_LORA = 64
D_G_LORA = 128
RWKV_EPS = 64e-5
P_A = A_INNER + A_XBC + 2 * A_HEADS
P_B = 2 * B_QK + 2 * B_INNER + 4 * B_HEADS
P_C = 5 * C_INNER
P_D = 3 * D_INNER + 2 * D_W_LORA + D_A_LORA + D_G_LORA
N_EXPERTS = 32
N_EXPERT_GROUPS = 8
EXPERTS_PER_GROUP = N_EXPERTS // N_EXPERT_GROUPS
TOP_K = 2
D_EXPERT = 512
MOE_BLOCK = 256
RANK_BLOCK = 512
ROW_TILE = 256
LANES = 128
DEEPNORM_ALPHA = (2 * DEPTH) ** 0.25
LN_EPS = 1e-5
M_INIT = -1e30
NEG_BIG = -1e30

SSD_CHUNK = 128
MLSTM_CHUNK = 128
GLA_CHUNK = 16
GLA_BLOCK = 64
GLA_HEADS_PER_STEP = 4
RWKV_CHUNK = 64
RWKV_PAIRS_PER_STEP = 4
SCAN_TIME_BLOCK = 256

VMEM_LIMIT_BYTES = 48 * 1024 * 1024
HI = lax.Precision.HIGHEST


def _dot(a, b, dims, exact):
    if exact:
        return lax.dot_general(a.astype(F32), b.astype(F32), (dims, ((), ())),
                               precision=HI, preferred_element_type=F32)
    return lax.dot_general(a.astype(BF16), b.astype(BF16), (dims, ((), ())),
                           preferred_element_type=F32)


def _nn(a, b, exact=False):
    return _dot(a, b, ((1,), (0,)), exact)


def _nt(a, b, exact=False):
    return _dot(a, b, ((1,), (1,)), exact)


def _tn(a, b, exact=False):
    return _dot(a, b, ((0,), (0,)), exact)


def _iota2(n, m):
    return (lax.broadcasted_iota(jnp.int32, (n, m), 0),
            lax.broadcasted_iota(jnp.int32, (n, m), 1))


def _split3(x):
    x1 = x.astype(BF16)
    r1 = x - x1.astype(F32)
    x2 = r1.astype(BF16)
    x3 = (r1 - x2.astype(F32)).astype(BF16)
    return x1, x2, x3


def _mask_nn(mask, x):
    mb = mask.astype(BF16)
    x1, x2, x3 = _split3(x)
    return _nn(mb, x1, False) + _nn(mb, x2, False) + _nn(mb, x3, False)


def _nn_mask(x, mask):
    mb = mask.astype(BF16)
    x1, x2, x3 = _split3(x)
    return _nn(x1, mb, False) + _nn(x2, mb, False) + _nn(x3, mb, False)


def _cum_mats(col, row, L):
    ri, ci = _iota2(L, L)
    ccol = _mask_nn(ci <= ri, jnp.broadcast_to(col, (L, L)))
    crow = _nn_mask(jnp.broadcast_to(row, (L, L)), ri <= ci)
    return ccol, crow


def _mm_kernel(x_ref, w_ref, o_ref, *, exact):
    o_ref[...] = _nn(x_ref[...], w_ref[...], exact)


def _matmul(x, w, tm=512, tn=512, exact=False):
    m, k = x.shape
    n = w.shape[1]
    n_pad = -(-n // tn) * tn
    m_pad = -(-m // tm) * tm
    xb = x if exact else x.astype(BF16)
    wb = w if exact else w.astype(BF16)
    if n_pad != n:
        wb = jnp.pad(wb, ((0, 0), (0, n_pad - n)))
    if m_pad != m:
        xb = jnp.pad(xb, ((0, m_pad - m), (0, 0)))
    out = pl.pallas_call(
        functools.partial(_mm_kernel, exact=exact),
        grid=(n_pad // tn, m_pad // tm),
        in_specs=[pl.BlockSpec((tm, k), lambda j, i: (i, 0)),
                  pl.BlockSpec((k, tn), lambda j, i: (0, j))],
        out_specs=pl.BlockSpec((tm, tn), lambda j, i: (i, j)),
        out_shape=jax.ShapeDtypeStruct((m_pad, n_pad), F32),
        compiler_params=pltpu.CompilerParams(
            dimension_semantics=("arbitrary", "arbitrary"),
            vmem_limit_bytes=VMEM_LIMIT_BYTES),
        name="dense_matmul",
    )(xb, wb)
    return out[:m, :n]


def _moe_kernel(blk_e_ref, n_used_ref, x_ref, wg_ref, wu_ref, wd_ref, o_ref):
    i = pl.program_id(0)

    @pl.when(i < n_used_ref[0])
    def _():
        x = x_ref[...]
        g = jnp.dot(x, wg_ref[...], preferred_element_type=F32)
        u = jnp.dot(x, wu_ref[...], preferred_element_type=F32)
        hmid = (g * jax.nn.sigmoid(g) * u).astype(BF16)
        o_ref[...] = jnp.dot(hmid, wd_ref[...], preferred_element_type=F32)

    @pl.when(i >= n_used_ref[0])
    def _():
        o_ref[...] = jnp.zeros_like(o_ref)


def _moe_experts(xp, blk_e, n_used, w_gate, w_up, w_down):
    n_rows, d = xp.shape
    n_blocks = n_rows // MOE_BLOCK
    grid_spec = pltpu.PrefetchScalarGridSpec(
        num_scalar_prefetch=2,
        grid=(n_blocks,),
        in_specs=[
            pl.BlockSpec((MOE_BLOCK, d), lambda i, be, nu: (i, 0)),
            pl.BlockSpec((None, d, D_EXPERT), lambda i, be, nu: (be[i], 0, 0)),
            pl.BlockSpec((None, d, D_EXPERT), lambda i, be, nu: (be[i], 0, 0)),
            pl.BlockSpec((None, D_EXPERT, d), lambda i, be, nu: (be[i], 0, 0)),
        ],
        out_specs=pl.BlockSpec((MOE_BLOCK, d), lambda i, be, nu: (i, 0)),
    )
    return pl.pallas_call(
        _moe_kernel,
        grid_spec=grid_spec,
        out_shape=jax.ShapeDtypeStruct((n_rows, d), F32),
        compiler_params=pltpu.CompilerParams(
            dimension_semantics=("arbitrary",),
            vmem_limit_bytes=VMEM_LIMIT_BYTES),
        name="moe_experts",
    )(blk_e, n_used, xp, w_gate, w_up, w_down)


def _top2(vals):
    m = len(vals)
    m1 = functools.reduce(jnp.maximum, vals)
    i1 = jnp.full_like(m1, float(m - 1))
    for j in reversed(range(m - 1)):
        i1 = jnp.where(vals[j] == m1, float(j), i1)
    rest = [jnp.where(i1 == float(j), -jnp.inf, vals[j]) for j in range(m)]
    m2 = functools.reduce(jnp.maximum, rest)
    i2 = jnp.full_like(m1, float(m - 1))
    for j in reversed(range(m - 1)):
        i2 = jnp.where(rest[j] == m2, float(j), i2)
    return m1, i1, m2, i2


def _router_kernel(h_ref, wt_ref, bias_ref, e_ref, w_ref, rank_ref, cnt_ref, carry_ref):
    tm = h_ref.shape[0]
    ng, per = N_EXPERT_GROUPS, EXPERTS_PER_GROUP

    @pl.when(pl.program_id(0) == 0)
    def _():
        carry_ref[...] = jnp.zeros_like(carry_ref)

    s = jax.nn.sigmoid(_nt(wt_ref[...], h_ref[...], True))
    sb = s + bias_ref[...]
    biased = [sb[j * ng:(j + 1) * ng, :] for j in range(per)]
    plain = [s[j * ng:(j + 1) * ng, :] for j in range(per)]
    m1, _, m2, _ = _top2(biased)
    gsum = m1 + m2
    rows = lax.broadcasted_iota(jnp.int32, (ng, tm), 0).astype(F32)
    gmax = jnp.max(gsum, axis=0, keepdims=True)
    gi = jnp.min(jnp.where(gsum == gmax, rows, float(ng)), axis=0, keepdims=True)
    sel = rows == gi
    pick = lambda v: jnp.sum(jnp.where(sel, v, 0.0), axis=0, keepdims=True)
    in_b = [pick(v) for v in biased]
    in_s = [pick(v) for v in plain]
    _, l1, _, l2 = _top2(in_b)
    w1 = functools.reduce(jnp.add, [jnp.where(l1 == float(j), in_s[j], 0.0) for j in range(per)])
    w2 = functools.reduce(jnp.add, [jnp.where(l2 == float(j), in_s[j], 0.0) for j in range(per)])
    e1 = gi * float(per) + l1
    e2 = gi * float(per) + l2
    wsum = w1 + w2
    e_ref[0:1, :] = e1.astype(jnp.int32)
    e_ref[1:2, :] = e2.astype(jnp.int32)
    w_ref[0:1, :] = w1 / wsum
    w_ref[1:2, :] = w2 / wsum
    row = lax.broadcasted_iota(jnp.int32, (N_EXPERTS, tm), 0)
    experts = ((row % ng) * per + row // ng).astype(F32)
    oh1 = jnp.where(experts == e1, 1.0, 0.0)
    oh2 = jnp.where(experts == e2, 1.0, 0.0)
    oh = oh1 + oh2
    ri, ci = _iota2(tm, tm)
    seen = _nn(oh, jnp.where(ri < ci, 1.0, 0.0)) + carry_ref[...]
    rank_ref[0:1, :] = jnp.sum(oh1 * seen, axis=0, keepdims=True).astype(jnp.int32)
    rank_ref[1:2, :] = jnp.sum(oh2 * seen, axis=0, keepdims=True).astype(jnp.int32)
    carry = carry_ref[...] + jnp.sum(oh, axis=1, keepdims=True)
    carry_ref[...] = carry
    cnt_ref[...] = carry.astype(jnp.int32)


def _route(h, router_w, router_bias):
    t, d = h.shape
    tm = RANK_BLOCK
    kt = lambda dt: jax.ShapeDtypeStruct((TOP_K, t), dt)
    blk = pl.BlockSpec((TOP_K, tm), lambda i: (0, i))
    member_major = lambda a: a.reshape(N_EXPERT_GROUPS, EXPERTS_PER_GROUP, -1).transpose(1, 0, 2).reshape(N_EXPERTS, -1)
    e, w, rank, counts = pl.pallas_call(
        _router_kernel,
        grid=(t // tm,),
        in_specs=[pl.BlockSpec((tm, d), lambda i: (i, 0)),
                  pl.BlockSpec((N_EXPERTS, d), lambda i: (0, 0)),
                  pl.BlockSpec((N_EXPERTS, 1), lambda i: (0, 0))],
        out_specs=[blk, blk, blk, pl.BlockSpec((N_EXPERTS, 1), lambda i: (0, 0))],
        out_shape=[kt(jnp.int32), kt(F32), kt(jnp.int32), jax.ShapeDtypeStruct((N_EXPERTS, 1), jnp.int32)],
        scratch_shapes=[pltpu.VMEM((N_EXPERTS, 1), F32)],
        compiler_params=pltpu.CompilerParams(dimension_semantics=("arbitrary",),
                                             vmem_limit_bytes=VMEM_LIMIT_BYTES),
        name="moe_router",
    )(h, member_major(router_w.T), member_major(router_bias.astype(F32).reshape(N_EXPERTS, 1)))
    counts = counts.reshape(EXPERTS_PER_GROUP, N_EXPERT_GROUPS).T.reshape(N_EXPERTS)
    return e, w, rank, counts


def _moe_ffn(h, hb, router_w, router_bias, w_gate, w_up, w_down):
    t, d = h.shape
    expert, wts, rank, counts = _route(h, router_w, router_bias)
    n_assign = t * TOP_K
    padded = (counts + MOE_BLOCK - 1) // MOE_BLOCK * MOE_BLOCK
    pends = jnp.cumsum(padded)
    pstarts = pends - padded
    dest = pstarts[expert] + rank
    n_blocks = -(-n_assign // MOE_BLOCK) + N_EXPERTS
    slot_token = jnp.zeros((n_blocks * MOE_BLOCK,), jnp.int32)
    for kk in range(TOP_K):
        slot_token = slot_token.at[dest[kk]].set(jnp.arange(t, dtype=jnp.int32), unique_indices=True)
    xp = hb[slot_token]
    blk_start = jnp.arange(n_blocks, dtype=jnp.int32) * MOE_BLOCK
    blk_e = jnp.minimum(jnp.sum(pends[None, :] <= blk_start[:, None], axis=1), N_EXPERTS - 1).astype(jnp.int32)
    n_used = (pends[-1] // MOE_BLOCK).astype(jnp.int32).reshape(1)
    yp = _moe_experts(xp, blk_e, n_used, w_gate.astype(BF16), w_up.astype(BF16), w_down.astype(BF16))
    out = yp[dest[0]] * wts[0][:, None]
    for kk in range(1, TOP_K):
        out = out + yp[dest[kk]] * wts[kk][:, None]
    return out


def _ssd_kernel(x_ref, lac_ref, lar_ref, b_ref, c_ref, y_ref, st_ref, *, L):
    d = pl.program_id(0)
    sgn = 1 - 2 * d
    hp, gn = A_INNER, A_GROUPS * A_STATE
    hpg = A_HEADS // A_GROUPS

    @pl.when(pl.program_id(2) == 0)
    def _():
        st_ref[...] = jnp.zeros_like(st_ref)

    ri, ci = _iota2(L, L)
    before = (ci - ri) * sgn <= 0
    before_t = (ri - ci) * sgn <= 0
    x = x_ref[...]
    bm = b_ref[...]
    cm = c_ref[...]
    ccol = _mask_nn(before, lac_ref[...])
    crow = _nn_mask(lar_ref[...], before_t)
    head_of_lane = lax.broadcasted_iota(jnp.int32, (A_HEADS, hp), 1) // A_HEAD_DIM
    expand = head_of_lane == lax.broadcasted_iota(jnp.int32, (A_HEADS, hp), 0)
    cum = _nn_mask(ccol, expand)
    end = _end_row(cum, d)
    group_of_lane = lax.broadcasted_iota(jnp.int32, (L, gn), 1) // A_STATE
    first_of_pair = (lax.broadcasted_iota(jnp.int32, (L, 2 * A_HEAD_DIM), 1) < A_HEAD_DIM)
    cbs = [_nt(jnp.where(group_of_lane == g, cm, 0.0), bm) for g in range(A_GROUPS)]
    pairs = []
    for p in range(A_HEADS // 2):
        xp = x[:, 2 * p * A_HEAD_DIM:(2 * p + 2) * A_HEAD_DIM]
        ys = []
        for h in (2 * p, 2 * p + 1):
            decay = jnp.exp(jnp.where(before, ccol[:, h:h + 1] - crow[h:h + 1, :], NEG_BIG))
            ys.append(_nn(cbs[h // hpg] * decay, xp))
        pairs.append(jnp.where(first_of_pair, ys[0], ys[1]))
    st = st_ref[...]
    y_ref[...] = jnp.concatenate(pairs, axis=1) + jnp.exp(cum) * _nn(cm, st)
    own_group = (lax.broadcasted_iota(jnp.int32, (gn, hp), 0) // A_STATE
                 == lax.broadcasted_iota(jnp.int32, (gn, hp), 1) // (A_HEAD_DIM * hpg))
    st_ref[...] = jnp.exp(end) * st + jnp.where(own_group, _tn(bm, x * jnp.exp(end - cum)), 0.0)


def _ssd_scan(x, la, bm, cm, n_ctx):
    _, b, t, hp = x.shape
    L = SSD_CHUNK
    nb, ncb = t // L, n_ctx // L
    tix = lambda d, j: _time_index(d, j, ncb, nb)
    shared = pl.BlockSpec((None, L, bm.shape[-1]), lambda d, i, j: (i, tix(d, j), 0))
    return pl.pallas_call(
        functools.partial(_ssd_kernel, L=L),
        grid=(2, b, nb),
        in_specs=[pl.BlockSpec((None, None, L, hp), lambda d, i, j: (d, i, tix(d, j), 0)),
                  pl.BlockSpec((None, None, L, A_HEADS), lambda d, i, j: (d, i, tix(d, j), 0)),
                  pl.BlockSpec((None, None, A_HEADS, L), lambda d, i, j: (d, i, 0, tix(d, j))),
                  shared, shared],
        out_specs=pl.BlockSpec((None, None, L, hp), lambda d, i, j: (d, i, tix(d, j), 0)),
        out_shape=jax.ShapeDtypeStruct((2, b, t, hp), F32),
        scratch_shapes=[pltpu.VMEM((bm.shape[-1], hp), F32)],
        compiler_params=pltpu.CompilerParams(
            dimension_semantics=("arbitrary",) * 3,
            vmem_limit_bytes=VMEM_LIMIT_BYTES),
        name="ssd_scan",
    )(x, la, jnp.swapaxes(la, 2, 3), bm, cm)


def _mlstm_kernel(q_ref, k_ref, v_ref, gc_ref, gr_ref, h_ref, c_ref, n_ref, m_ref, *, L):
    d = pl.program_id(0)
    sgn = 1 - 2 * d
    nh, dk, dv = B_HEADS, B_QK_DIM, B_V_DIM

    @pl.when(pl.program_id(2) == 0)
    def _():
        c_ref[...] = jnp.zeros_like(c_ref)
        n_ref[...] = jnp.zeros_like(n_ref)
        m_ref[...] = jnp.full_like(m_ref, M_INIT)

    ri, ci = _iota2(L, L)
    before = (ci - ri) * sgn <= 0
    before_t = (ri - ci) * sgn <= 0
    gc = gc_ref[...]
    gr = gr_ref[...]
    fcol = _mask_nn(before, gc[:, nh:])
    frow = _nn_mask(gr[nh:, :], before_t)
    lane_head = lax.broadcasted_iota(jnp.int32, (L, 2 * dk), 1) // dk
    for h in range(nh):
        slab = slice((h // 2) * 2 * dk, (h // 2 + 1) * 2 * dk)
        q = jnp.where(lane_head == h % 2, q_ref[:, slab], 0.0) * (dk ** -0.5)
        k = k_ref[:, slab]
        v = v_ref[:, h * dv:(h + 1) * dv]
        li_c = gc[:, h:h + 1]
        li_r = gr[h:h + 1, :]
        f_c = fcol[:, h:h + 1]
        ftot = _end_row(f_c, d)
        c_prev = c_ref[h]
        n_prev = n_ref[h]
        m_prev = m_ref[h]
        w_end = ftot - f_c + li_c
        m_loc = jnp.max(w_end, axis=0, keepdims=True)
        ke = k * jnp.exp(w_end - m_loc)
        c_loc = _tn(ke, v)
        n_loc = jnp.sum(ke, axis=0, keepdims=True)
        log_d = jnp.where(before, f_c - frow[h:h + 1, :] + li_r, NEG_BIG)
        log_inter = f_c + m_prev
        m_row = jnp.maximum(jnp.max(log_d, axis=-1, keepdims=True), log_inter)
        s = _nt(q, k) * jnp.exp(log_d - m_row)
        inter = jnp.exp(log_inter - m_row)
        num = _nn(s, v) + inter * _nn(q, c_prev)
        den = jnp.sum(s, axis=-1, keepdims=True) + inter * jnp.sum(q * n_prev, axis=-1, keepdims=True)
        h_ref[:, h * dv:(h + 1) * dv] = num / jnp.maximum(jnp.abs(den), jnp.exp(-m_row))
        m_new = jnp.maximum(ftot + m_prev, m_loc)
        sp = jnp.exp(ftot + m_prev - m_new)
        sc = jnp.exp(m_loc - m_new)
        c_ref[h] = sp * c_prev + sc * c_loc
        n_ref[h] = sp * n_prev + sc * n_loc
        m_ref[h] = m_new


def _mlstm_scan(q, k, v, gates, n_ctx):
    b, t, _ = q.shape
    L = MLSTM_CHUNK
    nb, ncb = t // L, n_ctx // L
    tix = lambda d, j: _time_index(d, j, ncb, nb)
    shared = lambda c: pl.BlockSpec((None, L, c), lambda d, i, j: (i, tix(d, j), 0))
    return pl.pallas_call(
        functools.partial(_mlstm_kernel, L=L),
        grid=(2, b, nb),
        in_specs=[shared(B_QK), shared(B_QK), shared(B_INNER),
                  pl.BlockSpec((None, None, L, 2 * B_HEADS), lambda d, i, j: (d, i, tix(d, j), 0)),
                  pl.BlockSpec((None, None, 2 * B_HEADS, L), lambda d, i, j: (d, i, 0, tix(d, j)))],
        out_specs=pl.BlockSpec((None, None, L, B_INNER), lambda d, i, j: (d, i, tix(d, j), 0)),
        out_shape=jax.ShapeDtypeStruct((2, b, t, B_INNER), F32),
        scratch_shapes=[pltpu.VMEM((B_HEADS, 2 * B_QK_DIM, B_V_DIM), F32),
                        pltpu.VMEM((B_HEADS, 1, 2 * B_QK_DIM), F32),
                        pltpu.VMEM((B_HEADS, 1, 1), F32)],
        compiler_params=pltpu.CompilerParams(
            dimension_semantics=("arbitrary",) * 3,
            vmem_limit_bytes=VMEM_LIMIT_BYTES),
        name="mlstm_scan",
    )(q, k, v, gates, jnp.swapaxes(gates, 2, 3))


def _time_index(d, j, n_ctx_blocks, n_blocks):
    rev = jnp.where(j < n_ctx_blocks, n_ctx_blocks - 1 - j, n_blocks - 1 - j + n_ctx_blocks)
    return jnp.where(d == 1, rev, j)


def _end_row(x, d):
    n = x.shape[0]
    return jnp.where(d == 1, x[0:1, :], x[n - 1:n, :])


def _gla_kernel(q_ref, v_ref, k_ref, lf_ref, y_ref, st_ref, *, L, sub, nck):
    d = pl.program_id(0)
    sgn = 1 - 2 * d

    @pl.when(pl.program_id(3) == 0)
    def _():
        st_ref[...] = jnp.zeros_like(st_ref)

    ri, ci = _iota2(L, L)
    before = (ci - ri) * sgn <= 0
    rows = lax.broadcasted_iota(jnp.int32, (L, 1), 0)

    each = lambda f, *cols: [f(*xs) for xs in zip(*cols)]
    hd = C_HEAD_DIM
    nhead = st_ref.shape[0]
    slices = [pl.ds(pl.multiple_of((i + d * (nck - 1 - 2 * i)) * L, L), L) for i in range(nck)]
    where = [(sl, slice(h * hd, (h + 1) * hd)) for h in range(nhead) for sl in slices]
    q, k, v, lf = ([ref[sl, lanes] for sl, lanes in where] for ref in (q_ref, k_ref, v_ref, lf_ref))
    lam = each(lambda x: _mask_nn(before, x), lf)
    lam_end = each(lambda x: _end_row(x, d), lam)
    blocks = [[] for _ in where]
    for c in range(L // sub):
        lo, hi = c * sub, (c + 1) * sub
        upto = jnp.where(d == 1, lo - 1 - rows, rows - hi) < 0
        for s in range(len(where)):
            zero = jnp.zeros_like(lam_end[s])
            ref_f = lam[s][lo - 1:lo, :] if lo > 0 else zero
            ref_b = lam[s][hi:hi + 1, :] if hi < L else zero
            ref = jnp.where(d == 1, ref_b, ref_f)
            qc = q[s][lo:hi, :] * jnp.exp(lam[s][lo:hi, :] - ref)
            kc = k[s] * jnp.exp(jnp.where(upto, ref - lam[s], NEG_BIG))
            blocks[s].append(_nt(qc, kc))
    att = each(lambda bl: jnp.where(before, jnp.concatenate(bl, axis=0), 0.0), blocks)
    y_intra = each(_nn, att, v)
    q_in = each(lambda x, l: x * jnp.exp(l), q, lam)
    kv = each(lambda x, y, l, le: _tn(x, y * jnp.exp(le - l)), v, k, lam, lam_end)
    dec = each(jnp.exp, lam_end)
    for h in range(nhead):
        st = st_ref[h]
        for i in range(nck):
            s = h * nck + i
            y_ref[where[s][0], where[s][1]] = y_intra[s] + _nt(q_in[s], st)
            st = st * dec[s] + kv[s]
        st_ref[h] = st


def _gla_scan(q, v, k, lf, n_ctx):
    b, t, c = q.shape
    hd = C_HEAD_DIM
    L = GLA_BLOCK
    tb = SCAN_TIME_BLOCK
    nck, nb, ncb = tb // L, t // tb, n_ctx // tb
    nhead = GLA_HEADS_PER_STEP
    shared = pl.BlockSpec((None, tb, nhead * hd), lambda d, i, h, j: (i, _time_index(d, j, ncb, nb), h))
    per_dir = pl.BlockSpec((None, None, tb, nhead * hd), lambda d, i, h, j: (d, i, _time_index(d, j, ncb, nb), h))
    return pl.pallas_call(
        functools.partial(_gla_kernel, L=L, sub=GLA_CHUNK, nck=nck),
        grid=(2, b, c // (nhead * hd), nb),
        in_specs=[shared, shared, per_dir, per_dir],
        out_specs=per_dir,
        out_shape=jax.ShapeDtypeStruct((2, b, t, c), F32),
        scratch_shapes=[pltpu.VMEM((nhead, hd, hd), F32)],
        compiler_params=pltpu.CompilerParams(
            dimension_semantics=("arbitrary",) * 4,
            vmem_limit_bytes=VMEM_LIMIT_BYTES),
        name="gla_scan",
    )(q, v, k, lf)


def _rwkv_kernel(r_ref, k_ref, v_ref, a_ref, b_ref, lw_ref, y_ref, h_ref, *, L, nck):
    d = pl.program_id(0)
    sgn = 1 - 2 * d
    L2 = 2 * L
    W = 2 * D_HEAD_DIM

    @pl.when(pl.program_id(3) == 0)
    def _():
        h_ref[...] = jnp.zeros_like(h_ref)

    ri, ci = _iota2(L, L)
    before = (ci - ri) * sgn <= 0
    r2, c2 = _iota2(L2, L2)
    order2 = ((c2 & (L - 1)) - (r2 & (L - 1))) * sgn
    strict2 = order2 < 0
    incl2 = order2 <= 0
    eye2 = jnp.where(r2 == c2, 1.0, 0.0)
    rw, cw = _iota2(W, W)
    eye_w = rw == cw
    head0 = lax.broadcasted_iota(jnp.int32, (L, W), 1) < D_HEAD_DIM
    stack = lambda x: jnp.concatenate([jnp.where(head0, x, 0.0), jnp.where(head0, 0.0, x)], axis=0)
    n_levels = int(math.log2(L))

    each = lambda f, *cols: [f(*xs) for xs in zip(*cols)]
    npair = h_ref.shape[0]
    slices = [pl.ds(pl.multiple_of((i + d * (nck - 1 - 2 * i)) * L, L), L) for i in range(nck)]
    where = [(sl, slice(p * W, (p + 1) * W)) for p in range(npair) for sl in slices]
    r, k, v, a, b, lw = ([ref[sl, lanes] for sl, lanes in where]
                         for ref in (r_ref, k_ref, v_ref, a_ref, b_ref, lw_ref))
    cum = each(lambda x: _mask_nn(before, x), lw)
    cum_end = each(lambda c: _end_row(c, d), cum)
    e_neg = each(lambda c: jnp.exp(-c), cum)
    e_end = each(lambda ce, c: jnp.exp(ce - c), cum_end, cum)
    at = each(lambda x, c, w: stack(x * jnp.exp(c - w)), a, cum, lw)
    rt = each(lambda x, c: stack(x * jnp.exp(c)), r, cum)
    bt = each(lambda x, e: stack(x * e), b, e_neg)
    kt = each(lambda x, e: stack(x * e), k, e_neg)
    vs = each(stack, v)
    gram = each(lambda p, q, s, t: _nt(jnp.concatenate([p, q], axis=0), jnp.concatenate([s, t], axis=0)),
                at, rt, bt, kt)
    nmat = each(lambda g: jnp.where(strict2, g[:L2, :L2], 0.0), gram)
    a_k = each(lambda g: jnp.where(strict2, g[:L2, L2:], 0.0), gram)
    r_bk = each(lambda g: jnp.where(jnp.concatenate([incl2, incl2], axis=1), g[L2:, :], 0.0), gram)
    tinv = each(lambda n: eye2 + n, nmat)
    pw = each(lambda n: _nn(n, n), nmat)
    for lev in range(1, n_levels):
        if lev < n_levels - 1:
            both = each(lambda p, t: _nn(p, jnp.concatenate([p, t], axis=1)), pw, tinv)
            pw = each(lambda x: x[:, :L2], both)
            tinv = each(lambda t, x: t + x[:, L2:], tinv, both)
        else:
            tinv = each(lambda p, t: t + _nn(p, t), pw, tinv)
    akv = each(_nn, a_k, vs)
    wu = each(lambda t, p, q: _nn(t, jnp.concatenate([p, q], axis=1)), tinv, at, akv)
    zs = each(lambda x, y: jnp.concatenate([x, jnp.concatenate([jnp.zeros_like(y), y], axis=1)], axis=0), wu, vs)
    qy = each(_nn, r_bk, zs)
    md = each(lambda x, y, e, z: _tn(jnp.concatenate([stack(x * e), stack(y * e)], axis=0), z),
              b, k, e_end, zs)
    dec = each(lambda ce: jnp.sum(jnp.where(eye_w, jnp.broadcast_to(jnp.exp(ce), (W, W)), 0.0),
                                  axis=1, keepdims=True), cum_end)
    hs = [h_ref[p] for p in range(npair)]
    for i in range(nck):
        for p in range(npair):
            s = p * nck + i
            ys = _nn(rt[s] + qy[s][:, :W], hs[p]) + qy[s][:, W:]
            y_ref[where[s][0], where[s][1]] = ys[:L, :] + ys[L:, :]
            hs[p] = dec[s] * hs[p] + _nn(md[s][:, :W], hs[p]) + md[s][:, W:]
    for p in range(npair):
        h_ref[p] = hs[p]


def _rwkv_scan(r, k, v, a, b, lw, n_ctx):
    bsz, t, c = r.shape
    w = 2 * D_HEAD_DIM
    L = RWKV_CHUNK
    tb = SCAN_TIME_BLOCK
    nck, nb, ncb = tb // L, t // tb, n_ctx // tb
    npair = RWKV_PAIRS_PER_STEP
    shared = pl.BlockSpec((None, tb, npair * w), lambda d, i, p, j: (i, _time_index(d, j, ncb, nb), p))
    per_dir = pl.BlockSpec((None, None, tb, npair * w), lambda d, i, p, j: (d, i, _time_index(d, j, ncb, nb), p))
    return pl.pallas_call(
        functools.partial(_rwkv_kernel, L=L, nck=nck),
        grid=(2, bsz, c // (npair * w), nb),
        in_specs=[shared] * 5 + [per_dir],
        out_specs=per_dir,
        out_shape=jax.ShapeDtypeStruct((2, bsz, t, c), F32),
        scratch_shapes=[pltpu.VMEM((npair, w, w), F32)],
        compiler_params=pltpu.CompilerParams(
            dimension_semantics=("arbitrary",) * 4,
            vmem_limit_bytes=VMEM_LIMIT_BYTES),
        name="rwkv7_scan",
    )(r, k, v, a, b, lw)


def _neighbours(x, n_ctx):
    pos = jnp.arange(x.shape[1])[None, :, None]
    prev = jnp.pad(x[:, :-1], ((0, 0), (1, 0), (0, 0)))
    nxt = jnp.pad(x[:, 1:], ((0, 0), (0, 1), (0, 0)))
    return jnp.where(pos == n_ctx, 0.0, prev), jnp.where(pos == n_ctx - 1, 0.0, nxt)


def _dwconv3(x, w, b, n_ctx):
    prev, nxt = _neighbours(x, n_ctx)
    return prev * w[0] + x * w[1] + nxt * w[2] + b


def _layer_norm(x, g, b):
    mu = x.mean(-1, keepdims=True)
    var = jnp.mean(jnp.square(x - mu), -1, keepdims=True)
    return (x - mu) * lax.rsqrt(var + LN_EPS) * g + b


def _head_norm(y, eps, center):
    if center:
        y = y - y.mean(-1, keepdims=True)
    y = y * lax.rsqrt(jnp.mean(y * y, -1, keepdims=True) + eps)
    return y.reshape(y.shape[0], y.shape[1], -1)


def _ssd_mixer(p, n_ctx, params):
    conv_w, conv_b, dt_bias, a_log, d_skip, norm_w = params
    b, t, _ = p.shape
    a = -jnp.exp(a_log)
    z = p[..., :A_INNER]
    xbc = jax.nn.silu(_dwconv3(p[..., A_INNER:A_INNER + A_XBC], conv_w, conv_b, n_ctx))
    xs = xbc[..., :A_INNER].reshape(b, t, A_HEADS, A_HEAD_DIM)
    bm = xbc[..., A_INNER:A_INNER + A_GROUPS * A_STATE]
    cm = xbc[..., A_INNER + A_GROUPS * A_STATE:]
    dt = jnp.moveaxis(jax.nn.softplus(p[..., A_INNER + A_XBC:].reshape(b, t, 2, A_HEADS) + dt_bias), 2, 0)
    x = (xs[None] * dt[..., None]).reshape(2, b, t, A_INNER)
    y = _ssd_scan(x, dt * a[:, None, None, :], bm, cm, n_ctx)
    y = ((y[0] + y[1]).reshape(b, t, A_HEADS, A_HEAD_DIM) + d_skip[:, None] * xs).reshape(b, t, A_INNER) * jax.nn.silu(z)
    return y * lax.rsqrt(jnp.mean(y * y, -1, keepdims=True) + 1e-6) * norm_w


def _mlstm_mixer(p, n_ctx, params):
    conv_w, conv_b, i_bias, f_bias, norm_w = params
    b, t, _ = p.shape
    qk = jax.nn.silu(_dwconv3(p[..., :2 * B_QK], conv_w, conv_b, n_ctx))
    v = p[..., 2 * B_QK:2 * B_QK + B_INNER]
    o = p[..., 2 * B_QK + B_INNER:2 * B_QK + 2 * B_INNER]
    gates = p[..., 2 * B_QK + 2 * B_INNER:].reshape(b, t, 2, 2, B_HEADS)
    log_i = gates[:, :, 0] + i_bias
    log_f = jax.nn.log_sigmoid(gates[:, :, 1] + f_bias)
    g = jnp.moveaxis(jnp.concatenate([log_i, log_f], axis=-1), 2, 0)
    h = _mlstm_scan(qk[..., :B_QK], qk[..., B_QK:], v, g, n_ctx)
    h = (h[0] + h[1]).reshape(b, t, B_HEADS, B_V_DIM)
    return jax.nn.sigmoid(o) * _head_norm(h, MLSTM_EPS, True) * norm_w


def _hgrn2_mixer(p, n_ctx, lb, params):
    f_bias, norm_w = params
    b, t, _ = p.shape
    q = jax.nn.silu(p[..., :C_INNER])
    f_pre = jnp.moveaxis(p[..., C_INNER:3 * C_INNER].reshape(b, t, 2, C_INNER), 2, 0) + f_bias[:, None, None, :]
    log_f = jnp.log(lb + (1 - lb) * jax.nn.sigmoid(f_pre))
    k = (1 - lb) * jax.nn.sigmoid(-f_pre)
    o = _gla_scan(q, p[..., 3 * C_INNER:4 * C_INNER], k, log_f, n_ctx)
    o = (o[0] + o[1]).reshape(b, t, C_HEADS, C_HEAD_DIM)
    return _head_norm(o, 1e-6, False) * norm_w * jax.nn.silu(p[..., 4 * C_INNER:])


def _rwkv7_mixer(p, n_ctx, params):
    mu, w0, w2, a0, a2, g2, k_k, k_a, r_k, ln_w, ln_b = params
    b, t, _ = p.shape
    hd = lambda u: u.reshape(b, t, D_HEADS, D_HEAD_DIM)
    prev, nxt = _neighbours(p, n_ctx)
    p = p + mu * (0.5 * (prev + nxt) - p)
    r = p[..., :D_INNER]
    k = p[..., D_INNER:2 * D_INNER]
    v = p[..., 2 * D_INNER:3 * D_INNER]
    o = 3 * D_INNER
    wl = jnp.tanh(p[..., o:o + 2 * D_W_LORA]).reshape(b * t, 2, D_W_LORA)
    al = p[..., o + 2 * D_W_LORA:o + 2 * D_W_LORA + D_A_LORA]
    gl = p[..., o + 2 * D_W_LORA + D_A_LORA:]
    w = jnp.stack([_matmul(wl[:, d], w2[d]).reshape(b, t, D_INNER) + w0[d] for d in range(2)])
    log_decay = -jnp.exp(-jax.nn.softplus(-w) - 0.5)
    a = jax.nn.sigmoid(a0 + _matmul(al.reshape(b * t, D_A_LORA), a2).reshape(b, t, D_INNER))
    g = _matmul(jax.nn.sigmoid(gl).reshape(b * t, D_G_LORA), g2).reshape(b, t, D_INNER)
    kk = hd(k * k_k)
    kk = (kk * lax.rsqrt(jnp.maximum(jnp.sum(kk * kk, -1, keepdims=True), 1e-12))).reshape(b, t, D_INNER)
    k = k * (1 + (a - 1) * k_a)
    y = _rwkv_scan(r, k, v, -kk, kk * a, log_decay, n_ctx)
    y = hd(y[0] + y[1])
    bonus = (jnp.sum(hd(r) * hd(k) * r_k, axis=-1, keepdims=True) * hd(v)).reshape(b, t, D_INNER)
    return (_head_norm(y, RWKV_EPS, True) * ln_w + ln_b + bonus) * g


def _to_col_major(u, rows):
    b, s, d = u.shape
    return u.reshape(b, rows, GRID_W, d).transpose(0, 2, 1, 3).reshape(b, s, d)


def _from_col_major(u, rows):
    b, s, d = u.shape
    return u.reshape(b, GRID_W, rows, d).transpose(0, 2, 1, 3).reshape(b, s, d)


def _tile_specs(bsz, t, n_ctx):
    tiles_per_seq, ctx_tiles = t // ROW_TILE, n_ctx // ROW_TILE
    mod_row = lambda i: jnp.where(i % tiles_per_seq < ctx_tiles, bsz, i // tiles_per_seq)
    rows = lambda c: pl.BlockSpec((ROW_TILE, c), lambda i: (i, 0))
    whole = lambda a: pl.BlockSpec(a.shape, lambda i: (0,) * a.ndim)
    mod = pl.BlockSpec((None, 6, D_MODEL), lambda i: (mod_row(i), 0, 0))
    params = pltpu.CompilerParams(dimension_semantics=("arbitrary",), vmem_limit_bytes=VMEM_LIMIT_BYTES)
    return rows, whole, mod, params


def _in_proj_kernel(x_ref, mod_ref, w_ref, o_ref):
    m = mod_ref[...]
    o_ref[...] = _nn(x_ref[...] * (1.0 + m[1:2, :]) + m[0:1, :], w_ref[...])


def _in_proj(xa, mods, w, bsz, n_ctx):
    n = w.shape[1]
    n_pad = -(-n // LANES) * LANES
    wb = jnp.pad(w.astype(BF16), ((0, 0), (0, n_pad - n)))
    rows, whole, mod, params = _tile_specs(bsz, xa.shape[0] // bsz, n_ctx)
    return pl.pallas_call(
        _in_proj_kernel,
        grid=(xa.shape[0] // ROW_TILE,),
        in_specs=[rows(D_MODEL), mod, whole(wb)],
        out_specs=rows(n_pad),
        out_shape=jax.ShapeDtypeStruct((xa.shape[0], n_pad), F32),
        compiler_params=params,
        name="in_proj",
    )(xa, mods, wb)


def _norm_rows(z, ln):
    mu = jnp.mean(z, axis=-1, keepdims=True)
    zc = z - mu
    var = jnp.mean(zc * zc, axis=-1, keepdims=True)
    return zc * lax.rsqrt(var + LN_EPS) * ln[0:1, :] + ln[1:2, :]


def _out_proj_kernel(fa_ref, fb_ref, w_ref, x_ref, mod_ref, ln_ref, xo_ref, h_ref, hb_ref):
    ka = fa_ref.shape[1]
    m = mod_ref[...]
    y = _nn(fa_ref[...], w_ref[:ka, :]) + _nn(fb_ref[...], w_ref[ka:, :])
    xn = _norm_rows(DEEPNORM_ALPHA * x_ref[...] + m[2:3, :] * y, ln_ref[...])
    xo_ref[...] = xn
    h = xn * (1.0 + m[4:5, :]) + m[3:4, :]
    h_ref[...] = h
    hb_ref[...] = h.astype(BF16)


def _out_proj(fa, fb, w, xa, mods, ln, bsz, n_ctx):
    t_all, d = xa.shape
    rows, whole, mod, params = _tile_specs(bsz, t_all // bsz, n_ctx)
    wb = w.astype(BF16)
    return pl.pallas_call(
        _out_proj_kernel,
        grid=(t_all // ROW_TILE,),
        in_specs=[rows(fa.shape[1]), rows(fb.shape[1]), whole(wb), rows(d), mod, whole(ln)],
        out_specs=[rows(d), rows(d), rows(d)],
        out_shape=[jax.ShapeDtypeStruct((t_all, d), F32), jax.ShapeDtypeStruct((t_all, d), F32),
                   jax.ShapeDtypeStruct((t_all, d), BF16)],
        compiler_params=params,
        name="out_proj_norm",
    )(fa, fb, wb, xa, mods, ln)


def _ffn_norm_kernel(f_ref, x_ref, mod_ref, ln_ref, xo_ref):
    m = mod_ref[...]
    xo_ref[...] = _norm_rows(DEEPNORM_ALPHA * x_ref[...] + m[5:6, :] * f_ref[...], ln_ref[...])


def _ffn_norm(f, xa, mods, ln, bsz, n_ctx):
    t_all, d = xa.shape
    rows, whole, mod, params = _tile_specs(bsz, t_all // bsz, n_ctx)
    return pl.pallas_call(
        _ffn_norm_kernel,
        grid=(t_all // ROW_TILE,),
        in_specs=[rows(d), rows(d), mod, whole(ln)],
        out_specs=rows(d),
        out_shape=jax.ShapeDtypeStruct((t_all, d), F32),
        compiler_params=params,
        name="ffn_residual_norm",
    )(f, xa, mods, ln)


def kernel(x, c, ctx, c_ctx, mod_w, mod_b, ln_g, ln_b, ev_w_in, ev_w_out, ssd_conv_w, ssd_conv_b, ssd_dt_bias, ssd_a_log, ssd_d, ssd_norm_w, mlstm_conv_w, mlstm_conv_b, mlstm_i_bias, mlstm_f_bias, mlstm_norm_w, od_w_in, od_w_out, hgrn_lb_logits, hgrn_f_bias, hgrn_norm_w, rwkv_mu, rwkv_w0, rwkv_w2, rwkv_a0, rwkv_a2, rwkv_g2, rwkv_k_k, rwkv_k_a, rwkv_r_k, rwkv_ln_w, rwkv_ln_b, router_w, router_bias, exp_w_gate, exp_w_up, exp_w_down):
    bsz, seq, _ = x.shape
    n_ctx = ctx.shape[1]
    rows = seq // GRID_W
    lb_all = jnp.cumsum(jax.nn.softmax(hgrn_lb_logits.astype(F32), axis=0), axis=0)
    lb_all = lb_all - lb_all[0]
    s_c = jax.nn.silu(c)
    s_cc = jax.nn.silu(c_ctx)
    t = n_ctx + seq
    xa = jnp.concatenate([ctx, x], axis=1).reshape(bsz * t, D_MODEL)
    seq3 = lambda a: a.reshape(bsz, t, a.shape[-1])
    flat = lambda a: a.reshape(bsz * t, a.shape[-1])
    lat_order = lambda a, f: flat(jnp.concatenate([seq3(a)[:, :n_ctx], f(seq3(a)[:, n_ctx:], rows)], axis=1))
    for layer in range(DEPTH):
        i = layer // 2
        mods = _matmul(jnp.concatenate([s_c, s_cc[None]], axis=0), mod_w[layer], tm=8, tn=512) + mod_b[layer]
        mods = mods.reshape(bsz + 1, 6, D_MODEL)
        ln = jnp.stack([ln_g[layer], ln_b[layer]], axis=1)
        if layer % 2 == 0:
            p = seq3(_in_proj(xa, mods, ev_w_in[i], bsz, n_ctx))
            fa = _ssd_mixer(p[..., :P_A], n_ctx,
                            (ssd_conv_w[i], ssd_conv_b[i], ssd_dt_bias[i], ssd_a_log[i], ssd_d[i], ssd_norm_w[i]))
            fb = _mlstm_mixer(p[..., P_A:P_A + P_B], n_ctx,
                              (mlstm_conv_w[i], mlstm_conv_b[i], mlstm_i_bias[i], mlstm_f_bias[i], mlstm_norm_w[i]))
            fa, fb, w_out = flat(fa), flat(fb), ev_w_out[i]
        else:
            p = seq3(_in_proj(lat_order(xa, _to_col_major), mods, od_w_in[i], bsz, n_ctx))
            fa = _hgrn2_mixer(p[..., :P_C], n_ctx, lb_all[layer], (hgrn_f_bias[i], hgrn_norm_w[i]))
            fb = _rwkv7_mixer(p[..., P_C:P_C + P_D], n_ctx,
                              (rwkv_mu[i], rwkv_w0[i], rwkv_w2[i], rwkv_a0[i], rwkv_a2[i], rwkv_g2[i],
                               rwkv_k_k[i], rwkv_k_a[i], rwkv_r_k[i], rwkv_ln_w[i], rwkv_ln_b[i]))
            fa, fb, w_out = lat_order(fa, _from_col_major), lat_order(fb, _from_col_major), od_w_out[i]
        xa, h, hb = _out_proj(fa, fb, w_out, xa, mods, ln[0], bsz, n_ctx)
        f = _moe_ffn(h, hb, router_w, router_bias, exp_w_gate[layer], exp_w_up[layer], exp_w_down[layer])
        xa = _ffn_norm(f, xa, mods, ln[1], bsz, n_ctx)
    return seq3(xa)[:, n_ctx:]
```

```python
import functools
import math

import jax
import jax.numpy as jnp
from jax import lax
from jax.experimental import pallas as pl
from jax.experimental.pallas import tpu as pltpu

F32 = jnp.float32
BF16 = jnp.bfloat16

D_MODEL = 1024
DEPTH = 4
GRID_W = 64
A_HEADS = 8
A_HEAD_DIM = 64
A_INNER = A_HEADS * A_HEAD_DIM
A_GROUPS = 2
A_STATE = 64
A_XBC = A_INNER + 2 * A_GROUPS * A_STATE
B_HEADS = 4
B_QK_DIM = 64
B_V_DIM = 128
B_QK = B_HEADS * B_QK_DIM
B_INNER = B_HEADS * B_V_DIM
MLSTM_EPS = 1e-6
C_HEADS = 4
C_HEAD_DIM = 128
C_INNER = C_HEADS * C_HEAD_DIM
D_HEADS = 8
D_HEAD_DIM = 64
D_INNER = D_HEADS * D_HEAD_DIM
D_W_LORA = 64
D_A_LORA = 64
D_G_LORA = 128
RWKV_EPS = 64e-5
P_A = A_INNER + A_XBC + 2 * A_HEADS
P_B = 2 * B_QK + 2 * B_INNER + 4 * B_HEADS
P_C = 5 * C_INNER
P_D = 3 * D_INNER + 2 * D_W_LORA + D_A_LORA + D_G_LORA
N_EXPERTS = 32
N_EXPERT_GROUPS = 8
EXPERTS_PER_GROUP = N_EXPERTS // N_EXPERT_GROUPS
TOP_K = 2
D_EXPERT = 512
MOE_BLOCK = 256
RANK_BLOCK = 512
ROW_TILE = 256
LANES = 128
DEEPNORM_ALPHA = (2 * DEPTH) ** 0.25
LN_EPS = 1e-5
M_INIT = -1e30
NEG_BIG = -1e30

SSD_CHUNK = 128
MLSTM_CHUNK = 128
GLA_CHUNK = 16
GLA_BLOCK = 64
GLA_HEADS_PER_STEP = 4
RWKV_CHUNK = 64
RWKV_PAIRS_PER_STEP = 4
SCAN_TIME_BLOCK = 256

VMEM_LIMIT_BYTES = 48 * 1024 * 1024
HI = lax.Precision.HIGHEST


def _dot(a, b, dims, exact):
    if exact:
        return lax.dot_general(a.astype(F32), b.astype(F32), (dims, ((), ())),
                               precision=HI, preferred_element_type=F32)
    return lax.dot_general(a.astype(BF16), b.astype(BF16), (dims, ((), ())),
                           preferred_element_type=F32)


def _nn(a, b, exact=False):
    return _dot(a, b, ((1,), (0,)), exact)


def _nt(a, b, exact=False):
    return _dot(a, b, ((1,), (1,)), exact)


def _tn(a, b, exact=False):
    return _dot(a, b, ((0,), (0,)), exact)


def _iota2(n, m):
    return (lax.broadcasted_iota(jnp.int32, (n, m), 0),
            lax.broadcasted_iota(jnp.int32, (n, m), 1))


def _split3(x):
    x1 = x.astype(BF16)
    r1 = x - x1.astype(F32)
    x2 = r1.astype(BF16)
    x3 = (r1 - x2.astype(F32)).astype(BF16)
    return x1, x2, x3


def _mask_nn(mask, x):
    mb = mask.astype(BF16)
    x1, x2, x3 = _split3(x)
    return _nn(mb, x1, False) + _nn(mb, x2, False) + _nn(mb, x3, False)


def _nn_mask(x, mask):
    mb = mask.astype(BF16)
    x1, x2, x3 = _split3(x)
    return _nn(x1, mb, False) + _nn(x2, mb, False) + _nn(x3, mb, False)


def _cum_mats(col, row, L):
    ri, ci = _iota2(L, L)
    ccol = _mask_nn(ci <= ri, jnp.broadcast_to(col, (L, L)))
    crow = _nn_mask(jnp.broadcast_to(row, (L, L)), ri <= ci)
    return ccol, crow


def _mm_kernel(x_ref, w_ref, o_ref, *, exact):
    o_ref[...] = _nn(x_ref[...], w_ref[...], exact)


def _matmul(x, w, tm=512, tn=512, exact=False):
    m, k = x.shape
    n = w.shape[1]
    n_pad = -(-n // tn) * tn
    m_pad = -(-m // tm) * tm
    xb = x if exact else x.astype(BF16)
    wb = w if exact else w.astype(BF16)
    if n_pad != n:
        wb = jnp.pad(wb, ((0, 0), (0, n_pad - n)))
    if m_pad != m:
        xb = jnp.pad(xb, ((0, m_pad - m), (0, 0)))
    out = pl.pallas_call(
        functools.partial(_mm_kernel, exact=exact),
        grid=(n_pad // tn, m_pad // tm),
        in_specs=[pl.BlockSpec((tm, k), lambda j, i: (i, 0)),
                  pl.BlockSpec((k, tn), lambda j, i: (0, j))],
        out_specs=pl.BlockSpec((tm, tn), lambda j, i: (i, j)),
        out_shape=jax.ShapeDtypeStruct((m_pad, n_pad), F32),
        compiler_params=pltpu.CompilerParams(
            dimension_semantics=("arbitrary", "arbitrary"),
            vmem_limit_bytes=VMEM_LIMIT_BYTES),
        name="dense_matmul",
    )(xb, wb)
    return out[:m, :n]


def _moe_kernel(blk_e_ref, n_used_ref, x_ref, wg_ref, wu_ref, wd_ref, o_ref):
    i = pl.program_id(0)

    @pl.when(i < n_used_ref[0])
    def _():
        x = x_ref[...]
        g = _nn(x, wg_ref[...])
        u = _nn(x, wu_ref[...])
        o_ref[...] = _nn(g * jax.nn.sigmoid(g) * u, wd_ref[...]).astype(o_ref.dtype)

    @pl.when(i >= n_used_ref[0])
    def _():
        o_ref[...] = jnp.zeros_like(o_ref)


def _moe_experts(xp, blk_e, n_used, w_gate, w_up, w_down):
    n_rows, d = xp.shape
    n_blocks = n_rows // MOE_BLOCK
    grid_spec = pltpu.PrefetchScalarGridSpec(
        num_scalar_prefetch=2,
        grid=(n_blocks,),
        in_specs=[
            pl.BlockSpec((MOE_BLOCK, d), lambda i, be, nu: (i, 0)),
            pl.BlockSpec((None, d, D_EXPERT), lambda i, be, nu: (be[i], 0, 0)),
            pl.BlockSpec((None, d, D_EXPERT), lambda i, be, nu: (be[i], 0, 0)),
            pl.BlockSpec((None, D_EXPERT, d), lambda i, be, nu: (be[i], 0, 0)),
        ],
        out_specs=pl.BlockSpec((MOE_BLOCK, d), lambda i, be, nu: (i, 0)),
    )
    return pl.pallas_call(
        _moe_kernel,
        grid_spec=grid_spec,
        out_shape=jax.ShapeDtypeStruct((n_rows, d), BF16),
        compiler_params=pltpu.CompilerParams(
            dimension_semantics=("arbitrary",),
            vmem_limit_bytes=VMEM_LIMIT_BYTES),
        name="moe_experts",
    )(blk_e, n_used, xp, w_gate, w_up, w_down)


def _top2(vals):
    m = len(vals)
    m1 = functools.reduce(jnp.maximum, vals)
    i1 = jnp.full_like(m1, float(m - 1))
    for j in reversed(range(m - 1)):
        i1 = jnp.where(vals[j] == m1, float(j), i1)
    rest = [jnp.where(i1 == float(j), -jnp.inf, vals[j]) for j in range(m)]
    m2 = functools.reduce(jnp.maximum, rest)
    i2 = jnp.full_like(m1, float(m - 1))
    for j in reversed(range(m - 1)):
        i2 = jnp.where(rest[j] == m2, float(j), i2)
    return m1, i1, m2, i2


def _router_kernel(h_ref, wt_ref, bias_ref, e_ref, w_ref, rank_ref, cnt_ref, carry_ref):
    tm = h_ref.shape[0]
    ng, per = N_EXPERT_GROUPS, EXPERTS_PER_GROUP

    @pl.when(pl.program_id(0) == 0)
    def _():
        carry_ref[...] = jnp.zeros_like(carry_ref)

    s = jax.nn.sigmoid(_nt(wt_ref[...], h_ref[...], True))
    sb = s + bias_ref[...]
    biased = [sb[j * ng:(j + 1) * ng, :] for j in range(per)]
    plain = [s[j * ng:(j + 1) * ng, :] for j in range(per)]
    m1, _, m2, _ = _top2(biased)
    gsum = m1 + m2
    rows = lax.broadcasted_iota(jnp.int32, (ng, tm), 0).astype(F32)
    gmax = jnp.max(gsum, axis=0, keepdims=True)
    gi = jnp.min(jnp.where(gsum == gmax, rows, float(ng)), axis=0, keepdims=True)
    sel = rows == gi
    pick = lambda v: jnp.sum(jnp.where(sel, v, 0.0), axis=0, keepdims=True)
    in_b = [pick(v) for v in biased]
    in_s = [pick(v) for v in plain]
    _, l1, _, l2 = _top2(in_b)
    w1 = functools.reduce(jnp.add, [jnp.where(l1 == float(j), in_s[j], 0.0) for j in range(per)])
    w2 = functools.reduce(jnp.add, [jnp.where(l2 == float(j), in_s[j], 0.0) for j in range(per)])
    e1 = gi * float(per) + l1
    e2 = gi * float(per) + l2
    wsum = w1 + w2
    e_ref[0:1, :] = e1.astype(jnp.int32)
    e_ref[1:2, :] = e2.astype(jnp.int32)
    w_ref[0:1, :] = w1 / wsum
    w_ref[1:2, :] = w2 / wsum
    row = lax.broadcasted_iota(jnp.int32, (N_EXPERTS, tm), 0)
    experts = ((row % ng) * per + row // ng).astype(F32)
    oh1 = jnp.where(experts == e1, 1.0, 0.0)
    oh2 = jnp.where(experts == e2, 1.0, 0.0)
    oh = oh1 + oh2
    ri, ci = _iota2(tm, tm)
    seen = _nn(oh, jnp.where(ri < ci, 1.0, 0.0)) + carry_ref[...]
    rank_ref[0:1, :] = jnp.sum(oh1 * seen, axis=0, keepdims=True).astype(jnp.int32)
    rank_ref[1:2, :] = jnp.sum(oh2 * seen, axis=0, keepdims=True).astype(jnp.int32)
    carry = carry_ref[...] + jnp.sum(oh, axis=1, keepdims=True)
    carry_ref[...] = carry
    cnt_ref[...] = carry.astype(jnp.int32)


def _route(h, router_w, router_bias):
    t, d = h.shape
    tm = RANK_BLOCK
    kt = lambda dt: jax.ShapeDtypeStruct((TOP_K, t), dt)
    blk = pl.BlockSpec((TOP_K, tm), lambda i: (0, i))
    member_major = lambda a: a.reshape(N_EXPERT_GROUPS, EXPERTS_PER_GROUP, -1).transpose(1, 0, 2).reshape(N_EXPERTS, -1)
    e, w, rank, counts = pl.pallas_call(
        _router_kernel,
        grid=(t // tm,),
        in_specs=[pl.BlockSpec((tm, d), lambda i: (i, 0)),
                  pl.BlockSpec((N_EXPERTS, d), lambda i: (0, 0)),
                  pl.BlockSpec((N_EXPERTS, 1), lambda i: (0, 0))],
        out_specs=[blk, blk, blk, pl.BlockSpec((N_EXPERTS, 1), lambda i: (0, 0))],
        out_shape=[kt(jnp.int32), kt(F32), kt(jnp.int32), jax.ShapeDtypeStruct((N_EXPERTS, 1), jnp.int32)],
        scratch_shapes=[pltpu.VMEM((N_EXPERTS, 1), F32)],
        compiler_params=pltpu.CompilerParams(dimension_semantics=("arbitrary",),
                                             vmem_limit_bytes=VMEM_LIMIT_BYTES),
        name="moe_router",
    )(h, member_major(router_w.T), member_major(router_bias.astype(F32).reshape(N_EXPERTS, 1)))
    counts = counts.reshape(EXPERTS_PER_GROUP, N_EXPERT_GROUPS).T.reshape(N_EXPERTS)
    return e, w, rank, counts


def _moe_ffn(h, hb, router_w, router_bias, w_gate, w_up, w_down):
    t, d = h.shape
    expert, wts, rank, counts = _route(h, router_w, router_bias)
    n_assign = t * TOP_K
    padded = (counts + MOE_BLOCK - 1) // MOE_BLOCK * MOE_BLOCK
    pends = jnp.cumsum(padded)
    pstarts = pends - padded
    dest = pstarts[expert] + rank
    n_blocks = -(-n_assign // MOE_BLOCK) + N_EXPERTS
    slot_token = jnp.zeros((n_blocks * MOE_BLOCK,), jnp.int32)
    for kk in range(TOP_K):
        slot_token = slot_token.at[dest[kk]].set(jnp.arange(t, dtype=jnp.int32), unique_indices=True)
    xp = hb[slot_token]
    blk_start = jnp.arange(n_blocks, dtype=jnp.int32) * MOE_BLOCK
    blk_e = jnp.minimum(jnp.sum(pends[None, :] <= blk_start[:, None], axis=1), N_EXPERTS - 1).astype(jnp.int32)
    n_used = (pends[-1] // MOE_BLOCK).astype(jnp.int32).reshape(1)
    yp = _moe_experts(xp, blk_e, n_used, w_gate, w_up, w_down)
    out = yp[dest[0]] * wts[0][:, None]
    for kk in range(1, TOP_K):
        out = out + yp[dest[kk]] * wts[kk][:, None]
    return out


def _ssd_kernel(x_ref, lac_ref, lar_ref, b_ref, c_ref, y_ref, st_ref, *, L):
    d = pl.program_id(0)
    sgn = 1 - 2 * d
    hp, gn = A_INNER, A_GROUPS * A_STATE
    hpg = A_HEADS // A_GROUPS

    @pl.when(pl.program_id(2) == 0)
    def _():
        st_ref[...] = jnp.zeros_like(st_ref)

    ri, ci = _iota2(L, L)
    before = (ci - ri) * sgn <= 0
    before_t = (ri - ci) * sgn <= 0
    x = x_ref[...]
    bm = b_ref[...]
    cm = c_ref[...]
    ccol = _mask_nn(before, lac_ref[...])
    crow = _nn_mask(lar_ref[...], before_t)
    head_of_lane = lax.broadcasted_iota(jnp.int32, (A_HEADS, hp), 1) // A_HEAD_DIM
    expand = head_of_lane == lax.broadcasted_iota(jnp.int32, (A_HEADS, hp), 0)
    cum = _nn_mask(ccol, expand)
    end = _end_row(cum, d)
    group_of_lane = lax.broadcasted_iota(jnp.int32, (L, gn), 1) // A_STATE
    first_of_pair = (lax.broadcasted_iota(jnp.int32, (L, 2 * A_HEAD_DIM), 1) < A_HEAD_DIM)
    cbs = [_nt(jnp.where(group_of_lane == g, cm, 0.0), bm) for g in range(A_GROUPS)]
    pairs = []
    for p in range(A_HEADS // 2):
        xp = x[:, 2 * p * A_HEAD_DIM:(2 * p + 2) * A_HEAD_DIM]
        ys = []
        for h in (2 * p, 2 * p + 1):
            decay = jnp.exp(jnp.where(before, ccol[:, h:h + 1] - crow[h:h + 1, :], NEG_BIG))
            ys.append(_nn(cbs[h // hpg] * decay, xp))
        pairs.append(jnp.where(first_of_pair, ys[0], ys[1]))
    st = st_ref[...]
    y_ref[...] = jnp.concatenate(pairs, axis=1) + jnp.exp(cum) * _nn(cm, st)
    own_group = (lax.broadcasted_iota(jnp.int32, (gn, hp), 0) // A_STATE
                 == lax.broadcasted_iota(jnp.int32, (gn, hp), 1) // (A_HEAD_DIM * hpg))
    st_ref[...] = jnp.exp(end) * st + jnp.where(own_group, _tn(bm, x * jnp.exp(end - cum)), 0.0)


def _ssd_scan(x, la, bm, cm, n_ctx):
    _, b, t, hp = x.shape
    L = SSD_CHUNK
    nb, ncb = t // L, n_ctx // L
    tix = lambda d, j: _time_index(d, j, ncb, nb)
    shared = pl.BlockSpec((None, L, bm.shape[-1]), lambda d, i, j: (i, tix(d, j), 0))
    return pl.pallas_call(
        functools.partial(_ssd_kernel, L=L),
        grid=(2, b, nb),
        in_specs=[pl.BlockSpec((None, None, L, hp), lambda d, i, j: (d, i, tix(d, j), 0)),
                  pl.BlockSpec((None, None, L, A_HEADS), lambda d, i, j: (d, i, tix(d, j), 0)),
                  pl.BlockSpec((None, None, A_HEADS, L), lambda d, i, j: (d, i, 0, tix(d, j))),
                  shared, shared],
        out_specs=pl.BlockSpec((None, None, L, hp), lambda d, i, j: (d, i, tix(d, j), 0)),
        out_shape=jax.ShapeDtypeStruct((2, b, t, hp), F32),
        scratch_shapes=[pltpu.VMEM((bm.shape[-1], hp), F32)],
        compiler_params=pltpu.CompilerParams(
            dimension_semantics=("arbitrary",) * 3,
            vmem_limit_bytes=VMEM_LIMIT_BYTES),
        name="ssd_scan",
    )(x, la, jnp.swapaxes(la, 2, 3), bm, cm)


def _mlstm_kernel(q_ref, k_ref, v_ref, gc_ref, gr_ref, h_ref, c_ref, n_ref, m_ref, *, L):
    d = pl.program_id(0)
    sgn = 1 - 2 * d
    nh, dk, dv = B_HEADS, B_QK_DIM, B_V_DIM

    @pl.when(pl.program_id(2) == 0)
    def _():
        c_ref[...] = jnp.zeros_like(c_ref)
        n_ref[...] = jnp.zeros_like(n_ref)
        m_ref[...] = jnp.full_like(m_ref, M_INIT)

    ri, ci = _iota2(L, L)
    before = (ci - ri) * sgn <= 0
    before_t = (ri - ci) * sgn <= 0
    gc = gc_ref[...]
    gr = gr_ref[...]
    fcol = _mask_nn(before, gc[:, nh:])
    frow = _nn_mask(gr[nh:, :], before_t)
    lane_head = lax.broadcasted_iota(jnp.int32, (L, 2 * dk), 1) // dk
    each = lambda f, *cols: [f(*xs) for xs in zip(*cols)]
    heads = list(range(nh))
    slab = [slice((h // 2) * 2 * dk, (h // 2 + 1) * 2 * dk) for h in heads]
    q = [jnp.where(lane_head == h % 2, q_ref[:, slab[h]], 0.0) * (dk ** -0.5) for h in heads]
    k = [k_ref[:, slab[h]] for h in heads]
    v = [v_ref[:, h * dv:(h + 1) * dv] for h in heads]
    li_c = [gc[:, h:h + 1] for h in heads]
    li_r = [gr[h:h + 1, :] for h in heads]
    f_c = [fcol[:, h:h + 1] for h in heads]
    f_r = [frow[h:h + 1, :] for h in heads]
    ftot = each(lambda x: _end_row(x, d), f_c)
    c_prev = [c_ref[h] for h in heads]
    n_prev = [n_ref[h] for h in heads]
    m_prev = [m_ref[h] for h in heads]
    w_end = each(lambda ft, fc, lc: ft - fc + lc, ftot, f_c, li_c)
    m_loc = each(lambda w: jnp.max(w, axis=0, keepdims=True), w_end)
    ke = each(lambda x, w, m: x * jnp.exp(w - m), k, w_end, m_loc)
    c_loc = each(_tn, ke, v)
    n_loc = each(lambda x: jnp.sum(x, axis=0, keepdims=True), ke)
    log_d = each(lambda fc, fr, lr: jnp.where(before, fc - fr + lr, NEG_BIG), f_c, f_r, li_r)
    log_inter = each(jnp.add, f_c, m_prev)
    m_row = each(lambda ld, lint: jnp.maximum(jnp.max(ld, axis=-1, keepdims=True), lint), log_d, log_inter)
    s = each(lambda a, b, ld, mr: _nt(a, b) * jnp.exp(ld - mr), q, k, log_d, m_row)
    inter = each(lambda lint, mr: jnp.exp(lint - mr), log_inter, m_row)
    num = each(lambda ss, vv, it, qq, cp: _nn(ss, vv) + it * _nn(qq, cp), s, v, inter, q, c_prev)
    den = each(lambda ss, it, qq, npv: jnp.sum(ss, axis=-1, keepdims=True)
               + it * jnp.sum(qq * npv, axis=-1, keepdims=True), s, inter, q, n_prev)
    out = each(lambda nu, de, mr: nu / jnp.maximum(jnp.abs(de), jnp.exp(-mr)), num, den, m_row)
    m_new = each(lambda ft, mp, ml: jnp.maximum(ft + mp, ml), ftot, m_prev, m_loc)
    sp = each(lambda ft, mp, mn: jnp.exp(ft + mp - mn), ftot, m_prev, m_new)
    sc = each(lambda ml, mn: jnp.exp(ml - mn), m_loc, m_new)
    for h in heads:
        h_ref[:, h * dv:(h + 1) * dv] = out[h]
        c_ref[h] = sp[h] * c_prev[h] + sc[h] * c_loc[h]
        n_ref[h] = sp[h] * n_prev[h] + sc[h] * n_loc[h]
        m_ref[h] = m_new[h]


def _mlstm_scan(q, k, v, gates, n_ctx):
    b, t, _ = q.shape
    L = MLSTM_CHUNK
    nb, ncb = t // L, n_ctx // L
    tix = lambda d, j: _time_index(d, j, ncb, nb)
    shared = lambda c: pl.BlockSpec((None, L, c), lambda d, i, j: (i, tix(d, j), 0))
    return pl.pallas_call(
        functools.partial(_mlstm_kernel, L=L),
        grid=(2, b, nb),
        in_specs=[shared(B_QK), shared(B_QK), shared(B_INNER),
                  pl.BlockSpec((None, None, L, 2 * B_HEADS), lambda d, i, j: (d, i, tix(d, j), 0)),
                  pl.BlockSpec((None, None, 2 * B_HEADS, L), lambda d, i, j: (d, i, 0, tix(d, j)))],
        out_specs=pl.BlockSpec((None, None, L, B_INNER), lambda d, i, j: (d, i, tix(d, j), 0)),
        out_shape=jax.ShapeDtypeStruct((2, b, t, B_INNER), F32),
        scratch_shapes=[pltpu.VMEM((B_HEADS, 2 * B_QK_DIM, B_V_DIM), F32),
                        pltpu.VMEM((B_HEADS, 1, 2 * B_QK_DIM), F32),
                        pltpu.VMEM((B_HEADS, 1, 1), F32)],
        compiler_params=pltpu.CompilerParams(
            dimension_semantics=("arbitrary",) * 3,
            vmem_limit_bytes=VMEM_LIMIT_BYTES),
        name="mlstm_scan",
    )(q, k, v, gates, jnp.swapaxes(gates, 2, 3))


def _time_index(d, j, n_ctx_blocks, n_blocks):
    rev = jnp.where(j < n_ctx_blocks, n_ctx_blocks - 1 - j, n_blocks - 1 - j + n_ctx_blocks)
    return jnp.where(d == 1, rev, j)


def _end_row(x, d):
    n = x.shape[0]
    return jnp.where(d == 1, x[0:1, :], x[n - 1:n, :])


def _gla_kernel(q_ref, v_ref, k_ref, lf_ref, y_ref, st_ref, *, L, sub, nck):
    d = pl.program_id(0)
    sgn = 1 - 2 * d

    @pl.when(pl.program_id(3) == 0)
    def _():
        st_ref[...] = jnp.zeros_like(st_ref)

    ri, ci = _iota2(L, L)
    before = (ci - ri) * sgn <= 0
    rows = lax.broadcasted_iota(jnp.int32, (L, 1), 0)

    each = lambda f, *cols: [f(*xs) for xs in zip(*cols)]
    hd = C_HEAD_DIM
    nhead = st_ref.shape[0]
    slices = [pl.ds(pl.multiple_of((i + d * (nck - 1 - 2 * i)) * L, L), L) for i in range(nck)]
    where = [(sl, slice(h * hd, (h + 1) * hd)) for h in range(nhead) for sl in slices]
    q, k, v, lf = ([ref[sl, lanes] for sl, lanes in where] for ref in (q_ref, k_ref, v_ref, lf_ref))
    lam = each(lambda x: _mask_nn(before, x), lf)
    lam_end = each(lambda x: _end_row(x, d), lam)
    blocks = [[] for _ in where]
    for c in range(L // sub):
        lo, hi = c * sub, (c + 1) * sub
        upto = jnp.where(d == 1, lo - 1 - rows, rows - hi) < 0
        for s in range(len(where)):
            zero = jnp.zeros_like(lam_end[s])
            ref_f = lam[s][lo - 1:lo, :] if lo > 0 else zero
            ref_b = lam[s][hi:hi + 1, :] if hi < L else zero
            ref = jnp.where(d == 1, ref_b, ref_f)
            qc = q[s][lo:hi, :] * jnp.exp(lam[s][lo:hi, :] - ref)
            kc = k[s] * jnp.exp(jnp.where(upto, ref - lam[s], NEG_BIG))
            blocks[s].append(_nt(qc, kc))
    att = each(lambda bl: jnp.where(before, jnp.concatenate(bl, axis=0), 0.0), blocks)
    y_intra = each(_nn, att, v)
    q_in = each(lambda x, l: x * jnp.exp(l), q, lam)
    kv = each(lambda x, y, l, le: _tn(x, y * jnp.exp(le - l)), v, k, lam, lam_end)
    dec = each(jnp.exp, lam_end)
    for h in range(nhead):
        st = st_ref[h]
        for i in range(nck):
            s = h * nck + i
            y_ref[where[s][0], where[s][1]] = y_intra[s] + _nt(q_in[s], st)
            st = st * dec[s] + kv[s]
        st_ref[h] = st


def _gla_scan(q, v, k, lf, n_ctx):
    b, t, c = q.shape
    hd = C_HEAD_DIM
    L = GLA_BLOCK
    tb = SCAN_TIME_BLOCK
    nck, nb, ncb = tb // L, t // tb, n_ctx // tb
    nhead = GLA_HEADS_PER_STEP
    shared = pl.BlockSpec((None, tb, nhead * hd), lambda d, i, h, j: (i, _time_index(d, j, ncb, nb), h))
    per_dir = pl.BlockSpec((None, None, tb, nhead * hd), lambda d, i, h, j: (d, i, _time_index(d, j, ncb, nb), h))
    return pl.pallas_call(
        functools.partial(_gla_kernel, L=L, sub=GLA_CHUNK, nck=nck),
        grid=(2, b, c // (nhead * hd), nb),
        in_specs=[shared, shared, per_dir, per_dir],
        out_specs=per_dir,
        out_shape=jax.ShapeDtypeStruct((2, b, t, c), F32),
        scratch_shapes=[pltpu.VMEM((nhead, hd, hd), F32)],
        compiler_params=pltpu.CompilerParams(
            dimension_semantics=("arbitrary",) * 4,
            vmem_limit_bytes=VMEM_LIMIT_BYTES),
        name="gla_scan",
    )(q, v, k, lf)


def _rwkv_kernel(r_ref, k_ref, v_ref, a_ref, b_ref, lw_ref, y_ref, h_ref, *, L, nck):
    d = pl.program_id(0)
    sgn = 1 - 2 * d
    L2 = 2 * L
    W = 2 * D_HEAD_DIM

    @pl.when(pl.program_id(3) == 0)
    def _():
        h_ref[...] = jnp.zeros_like(h_ref)

    ri, ci = _iota2(L, L)
    before = (ci - ri) * sgn <= 0
    r2, c2 = _iota2(L2, L2)
    order2 = ((c2 & (L - 1)) - (r2 & (L - 1))) * sgn
    strict2 = order2 < 0
    incl2 = order2 <= 0
    eye2 = jnp.where(r2 == c2, 1.0, 0.0)
    rw, cw = _iota2(W, W)
    eye_w = rw == cw
    head0 = lax.broadcasted_iota(jnp.int32, (L, W), 1) < D_HEAD_DIM
    stack = lambda x: jnp.concatenate([jnp.where(head0, x, 0.0), jnp.where(head0, 0.0, x)], axis=0)
    n_levels = int(math.log2(L))

    each = lambda f, *cols: [f(*xs) for xs in zip(*cols)]
    npair = h_ref.shape[0]
    slices = [pl.ds(pl.multiple_of((i + d * (nck - 1 - 2 * i)) * L, L), L) for i in range(nck)]
    where = [(sl, slice(p * W, (p + 1) * W)) for p in range(npair) for sl in slices]
    r, k, v, a, b, lw = ([ref[sl, lanes] for sl, lanes in where]
                         for ref in (r_ref, k_ref, v_ref, a_ref, b_ref, lw_ref))
    cum = each(lambda x: _mask_nn(before, x), lw)
    cum_end = each(lambda c: _end_row(c, d), cum)
    e_neg = each(lambda c: jnp.exp(-c), cum)
    e_end = each(lambda ce, c: jnp.exp(ce - c), cum_end, cum)
    at = each(lambda x, c, w: stack(x * jnp.exp(c - w)), a, cum, lw)
    rt = each(lambda x, c: stack(x * jnp.exp(c)), r, cum)
    bt = each(lambda x, e: stack(x * e), b, e_neg)
    kt = each(lambda x, e: stack(x * e), k, e_neg)
    vs = each(stack, v)
    gram = each(lambda p, q, s, t: _nt(jnp.concatenate([p, q], axis=0), jnp.concatenate([s, t], axis=0)),
                at, rt, bt, kt)
    nmat = each(lambda g: jnp.where(strict2, g[:L2, :L2], 0.0), gram)
    a_k = each(lambda g: jnp.where(strict2, g[:L2, L2:], 0.0), gram)
    r_bk = each(lambda g: jnp.where(jnp.concatenate([incl2, incl2], axis=1), g[L2:, :], 0.0), gram)
    tinv = each(lambda n: eye2 + n, nmat)
    pw = each(lambda n: _nn(n, n), nmat)
    for lev in range(1, n_levels):
        if lev < n_levels - 1:
            both = each(lambda p, t: _nn(p, jnp.concatenate([p, t], axis=1)), pw, tinv)
            pw = each(lambda x: x[:, :L2], both)
            tinv = each(lambda t, x: t + x[:, L2:], tinv, both)
        else:
            tinv = each(lambda p, t: t + _nn(p, t), pw, tinv)
    akv = each(_nn, a_k, vs)
    wu = each(lambda t, p, q: _nn(t, jnp.concatenate([p, q], axis=1)), tinv, at, akv)
    zs = each(lambda x, y: jnp.concatenate([x, jnp.concatenate([jnp.zeros_like(y), y], axis=1)], axis=0), wu, vs)
    qy = each(_nn, r_bk, zs)
    md = each(lambda x, y, e, z: _tn(jnp.concatenate([stack(x * e), stack(y * e)], axis=0), z),
              b, k, e_end, zs)
    dec = each(lambda ce: jnp.sum(jnp.where(eye_w, jnp.broadcast_to(jnp.exp(ce), (W, W)), 0.0),
                                  axis=1, keepdims=True), cum_end)
    hs = [h_ref[p] for p in range(npair)]
    for i in range(nck):
        for p in range(npair):
            s = p * nck + i
            ys = _nn(rt[s] + qy[s][:, :W], hs[p]) + qy[s][:, W:]
            y_ref[where[s][0], where[s][1]] = ys[:L, :] + ys[L:, :]
            hs[p] = dec[s] * hs[p] + _nn(md[s][:, :W], hs[p]) + md[s][:, W:]
    for p in range(npair):
        h_ref[p] = hs[p]


def _rwkv_scan(r, k, v, a, b, lw, n_ctx):
    bsz, t, c = r.shape
    w = 2 * D_HEAD_DIM
    L = RWKV_CHUNK
    tb = SCAN_TIME_BLOCK
    nck, nb, ncb = tb // L, t // tb, n_ctx // tb
    npair = RWKV_PAIRS_PER_STEP
    shared = pl.BlockSpec((None, tb, npair * w), lambda d, i, p, j: (i, _time_index(d, j, ncb, nb), p))
    per_dir = pl.BlockSpec((None, None, tb, npair * w), lambda d, i, p, j: (d, i, _time_index(d, j, ncb, nb), p))
    return pl.pallas_call(
        functools.partial(_rwkv_kernel, L=L, nck=nck),
        grid=(2, bsz, c // (npair * w), nb),
        in_specs=[shared] * 5 + [per_dir],
        out_specs=per_dir,
        out_shape=jax.ShapeDtypeStruct((2, bsz, t, c), F32),
        scratch_shapes=[pltpu.VMEM((npair, w, w), F32)],
        compiler_params=pltpu.CompilerParams(
            dimension_semantics=("arbitrary",) * 4,
            vmem_limit_bytes=VMEM_LIMIT_BYTES),
        name="rwkv7_scan",
    )(r, k, v, a, b, lw)


def _neighbours(x, n_ctx):
    pos = jnp.arange(x.shape[1])[None, :, None]
    prev = jnp.pad(x[:, :-1], ((0, 0), (1, 0), (0, 0)))
    nxt = jnp.pad(x[:, 1:], ((0, 0), (0, 1), (0, 0)))
    return jnp.where(pos == n_ctx, 0.0, prev), jnp.where(pos == n_ctx - 1, 0.0, nxt)


def _dwconv3(x, w, b, n_ctx):
    prev, nxt = _neighbours(x, n_ctx)
    return prev * w[0] + x * w[1] + nxt * w[2] + b


def _layer_norm(x, g, b):
    mu = x.mean(-1, keepdims=True)
    var = jnp.mean(jnp.square(x - mu), -1, keepdims=True)
    return (x - mu) * lax.rsqrt(var + LN_EPS) * g + b


def _head_norm(y, eps, center):
    if center:
        y = y - y.mean(-1, keepdims=True)
    y = y * lax.rsqrt(jnp.mean(y * y, -1, keepdims=True) + eps)
    return y.reshape(y.shape[0], y.shape[1], -1)


def _ssd_mixer(p, n_ctx, params):
    conv_w, conv_b, dt_bias, a_log, d_skip, norm_w = params
    b, t, _ = p.shape
    a = -jnp.exp(a_log)
    z = p[..., :A_INNER]
    xbc = jax.nn.silu(_dwconv3(p[..., A_INNER:A_INNER + A_XBC], conv_w, conv_b, n_ctx))
    xs = xbc[..., :A_INNER].reshape(b, t, A_HEADS, A_HEAD_DIM)
    bm = xbc[..., A_INNER:A_INNER + A_GROUPS * A_STATE]
    cm = xbc[..., A_INNER + A_GROUPS * A_STATE:]
    dt = jnp.moveaxis(jax.nn.softplus(p[..., A_INNER + A_XBC:].reshape(b, t, 2, A_HEADS) + dt_bias), 2, 0)
    x = (xs[None] * dt[..., None]).reshape(2, b, t, A_INNER)
    y = _ssd_scan(x, dt * a[:, None, None, :], bm, cm, n_ctx)
    y = ((y[0] + y[1]).reshape(b, t, A_HEADS, A_HEAD_DIM) + d_skip[:, None] * xs).reshape(b, t, A_INNER) * jax.nn.silu(z)
    return y * lax.rsqrt(jnp.mean(y * y, -1, keepdims=True) + 1e-6) * norm_w


def _mlstm_mixer(p, n_ctx, params):
    conv_w, conv_b, i_bias, f_bias, norm_w = params
    b, t, _ = p.shape
    qk = jax.nn.silu(_dwconv3(p[..., :2 * B_QK], conv_w, conv_b, n_ctx))
    v = p[..., 2 * B_QK:2 * B_QK + B_INNER]
    o = p[..., 2 * B_QK + B_INNER:2 * B_QK + 2 * B_INNER]
    gates = p[..., 2 * B_QK + 2 * B_INNER:].reshape(b, t, 2, 2, B_HEADS)
    log_i = gates[:, :, 0] + i_bias
    log_f = jax.nn.log_sigmoid(gates[:, :, 1] + f_bias)
    g = jnp.moveaxis(jnp.concatenate([log_i, log_f], axis=-1), 2, 0)
    h = _mlstm_scan(qk[..., :B_QK], qk[..., B_QK:], v, g, n_ctx)
    h = (h[0] + h[1]).reshape(b, t, B_HEADS, B_V_DIM)
    return jax.nn.sigmoid(o) * _head_norm(h, MLSTM_EPS, True) * norm_w


def _hgrn2_mixer(p, n_ctx, lb, params):
    f_bias, norm_w = params
    b, t, _ = p.shape
    q = jax.nn.silu(p[..., :C_INNER])
    f_pre = jnp.moveaxis(p[..., C_INNER:3 * C_INNER].reshape(b, t, 2, C_INNER), 2, 0) + f_bias[:, None, None, :]
    log_f = jnp.log(lb + (1 - lb) * jax.nn.sigmoid(f_pre))
    k = (1 - lb) * jax.nn.sigmoid(-f_pre)
    o = _gla_scan(q, p[..., 3 * C_INNER:4 * C_INNER], k, log_f, n_ctx)
    o = (o[0] + o[1]).reshape(b, t, C_HEADS, C_HEAD_DIM)
    return _head_norm(o, 1e-6, False) * norm_w * jax.nn.silu(p[..., 4 * C_INNER:])


def _rwkv7_mixer(p, n_ctx, params):
    mu, w0, w2, a0, a2, g2, k_k, k_a, r_k, ln_w, ln_b = params
    b, t, _ = p.shape
    hd = lambda u: u.reshape(b, t, D_HEADS, D_HEAD_DIM)
    prev, nxt = _neighbours(p, n_ctx)
    p = p + mu * (0.5 * (prev + nxt) - p)
    r = p[..., :D_INNER]
    k = p[..., D_INNER:2 * D_INNER]
    v = p[..., 2 * D_INNER:3 * D_INNER]
    o = 3 * D_INNER
    wl = jnp.tanh(p[..., o:o + 2 * D_W_LORA]).reshape(b * t, 2, D_W_LORA)
    al = p[..., o + 2 * D_W_LORA:o + 2 * D_W_LORA + D_A_LORA]
    gl = p[..., o + 2 * D_W_LORA + D_A_LORA:]
    w = jnp.stack([_matmul(wl[:, d], w2[d]).reshape(b, t, D_INNER) + w0[d] for d in range(2)])
    log_decay = -jnp.exp(-jax.nn.softplus(-w) - 0.5)
    a = jax.nn.sigmoid(a0 + _matmul(al.reshape(b * t, D_A_LORA), a2).reshape(b, t, D_INNER))
    g = _matmul(jax.nn.sigmoid(gl).reshape(b * t, D_G_LORA), g2).reshape(b, t, D_INNER)
    kk = hd(k * k_k)
    kk = (kk * lax.rsqrt(jnp.maximum(jnp.sum(kk * kk, -1, keepdims=True), 1e-12))).reshape(b, t, D_INNER)
    k = k * (1 + (a - 1) * k_a)
    y = _rwkv_scan(r, k, v, -kk, kk * a, log_decay, n_ctx)
    y = hd(y[0] + y[1])
    bonus = (jnp.sum(hd(r) * hd(k) * r_k, axis=-1, keepdims=True) * hd(v)).reshape(b, t, D_INNER)
    return (_head_norm(y, RWKV_EPS, True) * ln_w + ln_b + bonus) * g


def _to_col_major(u, rows):
    b, s, d = u.shape
    return u.reshape(b, rows, GRID_W, d).transpose(0, 2, 1, 3).reshape(b, s, d)


def _from_col_major(u, rows):
    b, s, d = u.shape
    return u.reshape(b, GRID_W, rows, d).transpose(0, 2, 1, 3).reshape(b, s, d)


def _tile_specs(bsz, t, n_ctx):
    tiles_per_seq, ctx_tiles = t // ROW_TILE, n_ctx // ROW_TILE
    mod_row = lambda i: jnp.where(i % tiles_per_seq < ctx_tiles, bsz, i // tiles_per_seq)
    rows = lambda c: pl.BlockSpec((ROW_TILE, c), lambda i: (i, 0))
    whole = lambda a: pl.BlockSpec(a.shape, lambda i: (0,) * a.ndim)
    mod = pl.BlockSpec((None, 6, D_MODEL), lambda i: (mod_row(i), 0, 0))
    params = pltpu.CompilerParams(dimension_semantics=("arbitrary",), vmem_limit_bytes=VMEM_LIMIT_BYTES)
    return rows, whole, mod, params


def _in_proj_kernel(x_ref, mod_ref, w_ref, o_ref):
    m = mod_ref[...]
    o_ref[...] = _nn(x_ref[...] * (1.0 + m[1:2, :]) + m[0:1, :], w_ref[...])


def _in_proj(xa, mods, w, bsz, n_ctx):
    n = w.shape[1]
    n_pad = -(-n // LANES) * LANES
    wb = jnp.pad(w.astype(BF16), ((0, 0), (0, n_pad - n)))
    rows, whole, mod, params = _tile_specs(bsz, xa.shape[0] // bsz, n_ctx)
    return pl.pallas_call(
        _in_proj_kernel,
        grid=(xa.shape[0] // ROW_TILE,),
        in_specs=[rows(D_MODEL), mod, whole(wb)],
        out_specs=rows(n_pad),
        out_shape=jax.ShapeDtypeStruct((xa.shape[0], n_pad), F32),
        compiler_params=params,
        name="in_proj",
    )(xa, mods, wb)


def _norm_rows(z, ln):
    mu = jnp.mean(z, axis=-1, keepdims=True)
    zc = z - mu
    var = jnp.mean(zc * zc, axis=-1, keepdims=True)
    return zc * lax.rsqrt(var + LN_EPS) * ln[0:1, :] + ln[1:2, :]


def _out_proj_kernel(fa_ref, fb_ref, w_ref, x_ref, mod_ref, ln_ref, xo_ref, h_ref, hb_ref):
    ka = fa_ref.shape[1]
    m = mod_ref[...]
    y = _nn(fa_ref[...], w_ref[:ka, :]) + _nn(fb_ref[...], w_ref[ka:, :])
    xn = _norm_rows(DEEPNORM_ALPHA * x_ref[...] + m[2:3, :] * y, ln_ref[...])
    xo_ref[...] = xn
    h = xn * (1.0 + m[4:5, :]) + m[3:4, :]
    h_ref[...] = h
    hb_ref[...] = h.astype(BF16)


def _out_proj(fa, fb, w, xa, mods, ln, bsz, n_ctx):
    t_all, d = xa.shape
    rows, whole, mod, params = _tile_specs(bsz, t_all // bsz, n_ctx)
    wb = w.astype(BF16)
    return pl.pallas_call(
        _out_proj_kernel,
        grid=(t_all // ROW_TILE,),
        in_specs=[rows(fa.shape[1]), rows(fb.shape[1]), whole(wb), rows(d), mod, whole(ln)],
        out_specs=[rows(d), rows(d), rows(d)],
        out_shape=[jax.ShapeDtypeStruct((t_all, d), F32), jax.ShapeDtypeStruct((t_all, d), F32),
                   jax.ShapeDtypeStruct((t_all, d), BF16)],
        compiler_params=params,
        name="out_proj_norm",
    )(fa, fb, wb, xa, mods, ln)


def _ffn_norm_kernel(f_ref, x_ref, mod_ref, ln_ref, xo_ref):
    m = mod_ref[...]
    xo_ref[...] = _norm_rows(DEEPNORM_ALPHA * x_ref[...] + m[5:6, :] * f_ref[...], ln_ref[...])


def _ffn_norm(f, xa, mods, ln, bsz, n_ctx):
    t_all, d = xa.shape
    rows, whole, mod, params = _tile_specs(bsz, t_all // bsz, n_ctx)
    return pl.pallas_call(
        _ffn_norm_kernel,
        grid=(t_all // ROW_TILE,),
        in_specs=[rows(d), rows(d), mod, whole(ln)],
        out_specs=rows(d),
        out_shape=jax.ShapeDtypeStruct((t_all, d), F32),
        compiler_params=params,
        name="ffn_residual_norm",
    )(f, xa, mods, ln)


def kernel(x, c, ctx, c_ctx, mod_w, mod_b, ln_g, ln_b, ev_w_in, ev_w_out, ssd_conv_w, ssd_conv_b, ssd_dt_bias, ssd_a_log, ssd_d, ssd_norm_w, mlstm_conv_w, mlstm_conv_b, mlstm_i_bias, mlstm_f_bias, mlstm_norm_w, od_w_in, od_w_out, hgrn_lb_logits, hgrn_f_bias, hgrn_norm_w, rwkv_mu, rwkv_w0, rwkv_w2, rwkv_a0, rwkv_a2, rwkv_g2, rwkv_k_k, rwkv_k_a, rwkv_r_k, rwkv_ln_w, rwkv_ln_b, router_w, router_bias, exp_w_gate, exp_w_up, exp_w_down):
    bsz, seq, _ = x.shape
    n_ctx = ctx.shape[1]
    rows = seq // GRID_W
    lb_all = jnp.cumsum(jax.nn.softmax(hgrn_lb_logits.astype(F32), axis=0), axis=0)
    lb_all = lb_all - lb_all[0]
    s_c = jax.nn.silu(c)
    s_cc = jax.nn.silu(c_ctx)
    t = n_ctx + seq
    xa = jnp.concatenate([ctx, x], axis=1).reshape(bsz * t, D_MODEL)
    seq3 = lambda a: a.reshape(bsz, t, a.shape[-1])
    flat = lambda a: a.reshape(bsz * t, a.shape[-1])
    lat_order = lambda a, f: flat(jnp.concatenate([seq3(a)[:, :n_ctx], f(seq3(a)[:, n_ctx:], rows)], axis=1))
    for layer in range(DEPTH):
        i = layer // 2
        mods = _matmul(jnp.concatenate([s_c, s_cc[None]], axis=0), mod_w[layer], tm=8, tn=512) + mod_b[layer]
        mods = mods.reshape(bsz + 1, 6, D_MODEL)
        ln = jnp.stack([ln_g[layer], ln_b[layer]], axis=1)
        if layer % 2 == 0:
            p = seq3(_in_proj(xa, mods, ev_w_in[i], bsz, n_ctx))
            fa = _ssd_mixer(p[..., :P_A], n_ctx,
                            (ssd_conv_w[i], ssd_conv_b[i], ssd_dt_bias[i], ssd_a_log[i], ssd_d[i], ssd_norm_w[i]))
            fb = _mlstm_mixer(p[..., P_A:P_A + P_B], n_ctx,
                              (mlstm_conv_w[i], mlstm_conv_b[i], mlstm_i_bias[i], mlstm_f_bias[i], mlstm_norm_w[i]))
            fa, fb, w_out = flat(fa), flat(fb), ev_w_out[i]
        else:
            p = seq3(_in_proj(lat_order(xa, _to_col_major), mods, od_w_in[i], bsz, n_ctx))
            fa = _hgrn2_mixer(p[..., :P_C], n_ctx, lb_all[layer], (hgrn_f_bias[i], hgrn_norm_w[i]))
            fb = _rwkv7_mixer(p[..., P_C:P_C + P_D], n_ctx,
                              (rwkv_mu[i], rwkv_w0[i], rwkv_w2[i], rwkv_a0[i], rwkv_a2[i], rwkv_g2[i],
                               rwkv_k_k[i], rwkv_k_a[i], rwkv_r_k[i], rwkv_ln_w[i], rwkv_ln_b[i]))
            fa, fb, w_out = lat_order(fa, _from_col_major), lat_order(fb, _from_col_major), od_w_out[i]
        xa, h, hb = _out_proj(fa, fb, w_out, xa, mods, ln[0], bsz, n_ctx)
        f = _moe_ffn(h, hb, router_w, router_bias, exp_w_gate[layer], exp_w_up[layer], exp_w_down[layer])
        xa = _ffn_norm(f, xa, mods, ln[1], bsz, n_ctx)
    return seq3(xa)[:, n_ctx:]
```

```python
import functools
import math

import jax
import jax.numpy as jnp
from jax import lax
from jax.experimental import pallas as pl
from jax.experimental.pallas import tpu as pltpu

F32 = jnp.float32
BF16 = jnp.bfloat16

D_MODEL = 1024
DEPTH = 4
GRID_W = 64
A_HEADS = 8
A_HEAD_DIM = 64
A_INNER = A_HEADS * A_HEAD_DIM
A_GROUPS = 2
A_STATE = 64
A_XBC = A_INNER + 2 * A_GROUPS * A_STATE
B_HEADS = 4
B_QK_DIM = 64
B_V_DIM = 128
B_QK = B_HEADS * B_QK_DIM
B_INNER = B_HEADS * B_V_DIM
MLSTM_EPS = 1e-6
C_HEADS = 4
C_HEAD_DIM = 128
C_INNER = C_HEADS * C_HEAD_DIM
D_HEADS = 8
D_HEAD_DIM = 64
D_INNER = D_HEADS * D_HEAD_DIM
D_W_LORA = 64
D_A_LORA = 64
D_G_LORA = 128
RWKV_EPS = 64e-5
P_A = A_INNER + A_XBC + 2 * A_HEADS
P_B = 2 * B_QK + 2 * B_INNER + 4 * B_HEADS
P_C = 5 * C_INNER
P_D = 3 * D_INNER + 2 * D_W_LORA + D_A_LORA + D_G_LORA
N_EXPERTS = 32
N_EXPERT_GROUPS = 8
EXPERTS_PER_GROUP = N_EXPERTS // N_EXPERT_GROUPS
TOP_K = 2
D_EXPERT = 512
MOE_BLOCK = 256
RANK_BLOCK = 512
ROW_TILE = 256
LANES = 128
DEEPNORM_ALPHA = (2 * DEPTH) ** 0.25
LN_EPS = 1e-5
M_INIT = -1e30
NEG_BIG = -1e30

SSD_CHUNK = 128
MLSTM_CHUNK = 128
GLA_CHUNK = 16
GLA_BLOCK = 64
GLA_HEADS_PER_STEP = 4
RWKV_CHUNK = 64
RWKV_PAIRS_PER_STEP = 4
SCAN_TIME_BLOCK = 256

VMEM_LIMIT_BYTES = 48 * 1024 * 1024
HI = lax.Precision.HIGHEST


def _dot(a, b, dims, exact):
    if exact:
        return lax.dot_general(a.astype(F32), b.astype(F32), (dims, ((), ())),
                               precision=HI, preferred_element_type=F32)
    return lax.dot_general(a.astype(BF16), b.astype(BF16), (dims, ((), ())),
                           preferred_element_type=F32)


def _nn(a, b, exact=False):
    return _dot(a, b, ((1,), (0,)), exact)


def _nt(a, b, exact=False):
    return _dot(a, b, ((1,), (1,)), exact)


def _tn(a, b, exact=False):
    return _dot(a, b, ((0,), (0,)), exact)


def _iota2(n, m):
    return (lax.broadcasted_iota(jnp.int32, (n, m), 0),
            lax.broadcasted_iota(jnp.int32, (n, m), 1))


def _split3(x):
    x1 = x.astype(BF16)
    r1 = x - x1.astype(F32)
    x2 = r1.astype(BF16)
    x3 = (r1 - x2.astype(F32)).astype(BF16)
    return x1, x2, x3


def _mask_nn(mask, x):
    mb = mask.astype(BF16)
    x1, x2, x3 = _split3(x)
    return _nn(mb, x1, False) + _nn(mb, x2, False) + _nn(mb, x3, False)


def _nn_mask(x, mask):
    mb = mask.astype(BF16)
    x1, x2, x3 = _split3(x)
    return _nn(x1, mb, False) + _nn(x2, mb, False) + _nn(x3, mb, False)


def _cum_mats(col, row, L):
    ri, ci = _iota2(L, L)
    ccol = _mask_nn(ci <= ri, jnp.broadcast_to(col, (L, L)))
    crow = _nn_mask(jnp.broadcast_to(row, (L, L)), ri <= ci)
    return ccol, crow


def _mm_kernel(x_ref, w_ref, o_ref, *, exact):
    o_ref[...] = _nn(x_ref[...], w_ref[...], exact)


def _matmul(x, w, tm=512, tn=512, exact=False):
    m, k = x.shape
    n = w.shape[1]
    n_pad = -(-n // tn) * tn
    m_pad = -(-m // tm) * tm
    xb = x if exact else x.astype(BF16)
    wb = w if exact else w.astype(BF16)
    if n_pad != n:
        wb = jnp.pad(wb, ((0, 0), (0, n_pad - n)))
    if m_pad != m:
        xb = jnp.pad(xb, ((0, m_pad - m), (0, 0)))
    out = pl.pallas_call(
        functools.partial(_mm_kernel, exact=exact),
        grid=(n_pad // tn, m_pad // tm),
        in_specs=[pl.BlockSpec((tm, k), lambda j, i: (i, 0)),
                  pl.BlockSpec((k, tn), lambda j, i: (0, j))],
        out_specs=pl.BlockSpec((tm, tn), lambda j, i: (i, j)),
        out_shape=jax.ShapeDtypeStruct((m_pad, n_pad), F32),
        compiler_params=pltpu.CompilerParams(
            dimension_semantics=("arbitrary", "arbitrary"),
            vmem_limit_bytes=VMEM_LIMIT_BYTES),
        name="dense_matmul",
    )(xb, wb)
    return out[:m, :n]


def _moe_kernel(blk_e_ref, n_used_ref, x_ref, wg_ref, wu_ref, wd_ref, o_ref):
    i = pl.program_id(0)

    @pl.when(i < n_used_ref[0])
    def _():
        x = x_ref[...]
        g = _nn(x, wg_ref[...])
        u = _nn(x, wu_ref[...])
        o_ref[...] = _nn(g * jax.nn.sigmoid(g) * u, wd_ref[...]).astype(o_ref.dtype)

    @pl.when(i >= n_used_ref[0])
    def _():
        o_ref[...] = jnp.zeros_like(o_ref)


def _moe_experts(xp, blk_e, n_used, w_gate, w_up, w_down, layer):
    n_rows, d = xp.shape
    n_blocks = n_rows // MOE_BLOCK
    grid_spec = pltpu.PrefetchScalarGridSpec(
        num_scalar_prefetch=2,
        grid=(n_blocks,),
        in_specs=[
            pl.BlockSpec((MOE_BLOCK, d), lambda i, be, nu: (i, 0)),
            pl.BlockSpec((None, None, d, D_EXPERT), lambda i, be, nu: (layer, be[i], 0, 0)),
            pl.BlockSpec((None, None, d, D_EXPERT), lambda i, be, nu: (layer, be[i], 0, 0)),
            pl.BlockSpec((None, None, D_EXPERT, d), lambda i, be, nu: (layer, be[i], 0, 0)),
        ],
        out_specs=pl.BlockSpec((MOE_BLOCK, d), lambda i, be, nu: (i, 0)),
    )
    return pl.pallas_call(
        _moe_kernel,
        grid_spec=grid_spec,
        out_shape=jax.ShapeDtypeStruct((n_rows, d), BF16),
        compiler_params=pltpu.CompilerParams(
            dimension_semantics=("arbitrary",),
            vmem_limit_bytes=VMEM_LIMIT_BYTES),
        name="moe_experts",
    )(blk_e, n_used, xp, w_gate, w_up, w_down)


def _top2(vals):
    m = len(vals)
    m1 = functools.reduce(jnp.maximum, vals)
    i1 = jnp.full_like(m1, float(m - 1))
    for j in reversed(range(m - 1)):
        i1 = jnp.where(vals[j] == m1, float(j), i1)
    rest = [jnp.where(i1 == float(j), -jnp.inf, vals[j]) for j in range(m)]
    m2 = functools.reduce(jnp.maximum, rest)
    i2 = jnp.full_like(m1, float(m - 1))
    for j in reversed(range(m - 1)):
        i2 = jnp.where(rest[j] == m2, float(j), i2)
    return m1, i1, m2, i2


def _router_kernel(h_ref, wt_ref, bias_ref, e_ref, w_ref, rank_ref, cnt_ref, carry_ref):
    tm = h_ref.shape[0]
    ng, per = N_EXPERT_GROUPS, EXPERTS_PER_GROUP

    @pl.when(pl.program_id(0) == 0)
    def _():
        carry_ref[...] = jnp.zeros_like(carry_ref)

    s = jax.nn.sigmoid(_nt(wt_ref[...], h_ref[...], True))
    sb = s + bias_ref[...]
    biased = [sb[j * ng:(j + 1) * ng, :] for j in range(per)]
    plain = [s[j * ng:(j + 1) * ng, :] for j in range(per)]
    m1, _, m2, _ = _top2(biased)
    gsum = m1 + m2
    rows = lax.broadcasted_iota(jnp.int32, (ng, tm), 0).astype(F32)
    gmax = jnp.max(gsum, axis=0, keepdims=True)
    gi = jnp.min(jnp.where(gsum == gmax, rows, float(ng)), axis=0, keepdims=True)
    sel = rows == gi
    pick = lambda v: jnp.sum(jnp.where(sel, v, 0.0), axis=0, keepdims=True)
    in_b = [pick(v) for v in biased]
    in_s = [pick(v) for v in plain]
    _, l1, _, l2 = _top2(in_b)
    w1 = functools.reduce(jnp.add, [jnp.where(l1 == float(j), in_s[j], 0.0) for j in range(per)])
    w2 = functools.reduce(jnp.add, [jnp.where(l2 == float(j), in_s[j], 0.0) for j in range(per)])
    e1 = gi * float(per) + l1
    e2 = gi * float(per) + l2
    wsum = w1 + w2
    e_ref[0:1, :] = e1.astype(jnp.int32)
    e_ref[1:2, :] = e2.astype(jnp.int32)
    w_ref[0:1, :] = w1 / wsum
    w_ref[1:2, :] = w2 / wsum
    row = lax.broadcasted_iota(jnp.int32, (N_EXPERTS, tm), 0)
    experts = ((row % ng) * per + row // ng).astype(F32)
    oh1 = jnp.where(experts == e1, 1.0, 0.0)
    oh2 = jnp.where(experts == e2, 1.0, 0.0)
    oh = oh1 + oh2
    ri, ci = _iota2(tm, tm)
    seen = _nn(oh, jnp.where(ri < ci, 1.0, 0.0)) + carry_ref[...]
    rank_ref[0:1, :] = jnp.sum(oh1 * seen, axis=0, keepdims=True).astype(jnp.int32)
    rank_ref[1:2, :] = jnp.sum(oh2 * seen, axis=0, keepdims=True).astype(jnp.int32)
    carry = carry_ref[...] + jnp.sum(oh, axis=1, keepdims=True)
    carry_ref[...] = carry
    cnt_ref[...] = carry.astype(jnp.int32)


def _route(h, router_w, router_bias):
    t, d = h.shape
    tm = RANK_BLOCK
    kt = lambda dt: jax.ShapeDtypeStruct((TOP_K, t), dt)
    blk = pl.BlockSpec((TOP_K, tm), lambda i: (0, i))
    member_major = lambda a: a.reshape(N_EXPERT_GROUPS, EXPERTS_PER_GROUP, -1).transpose(1, 0, 2).reshape(N_EXPERTS, -1)
    e, w, rank, counts = pl.pallas_call(
        _router_kernel,
        grid=(t // tm,),
        in_specs=[pl.BlockSpec((tm, d), lambda i: (i, 0)),
                  pl.BlockSpec((N_EXPERTS, d), lambda i: (0, 0)),
                  pl.BlockSpec((N_EXPERTS, 1), lambda i: (0, 0))],
        out_specs=[blk, blk, blk, pl.BlockSpec((N_EXPERTS, 1), lambda i: (0, 0))],
        out_shape=[kt(jnp.int32), kt(F32), kt(jnp.int32), jax.ShapeDtypeStruct((N_EXPERTS, 1), jnp.int32)],
        scratch_shapes=[pltpu.VMEM((N_EXPERTS, 1), F32)],
        compiler_params=pltpu.CompilerParams(dimension_semantics=("arbitrary",),
                                             vmem_limit_bytes=VMEM_LIMIT_BYTES),
        name="moe_router",
    )(h, member_major(router_w.T), member_major(router_bias.astype(F32).reshape(N_EXPERTS, 1)))
    counts = counts.reshape(EXPERTS_PER_GROUP, N_EXPERT_GROUPS).T.reshape(N_EXPERTS)
    return e, w, rank, counts


def _moe_ffn(h, hb, router_w, router_bias, w_gate, w_up, w_down, layer):
    t, d = h.shape
    expert, wts, rank, counts = _route(h, router_w, router_bias)
    n_assign = t * TOP_K
    padded = (counts + MOE_BLOCK - 1) // MOE_BLOCK * MOE_BLOCK
    pends = jnp.cumsum(padded)
    pstarts = pends - padded
    start_of = jnp.sum(jnp.where(expert[..., None] == jnp.arange(N_EXPERTS, dtype=jnp.int32), pstarts, 0), axis=-1)
    dest = start_of + rank
    n_blocks = -(-n_assign // MOE_BLOCK) + N_EXPERTS
    slot_token = jnp.arange(n_blocks * MOE_BLOCK, dtype=jnp.int32) % t
    for kk in range(TOP_K):
        slot_token = slot_token.at[dest[kk]].set(jnp.arange(t, dtype=jnp.int32), unique_indices=True)
    xp = hb[slot_token]
    blk_start = jnp.arange(n_blocks, dtype=jnp.int32) * MOE_BLOCK
    blk_e = jnp.minimum(jnp.sum(pends[None, :] <= blk_start[:, None], axis=1), N_EXPERTS - 1).astype(jnp.int32)
    n_used = (pends[-1] // MOE_BLOCK).astype(jnp.int32).reshape(1)
    yp = _moe_experts(xp, blk_e, n_used, w_gate, w_up, w_down, layer)
    out = yp[dest[0]] * wts[0][:, None]
    for kk in range(1, TOP_K):
        out = out + yp[dest[kk]] * wts[kk][:, None]
    return out


def _ssd_kernel(x_ref, lac_ref, lar_ref, b_ref, c_ref, y_ref, st_ref, *, L):
    d = pl.program_id(0)
    sgn = 1 - 2 * d
    hp, gn = A_INNER, A_GROUPS * A_STATE
    hpg = A_HEADS // A_GROUPS

    @pl.when(pl.program_id(2) == 0)
    def _():
        st_ref[...] = jnp.zeros_like(st_ref)

    ri, ci = _iota2(L, L)
    before = (ci - ri) * sgn <= 0
    before_t = (ri - ci) * sgn <= 0
    x = x_ref[...]
    bm = b_ref[...]
    cm = c_ref[...]
    ccol = _mask_nn(before, lac_ref[...])
    crow = _nn_mask(lar_ref[...], before_t)
    head_of_lane = lax.broadcasted_iota(jnp.int32, (A_HEADS, hp), 1) // A_HEAD_DIM
    expand = head_of_lane == lax.broadcasted_iota(jnp.int32, (A_HEADS, hp), 0)
    cum = _nn_mask(ccol, expand)
    end = _end_row(cum, d)
    group_of_lane = lax.broadcasted_iota(jnp.int32, (L, gn), 1) // A_STATE
    first_of_pair = (lax.broadcasted_iota(jnp.int32, (L, 2 * A_HEAD_DIM), 1) < A_HEAD_DIM)
    cbs = [_nt(jnp.where(group_of_lane == g, cm, 0.0), bm) for g in range(A_GROUPS)]
    pairs = []
    for p in range(A_HEADS // 2):
        xp = x[:, 2 * p * A_HEAD_DIM:(2 * p + 2) * A_HEAD_DIM]
        ys = []
        for h in (2 * p, 2 * p + 1):
            decay = jnp.exp(jnp.where(before, ccol[:, h:h + 1] - crow[h:h + 1, :], NEG_BIG))
            ys.append(_nn(cbs[h // hpg] * decay, xp))
        pairs.append(jnp.where(first_of_pair, ys[0], ys[1]))
    st = st_ref[...]
    y_ref[...] = jnp.concatenate(pairs, axis=1) + jnp.exp(cum) * _nn(cm, st)
    own_group = (lax.broadcasted_iota(jnp.int32, (gn, hp), 0) // A_STATE
                 == lax.broadcasted_iota(jnp.int32, (gn, hp), 1) // (A_HEAD_DIM * hpg))
    st_ref[...] = jnp.exp(end) * st + jnp.where(own_group, _tn(bm, x * jnp.exp(end - cum)), 0.0)


def _ssd_scan(x, la, bm, cm, n_ctx):
    _, b, t, hp = x.shape
    L = SSD_CHUNK
    nb, ncb = t // L, n_ctx // L
    tix = lambda d, j: _time_index(d, j, ncb, nb)
    shared = pl.BlockSpec((None, L, bm.shape[-1]), lambda d, i, j: (i, tix(d, j), 0))
    return pl.pallas_call(
        functools.partial(_ssd_kernel, L=L),
        grid=(2, b, nb),
        in_specs=[pl.BlockSpec((None, None, L, hp), lambda d, i, j: (d, i, tix(d, j), 0)),
                  pl.BlockSpec((None, None, L, A_HEADS), lambda d, i, j: (d, i, tix(d, j), 0)),
                  pl.BlockSpec((None, None, A_HEADS, L), lambda d, i, j: (d, i, 0, tix(d, j))),
                  shared, shared],
        out_specs=pl.BlockSpec((None, None, L, hp), lambda d, i, j: (d, i, tix(d, j), 0)),
        out_shape=jax.ShapeDtypeStruct((2, b, t, hp), F32),
        scratch_shapes=[pltpu.VMEM((bm.shape[-1], hp), F32)],
        compiler_params=pltpu.CompilerParams(
            dimension_semantics=("arbitrary",) * 3,
            vmem_limit_bytes=VMEM_LIMIT_BYTES),
        name="ssd_scan",
    )(x, la, jnp.swapaxes(la, 2, 3), bm, cm)


def _mlstm_kernel(q_ref, k_ref, v_ref, gc_ref, gr_ref, h_ref, c_ref, n_ref, m_ref, *, L):
    d = pl.program_id(0)
    sgn = 1 - 2 * d
    nh, dk, dv = B_HEADS, B_QK_DIM, B_V_DIM

    @pl.when(pl.program_id(2) == 0)
    def _():
        c_ref[...] = jnp.zeros_like(c_ref)
        n_ref[...] = jnp.zeros_like(n_ref)
        m_ref[...] = jnp.full_like(m_ref, M_INIT)

    ri, ci = _iota2(L, L)
    before = (ci - ri) * sgn <= 0
    before_t = (ri - ci) * sgn <= 0
    gc = gc_ref[...]
    gr = gr_ref[...]
    fcol = _mask_nn(before, gc[:, nh:])
    frow = _nn_mask(gr[nh:, :], before_t)
    lane_head = lax.broadcasted_iota(jnp.int32, (L, 2 * dk), 1) // dk
    each = lambda f, *cols: [f(*xs) for xs in zip(*cols)]
    heads = list(range(nh))
    slab = [slice((h // 2) * 2 * dk, (h // 2 + 1) * 2 * dk) for h in heads]
    q = [jnp.where(lane_head == h % 2, q_ref[:, slab[h]], 0.0) * (dk ** -0.5) for h in heads]
    k = [k_ref[:, slab[h]] for h in heads]
    v = [v_ref[:, h * dv:(h + 1) * dv] for h in heads]
    li_c = [gc[:, h:h + 1] for h in heads]
    li_r = [gr[h:h + 1, :] for h in heads]
    f_c = [fcol[:, h:h + 1] for h in heads]
    f_r = [frow[h:h + 1, :] for h in heads]
    ftot = each(lambda x: _end_row(x, d), f_c)
    c_prev = [c_ref[h] for h in heads]
    n_prev = [n_ref[h] for h in heads]
    m_prev = [m_ref[h] for h in heads]
    w_end = each(lambda ft, fc, lc: ft - fc + lc, ftot, f_c, li_c)
    m_loc = each(lambda w: jnp.max(w, axis=0, keepdims=True), w_end)
    ke = each(lambda x, w, m: x * jnp.exp(w - m), k, w_end, m_loc)
    c_loc = each(_tn, ke, v)
    n_loc = each(lambda x: jnp.sum(x, axis=0, keepdims=True), ke)
    log_d = each(lambda fc, fr, lr: jnp.where(before, fc - fr + lr, NEG_BIG), f_c, f_r, li_r)
    log_inter = each(jnp.add, f_c, m_prev)
    m_row = each(lambda ld, lint: jnp.maximum(jnp.max(ld, axis=-1, keepdims=True), lint), log_d, log_inter)
    s = each(lambda a, b, ld, mr: _nt(a, b) * jnp.exp(ld - mr), q, k, log_d, m_row)
    inter = each(lambda lint, mr: jnp.exp(lint - mr), log_inter, m_row)
    num = each(lambda ss, vv, it, qq, cp: _nn(ss, vv) + it * _nn(qq, cp), s, v, inter, q, c_prev)
    den = each(lambda ss, it, qq, npv: jnp.sum(ss, axis=-1, keepdims=True)
               + it * jnp.sum(qq * npv, axis=-1, keepdims=True), s, inter, q, n_prev)
    out = each(lambda nu, de, mr: nu / jnp.maximum(jnp.abs(de), jnp.exp(-mr)), num, den, m_row)
    m_new = each(lambda ft, mp, ml: jnp.maximum(ft + mp, ml), ftot, m_prev, m_loc)
    sp = each(lambda ft, mp, mn: jnp.exp(ft + mp - mn), ftot, m_prev, m_new)
    sc = each(lambda ml, mn: jnp.exp(ml - mn), m_loc, m_new)
    for h in heads:
        h_ref[:, h * dv:(h + 1) * dv] = out[h]
        c_ref[h] = sp[h] * c_prev[h] + sc[h] * c_loc[h]
        n_ref[h] = sp[h] * n_prev[h] + sc[h] * n_loc[h]
        m_ref[h] = m_new[h]


def _mlstm_scan(q, k, v, gates, n_ctx):
    b, t, _ = q.shape
    L = MLSTM_CHUNK
    nb, ncb = t // L, n_ctx // L
    tix = lambda d, j: _time_index(d, j, ncb, nb)
    shared = lambda c: pl.BlockSpec((None, L, c), lambda d, i, j: (i, tix(d, j), 0))
    return pl.pallas_call(
        functools.partial(_mlstm_kernel, L=L),
        grid=(2, b, nb),
        in_specs=[shared(B_QK), shared(B_QK), shared(B_INNER),
                  pl.BlockSpec((None, None, L, 2 * B_HEADS), lambda d, i, j: (d, i, tix(d, j), 0)),
                  pl.BlockSpec((None, None, 2 * B_HEADS, L), lambda d, i, j: (d, i, 0, tix(d, j)))],
        out_specs=pl.BlockSpec((None, None, L, B_INNER), lambda d, i, j: (d, i, tix(d, j), 0)),
        out_shape=jax.ShapeDtypeStruct((2, b, t, B_INNER), F32),
        scratch_shapes=[pltpu.VMEM((B_HEADS, 2 * B_QK_DIM, B_V_DIM), F32),
                        pltpu.VMEM((B_HEADS, 1, 2 * B_QK_DIM), F32),
                        pltpu.VMEM((B_HEADS, 1, 1), F32)],
        compiler_params=pltpu.CompilerParams(
            dimension_semantics=("arbitrary",) * 3,
            vmem_limit_bytes=VMEM_LIMIT_BYTES),
        name="mlstm_scan",
    )(q, k, v, gates, jnp.swapaxes(gates, 2, 3))


def _time_index(d, j, n_ctx_blocks, n_blocks):
    rev = jnp.where(j < n_ctx_blocks, n_ctx_blocks - 1 - j, n_blocks - 1 - j + n_ctx_blocks)
    return jnp.where(d == 1, rev, j)


def _end_row(x, d):
    n = x.shape[0]
    return jnp.where(d == 1, x[0:1, :], x[n - 1:n, :])


def _gla_kernel(q_ref, v_ref, k_ref, lf_ref, y_ref, st_ref, *, L, sub, nck):
    d = pl.program_id(0)
    sgn = 1 - 2 * d

    @pl.when(pl.program_id(3) == 0)
    def _():
        st_ref[...] = jnp.zeros_like(st_ref)

    ri, ci = _iota2(L, L)
    before = (ci - ri) * sgn <= 0
    rows = lax.broadcasted_iota(jnp.int32, (L, 1), 0)

    each = lambda f, *cols: [f(*xs) for xs in zip(*cols)]
    hd = C_HEAD_DIM
    nhead = st_ref.shape[0]
    slices = [pl.ds(pl.multiple_of((i + d * (nck - 1 - 2 * i)) * L, L), L) for i in range(nck)]
    where = [(sl, slice(h * hd, (h + 1) * hd)) for h in range(nhead) for sl in slices]
    q, k, v, lf = ([ref[sl, lanes] for sl, lanes in where] for ref in (q_ref, k_ref, v_ref, lf_ref))
    lam = each(lambda x: _mask_nn(before, x), lf)
    lam_end = each(lambda x: _end_row(x, d), lam)
    blocks = [[] for _ in where]
    for c in range(L // sub):
        lo, hi = c * sub, (c + 1) * sub
        upto = jnp.where(d == 1, lo - 1 - rows, rows - hi) < 0
        for s in range(len(where)):
            zero = jnp.zeros_like(lam_end[s])
            ref_f = lam[s][lo - 1:lo, :] if lo > 0 else zero
            ref_b = lam[s][hi:hi + 1, :] if hi < L else zero
            ref = jnp.where(d == 1, ref_b, ref_f)
            qc = q[s][lo:hi, :] * jnp.exp(lam[s][lo:hi, :] - ref)
            kc = k[s] * jnp.exp(jnp.where(upto, ref - lam[s], NEG_BIG))
            blocks[s].append(_nt(qc, kc))
    att = each(lambda bl: jnp.where(before, jnp.concatenate(bl, axis=0), 0.0), blocks)
    y_intra = each(_nn, att, v)
    q_in = each(lambda x, l: x * jnp.exp(l), q, lam)
    kv = each(lambda x, y, l, le: _tn(x, y * jnp.exp(le - l)), v, k, lam, lam_end)
    dec = each(jnp.exp, lam_end)
    for h in range(nhead):
        st = st_ref[h]
        for i in range(nck):
            s = h * nck + i
            y_ref[where[s][0], where[s][1]] = y_intra[s] + _nt(q_in[s], st)
            st = st * dec[s] + kv[s]
        st_ref[h] = st


def _gla_scan(q, v, k, lf, n_ctx):
    b, t, c = q.shape
    hd = C_HEAD_DIM
    L = GLA_BLOCK
    tb = SCAN_TIME_BLOCK
    nck, nb, ncb = tb // L, t // tb, n_ctx // tb
    nhead = GLA_HEADS_PER_STEP
    shared = pl.BlockSpec((None, tb, nhead * hd), lambda d, i, h, j: (i, _time_index(d, j, ncb, nb), h))
    per_dir = pl.BlockSpec((None, None, tb, nhead * hd), lambda d, i, h, j: (d, i, _time_index(d, j, ncb, nb), h))
    return pl.pallas_call(
        functools.partial(_gla_kernel, L=L, sub=GLA_CHUNK, nck=nck),
        grid=(2, b, c // (nhead * hd), nb),
        in_specs=[shared, shared, per_dir, per_dir],
        out_specs=per_dir,
        out_shape=jax.ShapeDtypeStruct((2, b, t, c), F32),
        scratch_shapes=[pltpu.VMEM((nhead, hd, hd), F32)],
        compiler_params=pltpu.CompilerParams(
            dimension_semantics=("arbitrary",) * 4,
            vmem_limit_bytes=VMEM_LIMIT_BYTES),
        name="gla_scan",
    )(q, v, k, lf)


def _rwkv_kernel(r_ref, k_ref, v_ref, a_ref, b_ref, lw_ref, y_ref, h_ref, *, L, nck):
    d = pl.program_id(0)
    sgn = 1 - 2 * d
    L2 = 2 * L
    W = 2 * D_HEAD_DIM

    @pl.when(pl.program_id(3) == 0)
    def _():
        h_ref[...] = jnp.zeros_like(h_ref)

    ri, ci = _iota2(L, L)
    before = (ci - ri) * sgn <= 0
    r2, c2 = _iota2(L2, L2)
    order2 = ((c2 & (L - 1)) - (r2 & (L - 1))) * sgn
    strict2 = order2 < 0
    incl2 = order2 <= 0
    eye2 = jnp.where(r2 == c2, 1.0, 0.0)
    rw, cw = _iota2(W, W)
    eye_w = rw == cw
    head0 = lax.broadcasted_iota(jnp.int32, (L, W), 1) < D_HEAD_DIM
    stack = lambda x: jnp.concatenate([jnp.where(head0, x, 0.0), jnp.where(head0, 0.0, x)], axis=0)
    n_levels = int(math.log2(L))

    each = lambda f, *cols: [f(*xs) for xs in zip(*cols)]
    npair = h_ref.shape[0]
    slices = [pl.ds(pl.multiple_of((i + d * (nck - 1 - 2 * i)) * L, L), L) for i in range(nck)]
    where = [(sl, slice(p * W, (p + 1) * W)) for p in range(npair) for sl in slices]
    r, k, v, a, b, lw = ([ref[sl, lanes] for sl, lanes in where]
                         for ref in (r_ref, k_ref, v_ref, a_ref, b_ref, lw_ref))
    cum = each(lambda x: _mask_nn(before, x), lw)
    cum_end = each(lambda c: _end_row(c, d), cum)
    e_neg = each(lambda c: jnp.exp(-c), cum)
    e_end = each(lambda ce, c: jnp.exp(ce - c), cum_end, cum)
    at = each(lambda x, c, w: stack(x * jnp.exp(c - w)), a, cum, lw)
    rt = each(lambda x, c: stack(x * jnp.exp(c)), r, cum)
    bt = each(lambda x, e: stack(x * e), b, e_neg)
    kt = each(lambda x, e: stack(x * e), k, e_neg)
    vs = each(stack, v)
    gram = each(lambda p, q, s, t: _nt(jnp.concatenate([p, q], axis=0), jnp.concatenate([s, t], axis=0)),
                at, rt, bt, kt)
    nmat = each(lambda g: jnp.where(strict2, g[:L2, :L2], 0.0), gram)
    a_k = each(lambda g: jnp.where(strict2, g[:L2, L2:], 0.0), gram)
    r_bk = each(lambda g: jnp.where(jnp.concatenate([incl2, incl2], axis=1), g[L2:, :], 0.0), gram)
    tinv = each(lambda n: eye2 + n, nmat)
    pw = each(lambda n: _nn(n, n), nmat)
    for lev in range(1, n_levels):
        if lev < n_levels - 1:
            both = each(lambda p, t: _nn(p, jnp.concatenate([p, t], axis=1)), pw, tinv)
            pw = each(lambda x: x[:, :L2], both)
            tinv = each(lambda t, x: t + x[:, L2:], tinv, both)
        else:
            tinv = each(lambda p, t: t + _nn(p, t), pw, tinv)
    akv = each(_nn, a_k, vs)
    wu = each(lambda t, p, q: _nn(t, jnp.concatenate([p, q], axis=1)), tinv, at, akv)
    zs = each(lambda x, y: jnp.concatenate([x, jnp.concatenate([jnp.zeros_like(y), y], axis=1)], axis=0), wu, vs)
    qy = each(_nn, r_bk, zs)
    md = each(lambda x, y, e, z: _tn(jnp.concatenate([stack(x * e), stack(y * e)], axis=0), z),
              b, k, e_end, zs)
    dec = each(lambda ce: jnp.sum(jnp.where(eye_w, jnp.broadcast_to(jnp.exp(ce), (W, W)), 0.0),
                                  axis=1, keepdims=True), cum_end)
    hs = [h_ref[p] for p in range(npair)]
    for i in range(nck):
        for p in range(npair):
            s = p * nck + i
            ys = _nn(rt[s] + qy[s][:, :W], hs[p]) + qy[s][:, W:]
            y_ref[where[s][0], where[s][1]] = ys[:L, :] + ys[L:, :]
            hs[p] = dec[s] * hs[p] + _nn(md[s][:, :W], hs[p]) + md[s][:, W:]
    for p in range(npair):
        h_ref[p] = hs[p]


def _rwkv_scan(r, k, v, a, b, lw, n_ctx):
    bsz, t, c = r.shape
    w = 2 * D_HEAD_DIM
    L = RWKV_CHUNK
    tb = SCAN_TIME_BLOCK
    nck, nb, ncb = tb // L, t // tb, n_ctx // tb
    npair = RWKV_PAIRS_PER_STEP
    shared = pl.BlockSpec((None, tb, npair * w), lambda d, i, p, j: (i, _time_index(d, j, ncb, nb), p))
    per_dir = pl.BlockSpec((None, None, tb, npair * w), lambda d, i, p, j: (d, i, _time_index(d, j, ncb, nb), p))
    return pl.pallas_call(
        functools.partial(_rwkv_kernel, L=L, nck=nck),
        grid=(2, bsz, c // (npair * w), nb),
        in_specs=[shared] * 5 + [per_dir],
        out_specs=per_dir,
        out_shape=jax.ShapeDtypeStruct((2, bsz, t, c), F32),
        scratch_shapes=[pltpu.VMEM((npair, w, w), F32)],
        compiler_params=pltpu.CompilerParams(
            dimension_semantics=("arbitrary",) * 4,
            vmem_limit_bytes=VMEM_LIMIT_BYTES),
        name="rwkv7_scan",
    )(r, k, v, a, b, lw)


def _neighbours(x, n_ctx):
    pos = jnp.arange(x.shape[1])[None, :, None]
    prev = jnp.pad(x[:, :-1], ((0, 0), (1, 0), (0, 0)))
    nxt = jnp.pad(x[:, 1:], ((0, 0), (0, 1), (0, 0)))
    return jnp.where(pos == n_ctx, 0.0, prev), jnp.where(pos == n_ctx - 1, 0.0, nxt)


def _dwconv3(x, w, b, n_ctx):
    prev, nxt = _neighbours(x, n_ctx)
    return prev * w[0] + x * w[1] + nxt * w[2] + b


def _layer_norm(x, g, b):
    mu = x.mean(-1, keepdims=True)
    var = jnp.mean(jnp.square(x - mu), -1, keepdims=True)
    return (x - mu) * lax.rsqrt(var + LN_EPS) * g + b


def _head_norm(y, eps, center):
    if center:
        y = y - y.mean(-1, keepdims=True)
    y = y * lax.rsqrt(jnp.mean(y * y, -1, keepdims=True) + eps)
    return y.reshape(y.shape[0], y.shape[1], -1)


def _ssd_mixer(p, n_ctx, params):
    conv_w, conv_b, dt_bias, a_log, d_skip, norm_w = params
    b, t, _ = p.shape
    a = -jnp.exp(a_log)
    z = p[..., :A_INNER]
    xbc = jax.nn.silu(_dwconv3(p[..., A_INNER:A_INNER + A_XBC], conv_w, conv_b, n_ctx))
    xs = xbc[..., :A_INNER].reshape(b, t, A_HEADS, A_HEAD_DIM)
    bm = xbc[..., A_INNER:A_INNER + A_GROUPS * A_STATE]
    cm = xbc[..., A_INNER + A_GROUPS * A_STATE:]
    dt = jnp.moveaxis(jax.nn.softplus(p[..., A_INNER + A_XBC:].reshape(b, t, 2, A_HEADS) + dt_bias), 2, 0)
    x = (xs[None] * dt[..., None]).reshape(2, b, t, A_INNER)
    y = _ssd_scan(x, dt * a[:, None, None, :], bm, cm, n_ctx)
    y = ((y[0] + y[1]).reshape(b, t, A_HEADS, A_HEAD_DIM) + d_skip[:, None] * xs).reshape(b, t, A_INNER) * jax.nn.silu(z)
    return (y * lax.rsqrt(jnp.mean(y * y, -1, keepdims=True) + 1e-6) * norm_w).astype(BF16)


def _mlstm_mixer(p, n_ctx, params):
    conv_w, conv_b, i_bias, f_bias, norm_w = params
    b, t, _ = p.shape
    qk = jax.nn.silu(_dwconv3(p[..., :2 * B_QK], conv_w, conv_b, n_ctx))
    v = p[..., 2 * B_QK:2 * B_QK + B_INNER]
    o = p[..., 2 * B_QK + B_INNER:2 * B_QK + 2 * B_INNER]
    gates = p[..., 2 * B_QK + 2 * B_INNER:].reshape(b, t, 2, 2, B_HEADS)
    log_i = gates[:, :, 0] + i_bias
    log_f = jax.nn.log_sigmoid(gates[:, :, 1] + f_bias)
    g = jnp.moveaxis(jnp.concatenate([log_i, log_f], axis=-1), 2, 0)
    h = _mlstm_scan(qk[..., :B_QK], qk[..., B_QK:], v, g, n_ctx)
    h = (h[0] + h[1]).reshape(b, t, B_HEADS, B_V_DIM)
    return (jax.nn.sigmoid(o) * _head_norm(h, MLSTM_EPS, True) * norm_w).astype(BF16)


def _hgrn2_mixer(p, n_ctx, lb, params):
    f_bias, norm_w = params
    b, t, _ = p.shape
    q = jax.nn.silu(p[..., :C_INNER])
    f_pre = jnp.moveaxis(p[..., C_INNER:3 * C_INNER].reshape(b, t, 2, C_INNER), 2, 0) + f_bias[:, None, None, :]
    log_f = jnp.log(lb + (1 - lb) * jax.nn.sigmoid(f_pre))
    k = (1 - lb) * jax.nn.sigmoid(-f_pre)
    o = _gla_scan(q, p[..., 3 * C_INNER:4 * C_INNER], k, log_f, n_ctx)
    o = (o[0] + o[1]).reshape(b, t, C_HEADS, C_HEAD_DIM)
    return (_head_norm(o, 1e-6, False) * norm_w * jax.nn.silu(p[..., 4 * C_INNER:])).astype(BF16)


def _rwkv7_mixer(p, n_ctx, params):
    mu, w0, w2, a0, a2, g2, k_k, k_a, r_k, ln_w, ln_b = params
    b, t, _ = p.shape
    hd = lambda u: u.reshape(b, t, D_HEADS, D_HEAD_DIM)
    prev, nxt = _neighbours(p, n_ctx)
    p = p + mu * (0.5 * (prev + nxt) - p)
    r = p[..., :D_INNER]
    k = p[..., D_INNER:2 * D_INNER]
    v = p[..., 2 * D_INNER:3 * D_INNER]
    o = 3 * D_INNER
    wl = jnp.tanh(p[..., o:o + 2 * D_W_LORA]).reshape(b * t, 2, D_W_LORA)
    al = p[..., o + 2 * D_W_LORA:o + 2 * D_W_LORA + D_A_LORA]
    gl = p[..., o + 2 * D_W_LORA + D_A_LORA:]
    w = jnp.stack([_matmul(wl[:, d], w2[d]).reshape(b, t, D_INNER) + w0[d] for d in range(2)])
    log_decay = -jnp.exp(-jax.nn.softplus(-w) - 0.5)
    a = jax.nn.sigmoid(a0 + _matmul(al.reshape(b * t, D_A_LORA), a2).reshape(b, t, D_INNER))
    g = _matmul(jax.nn.sigmoid(gl).reshape(b * t, D_G_LORA), g2).reshape(b, t, D_INNER)
    kk = hd(k * k_k)
    kk = (kk * lax.rsqrt(jnp.maximum(jnp.sum(kk * kk, -1, keepdims=True), 1e-12))).reshape(b, t, D_INNER)
    k = k * (1 + (a - 1) * k_a)
    y = _rwkv_scan(r, k, v, -kk, kk * a, log_decay, n_ctx)
    y = hd(y[0] + y[1])
    bonus = (jnp.sum(hd(r) * hd(k) * r_k, axis=-1, keepdims=True) * hd(v)).reshape(b, t, D_INNER)
    return ((_head_norm(y, RWKV_EPS, True) * ln_w + ln_b + bonus) * g).astype(BF16)


def _to_col_major(u, rows):
    b, s, d = u.shape
    return u.reshape(b, rows, GRID_W, d).transpose(0, 2, 1, 3).reshape(b, s, d)


def _from_col_major(u, rows):
    b, s, d = u.shape
    return u.reshape(b, GRID_W, rows, d).transpose(0, 2, 1, 3).reshape(b, s, d)


def _tile_specs(bsz, t, n_ctx):
    tiles_per_seq, ctx_tiles = t // ROW_TILE, n_ctx // ROW_TILE
    mod_row = lambda i: jnp.where(i % tiles_per_seq < ctx_tiles, bsz, i // tiles_per_seq)
    rows = lambda c: pl.BlockSpec((ROW_TILE, c), lambda i: (i, 0))
    whole = lambda a: pl.BlockSpec(a.shape, lambda i: (0,) * a.ndim)
    mod = pl.BlockSpec((None, 6, D_MODEL), lambda i: (mod_row(i), 0, 0))
    params = pltpu.CompilerParams(dimension_semantics=("arbitrary",), vmem_limit_bytes=VMEM_LIMIT_BYTES)
    return rows, whole, mod, params


def _in_proj_kernel(x_ref, mod_ref, w_ref, o_ref):
    m = mod_ref[...]
    o_ref[...] = _nn(x_ref[...] * (1.0 + m[1:2, :]) + m[0:1, :], w_ref[...])


def _in_proj(xa, mods, w, bsz, n_ctx):
    n = w.shape[1]
    n_pad = -(-n // LANES) * LANES
    wb = jnp.pad(w.astype(BF16), ((0, 0), (0, n_pad - n)))
    rows, whole, mod, params = _tile_specs(bsz, xa.shape[0] // bsz, n_ctx)
    return pl.pallas_call(
        _in_proj_kernel,
        grid=(xa.shape[0] // ROW_TILE,),
        in_specs=[rows(D_MODEL), mod, whole(wb)],
        out_specs=rows(n_pad),
        out_shape=jax.ShapeDtypeStruct((xa.shape[0], n_pad), F32),
        compiler_params=params,
        name="in_proj",
    )(xa, mods, wb)


def _norm_rows(z, ln):
    mu = jnp.mean(z, axis=-1, keepdims=True)
    zc = z - mu
    var = jnp.mean(zc * zc, axis=-1, keepdims=True)
    return zc * lax.rsqrt(var + LN_EPS) * ln[0:1, :] + ln[1:2, :]


def _out_proj_kernel(fa_ref, fb_ref, w_ref, x_ref, mod_ref, ln_ref, xo_ref, h_ref, hb_ref):
    ka = fa_ref.shape[1]
    m = mod_ref[...]
    y = _nn(fa_ref[...], w_ref[:ka, :]) + _nn(fb_ref[...], w_ref[ka:, :])
    xn = _norm_rows(DEEPNORM_ALPHA * x_ref[...] + m[2:3, :] * y, ln_ref[...])
    xo_ref[...] = xn
    h = xn * (1.0 + m[4:5, :]) + m[3:4, :]
    h_ref[...] = h
    hb_ref[...] = h.astype(BF16)


def _out_proj(fa, fb, w, xa, mods, ln, bsz, n_ctx):
    t_all, d = xa.shape
    rows, whole, mod, params = _tile_specs(bsz, t_all // bsz, n_ctx)
    wb = w.astype(BF16)
    return pl.pallas_call(
        _out_proj_kernel,
        grid=(t_all // ROW_TILE,),
        in_specs=[rows(fa.shape[1]), rows(fb.shape[1]), whole(wb), rows(d), mod, whole(ln)],
        out_specs=[rows(d), rows(d), rows(d)],
        out_shape=[jax.ShapeDtypeStruct((t_all, d), F32), jax.ShapeDtypeStruct((t_all, d), F32),
                   jax.ShapeDtypeStruct((t_all, d), BF16)],
        compiler_params=params,
        name="out_proj_norm",
    )(fa, fb, wb, xa, mods, ln)


def _ffn_norm_kernel(f_ref, x_ref, mod_ref, ln_ref, xo_ref):
    m = mod_ref[...]
    xo_ref[...] = _norm_rows(DEEPNORM_ALPHA * x_ref[...] + m[5:6, :] * f_ref[...], ln_ref[...])


def _ffn_norm(f, xa, mods, ln, bsz, n_ctx):
    t_all, d = xa.shape
    rows, whole, mod, params = _tile_specs(bsz, t_all // bsz, n_ctx)
    return pl.pallas_call(
        _ffn_norm_kernel,
        grid=(t_all // ROW_TILE,),
        in_specs=[rows(d), rows(d), mod, whole(ln)],
        out_specs=rows(d),
        out_shape=jax.ShapeDtypeStruct((t_all, d), F32),
        compiler_params=params,
        name="ffn_residual_norm",
    )(f, xa, mods, ln)


def kernel(x, c, ctx, c_ctx, mod_w, mod_b, ln_g, ln_b, ev_w_in, ev_w_out, ssd_conv_w, ssd_conv_b, ssd_dt_bias, ssd_a_log, ssd_d, ssd_norm_w, mlstm_conv_w, mlstm_conv_b, mlstm_i_bias, mlstm_f_bias, mlstm_norm_w, od_w_in, od_w_out, hgrn_lb_logits, hgrn_f_bias, hgrn_norm_w, rwkv_mu, rwkv_w0, rwkv_w2, rwkv_a0, rwkv_a2, rwkv_g2, rwkv_k_k, rwkv_k_a, rwkv_r_k, rwkv_ln_w, rwkv_ln_b, router_w, router_bias, exp_w_gate, exp_w_up, exp_w_down):
    bsz, seq, _ = x.shape
    n_ctx = ctx.shape[1]
    rows = seq // GRID_W
    lb_all = jnp.cumsum(jax.nn.softmax(hgrn_lb_logits.astype(F32), axis=0), axis=0)
    lb_all = lb_all - lb_all[0]
    s_c = jax.nn.silu(c)
    s_cc = jax.nn.silu(c_ctx)
    t = n_ctx + seq
    xa = jnp.concatenate([ctx, x], axis=1).reshape(bsz * t, D_MODEL)
    seq3 = lambda a: a.reshape(bsz, t, a.shape[-1])
    flat = lambda a: a.reshape(bsz * t, a.shape[-1])
    lat_order = lambda a, f: flat(jnp.concatenate([seq3(a)[:, :n_ctx], f(seq3(a)[:, n_ctx:], rows)], axis=1))
    for layer in range(DEPTH):
        i = layer // 2
        mods = _matmul(jnp.concatenate([s_c, s_cc[None]], axis=0), mod_w[layer], tm=8, tn=512) + mod_b[layer]
        mods = mods.reshape(bsz + 1, 6, D_MODEL)
        ln = jnp.stack([ln_g[layer], ln_b[layer]], axis=1)
        if layer % 2 == 0:
            p = seq3(_in_proj(xa, mods, ev_w_in[i], bsz, n_ctx))
            fa = _ssd_mixer(p[..., :P_A], n_ctx,
                            (ssd_conv_w[i], ssd_conv_b[i], ssd_dt_bias[i], ssd_a_log[i], ssd_d[i], ssd_norm_w[i]))
            fb = _mlstm_mixer(p[..., P_A:P_A + P_B], n_ctx,
                              (mlstm_conv_w[i], mlstm_conv_b[i], mlstm_i_bias[i], mlstm_f_bias[i], mlstm_norm_w[i]))
            fa, fb, w_out = flat(fa), flat(fb), ev_w_out[i]
        else:
            p = seq3(_in_proj(lat_order(xa, _to_col_major), mods, od_w_in[i], bsz, n_ctx))
            fa = _hgrn2_mixer(p[..., :P_C], n_ctx, lb_all[layer], (hgrn_f_bias[i], hgrn_norm_w[i]))
            fb = _rwkv7_mixer(p[..., P_C:P_C + P_D], n_ctx,
                              (rwkv_mu[i], rwkv_w0[i], rwkv_w2[i], rwkv_a0[i], rwkv_a2[i], rwkv_g2[i],
                               rwkv_k_k[i], rwkv_k_a[i], rwkv_r_k[i], rwkv_ln_w[i], rwkv_ln_b[i]))
            fa, fb, w_out = lat_order(fa, _from_col_major), lat_order(fb, _from_col_major), od_w_out[i]
        xa, h, hb = _out_proj(fa, fb, w_out, xa, mods, ln[0], bsz, n_ctx)
        f = _moe_ffn(h, hb, router_w, router_bias, exp_w_gate, exp_w_up, exp_w_down, layer)
        xa = _ffn_norm(f, xa, mods, ln[1], bsz, n_ctx)
    return seq3(xa)[:, n_ctx:]
```

```python
import functools
import math

import jax
import jax.numpy as jnp
from jax import lax
from jax.experimental import pallas as pl
from jax.experimental.pallas import tpu as pltpu

F32 = jnp.float32
BF16 = jnp.bfloat16

D_MODEL = 1024
DEPTH = 4
GRID_W = 64
A_HEADS = 8
A_HEAD_DIM = 64
A_INNER = A_HEADS * A_HEAD_DIM
A_GROUPS = 2
A_STATE = 64
A_XBC = A_INNER + 2 * A_GROUPS * A_STATE
B_HEADS = 4
B_QK_DIM = 64
B_V_DIM = 128
B_QK = B_HEADS * B_QK_DIM
B_INNER = B_HEADS * B_V_DIM
MLSTM_EPS = 1e-6
C_HEADS = 4
C_HEAD_DIM = 128
C_INNER = C_HEADS * C_HEAD_DIM
D_HEADS = 8
D_HEAD_DIM = 64
D_INNER = D_HEADS * D_HEAD_DIM
D_W_LORA = 64
D_A_LORA = 64
D_G_LORA = 128
RWKV_EPS = 64e-5
P_A = A_INNER + A_XBC + 2 * A_HEADS
P_B = 2 * B_QK + 2 * B_INNER + 4 * B_HEADS
P_C = 5 * C_INNER
P_D = 3 * D_INNER + 2 * D_W_LORA + D_A_LORA + D_G_LORA
EV_B0 = 3 * A_INNER
N_EXPERTS = 32
N_EXPERT_GROUPS = 8
EXPERTS_PER_GROUP = N_EXPERTS // N_EXPERT_GROUPS
TOP_K = 2
D_EXPERT = 512
MOE_BLOCK = 256
RANK_BLOCK = 512
ROW_TILE = 256
LANES = 128
DEEPNORM_ALPHA = (2 * DEPTH) ** 0.25
LN_EPS = 1e-5
M_INIT = -1e30
NEG_BIG = -1e30

SSD_CHUNK = 128
MLSTM_CHUNK = 128
GLA_CHUNK = 16
GLA_BLOCK = 64
RWKV_CHUNK = 64
SCAN_TIME_BLOCK = 256

VMEM_LIMIT_BYTES = 48 * 1024 * 1024
HI = lax.Precision.HIGHEST


def _dot(a, b, dims, exact):
    if exact:
        return lax.dot_general(a.astype(F32), b.astype(F32), (dims, ((), ())),
                               precision=HI, preferred_element_type=F32)
    return lax.dot_general(a.astype(BF16), b.astype(BF16), (dims, ((), ())),
                           preferred_element_type=F32)


def _nn(a, b, exact=False):
    return _dot(a, b, ((1,), (0,)), exact)


def _nt(a, b, exact=False):
    return _dot(a, b, ((1,), (1,)), exact)


def _tn(a, b, exact=False):
    return _dot(a, b, ((0,), (0,)), exact)


def _iota2(n, m):
    return (lax.broadcasted_iota(jnp.int32, (n, m), 0),
            lax.broadcasted_iota(jnp.int32, (n, m), 1))


def _split3(x):
    x1 = x.astype(BF16)
    r1 = x - x1.astype(F32)
    x2 = r1.astype(BF16)
    x3 = (r1 - x2.astype(F32)).astype(BF16)
    return x1, x2, x3


def _mask_nn(mask, x):
    mb = mask.astype(BF16)
    x1, x2, x3 = _split3(x)
    return _nn(mb, x1) + _nn(mb, x2) + _nn(mb, x3)


def _nn_mask(x, mask):
    mb = mask.astype(BF16)
    x1, x2, x3 = _split3(x)
    return _nn(x1, mb) + _nn(x2, mb) + _nn(x3, mb)


def _each(f, *cols):
    return [f(*xs) for xs in zip(*cols)]


def _silu(x):
    return x * jax.nn.sigmoid(x)


def _mm_kernel(x_ref, w_ref, o_ref, *, exact):
    o_ref[...] = _nn(x_ref[...], w_ref[...], exact)


def _matmul(x, w, tm=512, tn=512, exact=False):
    m, k = x.shape
    n = w.shape[1]
    n_pad = -(-n // tn) * tn
    m_pad = -(-m // tm) * tm
    xb = x if exact else x.astype(BF16)
    wb = w if exact else w.astype(BF16)
    if n_pad != n:
        wb = jnp.pad(wb, ((0, 0), (0, n_pad - n)))
    if m_pad != m:
        xb = jnp.pad(xb, ((0, m_pad - m), (0, 0)))
    out = pl.pallas_call(
        functools.partial(_mm_kernel, exact=exact),
        grid=(n_pad // tn, m_pad // tm),
        in_specs=[pl.BlockSpec((tm, k), lambda j, i: (i, 0)),
                  pl.BlockSpec((k, tn), lambda j, i: (0, j))],
        out_specs=pl.BlockSpec((tm, tn), lambda j, i: (i, j)),
        out_shape=jax.ShapeDtypeStruct((m_pad, n_pad), F32),
        compiler_params=pltpu.CompilerParams(
            dimension_semantics=("arbitrary", "arbitrary"),
            vmem_limit_bytes=VMEM_LIMIT_BYTES),
        name="dense_matmul",
    )(xb, wb)
    return out[:m, :n]


def _moe_kernel(blk_e_ref, n_used_ref, x_ref, wg_ref, wu_ref, wd_ref, o_ref):
    i = pl.program_id(0)

    @pl.when(i < n_used_ref[0])
    def _():
        x = x_ref[...]
        g = _nn(x, wg_ref[...])
        u = _nn(x, wu_ref[...])
        o_ref[...] = _nn(g * jax.nn.sigmoid(g) * u, wd_ref[...]).astype(o_ref.dtype)

    @pl.when(i >= n_used_ref[0])
    def _():
        o_ref[...] = jnp.zeros_like(o_ref)


def _moe_experts(xp, blk_e, n_used, w_gate, w_up, w_down, layer):
    n_rows, d = xp.shape
    n_blocks = n_rows // MOE_BLOCK
    grid_spec = pltpu.PrefetchScalarGridSpec(
        num_scalar_prefetch=2,
        grid=(n_blocks,),
        in_specs=[
            pl.BlockSpec((MOE_BLOCK, d), lambda i, be, nu: (i, 0)),
            pl.BlockSpec((None, None, d, D_EXPERT), lambda i, be, nu: (layer, be[i], 0, 0)),
            pl.BlockSpec((None, None, d, D_EXPERT), lambda i, be, nu: (layer, be[i], 0, 0)),
            pl.BlockSpec((None, None, D_EXPERT, d), lambda i, be, nu: (layer, be[i], 0, 0)),
        ],
        out_specs=pl.BlockSpec((MOE_BLOCK, d), lambda i, be, nu: (i, 0)),
    )
    return pl.pallas_call(
        _moe_kernel,
        grid_spec=grid_spec,
        out_shape=jax.ShapeDtypeStruct((n_rows, d), BF16),
        compiler_params=pltpu.CompilerParams(
            dimension_semantics=("arbitrary",),
            vmem_limit_bytes=VMEM_LIMIT_BYTES),
        name="moe_experts",
    )(blk_e, n_used, xp, w_gate, w_up, w_down)


def _top2(vals):
    m = len(vals)
    m1 = functools.reduce(jnp.maximum, vals)
    i1 = jnp.full_like(m1, float(m - 1))
    for j in reversed(range(m - 1)):
        i1 = jnp.where(vals[j] == m1, float(j), i1)
    rest = [jnp.where(i1 == float(j), -jnp.inf, vals[j]) for j in range(m)]
    m2 = functools.reduce(jnp.maximum, rest)
    i2 = jnp.full_like(m1, float(m - 1))
    for j in reversed(range(m - 1)):
        i2 = jnp.where(rest[j] == m2, float(j), i2)
    return m1, i1, m2, i2


def _router_kernel(h_ref, wt_ref, bias_ref, e_ref, w_ref, rank_ref, cnt_ref, carry_ref):
    tm = h_ref.shape[0]
    ng, per = N_EXPERT_GROUPS, EXPERTS_PER_GROUP

    @pl.when(pl.program_id(0) == 0)
    def _():
        carry_ref[...] = jnp.zeros_like(carry_ref)

    s = jax.nn.sigmoid(_nt(wt_ref[...], h_ref[...], True))
    sb = s + bias_ref[...]
    biased = [sb[j * ng:(j + 1) * ng, :] for j in range(per)]
    plain = [s[j * ng:(j + 1) * ng, :] for j in range(per)]
    m1, _, m2, _ = _top2(biased)
    gsum = m1 + m2
    rows = lax.broadcasted_iota(jnp.int32, (ng, tm), 0).astype(F32)
    gmax = jnp.max(gsum, axis=0, keepdims=True)
    gi = jnp.min(jnp.where(gsum == gmax, rows, float(ng)), axis=0, keepdims=True)
    sel = rows == gi
    pick = lambda v: jnp.sum(jnp.where(sel, v, 0.0), axis=0, keepdims=True)
    in_b = [pick(v) for v in biased]
    in_s = [pick(v) for v in plain]
    _, l1, _, l2 = _top2(in_b)
    w1 = functools.reduce(jnp.add, [jnp.where(l1 == float(j), in_s[j], 0.0) for j in range(per)])
    w2 = functools.reduce(jnp.add, [jnp.where(l2 == float(j), in_s[j], 0.0) for j in range(per)])
    e1 = gi * float(per) + l1
    e2 = gi * float(per) + l2
    wsum = w1 + w2
    e_ref[0:1, :] = e1.astype(jnp.int32)
    e_ref[1:2, :] = e2.astype(jnp.int32)
    w_ref[0:1, :] = w1 / wsum
    w_ref[1:2, :] = w2 / wsum
    row = lax.broadcasted_iota(jnp.int32, (N_EXPERTS, tm), 0)
    experts = ((row % ng) * per + row // ng).astype(F32)
    oh1 = jnp.where(experts == e1, 1.0, 0.0)
    oh2 = jnp.where(experts == e2, 1.0, 0.0)
    oh = oh1 + oh2
    ri, ci = _iota2(tm, tm)
    seen = _nn(oh, jnp.where(ri < ci, 1.0, 0.0)) + carry_ref[...]
    rank_ref[0:1, :] = jnp.sum(oh1 * seen, axis=0, keepdims=True).astype(jnp.int32)
    rank_ref[1:2, :] = jnp.sum(oh2 * seen, axis=0, keepdims=True).astype(jnp.int32)
    carry = carry_ref[...] + jnp.sum(oh, axis=1, keepdims=True)
    carry_ref[...] = carry
    cnt_ref[...] = carry.astype(jnp.int32)


def _route(h, router_w, router_bias):
    t, d = h.shape
    tm = RANK_BLOCK
    kt = lambda dt: jax.ShapeDtypeStruct((TOP_K, t), dt)
    blk = pl.BlockSpec((TOP_K, tm), lambda i: (0, i))
    member_major = lambda a: a.reshape(N_EXPERT_GROUPS, EXPERTS_PER_GROUP, -1).transpose(1, 0, 2).reshape(N_EXPERTS, -1)
    e, w, rank, counts = pl.pallas_call(
        _router_kernel,
        grid=(t // tm,),
        in_specs=[pl.BlockSpec((tm, d), lambda i: (i, 0)),
                  pl.BlockSpec((N_EXPERTS, d), lambda i: (0, 0)),
                  pl.BlockSpec((N_EXPERTS, 1), lambda i: (0, 0))],
        out_specs=[blk, blk, blk, pl.BlockSpec((N_EXPERTS, 1), lambda i: (0, 0))],
        out_shape=[kt(jnp.int32), kt(F32), kt(jnp.int32), jax.ShapeDtypeStruct((N_EXPERTS, 1), jnp.int32)],
        scratch_shapes=[pltpu.VMEM((N_EXPERTS, 1), F32)],
        compiler_params=pltpu.CompilerParams(dimension_semantics=("arbitrary",),
                                             vmem_limit_bytes=VMEM_LIMIT_BYTES),
        name="moe_router",
    )(h, member_major(router_w.T), member_major(router_bias.astype(F32).reshape(N_EXPERTS, 1)))
    counts = counts.reshape(EXPERTS_PER_GROUP, N_EXPERT_GROUPS).T.reshape(N_EXPERTS)
    return e, w, rank, counts


def _moe_ffn(h, hb, router_w, router_bias, w_gate, w_up, w_down, layer):
    t, d = h.shape
    expert, wts, rank, counts = _route(h, router_w, router_bias)
    n_assign = t * TOP_K
    padded = (counts + MOE_BLOCK - 1) // MOE_BLOCK * MOE_BLOCK
    pends = jnp.cumsum(padded)
    pstarts = pends - padded
    start_of = jnp.sum(jnp.where(expert[..., None] == jnp.arange(N_EXPERTS, dtype=jnp.int32), pstarts, 0), axis=-1)
    dest = start_of + rank
    n_blocks = -(-n_assign // MOE_BLOCK) + N_EXPERTS
    slot_token = jnp.arange(n_blocks * MOE_BLOCK, dtype=jnp.int32) % t
    for kk in range(TOP_K):
        slot_token = slot_token.at[dest[kk]].set(jnp.arange(t, dtype=jnp.int32), unique_indices=True)
    xp = hb[slot_token]
    blk_start = jnp.arange(n_blocks, dtype=jnp.int32) * MOE_BLOCK
    blk_e = jnp.minimum(jnp.sum(pends[None, :] <= blk_start[:, None], axis=1), N_EXPERTS - 1).astype(jnp.int32)
    n_used = (pends[-1] // MOE_BLOCK).astype(jnp.int32).reshape(1)
    yp = _moe_experts(xp, blk_e, n_used, w_gate, w_up, w_down, layer)
    out = yp[dest[0]] * wts[0][:, None]
    for kk in range(1, TOP_K):
        out = out + yp[dest[kk]] * wts[kk][:, None]
    return out


def _time_index(d, j, n_ctx_blocks, n_blocks):
    if d == 0:
        return j
    return jnp.where(j < n_ctx_blocks, n_ctx_blocks - 1 - j, n_blocks - 1 - j + n_ctx_blocks)


def _end_row(x, d):
    n = x.shape[0]
    return x[0:1, :] if d == 1 else x[n - 1:n, :]


def _before(n, d):
    ri, ci = _iota2(n, n)
    return ci >= ri if d == 1 else ci <= ri


def _chunk_slices(nck, L, d):
    order = range(nck - 1, -1, -1) if d == 1 else range(nck)
    return [slice(i * L, (i + 1) * L) for i in order]


def _scan_params(n_axes):
    return pltpu.CompilerParams(dimension_semantics=("arbitrary",) * n_axes, vmem_limit_bytes=VMEM_LIMIT_BYTES)


def _ssd_kernel(*refs, L, d):
    if d == 0:
        x_ref, dt_ref, lac_ref, lar_ref, b_ref, c_ref, y_ref, st_ref = refs
    else:
        x_ref, dt_ref, lac_ref, lar_ref, b_ref, c_ref, y0_ref, z_ref, par_ref, y_ref, st_ref = refs
    hp, gn = A_INNER, A_GROUPS * A_STATE
    hpg = A_HEADS // A_GROUPS

    @pl.when(pl.program_id(1) == 0)
    def _():
        st_ref[...] = jnp.zeros_like(st_ref)

    before = _before(L, d)
    xs = x_ref[...]
    bm = b_ref[...]
    cm = c_ref[...]
    ccol = _mask_nn(before, lac_ref[...])
    crow = _nn_mask(lar_ref[...], _before(L, 1 - d))
    head_of_lane = lax.broadcasted_iota(jnp.int32, (A_HEADS, hp), 1) // A_HEAD_DIM
    expand = head_of_lane == lax.broadcasted_iota(jnp.int32, (A_HEADS, hp), 0)
    cum = _nn_mask(ccol, expand)
    x = xs * _nn_mask(dt_ref[...], expand)
    end = _end_row(cum, d)
    group_of_lane = lax.broadcasted_iota(jnp.int32, (L, gn), 1) // A_STATE
    first_of_pair = (lax.broadcasted_iota(jnp.int32, (L, 2 * A_HEAD_DIM), 1) < A_HEAD_DIM)
    cbs = [_nt(jnp.where(group_of_lane == g, cm, 0.0), bm) for g in range(A_GROUPS)]
    heads = list(range(A_HEADS))
    decay = [jnp.exp(jnp.where(before, ccol[:, h:h + 1] - crow[h:h + 1, :], NEG_BIG)) for h in heads]
    yh = [_nn(cbs[h // hpg] * decay[h], x[:, (h // 2) * 2 * A_HEAD_DIM:(h // 2 + 1) * 2 * A_HEAD_DIM]) for h in heads]
    pairs = [jnp.where(first_of_pair, yh[2 * p], yh[2 * p + 1]) for p in range(A_HEADS // 2)]
    st = st_ref[...]
    y = jnp.concatenate(pairs, axis=1) + jnp.exp(cum) * _nn(cm, st)
    own_group = (lax.broadcasted_iota(jnp.int32, (gn, hp), 0) // A_STATE
                 == lax.broadcasted_iota(jnp.int32, (gn, hp), 1) // (A_HEAD_DIM * hpg))
    st_ref[...] = jnp.exp(end) * st + jnp.where(own_group, _tn(bm, x * jnp.exp(end - cum)), 0.0)
    if d == 0:
        y_ref[...] = y
    else:
        par = par_ref[...]
        u = (y0_ref[...] + y + par[0:1, :] * xs) * _silu(z_ref[...])
        y_ref[...] = (u * lax.rsqrt(jnp.mean(u * u, axis=-1, keepdims=True) + 1e-6) * par[1:2, :]).astype(y_ref.dtype)


def _ssd_scan(xs, dt, la, bm, cm, p, par, n_ctx):
    b, t, hp = xs.shape
    L = SSD_CHUNK
    nb, ncb = t // L, n_ctx // L
    lar = jnp.swapaxes(la, 2, 3)
    y0 = None
    for d in (0, 1):
        tix = lambda j, d=d: _time_index(d, j, ncb, nb)
        seq = lambda c: pl.BlockSpec((None, L, c), lambda i, j: (i, tix(j), 0))
        in_specs = [seq(hp),
                    pl.BlockSpec((None, None, L, A_HEADS), lambda i, j, d=d: (d, i, tix(j), 0)),
                    pl.BlockSpec((None, None, L, A_HEADS), lambda i, j, d=d: (d, i, tix(j), 0)),
                    pl.BlockSpec((None, None, A_HEADS, L), lambda i, j, d=d: (d, i, 0, tix(j))),
                    seq(bm.shape[-1]), seq(bm.shape[-1])]
        args = [xs, dt, la, lar, bm, cm]
        if d == 1:
            in_specs += [seq(hp), seq(hp), pl.BlockSpec(par.shape, lambda i, j: (0, 0))]
            args += [y0, p, par]
        y0 = pl.pallas_call(
            functools.partial(_ssd_kernel, L=L, d=d),
            grid=(b, nb),
            in_specs=in_specs,
            out_specs=seq(hp),
            out_shape=jax.ShapeDtypeStruct((b, t, hp), BF16 if d else F32),
            scratch_shapes=[pltpu.VMEM((bm.shape[-1], hp), F32)],
            compiler_params=_scan_params(2),
            name="ssd_scan",
        )(*args)
    return y0


def _mlstm_kernel(*refs, L, d):
    if d == 0:
        q_ref, k_ref, v_ref, gc_ref, gr_ref, h_ref, c_ref, n_ref, m_ref = refs
    else:
        q_ref, k_ref, v_ref, gc_ref, gr_ref, h0_ref, o_ref, par_ref, h_ref, c_ref, n_ref, m_ref = refs
    nh, dk, dv = B_HEADS, B_QK_DIM, B_V_DIM

    @pl.when(pl.program_id(1) == 0)
    def _():
        c_ref[...] = jnp.zeros_like(c_ref)
        n_ref[...] = jnp.zeros_like(n_ref)
        m_ref[...] = jnp.full_like(m_ref, M_INIT)

    before = _before(L, d)
    gc = gc_ref[...]
    gr = gr_ref[...]
    fcol = _mask_nn(before, gc[:, nh:])
    frow = _nn_mask(gr[nh:, :], _before(L, 1 - d))
    lane_head = lax.broadcasted_iota(jnp.int32, (L, 2 * dk), 1) // dk
    heads = list(range(nh))
    slab = [slice((h // 2) * 2 * dk, (h // 2 + 1) * 2 * dk) for h in heads]
    lanes = [slice(h * dv, (h + 1) * dv) for h in heads]
    q = [jnp.where(lane_head == h % 2, q_ref[:, slab[h]], 0.0) * (dk ** -0.5) for h in heads]
    k = [k_ref[:, slab[h]] for h in heads]
    v = [v_ref[:, lanes[h]] for h in heads]
    li_c = [gc[:, h:h + 1] for h in heads]
    li_r = [gr[h:h + 1, :] for h in heads]
    f_c = [fcol[:, h:h + 1] for h in heads]
    f_r = [frow[h:h + 1, :] for h in heads]
    ftot = _each(lambda x: _end_row(x, d), f_c)
    c_prev = [c_ref[h] for h in heads]
    n_prev = [n_ref[h] for h in heads]
    m_prev = [m_ref[h] for h in heads]
    w_end = _each(lambda ft, fc, lc: ft - fc + lc, ftot, f_c, li_c)
    m_loc = _each(lambda w: jnp.max(w, axis=0, keepdims=True), w_end)
    ke = _each(lambda x, w, m: x * jnp.exp(w - m), k, w_end, m_loc)
    c_loc = _each(_tn, ke, v)
    n_loc = _each(lambda x: jnp.sum(x, axis=0, keepdims=True), ke)
    log_d = _each(lambda fc, fr, lr: jnp.where(before, fc - fr + lr, NEG_BIG), f_c, f_r, li_r)
    log_inter = _each(jnp.add, f_c, m_prev)
    m_row = _each(lambda ld, lint: jnp.maximum(jnp.max(ld, axis=-1, keepdims=True), lint), log_d, log_inter)
    s = _each(lambda a, b, ld, mr: _nt(a, b) * jnp.exp(ld - mr), q, k, log_d, m_row)
    inter = _each(lambda lint, mr: jnp.exp(lint - mr), log_inter, m_row)
    num = _each(lambda ss, vv, it, qq, cp: _nn(ss, vv) + it * _nn(qq, cp), s, v, inter, q, c_prev)
    den = _each(lambda ss, it, qq, npv: jnp.sum(ss, axis=-1, keepdims=True)
                + it * jnp.sum(qq * npv, axis=-1, keepdims=True), s, inter, q, n_prev)
    out = _each(lambda nu, de, mr: nu / jnp.maximum(jnp.abs(de), jnp.exp(-mr)), num, den, m_row)
    m_new = _each(lambda ft, mp, ml: jnp.maximum(ft + mp, ml), ftot, m_prev, m_loc)
    sp = _each(lambda ft, mp, mn: jnp.exp(ft + mp - mn), ftot, m_prev, m_new)
    sc = _each(lambda ml, mn: jnp.exp(ml - mn), m_loc, m_new)
    for h in heads:
        c_ref[h] = sp[h] * c_prev[h] + sc[h] * c_loc[h]
        n_ref[h] = sp[h] * n_prev[h] + sc[h] * n_loc[h]
        m_ref[h] = m_new[h]
        if d == 0:
            h_ref[:, lanes[h]] = out[h]
        else:
            tot = h0_ref[:, lanes[h]] + out[h]
            cen = tot - jnp.mean(tot, axis=-1, keepdims=True)
            nrm = cen * lax.rsqrt(jnp.mean(cen * cen, axis=-1, keepdims=True) + MLSTM_EPS)
            h_ref[:, lanes[h]] = (jax.nn.sigmoid(o_ref[:, lanes[h]]) * nrm * par_ref[:, lanes[h]]).astype(h_ref.dtype)


def _mlstm_scan(q, k, gates, p, par, n_ctx):
    b, t, _ = q.shape
    L = MLSTM_CHUNK
    nb, ncb = t // L, n_ctx // L
    v_blk, o_blk = (EV_B0 + 2 * B_QK) // B_INNER, (EV_B0 + 2 * B_QK + B_INNER) // B_INNER
    gates_r = jnp.swapaxes(gates, 2, 3)
    h0 = None
    for d in (0, 1):
        tix = lambda j, d=d: _time_index(d, j, ncb, nb)
        seq = lambda c, blk=0: pl.BlockSpec((None, L, c), lambda i, j: (i, tix(j), blk))
        in_specs = [seq(B_QK), seq(B_QK), seq(B_INNER, v_blk),
                    pl.BlockSpec((None, None, L, 2 * B_HEADS), lambda i, j, d=d: (d, i, tix(j), 0)),
                    pl.BlockSpec((None, None, 2 * B_HEADS, L), lambda i, j, d=d: (d, i, 0, tix(j)))]
        args = [q, k, p, gates, gates_r]
        if d == 1:
            in_specs += [seq(B_INNER), seq(B_INNER, o_blk), pl.BlockSpec(par.shape, lambda i, j: (0, 0))]
            args += [h0, p, par]
        h0 = pl.pallas_call(
            functools.partial(_mlstm_kernel, L=L, d=d),
            grid=(b, nb),
            in_specs=in_specs,
            out_specs=seq(B_INNER),
            out_shape=jax.ShapeDtypeStruct((b, t, B_INNER), BF16 if d else F32),
            scratch_shapes=[pltpu.VMEM((B_HEADS, 2 * B_QK_DIM, B_V_DIM), F32),
                            pltpu.VMEM((B_HEADS, 1, 2 * B_QK_DIM), F32),
                            pltpu.VMEM((B_HEADS, 1, 1), F32)],
            compiler_params=_scan_params(2),
            name="mlstm_scan",
        )(*args)
    return h0


def _gla_kernel(*refs, L, sub, nck, d):
    if d == 0:
        q_ref, v_ref, k_ref, lf_ref, y_ref, st_ref = refs
    else:
        q_ref, v_ref, k_ref, lf_ref, y0_ref, g_ref, par_ref, y_ref, st_ref = refs
    hd = C_HEAD_DIM
    nhead = st_ref.shape[0]

    @pl.when(pl.program_id(1) == 0)
    def _():
        st_ref[...] = jnp.zeros_like(st_ref)

    before = _before(L, d)
    rows = lax.broadcasted_iota(jnp.int32, (L, 1), 0)
    slices = _chunk_slices(nck, L, d)
    where = [(sl, slice(h * hd, (h + 1) * hd)) for h in range(nhead) for sl in slices]
    q, k, v, lf = ([ref[sl, lanes] for sl, lanes in where] for ref in (q_ref, k_ref, v_ref, lf_ref))
    lam = _each(lambda x: _mask_nn(before, x), lf)
    lam_end = _each(lambda x: _end_row(x, d), lam)
    blocks = [[] for _ in where]
    for c in range(L // sub):
        lo, hi = c * sub, (c + 1) * sub
        upto = rows >= lo if d == 1 else rows < hi
        for s in range(len(where)):
            edge = (hi, hi + 1) if d == 1 else (lo - 1, lo)
            ref = lam[s][edge[0]:edge[1], :] if 0 <= edge[0] < L else jnp.zeros_like(lam_end[s])
            qc = q[s][lo:hi, :] * jnp.exp(lam[s][lo:hi, :] - ref)
            kc = k[s] * jnp.exp(jnp.where(upto, ref - lam[s], NEG_BIG))
            blocks[s].append(_nt(qc, kc))
    att = _each(lambda bl: jnp.where(before, jnp.concatenate(bl, axis=0), 0.0), blocks)
    y_intra = _each(_nn, att, v)
    q_in = _each(lambda x, l: x * jnp.exp(l), q, lam)
    kv = _each(lambda x, y, l, le: _tn(x, y * jnp.exp(le - l)), v, k, lam, lam_end)
    dec = _each(jnp.exp, lam_end)
    for h in range(nhead):
        st = st_ref[h]
        for i in range(nck):
            s = h * nck + i
            sl, lanes = where[s]
            y = y_intra[s] + _nt(q_in[s], st)
            st = st * dec[s] + kv[s]
            if d == 0:
                y_ref[sl, lanes] = y
            else:
                tot = y0_ref[sl, lanes] + y
                nrm = tot * lax.rsqrt(jnp.mean(tot * tot, axis=-1, keepdims=True) + 1e-6)
                y_ref[sl, lanes] = (nrm * par_ref[:, lanes] * _silu(g_ref[sl, lanes])).astype(y_ref.dtype)
        st_ref[h] = st


def _gla_scan(q, k, lf, p, par, n_ctx):
    b, t, c = q.shape
    L = GLA_BLOCK
    tb = SCAN_TIME_BLOCK
    nck, nb, ncb = tb // L, t // tb, n_ctx // tb
    y0 = None
    for d in (0, 1):
        tix = lambda j, d=d: _time_index(d, j, ncb, nb)
        seq = lambda blk=0: pl.BlockSpec((None, tb, c), lambda i, j: (i, tix(j), blk))
        per_dir = pl.BlockSpec((None, None, tb, c), lambda i, j, d=d: (d, i, tix(j), 0))
        in_specs = [seq(), seq(3), per_dir, per_dir]
        args = [q, p, k, lf]
        if d == 1:
            in_specs += [seq(), seq(4), pl.BlockSpec(par.shape, lambda i, j: (0, 0))]
            args += [y0, p, par]
        y0 = pl.pallas_call(
            functools.partial(_gla_kernel, L=L, sub=GLA_CHUNK, nck=nck, d=d),
            grid=(b, nb),
            in_specs=in_specs,
            out_specs=seq(),
            out_shape=jax.ShapeDtypeStruct((b, t, c), BF16 if d else F32),
            scratch_shapes=[pltpu.VMEM((c // C_HEAD_DIM, C_HEAD_DIM, C_HEAD_DIM), F32)],
            compiler_params=_scan_params(2),
            name="gla_scan",
        )(*args)
    return y0


def _rwkv_kernel(*refs, L, nck, d):
    if d == 0:
        r_ref, k_ref, v_ref, a_ref, b_ref, lw_ref, y_ref, h_ref = refs
    else:
        r_ref, k_ref, v_ref, a_ref, b_ref, lw_ref, y0_ref, g_ref, par_ref, y_ref, h_ref = refs
    L2 = 2 * L
    W = 2 * D_HEAD_DIM
    sgn = 1 - 2 * d

    @pl.when(pl.program_id(1) == 0)
    def _():
        h_ref[...] = jnp.zeros_like(h_ref)

    before = _before(L, d)
    r2, c2 = _iota2(L2, L2)
    order2 = ((c2 & (L - 1)) - (r2 & (L - 1))) * sgn
    strict2 = order2 < 0
    incl2 = order2 <= 0
    eye2 = jnp.where(r2 == c2, 1.0, 0.0)
    rw, cw = _iota2(W, W)
    eye_w = rw == cw
    head0 = lax.broadcasted_iota(jnp.int32, (L, W), 1) < D_HEAD_DIM
    stack = lambda x: jnp.concatenate([jnp.where(head0, x, 0.0), jnp.where(head0, 0.0, x)], axis=0)
    n_levels = int(math.log2(L))

    npair = h_ref.shape[0]
    slices = _chunk_slices(nck, L, d)
    where = [(sl, slice(p * W, (p + 1) * W)) for p in range(npair) for sl in slices]
    r, k, v, a, b, lw = ([ref[sl, lanes] for sl, lanes in where]
                         for ref in (r_ref, k_ref, v_ref, a_ref, b_ref, lw_ref))
    cum = _each(lambda x: _mask_nn(before, x), lw)
    cum_end = _each(lambda c: _end_row(c, d), cum)
    e_neg = _each(lambda c: jnp.exp(-c), cum)
    e_end = _each(lambda ce, c: jnp.exp(ce - c), cum_end, cum)
    at = _each(lambda x, c, w: stack(x * jnp.exp(c - w)), a, cum, lw)
    rt = _each(lambda x, c: stack(x * jnp.exp(c)), r, cum)
    bt = _each(lambda x, e: stack(x * e), b, e_neg)
    kt = _each(lambda x, e: stack(x * e), k, e_neg)
    vs = _each(stack, v)
    gram = _each(lambda p, q, s, t: _nt(jnp.concatenate([p, q], axis=0), jnp.concatenate([s, t], axis=0)),
                 at, rt, bt, kt)
    nmat = _each(lambda g: jnp.where(strict2, g[:L2, :L2], 0.0), gram)
    a_k = _each(lambda g: jnp.where(strict2, g[:L2, L2:], 0.0), gram)
    r_bk = _each(lambda g: jnp.where(jnp.concatenate([incl2, incl2], axis=1), g[L2:, :], 0.0), gram)
    tinv = _each(lambda n: eye2 + n, nmat)
    pw = _each(lambda n: _nn(n, n), nmat)
    for lev in range(1, n_levels):
        if lev < n_levels - 1:
            both = _each(lambda p, t: _nn(p, jnp.concatenate([p, t], axis=1)), pw, tinv)
            pw = _each(lambda x: x[:, :L2], both)
            tinv = _each(lambda t, x: t + x[:, L2:], tinv, both)
        else:
            tinv = _each(lambda p, t: t + _nn(p, t), pw, tinv)
    akv = _each(_nn, a_k, vs)
    wu = _each(lambda t, p, q: _nn(t, jnp.concatenate([p, q], axis=1)), tinv, at, akv)
    zs = _each(lambda x, y: jnp.concatenate([x, jnp.concatenate([jnp.zeros_like(y), y], axis=1)], axis=0), wu, vs)
    qy = _each(_nn, r_bk, zs)
    md = _each(lambda x, y, e, z: _tn(jnp.concatenate([stack(x * e), stack(y * e)], axis=0), z),
               b, k, e_end, zs)
    dec = _each(lambda ce: jnp.sum(jnp.where(eye_w, jnp.broadcast_to(jnp.exp(ce), (W, W)), 0.0),
                                   axis=1, keepdims=True), cum_end)

    def head_mean(x):
        m0 = jnp.sum(jnp.where(head0, x, 0.0), axis=-1, keepdims=True)
        m1 = jnp.sum(jnp.where(head0, 0.0, x), axis=-1, keepdims=True)
        return jnp.where(head0, m0, m1) * (1.0 / D_HEAD_DIM)

    hs = [h_ref[p] for p in range(npair)]
    for i in range(nck):
        for p in range(npair):
            s = p * nck + i
            sl, lanes = where[s]
            ys = _nn(rt[s] + qy[s][:, :W], hs[p]) + qy[s][:, W:]
            y = ys[:L, :] + ys[L:, :]
            hs[p] = dec[s] * hs[p] + _nn(md[s][:, :W], hs[p]) + md[s][:, W:]
            if d == 0:
                y_ref[sl, lanes] = y
            else:
                par = par_ref[:, lanes]
                tot = y0_ref[sl, lanes] + y
                cen = tot - head_mean(tot)
                nrm = cen * lax.rsqrt(head_mean(cen * cen) + RWKV_EPS)
                bonus = head_mean(r[s] * k[s] * par[0:1, :]) * float(D_HEAD_DIM) * v[s]
                y_ref[sl, lanes] = ((nrm * par[1:2, :] + par[2:3, :] + bonus) * g_ref[sl, lanes]).astype(y_ref.dtype)
    for p in range(npair):
        h_ref[p] = hs[p]


def _rwkv_scan(r, k, v, a, b, lw, g, par, n_ctx):
    bsz, t, c = r.shape
    L = RWKV_CHUNK
    tb = SCAN_TIME_BLOCK
    nck, nb, ncb = tb // L, t // tb, n_ctx // tb
    y0 = None
    for d in (0, 1):
        tix = lambda j, d=d: _time_index(d, j, ncb, nb)
        seq = pl.BlockSpec((None, tb, c), lambda i, j: (i, tix(j), 0))
        in_specs = [seq] * 5 + [pl.BlockSpec((None, None, tb, c), lambda i, j, d=d: (d, i, tix(j), 0))]
        args = [r, k, v, a, b, lw]
        if d == 1:
            in_specs += [seq, seq, pl.BlockSpec(par.shape, lambda i, j: (0, 0))]
            args += [y0, g, par]
        y0 = pl.pallas_call(
            functools.partial(_rwkv_kernel, L=L, nck=nck, d=d),
            grid=(bsz, nb),
            in_specs=in_specs,
            out_specs=seq,
            out_shape=jax.ShapeDtypeStruct((bsz, t, c), BF16 if d else F32),
            scratch_shapes=[pltpu.VMEM((c // (2 * D_HEAD_DIM), 2 * D_HEAD_DIM, 2 * D_HEAD_DIM), F32)],
            compiler_params=_scan_params(2),
            name="rwkv7_scan",
        )(*args)
    return y0


def _neighbours(x, n_ctx):
    pos = jnp.arange(x.shape[1])[None, :, None]
    prev = jnp.pad(x[:, :-1], ((0, 0), (1, 0), (0, 0)))
    nxt = jnp.pad(x[:, 1:], ((0, 0), (0, 1), (0, 0)))
    return jnp.where(pos == n_ctx, 0.0, prev), jnp.where(pos == n_ctx - 1, 0.0, nxt)


def _dwconv3(x, w, b, n_ctx):
    prev, nxt = _neighbours(x, n_ctx)
    return prev * w[0] + x * w[1] + nxt * w[2] + b


def _ssd_mixer(p, n_ctx, params):
    conv_w, conv_b, dt_bias, a_log, d_skip, norm_w = params
    b, t, _ = p.shape
    a = -jnp.exp(a_log)
    xbc = jax.nn.silu(_dwconv3(p[..., A_INNER:A_INNER + A_XBC], conv_w, conv_b, n_ctx))
    bm = xbc[..., A_INNER:A_INNER + A_GROUPS * A_STATE]
    cm = xbc[..., A_INNER + A_GROUPS * A_STATE:]
    dt = jnp.moveaxis(jax.nn.softplus(p[..., A_INNER + A_XBC:P_A].reshape(b, t, 2, A_HEADS) + dt_bias), 2, 0)
    par = jnp.stack([jnp.repeat(d_skip, A_HEAD_DIM), norm_w])
    return _ssd_scan(xbc[..., :A_INNER], dt, dt * a[:, None, None, :], bm, cm, p, par, n_ctx)


def _mlstm_mixer(p, n_ctx, params):
    conv_w, conv_b, i_bias, f_bias, norm_w = params
    b, t, _ = p.shape
    qk = jax.nn.silu(_dwconv3(p[..., EV_B0:EV_B0 + 2 * B_QK], conv_w, conv_b, n_ctx))
    g0 = EV_B0 + 2 * B_QK + 2 * B_INNER
    gates = p[..., g0:g0 + 4 * B_HEADS].reshape(b, t, 2, 2, B_HEADS)
    log_i = gates[:, :, 0] + i_bias
    log_f = jax.nn.log_sigmoid(gates[:, :, 1] + f_bias)
    g = jnp.moveaxis(jnp.concatenate([log_i, log_f], axis=-1), 2, 0)
    return _mlstm_scan(qk[..., :B_QK], qk[..., B_QK:], g, p, norm_w[None, :], n_ctx)


def _hgrn2_mixer(p, n_ctx, lb, params):
    f_bias, norm_w = params
    b, t, _ = p.shape
    q = jax.nn.silu(p[..., :C_INNER])
    f_pre = jnp.moveaxis(p[..., C_INNER:3 * C_INNER].reshape(b, t, 2, C_INNER), 2, 0) + f_bias[:, None, None, :]
    log_f = jnp.log(lb + (1 - lb) * jax.nn.sigmoid(f_pre))
    k = (1 - lb) * jax.nn.sigmoid(-f_pre)
    return _gla_scan(q, k, log_f, p, norm_w[None, :], n_ctx)


def _rwkv7_mixer(p, n_ctx, params):
    mu, w0, w2, a0, a2, g2, k_k, k_a, r_k, ln_w, ln_b = params
    b, t, _ = p.shape
    hd = lambda u: u.reshape(b, t, D_HEADS, D_HEAD_DIM)
    prev, nxt = _neighbours(p, n_ctx)
    p = p + mu * (0.5 * (prev + nxt) - p)
    r = p[..., :D_INNER]
    k = p[..., D_INNER:2 * D_INNER]
    v = p[..., 2 * D_INNER:3 * D_INNER]
    o = 3 * D_INNER
    wl = jnp.tanh(p[..., o:o + 2 * D_W_LORA]).reshape(b * t, 2, D_W_LORA)
    al = p[..., o + 2 * D_W_LORA:o + 2 * D_W_LORA + D_A_LORA]
    gl = p[..., o + 2 * D_W_LORA + D_A_LORA:]
    w = jnp.stack([_matmul(wl[:, d], w2[d]).reshape(b, t, D_INNER) + w0[d] for d in range(2)])
    log_decay = -jnp.exp(-jax.nn.softplus(-w) - 0.5)
    a = jax.nn.sigmoid(a0 + _matmul(al.reshape(b * t, D_A_LORA), a2).reshape(b, t, D_INNER))
    g = _matmul(jax.nn.sigmoid(gl).reshape(b * t, D_G_LORA), g2).reshape(b, t, D_INNER)
    kk = hd(k * k_k)
    kk = (kk * lax.rsqrt(jnp.maximum(jnp.sum(kk * kk, -1, keepdims=True), 1e-12))).reshape(b, t, D_INNER)
    k = k * (1 + (a - 1) * k_a)
    par = jnp.stack([r_k.reshape(D_INNER), ln_w, ln_b])
    return _rwkv_scan(r, k, v, -kk, kk * a, log_decay, g, par, n_ctx)


def _to_col_major(u, rows):
    b, s, d = u.shape
    return u.reshape(b, rows, GRID_W, d).transpose(0, 2, 1, 3).reshape(b, s, d)


def _from_col_major(u, rows):
    b, s, d = u.shape
    return u.reshape(b, GRID_W, rows, d).transpose(0, 2, 1, 3).reshape(b, s, d)


def _tile_specs(bsz, t, n_ctx):
    tiles_per_seq, ctx_tiles = t // ROW_TILE, n_ctx // ROW_TILE
    mod_row = lambda i: jnp.where(i % tiles_per_seq < ctx_tiles, bsz, i // tiles_per_seq)
    rows = lambda c: pl.BlockSpec((ROW_TILE, c), lambda i: (i, 0))
    whole = lambda a: pl.BlockSpec(a.shape, lambda i: (0,) * a.ndim)
    mod = pl.BlockSpec((None, 6, D_MODEL), lambda i: (mod_row(i), 0, 0))
    params = pltpu.CompilerParams(dimension_semantics=("arbitrary",), vmem_limit_bytes=VMEM_LIMIT_BYTES)
    return rows, whole, mod, params


def _in_proj_kernel(x_ref, mod_ref, w_ref, o_ref):
    m = mod_ref[...]
    o_ref[...] = _nn(x_ref[...] * (1.0 + m[1:2, :]) + m[0:1, :], w_ref[...])


def _in_proj(xa, mods, w, bsz, n_ctx):
    n = w.shape[1]
    n_pad = -(-n // LANES) * LANES
    wb = jnp.pad(w.astype(BF16), ((0, 0), (0, n_pad - n)))
    rows, whole, mod, params = _tile_specs(bsz, xa.shape[0] // bsz, n_ctx)
    return pl.pallas_call(
        _in_proj_kernel,
        grid=(xa.shape[0] // ROW_TILE,),
        in_specs=[rows(D_MODEL), mod, whole(wb)],
        out_specs=rows(n_pad),
        out_shape=jax.ShapeDtypeStruct((xa.shape[0], n_pad), F32),
        compiler_params=params,
        name="in_proj",
    )(xa, mods, wb)


def _norm_rows(z, ln):
    mu = jnp.mean(z, axis=-1, keepdims=True)
    zc = z - mu
    var = jnp.mean(zc * zc, axis=-1, keepdims=True)
    return zc * lax.rsqrt(var + LN_EPS) * ln[0:1, :] + ln[1:2, :]


def _out_proj_kernel(fa_ref, fb_ref, w_ref, x_ref, mod_ref, ln_ref, xo_ref, h_ref, hb_ref):
    ka = fa_ref.shape[1]
    m = mod_ref[...]
    y = _nn(fa_ref[...], w_ref[:ka, :]) + _nn(fb_ref[...], w_ref[ka:, :])
    xn = _norm_rows(DEEPNORM_ALPHA * x_ref[...] + m[2:3, :] * y, ln_ref[...])
    xo_ref[...] = xn
    h = xn * (1.0 + m[4:5, :]) + m[3:4, :]
    h_ref[...] = h
    hb_ref[...] = h.astype(BF16)


def _out_proj(fa, fb, w, xa, mods, ln, bsz, n_ctx):
    t_all, d = xa.shape
    rows, whole, mod, params = _tile_specs(bsz, t_all // bsz, n_ctx)
    wb = w.astype(BF16)
    return pl.pallas_call(
        _out_proj_kernel,
        grid=(t_all // ROW_TILE,),
        in_specs=[rows(fa.shape[1]), rows(fb.shape[1]), whole(wb), rows(d), mod, whole(ln)],
        out_specs=[rows(d), rows(d), rows(d)],
        out_shape=[jax.ShapeDtypeStruct((t_all, d), F32), jax.ShapeDtypeStruct((t_all, d), F32),
                   jax.ShapeDtypeStruct((t_all, d), BF16)],
        compiler_params=params,
        name="out_proj_norm",
    )(fa, fb, wb, xa, mods, ln)


def _ffn_norm_kernel(f_ref, x_ref, mod_ref, ln_ref, xo_ref):
    m = mod_ref[...]
    xo_ref[...] = _norm_rows(DEEPNORM_ALPHA * x_ref[...] + m[5:6, :] * f_ref[...], ln_ref[...])


def _ffn_norm(f, xa, mods, ln, bsz, n_ctx):
    t_all, d = xa.shape
    rows, whole, mod, params = _tile_specs(bsz, t_all // bsz, n_ctx)
    return pl.pallas_call(
        _ffn_norm_kernel,
        grid=(t_all // ROW_TILE,),
        in_specs=[rows(d), rows(d), mod, whole(ln)],
        out_specs=rows(d),
        out_shape=jax.ShapeDtypeStruct((t_all, d), F32),
        compiler_params=params,
        name="ffn_residual_norm",
    )(f, xa, mods, ln)


def kernel(x, c, ctx, c_ctx, mod_w, mod_b, ln_g, ln_b, ev_w_in, ev_w_out, ssd_conv_w, ssd_conv_b, ssd_dt_bias, ssd_a_log, ssd_d, ssd_norm_w, mlstm_conv_w, mlstm_conv_b, mlstm_i_bias, mlstm_f_bias, mlstm_norm_w, od_w_in, od_w_out, hgrn_lb_logits, hgrn_f_bias, hgrn_norm_w, rwkv_mu, rwkv_w0, rwkv_w2, rwkv_a0, rwkv_a2, rwkv_g2, rwkv_k_k, rwkv_k_a, rwkv_r_k, rwkv_ln_w, rwkv_ln_b, router_w, router_bias, exp_w_gate, exp_w_up, exp_w_down):
    bsz, seq, _ = x.shape
    n_ctx = ctx.shape[1]
    rows = seq // GRID_W
    lb_all = jnp.cumsum(jax.nn.softmax(hgrn_lb_logits.astype(F32), axis=0), axis=0)
    lb_all = lb_all - lb_all[0]
    s_c = jax.nn.silu(c)
    s_cc = jax.nn.silu(c_ctx)
    t = n_ctx + seq
    xa = jnp.concatenate([ctx, x], axis=1).reshape(bsz * t, D_MODEL)
    seq3 = lambda a: a.reshape(bsz, t, a.shape[-1])
    flat = lambda a: a.reshape(bsz * t, a.shape[-1])
    lat_order = lambda a, f: flat(jnp.concatenate([seq3(a)[:, :n_ctx], f(seq3(a)[:, n_ctx:], rows)], axis=1))
    for layer in range(DEPTH):
        i = layer // 2
        mods = _matmul(jnp.concatenate([s_c, s_cc[None]], axis=0), mod_w[layer], tm=8, tn=512) + mod_b[layer]
        mods = mods.reshape(bsz + 1, 6, D_MODEL)
        ln = jnp.stack([ln_g[layer], ln_b[layer]], axis=1)
        if layer % 2 == 0:
            w_in = jnp.concatenate([ev_w_in[i][:, :P_A], jnp.zeros((D_MODEL, EV_B0 - P_A), F32),
                                    ev_w_in[i][:, P_A:]], axis=1)
            p = seq3(_in_proj(xa, mods, w_in, bsz, n_ctx))
            fa = _ssd_mixer(p, n_ctx,
                            (ssd_conv_w[i], ssd_conv_b[i], ssd_dt_bias[i], ssd_a_log[i], ssd_d[i], ssd_norm_w[i]))
            fb = _mlstm_mixer(p, n_ctx,
                              (mlstm_conv_w[i], mlstm_conv_b[i], mlstm_i_bias[i], mlstm_f_bias[i], mlstm_norm_w[i]))
            fa, fb, w_out = flat(fa), flat(fb), ev_w_out[i]
        else:
            p = seq3(_in_proj(lat_order(xa, _to_col_major), mods, od_w_in[i], bsz, n_ctx))
            fa = _hgrn2_mixer(p, n_ctx, lb_all[layer], (hgrn_f_bias[i], hgrn_norm_w[i]))
            fb = _rwkv7_mixer(p[..., P_C:P_C + P_D], n_ctx,
                              (rwkv_mu[i], rwkv_w0[i], rwkv_w2[i], rwkv_a0[i], rwkv_a2[i], rwkv_g2[i],
                               rwkv_k_k[i], rwkv_k_a[i], rwkv_r_k[i], rwkv_ln_w[i], rwkv_ln_b[i]))
            fa, fb, w_out = lat_order(fa, _from_col_major), lat_order(fb, _from_col_major), od_w_out[i]
        xa, h, hb = _out_proj(fa, fb, w_out, xa, mods, ln[0], bsz, n_ctx)
        f = _moe_ffn(h, hb, router_w, router_bias, exp_w_gate, exp_w_up, exp_w_down, layer)
        xa = _ffn_norm(f, xa, mods, ln[1], bsz, n_ctx)
    return seq3(xa)[:, n_ctx:]
```

```python
import functools
import math

import jax
import jax.numpy as jnp
from jax import lax
from jax.experimental import pallas as pl
from jax.experimental.pallas import tpu as pltpu

F32 = jnp.float32
BF16 = jnp.bfloat16

D_MODEL = 1024
DEPTH = 4
GRID_W = 64
A_HEADS = 8
A_HEAD_DIM = 64
A_INNER = A_HEADS * A_HEAD_DIM
A_GROUPS = 2
A_STATE = 64
A_XBC = A_INNER + 2 * A_GROUPS * A_STATE
B_HEADS = 4
B_QK_DIM = 64
B_V_DIM = 128
B_QK = B_HEADS * B_QK_DIM
B_INNER = B_HEADS * B_V_DIM
MLSTM_EPS = 1e-6
C_HEADS = 4
C_HEAD_DIM = 128
C_INNER = C_HEADS * C_HEAD_DIM
D_HEADS = 8
D_HEAD_DIM = 64
D_INNER = D_HEADS * D_HEAD_DIM
D_W_LORA = 64
D_A_LORA = 64
D_G_LORA = 128
RWKV_EPS = 64e-5
P_A = A_INNER + A_XBC + 2 * A_HEADS
P_B = 2 * B_QK + 2 * B_INNER + 4 * B_HEADS
P_C = 5 * C_INNER
P_D = 3 * D_INNER + 2 * D_W_LORA + D_A_LORA + D_G_LORA
EV_B0 = 3 * A_INNER
N_EXPERTS = 32
N_EXPERT_GROUPS = 8
EXPERTS_PER_GROUP = N_EXPERTS // N_EXPERT_GROUPS
TOP_K = 2
D_EXPERT = 512
MOE_BLOCK = 256
RANK_BLOCK = 512
ROW_TILE = 256
LANES = 128
DEEPNORM_ALPHA = (2 * DEPTH) ** 0.25
LN_EPS = 1e-5
M_INIT = -1e30
NEG_BIG = -1e30

SSD_CHUNK = 128
MLSTM_CHUNK = 128
GLA_CHUNK = 16
GLA_BLOCK = 64
RWKV_CHUNK = 64
SCAN_TIME_BLOCK = 256

VMEM_LIMIT_BYTES = 48 * 1024 * 1024
HI = lax.Precision.HIGHEST


def _dot(a, b, dims, exact):
    if exact:
        return lax.dot_general(a.astype(F32), b.astype(F32), (dims, ((), ())),
                               precision=HI, preferred_element_type=F32)
    return lax.dot_general(a.astype(BF16), b.astype(BF16), (dims, ((), ())),
                           preferred_element_type=F32)


def _nn(a, b, exact=False):
    return _dot(a, b, ((1,), (0,)), exact)


def _nt(a, b, exact=False):
    return _dot(a, b, ((1,), (1,)), exact)


def _tn(a, b, exact=False):
    return _dot(a, b, ((0,), (0,)), exact)


def _iota2(n, m):
    return (lax.broadcasted_iota(jnp.int32, (n, m), 0),
            lax.broadcasted_iota(jnp.int32, (n, m), 1))


def _split3(x):
    x1 = x.astype(BF16)
    r1 = x - x1.astype(F32)
    x2 = r1.astype(BF16)
    x3 = (r1 - x2.astype(F32)).astype(BF16)
    return x1, x2, x3


def _mask_nn(mask, x):
    mb = mask.astype(BF16)
    x1, x2, x3 = _split3(x)
    return _nn(mb, x1) + _nn(mb, x2) + _nn(mb, x3)


def _nn_mask(x, mask):
    mb = mask.astype(BF16)
    x1, x2, x3 = _split3(x)
    return _nn(x1, mb) + _nn(x2, mb) + _nn(x3, mb)


def _each(f, *cols):
    return [f(*xs) for xs in zip(*cols)]


def _silu(x):
    return x * jax.nn.sigmoid(x)


def _mm_kernel(x_ref, w_ref, o_ref, *, exact):
    o_ref[...] = _nn(x_ref[...], w_ref[...], exact)


def _matmul(x, w, tm=512, tn=512, exact=False):
    m, k = x.shape
    n = w.shape[1]
    n_pad = -(-n // tn) * tn
    m_pad = -(-m // tm) * tm
    xb = x if exact else x.astype(BF16)
    wb = w if exact else w.astype(BF16)
    if n_pad != n:
        wb = jnp.pad(wb, ((0, 0), (0, n_pad - n)))
    if m_pad != m:
        xb = jnp.pad(xb, ((0, m_pad - m), (0, 0)))
    out = pl.pallas_call(
        functools.partial(_mm_kernel, exact=exact),
        grid=(n_pad // tn, m_pad // tm),
        in_specs=[pl.BlockSpec((tm, k), lambda j, i: (i, 0)),
                  pl.BlockSpec((k, tn), lambda j, i: (0, j))],
        out_specs=pl.BlockSpec((tm, tn), lambda j, i: (i, j)),
        out_shape=jax.ShapeDtypeStruct((m_pad, n_pad), F32),
        compiler_params=pltpu.CompilerParams(
            dimension_semantics=("arbitrary", "arbitrary"),
            vmem_limit_bytes=VMEM_LIMIT_BYTES),
        name="dense_matmul",
    )(xb, wb)
    return out[:m, :n]


def _moe_kernel(blk_e_ref, n_used_ref, x_ref, wg_ref, wu_ref, wd_ref, o_ref):
    i = pl.program_id(0)

    @pl.when(i < n_used_ref[0])
    def _():
        x = x_ref[...]
        g = _nn(x, wg_ref[...])
        u = _nn(x, wu_ref[...])
        o_ref[...] = _nn(g * jax.nn.sigmoid(g) * u, wd_ref[...]).astype(o_ref.dtype)

    @pl.when(i >= n_used_ref[0])
    def _():
        o_ref[...] = jnp.zeros_like(o_ref)


def _moe_experts(xp, blk_e, n_used, w_gate, w_up, w_down, layer):
    n_rows, d = xp.shape
    n_blocks = n_rows // MOE_BLOCK
    grid_spec = pltpu.PrefetchScalarGridSpec(
        num_scalar_prefetch=2,
        grid=(n_blocks,),
        in_specs=[
            pl.BlockSpec((MOE_BLOCK, d), lambda i, be, nu: (i, 0)),
            pl.BlockSpec((None, None, d, D_EXPERT), lambda i, be, nu: (layer, be[i], 0, 0)),
            pl.BlockSpec((None, None, d, D_EXPERT), lambda i, be, nu: (layer, be[i], 0, 0)),
            pl.BlockSpec((None, None, D_EXPERT, d), lambda i, be, nu: (layer, be[i], 0, 0)),
        ],
        out_specs=pl.BlockSpec((MOE_BLOCK, d), lambda i, be, nu: (i, 0)),
    )
    return pl.pallas_call(
        _moe_kernel,
        grid_spec=grid_spec,
        out_shape=jax.ShapeDtypeStruct((n_rows, d), BF16),
        compiler_params=pltpu.CompilerParams(
            dimension_semantics=("arbitrary",),
            vmem_limit_bytes=VMEM_LIMIT_BYTES),
        name="moe_experts",
    )(blk_e, n_used, xp, w_gate, w_up, w_down)


def _top2(vals):
    m = len(vals)
    m1 = functools.reduce(jnp.maximum, vals)
    i1 = jnp.full_like(m1, float(m - 1))
    for j in reversed(range(m - 1)):
        i1 = jnp.where(vals[j] == m1, float(j), i1)
    rest = [jnp.where(i1 == float(j), -jnp.inf, vals[j]) for j in range(m)]
    m2 = functools.reduce(jnp.maximum, rest)
    i2 = jnp.full_like(m1, float(m - 1))
    for j in reversed(range(m - 1)):
        i2 = jnp.where(rest[j] == m2, float(j), i2)
    return m1, i1, m2, i2


def _router_kernel(h_ref, wt_ref, bias_ref, e_ref, w_ref, rank_ref, cnt_ref, carry_ref):
    tm = h_ref.shape[0]
    ng, per = N_EXPERT_GROUPS, EXPERTS_PER_GROUP

    @pl.when(pl.program_id(0) == 0)
    def _():
        carry_ref[...] = jnp.zeros_like(carry_ref)

    s = jax.nn.sigmoid(_nt(wt_ref[...], h_ref[...], True))
    sb = s + bias_ref[...]
    biased = [sb[j * ng:(j + 1) * ng, :] for j in range(per)]
    plain = [s[j * ng:(j + 1) * ng, :] for j in range(per)]
    m1, _, m2, _ = _top2(biased)
    gsum = m1 + m2
    rows = lax.broadcasted_iota(jnp.int32, (ng, tm), 0).astype(F32)
    gmax = jnp.max(gsum, axis=0, keepdims=True)
    gi = jnp.min(jnp.where(gsum == gmax, rows, float(ng)), axis=0, keepdims=True)
    sel = rows == gi
    pick = lambda v: jnp.sum(jnp.where(sel, v, 0.0), axis=0, keepdims=True)
    in_b = [pick(v) for v in biased]
    in_s = [pick(v) for v in plain]
    _, l1, _, l2 = _top2(in_b)
    w1 = functools.reduce(jnp.add, [jnp.where(l1 == float(j), in_s[j], 0.0) for j in range(per)])
    w2 = functools.reduce(jnp.add, [jnp.where(l2 == float(j), in_s[j], 0.0) for j in range(per)])
    e1 = gi * float(per) + l1
    e2 = gi * float(per) + l2
    wsum = w1 + w2
    e_ref[0:1, :] = e1.astype(jnp.int32)
    e_ref[1:2, :] = e2.astype(jnp.int32)
    w_ref[0:1, :] = w1 / wsum
    w_ref[1:2, :] = w2 / wsum
    row = lax.broadcasted_iota(jnp.int32, (N_EXPERTS, tm), 0)
    experts = ((row % ng) * per + row // ng).astype(F32)
    oh1 = jnp.where(experts == e1, 1.0, 0.0)
    oh2 = jnp.where(experts == e2, 1.0, 0.0)
    oh = oh1 + oh2
    ri, ci = _iota2(tm, tm)
    seen = _nn(oh, jnp.where(ri < ci, 1.0, 0.0)) + carry_ref[...]
    rank_ref[0:1, :] = jnp.sum(oh1 * seen, axis=0, keepdims=True).astype(jnp.int32)
    rank_ref[1:2, :] = jnp.sum(oh2 * seen, axis=0, keepdims=True).astype(jnp.int32)
    carry = carry_ref[...] + jnp.sum(oh, axis=1, keepdims=True)
    carry_ref[...] = carry
    cnt_ref[...] = carry.astype(jnp.int32)


def _route(h, router_w, router_bias):
    t, d = h.shape
    tm = RANK_BLOCK
    kt = lambda dt: jax.ShapeDtypeStruct((TOP_K, t), dt)
    blk = pl.BlockSpec((TOP_K, tm), lambda i: (0, i))
    member_major = lambda a: a.reshape(N_EXPERT_GROUPS, EXPERTS_PER_GROUP, -1).transpose(1, 0, 2).reshape(N_EXPERTS, -1)
    e, w, rank, counts = pl.pallas_call(
        _router_kernel,
        grid=(t // tm,),
        in_specs=[pl.BlockSpec((tm, d), lambda i: (i, 0)),
                  pl.BlockSpec((N_EXPERTS, d), lambda i: (0, 0)),
                  pl.BlockSpec((N_EXPERTS, 1), lambda i: (0, 0))],
        out_specs=[blk, blk, blk, pl.BlockSpec((N_EXPERTS, 1), lambda i: (0, 0))],
        out_shape=[kt(jnp.int32), kt(F32), kt(jnp.int32), jax.ShapeDtypeStruct((N_EXPERTS, 1), jnp.int32)],
        scratch_shapes=[pltpu.VMEM((N_EXPERTS, 1), F32)],
        compiler_params=pltpu.CompilerParams(dimension_semantics=("arbitrary",),
                                             vmem_limit_bytes=VMEM_LIMIT_BYTES),
        name="moe_router",
    )(h, member_major(router_w.T), member_major(router_bias.astype(F32).reshape(N_EXPERTS, 1)))
    counts = counts.reshape(EXPERTS_PER_GROUP, N_EXPERT_GROUPS).T.reshape(N_EXPERTS)
    return e, w, rank, counts


def _moe_ffn(h, hb, router_w, router_bias, w_gate, w_up, w_down, layer):
    t, d = h.shape
    expert, wts, rank, counts = _route(h, router_w, router_bias)
    n_assign = t * TOP_K
    padded = (counts + MOE_BLOCK - 1) // MOE_BLOCK * MOE_BLOCK
    pends = jnp.cumsum(padded)
    pstarts = pends - padded
    start_of = jnp.sum(jnp.where(expert[..., None] == jnp.arange(N_EXPERTS, dtype=jnp.int32), pstarts, 0), axis=-1)
    dest = start_of + rank
    n_blocks = -(-n_assign // MOE_BLOCK) + N_EXPERTS
    slot_token = jnp.arange(n_blocks * MOE_BLOCK, dtype=jnp.int32) % t
    for kk in range(TOP_K):
        slot_token = slot_token.at[dest[kk]].set(jnp.arange(t, dtype=jnp.int32), unique_indices=True)
    xp = hb[slot_token]
    blk_start = jnp.arange(n_blocks, dtype=jnp.int32) * MOE_BLOCK
    blk_e = jnp.minimum(jnp.sum(pends[None, :] <= blk_start[:, None], axis=1), N_EXPERTS - 1).astype(jnp.int32)
    n_used = (pends[-1] // MOE_BLOCK).astype(jnp.int32).reshape(1)
    yp = _moe_experts(xp, blk_e, n_used, w_gate, w_up, w_down, layer)
    out = yp[dest[0]] * wts[0][:, None]
    for kk in range(1, TOP_K):
        out = out + yp[dest[kk]] * wts[kk][:, None]
    return out


def _time_index(d, j, n_ctx_blocks, n_blocks):
    if d == 0:
        return j
    return jnp.where(j < n_ctx_blocks, n_ctx_blocks - 1 - j, n_blocks - 1 - j + n_ctx_blocks)


def _end_row(x, d):
    n = x.shape[0]
    return x[0:1, :] if d == 1 else x[n - 1:n, :]


def _before(n, d):
    ri, ci = _iota2(n, n)
    return ci >= ri if d == 1 else ci <= ri


def _chunk_slices(nck, L, d):
    order = range(nck - 1, -1, -1) if d == 1 else range(nck)
    return [slice(i * L, (i + 1) * L) for i in order]


def _scan_params(n_axes):
    return pltpu.CompilerParams(dimension_semantics=("arbitrary",) * n_axes, vmem_limit_bytes=VMEM_LIMIT_BYTES)


def _ssd_kernel(*refs, L, d):
    if d == 0:
        x_ref, dt_ref, lac_ref, lar_ref, b_ref, c_ref, y_ref, st_ref = refs
    else:
        x_ref, dt_ref, lac_ref, lar_ref, b_ref, c_ref, y0_ref, z_ref, par_ref, y_ref, st_ref = refs
    hp, gn = A_INNER, A_GROUPS * A_STATE
    hpg = A_HEADS // A_GROUPS

    @pl.when(pl.program_id(1) == 0)
    def _():
        st_ref[...] = jnp.zeros_like(st_ref)

    before = _before(L, d)
    xs = x_ref[...]
    bm = b_ref[...]
    cm = c_ref[...]
    ccol = _mask_nn(before, lac_ref[...])
    crow = _nn_mask(lar_ref[...], _before(L, 1 - d))
    head_of_lane = lax.broadcasted_iota(jnp.int32, (A_HEADS, hp), 1) // A_HEAD_DIM
    expand = head_of_lane == lax.broadcasted_iota(jnp.int32, (A_HEADS, hp), 0)
    cum = _nn_mask(ccol, expand)
    x = xs * _nn_mask(dt_ref[...], expand)
    end = _end_row(cum, d)
    group_of_lane = lax.broadcasted_iota(jnp.int32, (L, gn), 1) // A_STATE
    first_of_pair = (lax.broadcasted_iota(jnp.int32, (L, 2 * A_HEAD_DIM), 1) < A_HEAD_DIM)
    cbs = [_nt(jnp.where(group_of_lane == g, cm, 0.0), bm) for g in range(A_GROUPS)]
    heads = list(range(A_HEADS))
    decay = [jnp.exp(jnp.where(before, ccol[:, h:h + 1] - crow[h:h + 1, :], NEG_BIG)) for h in heads]
    yh = [_nn(cbs[h // hpg] * decay[h], x[:, (h // 2) * 2 * A_HEAD_DIM:(h // 2 + 1) * 2 * A_HEAD_DIM]) for h in heads]
    pairs = [jnp.where(first_of_pair, yh[2 * p], yh[2 * p + 1]) for p in range(A_HEADS // 2)]
    st = st_ref[...]
    y = jnp.concatenate(pairs, axis=1) + jnp.exp(cum) * _nn(cm, st)
    own_group = (lax.broadcasted_iota(jnp.int32, (gn, hp), 0) // A_STATE
                 == lax.broadcasted_iota(jnp.int32, (gn, hp), 1) // (A_HEAD_DIM * hpg))
    st_ref[...] = jnp.exp(end) * st + jnp.where(own_group, _tn(bm, x * jnp.exp(end - cum)), 0.0)
    if d == 0:
        y_ref[...] = y
    else:
        par = par_ref[...]
        u = (y0_ref[...] + y + par[0:1, :] * xs) * _silu(z_ref[...])
        y_ref[...] = (u * lax.rsqrt(jnp.mean(u * u, axis=-1, keepdims=True) + 1e-6) * par[1:2, :]).astype(y_ref.dtype)


def _ssd_scan(xs, dt, la, bm, cm, p, par, n_ctx):
    b, t, hp = xs.shape
    L = SSD_CHUNK
    nb, ncb = t // L, n_ctx // L
    lar = jnp.swapaxes(la, 2, 3)
    y0 = None
    for d in (0, 1):
        tix = lambda j, d=d: _time_index(d, j, ncb, nb)
        seq = lambda c: pl.BlockSpec((None, L, c), lambda i, j: (i, tix(j), 0))
        in_specs = [seq(hp),
                    pl.BlockSpec((None, None, L, A_HEADS), lambda i, j, d=d: (d, i, tix(j), 0)),
                    pl.BlockSpec((None, None, L, A_HEADS), lambda i, j, d=d: (d, i, tix(j), 0)),
                    pl.BlockSpec((None, None, A_HEADS, L), lambda i, j, d=d: (d, i, 0, tix(j))),
                    seq(bm.shape[-1]), seq(bm.shape[-1])]
        args = [xs, dt, la, lar, bm, cm]
        if d == 1:
            in_specs += [seq(hp), seq(hp), pl.BlockSpec(par.shape, lambda i, j: (0, 0))]
            args += [y0, p, par]
        y0 = pl.pallas_call(
            functools.partial(_ssd_kernel, L=L, d=d),
            grid=(b, nb),
            in_specs=in_specs,
            out_specs=seq(hp),
            out_shape=jax.ShapeDtypeStruct((b, t, hp), BF16 if d else F32),
            scratch_shapes=[pltpu.VMEM((bm.shape[-1], hp), F32)],
            compiler_params=_scan_params(2),
            name="ssd_scan",
        )(*args)
    return y0


def _mlstm_kernel(*refs, L, d):
    if d == 0:
        q_ref, k_ref, v_ref, gc_ref, gr_ref, h_ref, c_ref, n_ref, m_ref = refs
    else:
        q_ref, k_ref, v_ref, gc_ref, gr_ref, h0_ref, o_ref, par_ref, h_ref, c_ref, n_ref, m_ref = refs
    nh, dk, dv = B_HEADS, B_QK_DIM, B_V_DIM

    @pl.when(pl.program_id(1) == 0)
    def _():
        c_ref[...] = jnp.zeros_like(c_ref)
        n_ref[...] = jnp.zeros_like(n_ref)
        m_ref[...] = jnp.full_like(m_ref, M_INIT)

    before = _before(L, d)
    gc = gc_ref[...]
    gr = gr_ref[...]
    fcol = _mask_nn(before, gc[:, nh:])
    frow = _nn_mask(gr[nh:, :], _before(L, 1 - d))
    lane_head = lax.broadcasted_iota(jnp.int32, (L, 2 * dk), 1) // dk
    heads = list(range(nh))
    slab = [slice((h // 2) * 2 * dk, (h // 2 + 1) * 2 * dk) for h in heads]
    lanes = [slice(h * dv, (h + 1) * dv) for h in heads]
    q = [jnp.where(lane_head == h % 2, q_ref[:, slab[h]], 0.0) * (dk ** -0.5) for h in heads]
    k = [k_ref[:, slab[h]] for h in heads]
    v = [v_ref[:, lanes[h]] for h in heads]
    li_c = [gc[:, h:h + 1] for h in heads]
    li_r = [gr[h:h + 1, :] for h in heads]
    f_c = [fcol[:, h:h + 1] for h in heads]
    f_r = [frow[h:h + 1, :] for h in heads]
    ftot = _each(lambda x: _end_row(x, d), f_c)
    c_prev = [c_ref[h] for h in heads]
    n_prev = [n_ref[h] for h in heads]
    m_prev = [m_ref[h] for h in heads]
    w_end = _each(lambda ft, fc, lc: ft - fc + lc, ftot, f_c, li_c)
    m_loc = _each(lambda w: jnp.max(w, axis=0, keepdims=True), w_end)
    ke = _each(lambda x, w, m: x * jnp.exp(w - m), k, w_end, m_loc)
    c_loc = _each(_tn, ke, v)
    n_loc = _each(lambda x: jnp.sum(x, axis=0, keepdims=True), ke)
    log_d = _each(lambda fc, fr, lr: jnp.where(before, fc - fr + lr, NEG_BIG), f_c, f_r, li_r)
    log_inter = _each(jnp.add, f_c, m_prev)
    m_row = _each(lambda ld, lint: jnp.maximum(jnp.max(ld, axis=-1, keepdims=True), lint), log_d, log_inter)
    s = _each(lambda a, b, ld, mr: _nt(a, b) * jnp.exp(ld - mr), q, k, log_d, m_row)
    inter = _each(lambda lint, mr: jnp.exp(lint - mr), log_inter, m_row)
    num = _each(lambda ss, vv, it, qq, cp: _nn(ss, vv) + it * _nn(qq, cp), s, v, inter, q, c_prev)
    den = _each(lambda ss, it, qq, npv: jnp.sum(ss, axis=-1, keepdims=True)
                + it * jnp.sum(qq * npv, axis=-1, keepdims=True), s, inter, q, n_prev)
    out = _each(lambda nu, de, mr: nu / jnp.maximum(jnp.abs(de), jnp.exp(-mr)), num, den, m_row)
    m_new = _each(lambda ft, mp, ml: jnp.maximum(ft + mp, ml), ftot, m_prev, m_loc)
    sp = _each(lambda ft, mp, mn: jnp.exp(ft + mp - mn), ftot, m_prev, m_new)
    sc = _each(lambda ml, mn: jnp.exp(ml - mn), m_loc, m_new)
    for h in heads:
        c_ref[h] = sp[h] * c_prev[h] + sc[h] * c_loc[h]
        n_ref[h] = sp[h] * n_prev[h] + sc[h] * n_loc[h]
        m_ref[h] = m_new[h]
        if d == 0:
            h_ref[:, lanes[h]] = out[h]
        else:
            tot = h0_ref[:, lanes[h]] + out[h]
            cen = tot - jnp.mean(tot, axis=-1, keepdims=True)
            nrm = cen * lax.rsqrt(jnp.mean(cen * cen, axis=-1, keepdims=True) + MLSTM_EPS)
            h_ref[:, lanes[h]] = (jax.nn.sigmoid(o_ref[:, lanes[h]]) * nrm * par_ref[:, lanes[h]]).astype(h_ref.dtype)


def _mlstm_scan(q, k, gates, p, par, n_ctx):
    b, t, _ = q.shape
    L = MLSTM_CHUNK
    nb, ncb = t // L, n_ctx // L
    v_blk, o_blk = (EV_B0 + 2 * B_QK) // B_INNER, (EV_B0 + 2 * B_QK + B_INNER) // B_INNER
    gates_r = jnp.swapaxes(gates, 2, 3)
    h0 = None
    for d in (0, 1):
        tix = lambda j, d=d: _time_index(d, j, ncb, nb)
        seq = lambda c, blk=0: pl.BlockSpec((None, L, c), lambda i, j: (i, tix(j), blk))
        in_specs = [seq(B_QK), seq(B_QK), seq(B_INNER, v_blk),
                    pl.BlockSpec((None, None, L, 2 * B_HEADS), lambda i, j, d=d: (d, i, tix(j), 0)),
                    pl.BlockSpec((None, None, 2 * B_HEADS, L), lambda i, j, d=d: (d, i, 0, tix(j)))]
        args = [q, k, p, gates, gates_r]
        if d == 1:
            in_specs += [seq(B_INNER), seq(B_INNER, o_blk), pl.BlockSpec(par.shape, lambda i, j: (0, 0))]
            args += [h0, p, par]
        h0 = pl.pallas_call(
            functools.partial(_mlstm_kernel, L=L, d=d),
            grid=(b, nb),
            in_specs=in_specs,
            out_specs=seq(B_INNER),
            out_shape=jax.ShapeDtypeStruct((b, t, B_INNER), BF16 if d else F32),
            scratch_shapes=[pltpu.VMEM((B_HEADS, 2 * B_QK_DIM, B_V_DIM), F32),
                            pltpu.VMEM((B_HEADS, 1, 2 * B_QK_DIM), F32),
                            pltpu.VMEM((B_HEADS, 1, 1), F32)],
            compiler_params=_scan_params(2),
            name="mlstm_scan",
        )(*args)
    return h0


def _gla_kernel(*refs, L, sub, nck, d):
    if d == 0:
        q_ref, f_ref, v_ref, par_ref, y_ref, st_ref = refs
    else:
        q_ref, f_ref, v_ref, par_ref, y0_ref, g_ref, y_ref, st_ref = refs
    hd = C_HEAD_DIM
    nhead = st_ref.shape[0]

    @pl.when(pl.program_id(1) == 0)
    def _():
        st_ref[...] = jnp.zeros_like(st_ref)

    before = _before(L, d)
    rows = lax.broadcasted_iota(jnp.int32, (L, 1), 0)
    slices = _chunk_slices(nck, L, d)
    where = [(sl, slice(h * hd, (h + 1) * hd)) for h in range(nhead) for sl in slices]
    lb = [par_ref[0:1, lanes] for _, lanes in where]
    f_pre = [f_ref[sl, lanes] + par_ref[1:2, lanes] for sl, lanes in where]
    q = [_silu(q_ref[sl, lanes]) for sl, lanes in where]
    v = [v_ref[sl, lanes] for sl, lanes in where]
    lf = _each(lambda b, x: jnp.log(b + (1.0 - b) * jax.nn.sigmoid(x)), lb, f_pre)
    k = _each(lambda b, x: (1.0 - b) * jax.nn.sigmoid(-x), lb, f_pre)
    lam = _each(lambda x: _mask_nn(before, x), lf)
    lam_end = _each(lambda x: _end_row(x, d), lam)
    blocks = [[] for _ in where]
    for c in range(L // sub):
        lo, hi = c * sub, (c + 1) * sub
        upto = rows >= lo if d == 1 else rows < hi
        for s in range(len(where)):
            edge = (hi, hi + 1) if d == 1 else (lo - 1, lo)
            ref = lam[s][edge[0]:edge[1], :] if 0 <= edge[0] < L else jnp.zeros_like(lam_end[s])
            qc = q[s][lo:hi, :] * jnp.exp(lam[s][lo:hi, :] - ref)
            kc = k[s] * jnp.exp(jnp.where(upto, ref - lam[s], NEG_BIG))
            blocks[s].append(_nt(qc, kc))
    att = _each(lambda bl: jnp.where(before, jnp.concatenate(bl, axis=0), 0.0), blocks)
    y_intra = _each(_nn, att, v)
    q_in = _each(lambda x, l: x * jnp.exp(l), q, lam)
    kv = _each(lambda x, y, l, le: _tn(x, y * jnp.exp(le - l)), v, k, lam, lam_end)
    dec = _each(jnp.exp, lam_end)
    for h in range(nhead):
        st = st_ref[h]
        for i in range(nck):
            s = h * nck + i
            sl, lanes = where[s]
            y = y_intra[s] + _nt(q_in[s], st)
            st = st * dec[s] + kv[s]
            if d == 0:
                y_ref[sl, lanes] = y
            else:
                tot = y0_ref[sl, lanes] + y
                nrm = tot * lax.rsqrt(jnp.mean(tot * tot, axis=-1, keepdims=True) + 1e-6)
                y_ref[sl, lanes] = (nrm * par_ref[2:3, lanes] * _silu(g_ref[sl, lanes])).astype(y_ref.dtype)
        st_ref[h] = st


def _gla_scan(p, lb, f_bias, norm_w, n_ctx):
    b, t, _ = p.shape
    c = C_INNER
    L = GLA_BLOCK
    tb = SCAN_TIME_BLOCK
    nck, nb, ncb = tb // L, t // tb, n_ctx // tb
    y0 = None
    for d in (0, 1):
        tix = lambda j, d=d: _time_index(d, j, ncb, nb)
        seq = lambda blk=0: pl.BlockSpec((None, tb, c), lambda i, j: (i, tix(j), blk))
        par = jnp.stack([jnp.broadcast_to(lb, (c,)), f_bias[d], norm_w])
        in_specs = [seq(0), seq(1 + d), seq(3), pl.BlockSpec(par.shape, lambda i, j: (0, 0))]
        args = [p, p, p, par]
        if d == 1:
            in_specs += [seq(), seq(4)]
            args += [y0, p]
        y0 = pl.pallas_call(
            functools.partial(_gla_kernel, L=L, sub=GLA_CHUNK, nck=nck, d=d),
            grid=(b, nb),
            in_specs=in_specs,
            out_specs=seq(),
            out_shape=jax.ShapeDtypeStruct((b, t, c), BF16 if d else F32),
            scratch_shapes=[pltpu.VMEM((c // C_HEAD_DIM, C_HEAD_DIM, C_HEAD_DIM), F32)],
            compiler_params=_scan_params(2),
            name="gla_scan",
        )(*args)
    return y0


def _rwkv_kernel(*refs, L, nck, d):
    if d == 0:
        r_ref, k_ref, v_ref, a_ref, b_ref, lw_ref, y_ref, h_ref = refs
    else:
        r_ref, k_ref, v_ref, a_ref, b_ref, lw_ref, y0_ref, g_ref, par_ref, y_ref, h_ref = refs
    L2 = 2 * L
    W = 2 * D_HEAD_DIM
    sgn = 1 - 2 * d

    @pl.when(pl.program_id(1) == 0)
    def _():
        h_ref[...] = jnp.zeros_like(h_ref)

    before = _before(L, d)
    r2, c2 = _iota2(L2, L2)
    order2 = ((c2 & (L - 1)) - (r2 & (L - 1))) * sgn
    strict2 = order2 < 0
    incl2 = order2 <= 0
    eye2 = jnp.where(r2 == c2, 1.0, 0.0)
    rw, cw = _iota2(W, W)
    eye_w = rw == cw
    head0 = lax.broadcasted_iota(jnp.int32, (L, W), 1) < D_HEAD_DIM
    stack = lambda x: jnp.concatenate([jnp.where(head0, x, 0.0), jnp.where(head0, 0.0, x)], axis=0)
    n_levels = int(math.log2(L))

    npair = h_ref.shape[0]
    slices = _chunk_slices(nck, L, d)
    where = [(sl, slice(p * W, (p + 1) * W)) for p in range(npair) for sl in slices]
    r, k, v, a, b, lw = ([ref[sl, lanes] for sl, lanes in where]
                         for ref in (r_ref, k_ref, v_ref, a_ref, b_ref, lw_ref))
    cum = _each(lambda x: _mask_nn(before, x), lw)
    cum_end = _each(lambda c: _end_row(c, d), cum)
    e_neg = _each(lambda c: jnp.exp(-c), cum)
    e_end = _each(lambda ce, c: jnp.exp(ce - c), cum_end, cum)
    at = _each(lambda x, c, w: stack(x * jnp.exp(c - w)), a, cum, lw)
    rt = _each(lambda x, c: stack(x * jnp.exp(c)), r, cum)
    bt = _each(lambda x, e: stack(x * e), b, e_neg)
    kt = _each(lambda x, e: stack(x * e), k, e_neg)
    vs = _each(stack, v)
    gram = _each(lambda p, q, s, t: _nt(jnp.concatenate([p, q], axis=0), jnp.concatenate([s, t], axis=0)),
                 at, rt, bt, kt)
    nmat = _each(lambda g: jnp.where(strict2, g[:L2, :L2], 0.0), gram)
    a_k = _each(lambda g: jnp.where(strict2, g[:L2, L2:], 0.0), gram)
    r_bk = _each(lambda g: jnp.where(jnp.concatenate([incl2, incl2], axis=1), g[L2:, :], 0.0), gram)
    tinv = _each(lambda n: eye2 + n, nmat)
    pw = _each(lambda n: _nn(n, n), nmat)
    for lev in range(1, n_levels):
        if lev < n_levels - 1:
            both = _each(lambda p, t: _nn(p, jnp.concatenate([p, t], axis=1)), pw, tinv)
            pw = _each(lambda x: x[:, :L2], both)
            tinv = _each(lambda t, x: t + x[:, L2:], tinv, both)
        else:
            tinv = _each(lambda p, t: t + _nn(p, t), pw, tinv)
    akv = _each(_nn, a_k, vs)
    wu = _each(lambda t, p, q: _nn(t, jnp.concatenate([p, q], axis=1)), tinv, at, akv)
    zs = _each(lambda x, y: jnp.concatenate([x, jnp.concatenate([jnp.zeros_like(y), y], axis=1)], axis=0), wu, vs)
    qy = _each(_nn, r_bk, zs)
    md = _each(lambda x, y, e, z: _tn(jnp.concatenate([stack(x * e), stack(y * e)], axis=0), z),
               b, k, e_end, zs)
    dec = _each(lambda ce: jnp.sum(jnp.where(eye_w, jnp.broadcast_to(jnp.exp(ce), (W, W)), 0.0),
                                   axis=1, keepdims=True), cum_end)

    def head_mean(x):
        m0 = jnp.sum(jnp.where(head0, x, 0.0), axis=-1, keepdims=True)
        m1 = jnp.sum(jnp.where(head0, 0.0, x), axis=-1, keepdims=True)
        return jnp.where(head0, m0, m1) * (1.0 / D_HEAD_DIM)

    hs = [h_ref[p] for p in range(npair)]
    for i in range(nck):
        for p in range(npair):
            s = p * nck + i
            sl, lanes = where[s]
            ys = _nn(rt[s] + qy[s][:, :W], hs[p]) + qy[s][:, W:]
            y = ys[:L, :] + ys[L:, :]
            hs[p] = dec[s] * hs[p] + _nn(md[s][:, :W], hs[p]) + md[s][:, W:]
            if d == 0:
                y_ref[sl, lanes] = y
            else:
                par = par_ref[:, lanes]
                tot = y0_ref[sl, lanes] + y
                cen = tot - head_mean(tot)
                nrm = cen * lax.rsqrt(head_mean(cen * cen) + RWKV_EPS)
                bonus = head_mean(r[s] * k[s] * par[0:1, :]) * float(D_HEAD_DIM) * v[s]
                y_ref[sl, lanes] = ((nrm * par[1:2, :] + par[2:3, :] + bonus) * g_ref[sl, lanes]).astype(y_ref.dtype)
    for p in range(npair):
        h_ref[p] = hs[p]


def _rwkv_scan(r, k, v, a, b, lw, g, par, n_ctx):
    bsz, t, c = r.shape
    L = RWKV_CHUNK
    tb = SCAN_TIME_BLOCK
    nck, nb, ncb = tb // L, t // tb, n_ctx // tb
    y0 = None
    for d in (0, 1):
        tix = lambda j, d=d: _time_index(d, j, ncb, nb)
        seq = pl.BlockSpec((None, tb, c), lambda i, j: (i, tix(j), 0))
        in_specs = [seq] * 5 + [pl.BlockSpec((None, None, tb, c), lambda i, j, d=d: (d, i, tix(j), 0))]
        args = [r, k, v, a, b, lw]
        if d == 1:
            in_specs += [seq, seq, pl.BlockSpec(par.shape, lambda i, j: (0, 0))]
            args += [y0, g, par]
        y0 = pl.pallas_call(
            functools.partial(_rwkv_kernel, L=L, nck=nck, d=d),
            grid=(bsz, nb),
            in_specs=in_specs,
            out_specs=seq,
            out_shape=jax.ShapeDtypeStruct((bsz, t, c), BF16 if d else F32),
            scratch_shapes=[pltpu.VMEM((c // (2 * D_HEAD_DIM), 2 * D_HEAD_DIM, 2 * D_HEAD_DIM), F32)],
            compiler_params=_scan_params(2),
            name="rwkv7_scan",
        )(*args)
    return y0


def _neighbours(x, n_ctx):
    pos = jnp.arange(x.shape[1])[None, :, None]
    prev = jnp.pad(x[:, :-1], ((0, 0), (1, 0), (0, 0)))
    nxt = jnp.pad(x[:, 1:], ((0, 0), (0, 1), (0, 0)))
    return jnp.where(pos == n_ctx, 0.0, prev), jnp.where(pos == n_ctx - 1, 0.0, nxt)


def _dwconv3(x, w, b, n_ctx):
    prev, nxt = _neighbours(x, n_ctx)
    return prev * w[0] + x * w[1] + nxt * w[2] + b


def _ssd_mixer(p, n_ctx, params):
    conv_w, conv_b, dt_bias, a_log, d_skip, norm_w = params
    b, t, _ = p.shape
    a = -jnp.exp(a_log)
    xbc = jax.nn.silu(_dwconv3(p[..., A_INNER:A_INNER + A_XBC], conv_w, conv_b, n_ctx))
    bm = xbc[..., A_INNER:A_INNER + A_GROUPS * A_STATE]
    cm = xbc[..., A_INNER + A_GROUPS * A_STATE:]
    dt = jnp.moveaxis(jax.nn.softplus(p[..., A_INNER + A_XBC:P_A].reshape(b, t, 2, A_HEADS) + dt_bias), 2, 0)
    par = jnp.stack([jnp.repeat(d_skip, A_HEAD_DIM), norm_w])
    return _ssd_scan(xbc[..., :A_INNER], dt, dt * a[:, None, None, :], bm, cm, p, par, n_ctx)


def _mlstm_mixer(p, n_ctx, params):
    conv_w, conv_b, i_bias, f_bias, norm_w = params
    b, t, _ = p.shape
    qk = jax.nn.silu(_dwconv3(p[..., EV_B0:EV_B0 + 2 * B_QK], conv_w, conv_b, n_ctx))
    g0 = EV_B0 + 2 * B_QK + 2 * B_INNER
    gates = p[..., g0:g0 + 4 * B_HEADS].reshape(b, t, 2, 2, B_HEADS)
    log_i = gates[:, :, 0] + i_bias
    log_f = jax.nn.log_sigmoid(gates[:, :, 1] + f_bias)
    g = jnp.moveaxis(jnp.concatenate([log_i, log_f], axis=-1), 2, 0)
    return _mlstm_scan(qk[..., :B_QK], qk[..., B_QK:], g, p, norm_w[None, :], n_ctx)


def _hgrn2_mixer(p, n_ctx, lb, params):
    f_bias, norm_w = params
    return _gla_scan(p, lb, f_bias, norm_w, n_ctx)


def _rwkv_prep_kernel(p_ref, prev_ref, next_ref, mu_ref, w2_ref, a2_ref, g2_ref, vec_ref,
                      r_ref, k_ref, v_ref, a_ref, b_ref, g_ref, lw_ref, *, tiles_per_seq, ctx_tiles):
    c = D_INNER
    j = pl.program_id(0) % tiles_per_seq
    x = p_ref[:, P_C:]
    n = x.shape[0]
    row = lax.broadcasted_iota(jnp.int32, x.shape, 0)
    first = jnp.logical_or(j == 0, j == ctx_tiles)
    last = jnp.logical_or(j == ctx_tiles - 1, j == tiles_per_seq - 1)
    before_tile = jnp.where(first, 0.0, prev_ref[7:8, P_C:])
    after_tile = jnp.where(last, 0.0, next_ref[0:1, P_C:])
    prev = jnp.where(row == 0, before_tile, pltpu.roll(x, 1, axis=0))
    nxt = jnp.where(row == n - 1, after_tile, pltpu.roll(x, n - 1, axis=0))
    x = x + mu_ref[...] * (0.5 * (prev + nxt) - x)
    r, k, v = x[:, :c], x[:, c:2 * c], x[:, 2 * c:3 * c]
    o = 3 * c
    wl = jnp.tanh(x[:, o:o + 2 * D_W_LORA])
    gl = x[:, o + 2 * D_W_LORA:o + 2 * D_W_LORA + D_G_LORA]
    al = x[:, o + 2 * D_W_LORA + D_G_LORA:o + 2 * D_W_LORA + D_G_LORA + D_A_LORA]
    vec = vec_ref[...]
    for d in range(2):
        w = vec[d:d + 1, :] + _nn(wl, w2_ref[d])
        z = -w
        softplus = jnp.maximum(z, 0.0) + jnp.log(1.0 + jnp.exp(-jnp.abs(z)))
        lw_ref[d] = -jnp.exp(-softplus - 0.5)
    a = jax.nn.sigmoid(vec[2:3, :] + _nn(al, a2_ref[...]))
    g_ref[...] = _nn(jax.nn.sigmoid(gl), g2_ref[...])
    kx = k * vec[3:4, :]
    sq = kx * kx
    head0 = (lax.broadcasted_iota(jnp.int32, (n, 2 * D_HEAD_DIM), 1) < D_HEAD_DIM)
    sums = []
    for s in range(c // (2 * D_HEAD_DIM)):
        blk = sq[:, s * 2 * D_HEAD_DIM:(s + 1) * 2 * D_HEAD_DIM]
        s0 = jnp.sum(jnp.where(head0, blk, 0.0), axis=-1, keepdims=True)
        s1 = jnp.sum(jnp.where(head0, 0.0, blk), axis=-1, keepdims=True)
        sums.append(jnp.where(head0, s0, s1))
    kk = kx * lax.rsqrt(jnp.maximum(jnp.concatenate(sums, axis=1), 1e-12))
    r_ref[...] = r
    k_ref[...] = k * (1.0 + (a - 1.0) * vec[4:5, :])
    v_ref[...] = v
    a_ref[...] = -kk
    b_ref[...] = kk * a


def _rwkv7_mixer(p, n_ctx, params):
    mu, w0, w2, a0, a2, g2, k_k, k_a, r_k, ln_w, ln_b = params
    b, t, n_pad = p.shape
    c = D_INNER
    width = n_pad - P_C
    p2 = p.reshape(b * t, n_pad)
    tiles = b * t // ROW_TILE
    halo = ROW_TILE // 8
    w2z = jnp.zeros((2, 2 * D_W_LORA, c), F32)
    w2z = w2z.at[0, :D_W_LORA].set(w2[0]).at[1, D_W_LORA:].set(w2[1])
    vec = jnp.stack([w0[0], w0[1], a0, k_k, k_a])
    whole = lambda a: pl.BlockSpec(a.shape, lambda i: (0,) * a.ndim)
    rows = pl.BlockSpec((ROW_TILE, c), lambda i: (i, 0))
    mu_pad = jnp.pad(mu, (0, width - mu.shape[0]))[None, :]
    outs = pl.pallas_call(
        functools.partial(_rwkv_prep_kernel, tiles_per_seq=t // ROW_TILE, ctx_tiles=n_ctx // ROW_TILE),
        grid=(tiles,),
        in_specs=[pl.BlockSpec((ROW_TILE, n_pad), lambda i: (i, 0)),
                  pl.BlockSpec((8, n_pad), lambda i: (jnp.maximum(i * halo - 1, 0), 0)),
                  pl.BlockSpec((8, n_pad), lambda i: (jnp.minimum((i + 1) * halo, tiles * halo - 1), 0)),
                  whole(mu_pad), whole(w2z), whole(a2), whole(g2), whole(vec)],
        out_specs=[rows] * 6 + [pl.BlockSpec((2, ROW_TILE, c), lambda i: (0, i, 0))],
        out_shape=[jax.ShapeDtypeStruct((b * t, c), F32)] * 6 + [jax.ShapeDtypeStruct((2, b * t, c), F32)],
        compiler_params=pltpu.CompilerParams(dimension_semantics=("arbitrary",), vmem_limit_bytes=VMEM_LIMIT_BYTES),
        name="rwkv7_prep",
    )(p2, p2, p2, mu_pad, w2z, a2, g2, vec)
    r, k, v, a, bb, g = (u.reshape(b, t, c) for u in outs[:6])
    par = jnp.stack([r_k.reshape(D_INNER), ln_w, ln_b])
    return _rwkv_scan(r, k, v, a, bb, outs[6].reshape(2, b, t, c), g, par, n_ctx)


def _to_col_major(u, rows):
    b, s, d = u.shape
    return u.reshape(b, rows, GRID_W, d).transpose(0, 2, 1, 3).reshape(b, s, d)


def _from_col_major(u, rows):
    b, s, d = u.shape
    return u.reshape(b, GRID_W, rows, d).transpose(0, 2, 1, 3).reshape(b, s, d)


def _tile_specs(bsz, t, n_ctx):
    tiles_per_seq, ctx_tiles = t // ROW_TILE, n_ctx // ROW_TILE
    mod_row = lambda i: jnp.where(i % tiles_per_seq < ctx_tiles, bsz, i // tiles_per_seq)
    rows = lambda c: pl.BlockSpec((ROW_TILE, c), lambda i: (i, 0))
    whole = lambda a: pl.BlockSpec(a.shape, lambda i: (0,) * a.ndim)
    mod = pl.BlockSpec((None, 6, D_MODEL), lambda i: (mod_row(i), 0, 0))
    params = pltpu.CompilerParams(dimension_semantics=("arbitrary",), vmem_limit_bytes=VMEM_LIMIT_BYTES)
    return rows, whole, mod, params


def _in_proj_kernel(x_ref, mod_ref, w_ref, o_ref):
    m = mod_ref[...]
    o_ref[...] = _nn(x_ref[...] * (1.0 + m[1:2, :]) + m[0:1, :], w_ref[...])


def _in_proj(xa, mods, w, bsz, n_ctx):
    n = w.shape[1]
    n_pad = -(-n // LANES) * LANES
    wb = jnp.pad(w.astype(BF16), ((0, 0), (0, n_pad - n)))
    rows, whole, mod, params = _tile_specs(bsz, xa.shape[0] // bsz, n_ctx)
    return pl.pallas_call(
        _in_proj_kernel,
        grid=(xa.shape[0] // ROW_TILE,),
        in_specs=[rows(D_MODEL), mod, whole(wb)],
        out_specs=rows(n_pad),
        out_shape=jax.ShapeDtypeStruct((xa.shape[0], n_pad), F32),
        compiler_params=params,
        name="in_proj",
    )(xa, mods, wb)


def _norm_rows(z, ln):
    mu = jnp.mean(z, axis=-1, keepdims=True)
    zc = z - mu
    var = jnp.mean(zc * zc, axis=-1, keepdims=True)
    return zc * lax.rsqrt(var + LN_EPS) * ln[0:1, :] + ln[1:2, :]


def _out_proj_kernel(fa_ref, fb_ref, w_ref, x_ref, mod_ref, ln_ref, xo_ref, h_ref, hb_ref):
    ka = fa_ref.shape[1]
    m = mod_ref[...]
    y = _nn(fa_ref[...], w_ref[:ka, :]) + _nn(fb_ref[...], w_ref[ka:, :])
    xn = _norm_rows(DEEPNORM_ALPHA * x_ref[...] + m[2:3, :] * y, ln_ref[...])
    xo_ref[...] = xn
    h = xn * (1.0 + m[4:5, :]) + m[3:4, :]
    h_ref[...] = h
    hb_ref[...] = h.astype(BF16)


def _out_proj(fa, fb, w, xa, mods, ln, bsz, n_ctx):
    t_all, d = xa.shape
    rows, whole, mod, params = _tile_specs(bsz, t_all // bsz, n_ctx)
    wb = w.astype(BF16)
    return pl.pallas_call(
        _out_proj_kernel,
        grid=(t_all // ROW_TILE,),
        in_specs=[rows(fa.shape[1]), rows(fb.shape[1]), whole(wb), rows(d), mod, whole(ln)],
        out_specs=[rows(d), rows(d), rows(d)],
        out_shape=[jax.ShapeDtypeStruct((t_all, d), F32), jax.ShapeDtypeStruct((t_all, d), F32),
                   jax.ShapeDtypeStruct((t_all, d), BF16)],
        compiler_params=params,
        name="out_proj_norm",
    )(fa, fb, wb, xa, mods, ln)


def _ffn_norm_kernel(f_ref, x_ref, mod_ref, ln_ref, xo_ref):
    m = mod_ref[...]
    xo_ref[...] = _norm_rows(DEEPNORM_ALPHA * x_ref[...] + m[5:6, :] * f_ref[...], ln_ref[...])


def _ffn_norm(f, xa, mods, ln, bsz, n_ctx):
    t_all, d = xa.shape
    rows, whole, mod, params = _tile_specs(bsz, t_all // bsz, n_ctx)
    return pl.pallas_call(
        _ffn_norm_kernel,
        grid=(t_all // ROW_TILE,),
        in_specs=[rows(d), rows(d), mod, whole(ln)],
        out_specs=rows(d),
        out_shape=jax.ShapeDtypeStruct((t_all, d), F32),
        compiler_params=params,
        name="ffn_residual_norm",
    )(f, xa, mods, ln)


def kernel(x, c, ctx, c_ctx, mod_w, mod_b, ln_g, ln_b, ev_w_in, ev_w_out, ssd_conv_w, ssd_conv_b, ssd_dt_bias, ssd_a_log, ssd_d, ssd_norm_w, mlstm_conv_w, mlstm_conv_b, mlstm_i_bias, mlstm_f_bias, mlstm_norm_w, od_w_in, od_w_out, hgrn_lb_logits, hgrn_f_bias, hgrn_norm_w, rwkv_mu, rwkv_w0, rwkv_w2, rwkv_a0, rwkv_a2, rwkv_g2, rwkv_k_k, rwkv_k_a, rwkv_r_k, rwkv_ln_w, rwkv_ln_b, router_w, router_bias, exp_w_gate, exp_w_up, exp_w_down):
    bsz, seq, _ = x.shape
    n_ctx = ctx.shape[1]
    rows = seq // GRID_W
    lb_all = jnp.cumsum(jax.nn.softmax(hgrn_lb_logits.astype(F32), axis=0), axis=0)
    lb_all = lb_all - lb_all[0]
    s_c = jax.nn.silu(c)
    s_cc = jax.nn.silu(c_ctx)
    t = n_ctx + seq
    xa = jnp.concatenate([ctx, x], axis=1).reshape(bsz * t, D_MODEL)
    seq3 = lambda a: a.reshape(bsz, t, a.shape[-1])
    flat = lambda a: a.reshape(bsz * t, a.shape[-1])
    lat_order = lambda a, f: flat(jnp.concatenate([seq3(a)[:, :n_ctx], f(seq3(a)[:, n_ctx:], rows)], axis=1))
    for layer in range(DEPTH):
        i = layer // 2
        mods = _matmul(jnp.concatenate([s_c, s_cc[None]], axis=0), mod_w[layer], tm=8, tn=512) + mod_b[layer]
        mods = mods.reshape(bsz + 1, 6, D_MODEL)
        ln = jnp.stack([ln_g[layer], ln_b[layer]], axis=1)
        if layer % 2 == 0:
            w_in = jnp.concatenate([ev_w_in[i][:, :P_A], jnp.zeros((D_MODEL, EV_B0 - P_A), F32),
                                    ev_w_in[i][:, P_A:]], axis=1)
            p = seq3(_in_proj(xa, mods, w_in, bsz, n_ctx))
            fa = _ssd_mixer(p, n_ctx,
                            (ssd_conv_w[i], ssd_conv_b[i], ssd_dt_bias[i], ssd_a_log[i], ssd_d[i], ssd_norm_w[i]))
            fb = _mlstm_mixer(p, n_ctx,
                              (mlstm_conv_w[i], mlstm_conv_b[i], mlstm_i_bias[i], mlstm_f_bias[i], mlstm_norm_w[i]))
            fa, fb, w_out = flat(fa), flat(fb), ev_w_out[i]
        else:
            g0, g1 = 3 * D_INNER + 2 * D_W_LORA + D_A_LORA, P_D
            lora_last = lambda a: jnp.concatenate([a[..., :g0 - D_A_LORA], a[..., g0:g1], a[..., g0 - D_A_LORA:g0]], axis=-1)
            w_in = jnp.concatenate([od_w_in[i][:, :P_C], lora_last(od_w_in[i][:, P_C:])], axis=1)
            p = seq3(_in_proj(lat_order(xa, _to_col_major), mods, w_in, bsz, n_ctx))
            fa = _hgrn2_mixer(p, n_ctx, lb_all[layer], (hgrn_f_bias[i], hgrn_norm_w[i]))
            fb = _rwkv7_mixer(p, n_ctx,
                              (lora_last(rwkv_mu[i]), rwkv_w0[i], rwkv_w2[i], rwkv_a0[i], rwkv_a2[i], rwkv_g2[i],
                               rwkv_k_k[i], rwkv_k_a[i], rwkv_r_k[i], rwkv_ln_w[i], rwkv_ln_b[i]))
            fa, fb, w_out = lat_order(fa, _from_col_major), lat_order(fb, _from_col_major), od_w_out[i]
        xa, h, hb = _out_proj(fa, fb, w_out, xa, mods, ln[0], bsz, n_ctx)
        f = _moe_ffn(h, hb, router_w, router_bias, exp_w_gate, exp_w_up, exp_w_down, layer)
        xa = _ffn_norm(f, xa, mods, ln[1], bsz, n_ctx)
    return seq3(xa)[:, n_ctx:]
```

```python
import functools
import math

import jax
import jax.numpy as jnp
from jax import lax
from jax.experimental import pallas as pl
from jax.experimental.pallas import tpu as pltpu

F32 = jnp.float32
BF16 = jnp.bfloat16

D_MODEL = 1024
DEPTH = 4
GRID_W = 64
A_HEADS = 8
A_HEAD_DIM = 64
A_INNER = A_HEADS * A_HEAD_DIM
A_GROUPS = 2
A_STATE = 64
A_XBC = A_INNER + 2 * A_GROUPS * A_STATE
B_HEADS = 4
B_QK_DIM = 64
B_V_DIM = 128
B_QK = B_HEADS * B_QK_DIM
B_INNER = B_HEADS * B_V_DIM
MLSTM_EPS = 1e-6
C_HEADS = 4
C_HEAD_DIM = 128
C_INNER = C_HEADS * C_HEAD_DIM
D_HEADS = 8
D_HEAD_DIM = 64
D_INNER = D_HEADS * D_HEAD_DIM
D_W_LORA = 64
D_A_LORA = 64
D_G_LORA = 128
RWKV_EPS = 64e-5
P_A = A_INNER + A_XBC + 2 * A_HEADS
P_B = 2 * B_QK + 2 * B_INNER + 4 * B_HEADS
P_C = 5 * C_INNER
P_D = 3 * D_INNER + 2 * D_W_LORA + D_A_LORA + D_G_LORA
EV_B0 = 3 * A_INNER
N_EXPERTS = 32
N_EXPERT_GROUPS = 8
EXPERTS_PER_GROUP = N_EXPERTS // N_EXPERT_GROUPS
TOP_K = 2
D_EXPERT = 512
MOE_BLOCK = 512
RANK_BLOCK = 512
ROW_TILE = 256
LANES = 128
DEEPNORM_ALPHA = (2 * DEPTH) ** 0.25
LN_EPS = 1e-5
M_INIT = -1e30
NEG_BIG = -1e30

SSD_CHUNK = 128
MLSTM_CHUNK = 128
GLA_CHUNK = 16
GLA_BLOCK = 64
RWKV_CHUNK = 64
SCAN_TIME_BLOCK = 256

VMEM_LIMIT_BYTES = 48 * 1024 * 1024
HI = lax.Precision.HIGHEST


def _dot(a, b, dims, exact):
    if exact:
        return lax.dot_general(a.astype(F32), b.astype(F32), (dims, ((), ())),
                               precision=HI, preferred_element_type=F32)
    return lax.dot_general(a.astype(BF16), b.astype(BF16), (dims, ((), ())),
                           preferred_element_type=F32)


def _nn(a, b, exact=False):
    return _dot(a, b, ((1,), (0,)), exact)


def _nt(a, b, exact=False):
    return _dot(a, b, ((1,), (1,)), exact)


def _tn(a, b, exact=False):
    return _dot(a, b, ((0,), (0,)), exact)


def _iota2(n, m):
    return (lax.broadcasted_iota(jnp.int32, (n, m), 0),
            lax.broadcasted_iota(jnp.int32, (n, m), 1))


def _split3(x):
    x1 = x.astype(BF16)
    r1 = x - x1.astype(F32)
    x2 = r1.astype(BF16)
    x3 = (r1 - x2.astype(F32)).astype(BF16)
    return x1, x2, x3


def _mask_nn(mask, x):
    mb = mask.astype(BF16)
    x1, x2, x3 = _split3(x)
    return _nn(mb, x1) + _nn(mb, x2) + _nn(mb, x3)


def _nn_mask(x, mask):
    mb = mask.astype(BF16)
    x1, x2, x3 = _split3(x)
    return _nn(x1, mb) + _nn(x2, mb) + _nn(x3, mb)


def _each(f, *cols):
    return [f(*xs) for xs in zip(*cols)]


def _silu(x):
    return x * jax.nn.sigmoid(x)


def _mm_kernel(x_ref, w_ref, o_ref, *, exact):
    o_ref[...] = _nn(x_ref[...], w_ref[...], exact)


def _matmul(x, w, tm=512, tn=512, exact=False):
    m, k = x.shape
    n = w.shape[1]
    n_pad = -(-n // tn) * tn
    m_pad = -(-m // tm) * tm
    xb = x if exact else x.astype(BF16)
    wb = w if exact else w.astype(BF16)
    if n_pad != n:
        wb = jnp.pad(wb, ((0, 0), (0, n_pad - n)))
    if m_pad != m:
        xb = jnp.pad(xb, ((0, m_pad - m), (0, 0)))
    out = pl.pallas_call(
        functools.partial(_mm_kernel, exact=exact),
        grid=(n_pad // tn, m_pad // tm),
        in_specs=[pl.BlockSpec((tm, k), lambda j, i: (i, 0)),
                  pl.BlockSpec((k, tn), lambda j, i: (0, j))],
        out_specs=pl.BlockSpec((tm, tn), lambda j, i: (i, j)),
        out_shape=jax.ShapeDtypeStruct((m_pad, n_pad), F32),
        compiler_params=pltpu.CompilerParams(
            dimension_semantics=("arbitrary", "arbitrary"),
            vmem_limit_bytes=VMEM_LIMIT_BYTES),
        name="dense_matmul",
    )(xb, wb)
    return out[:m, :n]


def _moe_kernel(blk_e_ref, n_used_ref, x_ref, wg_ref, wu_ref, wd_ref, o_ref):
    i = pl.program_id(0)

    @pl.when(i < n_used_ref[0])
    def _():
        x = x_ref[...]
        g = _nn(x, wg_ref[...])
        u = _nn(x, wu_ref[...])
        o_ref[...] = _nn(g * jax.nn.sigmoid(g) * u, wd_ref[...]).astype(o_ref.dtype)

    @pl.when(i >= n_used_ref[0])
    def _():
        o_ref[...] = jnp.zeros_like(o_ref)


def _moe_experts(xp, blk_e, n_used, w_gate, w_up, w_down, layer):
    n_rows, d = xp.shape
    n_blocks = n_rows // MOE_BLOCK
    grid_spec = pltpu.PrefetchScalarGridSpec(
        num_scalar_prefetch=2,
        grid=(n_blocks,),
        in_specs=[
            pl.BlockSpec((MOE_BLOCK, d), lambda i, be, nu: (i, 0)),
            pl.BlockSpec((None, None, d, D_EXPERT), lambda i, be, nu: (layer, be[i], 0, 0)),
            pl.BlockSpec((None, None, d, D_EXPERT), lambda i, be, nu: (layer, be[i], 0, 0)),
            pl.BlockSpec((None, None, D_EXPERT, d), lambda i, be, nu: (layer, be[i], 0, 0)),
        ],
        out_specs=pl.BlockSpec((MOE_BLOCK, d), lambda i, be, nu: (i, 0)),
    )
    return pl.pallas_call(
        _moe_kernel,
        grid_spec=grid_spec,
        out_shape=jax.ShapeDtypeStruct((n_rows, d), BF16),
        compiler_params=pltpu.CompilerParams(
            dimension_semantics=("arbitrary",),
            vmem_limit_bytes=VMEM_LIMIT_BYTES),
        name="moe_experts",
    )(blk_e, n_used, xp, w_gate, w_up, w_down)


def _top2(vals):
    m = len(vals)
    m1 = functools.reduce(jnp.maximum, vals)
    i1 = jnp.full_like(m1, float(m - 1))
    for j in reversed(range(m - 1)):
        i1 = jnp.where(vals[j] == m1, float(j), i1)
    rest = [jnp.where(i1 == float(j), -jnp.inf, vals[j]) for j in range(m)]
    m2 = functools.reduce(jnp.maximum, rest)
    i2 = jnp.full_like(m1, float(m - 1))
    for j in reversed(range(m - 1)):
        i2 = jnp.where(rest[j] == m2, float(j), i2)
    return m1, i1, m2, i2


def _router_kernel(h_ref, wt_ref, bias_ref, e_ref, w_ref, rank_ref, cnt_ref, carry_ref):
    tm = h_ref.shape[0]
    ng, per = N_EXPERT_GROUPS, EXPERTS_PER_GROUP

    @pl.when(pl.program_id(0) == 0)
    def _():
        carry_ref[...] = jnp.zeros_like(carry_ref)

    s = jax.nn.sigmoid(_nt(wt_ref[...], h_ref[...], True))
    sb = s + bias_ref[...]
    biased = [sb[j * ng:(j + 1) * ng, :] for j in range(per)]
    plain = [s[j * ng:(j + 1) * ng, :] for j in range(per)]
    m1, _, m2, _ = _top2(biased)
    gsum = m1 + m2
    rows = lax.broadcasted_iota(jnp.int32, (ng, tm), 0).astype(F32)
    gmax = jnp.max(gsum, axis=0, keepdims=True)
    gi = jnp.min(jnp.where(gsum == gmax, rows, float(ng)), axis=0, keepdims=True)
    sel = rows == gi
    pick = lambda v: jnp.sum(jnp.where(sel, v, 0.0), axis=0, keepdims=True)
    in_b = [pick(v) for v in biased]
    in_s = [pick(v) for v in plain]
    _, l1, _, l2 = _top2(in_b)
    w1 = functools.reduce(jnp.add, [jnp.where(l1 == float(j), in_s[j], 0.0) for j in range(per)])
    w2 = functools.reduce(jnp.add, [jnp.where(l2 == float(j), in_s[j], 0.0) for j in range(per)])
    e1 = gi * float(per) + l1
    e2 = gi * float(per) + l2
    wsum = w1 + w2
    e_ref[0:1, :] = e1.astype(jnp.int32)
    e_ref[1:2, :] = e2.astype(jnp.int32)
    w_ref[0:1, :] = w1 / wsum
    w_ref[1:2, :] = w2 / wsum
    row = lax.broadcasted_iota(jnp.int32, (N_EXPERTS, tm), 0)
    experts = ((row % ng) * per + row // ng).astype(F32)
    oh1 = jnp.where(experts == e1, 1.0, 0.0)
    oh2 = jnp.where(experts == e2, 1.0, 0.0)
    oh = oh1 + oh2
    ri, ci = _iota2(tm, tm)
    seen = _nn(oh, jnp.where(ri < ci, 1.0, 0.0)) + carry_ref[...]
    rank_ref[0:1, :] = jnp.sum(oh1 * seen, axis=0, keepdims=True).astype(jnp.int32)
    rank_ref[1:2, :] = jnp.sum(oh2 * seen, axis=0, keepdims=True).astype(jnp.int32)
    carry = carry_ref[...] + jnp.sum(oh, axis=1, keepdims=True)
    carry_ref[...] = carry
    cnt_ref[...] = carry.astype(jnp.int32)


def _route(h, router_w, router_bias):
    t, d = h.shape
    tm = RANK_BLOCK
    kt = lambda dt: jax.ShapeDtypeStruct((TOP_K, t), dt)
    blk = pl.BlockSpec((TOP_K, tm), lambda i: (0, i))
    member_major = lambda a: a.reshape(N_EXPERT_GROUPS, EXPERTS_PER_GROUP, -1).transpose(1, 0, 2).reshape(N_EXPERTS, -1)
    e, w, rank, counts = pl.pallas_call(
        _router_kernel,
        grid=(t // tm,),
        in_specs=[pl.BlockSpec((tm, d), lambda i: (i, 0)),
                  pl.BlockSpec((N_EXPERTS, d), lambda i: (0, 0)),
                  pl.BlockSpec((N_EXPERTS, 1), lambda i: (0, 0))],
        out_specs=[blk, blk, blk, pl.BlockSpec((N_EXPERTS, 1), lambda i: (0, 0))],
        out_shape=[kt(jnp.int32), kt(F32), kt(jnp.int32), jax.ShapeDtypeStruct((N_EXPERTS, 1), jnp.int32)],
        scratch_shapes=[pltpu.VMEM((N_EXPERTS, 1), F32)],
        compiler_params=pltpu.CompilerParams(dimension_semantics=("arbitrary",),
                                             vmem_limit_bytes=VMEM_LIMIT_BYTES),
        name="moe_router",
    )(h, member_major(router_w.T), member_major(router_bias.astype(F32).reshape(N_EXPERTS, 1)))
    counts = counts.reshape(EXPERTS_PER_GROUP, N_EXPERT_GROUPS).T.reshape(N_EXPERTS)
    return e, w, rank, counts


def _moe_ffn(h, hb, router_w, router_bias, w_gate, w_up, w_down, layer):
    t, d = h.shape
    expert, wts, rank, counts = _route(h, router_w, router_bias)
    n_assign = t * TOP_K
    padded = (counts + MOE_BLOCK - 1) // MOE_BLOCK * MOE_BLOCK
    pends = jnp.cumsum(padded)
    pstarts = pends - padded
    start_of = jnp.sum(jnp.where(expert[..., None] == jnp.arange(N_EXPERTS, dtype=jnp.int32), pstarts, 0), axis=-1)
    dest = start_of + rank
    n_blocks = -(-n_assign // MOE_BLOCK) + N_EXPERTS
    slot_token = jnp.arange(n_blocks * MOE_BLOCK, dtype=jnp.int32) % t
    for kk in range(TOP_K):
        slot_token = slot_token.at[dest[kk]].set(jnp.arange(t, dtype=jnp.int32), unique_indices=True)
    xp = hb[slot_token]
    blk_start = jnp.arange(n_blocks, dtype=jnp.int32) * MOE_BLOCK
    blk_e = jnp.minimum(jnp.sum(pends[None, :] <= blk_start[:, None], axis=1), N_EXPERTS - 1).astype(jnp.int32)
    n_used = (pends[-1] // MOE_BLOCK).astype(jnp.int32).reshape(1)
    yp = _moe_experts(xp, blk_e, n_used, w_gate, w_up, w_down, layer)
    out = yp[dest[0]] * wts[0][:, None]
    for kk in range(1, TOP_K):
        out = out + yp[dest[kk]] * wts[kk][:, None]
    return out


def _time_index(d, j, n_ctx_blocks, n_blocks):
    if d == 0:
        return j
    return jnp.where(j < n_ctx_blocks, n_ctx_blocks - 1 - j, n_blocks - 1 - j + n_ctx_blocks)


def _end_row(x, d):
    n = x.shape[0]
    return x[0:1, :] if d == 1 else x[n - 1:n, :]


def _before(n, d):
    ri, ci = _iota2(n, n)
    return ci >= ri if d == 1 else ci <= ri


def _chunk_slices(nck, L, d):
    order = range(nck - 1, -1, -1) if d == 1 else range(nck)
    return [slice(i * L, (i + 1) * L) for i in order]


def _scan_params(n_axes):
    return pltpu.CompilerParams(dimension_semantics=("arbitrary",) * n_axes, vmem_limit_bytes=VMEM_LIMIT_BYTES)


def _ssd_kernel(*refs, L, d):
    if d == 0:
        x_ref, dt_ref, lac_ref, lar_ref, b_ref, c_ref, y_ref, st_ref = refs
    else:
        x_ref, dt_ref, lac_ref, lar_ref, b_ref, c_ref, y0_ref, z_ref, par_ref, y_ref, st_ref = refs
    hp, gn = A_INNER, A_GROUPS * A_STATE
    hpg = A_HEADS // A_GROUPS

    @pl.when(pl.program_id(1) == 0)
    def _():
        st_ref[...] = jnp.zeros_like(st_ref)

    before = _before(L, d)
    xs = x_ref[...]
    bm = b_ref[...]
    cm = c_ref[...]
    ccol = _mask_nn(before, lac_ref[...])
    crow = _nn_mask(lar_ref[...], _before(L, 1 - d))
    head_of_lane = lax.broadcasted_iota(jnp.int32, (A_HEADS, hp), 1) // A_HEAD_DIM
    expand = head_of_lane == lax.broadcasted_iota(jnp.int32, (A_HEADS, hp), 0)
    cum = _nn_mask(ccol, expand)
    x = xs * _nn_mask(dt_ref[...], expand)
    end = _end_row(cum, d)
    group_of_lane = lax.broadcasted_iota(jnp.int32, (L, gn), 1) // A_STATE
    first_of_pair = (lax.broadcasted_iota(jnp.int32, (L, 2 * A_HEAD_DIM), 1) < A_HEAD_DIM)
    cbs = [_nt(jnp.where(group_of_lane == g, cm, 0.0), bm) for g in range(A_GROUPS)]
    heads = list(range(A_HEADS))
    decay = [jnp.exp(jnp.where(before, ccol[:, h:h + 1] - crow[h:h + 1, :], NEG_BIG)) for h in heads]
    yh = [_nn(cbs[h // hpg] * decay[h], x[:, (h // 2) * 2 * A_HEAD_DIM:(h // 2 + 1) * 2 * A_HEAD_DIM]) for h in heads]
    pairs = [jnp.where(first_of_pair, yh[2 * p], yh[2 * p + 1]) for p in range(A_HEADS // 2)]
    st = st_ref[...]
    y = jnp.concatenate(pairs, axis=1) + jnp.exp(cum) * _nn(cm, st)
    own_group = (lax.broadcasted_iota(jnp.int32, (gn, hp), 0) // A_STATE
                 == lax.broadcasted_iota(jnp.int32, (gn, hp), 1) // (A_HEAD_DIM * hpg))
    st_ref[...] = jnp.exp(end) * st + jnp.where(own_group, _tn(bm, x * jnp.exp(end - cum)), 0.0)
    if d == 0:
        y_ref[...] = y
    else:
        par = par_ref[...]
        u = (y0_ref[...] + y + par[0:1, :] * xs) * _silu(z_ref[...])
        y_ref[...] = (u * lax.rsqrt(jnp.mean(u * u, axis=-1, keepdims=True) + 1e-6) * par[1:2, :]).astype(y_ref.dtype)


def _ssd_scan(xs, dt, la, bm, cm, p, par, n_ctx):
    b, t, hp = xs.shape
    L = SSD_CHUNK
    nb, ncb = t // L, n_ctx // L
    lar = jnp.swapaxes(la, 2, 3)
    y0 = None
    for d in (0, 1):
        tix = lambda j, d=d: _time_index(d, j, ncb, nb)
        seq = lambda c: pl.BlockSpec((None, L, c), lambda i, j: (i, tix(j), 0))
        in_specs = [seq(hp),
                    pl.BlockSpec((None, None, L, A_HEADS), lambda i, j, d=d: (d, i, tix(j), 0)),
                    pl.BlockSpec((None, None, L, A_HEADS), lambda i, j, d=d: (d, i, tix(j), 0)),
                    pl.BlockSpec((None, None, A_HEADS, L), lambda i, j, d=d: (d, i, 0, tix(j))),
                    seq(bm.shape[-1]), seq(bm.shape[-1])]
        args = [xs, dt, la, lar, bm, cm]
        if d == 1:
            in_specs += [seq(hp), seq(hp), pl.BlockSpec(par.shape, lambda i, j: (0, 0))]
            args += [y0, p, par]
        y0 = pl.pallas_call(
            functools.partial(_ssd_kernel, L=L, d=d),
            grid=(b, nb),
            in_specs=in_specs,
            out_specs=seq(hp),
            out_shape=jax.ShapeDtypeStruct((b, t, hp), BF16 if d else F32),
            scratch_shapes=[pltpu.VMEM((bm.shape[-1], hp), F32)],
            compiler_params=_scan_params(2),
            name="ssd_scan",
        )(*args)
    return y0


def _mlstm_kernel(*refs, L, d):
    if d == 0:
        q_ref, k_ref, v_ref, gc_ref, gr_ref, h_ref, c_ref, n_ref, m_ref = refs
    else:
        q_ref, k_ref, v_ref, gc_ref, gr_ref, h0_ref, o_ref, par_ref, h_ref, c_ref, n_ref, m_ref = refs
    nh, dk, dv = B_HEADS, B_QK_DIM, B_V_DIM

    @pl.when(pl.program_id(1) == 0)
    def _():
        c_ref[...] = jnp.zeros_like(c_ref)
        n_ref[...] = jnp.zeros_like(n_ref)
        m_ref[...] = jnp.full_like(m_ref, M_INIT)

    before = _before(L, d)
    gc = gc_ref[...]
    gr = gr_ref[...]
    fcol = _mask_nn(before, gc[:, nh:])
    frow = _nn_mask(gr[nh:, :], _before(L, 1 - d))
    lane_head = lax.broadcasted_iota(jnp.int32, (L, 2 * dk), 1) // dk
    heads = list(range(nh))
    slab = [slice((h // 2) * 2 * dk, (h // 2 + 1) * 2 * dk) for h in heads]
    lanes = [slice(h * dv, (h + 1) * dv) for h in heads]
    q = [jnp.where(lane_head == h % 2, q_ref[:, slab[h]], 0.0) * (dk ** -0.5) for h in heads]
    k = [k_ref[:, slab[h]] for h in heads]
    v = [v_ref[:, lanes[h]] for h in heads]
    li_c = [gc[:, h:h + 1] for h in heads]
    li_r = [gr[h:h + 1, :] for h in heads]
    f_c = [fcol[:, h:h + 1] for h in heads]
    f_r = [frow[h:h + 1, :] for h in heads]
    ftot = _each(lambda x: _end_row(x, d), f_c)
    c_prev = [c_ref[h] for h in heads]
    n_prev = [n_ref[h] for h in heads]
    m_prev = [m_ref[h] for h in heads]
    w_end = _each(lambda ft, fc, lc: ft - fc + lc, ftot, f_c, li_c)
    m_loc = _each(lambda w: jnp.max(w, axis=0, keepdims=True), w_end)
    ke = _each(lambda x, w, m: x * jnp.exp(w - m), k, w_end, m_loc)
    c_loc = _each(_tn, ke, v)
    n_loc = _each(lambda x: jnp.sum(x, axis=0, keepdims=True), ke)
    log_d = _each(lambda fc, fr, lr: jnp.where(before, fc - fr + lr, NEG_BIG), f_c, f_r, li_r)
    log_inter = _each(jnp.add, f_c, m_prev)
    m_row = _each(lambda ld, lint: jnp.maximum(jnp.max(ld, axis=-1, keepdims=True), lint), log_d, log_inter)
    s = _each(lambda a, b, ld, mr: _nt(a, b) * jnp.exp(ld - mr), q, k, log_d, m_row)
    inter = _each(lambda lint, mr: jnp.exp(lint - mr), log_inter, m_row)
    num = _each(lambda ss, vv, it, qq, cp: _nn(ss, vv) + it * _nn(qq, cp), s, v, inter, q, c_prev)
    den = _each(lambda ss, it, qq, npv: jnp.sum(ss, axis=-1, keepdims=True)
                + it * jnp.sum(qq * npv, axis=-1, keepdims=True), s, inter, q, n_prev)
    out = _each(lambda nu, de, mr: nu / jnp.maximum(jnp.abs(de), jnp.exp(-mr)), num, den, m_row)
    m_new = _each(lambda ft, mp, ml: jnp.maximum(ft + mp, ml), ftot, m_prev, m_loc)
    sp = _each(lambda ft, mp, mn: jnp.exp(ft + mp - mn), ftot, m_prev, m_new)
    sc = _each(lambda ml, mn: jnp.exp(ml - mn), m_loc, m_new)
    for h in heads:
        c_ref[h] = sp[h] * c_prev[h] + sc[h] * c_loc[h]
        n_ref[h] = sp[h] * n_prev[h] + sc[h] * n_loc[h]
        m_ref[h] = m_new[h]
        if d == 0:
            h_ref[:, lanes[h]] = out[h]
        else:
            tot = h0_ref[:, lanes[h]] + out[h]
            cen = tot - jnp.mean(tot, axis=-1, keepdims=True)
            nrm = cen * lax.rsqrt(jnp.mean(cen * cen, axis=-1, keepdims=True) + MLSTM_EPS)
            h_ref[:, lanes[h]] = (jax.nn.sigmoid(o_ref[:, lanes[h]]) * nrm * par_ref[:, lanes[h]]).astype(h_ref.dtype)


def _mlstm_scan(q, k, gates, p, par, n_ctx):
    b, t, _ = q.shape
    L = MLSTM_CHUNK
    nb, ncb = t // L, n_ctx // L
    v_blk, o_blk = (EV_B0 + 2 * B_QK) // B_INNER, (EV_B0 + 2 * B_QK + B_INNER) // B_INNER
    gates_r = jnp.swapaxes(gates, 2, 3)
    h0 = None
    for d in (0, 1):
        tix = lambda j, d=d: _time_index(d, j, ncb, nb)
        seq = lambda c, blk=0: pl.BlockSpec((None, L, c), lambda i, j: (i, tix(j), blk))
        in_specs = [seq(B_QK), seq(B_QK), seq(B_INNER, v_blk),
                    pl.BlockSpec((None, None, L, 2 * B_HEADS), lambda i, j, d=d: (d, i, tix(j), 0)),
                    pl.BlockSpec((None, None, 2 * B_HEADS, L), lambda i, j, d=d: (d, i, 0, tix(j)))]
        args = [q, k, p, gates, gates_r]
        if d == 1:
            in_specs += [seq(B_INNER), seq(B_INNER, o_blk), pl.BlockSpec(par.shape, lambda i, j: (0, 0))]
            args += [h0, p, par]
        h0 = pl.pallas_call(
            functools.partial(_mlstm_kernel, L=L, d=d),
            grid=(b, nb),
            in_specs=in_specs,
            out_specs=seq(B_INNER),
            out_shape=jax.ShapeDtypeStruct((b, t, B_INNER), BF16 if d else F32),
            scratch_shapes=[pltpu.VMEM((B_HEADS, 2 * B_QK_DIM, B_V_DIM), F32),
                            pltpu.VMEM((B_HEADS, 1, 2 * B_QK_DIM), F32),
                            pltpu.VMEM((B_HEADS, 1, 1), F32)],
            compiler_params=_scan_params(2),
            name="mlstm_scan",
        )(*args)
    return h0


def _gla_kernel(*refs, L, sub, nck, d):
    if d == 0:
        q_ref, f_ref, v_ref, par_ref, y_ref, st_ref = refs
    else:
        q_ref, f_ref, v_ref, par_ref, y0_ref, g_ref, y_ref, st_ref = refs
    hd = C_HEAD_DIM
    nhead = st_ref.shape[0]

    @pl.when(pl.program_id(1) == 0)
    def _():
        st_ref[...] = jnp.zeros_like(st_ref)

    before = _before(L, d)
    rows = lax.broadcasted_iota(jnp.int32, (L, 1), 0)
    slices = _chunk_slices(nck, L, d)
    where = [(sl, slice(h * hd, (h + 1) * hd)) for h in range(nhead) for sl in slices]
    lb = [par_ref[0:1, lanes] for _, lanes in where]
    f_pre = [f_ref[sl, lanes] + par_ref[1:2, lanes] for sl, lanes in where]
    q = [_silu(q_ref[sl, lanes]) for sl, lanes in where]
    v = [v_ref[sl, lanes] for sl, lanes in where]
    lf = _each(lambda b, x: jnp.log(b + (1.0 - b) * jax.nn.sigmoid(x)), lb, f_pre)
    k = _each(lambda b, x: (1.0 - b) * jax.nn.sigmoid(-x), lb, f_pre)
    lam = _each(lambda x: _mask_nn(before, x), lf)
    lam_end = _each(lambda x: _end_row(x, d), lam)
    blocks = [[] for _ in where]
    for c in range(L // sub):
        lo, hi = c * sub, (c + 1) * sub
        upto = rows >= lo if d == 1 else rows < hi
        for s in range(len(where)):
            edge = (hi, hi + 1) if d == 1 else (lo - 1, lo)
            ref = lam[s][edge[0]:edge[1], :] if 0 <= edge[0] < L else jnp.zeros_like(lam_end[s])
            qc = q[s][lo:hi, :] * jnp.exp(lam[s][lo:hi, :] - ref)
            kc = k[s] * jnp.exp(jnp.where(upto, ref - lam[s], NEG_BIG))
            blocks[s].append(_nt(qc, kc))
    att = _each(lambda bl: jnp.where(before, jnp.concatenate(bl, axis=0), 0.0), blocks)
    y_intra = _each(_nn, att, v)
    q_in = _each(lambda x, l: x * jnp.exp(l), q, lam)
    kv = _each(lambda x, y, l, le: _tn(x, y * jnp.exp(le - l)), v, k, lam, lam_end)
    dec = _each(jnp.exp, lam_end)
    for h in range(nhead):
        st = st_ref[h]
        for i in range(nck):
            s = h * nck + i
            sl, lanes = where[s]
            y = y_intra[s] + _nt(q_in[s], st)
            st = st * dec[s] + kv[s]
            if d == 0:
                y_ref[sl, lanes] = y
            else:
                tot = y0_ref[sl, lanes] + y
                nrm = tot * lax.rsqrt(jnp.mean(tot * tot, axis=-1, keepdims=True) + 1e-6)
                y_ref[sl, lanes] = (nrm * par_ref[2:3, lanes] * _silu(g_ref[sl, lanes])).astype(y_ref.dtype)
        st_ref[h] = st


def _gla_scan(p, lb, f_bias, norm_w, n_ctx):
    b, t, _ = p.shape
    c = C_INNER
    L = GLA_BLOCK
    tb = SCAN_TIME_BLOCK
    nck, nb, ncb = tb // L, t // tb, n_ctx // tb
    y0 = None
    for d in (0, 1):
        tix = lambda j, d=d: _time_index(d, j, ncb, nb)
        seq = lambda blk=0: pl.BlockSpec((None, tb, c), lambda i, j: (i, tix(j), blk))
        par = jnp.stack([jnp.broadcast_to(lb, (c,)), f_bias[d], norm_w])
        in_specs = [seq(0), seq(1 + d), seq(3), pl.BlockSpec(par.shape, lambda i, j: (0, 0))]
        args = [p, p, p, par]
        if d == 1:
            in_specs += [seq(), seq(4)]
            args += [y0, p]
        y0 = pl.pallas_call(
            functools.partial(_gla_kernel, L=L, sub=GLA_CHUNK, nck=nck, d=d),
            grid=(b, nb),
            in_specs=in_specs,
            out_specs=seq(),
            out_shape=jax.ShapeDtypeStruct((b, t, c), BF16 if d else F32),
            scratch_shapes=[pltpu.VMEM((c // C_HEAD_DIM, C_HEAD_DIM, C_HEAD_DIM), F32)],
            compiler_params=_scan_params(2),
            name="gla_scan",
        )(*args)
    return y0


def _rwkv_kernel(*refs, L, nck, d):
    if d == 0:
        r_ref, k_ref, v_ref, a_ref, b_ref, lw_ref, y_ref, h_ref = refs
    else:
        r_ref, k_ref, v_ref, a_ref, b_ref, lw_ref, y0_ref, g_ref, par_ref, y_ref, h_ref = refs
    L2 = 2 * L
    W = 2 * D_HEAD_DIM
    sgn = 1 - 2 * d

    @pl.when(pl.program_id(1) == 0)
    def _():
        h_ref[...] = jnp.zeros_like(h_ref)

    before = _before(L, d)
    r2, c2 = _iota2(L2, L2)
    order2 = ((c2 & (L - 1)) - (r2 & (L - 1))) * sgn
    strict2 = order2 < 0
    incl2 = order2 <= 0
    eye2 = jnp.where(r2 == c2, 1.0, 0.0)
    rw, cw = _iota2(W, W)
    eye_w = rw == cw
    head0 = lax.broadcasted_iota(jnp.int32, (L, W), 1) < D_HEAD_DIM
    stack = lambda x: jnp.concatenate([jnp.where(head0, x, 0.0), jnp.where(head0, 0.0, x)], axis=0)
    n_levels = int(math.log2(L))

    npair = h_ref.shape[0]
    slices = _chunk_slices(nck, L, d)
    where = [(sl, slice(p * W, (p + 1) * W)) for p in range(npair) for sl in slices]
    r, k, v, a, b, lw = ([ref[sl, lanes] for sl, lanes in where]
                         for ref in (r_ref, k_ref, v_ref, a_ref, b_ref, lw_ref))
    cum = _each(lambda x: _mask_nn(before, x), lw)
    cum_end = _each(lambda c: _end_row(c, d), cum)
    e_neg = _each(lambda c: jnp.exp(-c), cum)
    e_end = _each(lambda ce, c: jnp.exp(ce - c), cum_end, cum)
    at = _each(lambda x, c, w: stack(x * jnp.exp(c - w)), a, cum, lw)
    rt = _each(lambda x, c: stack(x * jnp.exp(c)), r, cum)
    bt = _each(lambda x, e: stack(x * e), b, e_neg)
    kt = _each(lambda x, e: stack(x * e), k, e_neg)
    vs = _each(stack, v)
    gram = _each(lambda p, q, s, t: _nt(jnp.concatenate([p, q], axis=0), jnp.concatenate([s, t], axis=0)),
                 at, rt, bt, kt)
    nmat = _each(lambda g: jnp.where(strict2, g[:L2, :L2], 0.0), gram)
    a_k = _each(lambda g: jnp.where(strict2, g[:L2, L2:], 0.0), gram)
    r_bk = _each(lambda g: jnp.where(jnp.concatenate([incl2, incl2], axis=1), g[L2:, :], 0.0), gram)
    tinv = _each(lambda n: eye2 + n, nmat)
    pw = _each(lambda n: _nn(n, n), nmat)
    for lev in range(1, n_levels):
        if lev < n_levels - 1:
            both = _each(lambda p, t: _nn(p, jnp.concatenate([p, t], axis=1)), pw, tinv)
            pw = _each(lambda x: x[:, :L2], both)
            tinv = _each(lambda t, x: t + x[:, L2:], tinv, both)
        else:
            tinv = _each(lambda p, t: t + _nn(p, t), pw, tinv)
    akv = _each(_nn, a_k, vs)
    wu = _each(lambda t, p, q: _nn(t, jnp.concatenate([p, q], axis=1)), tinv, at, akv)
    zs = _each(lambda x, y: jnp.concatenate([x, jnp.concatenate([jnp.zeros_like(y), y], axis=1)], axis=0), wu, vs)
    qy = _each(_nn, r_bk, zs)
    md = _each(lambda x, y, e, z: _tn(jnp.concatenate([stack(x * e), stack(y * e)], axis=0), z),
               b, k, e_end, zs)
    dec = _each(lambda ce: jnp.sum(jnp.where(eye_w, jnp.broadcast_to(jnp.exp(ce), (W, W)), 0.0),
                                   axis=1, keepdims=True), cum_end)

    def head_mean(x):
        m0 = jnp.sum(jnp.where(head0, x, 0.0), axis=-1, keepdims=True)
        m1 = jnp.sum(jnp.where(head0, 0.0, x), axis=-1, keepdims=True)
        return jnp.where(head0, m0, m1) * (1.0 / D_HEAD_DIM)

    hs = [h_ref[p] for p in range(npair)]
    for i in range(nck):
        for p in range(npair):
            s = p * nck + i
            sl, lanes = where[s]
            ys = _nn(rt[s] + qy[s][:, :W], hs[p]) + qy[s][:, W:]
            y = ys[:L, :] + ys[L:, :]
            hs[p] = dec[s] * hs[p] + _nn(md[s][:, :W], hs[p]) + md[s][:, W:]
            if d == 0:
                y_ref[sl, lanes] = y
            else:
                par = par_ref[:, lanes]
                tot = y0_ref[sl, lanes] + y
                cen = tot - head_mean(tot)
                nrm = cen * lax.rsqrt(head_mean(cen * cen) + RWKV_EPS)
                bonus = head_mean(r[s] * k[s] * par[0:1, :]) * float(D_HEAD_DIM) * v[s]
                y_ref[sl, lanes] = ((nrm * par[1:2, :] + par[2:3, :] + bonus) * g_ref[sl, lanes]).astype(y_ref.dtype)
    for p in range(npair):
        h_ref[p] = hs[p]


def _rwkv_scan(r, k, v, a, b, lw, g, par, n_ctx):
    bsz, t, c = r.shape
    L = RWKV_CHUNK
    tb = SCAN_TIME_BLOCK
    nck, nb, ncb = tb // L, t // tb, n_ctx // tb
    y0 = None
    for d in (0, 1):
        tix = lambda j, d=d: _time_index(d, j, ncb, nb)
        seq = pl.BlockSpec((None, tb, c), lambda i, j: (i, tix(j), 0))
        in_specs = [seq] * 5 + [pl.BlockSpec((None, None, tb, c), lambda i, j, d=d: (d, i, tix(j), 0))]
        args = [r, k, v, a, b, lw]
        if d == 1:
            in_specs += [seq, seq, pl.BlockSpec(par.shape, lambda i, j: (0, 0))]
            args += [y0, g, par]
        y0 = pl.pallas_call(
            functools.partial(_rwkv_kernel, L=L, nck=nck, d=d),
            grid=(bsz, nb),
            in_specs=in_specs,
            out_specs=seq,
            out_shape=jax.ShapeDtypeStruct((bsz, t, c), BF16 if d else F32),
            scratch_shapes=[pltpu.VMEM((c // (2 * D_HEAD_DIM), 2 * D_HEAD_DIM, 2 * D_HEAD_DIM), F32)],
            compiler_params=_scan_params(2),
            name="rwkv7_scan",
        )(*args)
    return y0


def _neighbours(x, n_ctx):
    pos = jnp.arange(x.shape[1])[None, :, None]
    prev = jnp.pad(x[:, :-1], ((0, 0), (1, 0), (0, 0)))
    nxt = jnp.pad(x[:, 1:], ((0, 0), (0, 1), (0, 0)))
    return jnp.where(pos == n_ctx, 0.0, prev), jnp.where(pos == n_ctx - 1, 0.0, nxt)


def _dwconv3(x, w, b, n_ctx):
    prev, nxt = _neighbours(x, n_ctx)
    return prev * w[0] + x * w[1] + nxt * w[2] + b


def _softplus(x):
    return jnp.maximum(x, 0.0) + jnp.log(1.0 + jnp.exp(-jnp.abs(x)))


def _segment_neighbours(x, before_tile, after_tile):
    n = x.shape[0]
    row = lax.broadcasted_iota(jnp.int32, x.shape, 0)
    prev = jnp.where(row == 0, before_tile, pltpu.roll(x, 1, axis=0))
    nxt = jnp.where(row == n - 1, after_tile, pltpu.roll(x, n - 1, axis=0))
    return prev, nxt


def _even_prep_kernel(p_ref, prev_ref, next_ref, ca_ref, cb_ref, sm_ref,
                      xs_ref, bm_ref, cm_ref, dt_ref, la_ref, q_ref, k_ref, g_ref, *, tiles_per_seq, ctx_tiles):
    j = pl.program_id(0) % tiles_per_seq
    first = jnp.logical_or(j == 0, j == ctx_tiles)
    last = jnp.logical_or(j == ctx_tiles - 1, j == tiles_per_seq - 1)

    def conv_silu(lo, width, taps_ref):
        x = p_ref[:, lo:lo + width]
        prev, nxt = _segment_neighbours(x, jnp.where(first, 0.0, prev_ref[7:8, lo:lo + width]),
                                        jnp.where(last, 0.0, next_ref[0:1, lo:lo + width]))
        taps = taps_ref[...]
        return _silu(prev * taps[0:1, :] + x * taps[1:2, :] + nxt * taps[2:3, :] + taps[3:4, :])

    xbc = conv_silu(A_INNER, A_XBC, ca_ref)
    xs_ref[...] = xbc[:, :A_INNER]
    bm_ref[...] = xbc[:, A_INNER:A_INNER + A_GROUPS * A_STATE]
    cm_ref[...] = xbc[:, A_INNER + A_GROUPS * A_STATE:]
    qk = conv_silu(EV_B0, 2 * B_QK, cb_ref)
    q_ref[...] = qk[:, :B_QK]
    k_ref[...] = qk[:, B_QK:]
    sm = sm_ref[...]
    nh2 = 2 * A_HEADS
    dt = _softplus(p_ref[:, A_INNER + A_XBC:A_INNER + A_XBC + nh2] + sm[0:1, :])
    la = dt * sm[1:2, :]
    g0 = EV_B0 + 2 * B_QK + 2 * B_INNER
    gx = p_ref[:, g0:g0 + 4 * B_HEADS]
    g = jnp.where(sm[4:5, :] > 0.5, -_softplus(-(gx + sm[3:4, :])), gx + sm[2:3, :])
    for d in range(2):
        dt_ref[d] = dt[:, d * A_HEADS:(d + 1) * A_HEADS]
        la_ref[d] = la[:, d * A_HEADS:(d + 1) * A_HEADS]
        g_ref[d] = g[:, d * 2 * B_HEADS:(d + 1) * 2 * B_HEADS]


def _gate_lane_order(a):
    h = B_HEADS
    return jnp.concatenate([a[..., :-4 * h], a[..., -4 * h:-3 * h], a[..., -2 * h:-h], a[..., -3 * h:-2 * h], a[..., -h:]],
                           axis=-1)


def _even_mixers(p, n_ctx, ssd_params, mlstm_params):
    conv_w, conv_b, dt_bias, a_log, d_skip, norm_a = ssd_params
    conv_bw, conv_bb, i_bias, f_bias, norm_b = mlstm_params
    b, t, n_pad = p.shape
    p2 = p.reshape(b * t, n_pad)
    tiles = b * t // ROW_TILE
    halo = ROW_TILE // 8
    zeros = jnp.zeros((B_HEADS,), F32)
    sm = jnp.stack([dt_bias.reshape(-1), -jnp.exp(a_log).reshape(-1),
                    jnp.concatenate([i_bias[0], zeros, i_bias[1], zeros]),
                    jnp.concatenate([zeros, f_bias[0], zeros, f_bias[1]]),
                    jnp.concatenate([zeros, zeros + 1.0, zeros, zeros + 1.0])])
    ca = jnp.concatenate([conv_w, conv_b[None, :]], axis=0)
    cb = jnp.concatenate([conv_bw, conv_bb[None, :]], axis=0)
    whole = lambda a: pl.BlockSpec(a.shape, lambda i: (0,) * a.ndim)
    rows = lambda c: pl.BlockSpec((ROW_TILE, c), lambda i: (i, 0))
    per_dir = lambda c: pl.BlockSpec((2, ROW_TILE, c), lambda i: (0, i, 0))
    f32 = lambda *s: jax.ShapeDtypeStruct(s, F32)
    n = b * t
    gn = A_GROUPS * A_STATE
    xs, bm, cm, dt, la, q, k, g = pl.pallas_call(
        functools.partial(_even_prep_kernel, tiles_per_seq=t // ROW_TILE, ctx_tiles=n_ctx // ROW_TILE),
        grid=(tiles,),
        in_specs=[pl.BlockSpec((ROW_TILE, n_pad), lambda i: (i, 0)),
                  pl.BlockSpec((8, n_pad), lambda i: (jnp.maximum(i * halo - 1, 0), 0)),
                  pl.BlockSpec((8, n_pad), lambda i: (jnp.minimum((i + 1) * halo, tiles * halo - 1), 0)),
                  whole(ca), whole(cb), whole(sm)],
        out_specs=[rows(A_INNER), rows(gn), rows(gn), per_dir(A_HEADS), per_dir(A_HEADS),
                   rows(B_QK), rows(B_QK), per_dir(2 * B_HEADS)],
        out_shape=[f32(n, A_INNER), f32(n, gn), f32(n, gn), f32(2, n, A_HEADS), f32(2, n, A_HEADS),
                   f32(n, B_QK), f32(n, B_QK), f32(2, n, 2 * B_HEADS)],
        compiler_params=pltpu.CompilerParams(dimension_semantics=("arbitrary",), vmem_limit_bytes=VMEM_LIMIT_BYTES),
        name="even_prep",
    )(p2, p2, p2, ca, cb, sm)
    seq = lambda a: a.reshape(b, t, a.shape[-1])
    seq_d = lambda a: a.reshape(2, b, t, a.shape[-1])
    par = jnp.stack([jnp.repeat(d_skip, A_HEAD_DIM), norm_a])
    fa = _ssd_scan(seq(xs), seq_d(dt), seq_d(la), seq(bm), seq(cm), p, par, n_ctx)
    fb = _mlstm_scan(seq(q), seq(k), seq_d(g), p, norm_b[None, :], n_ctx)
    return fa, fb


def _hgrn2_mixer(p, n_ctx, lb, params):
    f_bias, norm_w = params
    return _gla_scan(p, lb, f_bias, norm_w, n_ctx)


def _rwkv_prep_kernel(p_ref, prev_ref, next_ref, mu_ref, w2_ref, a2_ref, g2_ref, vec_ref,
                      r_ref, k_ref, v_ref, a_ref, b_ref, g_ref, lw_ref, *, tiles_per_seq, ctx_tiles):
    c = D_INNER
    j = pl.program_id(0) % tiles_per_seq
    x = p_ref[:, P_C:]
    n = x.shape[0]
    row = lax.broadcasted_iota(jnp.int32, x.shape, 0)
    first = jnp.logical_or(j == 0, j == ctx_tiles)
    last = jnp.logical_or(j == ctx_tiles - 1, j == tiles_per_seq - 1)
    before_tile = jnp.where(first, 0.0, prev_ref[7:8, P_C:])
    after_tile = jnp.where(last, 0.0, next_ref[0:1, P_C:])
    prev = jnp.where(row == 0, before_tile, pltpu.roll(x, 1, axis=0))
    nxt = jnp.where(row == n - 1, after_tile, pltpu.roll(x, n - 1, axis=0))
    x = x + mu_ref[...] * (0.5 * (prev + nxt) - x)
    r, k, v = x[:, :c], x[:, c:2 * c], x[:, 2 * c:3 * c]
    o = 3 * c
    wl = jnp.tanh(x[:, o:o + 2 * D_W_LORA])
    gl = x[:, o + 2 * D_W_LORA:o + 2 * D_W_LORA + D_G_LORA]
    al = x[:, o + 2 * D_W_LORA + D_G_LORA:o + 2 * D_W_LORA + D_G_LORA + D_A_LORA]
    vec = vec_ref[...]
    for d in range(2):
        w = vec[d:d + 1, :] + _nn(wl, w2_ref[d])
        z = -w
        softplus = jnp.maximum(z, 0.0) + jnp.log(1.0 + jnp.exp(-jnp.abs(z)))
        lw_ref[d] = -jnp.exp(-softplus - 0.5)
    a = jax.nn.sigmoid(vec[2:3, :] + _nn(al, a2_ref[...]))
    g_ref[...] = _nn(jax.nn.sigmoid(gl), g2_ref[...])
    kx = k * vec[3:4, :]
    sq = kx * kx
    head0 = (lax.broadcasted_iota(jnp.int32, (n, 2 * D_HEAD_DIM), 1) < D_HEAD_DIM)
    sums = []
    for s in range(c // (2 * D_HEAD_DIM)):
        blk = sq[:, s * 2 * D_HEAD_DIM:(s + 1) * 2 * D_HEAD_DIM]
        s0 = jnp.sum(jnp.where(head0, blk, 0.0), axis=-1, keepdims=True)
        s1 = jnp.sum(jnp.where(head0, 0.0, blk), axis=-1, keepdims=True)
        sums.append(jnp.where(head0, s0, s1))
    kk = kx * lax.rsqrt(jnp.maximum(jnp.concatenate(sums, axis=1), 1e-12))
    r_ref[...] = r
    k_ref[...] = k * (1.0 + (a - 1.0) * vec[4:5, :])
    v_ref[...] = v
    a_ref[...] = -kk
    b_ref[...] = kk * a


def _rwkv7_mixer(p, n_ctx, params):
    mu, w0, w2, a0, a2, g2, k_k, k_a, r_k, ln_w, ln_b = params
    b, t, n_pad = p.shape
    c = D_INNER
    width = n_pad - P_C
    p2 = p.reshape(b * t, n_pad)
    tiles = b * t // ROW_TILE
    halo = ROW_TILE // 8
    w2z = jnp.zeros((2, 2 * D_W_LORA, c), F32)
    w2z = w2z.at[0, :D_W_LORA].set(w2[0]).at[1, D_W_LORA:].set(w2[1])
    vec = jnp.stack([w0[0], w0[1], a0, k_k, k_a])
    whole = lambda a: pl.BlockSpec(a.shape, lambda i: (0,) * a.ndim)
    rows = pl.BlockSpec((ROW_TILE, c), lambda i: (i, 0))
    mu_pad = jnp.pad(mu, (0, width - mu.shape[0]))[None, :]
    outs = pl.pallas_call(
        functools.partial(_rwkv_prep_kernel, tiles_per_seq=t // ROW_TILE, ctx_tiles=n_ctx // ROW_TILE),
        grid=(tiles,),
        in_specs=[pl.BlockSpec((ROW_TILE, n_pad), lambda i: (i, 0)),
                  pl.BlockSpec((8, n_pad), lambda i: (jnp.maximum(i * halo - 1, 0), 0)),
                  pl.BlockSpec((8, n_pad), lambda i: (jnp.minimum((i + 1) * halo, tiles * halo - 1), 0)),
                  whole(mu_pad), whole(w2z), whole(a2), whole(g2), whole(vec)],
        out_specs=[rows] * 6 + [pl.BlockSpec((2, ROW_TILE, c), lambda i: (0, i, 0))],
        out_shape=[jax.ShapeDtypeStruct((b * t, c), F32)] * 6 + [jax.ShapeDtypeStruct((2, b * t, c), F32)],
        compiler_params=pltpu.CompilerParams(dimension_semantics=("arbitrary",), vmem_limit_bytes=VMEM_LIMIT_BYTES),
        name="rwkv7_prep",
    )(p2, p2, p2, mu_pad, w2z, a2, g2, vec)
    r, k, v, a, bb, g = (u.reshape(b, t, c) for u in outs[:6])
    par = jnp.stack([r_k.reshape(D_INNER), ln_w, ln_b])
    return _rwkv_scan(r, k, v, a, bb, outs[6].reshape(2, b, t, c), g, par, n_ctx)


def _to_col_major(u, rows):
    b, s, d = u.shape
    return u.reshape(b, rows, GRID_W, d).transpose(0, 2, 1, 3).reshape(b, s, d)


def _from_col_major(u, rows):
    b, s, d = u.shape
    return u.reshape(b, GRID_W, rows, d).transpose(0, 2, 1, 3).reshape(b, s, d)


def _tile_specs(bsz, t, n_ctx):
    tiles_per_seq, ctx_tiles = t // ROW_TILE, n_ctx // ROW_TILE
    mod_row = lambda i: jnp.where(i % tiles_per_seq < ctx_tiles, bsz, i // tiles_per_seq)
    rows = lambda c: pl.BlockSpec((ROW_TILE, c), lambda i: (i, 0))
    whole = lambda a: pl.BlockSpec(a.shape, lambda i: (0,) * a.ndim)
    mod = pl.BlockSpec((None, 6, D_MODEL), lambda i: (mod_row(i), 0, 0))
    params = pltpu.CompilerParams(dimension_semantics=("arbitrary",), vmem_limit_bytes=VMEM_LIMIT_BYTES)
    return rows, whole, mod, params


def _in_proj_kernel(x_ref, mod_ref, w_ref, o_ref):
    m = mod_ref[...]
    o_ref[...] = _nn(x_ref[...] * (1.0 + m[1:2, :]) + m[0:1, :], w_ref[...])


def _in_proj(xa, mods, w, bsz, n_ctx):
    n = w.shape[1]
    n_pad = -(-n // LANES) * LANES
    wb = jnp.pad(w.astype(BF16), ((0, 0), (0, n_pad - n)))
    rows, whole, mod, params = _tile_specs(bsz, xa.shape[0] // bsz, n_ctx)
    return pl.pallas_call(
        _in_proj_kernel,
        grid=(xa.shape[0] // ROW_TILE,),
        in_specs=[rows(D_MODEL), mod, whole(wb)],
        out_specs=rows(n_pad),
        out_shape=jax.ShapeDtypeStruct((xa.shape[0], n_pad), F32),
        compiler_params=params,
        name="in_proj",
    )(xa, mods, wb)


def _norm_rows(z, ln):
    mu = jnp.mean(z, axis=-1, keepdims=True)
    zc = z - mu
    var = jnp.mean(zc * zc, axis=-1, keepdims=True)
    return zc * lax.rsqrt(var + LN_EPS) * ln[0:1, :] + ln[1:2, :]


def _out_proj_kernel(fa_ref, fb_ref, w_ref, x_ref, mod_ref, ln_ref, xo_ref, h_ref, hb_ref):
    ka = fa_ref.shape[1]
    m = mod_ref[...]
    y = _nn(fa_ref[...], w_ref[:ka, :]) + _nn(fb_ref[...], w_ref[ka:, :])
    xn = _norm_rows(DEEPNORM_ALPHA * x_ref[...] + m[2:3, :] * y, ln_ref[...])
    xo_ref[...] = xn
    h = xn * (1.0 + m[4:5, :]) + m[3:4, :]
    h_ref[...] = h
    hb_ref[...] = h.astype(BF16)


def _out_proj(fa, fb, w, xa, mods, ln, bsz, n_ctx):
    t_all, d = xa.shape
    rows, whole, mod, params = _tile_specs(bsz, t_all // bsz, n_ctx)
    wb = w.astype(BF16)
    return pl.pallas_call(
        _out_proj_kernel,
        grid=(t_all // ROW_TILE,),
        in_specs=[rows(fa.shape[1]), rows(fb.shape[1]), whole(wb), rows(d), mod, whole(ln)],
        out_specs=[rows(d), rows(d), rows(d)],
        out_shape=[jax.ShapeDtypeStruct((t_all, d), F32), jax.ShapeDtypeStruct((t_all, d), F32),
                   jax.ShapeDtypeStruct((t_all, d), BF16)],
        compiler_params=params,
        name="out_proj_norm",
    )(fa, fb, wb, xa, mods, ln)


def _ffn_norm_kernel(f_ref, x_ref, mod_ref, ln_ref, xo_ref):
    m = mod_ref[...]
    xo_ref[...] = _norm_rows(DEEPNORM_ALPHA * x_ref[...] + m[5:6, :] * f_ref[...], ln_ref[...])


def _ffn_norm(f, xa, mods, ln, bsz, n_ctx):
    t_all, d = xa.shape
    rows, whole, mod, params = _tile_specs(bsz, t_all // bsz, n_ctx)
    return pl.pallas_call(
        _ffn_norm_kernel,
        grid=(t_all // ROW_TILE,),
        in_specs=[rows(d), rows(d), mod, whole(ln)],
        out_specs=rows(d),
        out_shape=jax.ShapeDtypeStruct((t_all, d), F32),
        compiler_params=params,
        name="ffn_residual_norm",
    )(f, xa, mods, ln)


def kernel(x, c, ctx, c_ctx, mod_w, mod_b, ln_g, ln_b, ev_w_in, ev_w_out, ssd_conv_w, ssd_conv_b, ssd_dt_bias, ssd_a_log, ssd_d, ssd_norm_w, mlstm_conv_w, mlstm_conv_b, mlstm_i_bias, mlstm_f_bias, mlstm_norm_w, od_w_in, od_w_out, hgrn_lb_logits, hgrn_f_bias, hgrn_norm_w, rwkv_mu, rwkv_w0, rwkv_w2, rwkv_a0, rwkv_a2, rwkv_g2, rwkv_k_k, rwkv_k_a, rwkv_r_k, rwkv_ln_w, rwkv_ln_b, router_w, router_bias, exp_w_gate, exp_w_up, exp_w_down):
    bsz, seq, _ = x.shape
    n_ctx = ctx.shape[1]
    rows = seq // GRID_W
    lb_all = jnp.cumsum(jax.nn.softmax(hgrn_lb_logits.astype(F32), axis=0), axis=0)
    lb_all = lb_all - lb_all[0]
    s_c = jax.nn.silu(c)
    s_cc = jax.nn.silu(c_ctx)
    t = n_ctx + seq
    xa = jnp.concatenate([ctx, x], axis=1).reshape(bsz * t, D_MODEL)
    seq3 = lambda a: a.reshape(bsz, t, a.shape[-1])
    flat = lambda a: a.reshape(bsz * t, a.shape[-1])
    lat_order = lambda a, f: flat(jnp.concatenate([seq3(a)[:, :n_ctx], f(seq3(a)[:, n_ctx:], rows)], axis=1))
    for layer in range(DEPTH):
        i = layer // 2
        mods = _matmul(jnp.concatenate([s_c, s_cc[None]], axis=0), mod_w[layer], tm=8, tn=512) + mod_b[layer]
        mods = mods.reshape(bsz + 1, 6, D_MODEL)
        ln = jnp.stack([ln_g[layer], ln_b[layer]], axis=1)
        if layer % 2 == 0:
            w_in = jnp.concatenate([ev_w_in[i][:, :P_A], jnp.zeros((D_MODEL, EV_B0 - P_A), F32),
                                    _gate_lane_order(ev_w_in[i][:, P_A:])], axis=1)
            p = seq3(_in_proj(xa, mods, w_in, bsz, n_ctx))
            fa, fb = _even_mixers(
                p, n_ctx,
                (ssd_conv_w[i], ssd_conv_b[i], ssd_dt_bias[i], ssd_a_log[i], ssd_d[i], ssd_norm_w[i]),
                (mlstm_conv_w[i], mlstm_conv_b[i], mlstm_i_bias[i], mlstm_f_bias[i], mlstm_norm_w[i]))
            fa, fb, w_out = flat(fa), flat(fb), ev_w_out[i]
        else:
            g0, g1 = 3 * D_INNER + 2 * D_W_LORA + D_A_LORA, P_D
            lora_last = lambda a: jnp.concatenate([a[..., :g0 - D_A_LORA], a[..., g0:g1], a[..., g0 - D_A_LORA:g0]], axis=-1)
            w_in = jnp.concatenate([od_w_in[i][:, :P_C], lora_last(od_w_in[i][:, P_C:])], axis=1)
            p = seq3(_in_proj(lat_order(xa, _to_col_major), mods, w_in, bsz, n_ctx))
            fa = _hgrn2_mixer(p, n_ctx, lb_all[layer], (hgrn_f_bias[i], hgrn_norm_w[i]))
            fb = _rwkv7_mixer(p, n_ctx,
                              (lora_last(rwkv_mu[i]), rwkv_w0[i], rwkv_w2[i], rwkv_a0[i], rwkv_a2[i], rwkv_g2[i],
                               rwkv_k_k[i], rwkv_k_a[i], rwkv_r_k[i], rwkv_ln_w[i], rwkv_ln_b[i]))
            fa, fb, w_out = lat_order(fa, _from_col_major), lat_order(fb, _from_col_major), od_w_out[i]
        xa, h, hb = _out_proj(fa, fb, w_out, xa, mods, ln[0], bsz, n_ctx)
        f = _moe_ffn(h, hb, router_w, router_bias, exp_w_gate, exp_w_up, exp_w_down, layer)
        xa = _ffn_norm(f, xa, mods, ln[1], bsz, n_ctx)
    return seq3(xa)[:, n_ctx:]
```

```python
import functools
import math

import jax
import jax.numpy as jnp
from jax import lax
from jax.experimental import pallas as pl
from jax.experimental.pallas import tpu as pltpu

F32 = jnp.float32
BF16 = jnp.bfloat16

D_MODEL = 1024
DEPTH = 4
GRID_W = 64
A_HEADS = 8
A_HEAD_DIM = 64
A_INNER = A_HEADS * A_HEAD_DIM
A_GROUPS = 2
A_STATE = 64
A_XBC = A_INNER + 2 * A_GROUPS * A_STATE
B_HEADS = 4
B_QK_DIM = 64
B_V_DIM = 128
B_QK = B_HEADS * B_QK_DIM
B_INNER = B_HEADS * B_V_DIM
MLSTM_EPS = 1e-6
C_HEADS = 4
C_HEAD_DIM = 128
C_INNER = C_HEADS * C_HEAD_DIM
D_HEADS = 8
D_HEAD_DIM = 64
D_INNER = D_HEADS * D_HEAD_DIM
D_W_LORA = 64
D_A_LORA = 64
D_G_LORA = 128
RWKV_EPS = 64e-5
P_A = A_INNER + A_XBC + 2 * A_HEADS
P_B = 2 * B_QK + 2 * B_INNER + 4 * B_HEADS
P_C = 5 * C_INNER
P_D = 3 * D_INNER + 2 * D_W_LORA + D_A_LORA + D_G_LORA
EV_B0 = 3 * A_INNER
N_EXPERTS = 32
N_EXPERT_GROUPS = 8
EXPERTS_PER_GROUP = N_EXPERTS // N_EXPERT_GROUPS
TOP_K = 2
D_EXPERT = 512
MOE_BLOCK = 512
RANK_BLOCK = 512
ROW_TILE = 256
LANES = 128
DEEPNORM_ALPHA = (2 * DEPTH) ** 0.25
LN_EPS = 1e-5
M_INIT = -1e30
NEG_BIG = -1e30

SSD_CHUNK = 128
MLSTM_CHUNK = 128
GLA_CHUNK = 16
GLA_BLOCK = 64
RWKV_CHUNK = 64
SCAN_TIME_BLOCK = 256

VMEM_LIMIT_BYTES = 48 * 1024 * 1024
HI = lax.Precision.HIGHEST


def _dot(a, b, dims, exact):
    if exact:
        return lax.dot_general(a.astype(F32), b.astype(F32), (dims, ((), ())),
                               precision=HI, preferred_element_type=F32)
    return lax.dot_general(a.astype(BF16), b.astype(BF16), (dims, ((), ())),
                           preferred_element_type=F32)


def _nn(a, b, exact=False):
    return _dot(a, b, ((1,), (0,)), exact)


def _nt(a, b, exact=False):
    return _dot(a, b, ((1,), (1,)), exact)


def _tn(a, b, exact=False):
    return _dot(a, b, ((0,), (0,)), exact)


def _iota2(n, m):
    return (lax.broadcasted_iota(jnp.int32, (n, m), 0),
            lax.broadcasted_iota(jnp.int32, (n, m), 1))


def _split3(x):
    x1 = x.astype(BF16)
    r1 = x - x1.astype(F32)
    x2 = r1.astype(BF16)
    x3 = (r1 - x2.astype(F32)).astype(BF16)
    return x1, x2, x3


def _mask_nn(mask, x):
    mb = mask.astype(BF16)
    x1, x2, x3 = _split3(x)
    return _nn(mb, x1) + _nn(mb, x2) + _nn(mb, x3)


def _nn_mask(x, mask):
    mb = mask.astype(BF16)
    x1, x2, x3 = _split3(x)
    return _nn(x1, mb) + _nn(x2, mb) + _nn(x3, mb)


def _each(f, *cols):
    return [f(*xs) for xs in zip(*cols)]


def _silu(x):
    return x * jax.nn.sigmoid(x)


def _mm_kernel(x_ref, w_ref, o_ref, *, exact):
    o_ref[...] = _nn(x_ref[...], w_ref[...], exact)


def _matmul(x, w, tm=512, tn=512, exact=False):
    m, k = x.shape
    n = w.shape[1]
    n_pad = -(-n // tn) * tn
    m_pad = -(-m // tm) * tm
    xb = x if exact else x.astype(BF16)
    wb = w if exact else w.astype(BF16)
    if n_pad != n:
        wb = jnp.pad(wb, ((0, 0), (0, n_pad - n)))
    if m_pad != m:
        xb = jnp.pad(xb, ((0, m_pad - m), (0, 0)))
    out = pl.pallas_call(
        functools.partial(_mm_kernel, exact=exact),
        grid=(n_pad // tn, m_pad // tm),
        in_specs=[pl.BlockSpec((tm, k), lambda j, i: (i, 0)),
                  pl.BlockSpec((k, tn), lambda j, i: (0, j))],
        out_specs=pl.BlockSpec((tm, tn), lambda j, i: (i, j)),
        out_shape=jax.ShapeDtypeStruct((m_pad, n_pad), F32),
        compiler_params=pltpu.CompilerParams(
            dimension_semantics=("arbitrary", "arbitrary"),
            vmem_limit_bytes=VMEM_LIMIT_BYTES),
        name="dense_matmul",
    )(xb, wb)
    return out[:m, :n]


def _moe_kernel(blk_e_ref, n_used_ref, x_ref, wg_ref, wu_ref, wd_ref, o_ref):
    i = pl.program_id(0)

    @pl.when(i < n_used_ref[0])
    def _():
        x = x_ref[...]
        g = _nn(x, wg_ref[...])
        u = _nn(x, wu_ref[...])
        o_ref[...] = _nn(g * jax.nn.sigmoid(g) * u, wd_ref[...]).astype(o_ref.dtype)

    @pl.when(i >= n_used_ref[0])
    def _():
        o_ref[...] = jnp.zeros_like(o_ref)


def _moe_experts(xp, blk_e, n_used, w_gate, w_up, w_down, layer):
    n_rows, d = xp.shape
    n_blocks = n_rows // MOE_BLOCK
    grid_spec = pltpu.PrefetchScalarGridSpec(
        num_scalar_prefetch=2,
        grid=(n_blocks,),
        in_specs=[
            pl.BlockSpec((MOE_BLOCK, d), lambda i, be, nu: (i, 0)),
            pl.BlockSpec((None, None, d, D_EXPERT), lambda i, be, nu: (layer, be[i], 0, 0)),
            pl.BlockSpec((None, None, d, D_EXPERT), lambda i, be, nu: (layer, be[i], 0, 0)),
            pl.BlockSpec((None, None, D_EXPERT, d), lambda i, be, nu: (layer, be[i], 0, 0)),
        ],
        out_specs=pl.BlockSpec((MOE_BLOCK, d), lambda i, be, nu: (i, 0)),
    )
    return pl.pallas_call(
        _moe_kernel,
        grid_spec=grid_spec,
        out_shape=jax.ShapeDtypeStruct((n_rows, d), BF16),
        compiler_params=pltpu.CompilerParams(
            dimension_semantics=("arbitrary",),
            vmem_limit_bytes=VMEM_LIMIT_BYTES),
        name="moe_experts",
    )(blk_e, n_used, xp, w_gate, w_up, w_down)


def _top2(vals):
    m = len(vals)
    m1 = functools.reduce(jnp.maximum, vals)
    i1 = jnp.full_like(m1, float(m - 1))
    for j in reversed(range(m - 1)):
        i1 = jnp.where(vals[j] == m1, float(j), i1)
    rest = [jnp.where(i1 == float(j), -jnp.inf, vals[j]) for j in range(m)]
    m2 = functools.reduce(jnp.maximum, rest)
    i2 = jnp.full_like(m1, float(m - 1))
    for j in reversed(range(m - 1)):
        i2 = jnp.where(rest[j] == m2, float(j), i2)
    return m1, i1, m2, i2


def _router_kernel(h_ref, wt_ref, bias_ref, e_ref, w_ref, rank_ref, cnt_ref, carry_ref):
    tm = h_ref.shape[0]
    ng, per = N_EXPERT_GROUPS, EXPERTS_PER_GROUP

    @pl.when(pl.program_id(0) == 0)
    def _():
        carry_ref[...] = jnp.zeros_like(carry_ref)

    s = jax.nn.sigmoid(_nt(wt_ref[...], h_ref[...], True))
    sb = s + bias_ref[...]
    biased = [sb[j * ng:(j + 1) * ng, :] for j in range(per)]
    plain = [s[j * ng:(j + 1) * ng, :] for j in range(per)]
    m1, _, m2, _ = _top2(biased)
    gsum = m1 + m2
    rows = lax.broadcasted_iota(jnp.int32, (ng, tm), 0).astype(F32)
    gmax = jnp.max(gsum, axis=0, keepdims=True)
    gi = jnp.min(jnp.where(gsum == gmax, rows, float(ng)), axis=0, keepdims=True)
    sel = rows == gi
    pick = lambda v: jnp.sum(jnp.where(sel, v, 0.0), axis=0, keepdims=True)
    in_b = [pick(v) for v in biased]
    in_s = [pick(v) for v in plain]
    _, l1, _, l2 = _top2(in_b)
    w1 = functools.reduce(jnp.add, [jnp.where(l1 == float(j), in_s[j], 0.0) for j in range(per)])
    w2 = functools.reduce(jnp.add, [jnp.where(l2 == float(j), in_s[j], 0.0) for j in range(per)])
    e1 = gi * float(per) + l1
    e2 = gi * float(per) + l2
    wsum = w1 + w2
    e_ref[0:1, :] = e1.astype(jnp.int32)
    e_ref[1:2, :] = e2.astype(jnp.int32)
    w_ref[0:1, :] = w1 / wsum
    w_ref[1:2, :] = w2 / wsum
    row = lax.broadcasted_iota(jnp.int32, (N_EXPERTS, tm), 0)
    experts = ((row % ng) * per + row // ng).astype(F32)
    oh1 = jnp.where(experts == e1, 1.0, 0.0)
    oh2 = jnp.where(experts == e2, 1.0, 0.0)
    oh = oh1 + oh2
    ri, ci = _iota2(tm, tm)
    seen = _nn(oh, jnp.where(ri < ci, 1.0, 0.0)) + carry_ref[...]
    rank_ref[0:1, :] = jnp.sum(oh1 * seen, axis=0, keepdims=True).astype(jnp.int32)
    rank_ref[1:2, :] = jnp.sum(oh2 * seen, axis=0, keepdims=True).astype(jnp.int32)
    carry = carry_ref[...] + jnp.sum(oh, axis=1, keepdims=True)
    carry_ref[...] = carry
    cnt_ref[...] = carry.astype(jnp.int32)


def _route(h, router_w, router_bias):
    t, d = h.shape
    tm = RANK_BLOCK
    kt = lambda dt: jax.ShapeDtypeStruct((TOP_K, t), dt)
    blk = pl.BlockSpec((TOP_K, tm), lambda i: (0, i))
    member_major = lambda a: a.reshape(N_EXPERT_GROUPS, EXPERTS_PER_GROUP, -1).transpose(1, 0, 2).reshape(N_EXPERTS, -1)
    e, w, rank, counts = pl.pallas_call(
        _router_kernel,
        grid=(t // tm,),
        in_specs=[pl.BlockSpec((tm, d), lambda i: (i, 0)),
                  pl.BlockSpec((N_EXPERTS, d), lambda i: (0, 0)),
                  pl.BlockSpec((N_EXPERTS, 1), lambda i: (0, 0))],
        out_specs=[blk, blk, blk, pl.BlockSpec((N_EXPERTS, 1), lambda i: (0, 0))],
        out_shape=[kt(jnp.int32), kt(F32), kt(jnp.int32), jax.ShapeDtypeStruct((N_EXPERTS, 1), jnp.int32)],
        scratch_shapes=[pltpu.VMEM((N_EXPERTS, 1), F32)],
        compiler_params=pltpu.CompilerParams(dimension_semantics=("arbitrary",),
                                             vmem_limit_bytes=VMEM_LIMIT_BYTES),
        name="moe_router",
    )(h, member_major(router_w.T), member_major(router_bias.astype(F32).reshape(N_EXPERTS, 1)))
    counts = counts.reshape(EXPERTS_PER_GROUP, N_EXPERT_GROUPS).T.reshape(N_EXPERTS)
    return e, w, rank, counts


def _moe_ffn(h, hb, router_w, router_bias, w_gate, w_up, w_down, layer):
    t, d = h.shape
    expert, wts, rank, counts = _route(h, router_w, router_bias)
    n_assign = t * TOP_K
    padded = (counts + MOE_BLOCK - 1) // MOE_BLOCK * MOE_BLOCK
    pends = jnp.cumsum(padded)
    pstarts = pends - padded
    start_of = jnp.sum(jnp.where(expert[..., None] == jnp.arange(N_EXPERTS, dtype=jnp.int32), pstarts, 0), axis=-1)
    dest = start_of + rank
    n_blocks = -(-n_assign // MOE_BLOCK) + N_EXPERTS
    slot_token = jnp.arange(n_blocks * MOE_BLOCK, dtype=jnp.int32) % t
    slot_token = slot_token.at[dest.reshape(-1)].set(jnp.tile(jnp.arange(t, dtype=jnp.int32), TOP_K),
                                                      unique_indices=True)
    xp = hb[slot_token]
    blk_start = jnp.arange(n_blocks, dtype=jnp.int32) * MOE_BLOCK
    blk_e = jnp.minimum(jnp.sum(pends[None, :] <= blk_start[:, None], axis=1), N_EXPERTS - 1).astype(jnp.int32)
    n_used = (pends[-1] // MOE_BLOCK).astype(jnp.int32).reshape(1)
    yp = _moe_experts(xp, blk_e, n_used, w_gate, w_up, w_down, layer)
    return [yp[dest[kk]] for kk in range(TOP_K)], wts.T


def _time_index(d, j, n_ctx_blocks, n_blocks):
    if d == 0:
        return j
    return jnp.where(j < n_ctx_blocks, n_ctx_blocks - 1 - j, n_blocks - 1 - j + n_ctx_blocks)


def _end_row(x, d):
    n = x.shape[0]
    return x[0:1, :] if d == 1 else x[n - 1:n, :]


def _before(n, d):
    ri, ci = _iota2(n, n)
    return ci >= ri if d == 1 else ci <= ri


def _chunk_slices(nck, L, d):
    order = range(nck - 1, -1, -1) if d == 1 else range(nck)
    return [slice(i * L, (i + 1) * L) for i in order]


def _scan_params(n_axes):
    return pltpu.CompilerParams(dimension_semantics=("arbitrary",) * n_axes, vmem_limit_bytes=VMEM_LIMIT_BYTES)


def _ssd_kernel(*refs, L, d):
    if d == 0:
        x_ref, dt_ref, lac_ref, lar_ref, b_ref, c_ref, y_ref, st_ref = refs
    else:
        x_ref, dt_ref, lac_ref, lar_ref, b_ref, c_ref, y0_ref, z_ref, par_ref, y_ref, st_ref = refs
    hp, gn = A_INNER, A_GROUPS * A_STATE
    hpg = A_HEADS // A_GROUPS

    @pl.when(pl.program_id(1) == 0)
    def _():
        st_ref[...] = jnp.zeros_like(st_ref)

    before = _before(L, d)
    xs = x_ref[...]
    bm = b_ref[...]
    cm = c_ref[...]
    ccol = _mask_nn(before, lac_ref[...])
    crow = _nn_mask(lar_ref[...], _before(L, 1 - d))
    head_of_lane = lax.broadcasted_iota(jnp.int32, (A_HEADS, hp), 1) // A_HEAD_DIM
    expand = head_of_lane == lax.broadcasted_iota(jnp.int32, (A_HEADS, hp), 0)
    cum = _nn_mask(ccol, expand)
    x = xs * _nn_mask(dt_ref[...], expand)
    end = _end_row(cum, d)
    group_of_lane = lax.broadcasted_iota(jnp.int32, (L, gn), 1) // A_STATE
    first_of_pair = (lax.broadcasted_iota(jnp.int32, (L, 2 * A_HEAD_DIM), 1) < A_HEAD_DIM)
    cbs = [_nt(jnp.where(group_of_lane == g, cm, 0.0), bm) for g in range(A_GROUPS)]
    heads = list(range(A_HEADS))
    decay = [jnp.exp(jnp.where(before, ccol[:, h:h + 1] - crow[h:h + 1, :], NEG_BIG)) for h in heads]
    yh = [_nn(cbs[h // hpg] * decay[h], x[:, (h // 2) * 2 * A_HEAD_DIM:(h // 2 + 1) * 2 * A_HEAD_DIM]) for h in heads]
    pairs = [jnp.where(first_of_pair, yh[2 * p], yh[2 * p + 1]) for p in range(A_HEADS // 2)]
    st = st_ref[...]
    y = jnp.concatenate(pairs, axis=1) + jnp.exp(cum) * _nn(cm, st)
    own_group = (lax.broadcasted_iota(jnp.int32, (gn, hp), 0) // A_STATE
                 == lax.broadcasted_iota(jnp.int32, (gn, hp), 1) // (A_HEAD_DIM * hpg))
    st_ref[...] = jnp.exp(end) * st + jnp.where(own_group, _tn(bm, x * jnp.exp(end - cum)), 0.0)
    if d == 0:
        y_ref[...] = y
    else:
        par = par_ref[...]
        u = (y0_ref[...] + y + par[0:1, :] * xs) * _silu(z_ref[...])
        y_ref[...] = (u * lax.rsqrt(jnp.mean(u * u, axis=-1, keepdims=True) + 1e-6) * par[1:2, :]).astype(y_ref.dtype)


def _ssd_scan(xs, dt, la, bm, cm, p, par, n_ctx):
    b, t, hp = xs.shape
    L = SSD_CHUNK
    nb, ncb = t // L, n_ctx // L
    lar = jnp.swapaxes(la, 2, 3)
    y0 = None
    for d in (0, 1):
        tix = lambda j, d=d: _time_index(d, j, ncb, nb)
        seq = lambda c: pl.BlockSpec((None, L, c), lambda i, j: (i, tix(j), 0))
        in_specs = [seq(hp),
                    pl.BlockSpec((None, None, L, A_HEADS), lambda i, j, d=d: (d, i, tix(j), 0)),
                    pl.BlockSpec((None, None, L, A_HEADS), lambda i, j, d=d: (d, i, tix(j), 0)),
                    pl.BlockSpec((None, None, A_HEADS, L), lambda i, j, d=d: (d, i, 0, tix(j))),
                    seq(bm.shape[-1]), seq(bm.shape[-1])]
        args = [xs, dt, la, lar, bm, cm]
        if d == 1:
            in_specs += [seq(hp), seq(hp), pl.BlockSpec(par.shape, lambda i, j: (0, 0))]
            args += [y0, p, par]
        y0 = pl.pallas_call(
            functools.partial(_ssd_kernel, L=L, d=d),
            grid=(b, nb),
            in_specs=in_specs,
            out_specs=seq(hp),
            out_shape=jax.ShapeDtypeStruct((b, t, hp), BF16 if d else F32),
            scratch_shapes=[pltpu.VMEM((bm.shape[-1], hp), F32)],
            compiler_params=_scan_params(2),
            name="ssd_scan",
        )(*args)
    return y0


def _mlstm_kernel(*refs, L, d):
    if d == 0:
        q_ref, k_ref, v_ref, gc_ref, gr_ref, h_ref, c_ref, n_ref, m_ref = refs
    else:
        q_ref, k_ref, v_ref, gc_ref, gr_ref, h0_ref, o_ref, par_ref, h_ref, c_ref, n_ref, m_ref = refs
    nh, dk, dv = B_HEADS, B_QK_DIM, B_V_DIM

    @pl.when(pl.program_id(1) == 0)
    def _():
        c_ref[...] = jnp.zeros_like(c_ref)
        n_ref[...] = jnp.zeros_like(n_ref)
        m_ref[...] = jnp.full_like(m_ref, M_INIT)

    before = _before(L, d)
    gc = gc_ref[...]
    gr = gr_ref[...]
    fcol = _mask_nn(before, gc[:, nh:])
    frow = _nn_mask(gr[nh:, :], _before(L, 1 - d))
    lane_head = lax.broadcasted_iota(jnp.int32, (L, 2 * dk), 1) // dk
    heads = list(range(nh))
    slab = [slice((h // 2) * 2 * dk, (h // 2 + 1) * 2 * dk) for h in heads]
    lanes = [slice(h * dv, (h + 1) * dv) for h in heads]
    q = [jnp.where(lane_head == h % 2, q_ref[:, slab[h]], 0.0) * (dk ** -0.5) for h in heads]
    k = [k_ref[:, slab[h]] for h in heads]
    v = [v_ref[:, lanes[h]] for h in heads]
    li_c = [gc[:, h:h + 1] for h in heads]
    li_r = [gr[h:h + 1, :] for h in heads]
    f_c = [fcol[:, h:h + 1] for h in heads]
    f_r = [frow[h:h + 1, :] for h in heads]
    ftot = _each(lambda x: _end_row(x, d), f_c)
    c_prev = [c_ref[h] for h in heads]
    n_prev = [n_ref[h] for h in heads]
    m_prev = [m_ref[h] for h in heads]
    w_end = _each(lambda ft, fc, lc: ft - fc + lc, ftot, f_c, li_c)
    m_loc = _each(lambda w: jnp.max(w, axis=0, keepdims=True), w_end)
    ke = _each(lambda x, w, m: x * jnp.exp(w - m), k, w_end, m_loc)
    c_loc = _each(_tn, ke, v)
    n_loc = _each(lambda x: jnp.sum(x, axis=0, keepdims=True), ke)
    log_d = _each(lambda fc, fr, lr: jnp.where(before, fc - fr + lr, NEG_BIG), f_c, f_r, li_r)
    log_inter = _each(jnp.add, f_c, m_prev)
    m_row = _each(lambda ld, lint: jnp.maximum(jnp.max(ld, axis=-1, keepdims=True), lint), log_d, log_inter)
    s = _each(lambda a, b, ld, mr: _nt(a, b) * jnp.exp(ld - mr), q, k, log_d, m_row)
    inter = _each(lambda lint, mr: jnp.exp(lint - mr), log_inter, m_row)
    num = _each(lambda ss, vv, it, qq, cp: _nn(ss, vv) + it * _nn(qq, cp), s, v, inter, q, c_prev)
    den = _each(lambda ss, it, qq, npv: jnp.sum(ss, axis=-1, keepdims=True)
                + it * jnp.sum(qq * npv, axis=-1, keepdims=True), s, inter, q, n_prev)
    out = _each(lambda nu, de, mr: nu / jnp.maximum(jnp.abs(de), jnp.exp(-mr)), num, den, m_row)
    m_new = _each(lambda ft, mp, ml: jnp.maximum(ft + mp, ml), ftot, m_prev, m_loc)
    sp = _each(lambda ft, mp, mn: jnp.exp(ft + mp - mn), ftot, m_prev, m_new)
    sc = _each(lambda ml, mn: jnp.exp(ml - mn), m_loc, m_new)
    for h in heads:
        c_ref[h] = sp[h] * c_prev[h] + sc[h] * c_loc[h]
        n_ref[h] = sp[h] * n_prev[h] + sc[h] * n_loc[h]
        m_ref[h] = m_new[h]
        if d == 0:
            h_ref[:, lanes[h]] = out[h]
        else:
            tot = h0_ref[:, lanes[h]] + out[h]
            cen = tot - jnp.mean(tot, axis=-1, keepdims=True)
            nrm = cen * lax.rsqrt(jnp.mean(cen * cen, axis=-1, keepdims=True) + MLSTM_EPS)
            h_ref[:, lanes[h]] = (jax.nn.sigmoid(o_ref[:, lanes[h]]) * nrm * par_ref[:, lanes[h]]).astype(h_ref.dtype)


def _mlstm_scan(q, k, gates, p, par, n_ctx):
    b, t, _ = q.shape
    L = MLSTM_CHUNK
    nb, ncb = t // L, n_ctx // L
    v_blk, o_blk = (EV_B0 + 2 * B_QK) // B_INNER, (EV_B0 + 2 * B_QK + B_INNER) // B_INNER
    gates_r = jnp.swapaxes(gates, 2, 3)
    h0 = None
    for d in (0, 1):
        tix = lambda j, d=d: _time_index(d, j, ncb, nb)
        seq = lambda c, blk=0: pl.BlockSpec((None, L, c), lambda i, j: (i, tix(j), blk))
        in_specs = [seq(B_QK), seq(B_QK), seq(B_INNER, v_blk),
                    pl.BlockSpec((None, None, L, 2 * B_HEADS), lambda i, j, d=d: (d, i, tix(j), 0)),
                    pl.BlockSpec((None, None, 2 * B_HEADS, L), lambda i, j, d=d: (d, i, 0, tix(j)))]
        args = [q, k, p, gates, gates_r]
        if d == 1:
            in_specs += [seq(B_INNER), seq(B_INNER, o_blk), pl.BlockSpec(par.shape, lambda i, j: (0, 0))]
            args += [h0, p, par]
        h0 = pl.pallas_call(
            functools.partial(_mlstm_kernel, L=L, d=d),
            grid=(b, nb),
            in_specs=in_specs,
            out_specs=seq(B_INNER),
            out_shape=jax.ShapeDtypeStruct((b, t, B_INNER), BF16 if d else F32),
            scratch_shapes=[pltpu.VMEM((B_HEADS, 2 * B_QK_DIM, B_V_DIM), F32),
                            pltpu.VMEM((B_HEADS, 1, 2 * B_QK_DIM), F32),
                            pltpu.VMEM((B_HEADS, 1, 1), F32)],
            compiler_params=_scan_params(2),
            name="mlstm_scan",
        )(*args)
    return h0


def _gla_kernel(*refs, L, sub, nck, d):
    if d == 0:
        q_ref, f_ref, v_ref, par_ref, y_ref, st_ref = refs
    else:
        q_ref, f_ref, v_ref, par_ref, y0_ref, g_ref, y_ref, st_ref = refs
    hd = C_HEAD_DIM
    nhead = st_ref.shape[0]

    @pl.when(pl.program_id(1) == 0)
    def _():
        st_ref[...] = jnp.zeros_like(st_ref)

    before = _before(L, d)
    rows = lax.broadcasted_iota(jnp.int32, (L, 1), 0)
    slices = _chunk_slices(nck, L, d)
    where = [(sl, slice(h * hd, (h + 1) * hd)) for h in range(nhead) for sl in slices]
    lb = [par_ref[0:1, lanes] for _, lanes in where]
    f_pre = [f_ref[sl, lanes] + par_ref[1:2, lanes] for sl, lanes in where]
    q = [_silu(q_ref[sl, lanes]) for sl, lanes in where]
    v = [v_ref[sl, lanes] for sl, lanes in where]
    lf = _each(lambda b, x: jnp.log(b + (1.0 - b) * jax.nn.sigmoid(x)), lb, f_pre)
    k = _each(lambda b, x: (1.0 - b) * jax.nn.sigmoid(-x), lb, f_pre)
    lam = _each(lambda x: _mask_nn(before, x), lf)
    lam_end = _each(lambda x: _end_row(x, d), lam)
    blocks = [[] for _ in where]
    for c in range(L // sub):
        lo, hi = c * sub, (c + 1) * sub
        upto = rows >= lo if d == 1 else rows < hi
        for s in range(len(where)):
            edge = (hi, hi + 1) if d == 1 else (lo - 1, lo)
            ref = lam[s][edge[0]:edge[1], :] if 0 <= edge[0] < L else jnp.zeros_like(lam_end[s])
            qc = q[s][lo:hi, :] * jnp.exp(lam[s][lo:hi, :] - ref)
            kc = k[s] * jnp.exp(jnp.where(upto, ref - lam[s], NEG_BIG))
            blocks[s].append(_nt(qc, kc))
    att = _each(lambda bl: jnp.where(before, jnp.concatenate(bl, axis=0), 0.0), blocks)
    y_intra = _each(_nn, att, v)
    q_in = _each(lambda x, l: x * jnp.exp(l), q, lam)
    kv = _each(lambda x, y, l, le: _tn(x, y * jnp.exp(le - l)), v, k, lam, lam_end)
    dec = _each(jnp.exp, lam_end)
    for h in range(nhead):
        st = st_ref[h]
        for i in range(nck):
            s = h * nck + i
            sl, lanes = where[s]
            y = y_intra[s] + _nt(q_in[s], st)
            st = st * dec[s] + kv[s]
            if d == 0:
                y_ref[sl, lanes] = y
            else:
                tot = y0_ref[sl, lanes] + y
                nrm = tot * lax.rsqrt(jnp.mean(tot * tot, axis=-1, keepdims=True) + 1e-6)
                y_ref[sl, lanes] = (nrm * par_ref[2:3, lanes] * _silu(g_ref[sl, lanes])).astype(y_ref.dtype)
        st_ref[h] = st


def _gla_scan(p, lb, f_bias, norm_w, n_ctx):
    b, t, _ = p.shape
    c = C_INNER
    L = GLA_BLOCK
    tb = SCAN_TIME_BLOCK
    nck, nb, ncb = tb // L, t // tb, n_ctx // tb
    y0 = None
    for d in (0, 1):
        tix = lambda j, d=d: _time_index(d, j, ncb, nb)
        seq = lambda blk=0: pl.BlockSpec((None, tb, c), lambda i, j: (i, tix(j), blk))
        par = jnp.stack([jnp.broadcast_to(lb, (c,)), f_bias[d], norm_w])
        in_specs = [seq(0), seq(1 + d), seq(3), pl.BlockSpec(par.shape, lambda i, j: (0, 0))]
        args = [p, p, p, par]
        if d == 1:
            in_specs += [seq(), seq(4)]
            args += [y0, p]
        y0 = pl.pallas_call(
            functools.partial(_gla_kernel, L=L, sub=GLA_CHUNK, nck=nck, d=d),
            grid=(b, nb),
            in_specs=in_specs,
            out_specs=seq(),
            out_shape=jax.ShapeDtypeStruct((b, t, c), BF16 if d else F32),
            scratch_shapes=[pltpu.VMEM((c // C_HEAD_DIM, C_HEAD_DIM, C_HEAD_DIM), F32)],
            compiler_params=_scan_params(2),
            name="gla_scan",
        )(*args)
    return y0


def _rwkv_kernel(*refs, L, nck, d):
    if d == 0:
        r_ref, k_ref, v_ref, a_ref, b_ref, lw_ref, y_ref, h_ref = refs
    else:
        r_ref, k_ref, v_ref, a_ref, b_ref, lw_ref, y0_ref, g_ref, par_ref, y_ref, h_ref = refs
    L2 = 2 * L
    W = 2 * D_HEAD_DIM
    sgn = 1 - 2 * d

    @pl.when(pl.program_id(1) == 0)
    def _():
        h_ref[...] = jnp.zeros_like(h_ref)

    before = _before(L, d)
    r2, c2 = _iota2(L2, L2)
    order2 = ((c2 & (L - 1)) - (r2 & (L - 1))) * sgn
    strict2 = order2 < 0
    incl2 = order2 <= 0
    eye2 = jnp.where(r2 == c2, 1.0, 0.0)
    rw, cw = _iota2(W, W)
    eye_w = rw == cw
    head0 = lax.broadcasted_iota(jnp.int32, (L, W), 1) < D_HEAD_DIM
    stack = lambda x: jnp.concatenate([jnp.where(head0, x, 0.0), jnp.where(head0, 0.0, x)], axis=0)
    n_levels = int(math.log2(L))

    npair = h_ref.shape[0]
    slices = _chunk_slices(nck, L, d)
    where = [(sl, slice(p * W, (p + 1) * W)) for p in range(npair) for sl in slices]
    r, k, v, a, b, lw = ([ref[sl, lanes] for sl, lanes in where]
                         for ref in (r_ref, k_ref, v_ref, a_ref, b_ref, lw_ref))
    cum = _each(lambda x: _mask_nn(before, x), lw)
    cum_end = _each(lambda c: _end_row(c, d), cum)
    e_neg = _each(lambda c: jnp.exp(-c), cum)
    e_end = _each(lambda ce, c: jnp.exp(ce - c), cum_end, cum)
    at = _each(lambda x, c, w: stack(x * jnp.exp(c - w)), a, cum, lw)
    rt = _each(lambda x, c: stack(x * jnp.exp(c)), r, cum)
    bt = _each(lambda x, e: stack(x * e), b, e_neg)
    kt = _each(lambda x, e: stack(x * e), k, e_neg)
    vs = _each(stack, v)
    gram = _each(lambda p, q, s, t: _nt(jnp.concatenate([p, q], axis=0), jnp.concatenate([s, t], axis=0)),
                 at, rt, bt, kt)
    nmat = _each(lambda g: jnp.where(strict2, g[:L2, :L2], 0.0), gram)
    a_k = _each(lambda g: jnp.where(strict2, g[:L2, L2:], 0.0), gram)
    r_bk = _each(lambda g: jnp.where(jnp.concatenate([incl2, incl2], axis=1), g[L2:, :], 0.0), gram)
    tinv = _each(lambda n: eye2 + n, nmat)
    pw = _each(lambda n: _nn(n, n), nmat)
    for lev in range(1, n_levels):
        if lev < n_levels - 1:
            both = _each(lambda p, t: _nn(p, jnp.concatenate([p, t], axis=1)), pw, tinv)
            pw = _each(lambda x: x[:, :L2], both)
            tinv = _each(lambda t, x: t + x[:, L2:], tinv, both)
        else:
            tinv = _each(lambda p, t: t + _nn(p, t), pw, tinv)
    akv = _each(_nn, a_k, vs)
    wu = _each(lambda t, p, q: _nn(t, jnp.concatenate([p, q], axis=1)), tinv, at, akv)
    zs = _each(lambda x, y: jnp.concatenate([x, jnp.concatenate([jnp.zeros_like(y), y], axis=1)], axis=0), wu, vs)
    qy = _each(_nn, r_bk, zs)
    md = _each(lambda x, y, e, z: _tn(jnp.concatenate([stack(x * e), stack(y * e)], axis=0), z),
               b, k, e_end, zs)
    dec = _each(lambda ce: jnp.sum(jnp.where(eye_w, jnp.broadcast_to(jnp.exp(ce), (W, W)), 0.0),
                                   axis=1, keepdims=True), cum_end)

    def head_mean(x):
        m0 = jnp.sum(jnp.where(head0, x, 0.0), axis=-1, keepdims=True)
        m1 = jnp.sum(jnp.where(head0, 0.0, x), axis=-1, keepdims=True)
        return jnp.where(head0, m0, m1) * (1.0 / D_HEAD_DIM)

    hs = [h_ref[p] for p in range(npair)]
    for i in range(nck):
        for p in range(npair):
            s = p * nck + i
            sl, lanes = where[s]
            ys = _nn(rt[s] + qy[s][:, :W], hs[p]) + qy[s][:, W:]
            y = ys[:L, :] + ys[L:, :]
            hs[p] = dec[s] * hs[p] + _nn(md[s][:, :W], hs[p]) + md[s][:, W:]
            if d == 0:
                y_ref[sl, lanes] = y
            else:
                par = par_ref[:, lanes]
                tot = y0_ref[sl, lanes] + y
                cen = tot - head_mean(tot)
                nrm = cen * lax.rsqrt(head_mean(cen * cen) + RWKV_EPS)
                bonus = head_mean(r[s] * k[s] * par[0:1, :]) * float(D_HEAD_DIM) * v[s]
                y_ref[sl, lanes] = ((nrm * par[1:2, :] + par[2:3, :] + bonus) * g_ref[sl, lanes]).astype(y_ref.dtype)
    for p in range(npair):
        h_ref[p] = hs[p]


def _rwkv_scan(r, k, v, a, b, lw, g, par, n_ctx):
    bsz, t, c = r.shape
    L = RWKV_CHUNK
    tb = SCAN_TIME_BLOCK
    nck, nb, ncb = tb // L, t // tb, n_ctx // tb
    y0 = None
    for d in (0, 1):
        tix = lambda j, d=d: _time_index(d, j, ncb, nb)
        seq = pl.BlockSpec((None, tb, c), lambda i, j: (i, tix(j), 0))
        in_specs = [seq] * 5 + [pl.BlockSpec((None, None, tb, c), lambda i, j, d=d: (d, i, tix(j), 0))]
        args = [r, k, v, a, b, lw]
        if d == 1:
            in_specs += [seq, seq, pl.BlockSpec(par.shape, lambda i, j: (0, 0))]
            args += [y0, g, par]
        y0 = pl.pallas_call(
            functools.partial(_rwkv_kernel, L=L, nck=nck, d=d),
            grid=(bsz, nb),
            in_specs=in_specs,
            out_specs=seq,
            out_shape=jax.ShapeDtypeStruct((bsz, t, c), BF16 if d else F32),
            scratch_shapes=[pltpu.VMEM((c // (2 * D_HEAD_DIM), 2 * D_HEAD_DIM, 2 * D_HEAD_DIM), F32)],
            compiler_params=_scan_params(2),
            name="rwkv7_scan",
        )(*args)
    return y0


def _softplus(x):
    return jnp.maximum(x, 0.0) + jnp.log(1.0 + jnp.exp(-jnp.abs(x)))


def _segment_neighbours(x, before_tile, after_tile):
    n = x.shape[0]
    row = lax.broadcasted_iota(jnp.int32, x.shape, 0)
    prev = jnp.where(row == 0, before_tile, pltpu.roll(x, 1, axis=0))
    nxt = jnp.where(row == n - 1, after_tile, pltpu.roll(x, n - 1, axis=0))
    return prev, nxt


def _even_prep_kernel(p_ref, prev_ref, next_ref, ca_ref, cb_ref, sm_ref,
                      xs_ref, bm_ref, cm_ref, dt_ref, la_ref, q_ref, k_ref, g_ref, *, tiles_per_seq, ctx_tiles):
    j = pl.program_id(0) % tiles_per_seq
    first = jnp.logical_or(j == 0, j == ctx_tiles)
    last = jnp.logical_or(j == ctx_tiles - 1, j == tiles_per_seq - 1)

    def conv_silu(lo, width, taps_ref):
        x = p_ref[:, lo:lo + width]
        prev, nxt = _segment_neighbours(x, jnp.where(first, 0.0, prev_ref[7:8, lo:lo + width]),
                                        jnp.where(last, 0.0, next_ref[0:1, lo:lo + width]))
        taps = taps_ref[...]
        return _silu(prev * taps[0:1, :] + x * taps[1:2, :] + nxt * taps[2:3, :] + taps[3:4, :])

    xbc = conv_silu(A_INNER, A_XBC, ca_ref)
    xs_ref[...] = xbc[:, :A_INNER]
    bm_ref[...] = xbc[:, A_INNER:A_INNER + A_GROUPS * A_STATE]
    cm_ref[...] = xbc[:, A_INNER + A_GROUPS * A_STATE:]
    qk = conv_silu(EV_B0, 2 * B_QK, cb_ref)
    q_ref[...] = qk[:, :B_QK]
    k_ref[...] = qk[:, B_QK:]
    sm = sm_ref[...]
    nh2 = 2 * A_HEADS
    dt = _softplus(p_ref[:, A_INNER + A_XBC:A_INNER + A_XBC + nh2] + sm[0:1, :])
    la = dt * sm[1:2, :]
    g0 = EV_B0 + 2 * B_QK + 2 * B_INNER
    gx = p_ref[:, g0:g0 + 4 * B_HEADS]
    g = jnp.where(sm[4:5, :] > 0.5, -_softplus(-(gx + sm[3:4, :])), gx + sm[2:3, :])
    for d in range(2):
        dt_ref[d] = dt[:, d * A_HEADS:(d + 1) * A_HEADS]
        la_ref[d] = la[:, d * A_HEADS:(d + 1) * A_HEADS]
        g_ref[d] = g[:, d * 2 * B_HEADS:(d + 1) * 2 * B_HEADS]


def _gate_lane_order(a):
    h = B_HEADS
    return jnp.concatenate([a[..., :-4 * h], a[..., -4 * h:-3 * h], a[..., -2 * h:-h], a[..., -3 * h:-2 * h], a[..., -h:]],
                           axis=-1)


def _even_mixers(p, n_ctx, ssd_params, mlstm_params):
    conv_w, conv_b, dt_bias, a_log, d_skip, norm_a = ssd_params
    conv_bw, conv_bb, i_bias, f_bias, norm_b = mlstm_params
    b, t, n_pad = p.shape
    p2 = p.reshape(b * t, n_pad)
    tiles = b * t // ROW_TILE
    halo = ROW_TILE // 8
    zeros = jnp.zeros((B_HEADS,), F32)
    sm = jnp.stack([dt_bias.reshape(-1), -jnp.exp(a_log).reshape(-1),
                    jnp.concatenate([i_bias[0], zeros, i_bias[1], zeros]),
                    jnp.concatenate([zeros, f_bias[0], zeros, f_bias[1]]),
                    jnp.concatenate([zeros, zeros + 1.0, zeros, zeros + 1.0])])
    ca = jnp.concatenate([conv_w, conv_b[None, :]], axis=0)
    cb = jnp.concatenate([conv_bw, conv_bb[None, :]], axis=0)
    whole = lambda a: pl.BlockSpec(a.shape, lambda i: (0,) * a.ndim)
    rows = lambda c: pl.BlockSpec((ROW_TILE, c), lambda i: (i, 0))
    per_dir = lambda c: pl.BlockSpec((2, ROW_TILE, c), lambda i: (0, i, 0))
    f32 = lambda *s: jax.ShapeDtypeStruct(s, F32)
    n = b * t
    gn = A_GROUPS * A_STATE
    xs, bm, cm, dt, la, q, k, g = pl.pallas_call(
        functools.partial(_even_prep_kernel, tiles_per_seq=t // ROW_TILE, ctx_tiles=n_ctx // ROW_TILE),
        grid=(tiles,),
        in_specs=[pl.BlockSpec((ROW_TILE, n_pad), lambda i: (i, 0)),
                  pl.BlockSpec((8, n_pad), lambda i: (jnp.maximum(i * halo - 1, 0), 0)),
                  pl.BlockSpec((8, n_pad), lambda i: (jnp.minimum((i + 1) * halo, tiles * halo - 1), 0)),
                  whole(ca), whole(cb), whole(sm)],
        out_specs=[rows(A_INNER), rows(gn), rows(gn), per_dir(A_HEADS), per_dir(A_HEADS),
                   rows(B_QK), rows(B_QK), per_dir(2 * B_HEADS)],
        out_shape=[f32(n, A_INNER), f32(n, gn), f32(n, gn), f32(2, n, A_HEADS), f32(2, n, A_HEADS),
                   f32(n, B_QK), f32(n, B_QK), f32(2, n, 2 * B_HEADS)],
        compiler_params=pltpu.CompilerParams(dimension_semantics=("arbitrary",), vmem_limit_bytes=VMEM_LIMIT_BYTES),
        name="even_prep",
    )(p2, p2, p2, ca, cb, sm)
    seq = lambda a: a.reshape(b, t, a.shape[-1])
    seq_d = lambda a: a.reshape(2, b, t, a.shape[-1])
    par = jnp.stack([jnp.repeat(d_skip, A_HEAD_DIM), norm_a])
    fa = _ssd_scan(seq(xs), seq_d(dt), seq_d(la), seq(bm), seq(cm), p, par, n_ctx)
    fb = _mlstm_scan(seq(q), seq(k), seq_d(g), p, norm_b[None, :], n_ctx)
    return fa, fb


def _hgrn2_mixer(p, n_ctx, lb, params):
    f_bias, norm_w = params
    return _gla_scan(p, lb, f_bias, norm_w, n_ctx)


def _rwkv_prep_kernel(p_ref, prev_ref, next_ref, mu_ref, w2_ref, a2_ref, g2_ref, vec_ref,
                      r_ref, k_ref, v_ref, a_ref, b_ref, g_ref, lw_ref, *, tiles_per_seq, ctx_tiles):
    c = D_INNER
    j = pl.program_id(0) % tiles_per_seq
    x = p_ref[:, P_C:]
    n = x.shape[0]
    first = jnp.logical_or(j == 0, j == ctx_tiles)
    last = jnp.logical_or(j == ctx_tiles - 1, j == tiles_per_seq - 1)
    prev, nxt = _segment_neighbours(x, jnp.where(first, 0.0, prev_ref[7:8, P_C:]),
                                    jnp.where(last, 0.0, next_ref[0:1, P_C:]))
    x = x + mu_ref[...] * (0.5 * (prev + nxt) - x)
    r, k, v = x[:, :c], x[:, c:2 * c], x[:, 2 * c:3 * c]
    o = 3 * c
    wl = jnp.tanh(x[:, o:o + 2 * D_W_LORA])
    gl = x[:, o + 2 * D_W_LORA:o + 2 * D_W_LORA + D_G_LORA]
    al = x[:, o + 2 * D_W_LORA + D_G_LORA:o + 2 * D_W_LORA + D_G_LORA + D_A_LORA]
    vec = vec_ref[...]
    for d in range(2):
        w = vec[d:d + 1, :] + _nn(wl, w2_ref[d])
        lw_ref[d] = -jnp.exp(-_softplus(-w) - 0.5)
    a = jax.nn.sigmoid(vec[2:3, :] + _nn(al, a2_ref[...]))
    g_ref[...] = _nn(jax.nn.sigmoid(gl), g2_ref[...])
    kx = k * vec[3:4, :]
    sq = kx * kx
    head0 = (lax.broadcasted_iota(jnp.int32, (n, 2 * D_HEAD_DIM), 1) < D_HEAD_DIM)
    sums = []
    for s in range(c // (2 * D_HEAD_DIM)):
        blk = sq[:, s * 2 * D_HEAD_DIM:(s + 1) * 2 * D_HEAD_DIM]
        s0 = jnp.sum(jnp.where(head0, blk, 0.0), axis=-1, keepdims=True)
        s1 = jnp.sum(jnp.where(head0, 0.0, blk), axis=-1, keepdims=True)
        sums.append(jnp.where(head0, s0, s1))
    kk = kx * lax.rsqrt(jnp.maximum(jnp.concatenate(sums, axis=1), 1e-12))
    r_ref[...] = r
    k_ref[...] = k * (1.0 + (a - 1.0) * vec[4:5, :])
    v_ref[...] = v
    a_ref[...] = -kk
    b_ref[...] = kk * a


def _rwkv7_mixer(p, n_ctx, params):
    mu, w0, w2, a0, a2, g2, k_k, k_a, r_k, ln_w, ln_b = params
    b, t, n_pad = p.shape
    c = D_INNER
    width = n_pad - P_C
    p2 = p.reshape(b * t, n_pad)
    tiles = b * t // ROW_TILE
    halo = ROW_TILE // 8
    w2z = jnp.zeros((2, 2 * D_W_LORA, c), F32)
    w2z = w2z.at[0, :D_W_LORA].set(w2[0]).at[1, D_W_LORA:].set(w2[1])
    vec = jnp.stack([w0[0], w0[1], a0, k_k, k_a])
    whole = lambda a: pl.BlockSpec(a.shape, lambda i: (0,) * a.ndim)
    rows = pl.BlockSpec((ROW_TILE, c), lambda i: (i, 0))
    mu_pad = jnp.pad(mu, (0, width - mu.shape[0]))[None, :]
    outs = pl.pallas_call(
        functools.partial(_rwkv_prep_kernel, tiles_per_seq=t // ROW_TILE, ctx_tiles=n_ctx // ROW_TILE),
        grid=(tiles,),
        in_specs=[pl.BlockSpec((ROW_TILE, n_pad), lambda i: (i, 0)),
                  pl.BlockSpec((8, n_pad), lambda i: (jnp.maximum(i * halo - 1, 0), 0)),
                  pl.BlockSpec((8, n_pad), lambda i: (jnp.minimum((i + 1) * halo, tiles * halo - 1), 0)),
                  whole(mu_pad), whole(w2z), whole(a2), whole(g2), whole(vec)],
        out_specs=[rows] * 6 + [pl.BlockSpec((2, ROW_TILE, c), lambda i: (0, i, 0))],
        out_shape=[jax.ShapeDtypeStruct((b * t, c), F32)] * 6 + [jax.ShapeDtypeStruct((2, b * t, c), F32)],
        compiler_params=pltpu.CompilerParams(dimension_semantics=("arbitrary",), vmem_limit_bytes=VMEM_LIMIT_BYTES),
        name="rwkv7_prep",
    )(p2, p2, p2, mu_pad, w2z, a2, g2, vec)
    r, k, v, a, bb, g = (u.reshape(b, t, c) for u in outs[:6])
    par = jnp.stack([r_k.reshape(D_INNER), ln_w, ln_b])
    return _rwkv_scan(r, k, v, a, bb, outs[6].reshape(2, b, t, c), g, par, n_ctx)


def _to_col_major(u, rows):
    b, s, d = u.shape
    return u.reshape(b, rows, GRID_W, d).transpose(0, 2, 1, 3).reshape(b, s, d)


def _from_col_major(u, rows):
    b, s, d = u.shape
    return u.reshape(b, GRID_W, rows, d).transpose(0, 2, 1, 3).reshape(b, s, d)


def _tile_specs(bsz, t, n_ctx):
    tiles_per_seq, ctx_tiles = t // ROW_TILE, n_ctx // ROW_TILE
    mod_row = lambda i: jnp.where(i % tiles_per_seq < ctx_tiles, bsz, i // tiles_per_seq)
    rows = lambda c: pl.BlockSpec((ROW_TILE, c), lambda i: (i, 0))
    whole = lambda a: pl.BlockSpec(a.shape, lambda i: (0,) * a.ndim)
    mod = pl.BlockSpec((None, 6, D_MODEL), lambda i: (mod_row(i), 0, 0))
    params = pltpu.CompilerParams(dimension_semantics=("arbitrary",), vmem_limit_bytes=VMEM_LIMIT_BYTES)
    return rows, whole, mod, params


def _in_proj_kernel(x_ref, mod_ref, w_ref, o_ref):
    m = mod_ref[...]
    o_ref[...] = _nn(x_ref[...] * (1.0 + m[1:2, :]) + m[0:1, :], w_ref[...])


def _in_proj(xa, mods, w, bsz, n_ctx):
    n = w.shape[1]
    n_pad = -(-n // LANES) * LANES
    wb = jnp.pad(w.astype(BF16), ((0, 0), (0, n_pad - n)))
    rows, whole, mod, params = _tile_specs(bsz, xa.shape[0] // bsz, n_ctx)
    return pl.pallas_call(
        _in_proj_kernel,
        grid=(xa.shape[0] // ROW_TILE,),
        in_specs=[rows(D_MODEL), mod, whole(wb)],
        out_specs=rows(n_pad),
        out_shape=jax.ShapeDtypeStruct((xa.shape[0], n_pad), F32),
        compiler_params=params,
        name="in_proj",
    )(xa, mods, wb)


def _norm_rows(z, ln):
    mu = jnp.mean(z, axis=-1, keepdims=True)
    zc = z - mu
    var = jnp.mean(zc * zc, axis=-1, keepdims=True)
    return zc * lax.rsqrt(var + LN_EPS) * ln[0:1, :] + ln[1:2, :]


def _out_proj_kernel(fa_ref, fb_ref, w_ref, x_ref, mod_ref, ln_ref, xo_ref, h_ref, hb_ref):
    ka = fa_ref.shape[1]
    m = mod_ref[...]
    y = _nn(fa_ref[...], w_ref[:ka, :]) + _nn(fb_ref[...], w_ref[ka:, :])
    xn = _norm_rows(DEEPNORM_ALPHA * x_ref[...] + m[2:3, :] * y, ln_ref[...])
    xo_ref[...] = xn
    h = xn * (1.0 + m[4:5, :]) + m[3:4, :]
    h_ref[...] = h
    hb_ref[...] = h.astype(BF16)


def _out_proj(fa, fb, w, xa, mods, ln, bsz, n_ctx):
    t_all, d = xa.shape
    rows, whole, mod, params = _tile_specs(bsz, t_all // bsz, n_ctx)
    wb = w.astype(BF16)
    return pl.pallas_call(
        _out_proj_kernel,
        grid=(t_all // ROW_TILE,),
        in_specs=[rows(fa.shape[1]), rows(fb.shape[1]), whole(wb), rows(d), mod, whole(ln)],
        out_specs=[rows(d), rows(d), rows(d)],
        out_shape=[jax.ShapeDtypeStruct((t_all, d), F32), jax.ShapeDtypeStruct((t_all, d), F32),
                   jax.ShapeDtypeStruct((t_all, d), BF16)],
        compiler_params=params,
        name="out_proj_norm",
    )(fa, fb, wb, xa, mods, ln)


def _ffn_norm_kernel(*refs):
    y_refs, (w_ref, x_ref, mod_ref, ln_ref, xo_ref) = refs[:TOP_K], refs[TOP_K:]
    m = mod_ref[...]
    w = w_ref[...]
    f = functools.reduce(jnp.add, [y_refs[kk][...].astype(F32) * w[:, kk:kk + 1] for kk in range(TOP_K)])
    xo_ref[...] = _norm_rows(DEEPNORM_ALPHA * x_ref[...] + m[5:6, :] * f, ln_ref[...])


def _ffn_norm(ys, wts, xa, mods, ln, bsz, n_ctx):
    t_all, d = xa.shape
    rows, whole, mod, params = _tile_specs(bsz, t_all // bsz, n_ctx)
    return pl.pallas_call(
        _ffn_norm_kernel,
        grid=(t_all // ROW_TILE,),
        in_specs=[rows(d)] * TOP_K + [rows(TOP_K), rows(d), mod, whole(ln)],
        out_specs=rows(d),
        out_shape=jax.ShapeDtypeStruct((t_all, d), F32),
        compiler_params=params,
        name="ffn_residual_norm",
    )(*ys, wts, xa, mods, ln)


def kernel(x, c, ctx, c_ctx, mod_w, mod_b, ln_g, ln_b, ev_w_in, ev_w_out, ssd_conv_w, ssd_conv_b, ssd_dt_bias, ssd_a_log, ssd_d, ssd_norm_w, mlstm_conv_w, mlstm_conv_b, mlstm_i_bias, mlstm_f_bias, mlstm_norm_w, od_w_in, od_w_out, hgrn_lb_logits, hgrn_f_bias, hgrn_norm_w, rwkv_mu, rwkv_w0, rwkv_w2, rwkv_a0, rwkv_a2, rwkv_g2, rwkv_k_k, rwkv_k_a, rwkv_r_k, rwkv_ln_w, rwkv_ln_b, router_w, router_bias, exp_w_gate, exp_w_up, exp_w_down):
    bsz, seq, _ = x.shape
    n_ctx = ctx.shape[1]
    rows = seq // GRID_W
    lb_all = jnp.cumsum(jax.nn.softmax(hgrn_lb_logits.astype(F32), axis=0), axis=0)
    lb_all = lb_all - lb_all[0]
    s_c = jax.nn.silu(c)
    s_cc = jax.nn.silu(c_ctx)
    t = n_ctx + seq
    xa = jnp.concatenate([ctx, x], axis=1).reshape(bsz * t, D_MODEL)
    seq3 = lambda a: a.reshape(bsz, t, a.shape[-1])
    flat = lambda a: a.reshape(bsz * t, a.shape[-1])
    lat_order = lambda a, f: flat(jnp.concatenate([seq3(a)[:, :n_ctx], f(seq3(a)[:, n_ctx:], rows)], axis=1))
    for layer in range(DEPTH):
        i = layer // 2
        mods = _matmul(jnp.concatenate([s_c, s_cc[None]], axis=0), mod_w[layer], tm=8, tn=512) + mod_b[layer]
        mods = mods.reshape(bsz + 1, 6, D_MODEL)
        ln = jnp.stack([ln_g[layer], ln_b[layer]], axis=1)
        if layer % 2 == 0:
            w_in = jnp.concatenate([ev_w_in[i][:, :P_A], jnp.zeros((D_MODEL, EV_B0 - P_A), F32),
                                    _gate_lane_order(ev_w_in[i][:, P_A:])], axis=1)
            p = seq3(_in_proj(xa, mods, w_in, bsz, n_ctx))
            fa, fb = _even_mixers(
                p, n_ctx,
                (ssd_conv_w[i], ssd_conv_b[i], ssd_dt_bias[i], ssd_a_log[i], ssd_d[i], ssd_norm_w[i]),
                (mlstm_conv_w[i], mlstm_conv_b[i], mlstm_i_bias[i], mlstm_f_bias[i], mlstm_norm_w[i]))
            fa, fb, w_out = flat(fa), flat(fb), ev_w_out[i]
        else:
            g0, g1 = 3 * D_INNER + 2 * D_W_LORA + D_A_LORA, P_D
            lora_last = lambda a: jnp.concatenate([a[..., :g0 - D_A_LORA], a[..., g0:g1], a[..., g0 - D_A_LORA:g0]], axis=-1)
            w_in = jnp.concatenate([od_w_in[i][:, :P_C], lora_last(od_w_in[i][:, P_C:])], axis=1)
            p = seq3(_in_proj(lat_order(xa, _to_col_major), mods, w_in, bsz, n_ctx))
            fa = _hgrn2_mixer(p, n_ctx, lb_all[layer], (hgrn_f_bias[i], hgrn_norm_w[i]))
            fb = _rwkv7_mixer(p, n_ctx,
                              (lora_last(rwkv_mu[i]), rwkv_w0[i], rwkv_w2[i], rwkv_a0[i], rwkv_a2[i], rwkv_g2[i],
                               rwkv_k_k[i], rwkv_k_a[i], rwkv_r_k[i], rwkv_ln_w[i], rwkv_ln_b[i]))
            fa, fb, w_out = lat_order(fa, _from_col_major), lat_order(fb, _from_col_major), od_w_out[i]
        xa, h, hb = _out_proj(fa, fb, w_out, xa, mods, ln[0], bsz, n_ctx)
        ys, wts = _moe_ffn(h, hb, router_w, router_bias, exp_w_gate, exp_w_up, exp_w_down, layer)
        xa = _ffn_norm(ys, wts, xa, mods, ln[1], bsz, n_ctx)
    return seq3(xa)[:, n_ctx:]
```

```python
import functools
import math

import jax
import jax.numpy as jnp
from jax import lax
from jax.experimental import pallas as pl
from jax.experimental.pallas import tpu as pltpu

F32 = jnp.float32
BF16 = jnp.bfloat16

D_MODEL = 1024
DEPTH = 4
GRID_W = 64
A_HEADS = 8
A_HEAD_DIM = 64
A_INNER = A_HEADS * A_HEAD_DIM
A_GROUPS = 2
A_STATE = 64
A_XBC = A_INNER + 2 * A_GROUPS * A_STATE
B_HEADS = 4
B_QK_DIM = 64
B_V_DIM = 128
B_QK = B_HEADS * B_QK_DIM
B_INNER = B_HEADS * B_V_DIM
MLSTM_EPS = 1e-6
C_HEADS = 4
C_HEAD_DIM = 128
C_INNER = C_HEADS * C_HEAD_DIM
D_HEADS = 8
D_HEAD_DIM = 64
D_INNER = D_HEADS * D_HEAD_DIM
D_W_LORA = 64
D_A_LORA = 64
D_G_LORA = 128
RWKV_EPS = 64e-5
P_A = A_INNER + A_XBC + 2 * A_HEADS
P_B = 2 * B_QK + 2 * B_INNER + 4 * B_HEADS
P_C = 5 * C_INNER
P_D = 3 * D_INNER + 2 * D_W_LORA + D_A_LORA + D_G_LORA
EV_B0 = 3 * A_INNER
N_EXPERTS = 32
N_EXPERT_GROUPS = 8
EXPERTS_PER_GROUP = N_EXPERTS // N_EXPERT_GROUPS
TOP_K = 2
D_EXPERT = 512
MOE_BLOCK = 512
ROW_TILE = 256
LANES = 128
DEEPNORM_ALPHA = (2 * DEPTH) ** 0.25
LN_EPS = 1e-5
M_INIT = -1e30
NEG_BIG = -1e30

SSD_CHUNK = 128
MLSTM_CHUNK = 128
GLA_CHUNK = 16
GLA_BLOCK = 64
RWKV_CHUNK = 64
SCAN_TIME_BLOCK = 256

VMEM_LIMIT_BYTES = 48 * 1024 * 1024
HI = lax.Precision.HIGHEST


def _dot(a, b, dims, exact):
    if exact:
        return lax.dot_general(a.astype(F32), b.astype(F32), (dims, ((), ())),
                               precision=HI, preferred_element_type=F32)
    return lax.dot_general(a.astype(BF16), b.astype(BF16), (dims, ((), ())),
                           preferred_element_type=F32)


def _nn(a, b, exact=False):
    return _dot(a, b, ((1,), (0,)), exact)


def _nt(a, b, exact=False):
    return _dot(a, b, ((1,), (1,)), exact)


def _tn(a, b, exact=False):
    return _dot(a, b, ((0,), (0,)), exact)


def _iota2(n, m):
    return (lax.broadcasted_iota(jnp.int32, (n, m), 0),
            lax.broadcasted_iota(jnp.int32, (n, m), 1))


def _split3(x):
    x1 = x.astype(BF16)
    r1 = x - x1.astype(F32)
    x2 = r1.astype(BF16)
    x3 = (r1 - x2.astype(F32)).astype(BF16)
    return x1, x2, x3


def _mask_nn(mask, x):
    mb = mask.astype(BF16)
    x1, x2, x3 = _split3(x)
    return _nn(mb, x1) + _nn(mb, x2) + _nn(mb, x3)


def _nn_mask(x, mask):
    mb = mask.astype(BF16)
    x1, x2, x3 = _split3(x)
    return _nn(x1, mb) + _nn(x2, mb) + _nn(x3, mb)


def _each(f, *cols):
    return [f(*xs) for xs in zip(*cols)]


def _silu(x):
    return x * jax.nn.sigmoid(x)


def _mm_kernel(x_ref, w_ref, o_ref, *, exact):
    o_ref[...] = _nn(x_ref[...], w_ref[...], exact)


def _matmul(x, w, tm=512, tn=512, exact=False):
    m, k = x.shape
    n = w.shape[1]
    n_pad = -(-n // tn) * tn
    m_pad = -(-m // tm) * tm
    xb = x if exact else x.astype(BF16)
    wb = w if exact else w.astype(BF16)
    if n_pad != n:
        wb = jnp.pad(wb, ((0, 0), (0, n_pad - n)))
    if m_pad != m:
        xb = jnp.pad(xb, ((0, m_pad - m), (0, 0)))
    out = pl.pallas_call(
        functools.partial(_mm_kernel, exact=exact),
        grid=(n_pad // tn, m_pad // tm),
        in_specs=[pl.BlockSpec((tm, k), lambda j, i: (i, 0)),
                  pl.BlockSpec((k, tn), lambda j, i: (0, j))],
        out_specs=pl.BlockSpec((tm, tn), lambda j, i: (i, j)),
        out_shape=jax.ShapeDtypeStruct((m_pad, n_pad), F32),
        compiler_params=pltpu.CompilerParams(
            dimension_semantics=("arbitrary", "arbitrary"),
            vmem_limit_bytes=VMEM_LIMIT_BYTES),
        name="dense_matmul",
    )(xb, wb)
    return out[:m, :n]


def _moe_kernel(blk_e_ref, n_used_ref, x_ref, wg_ref, wu_ref, wd_ref, o_ref):
    i = pl.program_id(0)

    @pl.when(i < n_used_ref[0])
    def _():
        x = x_ref[...]
        g = _nn(x, wg_ref[...])
        u = _nn(x, wu_ref[...])
        o_ref[...] = _nn(g * jax.nn.sigmoid(g) * u, wd_ref[...]).astype(o_ref.dtype)

    @pl.when(i >= n_used_ref[0])
    def _():
        o_ref[...] = jnp.zeros_like(o_ref)


def _moe_experts(xp, blk_e, n_used, w_gate, w_up, w_down, layer):
    n_rows, d = xp.shape
    n_blocks = n_rows // MOE_BLOCK
    grid_spec = pltpu.PrefetchScalarGridSpec(
        num_scalar_prefetch=2,
        grid=(n_blocks,),
        in_specs=[
            pl.BlockSpec((MOE_BLOCK, d), lambda i, be, nu: (i, 0)),
            pl.BlockSpec((None, None, d, D_EXPERT), lambda i, be, nu: (layer, be[i], 0, 0)),
            pl.BlockSpec((None, None, d, D_EXPERT), lambda i, be, nu: (layer, be[i], 0, 0)),
            pl.BlockSpec((None, None, D_EXPERT, d), lambda i, be, nu: (layer, be[i], 0, 0)),
        ],
        out_specs=pl.BlockSpec((MOE_BLOCK, d), lambda i, be, nu: (i, 0)),
    )
    return pl.pallas_call(
        _moe_kernel,
        grid_spec=grid_spec,
        out_shape=jax.ShapeDtypeStruct((n_rows, d), BF16),
        compiler_params=pltpu.CompilerParams(
            dimension_semantics=("arbitrary",),
            vmem_limit_bytes=VMEM_LIMIT_BYTES),
        name="moe_experts",
    )(blk_e, n_used, xp, w_gate, w_up, w_down)


def _top2(vals):
    m = len(vals)
    m1 = functools.reduce(jnp.maximum, vals)
    i1 = jnp.full_like(m1, float(m - 1))
    for j in reversed(range(m - 1)):
        i1 = jnp.where(vals[j] == m1, float(j), i1)
    rest = [jnp.where(i1 == float(j), -jnp.inf, vals[j]) for j in range(m)]
    m2 = functools.reduce(jnp.maximum, rest)
    i2 = jnp.full_like(m1, float(m - 1))
    for j in reversed(range(m - 1)):
        i2 = jnp.where(rest[j] == m2, float(j), i2)
    return m1, i1, m2, i2


def _route_tile(h, wt_ref, bias_ref, e_ref, w_ref, rank_ref, cnt_ref, carry_ref):
    tm = h.shape[0]
    ng, per = N_EXPERT_GROUPS, EXPERTS_PER_GROUP

    @pl.when(pl.program_id(0) == 0)
    def _():
        carry_ref[...] = jnp.zeros_like(carry_ref)

    s = jax.nn.sigmoid(_nt(wt_ref[...], h, True))
    sb = s + bias_ref[...]
    biased = [sb[j * ng:(j + 1) * ng, :] for j in range(per)]
    plain = [s[j * ng:(j + 1) * ng, :] for j in range(per)]
    m1, _, m2, _ = _top2(biased)
    gsum = m1 + m2
    rows = lax.broadcasted_iota(jnp.int32, (ng, tm), 0).astype(F32)
    gmax = jnp.max(gsum, axis=0, keepdims=True)
    gi = jnp.min(jnp.where(gsum == gmax, rows, float(ng)), axis=0, keepdims=True)
    sel = rows == gi
    pick = lambda v: jnp.sum(jnp.where(sel, v, 0.0), axis=0, keepdims=True)
    in_b = [pick(v) for v in biased]
    in_s = [pick(v) for v in plain]
    _, l1, _, l2 = _top2(in_b)
    w1 = functools.reduce(jnp.add, [jnp.where(l1 == float(j), in_s[j], 0.0) for j in range(per)])
    w2 = functools.reduce(jnp.add, [jnp.where(l2 == float(j), in_s[j], 0.0) for j in range(per)])
    e1 = gi * float(per) + l1
    e2 = gi * float(per) + l2
    wsum = w1 + w2
    e_ref[0:1, :] = e1.astype(jnp.int32)
    e_ref[1:2, :] = e2.astype(jnp.int32)
    w_ref[0:1, :] = w1 / wsum
    w_ref[1:2, :] = w2 / wsum
    row = lax.broadcasted_iota(jnp.int32, (N_EXPERTS, tm), 0)
    experts = ((row % ng) * per + row // ng).astype(F32)
    oh1 = jnp.where(experts == e1, 1.0, 0.0)
    oh2 = jnp.where(experts == e2, 1.0, 0.0)
    oh = oh1 + oh2
    ri, ci = _iota2(tm, tm)
    seen = _nn(oh, jnp.where(ri < ci, 1.0, 0.0)) + carry_ref[...]
    rank_ref[0:1, :] = jnp.sum(oh1 * seen, axis=0, keepdims=True).astype(jnp.int32)
    rank_ref[1:2, :] = jnp.sum(oh2 * seen, axis=0, keepdims=True).astype(jnp.int32)
    carry = carry_ref[...] + jnp.sum(oh, axis=1, keepdims=True)
    carry_ref[...] = carry
    cnt_ref[...] = carry.astype(jnp.int32)


def _member_major(a):
    return a.reshape(N_EXPERT_GROUPS, EXPERTS_PER_GROUP, -1).transpose(1, 0, 2).reshape(N_EXPERTS, -1)


def _moe_ffn(hb, routing, w_gate, w_up, w_down, layer):
    t, d = hb.shape
    expert, wts, rank, counts = routing
    counts = counts.reshape(EXPERTS_PER_GROUP, N_EXPERT_GROUPS).T.reshape(N_EXPERTS)
    n_assign = t * TOP_K
    padded = (counts + MOE_BLOCK - 1) // MOE_BLOCK * MOE_BLOCK
    pends = jnp.cumsum(padded)
    pstarts = pends - padded
    start_of = jnp.sum(jnp.where(expert[..., None] == jnp.arange(N_EXPERTS, dtype=jnp.int32), pstarts, 0), axis=-1)
    dest = start_of + rank
    n_blocks = -(-n_assign // MOE_BLOCK) + N_EXPERTS
    slot_token = jnp.arange(n_blocks * MOE_BLOCK, dtype=jnp.int32) % t
    slot_token = slot_token.at[dest.reshape(-1)].set(jnp.tile(jnp.arange(t, dtype=jnp.int32), TOP_K),
                                                      unique_indices=True)
    xp = hb[slot_token]
    blk_start = jnp.arange(n_blocks, dtype=jnp.int32) * MOE_BLOCK
    blk_e = jnp.minimum(jnp.sum(pends[None, :] <= blk_start[:, None], axis=1), N_EXPERTS - 1).astype(jnp.int32)
    n_used = (pends[-1] // MOE_BLOCK).astype(jnp.int32).reshape(1)
    yp = _moe_experts(xp, blk_e, n_used, w_gate, w_up, w_down, layer)
    return [yp[dest[kk]] for kk in range(TOP_K)], wts.T


def _time_index(d, j, n_ctx_blocks, n_blocks):
    if d == 0:
        return j
    return jnp.where(j < n_ctx_blocks, n_ctx_blocks - 1 - j, n_blocks - 1 - j + n_ctx_blocks)


def _end_row(x, d):
    n = x.shape[0]
    return x[0:1, :] if d == 1 else x[n - 1:n, :]


def _before(n, d):
    ri, ci = _iota2(n, n)
    return ci >= ri if d == 1 else ci <= ri


def _chunk_slices(nck, L, d):
    order = range(nck - 1, -1, -1) if d == 1 else range(nck)
    return [slice(i * L, (i + 1) * L) for i in order]


def _scan_params(n_axes):
    return pltpu.CompilerParams(dimension_semantics=("arbitrary",) * n_axes, vmem_limit_bytes=VMEM_LIMIT_BYTES)


def _ssd_kernel(*refs, L, d):
    if d == 0:
        x_ref, dt_ref, lac_ref, lar_ref, b_ref, c_ref, y_ref, st_ref = refs
    else:
        x_ref, dt_ref, lac_ref, lar_ref, b_ref, c_ref, y0_ref, z_ref, par_ref, y_ref, st_ref = refs
    hp, gn = A_INNER, A_GROUPS * A_STATE
    hpg = A_HEADS // A_GROUPS

    @pl.when(pl.program_id(1) == 0)
    def _():
        st_ref[...] = jnp.zeros_like(st_ref)

    before = _before(L, d)
    xs = x_ref[...]
    bm = b_ref[...]
    cm = c_ref[...]
    ccol = _mask_nn(before, lac_ref[...])
    crow = _nn_mask(lar_ref[...], _before(L, 1 - d))
    head_of_lane = lax.broadcasted_iota(jnp.int32, (A_HEADS, hp), 1) // A_HEAD_DIM
    expand = head_of_lane == lax.broadcasted_iota(jnp.int32, (A_HEADS, hp), 0)
    cum = _nn_mask(ccol, expand)
    x = xs * _nn_mask(dt_ref[...], expand)
    end = _end_row(cum, d)
    group_of_lane = lax.broadcasted_iota(jnp.int32, (L, gn), 1) // A_STATE
    first_of_pair = (lax.broadcasted_iota(jnp.int32, (L, 2 * A_HEAD_DIM), 1) < A_HEAD_DIM)
    cbs = [_nt(jnp.where(group_of_lane == g, cm, 0.0), bm) for g in range(A_GROUPS)]
    heads = list(range(A_HEADS))
    decay = [jnp.exp(jnp.where(before, ccol[:, h:h + 1] - crow[h:h + 1, :], NEG_BIG)) for h in heads]
    yh = [_nn(cbs[h // hpg] * decay[h], x[:, (h // 2) * 2 * A_HEAD_DIM:(h // 2 + 1) * 2 * A_HEAD_DIM]) for h in heads]
    pairs = [jnp.where(first_of_pair, yh[2 * p], yh[2 * p + 1]) for p in range(A_HEADS // 2)]
    st = st_ref[...]
    y = jnp.concatenate(pairs, axis=1) + jnp.exp(cum) * _nn(cm, st)
    own_group = (lax.broadcasted_iota(jnp.int32, (gn, hp), 0) // A_STATE
                 == lax.broadcasted_iota(jnp.int32, (gn, hp), 1) // (A_HEAD_DIM * hpg))
    st_ref[...] = jnp.exp(end) * st + jnp.where(own_group, _tn(bm, x * jnp.exp(end - cum)), 0.0)
    if d == 0:
        y_ref[...] = y
    else:
        par = par_ref[...]
        u = (y0_ref[...] + y + par[0:1, :] * xs) * _silu(z_ref[...])
        y_ref[...] = (u * lax.rsqrt(jnp.mean(u * u, axis=-1, keepdims=True) + 1e-6) * par[1:2, :]).astype(y_ref.dtype)


def _ssd_scan(xs, dt, la, bm, cm, p, par, n_ctx):
    b, t, hp = xs.shape
    L = SSD_CHUNK
    nb, ncb = t // L, n_ctx // L
    lar = jnp.swapaxes(la, 2, 3)
    y0 = None
    for d in (0, 1):
        tix = lambda j, d=d: _time_index(d, j, ncb, nb)
        seq = lambda c: pl.BlockSpec((None, L, c), lambda i, j: (i, tix(j), 0))
        in_specs = [seq(hp),
                    pl.BlockSpec((None, None, L, A_HEADS), lambda i, j, d=d: (d, i, tix(j), 0)),
                    pl.BlockSpec((None, None, L, A_HEADS), lambda i, j, d=d: (d, i, tix(j), 0)),
                    pl.BlockSpec((None, None, A_HEADS, L), lambda i, j, d=d: (d, i, 0, tix(j))),
                    seq(bm.shape[-1]), seq(bm.shape[-1])]
        args = [xs, dt, la, lar, bm, cm]
        if d == 1:
            in_specs += [seq(hp), seq(hp), pl.BlockSpec(par.shape, lambda i, j: (0, 0))]
            args += [y0, p, par]
        y0 = pl.pallas_call(
            functools.partial(_ssd_kernel, L=L, d=d),
            grid=(b, nb),
            in_specs=in_specs,
            out_specs=seq(hp),
            out_shape=jax.ShapeDtypeStruct((b, t, hp), BF16 if d else F32),
            scratch_shapes=[pltpu.VMEM((bm.shape[-1], hp), F32)],
            compiler_params=_scan_params(2),
            name="ssd_scan",
        )(*args)
    return y0


def _mlstm_kernel(*refs, L, d):
    if d == 0:
        q_ref, k_ref, v_ref, gc_ref, gr_ref, h_ref, c_ref, n_ref, m_ref = refs
    else:
        q_ref, k_ref, v_ref, gc_ref, gr_ref, h0_ref, o_ref, par_ref, h_ref, c_ref, n_ref, m_ref = refs
    nh, dk, dv = B_HEADS, B_QK_DIM, B_V_DIM

    @pl.when(pl.program_id(1) == 0)
    def _():
        c_ref[...] = jnp.zeros_like(c_ref)
        n_ref[...] = jnp.zeros_like(n_ref)
        m_ref[...] = jnp.full_like(m_ref, M_INIT)

    before = _before(L, d)
    gc = gc_ref[...]
    gr = gr_ref[...]
    fcol = _mask_nn(before, gc[:, nh:])
    frow = _nn_mask(gr[nh:, :], _before(L, 1 - d))
    lane_head = lax.broadcasted_iota(jnp.int32, (L, 2 * dk), 1) // dk
    heads = list(range(nh))
    slab = [slice((h // 2) * 2 * dk, (h // 2 + 1) * 2 * dk) for h in heads]
    lanes = [slice(h * dv, (h + 1) * dv) for h in heads]
    q = [jnp.where(lane_head == h % 2, q_ref[:, slab[h]], 0.0) * (dk ** -0.5) for h in heads]
    k = [k_ref[:, slab[h]] for h in heads]
    v = [v_ref[:, lanes[h]] for h in heads]
    li_c = [gc[:, h:h + 1] for h in heads]
    li_r = [gr[h:h + 1, :] for h in heads]
    f_c = [fcol[:, h:h + 1] for h in heads]
    f_r = [frow[h:h + 1, :] for h in heads]
    ftot = _each(lambda x: _end_row(x, d), f_c)
    c_prev = [c_ref[h] for h in heads]
    n_prev = [n_ref[h] for h in heads]
    m_prev = [m_ref[h] for h in heads]
    w_end = _each(lambda ft, fc, lc: ft - fc + lc, ftot, f_c, li_c)
    m_loc = _each(lambda w: jnp.max(w, axis=0, keepdims=True), w_end)
    ke = _each(lambda x, w, m: x * jnp.exp(w - m), k, w_end, m_loc)
    c_loc = _each(_tn, ke, v)
    n_loc = _each(lambda x: jnp.sum(x, axis=0, keepdims=True), ke)
    log_d = _each(lambda fc, fr, lr: jnp.where(before, fc - fr + lr, NEG_BIG), f_c, f_r, li_r)
    log_inter = _each(jnp.add, f_c, m_prev)
    m_row = _each(lambda ld, lint: jnp.maximum(jnp.max(ld, axis=-1, keepdims=True), lint), log_d, log_inter)
    s = _each(lambda a, b, ld, mr: _nt(a, b) * jnp.exp(ld - mr), q, k, log_d, m_row)
    inter = _each(lambda lint, mr: jnp.exp(lint - mr), log_inter, m_row)
    num = _each(lambda ss, vv, it, qq, cp: _nn(ss, vv) + it * _nn(qq, cp), s, v, inter, q, c_prev)
    den = _each(lambda ss, it, qq, npv: jnp.sum(ss, axis=-1, keepdims=True)
                + it * jnp.sum(qq * npv, axis=-1, keepdims=True), s, inter, q, n_prev)
    out = _each(lambda nu, de, mr: nu / jnp.maximum(jnp.abs(de), jnp.exp(-mr)), num, den, m_row)
    m_new = _each(lambda ft, mp, ml: jnp.maximum(ft + mp, ml), ftot, m_prev, m_loc)
    sp = _each(lambda ft, mp, mn: jnp.exp(ft + mp - mn), ftot, m_prev, m_new)
    sc = _each(lambda ml, mn: jnp.exp(ml - mn), m_loc, m_new)
    for h in heads:
        c_ref[h] = sp[h] * c_prev[h] + sc[h] * c_loc[h]
        n_ref[h] = sp[h] * n_prev[h] + sc[h] * n_loc[h]
        m_ref[h] = m_new[h]
        if d == 0:
            h_ref[:, lanes[h]] = out[h]
        else:
            tot = h0_ref[:, lanes[h]] + out[h]
            cen = tot - jnp.mean(tot, axis=-1, keepdims=True)
            nrm = cen * lax.rsqrt(jnp.mean(cen * cen, axis=-1, keepdims=True) + MLSTM_EPS)
            h_ref[:, lanes[h]] = (jax.nn.sigmoid(o_ref[:, lanes[h]]) * nrm * par_ref[:, lanes[h]]).astype(h_ref.dtype)


def _mlstm_scan(q, k, gates, p, par, n_ctx):
    b, t, _ = q.shape
    L = MLSTM_CHUNK
    nb, ncb = t // L, n_ctx // L
    v_blk, o_blk = (EV_B0 + 2 * B_QK) // B_INNER, (EV_B0 + 2 * B_QK + B_INNER) // B_INNER
    gates_r = jnp.swapaxes(gates, 2, 3)
    h0 = None
    for d in (0, 1):
        tix = lambda j, d=d: _time_index(d, j, ncb, nb)
        seq = lambda c, blk=0: pl.BlockSpec((None, L, c), lambda i, j: (i, tix(j), blk))
        in_specs = [seq(B_QK), seq(B_QK), seq(B_INNER, v_blk),
                    pl.BlockSpec((None, None, L, 2 * B_HEADS), lambda i, j, d=d: (d, i, tix(j), 0)),
                    pl.BlockSpec((None, None, 2 * B_HEADS, L), lambda i, j, d=d: (d, i, 0, tix(j)))]
        args = [q, k, p, gates, gates_r]
        if d == 1:
            in_specs += [seq(B_INNER), seq(B_INNER, o_blk), pl.BlockSpec(par.shape, lambda i, j: (0, 0))]
            args += [h0, p, par]
        h0 = pl.pallas_call(
            functools.partial(_mlstm_kernel, L=L, d=d),
            grid=(b, nb),
            in_specs=in_specs,
            out_specs=seq(B_INNER),
            out_shape=jax.ShapeDtypeStruct((b, t, B_INNER), BF16 if d else F32),
            scratch_shapes=[pltpu.VMEM((B_HEADS, 2 * B_QK_DIM, B_V_DIM), F32),
                            pltpu.VMEM((B_HEADS, 1, 2 * B_QK_DIM), F32),
                            pltpu.VMEM((B_HEADS, 1, 1), F32)],
            compiler_params=_scan_params(2),
            name="mlstm_scan",
        )(*args)
    return h0


def _gla_kernel(*refs, L, sub, nck, d):
    if d == 0:
        q_ref, f_ref, v_ref, par_ref, y_ref, st_ref = refs
    else:
        q_ref, f_ref, v_ref, par_ref, y0_ref, g_ref, y_ref, st_ref = refs
    hd = C_HEAD_DIM
    nhead = st_ref.shape[0]

    @pl.when(pl.program_id(1) == 0)
    def _():
        st_ref[...] = jnp.zeros_like(st_ref)

    before = _before(L, d)
    rows = lax.broadcasted_iota(jnp.int32, (L, 1), 0)
    slices = _chunk_slices(nck, L, d)
    where = [(sl, slice(h * hd, (h + 1) * hd)) for h in range(nhead) for sl in slices]
    lb = [par_ref[0:1, lanes] for _, lanes in where]
    f_pre = [f_ref[sl, lanes] + par_ref[1:2, lanes] for sl, lanes in where]
    q = [_silu(q_ref[sl, lanes]) for sl, lanes in where]
    v = [v_ref[sl, lanes] for sl, lanes in where]
    lf = _each(lambda b, x: jnp.log(b + (1.0 - b) * jax.nn.sigmoid(x)), lb, f_pre)
    k = _each(lambda b, x: (1.0 - b) * jax.nn.sigmoid(-x), lb, f_pre)
    lam = _each(lambda x: _mask_nn(before, x), lf)
    lam_end = _each(lambda x: _end_row(x, d), lam)
    blocks = [[] for _ in where]
    for c in range(L // sub):
        lo, hi = c * sub, (c + 1) * sub
        upto = rows >= lo if d == 1 else rows < hi
        for s in range(len(where)):
            edge = (hi, hi + 1) if d == 1 else (lo - 1, lo)
            ref = lam[s][edge[0]:edge[1], :] if 0 <= edge[0] < L else jnp.zeros_like(lam_end[s])
            qc = q[s][lo:hi, :] * jnp.exp(lam[s][lo:hi, :] - ref)
            kc = k[s] * jnp.exp(jnp.where(upto, ref - lam[s], NEG_BIG))
            blocks[s].append(_nt(qc, kc))
    att = _each(lambda bl: jnp.where(before, jnp.concatenate(bl, axis=0), 0.0), blocks)
    y_intra = _each(_nn, att, v)
    q_in = _each(lambda x, l: x * jnp.exp(l), q, lam)
    kv = _each(lambda x, y, l, le: _tn(x, y * jnp.exp(le - l)), v, k, lam, lam_end)
    dec = _each(jnp.exp, lam_end)
    for h in range(nhead):
        st = st_ref[h]
        for i in range(nck):
            s = h * nck + i
            sl, lanes = where[s]
            y = y_intra[s] + _nt(q_in[s], st)
            st = st * dec[s] + kv[s]
            if d == 0:
                y_ref[sl, lanes] = y
            else:
                tot = y0_ref[sl, lanes] + y
                nrm = tot * lax.rsqrt(jnp.mean(tot * tot, axis=-1, keepdims=True) + 1e-6)
                y_ref[sl, lanes] = (nrm * par_ref[2:3, lanes] * _silu(g_ref[sl, lanes])).astype(y_ref.dtype)
        st_ref[h] = st


def _gla_scan(p, lb, f_bias, norm_w, n_ctx):
    b, t, _ = p.shape
    c = C_INNER
    L = GLA_BLOCK
    tb = SCAN_TIME_BLOCK
    nck, nb, ncb = tb // L, t // tb, n_ctx // tb
    y0 = None
    for d in (0, 1):
        tix = lambda j, d=d: _time_index(d, j, ncb, nb)
        seq = lambda blk=0: pl.BlockSpec((None, tb, c), lambda i, j: (i, tix(j), blk))
        par = jnp.stack([jnp.broadcast_to(lb, (c,)), f_bias[d], norm_w])
        in_specs = [seq(0), seq(1 + d), seq(3), pl.BlockSpec(par.shape, lambda i, j: (0, 0))]
        args = [p, p, p, par]
        if d == 1:
            in_specs += [seq(), seq(4)]
            args += [y0, p]
        y0 = pl.pallas_call(
            functools.partial(_gla_kernel, L=L, sub=GLA_CHUNK, nck=nck, d=d),
            grid=(b, nb),
            in_specs=in_specs,
            out_specs=seq(),
            out_shape=jax.ShapeDtypeStruct((b, t, c), BF16 if d else F32),
            scratch_shapes=[pltpu.VMEM((c // C_HEAD_DIM, C_HEAD_DIM, C_HEAD_DIM), F32)],
            compiler_params=_scan_params(2),
            name="gla_scan",
        )(*args)
    return y0


def _rwkv_kernel(*refs, L, nck, d):
    if d == 0:
        r_ref, k_ref, v_ref, a_ref, b_ref, lw_ref, y_ref, h_ref = refs
    else:
        r_ref, k_ref, v_ref, a_ref, b_ref, lw_ref, y0_ref, g_ref, par_ref, y_ref, h_ref = refs
    L2 = 2 * L
    W = 2 * D_HEAD_DIM
    sgn = 1 - 2 * d

    @pl.when(pl.program_id(1) == 0)
    def _():
        h_ref[...] = jnp.zeros_like(h_ref)

    before = _before(L, d)
    r2, c2 = _iota2(L2, L2)
    order2 = ((c2 & (L - 1)) - (r2 & (L - 1))) * sgn
    strict2 = order2 < 0
    incl2 = order2 <= 0
    eye2 = jnp.where(r2 == c2, 1.0, 0.0)
    rw, cw = _iota2(W, W)
    eye_w = rw == cw
    head0 = lax.broadcasted_iota(jnp.int32, (L, W), 1) < D_HEAD_DIM
    stack = lambda x: jnp.concatenate([jnp.where(head0, x, 0.0), jnp.where(head0, 0.0, x)], axis=0)
    n_levels = int(math.log2(L))

    npair = h_ref.shape[0]
    slices = _chunk_slices(nck, L, d)
    where = [(sl, slice(p * W, (p + 1) * W)) for p in range(npair) for sl in slices]
    r, k, v, a, b, lw = ([ref[sl, lanes] for sl, lanes in where]
                         for ref in (r_ref, k_ref, v_ref, a_ref, b_ref, lw_ref))
    cum = _each(lambda x: _mask_nn(before, x), lw)
    cum_end = _each(lambda c: _end_row(c, d), cum)
    e_neg = _each(lambda c: jnp.exp(-c), cum)
    e_end = _each(lambda ce, c: jnp.exp(ce - c), cum_end, cum)
    at = _each(lambda x, c, w: stack(x * jnp.exp(c - w)), a, cum, lw)
    rt = _each(lambda x, c: stack(x * jnp.exp(c)), r, cum)
    bt = _each(lambda x, e: stack(x * e), b, e_neg)
    kt = _each(lambda x, e: stack(x * e), k, e_neg)
    vs = _each(stack, v)
    gram = _each(lambda p, q, s, t: _nt(jnp.concatenate([p, q], axis=0), jnp.concatenate([s, t], axis=0)),
                 at, rt, bt, kt)
    nmat = _each(lambda g: jnp.where(strict2, g[:L2, :L2], 0.0), gram)
    a_k = _each(lambda g: jnp.where(strict2, g[:L2, L2:], 0.0), gram)
    r_bk = _each(lambda g: jnp.where(jnp.concatenate([incl2, incl2], axis=1), g[L2:, :], 0.0), gram)
    tinv = _each(lambda n: eye2 + n, nmat)
    pw = _each(lambda n: _nn(n, n), nmat)
    for lev in range(1, n_levels):
        if lev < n_levels - 1:
            both = _each(lambda p, t: _nn(p, jnp.concatenate([p, t], axis=1)), pw, tinv)
            pw = _each(lambda x: x[:, :L2], both)
            tinv = _each(lambda t, x: t + x[:, L2:], tinv, both)
        else:
            tinv = _each(lambda p, t: t + _nn(p, t), pw, tinv)
    akv = _each(_nn, a_k, vs)
    wu = _each(lambda t, p, q: _nn(t, jnp.concatenate([p, q], axis=1)), tinv, at, akv)
    zs = _each(lambda x, y: jnp.concatenate([x, jnp.concatenate([jnp.zeros_like(y), y], axis=1)], axis=0), wu, vs)
    qy = _each(_nn, r_bk, zs)
    md = _each(lambda x, y, e, z: _tn(jnp.concatenate([stack(x * e), stack(y * e)], axis=0), z),
               b, k, e_end, zs)
    dec = _each(lambda ce: jnp.sum(jnp.where(eye_w, jnp.broadcast_to(jnp.exp(ce), (W, W)), 0.0),
                                   axis=1, keepdims=True), cum_end)

    def head_mean(x):
        m0 = jnp.sum(jnp.where(head0, x, 0.0), axis=-1, keepdims=True)
        m1 = jnp.sum(jnp.where(head0, 0.0, x), axis=-1, keepdims=True)
        return jnp.where(head0, m0, m1) * (1.0 / D_HEAD_DIM)

    hs = [h_ref[p] for p in range(npair)]
    for i in range(nck):
        for p in range(npair):
            s = p * nck + i
            sl, lanes = where[s]
            ys = _nn(rt[s] + qy[s][:, :W], hs[p]) + qy[s][:, W:]
            y = ys[:L, :] + ys[L:, :]
            hs[p] = dec[s] * hs[p] + _nn(md[s][:, :W], hs[p]) + md[s][:, W:]
            if d == 0:
                y_ref[sl, lanes] = y
            else:
                par = par_ref[:, lanes]
                tot = y0_ref[sl, lanes] + y
                cen = tot - head_mean(tot)
                nrm = cen * lax.rsqrt(head_mean(cen * cen) + RWKV_EPS)
                bonus = head_mean(r[s] * k[s] * par[0:1, :]) * float(D_HEAD_DIM) * v[s]
                y_ref[sl, lanes] = ((nrm * par[1:2, :] + par[2:3, :] + bonus) * g_ref[sl, lanes]).astype(y_ref.dtype)
    for p in range(npair):
        h_ref[p] = hs[p]


def _rwkv_scan(r, k, v, a, b, lw, g, par, n_ctx):
    bsz, t, c = r.shape
    L = RWKV_CHUNK
    tb = SCAN_TIME_BLOCK
    nck, nb, ncb = tb // L, t // tb, n_ctx // tb
    y0 = None
    for d in (0, 1):
        tix = lambda j, d=d: _time_index(d, j, ncb, nb)
        seq = pl.BlockSpec((None, tb, c), lambda i, j: (i, tix(j), 0))
        in_specs = [seq] * 5 + [pl.BlockSpec((None, None, tb, c), lambda i, j, d=d: (d, i, tix(j), 0))]
        args = [r, k, v, a, b, lw]
        if d == 1:
            in_specs += [seq, seq, pl.BlockSpec(par.shape, lambda i, j: (0, 0))]
            args += [y0, g, par]
        y0 = pl.pallas_call(
            functools.partial(_rwkv_kernel, L=L, nck=nck, d=d),
            grid=(bsz, nb),
            in_specs=in_specs,
            out_specs=seq,
            out_shape=jax.ShapeDtypeStruct((bsz, t, c), BF16 if d else F32),
            scratch_shapes=[pltpu.VMEM((c // (2 * D_HEAD_DIM), 2 * D_HEAD_DIM, 2 * D_HEAD_DIM), F32)],
            compiler_params=_scan_params(2),
            name="rwkv7_scan",
        )(*args)
    return y0


def _softplus(x):
    return jnp.maximum(x, 0.0) + jnp.log(1.0 + jnp.exp(-jnp.abs(x)))


def _segment_neighbours(x, before_tile, after_tile):
    n = x.shape[0]
    row = lax.broadcasted_iota(jnp.int32, x.shape, 0)
    prev = jnp.where(row == 0, before_tile, pltpu.roll(x, 1, axis=0))
    nxt = jnp.where(row == n - 1, after_tile, pltpu.roll(x, n - 1, axis=0))
    return prev, nxt


def _even_prep_kernel(p_ref, prev_ref, next_ref, ca_ref, cb_ref, sm_ref,
                      xs_ref, bm_ref, cm_ref, dt_ref, la_ref, q_ref, k_ref, g_ref, *, tiles_per_seq, ctx_tiles):
    j = pl.program_id(0) % tiles_per_seq
    first = jnp.logical_or(j == 0, j == ctx_tiles)
    last = jnp.logical_or(j == ctx_tiles - 1, j == tiles_per_seq - 1)

    def conv_silu(lo, width, taps_ref):
        x = p_ref[:, lo:lo + width]
        prev, nxt = _segment_neighbours(x, jnp.where(first, 0.0, prev_ref[7:8, lo:lo + width]),
                                        jnp.where(last, 0.0, next_ref[0:1, lo:lo + width]))
        taps = taps_ref[...]
        return _silu(prev * taps[0:1, :] + x * taps[1:2, :] + nxt * taps[2:3, :] + taps[3:4, :])

    xbc = conv_silu(A_INNER, A_XBC, ca_ref)
    xs_ref[...] = xbc[:, :A_INNER]
    bm_ref[...] = xbc[:, A_INNER:A_INNER + A_GROUPS * A_STATE]
    cm_ref[...] = xbc[:, A_INNER + A_GROUPS * A_STATE:]
    qk = conv_silu(EV_B0, 2 * B_QK, cb_ref)
    q_ref[...] = qk[:, :B_QK]
    k_ref[...] = qk[:, B_QK:]
    sm = sm_ref[...]
    nh2 = 2 * A_HEADS
    dt = _softplus(p_ref[:, A_INNER + A_XBC:A_INNER + A_XBC + nh2] + sm[0:1, :])
    la = dt * sm[1:2, :]
    g0 = EV_B0 + 2 * B_QK + 2 * B_INNER
    gx = p_ref[:, g0:g0 + 4 * B_HEADS]
    g = jnp.where(sm[4:5, :] > 0.5, -_softplus(-(gx + sm[3:4, :])), gx + sm[2:3, :])
    for d in range(2):
        dt_ref[d] = dt[:, d * A_HEADS:(d + 1) * A_HEADS]
        la_ref[d] = la[:, d * A_HEADS:(d + 1) * A_HEADS]
        g_ref[d] = g[:, d * 2 * B_HEADS:(d + 1) * 2 * B_HEADS]


def _gate_lane_order(a):
    h = B_HEADS
    return jnp.concatenate([a[..., :-4 * h], a[..., -4 * h:-3 * h], a[..., -2 * h:-h], a[..., -3 * h:-2 * h], a[..., -h:]],
                           axis=-1)


def _even_mixers(p, n_ctx, ssd_params, mlstm_params):
    conv_w, conv_b, dt_bias, a_log, d_skip, norm_a = ssd_params
    conv_bw, conv_bb, i_bias, f_bias, norm_b = mlstm_params
    b, t, n_pad = p.shape
    p2 = p.reshape(b * t, n_pad)
    tiles = b * t // ROW_TILE
    halo = ROW_TILE // 8
    zeros = jnp.zeros((B_HEADS,), F32)
    sm = jnp.stack([dt_bias.reshape(-1), -jnp.exp(a_log).reshape(-1),
                    jnp.concatenate([i_bias[0], zeros, i_bias[1], zeros]),
                    jnp.concatenate([zeros, f_bias[0], zeros, f_bias[1]]),
                    jnp.concatenate([zeros, zeros + 1.0, zeros, zeros + 1.0])])
    ca = jnp.concatenate([conv_w, conv_b[None, :]], axis=0)
    cb = jnp.concatenate([conv_bw, conv_bb[None, :]], axis=0)
    whole = lambda a: pl.BlockSpec(a.shape, lambda i: (0,) * a.ndim)
    rows = lambda c: pl.BlockSpec((ROW_TILE, c), lambda i: (i, 0))
    per_dir = lambda c: pl.BlockSpec((2, ROW_TILE, c), lambda i: (0, i, 0))
    f32 = lambda *s: jax.ShapeDtypeStruct(s, F32)
    n = b * t
    gn = A_GROUPS * A_STATE
    xs, bm, cm, dt, la, q, k, g = pl.pallas_call(
        functools.partial(_even_prep_kernel, tiles_per_seq=t // ROW_TILE, ctx_tiles=n_ctx // ROW_TILE),
        grid=(tiles,),
        in_specs=[pl.BlockSpec((ROW_TILE, n_pad), lambda i: (i, 0)),
                  pl.BlockSpec((8, n_pad), lambda i: (jnp.maximum(i * halo - 1, 0), 0)),
                  pl.BlockSpec((8, n_pad), lambda i: (jnp.minimum((i + 1) * halo, tiles * halo - 1), 0)),
                  whole(ca), whole(cb), whole(sm)],
        out_specs=[rows(A_INNER), rows(gn), rows(gn), per_dir(A_HEADS), per_dir(A_HEADS),
                   rows(B_QK), rows(B_QK), per_dir(2 * B_HEADS)],
        out_shape=[f32(n, A_INNER), f32(n, gn), f32(n, gn), f32(2, n, A_HEADS), f32(2, n, A_HEADS),
                   f32(n, B_QK), f32(n, B_QK), f32(2, n, 2 * B_HEADS)],
        compiler_params=pltpu.CompilerParams(dimension_semantics=("arbitrary",), vmem_limit_bytes=VMEM_LIMIT_BYTES),
        name="even_prep",
    )(p2, p2, p2, ca, cb, sm)
    seq = lambda a: a.reshape(b, t, a.shape[-1])
    seq_d = lambda a: a.reshape(2, b, t, a.shape[-1])
    par = jnp.stack([jnp.repeat(d_skip, A_HEAD_DIM), norm_a])
    fa = _ssd_scan(seq(xs), seq_d(dt), seq_d(la), seq(bm), seq(cm), p, par, n_ctx)
    fb = _mlstm_scan(seq(q), seq(k), seq_d(g), p, norm_b[None, :], n_ctx)
    return fa, fb


def _hgrn2_mixer(p, n_ctx, lb, params):
    f_bias, norm_w = params
    return _gla_scan(p, lb, f_bias, norm_w, n_ctx)


def _rwkv_prep_kernel(p_ref, prev_ref, next_ref, mu_ref, w2_ref, a2_ref, g2_ref, vec_ref,
                      r_ref, k_ref, v_ref, a_ref, b_ref, g_ref, lw_ref, *, tiles_per_seq, ctx_tiles):
    c = D_INNER
    j = pl.program_id(0) % tiles_per_seq
    x = p_ref[:, P_C:]
    n = x.shape[0]
    first = jnp.logical_or(j == 0, j == ctx_tiles)
    last = jnp.logical_or(j == ctx_tiles - 1, j == tiles_per_seq - 1)
    prev, nxt = _segment_neighbours(x, jnp.where(first, 0.0, prev_ref[7:8, P_C:]),
                                    jnp.where(last, 0.0, next_ref[0:1, P_C:]))
    x = x + mu_ref[...] * (0.5 * (prev + nxt) - x)
    r, k, v = x[:, :c], x[:, c:2 * c], x[:, 2 * c:3 * c]
    o = 3 * c
    wl = jnp.tanh(x[:, o:o + 2 * D_W_LORA])
    gl = x[:, o + 2 * D_W_LORA:o + 2 * D_W_LORA + D_G_LORA]
    al = x[:, o + 2 * D_W_LORA + D_G_LORA:o + 2 * D_W_LORA + D_G_LORA + D_A_LORA]
    vec = vec_ref[...]
    for d in range(2):
        w = vec[d:d + 1, :] + _nn(wl, w2_ref[d])
        lw_ref[d] = -jnp.exp(-_softplus(-w) - 0.5)
    a = jax.nn.sigmoid(vec[2:3, :] + _nn(al, a2_ref[...]))
    g_ref[...] = _nn(jax.nn.sigmoid(gl), g2_ref[...])
    kx = k * vec[3:4, :]
    sq = kx * kx
    head0 = (lax.broadcasted_iota(jnp.int32, (n, 2 * D_HEAD_DIM), 1) < D_HEAD_DIM)
    sums = []
    for s in range(c // (2 * D_HEAD_DIM)):
        blk = sq[:, s * 2 * D_HEAD_DIM:(s + 1) * 2 * D_HEAD_DIM]
        s0 = jnp.sum(jnp.where(head0, blk, 0.0), axis=-1, keepdims=True)
        s1 = jnp.sum(jnp.where(head0, 0.0, blk), axis=-1, keepdims=True)
        sums.append(jnp.where(head0, s0, s1))
    kk = kx * lax.rsqrt(jnp.maximum(jnp.concatenate(sums, axis=1), 1e-12))
    r_ref[...] = r
    k_ref[...] = k * (1.0 + (a - 1.0) * vec[4:5, :])
    v_ref[...] = v
    a_ref[...] = -kk
    b_ref[...] = kk * a


def _rwkv7_mixer(p, n_ctx, params):
    mu, w0, w2, a0, a2, g2, k_k, k_a, r_k, ln_w, ln_b = params
    b, t, n_pad = p.shape
    c = D_INNER
    width = n_pad - P_C
    p2 = p.reshape(b * t, n_pad)
    tiles = b * t // ROW_TILE
    halo = ROW_TILE // 8
    w2z = jnp.zeros((2, 2 * D_W_LORA, c), F32)
    w2z = w2z.at[0, :D_W_LORA].set(w2[0]).at[1, D_W_LORA:].set(w2[1])
    vec = jnp.stack([w0[0], w0[1], a0, k_k, k_a])
    whole = lambda a: pl.BlockSpec(a.shape, lambda i: (0,) * a.ndim)
    rows = pl.BlockSpec((ROW_TILE, c), lambda i: (i, 0))
    mu_pad = jnp.pad(mu, (0, width - mu.shape[0]))[None, :]
    outs = pl.pallas_call(
        functools.partial(_rwkv_prep_kernel, tiles_per_seq=t // ROW_TILE, ctx_tiles=n_ctx // ROW_TILE),
        grid=(tiles,),
        in_specs=[pl.BlockSpec((ROW_TILE, n_pad), lambda i: (i, 0)),
                  pl.BlockSpec((8, n_pad), lambda i: (jnp.maximum(i * halo - 1, 0), 0)),
                  pl.BlockSpec((8, n_pad), lambda i: (jnp.minimum((i + 1) * halo, tiles * halo - 1), 0)),
                  whole(mu_pad), whole(w2z), whole(a2), whole(g2), whole(vec)],
        out_specs=[rows] * 6 + [pl.BlockSpec((2, ROW_TILE, c), lambda i: (0, i, 0))],
        out_shape=[jax.ShapeDtypeStruct((b * t, c), F32)] * 6 + [jax.ShapeDtypeStruct((2, b * t, c), F32)],
        compiler_params=pltpu.CompilerParams(dimension_semantics=("arbitrary",), vmem_limit_bytes=VMEM_LIMIT_BYTES),
        name="rwkv7_prep",
    )(p2, p2, p2, mu_pad, w2z, a2, g2, vec)
    r, k, v, a, bb, g = (u.reshape(b, t, c) for u in outs[:6])
    par = jnp.stack([r_k.reshape(D_INNER), ln_w, ln_b])
    return _rwkv_scan(r, k, v, a, bb, outs[6].reshape(2, b, t, c), g, par, n_ctx)


def _to_col_major(u, rows):
    b, s, d = u.shape
    return u.reshape(b, rows, GRID_W, d).transpose(0, 2, 1, 3).reshape(b, s, d)


def _from_col_major(u, rows):
    b, s, d = u.shape
    return u.reshape(b, GRID_W, rows, d).transpose(0, 2, 1, 3).reshape(b, s, d)


def _tile_specs(bsz, t, n_ctx):
    tiles_per_seq, ctx_tiles = t // ROW_TILE, n_ctx // ROW_TILE
    mod_row = lambda i: jnp.where(i % tiles_per_seq < ctx_tiles, bsz, i // tiles_per_seq)
    rows = lambda c: pl.BlockSpec((ROW_TILE, c), lambda i: (i, 0))
    whole = lambda a: pl.BlockSpec(a.shape, lambda i: (0,) * a.ndim)
    mod = pl.BlockSpec((None, 6, D_MODEL), lambda i: (mod_row(i), 0, 0))
    params = pltpu.CompilerParams(dimension_semantics=("arbitrary",), vmem_limit_bytes=VMEM_LIMIT_BYTES)
    return rows, whole, mod, params


def _in_proj_kernel(x_ref, mod_ref, w_ref, o_ref):
    m = mod_ref[...]
    o_ref[...] = _nn(x_ref[...] * (1.0 + m[1:2, :]) + m[0:1, :], w_ref[...])


def _in_proj(xa, mods, w, bsz, n_ctx):
    n = w.shape[1]
    n_pad = -(-n // LANES) * LANES
    wb = jnp.pad(w.astype(BF16), ((0, 0), (0, n_pad - n)))
    rows, whole, mod, params = _tile_specs(bsz, xa.shape[0] // bsz, n_ctx)
    return pl.pallas_call(
        _in_proj_kernel,
        grid=(xa.shape[0] // ROW_TILE,),
        in_specs=[rows(D_MODEL), mod, whole(wb)],
        out_specs=rows(n_pad),
        out_shape=jax.ShapeDtypeStruct((xa.shape[0], n_pad), F32),
        compiler_params=params,
        name="in_proj",
    )(xa, mods, wb)


def _norm_rows(z, ln):
    mu = jnp.mean(z, axis=-1, keepdims=True)
    zc = z - mu
    var = jnp.mean(zc * zc, axis=-1, keepdims=True)
    return zc * lax.rsqrt(var + LN_EPS) * ln[0:1, :] + ln[1:2, :]


def _out_proj_kernel(fa_ref, fb_ref, w_ref, x_ref, mod_ref, ln_ref, rw_ref, rb_ref,
                     xo_ref, hb_ref, e_ref, wt_ref, rank_ref, cnt_ref, carry_ref):
    ka = fa_ref.shape[1]
    m = mod_ref[...]
    y = _nn(fa_ref[...], w_ref[:ka, :]) + _nn(fb_ref[...], w_ref[ka:, :])
    xn = _norm_rows(DEEPNORM_ALPHA * x_ref[...] + m[2:3, :] * y, ln_ref[...])
    xo_ref[...] = xn
    h = xn * (1.0 + m[4:5, :]) + m[3:4, :]
    hb_ref[...] = h.astype(BF16)
    _route_tile(h, rw_ref, rb_ref, e_ref, wt_ref, rank_ref, cnt_ref, carry_ref)


def _out_proj(fa, fb, w, xa, mods, ln, router_w, router_bias, bsz, n_ctx):
    t_all, d = xa.shape
    rows, whole, mod, params = _tile_specs(bsz, t_all // bsz, n_ctx)
    wb = w.astype(BF16)
    rw = _member_major(router_w.T)
    rb = _member_major(router_bias.astype(F32).reshape(N_EXPERTS, 1))
    kt = lambda dt: jax.ShapeDtypeStruct((TOP_K, t_all), dt)
    blk = pl.BlockSpec((TOP_K, ROW_TILE), lambda i: (0, i))
    xo, hb, *routing = pl.pallas_call(
        _out_proj_kernel,
        grid=(t_all // ROW_TILE,),
        in_specs=[rows(fa.shape[1]), rows(fb.shape[1]), whole(wb), rows(d), mod, whole(ln), whole(rw), whole(rb)],
        out_specs=[rows(d), rows(d), blk, blk, blk, pl.BlockSpec((N_EXPERTS, 1), lambda i: (0, 0))],
        out_shape=[jax.ShapeDtypeStruct((t_all, d), F32), jax.ShapeDtypeStruct((t_all, d), BF16),
                   kt(jnp.int32), kt(F32), kt(jnp.int32), jax.ShapeDtypeStruct((N_EXPERTS, 1), jnp.int32)],
        scratch_shapes=[pltpu.VMEM((N_EXPERTS, 1), F32)],
        compiler_params=params,
        name="out_proj_norm_route",
    )(fa, fb, wb, xa, mods, ln, rw, rb)
    return xo, hb, routing


def _ffn_norm_kernel(*refs):
    y_refs, (w_ref, x_ref, mod_ref, ln_ref, xo_ref) = refs[:TOP_K], refs[TOP_K:]
    m = mod_ref[...]
    w = w_ref[...]
    f = functools.reduce(jnp.add, [y_refs[kk][...].astype(F32) * w[:, kk:kk + 1] for kk in range(TOP_K)])
    xo_ref[...] = _norm_rows(DEEPNORM_ALPHA * x_ref[...] + m[5:6, :] * f, ln_ref[...])


def _ffn_norm(ys, wts, xa, mods, ln, bsz, n_ctx):
    t_all, d = xa.shape
    rows, whole, mod, params = _tile_specs(bsz, t_all // bsz, n_ctx)
    return pl.pallas_call(
        _ffn_norm_kernel,
        grid=(t_all // ROW_TILE,),
        in_specs=[rows(d)] * TOP_K + [rows(TOP_K), rows(d), mod, whole(ln)],
        out_specs=rows(d),
        out_shape=jax.ShapeDtypeStruct((t_all, d), F32),
        compiler_params=params,
        name="ffn_residual_norm",
    )(*ys, wts, xa, mods, ln)


def kernel(x, c, ctx, c_ctx, mod_w, mod_b, ln_g, ln_b, ev_w_in, ev_w_out, ssd_conv_w, ssd_conv_b, ssd_dt_bias, ssd_a_log, ssd_d, ssd_norm_w, mlstm_conv_w, mlstm_conv_b, mlstm_i_bias, mlstm_f_bias, mlstm_norm_w, od_w_in, od_w_out, hgrn_lb_logits, hgrn_f_bias, hgrn_norm_w, rwkv_mu, rwkv_w0, rwkv_w2, rwkv_a0, rwkv_a2, rwkv_g2, rwkv_k_k, rwkv_k_a, rwkv_r_k, rwkv_ln_w, rwkv_ln_b, router_w, router_bias, exp_w_gate, exp_w_up, exp_w_down):
    bsz, seq, _ = x.shape
    n_ctx = ctx.shape[1]
    rows = seq // GRID_W
    lb_all = jnp.cumsum(jax.nn.softmax(hgrn_lb_logits.astype(F32), axis=0), axis=0)
    lb_all = lb_all - lb_all[0]
    s_c = jax.nn.silu(c)
    s_cc = jax.nn.silu(c_ctx)
    t = n_ctx + seq
    xa = jnp.concatenate([ctx, x], axis=1).reshape(bsz * t, D_MODEL)
    seq3 = lambda a: a.reshape(bsz, t, a.shape[-1])
    flat = lambda a: a.reshape(bsz * t, a.shape[-1])
    lat_order = lambda a, f: flat(jnp.concatenate([seq3(a)[:, :n_ctx], f(seq3(a)[:, n_ctx:], rows)], axis=1))
    for layer in range(DEPTH):
        i = layer // 2
        mods = _matmul(jnp.concatenate([s_c, s_cc[None]], axis=0), mod_w[layer], tm=8, tn=512) + mod_b[layer]
        mods = mods.reshape(bsz + 1, 6, D_MODEL)
        ln = jnp.stack([ln_g[layer], ln_b[layer]], axis=1)
        if layer % 2 == 0:
            w_in = jnp.concatenate([ev_w_in[i][:, :P_A], jnp.zeros((D_MODEL, EV_B0 - P_A), F32),
                                    _gate_lane_order(ev_w_in[i][:, P_A:])], axis=1)
            p = seq3(_in_proj(xa, mods, w_in, bsz, n_ctx))
            fa, fb = _even_mixers(
                p, n_ctx,
                (ssd_conv_w[i], ssd_conv_b[i], ssd_dt_bias[i], ssd_a_log[i], ssd_d[i], ssd_norm_w[i]),
                (mlstm_conv_w[i], mlstm_conv_b[i], mlstm_i_bias[i], mlstm_f_bias[i], mlstm_norm_w[i]))
            fa, fb, w_out = flat(fa), flat(fb), ev_w_out[i]
        else:
            g0, g1 = 3 * D_INNER + 2 * D_W_LORA + D_A_LORA, P_D
            lora_last = lambda a: jnp.concatenate([a[..., :g0 - D_A_LORA], a[..., g0:g1], a[..., g0 - D_A_LORA:g0]], axis=-1)
            w_in = jnp.concatenate([od_w_in[i][:, :P_C], lora_last(od_w_in[i][:, P_C:])], axis=1)
            p = seq3(_in_proj(lat_order(xa, _to_col_major), mods, w_in, bsz, n_ctx))
            fa = _hgrn2_mixer(p, n_ctx, lb_all[layer], (hgrn_f_bias[i], hgrn_norm_w[i]))
            fb = _rwkv7_mixer(p, n_ctx,
                              (lora_last(rwkv_mu[i]), rwkv_w0[i], rwkv_w2[i], rwkv_a0[i], rwkv_a2[i], rwkv_g2[i],
                               rwkv_k_k[i], rwkv_k_a[i], rwkv_r_k[i], rwkv_ln_w[i], rwkv_ln_b[i]))
            fa, fb, w_out = lat_order(fa, _from_col_major), lat_order(fb, _from_col_major), od_w_out[i]
        xa, hb, routing = _out_proj(fa, fb, w_out, xa, mods, ln[0], router_w, router_bias, bsz, n_ctx)
        ys, wts = _moe_ffn(hb, routing, exp_w_gate, exp_w_up, exp_w_down, layer)
        xa = _ffn_norm(ys, wts, xa, mods, ln[1], bsz, n_ctx)
    return seq3(xa)[:, n_ctx:]
```

```python
import functools
import math

import jax
import jax.numpy as jnp
from jax import lax
from jax.experimental import pallas as pl
from jax.experimental.pallas import tpu as pltpu

F32 = jnp.float32
BF16 = jnp.bfloat16

D_MODEL = 1024
DEPTH = 4
GRID_W = 64
A_HEADS = 8
A_HEAD_DIM = 64
A_INNER = A_HEADS * A_HEAD_DIM
A_GROUPS = 2
A_STATE = 64
A_XBC = A_INNER + 2 * A_GROUPS * A_STATE
B_HEADS = 4
B_QK_DIM = 64
B_V_DIM = 128
B_QK = B_HEADS * B_QK_DIM
B_INNER = B_HEADS * B_V_DIM
MLSTM_EPS = 1e-6
C_HEADS = 4
C_HEAD_DIM = 128
C_INNER = C_HEADS * C_HEAD_DIM
D_HEADS = 8
D_HEAD_DIM = 64
D_INNER = D_HEADS * D_HEAD_DIM
D_W_LORA = 64
D_A_LORA = 64
D_G_LORA = 128
RWKV_EPS = 64e-5
P_A = A_INNER + A_XBC + 2 * A_HEADS
P_B = 2 * B_QK + 2 * B_INNER + 4 * B_HEADS
P_C = 5 * C_INNER
P_D = 3 * D_INNER + 2 * D_W_LORA + D_A_LORA + D_G_LORA
EV_B0 = 3 * A_INNER
N_EXPERTS = 32
N_EXPERT_GROUPS = 8
EXPERTS_PER_GROUP = N_EXPERTS // N_EXPERT_GROUPS
TOP_K = 2
D_EXPERT = 512
MOE_BLOCK = 512
RANK_BLOCK = 512
ROW_TILE = 256
HALO_ROWS = 16
LANES = 128
DEEPNORM_ALPHA = (2 * DEPTH) ** 0.25
LN_EPS = 1e-5
M_INIT = -1e30
NEG_BIG = -1e30

SSD_CHUNK = 128
MLSTM_CHUNK = 128
GLA_CHUNK = 16
GLA_BLOCK = 64
RWKV_CHUNK = 64
SCAN_TIME_BLOCK = 256

VMEM_LIMIT_BYTES = 48 * 1024 * 1024
HI = lax.Precision.HIGHEST


def _dot(a, b, dims, exact):
    if exact:
        return lax.dot_general(a.astype(F32), b.astype(F32), (dims, ((), ())),
                               precision=HI, preferred_element_type=F32)
    return lax.dot_general(a.astype(BF16), b.astype(BF16), (dims, ((), ())),
                           preferred_element_type=F32)


def _nn(a, b, exact=False):
    return _dot(a, b, ((1,), (0,)), exact)


def _nt(a, b, exact=False):
    return _dot(a, b, ((1,), (1,)), exact)


def _tn(a, b, exact=False):
    return _dot(a, b, ((0,), (0,)), exact)


def _iota2(n, m):
    return (lax.broadcasted_iota(jnp.int32, (n, m), 0),
            lax.broadcasted_iota(jnp.int32, (n, m), 1))


def _split3(x):
    x1 = x.astype(BF16)
    r1 = x - x1.astype(F32)
    x2 = r1.astype(BF16)
    x3 = (r1 - x2.astype(F32)).astype(BF16)
    return x1, x2, x3


def _mask_nn(mask, x):
    mb = mask.astype(BF16)
    x1, x2, x3 = _split3(x)
    return _nn(mb, x1) + _nn(mb, x2) + _nn(mb, x3)


def _nn_mask(x, mask):
    mb = mask.astype(BF16)
    x1, x2, x3 = _split3(x)
    return _nn(x1, mb) + _nn(x2, mb) + _nn(x3, mb)


def _each(f, *cols):
    return [f(*xs) for xs in zip(*cols)]


def _silu(x):
    return x * jax.nn.sigmoid(x)


def _mm_kernel(x_ref, w_ref, o_ref, *, exact):
    o_ref[...] = _nn(x_ref[...], w_ref[...], exact)


def _matmul(x, w, tm=512, tn=512, exact=False):
    m, k = x.shape
    n = w.shape[1]
    n_pad = -(-n // tn) * tn
    m_pad = -(-m // tm) * tm
    xb = x if exact else x.astype(BF16)
    wb = w if exact else w.astype(BF16)
    if n_pad != n:
        wb = jnp.pad(wb, ((0, 0), (0, n_pad - n)))
    if m_pad != m:
        xb = jnp.pad(xb, ((0, m_pad - m), (0, 0)))
    out = pl.pallas_call(
        functools.partial(_mm_kernel, exact=exact),
        grid=(n_pad // tn, m_pad // tm),
        in_specs=[pl.BlockSpec((tm, k), lambda j, i: (i, 0)),
                  pl.BlockSpec((k, tn), lambda j, i: (0, j))],
        out_specs=pl.BlockSpec((tm, tn), lambda j, i: (i, j)),
        out_shape=jax.ShapeDtypeStruct((m_pad, n_pad), F32),
        compiler_params=pltpu.CompilerParams(
            dimension_semantics=("arbitrary", "arbitrary"),
            vmem_limit_bytes=VMEM_LIMIT_BYTES),
        name="dense_matmul",
    )(xb, wb)
    return out[:m, :n]


def _moe_kernel(blk_e_ref, n_used_ref, x_ref, wg_ref, wu_ref, wd_ref, o_ref):
    i = pl.program_id(0)

    @pl.when(i < n_used_ref[0])
    def _():
        x = x_ref[...]
        g = _nn(x, wg_ref[...])
        u = _nn(x, wu_ref[...])
        o_ref[...] = _nn(g * jax.nn.sigmoid(g) * u, wd_ref[...]).astype(o_ref.dtype)

    @pl.when(i >= n_used_ref[0])
    def _():
        o_ref[...] = jnp.zeros_like(o_ref)


def _moe_experts(xp, blk_e, n_used, w_gate, w_up, w_down, layer):
    n_rows, d = xp.shape
    n_blocks = n_rows // MOE_BLOCK
    grid_spec = pltpu.PrefetchScalarGridSpec(
        num_scalar_prefetch=2,
        grid=(n_blocks,),
        in_specs=[
            pl.BlockSpec((MOE_BLOCK, d), lambda i, be, nu: (i, 0)),
            pl.BlockSpec((None, None, d, D_EXPERT), lambda i, be, nu: (layer, be[i], 0, 0)),
            pl.BlockSpec((None, None, d, D_EXPERT), lambda i, be, nu: (layer, be[i], 0, 0)),
            pl.BlockSpec((None, None, D_EXPERT, d), lambda i, be, nu: (layer, be[i], 0, 0)),
        ],
        out_specs=pl.BlockSpec((MOE_BLOCK, d), lambda i, be, nu: (i, 0)),
    )
    return pl.pallas_call(
        _moe_kernel,
        grid_spec=grid_spec,
        out_shape=jax.ShapeDtypeStruct((n_rows, d), BF16),
        compiler_params=pltpu.CompilerParams(
            dimension_semantics=("arbitrary",),
            vmem_limit_bytes=VMEM_LIMIT_BYTES),
        name="moe_experts",
    )(blk_e, n_used, xp, w_gate, w_up, w_down)


def _top2(vals):
    m = len(vals)
    m1 = functools.reduce(jnp.maximum, vals)
    i1 = jnp.full_like(m1, float(m - 1))
    for j in reversed(range(m - 1)):
        i1 = jnp.where(vals[j] == m1, float(j), i1)
    rest = [jnp.where(i1 == float(j), -jnp.inf, vals[j]) for j in range(m)]
    m2 = functools.reduce(jnp.maximum, rest)
    i2 = jnp.full_like(m1, float(m - 1))
    for j in reversed(range(m - 1)):
        i2 = jnp.where(rest[j] == m2, float(j), i2)
    return m1, i1, m2, i2


def _router_kernel(h_ref, wt_ref, bias_ref, e_ref, w_ref, rank_ref, cnt_ref, carry_ref):
    tm = h_ref.shape[0]
    ng, per = N_EXPERT_GROUPS, EXPERTS_PER_GROUP

    @pl.when(pl.program_id(0) == 0)
    def _():
        carry_ref[...] = jnp.zeros_like(carry_ref)

    s = jax.nn.sigmoid(_nt(wt_ref[...], h_ref[...], True))
    sb = s + bias_ref[...]
    biased = [sb[j * ng:(j + 1) * ng, :] for j in range(per)]
    plain = [s[j * ng:(j + 1) * ng, :] for j in range(per)]
    m1, _, m2, _ = _top2(biased)
    gsum = m1 + m2
    rows = lax.broadcasted_iota(jnp.int32, (ng, tm), 0).astype(F32)
    gmax = jnp.max(gsum, axis=0, keepdims=True)
    gi = jnp.min(jnp.where(gsum == gmax, rows, float(ng)), axis=0, keepdims=True)
    sel = rows == gi
    pick = lambda v: jnp.sum(jnp.where(sel, v, 0.0), axis=0, keepdims=True)
    in_b = [pick(v) for v in biased]
    in_s = [pick(v) for v in plain]
    _, l1, _, l2 = _top2(in_b)
    w1 = functools.reduce(jnp.add, [jnp.where(l1 == float(j), in_s[j], 0.0) for j in range(per)])
    w2 = functools.reduce(jnp.add, [jnp.where(l2 == float(j), in_s[j], 0.0) for j in range(per)])
    e1 = gi * float(per) + l1
    e2 = gi * float(per) + l2
    wsum = w1 + w2
    e_ref[0:1, :] = e1.astype(jnp.int32)
    e_ref[1:2, :] = e2.astype(jnp.int32)
    w_ref[0:1, :] = w1 / wsum
    w_ref[1:2, :] = w2 / wsum
    row = lax.broadcasted_iota(jnp.int32, (N_EXPERTS, tm), 0)
    experts = ((row % ng) * per + row // ng).astype(F32)
    oh1 = jnp.where(experts == e1, 1.0, 0.0)
    oh2 = jnp.where(experts == e2, 1.0, 0.0)
    oh = oh1 + oh2
    ri, ci = _iota2(tm, tm)
    seen = _nn(oh, jnp.where(ri < ci, 1.0, 0.0)) + carry_ref[...]
    rank_ref[0:1, :] = jnp.sum(oh1 * seen, axis=0, keepdims=True).astype(jnp.int32)
    rank_ref[1:2, :] = jnp.sum(oh2 * seen, axis=0, keepdims=True).astype(jnp.int32)
    carry = carry_ref[...] + jnp.sum(oh, axis=1, keepdims=True)
    carry_ref[...] = carry
    cnt_ref[...] = carry.astype(jnp.int32)


def _route(h, router_w, router_bias):
    t, d = h.shape
    tm = RANK_BLOCK
    kt = lambda dt: jax.ShapeDtypeStruct((TOP_K, t), dt)
    blk = pl.BlockSpec((TOP_K, tm), lambda i: (0, i))
    member_major = lambda a: a.reshape(N_EXPERT_GROUPS, EXPERTS_PER_GROUP, -1).transpose(1, 0, 2).reshape(N_EXPERTS, -1)
    e, w, rank, counts = pl.pallas_call(
        _router_kernel,
        grid=(t // tm,),
        in_specs=[pl.BlockSpec((tm, d), lambda i: (i, 0)),
                  pl.BlockSpec((N_EXPERTS, d), lambda i: (0, 0)),
                  pl.BlockSpec((N_EXPERTS, 1), lambda i: (0, 0))],
        out_specs=[blk, blk, blk, pl.BlockSpec((N_EXPERTS, 1), lambda i: (0, 0))],
        out_shape=[kt(jnp.int32), kt(F32), kt(jnp.int32), jax.ShapeDtypeStruct((N_EXPERTS, 1), jnp.int32)],
        scratch_shapes=[pltpu.VMEM((N_EXPERTS, 1), F32)],
        compiler_params=pltpu.CompilerParams(dimension_semantics=("arbitrary",),
                                             vmem_limit_bytes=VMEM_LIMIT_BYTES),
        name="moe_router",
    )(h, member_major(router_w.T), member_major(router_bias.astype(F32).reshape(N_EXPERTS, 1)))
    counts = counts.reshape(EXPERTS_PER_GROUP, N_EXPERT_GROUPS).T.reshape(N_EXPERTS)
    return e, w, rank, counts


def _moe_ffn(h, hb, router_w, router_bias, w_gate, w_up, w_down, layer):
    t, d = h.shape
    expert, wts, rank, counts = _route(h, router_w, router_bias)
    n_assign = t * TOP_K
    padded = (counts + MOE_BLOCK - 1) // MOE_BLOCK * MOE_BLOCK
    pends = jnp.cumsum(padded)
    pstarts = pends - padded
    start_of = jnp.sum(jnp.where(expert[..., None] == jnp.arange(N_EXPERTS, dtype=jnp.int32), pstarts, 0), axis=-1)
    dest = start_of + rank
    n_blocks = -(-n_assign // MOE_BLOCK) + N_EXPERTS
    slot_token = jnp.arange(n_blocks * MOE_BLOCK, dtype=jnp.int32) % t
    slot_token = slot_token.at[dest.reshape(-1)].set(jnp.tile(jnp.arange(t, dtype=jnp.int32), TOP_K),
                                                      unique_indices=True)
    xp = hb[slot_token]
    blk_start = jnp.arange(n_blocks, dtype=jnp.int32) * MOE_BLOCK
    blk_e = jnp.minimum(jnp.sum(pends[None, :] <= blk_start[:, None], axis=1), N_EXPERTS - 1).astype(jnp.int32)
    n_used = (pends[-1] // MOE_BLOCK).astype(jnp.int32).reshape(1)
    yp = _moe_experts(xp, blk_e, n_used, w_gate, w_up, w_down, layer)
    return [yp[dest[kk]] for kk in range(TOP_K)], wts.T


def _time_index(d, j, n_ctx_blocks, n_blocks):
    if d == 0:
        return j
    return jnp.where(j < n_ctx_blocks, n_ctx_blocks - 1 - j, n_blocks - 1 - j + n_ctx_blocks)


def _end_row(x, d):
    n = x.shape[0]
    return x[0:1, :] if d == 1 else x[n - 1:n, :]


def _before(n, d):
    ri, ci = _iota2(n, n)
    return ci >= ri if d == 1 else ci <= ri


def _chunk_slices(nck, L, d):
    order = range(nck - 1, -1, -1) if d == 1 else range(nck)
    return [slice(i * L, (i + 1) * L) for i in order]


def _scan_params(n_axes):
    return pltpu.CompilerParams(dimension_semantics=("arbitrary",) * n_axes, vmem_limit_bytes=VMEM_LIMIT_BYTES)


def _ssd_kernel(*refs, L, d):
    if d == 0:
        x_ref, dt_ref, lac_ref, lar_ref, b_ref, c_ref, y_ref, st_ref = refs
    else:
        x_ref, dt_ref, lac_ref, lar_ref, b_ref, c_ref, y0_ref, z_ref, par_ref, y_ref, st_ref = refs
    hp, gn = A_INNER, A_GROUPS * A_STATE
    hpg = A_HEADS // A_GROUPS

    @pl.when(pl.program_id(1) == 0)
    def _():
        st_ref[...] = jnp.zeros_like(st_ref)

    before = _before(L, d)
    xs = x_ref[...]
    bm = b_ref[...]
    cm = c_ref[...]
    ccol = _mask_nn(before, lac_ref[...])
    crow = _nn_mask(lar_ref[...], _before(L, 1 - d))
    head_of_lane = lax.broadcasted_iota(jnp.int32, (A_HEADS, hp), 1) // A_HEAD_DIM
    expand = head_of_lane == lax.broadcasted_iota(jnp.int32, (A_HEADS, hp), 0)
    cum = _nn_mask(ccol, expand)
    x = xs * _nn_mask(dt_ref[...], expand)
    end = _end_row(cum, d)
    group_of_lane = lax.broadcasted_iota(jnp.int32, (L, gn), 1) // A_STATE
    first_of_pair = (lax.broadcasted_iota(jnp.int32, (L, 2 * A_HEAD_DIM), 1) < A_HEAD_DIM)
    cbs = [_nt(jnp.where(group_of_lane == g, cm, 0.0), bm) for g in range(A_GROUPS)]
    heads = list(range(A_HEADS))
    decay = [jnp.exp(jnp.where(before, ccol[:, h:h + 1] - crow[h:h + 1, :], NEG_BIG)) for h in heads]
    yh = [_nn(cbs[h // hpg] * decay[h], x[:, (h // 2) * 2 * A_HEAD_DIM:(h // 2 + 1) * 2 * A_HEAD_DIM]) for h in heads]
    pairs = [jnp.where(first_of_pair, yh[2 * p], yh[2 * p + 1]) for p in range(A_HEADS // 2)]
    st = st_ref[...]
    y = jnp.concatenate(pairs, axis=1) + jnp.exp(cum) * _nn(cm, st)
    own_group = (lax.broadcasted_iota(jnp.int32, (gn, hp), 0) // A_STATE
                 == lax.broadcasted_iota(jnp.int32, (gn, hp), 1) // (A_HEAD_DIM * hpg))
    st_ref[...] = jnp.exp(end) * st + jnp.where(own_group, _tn(bm, x * jnp.exp(end - cum)), 0.0)
    if d == 0:
        y_ref[...] = y
    else:
        par = par_ref[...]
        u = (y0_ref[...] + y + par[0:1, :] * xs) * _silu(z_ref[...].astype(F32))
        y_ref[...] = (u * lax.rsqrt(jnp.mean(u * u, axis=-1, keepdims=True) + 1e-6) * par[1:2, :]).astype(y_ref.dtype)


def _ssd_scan(xs, dt, la, bm, cm, p, par, n_ctx):
    b, t, hp = xs.shape
    L = SSD_CHUNK
    nb, ncb = t // L, n_ctx // L
    lar = jnp.swapaxes(la, 2, 3)
    y0 = None
    for d in (0, 1):
        tix = lambda j, d=d: _time_index(d, j, ncb, nb)
        seq = lambda c: pl.BlockSpec((None, L, c), lambda i, j: (i, tix(j), 0))
        in_specs = [seq(hp),
                    pl.BlockSpec((None, None, L, A_HEADS), lambda i, j, d=d: (d, i, tix(j), 0)),
                    pl.BlockSpec((None, None, L, A_HEADS), lambda i, j, d=d: (d, i, tix(j), 0)),
                    pl.BlockSpec((None, None, A_HEADS, L), lambda i, j, d=d: (d, i, 0, tix(j))),
                    seq(bm.shape[-1]), seq(bm.shape[-1])]
        args = [xs, dt, la, lar, bm, cm]
        if d == 1:
            in_specs += [seq(hp), seq(hp), pl.BlockSpec(par.shape, lambda i, j: (0, 0))]
            args += [y0, p, par]
        y0 = pl.pallas_call(
            functools.partial(_ssd_kernel, L=L, d=d),
            grid=(b, nb),
            in_specs=in_specs,
            out_specs=seq(hp),
            out_shape=jax.ShapeDtypeStruct((b, t, hp), BF16 if d else F32),
            scratch_shapes=[pltpu.VMEM((bm.shape[-1], hp), F32)],
            compiler_params=_scan_params(2),
            name="ssd_scan",
        )(*args)
    return y0


def _mlstm_kernel(*refs, L, d):
    if d == 0:
        q_ref, k_ref, v_ref, gc_ref, gr_ref, h_ref, c_ref, n_ref, m_ref = refs
    else:
        q_ref, k_ref, v_ref, gc_ref, gr_ref, h0_ref, o_ref, par_ref, h_ref, c_ref, n_ref, m_ref = refs
    nh, dk, dv = B_HEADS, B_QK_DIM, B_V_DIM

    @pl.when(pl.program_id(1) == 0)
    def _():
        c_ref[...] = jnp.zeros_like(c_ref)
        n_ref[...] = jnp.zeros_like(n_ref)
        m_ref[...] = jnp.full_like(m_ref, M_INIT)

    before = _before(L, d)
    gc = gc_ref[...]
    gr = gr_ref[...]
    fcol = _mask_nn(before, gc[:, nh:])
    frow = _nn_mask(gr[nh:, :], _before(L, 1 - d))
    lane_head = lax.broadcasted_iota(jnp.int32, (L, 2 * dk), 1) // dk
    heads = list(range(nh))
    slab = [slice((h // 2) * 2 * dk, (h // 2 + 1) * 2 * dk) for h in heads]
    lanes = [slice(h * dv, (h + 1) * dv) for h in heads]
    q = [jnp.where(lane_head == h % 2, q_ref[:, slab[h]], 0.0) * (dk ** -0.5) for h in heads]
    k = [k_ref[:, slab[h]] for h in heads]
    v = [v_ref[:, lanes[h]] for h in heads]
    li_c = [gc[:, h:h + 1] for h in heads]
    li_r = [gr[h:h + 1, :] for h in heads]
    f_c = [fcol[:, h:h + 1] for h in heads]
    f_r = [frow[h:h + 1, :] for h in heads]
    ftot = _each(lambda x: _end_row(x, d), f_c)
    c_prev = [c_ref[h] for h in heads]
    n_prev = [n_ref[h] for h in heads]
    m_prev = [m_ref[h] for h in heads]
    w_end = _each(lambda ft, fc, lc: ft - fc + lc, ftot, f_c, li_c)
    m_loc = _each(lambda w: jnp.max(w, axis=0, keepdims=True), w_end)
    ke = _each(lambda x, w, m: x * jnp.exp(w - m), k, w_end, m_loc)
    c_loc = _each(_tn, ke, v)
    n_loc = _each(lambda x: jnp.sum(x, axis=0, keepdims=True), ke)
    log_d = _each(lambda fc, fr, lr: jnp.where(before, fc - fr + lr, NEG_BIG), f_c, f_r, li_r)
    log_inter = _each(jnp.add, f_c, m_prev)
    m_row = _each(lambda ld, lint: jnp.maximum(jnp.max(ld, axis=-1, keepdims=True), lint), log_d, log_inter)
    s = _each(lambda a, b, ld, mr: _nt(a, b) * jnp.exp(ld - mr), q, k, log_d, m_row)
    inter = _each(lambda lint, mr: jnp.exp(lint - mr), log_inter, m_row)
    num = _each(lambda ss, vv, it, qq, cp: _nn(ss, vv) + it * _nn(qq, cp), s, v, inter, q, c_prev)
    den = _each(lambda ss, it, qq, npv: jnp.sum(ss, axis=-1, keepdims=True)
                + it * jnp.sum(qq * npv, axis=-1, keepdims=True), s, inter, q, n_prev)
    out = _each(lambda nu, de, mr: nu / jnp.maximum(jnp.abs(de), jnp.exp(-mr)), num, den, m_row)
    m_new = _each(lambda ft, mp, ml: jnp.maximum(ft + mp, ml), ftot, m_prev, m_loc)
    sp = _each(lambda ft, mp, mn: jnp.exp(ft + mp - mn), ftot, m_prev, m_new)
    sc = _each(lambda ml, mn: jnp.exp(ml - mn), m_loc, m_new)
    for h in heads:
        c_ref[h] = sp[h] * c_prev[h] + sc[h] * c_loc[h]
        n_ref[h] = sp[h] * n_prev[h] + sc[h] * n_loc[h]
        m_ref[h] = m_new[h]
        if d == 0:
            h_ref[:, lanes[h]] = out[h]
        else:
            tot = h0_ref[:, lanes[h]] + out[h]
            cen = tot - jnp.mean(tot, axis=-1, keepdims=True)
            nrm = cen * lax.rsqrt(jnp.mean(cen * cen, axis=-1, keepdims=True) + MLSTM_EPS)
            h_ref[:, lanes[h]] = (jax.nn.sigmoid(o_ref[:, lanes[h]].astype(F32)) * nrm * par_ref[:, lanes[h]]).astype(h_ref.dtype)


def _mlstm_scan(q, k, gates, p, par, n_ctx):
    b, t, _ = q.shape
    L = MLSTM_CHUNK
    nb, ncb = t // L, n_ctx // L
    v_blk, o_blk = (EV_B0 + 2 * B_QK) // B_INNER, (EV_B0 + 2 * B_QK + B_INNER) // B_INNER
    gates_r = jnp.swapaxes(gates, 2, 3)
    h0 = None
    for d in (0, 1):
        tix = lambda j, d=d: _time_index(d, j, ncb, nb)
        seq = lambda c, blk=0: pl.BlockSpec((None, L, c), lambda i, j: (i, tix(j), blk))
        in_specs = [seq(B_QK), seq(B_QK), seq(B_INNER, v_blk),
                    pl.BlockSpec((None, None, L, 2 * B_HEADS), lambda i, j, d=d: (d, i, tix(j), 0)),
                    pl.BlockSpec((None, None, 2 * B_HEADS, L), lambda i, j, d=d: (d, i, 0, tix(j)))]
        args = [q, k, p, gates, gates_r]
        if d == 1:
            in_specs += [seq(B_INNER), seq(B_INNER, o_blk), pl.BlockSpec(par.shape, lambda i, j: (0, 0))]
            args += [h0, p, par]
        h0 = pl.pallas_call(
            functools.partial(_mlstm_kernel, L=L, d=d),
            grid=(b, nb),
            in_specs=in_specs,
            out_specs=seq(B_INNER),
            out_shape=jax.ShapeDtypeStruct((b, t, B_INNER), BF16 if d else F32),
            scratch_shapes=[pltpu.VMEM((B_HEADS, 2 * B_QK_DIM, B_V_DIM), F32),
                            pltpu.VMEM((B_HEADS, 1, 2 * B_QK_DIM), F32),
                            pltpu.VMEM((B_HEADS, 1, 1), F32)],
            compiler_params=_scan_params(2),
            name="mlstm_scan",
        )(*args)
    return h0


def _gla_kernel(*refs, L, sub, nck, d):
    if d == 0:
        q_ref, f_ref, v_ref, par_ref, y_ref, st_ref = refs
    else:
        q_ref, f_ref, v_ref, par_ref, y0_ref, g_ref, y_ref, st_ref = refs
    hd = C_HEAD_DIM
    nhead = st_ref.shape[0]

    @pl.when(pl.program_id(1) == 0)
    def _():
        st_ref[...] = jnp.zeros_like(st_ref)

    before = _before(L, d)
    rows = lax.broadcasted_iota(jnp.int32, (L, 1), 0)
    slices = _chunk_slices(nck, L, d)
    where = [(sl, slice(h * hd, (h + 1) * hd)) for h in range(nhead) for sl in slices]
    lb = [par_ref[0:1, lanes] for _, lanes in where]
    f_pre = [f_ref[sl, lanes].astype(F32) + par_ref[1:2, lanes] for sl, lanes in where]
    q = [_silu(q_ref[sl, lanes].astype(F32)) for sl, lanes in where]
    v = [v_ref[sl, lanes] for sl, lanes in where]
    lf = _each(lambda b, x: jnp.log(b + (1.0 - b) * jax.nn.sigmoid(x)), lb, f_pre)
    k = _each(lambda b, x: (1.0 - b) * jax.nn.sigmoid(-x), lb, f_pre)
    lam = _each(lambda x: _mask_nn(before, x), lf)
    lam_end = _each(lambda x: _end_row(x, d), lam)
    blocks = [[] for _ in where]
    for c in range(L // sub):
        lo, hi = c * sub, (c + 1) * sub
        upto = rows >= lo if d == 1 else rows < hi
        for s in range(len(where)):
            edge = (hi, hi + 1) if d == 1 else (lo - 1, lo)
            ref = lam[s][edge[0]:edge[1], :] if 0 <= edge[0] < L else jnp.zeros_like(lam_end[s])
            qc = q[s][lo:hi, :] * jnp.exp(lam[s][lo:hi, :] - ref)
            kc = k[s] * jnp.exp(jnp.where(upto, ref - lam[s], NEG_BIG))
            blocks[s].append(_nt(qc, kc))
    att = _each(lambda bl: jnp.where(before, jnp.concatenate(bl, axis=0), 0.0), blocks)
    y_intra = _each(_nn, att, v)
    q_in = _each(lambda x, l: x * jnp.exp(l), q, lam)
    kv = _each(lambda x, y, l, le: _tn(x, y * jnp.exp(le - l)), v, k, lam, lam_end)
    dec = _each(jnp.exp, lam_end)
    for h in range(nhead):
        st = st_ref[h]
        for i in range(nck):
            s = h * nck + i
            sl, lanes = where[s]
            y = y_intra[s] + _nt(q_in[s], st)
            st = st * dec[s] + kv[s]
            if d == 0:
                y_ref[sl, lanes] = y
            else:
                tot = y0_ref[sl, lanes] + y
                nrm = tot * lax.rsqrt(jnp.mean(tot * tot, axis=-1, keepdims=True) + 1e-6)
                y_ref[sl, lanes] = (nrm * par_ref[2:3, lanes] * _silu(g_ref[sl, lanes].astype(F32))).astype(y_ref.dtype)
        st_ref[h] = st


def _gla_scan(p, lb, f_bias, norm_w, n_ctx):
    b, t, _ = p.shape
    c = C_INNER
    L = GLA_BLOCK
    tb = SCAN_TIME_BLOCK
    nck, nb, ncb = tb // L, t // tb, n_ctx // tb
    y0 = None
    for d in (0, 1):
        tix = lambda j, d=d: _time_index(d, j, ncb, nb)
        seq = lambda blk=0: pl.BlockSpec((None, tb, c), lambda i, j: (i, tix(j), blk))
        par = jnp.stack([jnp.broadcast_to(lb, (c,)), f_bias[d], norm_w])
        in_specs = [seq(0), seq(1 + d), seq(3), pl.BlockSpec(par.shape, lambda i, j: (0, 0))]
        args = [p, p, p, par]
        if d == 1:
            in_specs += [seq(), seq(4)]
            args += [y0, p]
        y0 = pl.pallas_call(
            functools.partial(_gla_kernel, L=L, sub=GLA_CHUNK, nck=nck, d=d),
            grid=(b, nb),
            in_specs=in_specs,
            out_specs=seq(),
            out_shape=jax.ShapeDtypeStruct((b, t, c), BF16 if d else F32),
            scratch_shapes=[pltpu.VMEM((c // C_HEAD_DIM, C_HEAD_DIM, C_HEAD_DIM), F32)],
            compiler_params=_scan_params(2),
            name="gla_scan",
        )(*args)
    return y0


def _rwkv_kernel(*refs, L, nck, d):
    if d == 0:
        r_ref, k_ref, v_ref, a_ref, b_ref, lw_ref, y_ref, h_ref = refs
    else:
        r_ref, k_ref, v_ref, a_ref, b_ref, lw_ref, y0_ref, g_ref, par_ref, y_ref, h_ref = refs
    L2 = 2 * L
    W = 2 * D_HEAD_DIM
    sgn = 1 - 2 * d

    @pl.when(pl.program_id(1) == 0)
    def _():
        h_ref[...] = jnp.zeros_like(h_ref)

    before = _before(L, d)
    r2, c2 = _iota2(L2, L2)
    order2 = ((c2 & (L - 1)) - (r2 & (L - 1))) * sgn
    strict2 = order2 < 0
    incl2 = order2 <= 0
    eye2 = jnp.where(r2 == c2, 1.0, 0.0)
    rw, cw = _iota2(W, W)
    eye_w = rw == cw
    head0 = lax.broadcasted_iota(jnp.int32, (L, W), 1) < D_HEAD_DIM
    stack = lambda x: jnp.concatenate([jnp.where(head0, x, 0.0), jnp.where(head0, 0.0, x)], axis=0)
    n_levels = int(math.log2(L))

    npair = h_ref.shape[0]
    slices = _chunk_slices(nck, L, d)
    where = [(sl, slice(p * W, (p + 1) * W)) for p in range(npair) for sl in slices]
    r, k, v, a, b, lw = ([ref[sl, lanes] for sl, lanes in where]
                         for ref in (r_ref, k_ref, v_ref, a_ref, b_ref, lw_ref))
    cum = _each(lambda x: _mask_nn(before, x), lw)
    cum_end = _each(lambda c: _end_row(c, d), cum)
    e_neg = _each(lambda c: jnp.exp(-c), cum)
    e_end = _each(lambda ce, c: jnp.exp(ce - c), cum_end, cum)
    at = _each(lambda x, c, w: stack(x * jnp.exp(c - w)), a, cum, lw)
    rt = _each(lambda x, c: stack(x * jnp.exp(c)), r, cum)
    bt = _each(lambda x, e: stack(x * e), b, e_neg)
    kt = _each(lambda x, e: stack(x * e), k, e_neg)
    vs = _each(stack, v)
    gram = _each(lambda p, q, s, t: _nt(jnp.concatenate([p, q], axis=0), jnp.concatenate([s, t], axis=0)),
                 at, rt, bt, kt)
    nmat = _each(lambda g: jnp.where(strict2, g[:L2, :L2], 0.0), gram)
    a_k = _each(lambda g: jnp.where(strict2, g[:L2, L2:], 0.0), gram)
    r_bk = _each(lambda g: jnp.where(jnp.concatenate([incl2, incl2], axis=1), g[L2:, :], 0.0), gram)
    tinv = _each(lambda n: eye2 + n, nmat)
    pw = _each(lambda n: _nn(n, n), nmat)
    for lev in range(1, n_levels):
        if lev < n_levels - 1:
            both = _each(lambda p, t: _nn(p, jnp.concatenate([p, t], axis=1)), pw, tinv)
            pw = _each(lambda x: x[:, :L2], both)
            tinv = _each(lambda t, x: t + x[:, L2:], tinv, both)
        else:
            tinv = _each(lambda p, t: t + _nn(p, t), pw, tinv)
    akv = _each(_nn, a_k, vs)
    wu = _each(lambda t, p, q: _nn(t, jnp.concatenate([p, q], axis=1)), tinv, at, akv)
    zs = _each(lambda x, y: jnp.concatenate([x, jnp.concatenate([jnp.zeros_like(y), y], axis=1)], axis=0), wu, vs)
    qy = _each(_nn, r_bk, zs)
    md = _each(lambda x, y, e, z: _tn(jnp.concatenate([stack(x * e), stack(y * e)], axis=0), z),
               b, k, e_end, zs)
    dec = _each(lambda ce: jnp.sum(jnp.where(eye_w, jnp.broadcast_to(jnp.exp(ce), (W, W)), 0.0),
                                   axis=1, keepdims=True), cum_end)

    def head_mean(x):
        m0 = jnp.sum(jnp.where(head0, x, 0.0), axis=-1, keepdims=True)
        m1 = jnp.sum(jnp.where(head0, 0.0, x), axis=-1, keepdims=True)
        return jnp.where(head0, m0, m1) * (1.0 / D_HEAD_DIM)

    hs = [h_ref[p] for p in range(npair)]
    for i in range(nck):
        for p in range(npair):
            s = p * nck + i
            sl, lanes = where[s]
            ys = _nn(rt[s] + qy[s][:, :W], hs[p]) + qy[s][:, W:]
            y = ys[:L, :] + ys[L:, :]
            hs[p] = dec[s] * hs[p] + _nn(md[s][:, :W], hs[p]) + md[s][:, W:]
            if d == 0:
                y_ref[sl, lanes] = y
            else:
                par = par_ref[:, lanes]
                tot = y0_ref[sl, lanes] + y
                cen = tot - head_mean(tot)
                nrm = cen * lax.rsqrt(head_mean(cen * cen) + RWKV_EPS)
                bonus = head_mean(r[s] * k[s] * par[0:1, :]) * float(D_HEAD_DIM) * v[s]
                y_ref[sl, lanes] = ((nrm * par[1:2, :] + par[2:3, :] + bonus) * g_ref[sl, lanes]).astype(y_ref.dtype)
    for p in range(npair):
        h_ref[p] = hs[p]


def _rwkv_scan(r, k, v, a, b, lw, g, par, n_ctx):
    bsz, t, c = r.shape
    L = RWKV_CHUNK
    tb = SCAN_TIME_BLOCK
    nck, nb, ncb = tb // L, t // tb, n_ctx // tb
    y0 = None
    for d in (0, 1):
        tix = lambda j, d=d: _time_index(d, j, ncb, nb)
        seq = pl.BlockSpec((None, tb, c), lambda i, j: (i, tix(j), 0))
        in_specs = [seq] * 5 + [pl.BlockSpec((None, None, tb, c), lambda i, j, d=d: (d, i, tix(j), 0))]
        args = [r, k, v, a, b, lw]
        if d == 1:
            in_specs += [seq, seq, pl.BlockSpec(par.shape, lambda i, j: (0, 0))]
            args += [y0, g, par]
        y0 = pl.pallas_call(
            functools.partial(_rwkv_kernel, L=L, nck=nck, d=d),
            grid=(bsz, nb),
            in_specs=in_specs,
            out_specs=seq,
            out_shape=jax.ShapeDtypeStruct((bsz, t, c), BF16 if d else F32),
            scratch_shapes=[pltpu.VMEM((c // (2 * D_HEAD_DIM), 2 * D_HEAD_DIM, 2 * D_HEAD_DIM), F32)],
            compiler_params=_scan_params(2),
            name="rwkv7_scan",
        )(*args)
    return y0


def _softplus(x):
    return jnp.maximum(x, 0.0) + jnp.log(1.0 + jnp.exp(-jnp.abs(x)))


def _segment_neighbours(x, before_tile, after_tile):
    n = x.shape[0]
    row = lax.broadcasted_iota(jnp.int32, x.shape, 0)
    prev = jnp.where(row == 0, before_tile, pltpu.roll(x, 1, axis=0))
    nxt = jnp.where(row == n - 1, after_tile, pltpu.roll(x, n - 1, axis=0))
    return prev, nxt


def _even_prep_kernel(p_ref, prev_ref, next_ref, ca_ref, cb_ref, sm_ref,
                      xs_ref, bm_ref, cm_ref, dt_ref, la_ref, q_ref, k_ref, g_ref, *, tiles_per_seq, ctx_tiles):
    j = pl.program_id(0) % tiles_per_seq
    first = jnp.logical_or(j == 0, j == ctx_tiles)
    last = jnp.logical_or(j == ctx_tiles - 1, j == tiles_per_seq - 1)

    def conv_silu(lo, width, taps_ref):
        x = p_ref[:, lo:lo + width].astype(F32)
        prev, nxt = _segment_neighbours(
            x, jnp.where(first, 0.0, prev_ref[HALO_ROWS - 1:HALO_ROWS, lo:lo + width].astype(F32)),
            jnp.where(last, 0.0, next_ref[0:1, lo:lo + width].astype(F32)))
        taps = taps_ref[...]
        return _silu(prev * taps[0:1, :] + x * taps[1:2, :] + nxt * taps[2:3, :] + taps[3:4, :])

    xbc = conv_silu(A_INNER, A_XBC, ca_ref)
    xs_ref[...] = xbc[:, :A_INNER]
    bm_ref[...] = xbc[:, A_INNER:A_INNER + A_GROUPS * A_STATE]
    cm_ref[...] = xbc[:, A_INNER + A_GROUPS * A_STATE:]
    qk = conv_silu(EV_B0, 2 * B_QK, cb_ref)
    q_ref[...] = qk[:, :B_QK]
    k_ref[...] = qk[:, B_QK:]
    sm = sm_ref[...]
    nh2 = 2 * A_HEADS
    dt = _softplus(p_ref[:, A_INNER + A_XBC:A_INNER + A_XBC + nh2].astype(F32) + sm[0:1, :])
    la = dt * sm[1:2, :]
    g0 = EV_B0 + 2 * B_QK + 2 * B_INNER
    gx = p_ref[:, g0:g0 + 4 * B_HEADS].astype(F32)
    g = jnp.where(sm[4:5, :] > 0.5, -_softplus(-(gx + sm[3:4, :])), gx + sm[2:3, :])
    for d in range(2):
        dt_ref[d] = dt[:, d * A_HEADS:(d + 1) * A_HEADS]
        la_ref[d] = la[:, d * A_HEADS:(d + 1) * A_HEADS]
        g_ref[d] = g[:, d * 2 * B_HEADS:(d + 1) * 2 * B_HEADS]


def _gate_lane_order(a):
    h = B_HEADS
    return jnp.concatenate([a[..., :-4 * h], a[..., -4 * h:-3 * h], a[..., -2 * h:-h], a[..., -3 * h:-2 * h], a[..., -h:]],
                           axis=-1)


def _even_mixers(p, n_ctx, ssd_params, mlstm_params):
    conv_w, conv_b, dt_bias, a_log, d_skip, norm_a = ssd_params
    conv_bw, conv_bb, i_bias, f_bias, norm_b = mlstm_params
    b, t, n_pad = p.shape
    p2 = p.reshape(b * t, n_pad)
    tiles = b * t // ROW_TILE
    halo = ROW_TILE // HALO_ROWS
    zeros = jnp.zeros((B_HEADS,), F32)
    sm = jnp.stack([dt_bias.reshape(-1), -jnp.exp(a_log).reshape(-1),
                    jnp.concatenate([i_bias[0], zeros, i_bias[1], zeros]),
                    jnp.concatenate([zeros, f_bias[0], zeros, f_bias[1]]),
                    jnp.concatenate([zeros, zeros + 1.0, zeros, zeros + 1.0])])
    ca = jnp.concatenate([conv_w, conv_b[None, :]], axis=0)
    cb = jnp.concatenate([conv_bw, conv_bb[None, :]], axis=0)
    whole = lambda a: pl.BlockSpec(a.shape, lambda i: (0,) * a.ndim)
    rows = lambda c: pl.BlockSpec((ROW_TILE, c), lambda i: (i, 0))
    per_dir = lambda c: pl.BlockSpec((2, ROW_TILE, c), lambda i: (0, i, 0))
    f32 = lambda *s: jax.ShapeDtypeStruct(s, F32)
    n = b * t
    gn = A_GROUPS * A_STATE
    xs, bm, cm, dt, la, q, k, g = pl.pallas_call(
        functools.partial(_even_prep_kernel, tiles_per_seq=t // ROW_TILE, ctx_tiles=n_ctx // ROW_TILE),
        grid=(tiles,),
        in_specs=[pl.BlockSpec((ROW_TILE, n_pad), lambda i: (i, 0)),
                  pl.BlockSpec((HALO_ROWS, n_pad), lambda i: (jnp.maximum(i * halo - 1, 0), 0)),
                  pl.BlockSpec((HALO_ROWS, n_pad), lambda i: (jnp.minimum((i + 1) * halo, tiles * halo - 1), 0)),
                  whole(ca), whole(cb), whole(sm)],
        out_specs=[rows(A_INNER), rows(gn), rows(gn), per_dir(A_HEADS), per_dir(A_HEADS),
                   rows(B_QK), rows(B_QK), per_dir(2 * B_HEADS)],
        out_shape=[f32(n, A_INNER), f32(n, gn), f32(n, gn), f32(2, n, A_HEADS), f32(2, n, A_HEADS),
                   f32(n, B_QK), f32(n, B_QK), f32(2, n, 2 * B_HEADS)],
        compiler_params=pltpu.CompilerParams(dimension_semantics=("arbitrary",), vmem_limit_bytes=VMEM_LIMIT_BYTES),
        name="even_prep",
    )(p2, p2, p2, ca, cb, sm)
    seq = lambda a: a.reshape(b, t, a.shape[-1])
    seq_d = lambda a: a.reshape(2, b, t, a.shape[-1])
    par = jnp.stack([jnp.repeat(d_skip, A_HEAD_DIM), norm_a])
    fa = _ssd_scan(seq(xs), seq_d(dt), seq_d(la), seq(bm), seq(cm), p, par, n_ctx)
    fb = _mlstm_scan(seq(q), seq(k), seq_d(g), p, norm_b[None, :], n_ctx)
    return fa, fb


def _hgrn2_mixer(p, n_ctx, lb, params):
    f_bias, norm_w = params
    return _gla_scan(p, lb, f_bias, norm_w, n_ctx)


def _rwkv_prep_kernel(p_ref, prev_ref, next_ref, mu_ref, w2_ref, a2_ref, g2_ref, vec_ref,
                      r_ref, k_ref, v_ref, a_ref, b_ref, g_ref, lw_ref, *, tiles_per_seq, ctx_tiles):
    c = D_INNER
    j = pl.program_id(0) % tiles_per_seq
    x = p_ref[:, P_C:].astype(F32)
    n = x.shape[0]
    first = jnp.logical_or(j == 0, j == ctx_tiles)
    last = jnp.logical_or(j == ctx_tiles - 1, j == tiles_per_seq - 1)
    prev, nxt = _segment_neighbours(x, jnp.where(first, 0.0, prev_ref[HALO_ROWS - 1:HALO_ROWS, P_C:].astype(F32)),
                                    jnp.where(last, 0.0, next_ref[0:1, P_C:].astype(F32)))
    x = x + mu_ref[...] * (0.5 * (prev + nxt) - x)
    r, k, v = x[:, :c], x[:, c:2 * c], x[:, 2 * c:3 * c]
    o = 3 * c
    wl = jnp.tanh(x[:, o:o + 2 * D_W_LORA])
    gl = x[:, o + 2 * D_W_LORA:o + 2 * D_W_LORA + D_G_LORA]
    al = x[:, o + 2 * D_W_LORA + D_G_LORA:o + 2 * D_W_LORA + D_G_LORA + D_A_LORA]
    vec = vec_ref[...]
    for d in range(2):
        w = vec[d:d + 1, :] + _nn(wl, w2_ref[d])
        lw_ref[d] = -jnp.exp(-_softplus(-w) - 0.5)
    a = jax.nn.sigmoid(vec[2:3, :] + _nn(al, a2_ref[...]))
    g_ref[...] = _nn(jax.nn.sigmoid(gl), g2_ref[...])
    kx = k * vec[3:4, :]
    sq = kx * kx
    head0 = (lax.broadcasted_iota(jnp.int32, (n, 2 * D_HEAD_DIM), 1) < D_HEAD_DIM)
    sums = []
    for s in range(c // (2 * D_HEAD_DIM)):
        blk = sq[:, s * 2 * D_HEAD_DIM:(s + 1) * 2 * D_HEAD_DIM]
        s0 = jnp.sum(jnp.where(head0, blk, 0.0), axis=-1, keepdims=True)
        s1 = jnp.sum(jnp.where(head0, 0.0, blk), axis=-1, keepdims=True)
        sums.append(jnp.where(head0, s0, s1))
    kk = kx * lax.rsqrt(jnp.maximum(jnp.concatenate(sums, axis=1), 1e-12))
    r_ref[...] = r
    k_ref[...] = k * (1.0 + (a - 1.0) * vec[4:5, :])
    v_ref[...] = v
    a_ref[...] = -kk
    b_ref[...] = kk * a


def _rwkv7_mixer(p, n_ctx, params):
    mu, w0, w2, a0, a2, g2, k_k, k_a, r_k, ln_w, ln_b = params
    b, t, n_pad = p.shape
    c = D_INNER
    width = n_pad - P_C
    p2 = p.reshape(b * t, n_pad)
    tiles = b * t // ROW_TILE
    halo = ROW_TILE // HALO_ROWS
    w2z = jnp.zeros((2, 2 * D_W_LORA, c), F32)
    w2z = w2z.at[0, :D_W_LORA].set(w2[0]).at[1, D_W_LORA:].set(w2[1])
    vec = jnp.stack([w0[0], w0[1], a0, k_k, k_a])
    whole = lambda a: pl.BlockSpec(a.shape, lambda i: (0,) * a.ndim)
    rows = pl.BlockSpec((ROW_TILE, c), lambda i: (i, 0))
    mu_pad = jnp.pad(mu, (0, width - mu.shape[0]))[None, :]
    outs = pl.pallas_call(
        functools.partial(_rwkv_prep_kernel, tiles_per_seq=t // ROW_TILE, ctx_tiles=n_ctx // ROW_TILE),
        grid=(tiles,),
        in_specs=[pl.BlockSpec((ROW_TILE, n_pad), lambda i: (i, 0)),
                  pl.BlockSpec((HALO_ROWS, n_pad), lambda i: (jnp.maximum(i * halo - 1, 0), 0)),
                  pl.BlockSpec((HALO_ROWS, n_pad), lambda i: (jnp.minimum((i + 1) * halo, tiles * halo - 1), 0)),
                  whole(mu_pad), whole(w2z), whole(a2), whole(g2), whole(vec)],
        out_specs=[rows] * 6 + [pl.BlockSpec((2, ROW_TILE, c), lambda i: (0, i, 0))],
        out_shape=[jax.ShapeDtypeStruct((b * t, c), F32)] * 6 + [jax.ShapeDtypeStruct((2, b * t, c), F32)],
        compiler_params=pltpu.CompilerParams(dimension_semantics=("arbitrary",), vmem_limit_bytes=VMEM_LIMIT_BYTES),
        name="rwkv7_prep",
    )(p2, p2, p2, mu_pad, w2z, a2, g2, vec)
    r, k, v, a, bb, g = (u.reshape(b, t, c) for u in outs[:6])
    par = jnp.stack([r_k.reshape(D_INNER), ln_w, ln_b])
    return _rwkv_scan(r, k, v, a, bb, outs[6].reshape(2, b, t, c), g, par, n_ctx)


def _to_col_major(u, rows):
    b, s, d = u.shape
    return u.reshape(b, rows, GRID_W, d).transpose(0, 2, 1, 3).reshape(b, s, d)


def _from_col_major(u, rows):
    b, s, d = u.shape
    return u.reshape(b, GRID_W, rows, d).transpose(0, 2, 1, 3).reshape(b, s, d)


def _tile_specs(bsz, t, n_ctx):
    tiles_per_seq, ctx_tiles = t // ROW_TILE, n_ctx // ROW_TILE
    mod_row = lambda i: jnp.where(i % tiles_per_seq < ctx_tiles, bsz, i // tiles_per_seq)
    rows = lambda c: pl.BlockSpec((ROW_TILE, c), lambda i: (i, 0))
    whole = lambda a: pl.BlockSpec(a.shape, lambda i: (0,) * a.ndim)
    mod = pl.BlockSpec((None, 6, D_MODEL), lambda i: (mod_row(i), 0, 0))
    params = pltpu.CompilerParams(dimension_semantics=("arbitrary",), vmem_limit_bytes=VMEM_LIMIT_BYTES)
    return rows, whole, mod, params


def _in_proj_kernel(x_ref, mod_ref, w_ref, o_ref):
    m = mod_ref[...]
    o_ref[...] = _nn(x_ref[...] * (1.0 + m[1:2, :]) + m[0:1, :], w_ref[...]).astype(o_ref.dtype)


def _in_proj(xa, mods, w, bsz, n_ctx):
    n = w.shape[1]
    n_pad = -(-n // LANES) * LANES
    wb = jnp.pad(w.astype(BF16), ((0, 0), (0, n_pad - n)))
    rows, whole, mod, params = _tile_specs(bsz, xa.shape[0] // bsz, n_ctx)
    return pl.pallas_call(
        _in_proj_kernel,
        grid=(xa.shape[0] // ROW_TILE,),
        in_specs=[rows(D_MODEL), mod, whole(wb)],
        out_specs=rows(n_pad),
        out_shape=jax.ShapeDtypeStruct((xa.shape[0], n_pad), BF16),
        compiler_params=params,
        name="in_proj",
    )(xa, mods, wb)


def _norm_rows(z, ln):
    mu = jnp.mean(z, axis=-1, keepdims=True)
    zc = z - mu
    var = jnp.mean(zc * zc, axis=-1, keepdims=True)
    return zc * lax.rsqrt(var + LN_EPS) * ln[0:1, :] + ln[1:2, :]


def _out_proj_kernel(fa_ref, fb_ref, w_ref, x_ref, mod_ref, ln_ref, xo_ref, h_ref, hb_ref):
    ka = fa_ref.shape[1]
    m = mod_ref[...]
    y = _nn(fa_ref[...], w_ref[:ka, :]) + _nn(fb_ref[...], w_ref[ka:, :])
    xn = _norm_rows(DEEPNORM_ALPHA * x_ref[...] + m[2:3, :] * y, ln_ref[...])
    xo_ref[...] = xn
    h = xn * (1.0 + m[4:5, :]) + m[3:4, :]
    h_ref[...] = h
    hb_ref[...] = h.astype(BF16)


def _out_proj(fa, fb, w, xa, mods, ln, bsz, n_ctx):
    t_all, d = xa.shape
    rows, whole, mod, params = _tile_specs(bsz, t_all // bsz, n_ctx)
    wb = w.astype(BF16)
    return pl.pallas_call(
        _out_proj_kernel,
        grid=(t_all // ROW_TILE,),
        in_specs=[rows(fa.shape[1]), rows(fb.shape[1]), whole(wb), rows(d), mod, whole(ln)],
        out_specs=[rows(d), rows(d), rows(d)],
        out_shape=[jax.ShapeDtypeStruct((t_all, d), F32), jax.ShapeDtypeStruct((t_all, d), F32),
                   jax.ShapeDtypeStruct((t_all, d), BF16)],
        compiler_params=params,
        name="out_proj_norm",
    )(fa, fb, wb, xa, mods, ln)


def _ffn_norm_kernel(*refs):
    y_refs, (w_ref, x_ref, mod_ref, ln_ref, xo_ref) = refs[:TOP_K], refs[TOP_K:]
    m = mod_ref[...]
    w = w_ref[...]
    f = functools.reduce(jnp.add, [y_refs[kk][...].astype(F32) * w[:, kk:kk + 1] for kk in range(TOP_K)])
    xo_ref[...] = _norm_rows(DEEPNORM_ALPHA * x_ref[...] + m[5:6, :] * f, ln_ref[...])


def _ffn_norm(ys, wts, xa, mods, ln, bsz, n_ctx):
    t_all, d = xa.shape
    rows, whole, mod, params = _tile_specs(bsz, t_all // bsz, n_ctx)
    return pl.pallas_call(
        _ffn_norm_kernel,
        grid=(t_all // ROW_TILE,),
        in_specs=[rows(d)] * TOP_K + [rows(TOP_K), rows(d), mod, whole(ln)],
        out_specs=rows(d),
        out_shape=jax.ShapeDtypeStruct((t_all, d), F32),
        compiler_params=params,
        name="ffn_residual_norm",
    )(*ys, wts, xa, mods, ln)


def kernel(x, c, ctx, c_ctx, mod_w, mod_b, ln_g, ln_b, ev_w_in, ev_w_out, ssd_conv_w, ssd_conv_b, ssd_dt_bias, ssd_a_log, ssd_d, ssd_norm_w, mlstm_conv_w, mlstm_conv_b, mlstm_i_bias, mlstm_f_bias, mlstm_norm_w, od_w_in, od_w_out, hgrn_lb_logits, hgrn_f_bias, hgrn_norm_w, rwkv_mu, rwkv_w0, rwkv_w2, rwkv_a0, rwkv_a2, rwkv_g2, rwkv_k_k, rwkv_k_a, rwkv_r_k, rwkv_ln_w, rwkv_ln_b, router_w, router_bias, exp_w_gate, exp_w_up, exp_w_down):
    bsz, seq, _ = x.shape
    n_ctx = ctx.shape[1]
    rows = seq // GRID_W
    lb_all = jnp.cumsum(jax.nn.softmax(hgrn_lb_logits.astype(F32), axis=0), axis=0)
    lb_all = lb_all - lb_all[0]
    s_c = jax.nn.silu(c)
    s_cc = jax.nn.silu(c_ctx)
    t = n_ctx + seq
    xa = jnp.concatenate([ctx, x], axis=1).reshape(bsz * t, D_MODEL)
    seq3 = lambda a: a.reshape(bsz, t, a.shape[-1])
    flat = lambda a: a.reshape(bsz * t, a.shape[-1])
    lat_order = lambda a, f: flat(jnp.concatenate([seq3(a)[:, :n_ctx], f(seq3(a)[:, n_ctx:], rows)], axis=1))
    for layer in range(DEPTH):
        i = layer // 2
        mods = _matmul(jnp.concatenate([s_c, s_cc[None]], axis=0), mod_w[layer], tm=8, tn=512) + mod_b[layer]
        mods = mods.reshape(bsz + 1, 6, D_MODEL)
        ln = jnp.stack([ln_g[layer], ln_b[layer]], axis=1)
        if layer % 2 == 0:
            w_in = jnp.concatenate([ev_w_in[i][:, :P_A], jnp.zeros((D_MODEL, EV_B0 - P_A), F32),
                                    _gate_lane_order(ev_w_in[i][:, P_A:])], axis=1)
            p = seq3(_in_proj(xa, mods, w_in, bsz, n_ctx))
            fa, fb = _even_mixers(
                p, n_ctx,
                (ssd_conv_w[i], ssd_conv_b[i], ssd_dt_bias[i], ssd_a_log[i], ssd_d[i], ssd_norm_w[i]),
                (mlstm_conv_w[i], mlstm_conv_b[i], mlstm_i_bias[i], mlstm_f_bias[i], mlstm_norm_w[i]))
            fa, fb, w_out = flat(fa), flat(fb), ev_w_out[i]
        else:
            g0, g1 = 3 * D_INNER + 2 * D_W_LORA + D_A_LORA, P_D
            lora_last = lambda a: jnp.concatenate([a[..., :g0 - D_A_LORA], a[..., g0:g1], a[..., g0 - D_A_LORA:g0]], axis=-1)
            w_in = jnp.concatenate([od_w_in[i][:, :P_C], lora_last(od_w_in[i][:, P_C:])], axis=1)
            p = seq3(_in_proj(lat_order(xa, _to_col_major), mods, w_in, bsz, n_ctx))
            fa = _hgrn2_mixer(p, n_ctx, lb_all[layer], (hgrn_f_bias[i], hgrn_norm_w[i]))
            fb = _rwkv7_mixer(p, n_ctx,
                              (lora_last(rwkv_mu[i]), rwkv_w0[i], rwkv_w2[i], rwkv_a0[i], rwkv_a2[i], rwkv_g2[i],
                               rwkv_k_k[i], rwkv_k_a[i], rwkv_r_k[i], rwkv_ln_w[i], rwkv_ln_b[i]))
            fa, fb, w_out = lat_order(fa, _from_col_major), lat_order(fb, _from_col_major), od_w_out[i]
        xa, h, hb = _out_proj(fa, fb, w_out, xa, mods, ln[0], bsz, n_ctx)
        ys, wts = _moe_ffn(h, hb, router_w, router_bias, exp_w_gate, exp_w_up, exp_w_down, layer)
        xa = _ffn_norm(ys, wts, xa, mods, ln[1], bsz, n_ctx)
    return seq3(xa)[:, n_ctx:]
```

```python
import functools
import math

import jax
import jax.numpy as jnp
from jax import lax
from jax.experimental import pallas as pl
from jax.experimental.pallas import tpu as pltpu

F32 = jnp.float32
BF16 = jnp.bfloat16

D_MODEL = 1024
DEPTH = 4
GRID_W = 64
A_HEADS = 8
A_HEAD_DIM = 64
A_INNER = A_HEADS * A_HEAD_DIM
A_GROUPS = 2
A_STATE = 64
A_XBC = A_INNER + 2 * A_GROUPS * A_STATE
B_HEADS = 4
B_QK_DIM = 64
B_V_DIM = 128
B_QK = B_HEADS * B_QK_DIM
B_INNER = B_HEADS * B_V_DIM
MLSTM_EPS = 1e-6
C_HEADS = 4
C_HEAD_DIM = 128
C_INNER = C_HEADS * C_HEAD_DIM
D_HEADS = 8
D_HEAD_DIM = 64
D_INNER = D_HEADS * D_HEAD_DIM
D_W_LORA = 64
D_A_LORA = 64
D_G_LORA = 128
RWKV_EPS = 64e-5
P_A = A_INNER + A_XBC + 2 * A_HEADS
P_B = 2 * B_QK + 2 * B_INNER + 4 * B_HEADS
P_C = 5 * C_INNER
P_D = 3 * D_INNER + 2 * D_W_LORA + D_A_LORA + D_G_LORA
EV_B0 = 3 * A_INNER
N_EXPERTS = 32
N_EXPERT_GROUPS = 8
EXPERTS_PER_GROUP = N_EXPERTS // N_EXPERT_GROUPS
TOP_K = 2
D_EXPERT = 512
MOE_BLOCK = 512
RANK_BLOCK = 512
ROW_TILE = 256
HALO_ROWS = 16
LANES = 128
DEEPNORM_ALPHA = (2 * DEPTH) ** 0.25
LN_EPS = 1e-5
M_INIT = -1e30
NEG_BIG = -1e30

SSD_CHUNK = 128
MLSTM_CHUNK = 128
GLA_CHUNK = 16
GLA_BLOCK = 64
RWKV_CHUNK = 64
SCAN_TIME_BLOCK = 256

VMEM_LIMIT_BYTES = 48 * 1024 * 1024
HI = lax.Precision.HIGHEST


def _dot(a, b, dims, exact):
    if exact:
        return lax.dot_general(a.astype(F32), b.astype(F32), (dims, ((), ())),
                               precision=HI, preferred_element_type=F32)
    return lax.dot_general(a.astype(BF16), b.astype(BF16), (dims, ((), ())),
                           preferred_element_type=F32)


def _nn(a, b, exact=False):
    return _dot(a, b, ((1,), (0,)), exact)


def _nt(a, b, exact=False):
    return _dot(a, b, ((1,), (1,)), exact)


def _tn(a, b, exact=False):
    return _dot(a, b, ((0,), (0,)), exact)


def _iota2(n, m):
    return (lax.broadcasted_iota(jnp.int32, (n, m), 0),
            lax.broadcasted_iota(jnp.int32, (n, m), 1))


def _split3(x):
    x1 = x.astype(BF16)
    r1 = x - x1.astype(F32)
    x2 = r1.astype(BF16)
    x3 = (r1 - x2.astype(F32)).astype(BF16)
    return x1, x2, x3


def _mask_nn(mask, x):
    mb = mask.astype(BF16)
    x1, x2, x3 = _split3(x)
    return _nn(mb, x1) + _nn(mb, x2) + _nn(mb, x3)


def _nn_mask(x, mask):
    mb = mask.astype(BF16)
    x1, x2, x3 = _split3(x)
    return _nn(x1, mb) + _nn(x2, mb) + _nn(x3, mb)


def _each(f, *cols):
    return [f(*xs) for xs in zip(*cols)]


def _silu(x):
    return x * jax.nn.sigmoid(x)


def _mm_kernel(x_ref, w_ref, o_ref, *, exact):
    o_ref[...] = _nn(x_ref[...], w_ref[...], exact)


def _matmul(x, w, tm=512, tn=512, exact=False):
    m, k = x.shape
    n = w.shape[1]
    n_pad = -(-n // tn) * tn
    m_pad = -(-m // tm) * tm
    xb = x if exact else x.astype(BF16)
    wb = w if exact else w.astype(BF16)
    if n_pad != n:
        wb = jnp.pad(wb, ((0, 0), (0, n_pad - n)))
    if m_pad != m:
        xb = jnp.pad(xb, ((0, m_pad - m), (0, 0)))
    out = pl.pallas_call(
        functools.partial(_mm_kernel, exact=exact),
        grid=(n_pad // tn, m_pad // tm),
        in_specs=[pl.BlockSpec((tm, k), lambda j, i: (i, 0)),
                  pl.BlockSpec((k, tn), lambda j, i: (0, j))],
        out_specs=pl.BlockSpec((tm, tn), lambda j, i: (i, j)),
        out_shape=jax.ShapeDtypeStruct((m_pad, n_pad), F32),
        compiler_params=pltpu.CompilerParams(
            dimension_semantics=("arbitrary", "arbitrary"),
            vmem_limit_bytes=VMEM_LIMIT_BYTES),
        name="dense_matmul",
    )(xb, wb)
    return out[:m, :n]


def _moe_kernel(blk_e_ref, n_used_ref, x_ref, wg_ref, wu_ref, wd_ref, o_ref):
    i = pl.program_id(0)

    @pl.when(i < n_used_ref[0])
    def _():
        x = x_ref[...]
        g = _nn(x, wg_ref[...])
        u = _nn(x, wu_ref[...])
        o_ref[...] = _nn(g * jax.nn.sigmoid(g) * u, wd_ref[...]).astype(o_ref.dtype)

    @pl.when(i >= n_used_ref[0])
    def _():
        o_ref[...] = jnp.zeros_like(o_ref)


def _moe_experts(xp, blk_e, n_used, w_gate, w_up, w_down, layer):
    n_rows, d = xp.shape
    n_blocks = n_rows // MOE_BLOCK
    grid_spec = pltpu.PrefetchScalarGridSpec(
        num_scalar_prefetch=2,
        grid=(n_blocks,),
        in_specs=[
            pl.BlockSpec((MOE_BLOCK, d), lambda i, be, nu: (i, 0)),
            pl.BlockSpec((None, None, d, D_EXPERT), lambda i, be, nu: (layer, be[i], 0, 0)),
            pl.BlockSpec((None, None, d, D_EXPERT), lambda i, be, nu: (layer, be[i], 0, 0)),
            pl.BlockSpec((None, None, D_EXPERT, d), lambda i, be, nu: (layer, be[i], 0, 0)),
        ],
        out_specs=pl.BlockSpec((MOE_BLOCK, d), lambda i, be, nu: (i, 0)),
    )
    return pl.pallas_call(
        _moe_kernel,
        grid_spec=grid_spec,
        out_shape=jax.ShapeDtypeStruct((n_rows, d), BF16),
        compiler_params=pltpu.CompilerParams(
            dimension_semantics=("arbitrary",),
            vmem_limit_bytes=VMEM_LIMIT_BYTES),
        name="moe_experts",
    )(blk_e, n_used, xp, w_gate, w_up, w_down)


def _top2(vals):
    m = len(vals)
    m1 = functools.reduce(jnp.maximum, vals)
    i1 = jnp.full_like(m1, float(m - 1))
    for j in reversed(range(m - 1)):
        i1 = jnp.where(vals[j] == m1, float(j), i1)
    rest = [jnp.where(i1 == float(j), -jnp.inf, vals[j]) for j in range(m)]
    m2 = functools.reduce(jnp.maximum, rest)
    i2 = jnp.full_like(m1, float(m - 1))
    for j in reversed(range(m - 1)):
        i2 = jnp.where(rest[j] == m2, float(j), i2)
    return m1, i1, m2, i2


def _router_kernel(h_ref, wt_ref, bias_ref, e_ref, w_ref, rank_ref, cnt_ref, carry_ref):
    tm = h_ref.shape[0]
    ng, per = N_EXPERT_GROUPS, EXPERTS_PER_GROUP

    @pl.when(pl.program_id(0) == 0)
    def _():
        carry_ref[...] = jnp.zeros_like(carry_ref)

    s = jax.nn.sigmoid(_nt(wt_ref[...], h_ref[...], True))
    sb = s + bias_ref[...]
    biased = [sb[j * ng:(j + 1) * ng, :] for j in range(per)]
    plain = [s[j * ng:(j + 1) * ng, :] for j in range(per)]
    m1, _, m2, _ = _top2(biased)
    gsum = m1 + m2
    rows = lax.broadcasted_iota(jnp.int32, (ng, tm), 0).astype(F32)
    gmax = jnp.max(gsum, axis=0, keepdims=True)
    gi = jnp.min(jnp.where(gsum == gmax, rows, float(ng)), axis=0, keepdims=True)
    sel = rows == gi
    pick = lambda v: jnp.sum(jnp.where(sel, v, 0.0), axis=0, keepdims=True)
    in_b = [pick(v) for v in biased]
    in_s = [pick(v) for v in plain]
    _, l1, _, l2 = _top2(in_b)
    w1 = functools.reduce(jnp.add, [jnp.where(l1 == float(j), in_s[j], 0.0) for j in range(per)])
    w2 = functools.reduce(jnp.add, [jnp.where(l2 == float(j), in_s[j], 0.0) for j in range(per)])
    e1 = gi * float(per) + l1
    e2 = gi * float(per) + l2
    wsum = w1 + w2
    e_ref[0:1, :] = e1.astype(jnp.int32)
    e_ref[1:2, :] = e2.astype(jnp.int32)
    w_ref[0:1, :] = w1 / wsum
    w_ref[1:2, :] = w2 / wsum
    row = lax.broadcasted_iota(jnp.int32, (N_EXPERTS, tm), 0)
    experts = ((row % ng) * per + row // ng).astype(F32)
    oh1 = jnp.where(experts == e1, 1.0, 0.0)
    oh2 = jnp.where(experts == e2, 1.0, 0.0)
    oh = oh1 + oh2
    ri, ci = _iota2(tm, tm)
    seen = _nn(oh, jnp.where(ri < ci, 1.0, 0.0)) + carry_ref[...]
    rank_ref[0:1, :] = jnp.sum(oh1 * seen, axis=0, keepdims=True).astype(jnp.int32)
    rank_ref[1:2, :] = jnp.sum(oh2 * seen, axis=0, keepdims=True).astype(jnp.int32)
    carry = carry_ref[...] + jnp.sum(oh, axis=1, keepdims=True)
    carry_ref[...] = carry
    cnt_ref[...] = carry.astype(jnp.int32)


def _route(h, router_w, router_bias):
    t, d = h.shape
    tm = RANK_BLOCK
    kt = lambda dt: jax.ShapeDtypeStruct((TOP_K, t), dt)
    blk = pl.BlockSpec((TOP_K, tm), lambda i: (0, i))
    member_major = lambda a: a.reshape(N_EXPERT_GROUPS, EXPERTS_PER_GROUP, -1).transpose(1, 0, 2).reshape(N_EXPERTS, -1)
    e, w, rank, counts = pl.pallas_call(
        _router_kernel,
        grid=(t // tm,),
        in_specs=[pl.BlockSpec((tm, d), lambda i: (i, 0)),
                  pl.BlockSpec((N_EXPERTS, d), lambda i: (0, 0)),
                  pl.BlockSpec((N_EXPERTS, 1), lambda i: (0, 0))],
        out_specs=[blk, blk, blk, pl.BlockSpec((N_EXPERTS, 1), lambda i: (0, 0))],
        out_shape=[kt(jnp.int32), kt(F32), kt(jnp.int32), jax.ShapeDtypeStruct((N_EXPERTS, 1), jnp.int32)],
        scratch_shapes=[pltpu.VMEM((N_EXPERTS, 1), F32)],
        compiler_params=pltpu.CompilerParams(dimension_semantics=("arbitrary",),
                                             vmem_limit_bytes=VMEM_LIMIT_BYTES),
        name="moe_router",
    )(h, member_major(router_w.T), member_major(router_bias.astype(F32).reshape(N_EXPERTS, 1)))
    counts = counts.reshape(EXPERTS_PER_GROUP, N_EXPERT_GROUPS).T.reshape(N_EXPERTS)
    return e, w, rank, counts


def _moe_ffn(h, hb, router_w, router_bias, w_gate, w_up, w_down, layer):
    t, d = h.shape
    expert, wts, rank, counts = _route(h, router_w, router_bias)
    n_assign = t * TOP_K
    padded = (counts + MOE_BLOCK - 1) // MOE_BLOCK * MOE_BLOCK
    pends = jnp.cumsum(padded)
    pstarts = pends - padded
    start_of = jnp.sum(jnp.where(expert[..., None] == jnp.arange(N_EXPERTS, dtype=jnp.int32), pstarts, 0), axis=-1)
    dest = start_of + rank
    n_blocks = -(-n_assign // MOE_BLOCK) + N_EXPERTS
    slot_token = jnp.arange(n_blocks * MOE_BLOCK, dtype=jnp.int32) % t
    slot_token = slot_token.at[dest.reshape(-1)].set(jnp.tile(jnp.arange(t, dtype=jnp.int32), TOP_K),
                                                      unique_indices=True)
    xp = hb[slot_token]
    blk_start = jnp.arange(n_blocks, dtype=jnp.int32) * MOE_BLOCK
    blk_e = jnp.minimum(jnp.sum(pends[None, :] <= blk_start[:, None], axis=1), N_EXPERTS - 1).astype(jnp.int32)
    n_used = (pends[-1] // MOE_BLOCK).astype(jnp.int32).reshape(1)
    yp = _moe_experts(xp, blk_e, n_used, w_gate, w_up, w_down, layer)
    return [yp[dest[kk]] for kk in range(TOP_K)], wts.T


def _time_index(d, j, n_ctx_blocks, n_blocks):
    if d == 0:
        return j
    return jnp.where(j < n_ctx_blocks, n_ctx_blocks - 1 - j, n_blocks - 1 - j + n_ctx_blocks)


def _end_row(x, d):
    n = x.shape[0]
    return x[0:1, :] if d == 1 else x[n - 1:n, :]


def _before(n, d):
    ri, ci = _iota2(n, n)
    return ci >= ri if d == 1 else ci <= ri


def _chunk_slices(nck, L, d):
    order = range(nck - 1, -1, -1) if d == 1 else range(nck)
    return [slice(i * L, (i + 1) * L) for i in order]


def _scan_params(n_axes):
    return pltpu.CompilerParams(dimension_semantics=("arbitrary",) * n_axes, vmem_limit_bytes=VMEM_LIMIT_BYTES)


def _ssd_kernel(*refs, L, d):
    if d == 0:
        x_ref, dt_ref, lac_ref, lar_ref, b_ref, c_ref, y_ref, st_ref = refs
    else:
        x_ref, dt_ref, lac_ref, lar_ref, b_ref, c_ref, y0_ref, z_ref, par_ref, y_ref, st_ref = refs
    hp, gn = A_INNER, A_GROUPS * A_STATE
    hpg = A_HEADS // A_GROUPS

    @pl.when(pl.program_id(1) == 0)
    def _():
        st_ref[...] = jnp.zeros_like(st_ref)

    before = _before(L, d)
    xs = x_ref[...]
    bm = b_ref[...]
    cm = c_ref[...]
    ccol = _mask_nn(before, lac_ref[...])
    crow = _nn_mask(lar_ref[...], _before(L, 1 - d))
    head_of_lane = lax.broadcasted_iota(jnp.int32, (A_HEADS, hp), 1) // A_HEAD_DIM
    expand = head_of_lane == lax.broadcasted_iota(jnp.int32, (A_HEADS, hp), 0)
    cum = _nn_mask(ccol, expand)
    x = xs * _nn_mask(dt_ref[...], expand)
    end = _end_row(cum, d)
    group_of_lane = lax.broadcasted_iota(jnp.int32, (L, gn), 1) // A_STATE
    first_of_pair = (lax.broadcasted_iota(jnp.int32, (L, 2 * A_HEAD_DIM), 1) < A_HEAD_DIM)
    cbs = [_nt(jnp.where(group_of_lane == g, cm, 0.0), bm) for g in range(A_GROUPS)]
    heads = list(range(A_HEADS))
    decay = [jnp.exp(jnp.where(before, ccol[:, h:h + 1] - crow[h:h + 1, :], NEG_BIG)) for h in heads]
    yh = [_nn(cbs[h // hpg] * decay[h], x[:, (h // 2) * 2 * A_HEAD_DIM:(h // 2 + 1) * 2 * A_HEAD_DIM]) for h in heads]
    pairs = [jnp.where(first_of_pair, yh[2 * p], yh[2 * p + 1]) for p in range(A_HEADS // 2)]
    st = st_ref[...]
    y = jnp.concatenate(pairs, axis=1) + jnp.exp(cum) * _nn(cm, st)
    own_group = (lax.broadcasted_iota(jnp.int32, (gn, hp), 0) // A_STATE
                 == lax.broadcasted_iota(jnp.int32, (gn, hp), 1) // (A_HEAD_DIM * hpg))
    st_ref[...] = jnp.exp(end) * st + jnp.where(own_group, _tn(bm, x * jnp.exp(end - cum)), 0.0)
    if d == 0:
        y_ref[...] = y
    else:
        par = par_ref[...]
        u = (y0_ref[...] + y + par[0:1, :] * xs) * _silu(z_ref[...].astype(F32))
        y_ref[...] = (u * lax.rsqrt(jnp.mean(u * u, axis=-1, keepdims=True) + 1e-6) * par[1:2, :]).astype(y_ref.dtype)


def _ssd_scan(xs, dt, la, bm, cm, p, par, n_ctx):
    b, t, hp = xs.shape
    L = SSD_CHUNK
    nb, ncb = t // L, n_ctx // L
    lar = jnp.swapaxes(la, 2, 3)
    y0 = None
    for d in (0, 1):
        tix = lambda j, d=d: _time_index(d, j, ncb, nb)
        seq = lambda c: pl.BlockSpec((None, L, c), lambda i, j: (i, tix(j), 0))
        in_specs = [seq(hp),
                    pl.BlockSpec((None, None, L, A_HEADS), lambda i, j, d=d: (d, i, tix(j), 0)),
                    pl.BlockSpec((None, None, L, A_HEADS), lambda i, j, d=d: (d, i, tix(j), 0)),
                    pl.BlockSpec((None, None, A_HEADS, L), lambda i, j, d=d: (d, i, 0, tix(j))),
                    seq(bm.shape[-1]), seq(bm.shape[-1])]
        args = [xs, dt, la, lar, bm, cm]
        if d == 1:
            in_specs += [seq(hp), seq(hp), pl.BlockSpec(par.shape, lambda i, j: (0, 0))]
            args += [y0, p, par]
        y0 = pl.pallas_call(
            functools.partial(_ssd_kernel, L=L, d=d),
            grid=(b, nb),
            in_specs=in_specs,
            out_specs=seq(hp),
            out_shape=jax.ShapeDtypeStruct((b, t, hp), BF16 if d else F32),
            scratch_shapes=[pltpu.VMEM((bm.shape[-1], hp), F32)],
            compiler_params=_scan_params(2),
            name="ssd_scan",
        )(*args)
    return y0


def _mlstm_kernel(*refs, L, d):
    if d == 0:
        q_ref, k_ref, v_ref, gc_ref, gr_ref, h_ref, c_ref, n_ref, m_ref = refs
    else:
        q_ref, k_ref, v_ref, gc_ref, gr_ref, h0_ref, o_ref, par_ref, h_ref, c_ref, n_ref, m_ref = refs
    nh, dk, dv = B_HEADS, B_QK_DIM, B_V_DIM

    @pl.when(pl.program_id(1) == 0)
    def _():
        c_ref[...] = jnp.zeros_like(c_ref)
        n_ref[...] = jnp.zeros_like(n_ref)
        m_ref[...] = jnp.full_like(m_ref, M_INIT)

    before = _before(L, d)
    gc = gc_ref[...]
    gr = gr_ref[...]
    fcol = _mask_nn(before, gc[:, nh:])
    frow = _nn_mask(gr[nh:, :], _before(L, 1 - d))
    lane_head = lax.broadcasted_iota(jnp.int32, (L, 2 * dk), 1) // dk
    heads = list(range(nh))
    slab = [slice((h // 2) * 2 * dk, (h // 2 + 1) * 2 * dk) for h in heads]
    lanes = [slice(h * dv, (h + 1) * dv) for h in heads]
    q = [jnp.where(lane_head == h % 2, q_ref[:, slab[h]], 0.0) * (dk ** -0.5) for h in heads]
    k = [k_ref[:, slab[h]] for h in heads]
    v = [v_ref[:, lanes[h]] for h in heads]
    li_c = [gc[:, h:h + 1] for h in heads]
    li_r = [gr[h:h + 1, :] for h in heads]
    f_c = [fcol[:, h:h + 1] for h in heads]
    f_r = [frow[h:h + 1, :] for h in heads]
    ftot = _each(lambda x: _end_row(x, d), f_c)
    c_prev = [c_ref[h] for h in heads]
    n_prev = [n_ref[h] for h in heads]
    m_prev = [m_ref[h] for h in heads]
    w_end = _each(lambda ft, fc, lc: ft - fc + lc, ftot, f_c, li_c)
    m_loc = _each(lambda w: jnp.max(w, axis=0, keepdims=True), w_end)
    ke = _each(lambda x, w, m: x * jnp.exp(w - m), k, w_end, m_loc)
    c_loc = _each(_tn, ke, v)
    n_loc = _each(lambda x: jnp.sum(x, axis=0, keepdims=True), ke)
    log_d = _each(lambda fc, fr, lr: jnp.where(before, fc - fr + lr, NEG_BIG), f_c, f_r, li_r)
    log_inter = _each(jnp.add, f_c, m_prev)
    m_row = _each(lambda ld, lint: jnp.maximum(jnp.max(ld, axis=-1, keepdims=True), lint), log_d, log_inter)
    s = _each(lambda a, b, ld, mr: _nt(a, b) * jnp.exp(ld - mr), q, k, log_d, m_row)
    inter = _each(lambda lint, mr: jnp.exp(lint - mr), log_inter, m_row)
    num = _each(lambda ss, vv, it, qq, cp: _nn(ss, vv) + it * _nn(qq, cp), s, v, inter, q, c_prev)
    den = _each(lambda ss, it, qq, npv: jnp.sum(ss, axis=-1, keepdims=True)
                + it * jnp.sum(qq * npv, axis=-1, keepdims=True), s, inter, q, n_prev)
    out = _each(lambda nu, de, mr: nu / jnp.maximum(jnp.abs(de), jnp.exp(-mr)), num, den, m_row)
    m_new = _each(lambda ft, mp, ml: jnp.maximum(ft + mp, ml), ftot, m_prev, m_loc)
    sp = _each(lambda ft, mp, mn: jnp.exp(ft + mp - mn), ftot, m_prev, m_new)
    sc = _each(lambda ml, mn: jnp.exp(ml - mn), m_loc, m_new)
    for h in heads:
        c_ref[h] = sp[h] * c_prev[h] + sc[h] * c_loc[h]
        n_ref[h] = sp[h] * n_prev[h] + sc[h] * n_loc[h]
        m_ref[h] = m_new[h]
        if d == 0:
            h_ref[:, lanes[h]] = out[h]
        else:
            tot = h0_ref[:, lanes[h]] + out[h]
            cen = tot - jnp.mean(tot, axis=-1, keepdims=True)
            nrm = cen * lax.rsqrt(jnp.mean(cen * cen, axis=-1, keepdims=True) + MLSTM_EPS)
            h_ref[:, lanes[h]] = (jax.nn.sigmoid(o_ref[:, lanes[h]].astype(F32)) * nrm * par_ref[:, lanes[h]]).astype(h_ref.dtype)


def _mlstm_scan(q, k, gates, p, par, n_ctx):
    b, t, _ = q.shape
    L = MLSTM_CHUNK
    nb, ncb = t // L, n_ctx // L
    v_blk, o_blk = (EV_B0 + 2 * B_QK) // B_INNER, (EV_B0 + 2 * B_QK + B_INNER) // B_INNER
    gates_r = jnp.swapaxes(gates, 2, 3)
    h0 = None
    for d in (0, 1):
        tix = lambda j, d=d: _time_index(d, j, ncb, nb)
        seq = lambda c, blk=0: pl.BlockSpec((None, L, c), lambda i, j: (i, tix(j), blk))
        in_specs = [seq(B_QK), seq(B_QK), seq(B_INNER, v_blk),
                    pl.BlockSpec((None, None, L, 2 * B_HEADS), lambda i, j, d=d: (d, i, tix(j), 0)),
                    pl.BlockSpec((None, None, 2 * B_HEADS, L), lambda i, j, d=d: (d, i, 0, tix(j)))]
        args = [q, k, p, gates, gates_r]
        if d == 1:
            in_specs += [seq(B_INNER), seq(B_INNER, o_blk), pl.BlockSpec(par.shape, lambda i, j: (0, 0))]
            args += [h0, p, par]
        h0 = pl.pallas_call(
            functools.partial(_mlstm_kernel, L=L, d=d),
            grid=(b, nb),
            in_specs=in_specs,
            out_specs=seq(B_INNER),
            out_shape=jax.ShapeDtypeStruct((b, t, B_INNER), BF16 if d else F32),
            scratch_shapes=[pltpu.VMEM((B_HEADS, 2 * B_QK_DIM, B_V_DIM), F32),
                            pltpu.VMEM((B_HEADS, 1, 2 * B_QK_DIM), F32),
                            pltpu.VMEM((B_HEADS, 1, 1), F32)],
            compiler_params=_scan_params(2),
            name="mlstm_scan",
        )(*args)
    return h0


def _gla_kernel(*refs, L, sub, nck, d):
    if d == 0:
        q_ref, f_ref, v_ref, par_ref, y_ref, st_ref = refs
    else:
        q_ref, f_ref, v_ref, par_ref, y0_ref, g_ref, y_ref, st_ref = refs
    hd = C_HEAD_DIM
    nhead = st_ref.shape[0]

    @pl.when(pl.program_id(1) == 0)
    def _():
        st_ref[...] = jnp.zeros_like(st_ref)

    before = _before(L, d)
    rows = lax.broadcasted_iota(jnp.int32, (L, 1), 0)
    slices = _chunk_slices(nck, L, d)
    where = [(sl, slice(h * hd, (h + 1) * hd)) for h in range(nhead) for sl in slices]
    lb = [par_ref[0:1, lanes] for _, lanes in where]
    f_pre = [f_ref[sl, lanes].astype(F32) + par_ref[1:2, lanes] for sl, lanes in where]
    q = [_silu(q_ref[sl, lanes].astype(F32)) for sl, lanes in where]
    v = [v_ref[sl, lanes] for sl, lanes in where]
    lf = _each(lambda b, x: jnp.log(b + (1.0 - b) * jax.nn.sigmoid(x)), lb, f_pre)
    k = _each(lambda b, x: (1.0 - b) * jax.nn.sigmoid(-x), lb, f_pre)
    lam = _each(lambda x: _mask_nn(before, x), lf)
    lam_end = _each(lambda x: _end_row(x, d), lam)
    blocks = [[] for _ in where]
    for c in range(L // sub):
        lo, hi = c * sub, (c + 1) * sub
        upto = rows >= lo if d == 1 else rows < hi
        for s in range(len(where)):
            edge = (hi, hi + 1) if d == 1 else (lo - 1, lo)
            ref = lam[s][edge[0]:edge[1], :] if 0 <= edge[0] < L else jnp.zeros_like(lam_end[s])
            qc = q[s][lo:hi, :] * jnp.exp(lam[s][lo:hi, :] - ref)
            kc = k[s] * jnp.exp(jnp.where(upto, ref - lam[s], NEG_BIG))
            blocks[s].append(_nt(qc, kc))
    att = _each(lambda bl: jnp.where(before, jnp.concatenate(bl, axis=0), 0.0), blocks)
    y_intra = _each(_nn, att, v)
    q_in = _each(lambda x, l: x * jnp.exp(l), q, lam)
    kv = _each(lambda x, y, l, le: _tn(x, y * jnp.exp(le - l)), v, k, lam, lam_end)
    dec = _each(jnp.exp, lam_end)
    for h in range(nhead):
        st = st_ref[h]
        for i in range(nck):
            s = h * nck + i
            sl, lanes = where[s]
            y = y_intra[s] + _nt(q_in[s], st)
            st = st * dec[s] + kv[s]
            if d == 0:
                y_ref[sl, lanes] = y
            else:
                tot = y0_ref[sl, lanes] + y
                nrm = tot * lax.rsqrt(jnp.mean(tot * tot, axis=-1, keepdims=True) + 1e-6)
                y_ref[sl, lanes] = (nrm * par_ref[2:3, lanes] * _silu(g_ref[sl, lanes].astype(F32))).astype(y_ref.dtype)
        st_ref[h] = st


def _gla_scan(p, lb, f_bias, norm_w, n_ctx):
    b, t, _ = p.shape
    c = C_INNER
    L = GLA_BLOCK
    tb = SCAN_TIME_BLOCK
    nck, nb, ncb = tb // L, t // tb, n_ctx // tb
    y0 = None
    for d in (0, 1):
        tix = lambda j, d=d: _time_index(d, j, ncb, nb)
        seq = lambda blk=0: pl.BlockSpec((None, tb, c), lambda i, j: (i, tix(j), blk))
        par = jnp.stack([jnp.broadcast_to(lb, (c,)), f_bias[d], norm_w])
        in_specs = [seq(0), seq(1 + d), seq(3), pl.BlockSpec(par.shape, lambda i, j: (0, 0))]
        args = [p, p, p, par]
        if d == 1:
            in_specs += [seq(), seq(4)]
            args += [y0, p]
        y0 = pl.pallas_call(
            functools.partial(_gla_kernel, L=L, sub=GLA_CHUNK, nck=nck, d=d),
            grid=(b, nb),
            in_specs=in_specs,
            out_specs=seq(),
            out_shape=jax.ShapeDtypeStruct((b, t, c), BF16 if d else F32),
            scratch_shapes=[pltpu.VMEM((c // C_HEAD_DIM, C_HEAD_DIM, C_HEAD_DIM), F32)],
            compiler_params=_scan_params(2),
            name="gla_scan",
        )(*args)
    return y0


def _rwkv_kernel(*refs, L, nck, d):
    if d == 0:
        r_ref, k_ref, v_ref, a_ref, b_ref, lw_ref, y_ref, h_ref = refs
    else:
        r_ref, k_ref, v_ref, a_ref, b_ref, lw_ref, y0_ref, g_ref, par_ref, y_ref, h_ref = refs
    L2 = 2 * L
    W = 2 * D_HEAD_DIM
    sgn = 1 - 2 * d

    @pl.when(pl.program_id(1) == 0)
    def _():
        h_ref[...] = jnp.zeros_like(h_ref)

    before = _before(L, d)
    r2, c2 = _iota2(L2, L2)
    order2 = ((c2 & (L - 1)) - (r2 & (L - 1))) * sgn
    strict2 = order2 < 0
    incl2 = order2 <= 0
    eye2 = jnp.where(r2 == c2, 1.0, 0.0)
    rw, cw = _iota2(W, W)
    eye_w = rw == cw
    head0 = lax.broadcasted_iota(jnp.int32, (L, W), 1) < D_HEAD_DIM
    stack = lambda x: jnp.concatenate([jnp.where(head0, x, 0.0), jnp.where(head0, 0.0, x)], axis=0)
    n_levels = int(math.log2(L))

    npair = h_ref.shape[0]
    slices = _chunk_slices(nck, L, d)
    where = [(sl, slice(p * W, (p + 1) * W)) for p in range(npair) for sl in slices]
    r, k, v, a, b, lw = ([ref[sl, lanes] for sl, lanes in where]
                         for ref in (r_ref, k_ref, v_ref, a_ref, b_ref, lw_ref))
    cum = _each(lambda x: _mask_nn(before, x), lw)
    cum_end = _each(lambda c: _end_row(c, d), cum)
    e_neg = _each(lambda c: jnp.exp(-c), cum)
    e_end = _each(lambda ce, c: jnp.exp(ce - c), cum_end, cum)
    at = _each(lambda x, c, w: stack(x * jnp.exp(c - w)), a, cum, lw)
    rt = _each(lambda x, c: stack(x * jnp.exp(c)), r, cum)
    bt = _each(lambda x, e: stack(x * e), b, e_neg)
    kt = _each(lambda x, e: stack(x * e), k, e_neg)
    vs = _each(stack, v)
    gram = _each(lambda p, q, s, t: _nt(jnp.concatenate([p, q], axis=0), jnp.concatenate([s, t], axis=0)),
                 at, rt, bt, kt)
    nmat = _each(lambda g: jnp.where(strict2, g[:L2, :L2], 0.0), gram)
    a_k = _each(lambda g: jnp.where(strict2, g[:L2, L2:], 0.0), gram)
    r_bk = _each(lambda g: jnp.where(jnp.concatenate([incl2, incl2], axis=1), g[L2:, :], 0.0), gram)
    tinv = _each(lambda n: eye2 + n, nmat)
    pw = _each(lambda n: _nn(n, n), nmat)
    for lev in range(1, n_levels):
        if lev < n_levels - 1:
            both = _each(lambda p, t: _nn(p, jnp.concatenate([p, t], axis=1)), pw, tinv)
            pw = _each(lambda x: x[:, :L2], both)
            tinv = _each(lambda t, x: t + x[:, L2:], tinv, both)
        else:
            tinv = _each(lambda p, t: t + _nn(p, t), pw, tinv)
    akv = _each(_nn, a_k, vs)
    wu = _each(lambda t, p, q: _nn(t, jnp.concatenate([p, q], axis=1)), tinv, at, akv)
    zs = _each(lambda x, y: jnp.concatenate([x, jnp.concatenate([jnp.zeros_like(y), y], axis=1)], axis=0), wu, vs)
    qy = _each(_nn, r_bk, zs)
    md = _each(lambda x, y, e, z: _tn(jnp.concatenate([stack(x * e), stack(y * e)], axis=0), z),
               b, k, e_end, zs)
    dec = _each(lambda ce: jnp.sum(jnp.where(eye_w, jnp.broadcast_to(jnp.exp(ce), (W, W)), 0.0),
                                   axis=1, keepdims=True), cum_end)

    def head_mean(x):
        m0 = jnp.sum(jnp.where(head0, x, 0.0), axis=-1, keepdims=True)
        m1 = jnp.sum(jnp.where(head0, 0.0, x), axis=-1, keepdims=True)
        return jnp.where(head0, m0, m1) * (1.0 / D_HEAD_DIM)

    hs = [h_ref[p] for p in range(npair)]
    for i in range(nck):
        for p in range(npair):
            s = p * nck + i
            sl, lanes = where[s]
            ys = _nn(rt[s] + qy[s][:, :W], hs[p]) + qy[s][:, W:]
            y = ys[:L, :] + ys[L:, :]
            hs[p] = dec[s] * hs[p] + _nn(md[s][:, :W], hs[p]) + md[s][:, W:]
            if d == 0:
                y_ref[sl, lanes] = y
            else:
                par = par_ref[:, lanes]
                tot = y0_ref[sl, lanes] + y
                cen = tot - head_mean(tot)
                nrm = cen * lax.rsqrt(head_mean(cen * cen) + RWKV_EPS)
                bonus = head_mean(r[s] * k[s] * par[0:1, :]) * float(D_HEAD_DIM) * v[s]
                y_ref[sl, lanes] = ((nrm * par[1:2, :] + par[2:3, :] + bonus) * g_ref[sl, lanes]).astype(y_ref.dtype)
    for p in range(npair):
        h_ref[p] = hs[p]


def _rwkv_scan(r, k, v, a, b, lw, g, par, n_ctx):
    bsz, t, c = r.shape
    L = RWKV_CHUNK
    tb = SCAN_TIME_BLOCK
    nck, nb, ncb = tb // L, t // tb, n_ctx // tb
    y0 = None
    for d in (0, 1):
        tix = lambda j, d=d: _time_index(d, j, ncb, nb)
        seq = pl.BlockSpec((None, tb, c), lambda i, j: (i, tix(j), 0))
        in_specs = [seq] * 5 + [pl.BlockSpec((None, None, tb, c), lambda i, j, d=d: (d, i, tix(j), 0))]
        args = [r, k, v, a, b, lw]
        if d == 1:
            in_specs += [seq, seq, pl.BlockSpec(par.shape, lambda i, j: (0, 0))]
            args += [y0, g, par]
        y0 = pl.pallas_call(
            functools.partial(_rwkv_kernel, L=L, nck=nck, d=d),
            grid=(bsz, nb),
            in_specs=in_specs,
            out_specs=seq,
            out_shape=jax.ShapeDtypeStruct((bsz, t, c), BF16 if d else F32),
            scratch_shapes=[pltpu.VMEM((c // (2 * D_HEAD_DIM), 2 * D_HEAD_DIM, 2 * D_HEAD_DIM), F32)],
            compiler_params=_scan_params(2),
            name="rwkv7_scan",
        )(*args)
    return y0


def _softplus(x):
    return jnp.maximum(x, 0.0) + jnp.log(1.0 + jnp.exp(-jnp.abs(x)))


def _segment_neighbours(x, before_tile, after_tile):
    n = x.shape[0]
    row = lax.broadcasted_iota(jnp.int32, x.shape, 0)
    prev = jnp.where(row == 0, before_tile, pltpu.roll(x, 1, axis=0))
    nxt = jnp.where(row == n - 1, after_tile, pltpu.roll(x, n - 1, axis=0))
    return prev, nxt


def _even_prep_kernel(p_ref, prev_ref, next_ref, ca_ref, cb_ref, sm_ref,
                      xs_ref, bm_ref, cm_ref, dt_ref, la_ref, q_ref, k_ref, g_ref, *, tiles_per_seq, ctx_tiles):
    j = pl.program_id(0) % tiles_per_seq
    first = jnp.logical_or(j == 0, j == ctx_tiles)
    last = jnp.logical_or(j == ctx_tiles - 1, j == tiles_per_seq - 1)

    def conv_silu(lo, width, taps_ref):
        x = p_ref[:, lo:lo + width].astype(F32)
        prev, nxt = _segment_neighbours(
            x, jnp.where(first, 0.0, prev_ref[HALO_ROWS - 1:HALO_ROWS, lo:lo + width].astype(F32)),
            jnp.where(last, 0.0, next_ref[0:1, lo:lo + width].astype(F32)))
        taps = taps_ref[...]
        return _silu(prev * taps[0:1, :] + x * taps[1:2, :] + nxt * taps[2:3, :] + taps[3:4, :])

    xbc = conv_silu(A_INNER, A_XBC, ca_ref)
    xs_ref[...] = xbc[:, :A_INNER]
    bm_ref[...] = xbc[:, A_INNER:A_INNER + A_GROUPS * A_STATE]
    cm_ref[...] = xbc[:, A_INNER + A_GROUPS * A_STATE:]
    qk = conv_silu(EV_B0, 2 * B_QK, cb_ref)
    q_ref[...] = qk[:, :B_QK]
    k_ref[...] = qk[:, B_QK:]
    sm = sm_ref[...]
    nh2 = 2 * A_HEADS
    dt = _softplus(p_ref[:, A_INNER + A_XBC:A_INNER + A_XBC + nh2].astype(F32) + sm[0:1, :])
    la = dt * sm[1:2, :]
    g0 = EV_B0 + 2 * B_QK + 2 * B_INNER
    gx = p_ref[:, g0:g0 + 4 * B_HEADS].astype(F32)
    g = jnp.where(sm[4:5, :] > 0.5, -_softplus(-(gx + sm[3:4, :])), gx + sm[2:3, :])
    for d in range(2):
        dt_ref[d] = dt[:, d * A_HEADS:(d + 1) * A_HEADS]
        la_ref[d] = la[:, d * A_HEADS:(d + 1) * A_HEADS]
        g_ref[d] = g[:, d * 2 * B_HEADS:(d + 1) * 2 * B_HEADS]


def _gate_lane_order(a):
    h = B_HEADS
    return jnp.concatenate([a[..., :-4 * h], a[..., -4 * h:-3 * h], a[..., -2 * h:-h], a[..., -3 * h:-2 * h], a[..., -h:]],
                           axis=-1)


def _even_mixers(p, n_ctx, ssd_params, mlstm_params):
    conv_w, conv_b, dt_bias, a_log, d_skip, norm_a = ssd_params
    conv_bw, conv_bb, i_bias, f_bias, norm_b = mlstm_params
    b, t, n_pad = p.shape
    p2 = p.reshape(b * t, n_pad)
    tiles = b * t // ROW_TILE
    halo = ROW_TILE // HALO_ROWS
    zeros = jnp.zeros((B_HEADS,), F32)
    sm = jnp.stack([dt_bias.reshape(-1), -jnp.exp(a_log).reshape(-1),
                    jnp.concatenate([i_bias[0], zeros, i_bias[1], zeros]),
                    jnp.concatenate([zeros, f_bias[0], zeros, f_bias[1]]),
                    jnp.concatenate([zeros, zeros + 1.0, zeros, zeros + 1.0])])
    ca = jnp.concatenate([conv_w, conv_b[None, :]], axis=0)
    cb = jnp.concatenate([conv_bw, conv_bb[None, :]], axis=0)
    whole = lambda a: pl.BlockSpec(a.shape, lambda i: (0,) * a.ndim)
    rows = lambda c: pl.BlockSpec((ROW_TILE, c), lambda i: (i, 0))
    per_dir = lambda c: pl.BlockSpec((2, ROW_TILE, c), lambda i: (0, i, 0))
    f32 = lambda *s: jax.ShapeDtypeStruct(s, F32)
    n = b * t
    gn = A_GROUPS * A_STATE
    xs, bm, cm, dt, la, q, k, g = pl.pallas_call(
        functools.partial(_even_prep_kernel, tiles_per_seq=t // ROW_TILE, ctx_tiles=n_ctx // ROW_TILE),
        grid=(tiles,),
        in_specs=[pl.BlockSpec((ROW_TILE, n_pad), lambda i: (i, 0)),
                  pl.BlockSpec((HALO_ROWS, n_pad), lambda i: (jnp.maximum(i * halo - 1, 0), 0)),
                  pl.BlockSpec((HALO_ROWS, n_pad), lambda i: (jnp.minimum((i + 1) * halo, tiles * halo - 1), 0)),
                  whole(ca), whole(cb), whole(sm)],
        out_specs=[rows(A_INNER), rows(gn), rows(gn), per_dir(A_HEADS), per_dir(A_HEADS),
                   rows(B_QK), rows(B_QK), per_dir(2 * B_HEADS)],
        out_shape=[f32(n, A_INNER), f32(n, gn), f32(n, gn), f32(2, n, A_HEADS), f32(2, n, A_HEADS),
                   f32(n, B_QK), f32(n, B_QK), f32(2, n, 2 * B_HEADS)],
        compiler_params=pltpu.CompilerParams(dimension_semantics=("arbitrary",), vmem_limit_bytes=VMEM_LIMIT_BYTES),
        name="even_prep",
    )(p2, p2, p2, ca, cb, sm)
    seq = lambda a: a.reshape(b, t, a.shape[-1])
    seq_d = lambda a: a.reshape(2, b, t, a.shape[-1])
    par = jnp.stack([jnp.repeat(d_skip, A_HEAD_DIM), norm_a])
    fa = _ssd_scan(seq(xs), seq_d(dt), seq_d(la), seq(bm), seq(cm), p, par, n_ctx)
    fb = _mlstm_scan(seq(q), seq(k), seq_d(g), p, norm_b[None, :], n_ctx)
    return fa, fb


def _hgrn2_mixer(p, n_ctx, lb, params):
    f_bias, norm_w = params
    return _gla_scan(p, lb, f_bias, norm_w, n_ctx)


def _rwkv_prep_kernel(p_ref, prev_ref, next_ref, mu_ref, w2_ref, a2_ref, g2_ref, vec_ref,
                      r_ref, k_ref, v_ref, a_ref, b_ref, g_ref, lw_ref, *, tiles_per_seq, ctx_tiles):
    c = D_INNER
    j = pl.program_id(0) % tiles_per_seq
    x = p_ref[:, P_C:].astype(F32)
    n = x.shape[0]
    first = jnp.logical_or(j == 0, j == ctx_tiles)
    last = jnp.logical_or(j == ctx_tiles - 1, j == tiles_per_seq - 1)
    prev, nxt = _segment_neighbours(x, jnp.where(first, 0.0, prev_ref[HALO_ROWS - 1:HALO_ROWS, P_C:].astype(F32)),
                                    jnp.where(last, 0.0, next_ref[0:1, P_C:].astype(F32)))
    x = x + mu_ref[...] * (0.5 * (prev + nxt) - x)
    r, k, v = x[:, :c], x[:, c:2 * c], x[:, 2 * c:3 * c]
    o = 3 * c
    wl = jnp.tanh(x[:, o:o + 2 * D_W_LORA])
    gl = x[:, o + 2 * D_W_LORA:o + 2 * D_W_LORA + D_G_LORA]
    al = x[:, o + 2 * D_W_LORA + D_G_LORA:o + 2 * D_W_LORA + D_G_LORA + D_A_LORA]
    vec = vec_ref[...]
    for d in range(2):
        w = vec[d:d + 1, :] + _nn(wl, w2_ref[d])
        lw_ref[d] = -jnp.exp(-_softplus(-w) - 0.5)
    a = jax.nn.sigmoid(vec[2:3, :] + _nn(al, a2_ref[...]))
    g_ref[...] = _nn(jax.nn.sigmoid(gl), g2_ref[...])
    kx = k * vec[3:4, :]
    sq = kx * kx
    head0 = (lax.broadcasted_iota(jnp.int32, (n, 2 * D_HEAD_DIM), 1) < D_HEAD_DIM)
    sums = []
    for s in range(c // (2 * D_HEAD_DIM)):
        blk = sq[:, s * 2 * D_HEAD_DIM:(s + 1) * 2 * D_HEAD_DIM]
        s0 = jnp.sum(jnp.where(head0, blk, 0.0), axis=-1, keepdims=True)
        s1 = jnp.sum(jnp.where(head0, 0.0, blk), axis=-1, keepdims=True)
        sums.append(jnp.where(head0, s0, s1))
    kk = kx * lax.rsqrt(jnp.maximum(jnp.concatenate(sums, axis=1), 1e-12))
    r_ref[...] = r
    k_ref[...] = k * (1.0 + (a - 1.0) * vec[4:5, :])
    v_ref[...] = v
    a_ref[...] = -kk
    b_ref[...] = kk * a


def _rwkv7_mixer(p, n_ctx, params):
    mu, w0, w2, a0, a2, g2, k_k, k_a, r_k, ln_w, ln_b = params
    b, t, n_pad = p.shape
    c = D_INNER
    width = n_pad - P_C
    p2 = p.reshape(b * t, n_pad)
    tiles = b * t // ROW_TILE
    halo = ROW_TILE // HALO_ROWS
    w2z = jnp.zeros((2, 2 * D_W_LORA, c), F32)
    w2z = w2z.at[0, :D_W_LORA].set(w2[0]).at[1, D_W_LORA:].set(w2[1])
    vec = jnp.stack([w0[0], w0[1], a0, k_k, k_a])
    whole = lambda a: pl.BlockSpec(a.shape, lambda i: (0,) * a.ndim)
    rows = pl.BlockSpec((ROW_TILE, c), lambda i: (i, 0))
    mu_pad = jnp.pad(mu, (0, width - mu.shape[0]))[None, :]
    outs = pl.pallas_call(
        functools.partial(_rwkv_prep_kernel, tiles_per_seq=t // ROW_TILE, ctx_tiles=n_ctx // ROW_TILE),
        grid=(tiles,),
        in_specs=[pl.BlockSpec((ROW_TILE, n_pad), lambda i: (i, 0)),
                  pl.BlockSpec((HALO_ROWS, n_pad), lambda i: (jnp.maximum(i * halo - 1, 0), 0)),
                  pl.BlockSpec((HALO_ROWS, n_pad), lambda i: (jnp.minimum((i + 1) * halo, tiles * halo - 1), 0)),
                  whole(mu_pad), whole(w2z), whole(a2), whole(g2), whole(vec)],
        out_specs=[rows] * 6 + [pl.BlockSpec((2, ROW_TILE, c), lambda i: (0, i, 0))],
        out_shape=[jax.ShapeDtypeStruct((b * t, c), F32)] * 6 + [jax.ShapeDtypeStruct((2, b * t, c), F32)],
        compiler_params=pltpu.CompilerParams(dimension_semantics=("arbitrary",), vmem_limit_bytes=VMEM_LIMIT_BYTES),
        name="rwkv7_prep",
    )(p2, p2, p2, mu_pad, w2z, a2, g2, vec)
    r, k, v, a, bb, g = (u.reshape(b, t, c) for u in outs[:6])
    par = jnp.stack([r_k.reshape(D_INNER), ln_w, ln_b])
    return _rwkv_scan(r, k, v, a, bb, outs[6].reshape(2, b, t, c), g, par, n_ctx)


def _scan_order(n_ctx, rows):
    k = jnp.arange(rows * GRID_W, dtype=jnp.int32)
    ctx = jnp.arange(n_ctx, dtype=jnp.int32)
    to_cm = jnp.concatenate([ctx, n_ctx + (k % rows) * GRID_W + k // rows])
    from_cm = jnp.concatenate([ctx, n_ctx + (k % GRID_W) * rows + k // GRID_W])
    return to_cm, from_cm


def _tile_specs(bsz, t, n_ctx):
    tiles_per_seq, ctx_tiles = t // ROW_TILE, n_ctx // ROW_TILE
    mod_row = lambda i: jnp.where(i % tiles_per_seq < ctx_tiles, bsz, i // tiles_per_seq)
    rows = lambda c: pl.BlockSpec((ROW_TILE, c), lambda i: (i, 0))
    whole = lambda a: pl.BlockSpec(a.shape, lambda i: (0,) * a.ndim)
    mod = pl.BlockSpec((None, 6, D_MODEL), lambda i: (mod_row(i), 0, 0))
    params = pltpu.CompilerParams(dimension_semantics=("arbitrary",), vmem_limit_bytes=VMEM_LIMIT_BYTES)
    return rows, whole, mod, params


def _in_proj_kernel(x_ref, mod_ref, w_ref, o_ref):
    m = mod_ref[...]
    o_ref[...] = _nn(x_ref[...] * (1.0 + m[1:2, :]) + m[0:1, :], w_ref[...]).astype(o_ref.dtype)


def _in_proj(xa, mods, w, bsz, n_ctx):
    n = w.shape[1]
    n_pad = -(-n // LANES) * LANES
    wb = jnp.pad(w.astype(BF16), ((0, 0), (0, n_pad - n)))
    rows, whole, mod, params = _tile_specs(bsz, xa.shape[0] // bsz, n_ctx)
    return pl.pallas_call(
        _in_proj_kernel,
        grid=(xa.shape[0] // ROW_TILE,),
        in_specs=[rows(D_MODEL), mod, whole(wb)],
        out_specs=rows(n_pad),
        out_shape=jax.ShapeDtypeStruct((xa.shape[0], n_pad), BF16),
        compiler_params=params,
        name="in_proj",
    )(xa, mods, wb)


def _norm_rows(z, ln):
    mu = jnp.mean(z, axis=-1, keepdims=True)
    zc = z - mu
    var = jnp.mean(zc * zc, axis=-1, keepdims=True)
    return zc * lax.rsqrt(var + LN_EPS) * ln[0:1, :] + ln[1:2, :]


def _out_proj_kernel(fa_ref, fb_ref, w_ref, x_ref, mod_ref, ln_ref, xo_ref, h_ref, hb_ref):
    ka = fa_ref.shape[1]
    m = mod_ref[...]
    y = _nn(fa_ref[...], w_ref[:ka, :]) + _nn(fb_ref[...], w_ref[ka:, :])
    xn = _norm_rows(DEEPNORM_ALPHA * x_ref[...] + m[2:3, :] * y, ln_ref[...])
    xo_ref[...] = xn
    h = xn * (1.0 + m[4:5, :]) + m[3:4, :]
    h_ref[...] = h
    hb_ref[...] = h.astype(BF16)


def _out_proj(fa, fb, w, xa, mods, ln, bsz, n_ctx):
    t_all, d = xa.shape
    rows, whole, mod, params = _tile_specs(bsz, t_all // bsz, n_ctx)
    wb = w.astype(BF16)
    return pl.pallas_call(
        _out_proj_kernel,
        grid=(t_all // ROW_TILE,),
        in_specs=[rows(fa.shape[1]), rows(fb.shape[1]), whole(wb), rows(d), mod, whole(ln)],
        out_specs=[rows(d), rows(d), rows(d)],
        out_shape=[jax.ShapeDtypeStruct((t_all, d), F32), jax.ShapeDtypeStruct((t_all, d), F32),
                   jax.ShapeDtypeStruct((t_all, d), BF16)],
        compiler_params=params,
        name="out_proj_norm",
    )(fa, fb, wb, xa, mods, ln)


def _ffn_norm_kernel(*refs):
    y_refs, (w_ref, x_ref, mod_ref, ln_ref, xo_ref) = refs[:TOP_K], refs[TOP_K:]
    m = mod_ref[...]
    w = w_ref[...]
    f = functools.reduce(jnp.add, [y_refs[kk][...].astype(F32) * w[:, kk:kk + 1] for kk in range(TOP_K)])
    xo_ref[...] = _norm_rows(DEEPNORM_ALPHA * x_ref[...] + m[5:6, :] * f, ln_ref[...])


def _ffn_norm(ys, wts, xa, mods, ln, bsz, n_ctx):
    t_all, d = xa.shape
    rows, whole, mod, params = _tile_specs(bsz, t_all // bsz, n_ctx)
    return pl.pallas_call(
        _ffn_norm_kernel,
        grid=(t_all // ROW_TILE,),
        in_specs=[rows(d)] * TOP_K + [rows(TOP_K), rows(d), mod, whole(ln)],
        out_specs=rows(d),
        out_shape=jax.ShapeDtypeStruct((t_all, d), F32),
        compiler_params=params,
        name="ffn_residual_norm",
    )(*ys, wts, xa, mods, ln)


def kernel(x, c, ctx, c_ctx, mod_w, mod_b, ln_g, ln_b, ev_w_in, ev_w_out, ssd_conv_w, ssd_conv_b, ssd_dt_bias, ssd_a_log, ssd_d, ssd_norm_w, mlstm_conv_w, mlstm_conv_b, mlstm_i_bias, mlstm_f_bias, mlstm_norm_w, od_w_in, od_w_out, hgrn_lb_logits, hgrn_f_bias, hgrn_norm_w, rwkv_mu, rwkv_w0, rwkv_w2, rwkv_a0, rwkv_a2, rwkv_g2, rwkv_k_k, rwkv_k_a, rwkv_r_k, rwkv_ln_w, rwkv_ln_b, router_w, router_bias, exp_w_gate, exp_w_up, exp_w_down):
    bsz, seq, _ = x.shape
    n_ctx = ctx.shape[1]
    rows = seq // GRID_W
    lb_all = jnp.cumsum(jax.nn.softmax(hgrn_lb_logits.astype(F32), axis=0), axis=0)
    lb_all = lb_all - lb_all[0]
    s_c = jax.nn.silu(c)
    s_cc = jax.nn.silu(c_ctx)
    t = n_ctx + seq
    xa = jnp.concatenate([ctx, x], axis=1).reshape(bsz * t, D_MODEL)
    seq3 = lambda a: a.reshape(bsz, t, a.shape[-1])
    flat = lambda a: a.reshape(bsz * t, a.shape[-1])
    to_cm, from_cm = _scan_order(n_ctx, rows)
    reorder = lambda a, src: flat(jnp.take(seq3(a), src, axis=1))
    for layer in range(DEPTH):
        i = layer // 2
        mods = _matmul(jnp.concatenate([s_c, s_cc[None]], axis=0), mod_w[layer], tm=8, tn=512) + mod_b[layer]
        mods = mods.reshape(bsz + 1, 6, D_MODEL)
        ln = jnp.stack([ln_g[layer], ln_b[layer]], axis=1)
        if layer % 2 == 0:
            w_in = jnp.concatenate([ev_w_in[i][:, :P_A], jnp.zeros((D_MODEL, EV_B0 - P_A), F32),
                                    _gate_lane_order(ev_w_in[i][:, P_A:])], axis=1)
            p = seq3(_in_proj(xa, mods, w_in, bsz, n_ctx))
            fa, fb = _even_mixers(
                p, n_ctx,
                (ssd_conv_w[i], ssd_conv_b[i], ssd_dt_bias[i], ssd_a_log[i], ssd_d[i], ssd_norm_w[i]),
                (mlstm_conv_w[i], mlstm_conv_b[i], mlstm_i_bias[i], mlstm_f_bias[i], mlstm_norm_w[i]))
            fa, fb, w_out = flat(fa), flat(fb), ev_w_out[i]
        else:
            g0, g1 = 3 * D_INNER + 2 * D_W_LORA + D_A_LORA, P_D
            lora_last = lambda a: jnp.concatenate([a[..., :g0 - D_A_LORA], a[..., g0:g1], a[..., g0 - D_A_LORA:g0]], axis=-1)
            w_in = jnp.concatenate([od_w_in[i][:, :P_C], lora_last(od_w_in[i][:, P_C:])], axis=1)
            p = seq3(_in_proj(reorder(xa, to_cm), mods, w_in, bsz, n_ctx))
            fa = _hgrn2_mixer(p, n_ctx, lb_all[layer], (hgrn_f_bias[i], hgrn_norm_w[i]))
            fb = _rwkv7_mixer(p, n_ctx,
                              (lora_last(rwkv_mu[i]), rwkv_w0[i], rwkv_w2[i], rwkv_a0[i], rwkv_a2[i], rwkv_g2[i],
                               rwkv_k_k[i], rwkv_k_a[i], rwkv_r_k[i], rwkv_ln_w[i], rwkv_ln_b[i]))
            fa, fb, w_out = reorder(fa, from_cm), reorder(fb, from_cm), od_w_out[i]
        xa, h, hb = _out_proj(fa, fb, w_out, xa, mods, ln[0], bsz, n_ctx)
        ys, wts = _moe_ffn(h, hb, router_w, router_bias, exp_w_gate, exp_w_up, exp_w_down, layer)
        xa = _ffn_norm(ys, wts, xa, mods, ln[1], bsz, n_ctx)
    return seq3(xa)[:, n_ctx:]
```

```python
import functools
import math

import jax
import jax.numpy as jnp
from jax import lax
from jax.experimental import pallas as pl
from jax.experimental.pallas import tpu as pltpu

F32 = jnp.float32
BF16 = jnp.bfloat16

D_MODEL = 1024
DEPTH = 4
GRID_W = 64
A_HEADS = 8
A_HEAD_DIM = 64
A_INNER = A_HEADS * A_HEAD_DIM
A_GROUPS = 2
A_STATE = 64
A_XBC = A_INNER + 2 * A_GROUPS * A_STATE
B_HEADS = 4
B_QK_DIM = 64
B_V_DIM = 128
B_QK = B_HEADS * B_QK_DIM
B_INNER = B_HEADS * B_V_DIM
MLSTM_EPS = 1e-6
C_HEADS = 4
C_HEAD_DIM = 128
C_INNER = C_HEADS * C_HEAD_DIM
D_HEADS = 8
D_HEAD_DIM = 64
D_INNER = D_HEADS * D_HEAD_DIM
D_W_LORA = 64
D_A_LORA = 64
D_G_LORA = 128
RWKV_EPS = 64e-5
P_A = A_INNER + A_XBC + 2 * A_HEADS
P_B = 2 * B_QK + 2 * B_INNER + 4 * B_HEADS
P_C = 5 * C_INNER
P_D = 3 * D_INNER + 2 * D_W_LORA + D_A_LORA + D_G_LORA
EV_B0 = 3 * A_INNER
N_EXPERTS = 32
N_EXPERT_GROUPS = 8
EXPERTS_PER_GROUP = N_EXPERTS // N_EXPERT_GROUPS
TOP_K = 2
D_EXPERT = 512
MOE_BLOCK = 512
RANK_BLOCK = 512
ROW_TILE = 256
HALO_ROWS = 16
LANES = 128
DEEPNORM_ALPHA = (2 * DEPTH) ** 0.25
LN_EPS = 1e-5
M_INIT = -1e30
NEG_BIG = -1e30

SSD_CHUNK = 128
MLSTM_CHUNK = 128
GLA_CHUNK = 16
GLA_BLOCK = 64
RWKV_CHUNK = 64
SCAN_TIME_BLOCK = 256

VMEM_LIMIT_BYTES = 48 * 1024 * 1024
HI = lax.Precision.HIGHEST


def _dot(a, b, dims, exact):
    if exact:
        return lax.dot_general(a.astype(F32), b.astype(F32), (dims, ((), ())),
                               precision=HI, preferred_element_type=F32)
    return lax.dot_general(a.astype(BF16), b.astype(BF16), (dims, ((), ())),
                           preferred_element_type=F32)


def _nn(a, b, exact=False):
    return _dot(a, b, ((1,), (0,)), exact)


def _nt(a, b, exact=False):
    return _dot(a, b, ((1,), (1,)), exact)


def _tn(a, b, exact=False):
    return _dot(a, b, ((0,), (0,)), exact)


def _iota2(n, m):
    return (lax.broadcasted_iota(jnp.int32, (n, m), 0),
            lax.broadcasted_iota(jnp.int32, (n, m), 1))


def _split3(x):
    x1 = x.astype(BF16)
    r1 = x - x1.astype(F32)
    x2 = r1.astype(BF16)
    x3 = (r1 - x2.astype(F32)).astype(BF16)
    return x1, x2, x3


def _mask_nn(mask, x):
    mb = mask.astype(BF16)
    x1, x2, x3 = _split3(x)
    return _nn(mb, x1) + _nn(mb, x2) + _nn(mb, x3)


def _nn_mask(x, mask):
    mb = mask.astype(BF16)
    x1, x2, x3 = _split3(x)
    return _nn(x1, mb) + _nn(x2, mb) + _nn(x3, mb)


def _each(f, *cols):
    return [f(*xs) for xs in zip(*cols)]


def _silu(x):
    return x * jax.nn.sigmoid(x)


def _mm_kernel(x_ref, w_ref, o_ref, *, exact):
    o_ref[...] = _nn(x_ref[...], w_ref[...], exact)


def _matmul(x, w, tm=512, tn=512, exact=False):
    m, k = x.shape
    n = w.shape[1]
    n_pad = -(-n // tn) * tn
    m_pad = -(-m // tm) * tm
    xb = x if exact else x.astype(BF16)
    wb = w if exact else w.astype(BF16)
    if n_pad != n:
        wb = jnp.pad(wb, ((0, 0), (0, n_pad - n)))
    if m_pad != m:
        xb = jnp.pad(xb, ((0, m_pad - m), (0, 0)))
    out = pl.pallas_call(
        functools.partial(_mm_kernel, exact=exact),
        grid=(n_pad // tn, m_pad // tm),
        in_specs=[pl.BlockSpec((tm, k), lambda j, i: (i, 0)),
                  pl.BlockSpec((k, tn), lambda j, i: (0, j))],
        out_specs=pl.BlockSpec((tm, tn), lambda j, i: (i, j)),
        out_shape=jax.ShapeDtypeStruct((m_pad, n_pad), F32),
        compiler_params=pltpu.CompilerParams(
            dimension_semantics=("arbitrary", "arbitrary"),
            vmem_limit_bytes=VMEM_LIMIT_BYTES),
        name="dense_matmul",
    )(xb, wb)
    return out[:m, :n]


def _moe_kernel(blk_e_ref, n_used_ref, x_ref, wg_ref, wu_ref, wd_ref, o_ref):
    i = pl.program_id(0)

    @pl.when(i < n_used_ref[0])
    def _():
        x = x_ref[...]
        g = _nn(x, wg_ref[...])
        u = _nn(x, wu_ref[...])
        o_ref[...] = _nn(g * jax.nn.sigmoid(g) * u, wd_ref[...]).astype(o_ref.dtype)

    @pl.when(i >= n_used_ref[0])
    def _():
        o_ref[...] = jnp.zeros_like(o_ref)


def _moe_experts(xp, blk_e, n_used, w_gate, w_up, w_down, layer):
    n_rows, d = xp.shape
    n_blocks = n_rows // MOE_BLOCK
    grid_spec = pltpu.PrefetchScalarGridSpec(
        num_scalar_prefetch=2,
        grid=(n_blocks,),
        in_specs=[
            pl.BlockSpec((MOE_BLOCK, d), lambda i, be, nu: (i, 0)),
            pl.BlockSpec((None, None, d, D_EXPERT), lambda i, be, nu: (layer, be[i], 0, 0)),
            pl.BlockSpec((None, None, d, D_EXPERT), lambda i, be, nu: (layer, be[i], 0, 0)),
            pl.BlockSpec((None, None, D_EXPERT, d), lambda i, be, nu: (layer, be[i], 0, 0)),
        ],
        out_specs=pl.BlockSpec((MOE_BLOCK, d), lambda i, be, nu: (i, 0)),
    )
    return pl.pallas_call(
        _moe_kernel,
        grid_spec=grid_spec,
        out_shape=jax.ShapeDtypeStruct((n_rows, d), BF16),
        compiler_params=pltpu.CompilerParams(
            dimension_semantics=("arbitrary",),
            vmem_limit_bytes=VMEM_LIMIT_BYTES),
        name="moe_experts",
    )(blk_e, n_used, xp, w_gate, w_up, w_down)


def _top2(vals):
    m = len(vals)
    m1 = functools.reduce(jnp.maximum, vals)
    i1 = jnp.full_like(m1, float(m - 1))
    for j in reversed(range(m - 1)):
        i1 = jnp.where(vals[j] == m1, float(j), i1)
    rest = [jnp.where(i1 == float(j), -jnp.inf, vals[j]) for j in range(m)]
    m2 = functools.reduce(jnp.maximum, rest)
    i2 = jnp.full_like(m1, float(m - 1))
    for j in reversed(range(m - 1)):
        i2 = jnp.where(rest[j] == m2, float(j), i2)
    return m1, i1, m2, i2


def _router_kernel(h_ref, wt_ref, bias_ref, e_ref, w_ref, rank_ref, cnt_ref, carry_ref):
    tm = h_ref.shape[0]
    ng, per = N_EXPERT_GROUPS, EXPERTS_PER_GROUP

    @pl.when(pl.program_id(0) == 0)
    def _():
        carry_ref[...] = jnp.zeros_like(carry_ref)

    s = jax.nn.sigmoid(_nt(wt_ref[...], h_ref[...], True))
    sb = s + bias_ref[...]
    biased = [sb[j * ng:(j + 1) * ng, :] for j in range(per)]
    plain = [s[j * ng:(j + 1) * ng, :] for j in range(per)]
    m1, _, m2, _ = _top2(biased)
    gsum = m1 + m2
    rows = lax.broadcasted_iota(jnp.int32, (ng, tm), 0).astype(F32)
    gmax = jnp.max(gsum, axis=0, keepdims=True)
    gi = jnp.min(jnp.where(gsum == gmax, rows, float(ng)), axis=0, keepdims=True)
    sel = rows == gi
    pick = lambda v: jnp.sum(jnp.where(sel, v, 0.0), axis=0, keepdims=True)
    in_b = [pick(v) for v in biased]
    in_s = [pick(v) for v in plain]
    _, l1, _, l2 = _top2(in_b)
    w1 = functools.reduce(jnp.add, [jnp.where(l1 == float(j), in_s[j], 0.0) for j in range(per)])
    w2 = functools.reduce(jnp.add, [jnp.where(l2 == float(j), in_s[j], 0.0) for j in range(per)])
    e1 = gi * float(per) + l1
    e2 = gi * float(per) + l2
    wsum = w1 + w2
    e_ref[0:1, :] = e1.astype(jnp.int32)
    e_ref[1:2, :] = e2.astype(jnp.int32)
    w_ref[0:1, :] = w1 / wsum
    w_ref[1:2, :] = w2 / wsum
    row = lax.broadcasted_iota(jnp.int32, (N_EXPERTS, tm), 0)
    experts = ((row % ng) * per + row // ng).astype(F32)
    oh1 = jnp.where(experts == e1, 1.0, 0.0)
    oh2 = jnp.where(experts == e2, 1.0, 0.0)
    oh = oh1 + oh2
    ri, ci = _iota2(tm, tm)
    seen = _nn(oh, jnp.where(ri < ci, 1.0, 0.0)) + carry_ref[...]
    rank_ref[0:1, :] = jnp.sum(oh1 * seen, axis=0, keepdims=True).astype(jnp.int32)
    rank_ref[1:2, :] = jnp.sum(oh2 * seen, axis=0, keepdims=True).astype(jnp.int32)
    carry = carry_ref[...] + jnp.sum(oh, axis=1, keepdims=True)
    carry_ref[...] = carry
    cnt_ref[...] = carry.astype(jnp.int32)


def _route(h, router_w, router_bias):
    t, d = h.shape
    tm = RANK_BLOCK
    kt = lambda dt: jax.ShapeDtypeStruct((TOP_K, t), dt)
    blk = pl.BlockSpec((TOP_K, tm), lambda i: (0, i))
    member_major = lambda a: a.reshape(N_EXPERT_GROUPS, EXPERTS_PER_GROUP, -1).transpose(1, 0, 2).reshape(N_EXPERTS, -1)
    e, w, rank, counts = pl.pallas_call(
        _router_kernel,
        grid=(t // tm,),
        in_specs=[pl.BlockSpec((tm, d), lambda i: (i, 0)),
                  pl.BlockSpec((N_EXPERTS, d), lambda i: (0, 0)),
                  pl.BlockSpec((N_EXPERTS, 1), lambda i: (0, 0))],
        out_specs=[blk, blk, blk, pl.BlockSpec((N_EXPERTS, 1), lambda i: (0, 0))],
        out_shape=[kt(jnp.int32), kt(F32), kt(jnp.int32), jax.ShapeDtypeStruct((N_EXPERTS, 1), jnp.int32)],
        scratch_shapes=[pltpu.VMEM((N_EXPERTS, 1), F32)],
        compiler_params=pltpu.CompilerParams(dimension_semantics=("arbitrary",),
                                             vmem_limit_bytes=VMEM_LIMIT_BYTES),
        name="moe_router",
    )(h, member_major(router_w.T), member_major(router_bias.astype(F32).reshape(N_EXPERTS, 1)))
    counts = counts.reshape(EXPERTS_PER_GROUP, N_EXPERT_GROUPS).T.reshape(N_EXPERTS)
    return e, w, rank, counts


def _moe_ffn(h, hb, router_w, router_bias, w_gate, w_up, w_down, layer):
    t, d = h.shape
    expert, wts, rank, counts = _route(h, router_w, router_bias)
    n_assign = t * TOP_K
    padded = (counts + MOE_BLOCK - 1) // MOE_BLOCK * MOE_BLOCK
    pends = jnp.cumsum(padded)
    pstarts = pends - padded
    start_of = jnp.sum(jnp.where(expert[..., None] == jnp.arange(N_EXPERTS, dtype=jnp.int32), pstarts, 0), axis=-1)
    dest = start_of + rank
    n_blocks = -(-n_assign // MOE_BLOCK) + N_EXPERTS
    slot_token = jnp.arange(n_blocks * MOE_BLOCK, dtype=jnp.int32) % t
    slot_token = slot_token.at[dest.reshape(-1)].set(jnp.tile(jnp.arange(t, dtype=jnp.int32), TOP_K),
                                                      unique_indices=True)
    xp = hb[slot_token]
    blk_start = jnp.arange(n_blocks, dtype=jnp.int32) * MOE_BLOCK
    blk_e = jnp.minimum(jnp.sum(pends[None, :] <= blk_start[:, None], axis=1), N_EXPERTS - 1).astype(jnp.int32)
    n_used = (pends[-1] // MOE_BLOCK).astype(jnp.int32).reshape(1)
    yp = _moe_experts(xp, blk_e, n_used, w_gate, w_up, w_down, layer)
    return [yp[dest[kk]] for kk in range(TOP_K)], wts.T


def _time_index(d, j, n_ctx_blocks, n_blocks):
    if d == 0:
        return j
    return jnp.where(j < n_ctx_blocks, n_ctx_blocks - 1 - j, n_blocks - 1 - j + n_ctx_blocks)


def _end_row(x, d):
    n = x.shape[0]
    return x[0:1, :] if d == 1 else x[n - 1:n, :]


def _before(n, d):
    ri, ci = _iota2(n, n)
    return ci >= ri if d == 1 else ci <= ri


def _chunk_slices(nck, L, d):
    order = range(nck - 1, -1, -1) if d == 1 else range(nck)
    return [slice(i * L, (i + 1) * L) for i in order]


def _scan_params(n_axes):
    return pltpu.CompilerParams(dimension_semantics=("arbitrary",) * n_axes, vmem_limit_bytes=VMEM_LIMIT_BYTES)


def _ssd_kernel(*refs, L, nck, d):
    if d == 0:
        x_ref, dt_ref, lac_ref, lar_ref, b_ref, c_ref, y_ref, st_ref = refs
    else:
        x_ref, dt_ref, lac_ref, lar_ref, b_ref, c_ref, y0_ref, z_ref, par_ref, y_ref, st_ref = refs
    hp, gn = A_INNER, A_GROUPS * A_STATE
    hpg = A_HEADS // A_GROUPS

    @pl.when(pl.program_id(1) == 0)
    def _():
        st_ref[...] = jnp.zeros_like(st_ref)

    before = _before(L, d)
    before_t = _before(L, 1 - d)
    head_of_lane = lax.broadcasted_iota(jnp.int32, (A_HEADS, hp), 1) // A_HEAD_DIM
    expand = head_of_lane == lax.broadcasted_iota(jnp.int32, (A_HEADS, hp), 0)
    group_of_lane = lax.broadcasted_iota(jnp.int32, (L, gn), 1) // A_STATE
    first_of_pair = (lax.broadcasted_iota(jnp.int32, (L, 2 * A_HEAD_DIM), 1) < A_HEAD_DIM)
    own_group = (lax.broadcasted_iota(jnp.int32, (gn, hp), 0) // A_STATE
                 == lax.broadcasted_iota(jnp.int32, (gn, hp), 1) // (A_HEAD_DIM * hpg))
    heads = list(range(A_HEADS))
    st = st_ref[...]
    for sl in _chunk_slices(nck, L, d):
        xs = x_ref[sl, :]
        bm = b_ref[sl, :]
        cm = c_ref[sl, :]
        ccol = _mask_nn(before, lac_ref[sl, :])
        crow = _nn_mask(lar_ref[:, sl], before_t)
        cum = _nn_mask(ccol, expand)
        x = xs * _nn_mask(dt_ref[sl, :], expand)
        end = _end_row(cum, d)
        cbs = [_nt(jnp.where(group_of_lane == g, cm, 0.0), bm) for g in range(A_GROUPS)]
        decay = [jnp.exp(jnp.where(before, ccol[:, h:h + 1] - crow[h:h + 1, :], NEG_BIG)) for h in heads]
        yh = [_nn(cbs[h // hpg] * decay[h], x[:, (h // 2) * 2 * A_HEAD_DIM:(h // 2 + 1) * 2 * A_HEAD_DIM])
              for h in heads]
        pairs = [jnp.where(first_of_pair, yh[2 * p], yh[2 * p + 1]) for p in range(A_HEADS // 2)]
        y = jnp.concatenate(pairs, axis=1) + jnp.exp(cum) * _nn(cm, st)
        st = jnp.exp(end) * st + jnp.where(own_group, _tn(bm, x * jnp.exp(end - cum)), 0.0)
        if d == 0:
            y_ref[sl, :] = y
        else:
            par = par_ref[...]
            u = (y0_ref[sl, :] + y + par[0:1, :] * xs) * _silu(z_ref[sl, :].astype(F32))
            y_ref[sl, :] = (u * lax.rsqrt(jnp.mean(u * u, axis=-1, keepdims=True) + 1e-6)
                            * par[1:2, :]).astype(y_ref.dtype)
    st_ref[...] = st


def _ssd_scan(xs, dt, la, bm, cm, p, par, n_ctx):
    b, t, hp = xs.shape
    L = SSD_CHUNK
    tb = SCAN_TIME_BLOCK
    nck, nb, ncb = tb // L, t // tb, n_ctx // tb
    lar = jnp.swapaxes(la, 2, 3)
    y0 = None
    for d in (0, 1):
        tix = lambda j, d=d: _time_index(d, j, ncb, nb)
        seq = lambda c: pl.BlockSpec((None, tb, c), lambda i, j: (i, tix(j), 0))
        in_specs = [seq(hp),
                    pl.BlockSpec((None, None, tb, A_HEADS), lambda i, j, d=d: (d, i, tix(j), 0)),
                    pl.BlockSpec((None, None, tb, A_HEADS), lambda i, j, d=d: (d, i, tix(j), 0)),
                    pl.BlockSpec((None, None, A_HEADS, tb), lambda i, j, d=d: (d, i, 0, tix(j))),
                    seq(bm.shape[-1]), seq(bm.shape[-1])]
        args = [xs, dt, la, lar, bm, cm]
        if d == 1:
            in_specs += [seq(hp), seq(hp), pl.BlockSpec(par.shape, lambda i, j: (0, 0))]
            args += [y0, p, par]
        y0 = pl.pallas_call(
            functools.partial(_ssd_kernel, L=L, nck=nck, d=d),
            grid=(b, nb),
            in_specs=in_specs,
            out_specs=seq(hp),
            out_shape=jax.ShapeDtypeStruct((b, t, hp), BF16 if d else F32),
            scratch_shapes=[pltpu.VMEM((bm.shape[-1], hp), F32)],
            compiler_params=_scan_params(2),
            name="ssd_scan",
        )(*args)
    return y0


def _mlstm_kernel(*refs, L, d):
    if d == 0:
        q_ref, k_ref, v_ref, gc_ref, gr_ref, h_ref, c_ref, n_ref, m_ref = refs
    else:
        q_ref, k_ref, v_ref, gc_ref, gr_ref, h0_ref, o_ref, par_ref, h_ref, c_ref, n_ref, m_ref = refs
    nh, dk, dv = B_HEADS, B_QK_DIM, B_V_DIM

    @pl.when(pl.program_id(1) == 0)
    def _():
        c_ref[...] = jnp.zeros_like(c_ref)
        n_ref[...] = jnp.zeros_like(n_ref)
        m_ref[...] = jnp.full_like(m_ref, M_INIT)

    before = _before(L, d)
    gc = gc_ref[...]
    gr = gr_ref[...]
    fcol = _mask_nn(before, gc[:, nh:])
    frow = _nn_mask(gr[nh:, :], _before(L, 1 - d))
    lane_head = lax.broadcasted_iota(jnp.int32, (L, 2 * dk), 1) // dk
    heads = list(range(nh))
    slab = [slice((h // 2) * 2 * dk, (h // 2 + 1) * 2 * dk) for h in heads]
    lanes = [slice(h * dv, (h + 1) * dv) for h in heads]
    q = [jnp.where(lane_head == h % 2, q_ref[:, slab[h]], 0.0) * (dk ** -0.5) for h in heads]
    k = [k_ref[:, slab[h]] for h in heads]
    v = [v_ref[:, lanes[h]] for h in heads]
    li_c = [gc[:, h:h + 1] for h in heads]
    li_r = [gr[h:h + 1, :] for h in heads]
    f_c = [fcol[:, h:h + 1] for h in heads]
    f_r = [frow[h:h + 1, :] for h in heads]
    ftot = _each(lambda x: _end_row(x, d), f_c)
    c_prev = [c_ref[h] for h in heads]
    n_prev = [n_ref[h] for h in heads]
    m_prev = [m_ref[h] for h in heads]
    w_end = _each(lambda ft, fc, lc: ft - fc + lc, ftot, f_c, li_c)
    m_loc = _each(lambda w: jnp.max(w, axis=0, keepdims=True), w_end)
    ke = _each(lambda x, w, m: x * jnp.exp(w - m), k, w_end, m_loc)
    c_loc = _each(_tn, ke, v)
    n_loc = _each(lambda x: jnp.sum(x, axis=0, keepdims=True), ke)
    log_d = _each(lambda fc, fr, lr: jnp.where(before, fc - fr + lr, NEG_BIG), f_c, f_r, li_r)
    log_inter = _each(jnp.add, f_c, m_prev)
    m_row = _each(lambda ld, lint: jnp.maximum(jnp.max(ld, axis=-1, keepdims=True), lint), log_d, log_inter)
    s = _each(lambda a, b, ld, mr: _nt(a, b) * jnp.exp(ld - mr), q, k, log_d, m_row)
    inter = _each(lambda lint, mr: jnp.exp(lint - mr), log_inter, m_row)
    num = _each(lambda ss, vv, it, qq, cp: _nn(ss, vv) + it * _nn(qq, cp), s, v, inter, q, c_prev)
    den = _each(lambda ss, it, qq, npv: jnp.sum(ss, axis=-1, keepdims=True)
                + it * jnp.sum(qq * npv, axis=-1, keepdims=True), s, inter, q, n_prev)
    out = _each(lambda nu, de, mr: nu / jnp.maximum(jnp.abs(de), jnp.exp(-mr)), num, den, m_row)
    m_new = _each(lambda ft, mp, ml: jnp.maximum(ft + mp, ml), ftot, m_prev, m_loc)
    sp = _each(lambda ft, mp, mn: jnp.exp(ft + mp - mn), ftot, m_prev, m_new)
    sc = _each(lambda ml, mn: jnp.exp(ml - mn), m_loc, m_new)
    for h in heads:
        c_ref[h] = sp[h] * c_prev[h] + sc[h] * c_loc[h]
        n_ref[h] = sp[h] * n_prev[h] + sc[h] * n_loc[h]
        m_ref[h] = m_new[h]
        if d == 0:
            h_ref[:, lanes[h]] = out[h]
        else:
            tot = h0_ref[:, lanes[h]] + out[h]
            cen = tot - jnp.mean(tot, axis=-1, keepdims=True)
            nrm = cen * lax.rsqrt(jnp.mean(cen * cen, axis=-1, keepdims=True) + MLSTM_EPS)
            h_ref[:, lanes[h]] = (jax.nn.sigmoid(o_ref[:, lanes[h]].astype(F32)) * nrm * par_ref[:, lanes[h]]).astype(h_ref.dtype)


def _mlstm_scan(q, k, gates, p, par, n_ctx):
    b, t, _ = q.shape
    L = MLSTM_CHUNK
    nb, ncb = t // L, n_ctx // L
    v_blk, o_blk = (EV_B0 + 2 * B_QK) // B_INNER, (EV_B0 + 2 * B_QK + B_INNER) // B_INNER
    gates_r = jnp.swapaxes(gates, 2, 3)
    h0 = None
    for d in (0, 1):
        tix = lambda j, d=d: _time_index(d, j, ncb, nb)
        seq = lambda c, blk=0: pl.BlockSpec((None, L, c), lambda i, j: (i, tix(j), blk))
        in_specs = [seq(B_QK), seq(B_QK), seq(B_INNER, v_blk),
                    pl.BlockSpec((None, None, L, 2 * B_HEADS), lambda i, j, d=d: (d, i, tix(j), 0)),
                    pl.BlockSpec((None, None, 2 * B_HEADS, L), lambda i, j, d=d: (d, i, 0, tix(j)))]
        args = [q, k, p, gates, gates_r]
        if d == 1:
            in_specs += [seq(B_INNER), seq(B_INNER, o_blk), pl.BlockSpec(par.shape, lambda i, j: (0, 0))]
            args += [h0, p, par]
        h0 = pl.pallas_call(
            functools.partial(_mlstm_kernel, L=L, d=d),
            grid=(b, nb),
            in_specs=in_specs,
            out_specs=seq(B_INNER),
            out_shape=jax.ShapeDtypeStruct((b, t, B_INNER), BF16 if d else F32),
            scratch_shapes=[pltpu.VMEM((B_HEADS, 2 * B_QK_DIM, B_V_DIM), F32),
                            pltpu.VMEM((B_HEADS, 1, 2 * B_QK_DIM), F32),
                            pltpu.VMEM((B_HEADS, 1, 1), F32)],
            compiler_params=_scan_params(2),
            name="mlstm_scan",
        )(*args)
    return h0


def _gla_kernel(*refs, L, sub, nck, d):
    if d == 0:
        q_ref, f_ref, v_ref, par_ref, y_ref, st_ref = refs
    else:
        q_ref, f_ref, v_ref, par_ref, y0_ref, g_ref, y_ref, st_ref = refs
    hd = C_HEAD_DIM
    nhead = st_ref.shape[0]

    @pl.when(pl.program_id(1) == 0)
    def _():
        st_ref[...] = jnp.zeros_like(st_ref)

    before = _before(L, d)
    rows = lax.broadcasted_iota(jnp.int32, (L, 1), 0)
    slices = _chunk_slices(nck, L, d)
    where = [(sl, slice(h * hd, (h + 1) * hd)) for h in range(nhead) for sl in slices]
    lb = [par_ref[0:1, lanes] for _, lanes in where]
    f_pre = [f_ref[sl, lanes].astype(F32) + par_ref[1:2, lanes] for sl, lanes in where]
    q = [_silu(q_ref[sl, lanes].astype(F32)) for sl, lanes in where]
    v = [v_ref[sl, lanes] for sl, lanes in where]
    lf = _each(lambda b, x: jnp.log(b + (1.0 - b) * jax.nn.sigmoid(x)), lb, f_pre)
    k = _each(lambda b, x: (1.0 - b) * jax.nn.sigmoid(-x), lb, f_pre)
    lam = _each(lambda x: _mask_nn(before, x), lf)
    lam_end = _each(lambda x: _end_row(x, d), lam)
    blocks = [[] for _ in where]
    for c in range(L // sub):
        lo, hi = c * sub, (c + 1) * sub
        upto = rows >= lo if d == 1 else rows < hi
        for s in range(len(where)):
            edge = (hi, hi + 1) if d == 1 else (lo - 1, lo)
            ref = lam[s][edge[0]:edge[1], :] if 0 <= edge[0] < L else jnp.zeros_like(lam_end[s])
            qc = q[s][lo:hi, :] * jnp.exp(lam[s][lo:hi, :] - ref)
            kc = k[s] * jnp.exp(jnp.where(upto, ref - lam[s], NEG_BIG))
            blocks[s].append(_nt(qc, kc))
    att = _each(lambda bl: jnp.where(before, jnp.concatenate(bl, axis=0), 0.0), blocks)
    y_intra = _each(_nn, att, v)
    q_in = _each(lambda x, l: x * jnp.exp(l), q, lam)
    kv = _each(lambda x, y, l, le: _tn(x, y * jnp.exp(le - l)), v, k, lam, lam_end)
    dec = _each(jnp.exp, lam_end)
    for h in range(nhead):
        st = st_ref[h]
        for i in range(nck):
            s = h * nck + i
            sl, lanes = where[s]
            y = y_intra[s] + _nt(q_in[s], st)
            st = st * dec[s] + kv[s]
            if d == 0:
                y_ref[sl, lanes] = y
            else:
                tot = y0_ref[sl, lanes] + y
                nrm = tot * lax.rsqrt(jnp.mean(tot * tot, axis=-1, keepdims=True) + 1e-6)
                y_ref[sl, lanes] = (nrm * par_ref[2:3, lanes] * _silu(g_ref[sl, lanes].astype(F32))).astype(y_ref.dtype)
        st_ref[h] = st


def _gla_scan(p, lb, f_bias, norm_w, n_ctx):
    b, t, _ = p.shape
    c = C_INNER
    L = GLA_BLOCK
    tb = SCAN_TIME_BLOCK
    nck, nb, ncb = tb // L, t // tb, n_ctx // tb
    y0 = None
    for d in (0, 1):
        tix = lambda j, d=d: _time_index(d, j, ncb, nb)
        seq = lambda blk=0: pl.BlockSpec((None, tb, c), lambda i, j: (i, tix(j), blk))
        par = jnp.stack([jnp.broadcast_to(lb, (c,)), f_bias[d], norm_w])
        in_specs = [seq(0), seq(1 + d), seq(3), pl.BlockSpec(par.shape, lambda i, j: (0, 0))]
        args = [p, p, p, par]
        if d == 1:
            in_specs += [seq(), seq(4)]
            args += [y0, p]
        y0 = pl.pallas_call(
            functools.partial(_gla_kernel, L=L, sub=GLA_CHUNK, nck=nck, d=d),
            grid=(b, nb),
            in_specs=in_specs,
            out_specs=seq(),
            out_shape=jax.ShapeDtypeStruct((b, t, c), BF16 if d else F32),
            scratch_shapes=[pltpu.VMEM((c // C_HEAD_DIM, C_HEAD_DIM, C_HEAD_DIM), F32)],
            compiler_params=_scan_params(2),
            name="gla_scan",
        )(*args)
    return y0


def _rwkv_kernel(*refs, L, nck, d):
    if d == 0:
        r_ref, k_ref, v_ref, a_ref, b_ref, lw_ref, y_ref, h_ref = refs
    else:
        r_ref, k_ref, v_ref, a_ref, b_ref, lw_ref, y0_ref, g_ref, par_ref, y_ref, h_ref = refs
    L2 = 2 * L
    W = 2 * D_HEAD_DIM
    sgn = 1 - 2 * d

    @pl.when(pl.program_id(1) == 0)
    def _():
        h_ref[...] = jnp.zeros_like(h_ref)

    before = _before(L, d)
    r2, c2 = _iota2(L2, L2)
    order2 = ((c2 & (L - 1)) - (r2 & (L - 1))) * sgn
    strict2 = order2 < 0
    incl2 = order2 <= 0
    eye2 = jnp.where(r2 == c2, 1.0, 0.0)
    rw, cw = _iota2(W, W)
    eye_w = rw == cw
    head0 = lax.broadcasted_iota(jnp.int32, (L, W), 1) < D_HEAD_DIM
    stack = lambda x: jnp.concatenate([jnp.where(head0, x, 0.0), jnp.where(head0, 0.0, x)], axis=0)
    n_levels = int(math.log2(L))

    npair = h_ref.shape[0]
    slices = _chunk_slices(nck, L, d)
    where = [(sl, slice(p * W, (p + 1) * W)) for p in range(npair) for sl in slices]
    r, k, v, a, b, lw = ([ref[sl, lanes] for sl, lanes in where]
                         for ref in (r_ref, k_ref, v_ref, a_ref, b_ref, lw_ref))
    cum = _each(lambda x: _mask_nn(before, x), lw)
    cum_end = _each(lambda c: _end_row(c, d), cum)
    e_neg = _each(lambda c: jnp.exp(-c), cum)
    e_end = _each(lambda ce, c: jnp.exp(ce - c), cum_end, cum)
    at = _each(lambda x, c, w: stack(x * jnp.exp(c - w)), a, cum, lw)
    rt = _each(lambda x, c: stack(x * jnp.exp(c)), r, cum)
    bt = _each(lambda x, e: stack(x * e), b, e_neg)
    kt = _each(lambda x, e: stack(x * e), k, e_neg)
    vs = _each(stack, v)
    gram = _each(lambda p, q, s, t: _nt(jnp.concatenate([p, q], axis=0), jnp.concatenate([s, t], axis=0)),
                 at, rt, bt, kt)
    nmat = _each(lambda g: jnp.where(strict2, g[:L2, :L2], 0.0), gram)
    a_k = _each(lambda g: jnp.where(strict2, g[:L2, L2:], 0.0), gram)
    r_bk = _each(lambda g: jnp.where(jnp.concatenate([incl2, incl2], axis=1), g[L2:, :], 0.0), gram)
    tinv = _each(lambda n: eye2 + n, nmat)
    pw = _each(lambda n: _nn(n, n), nmat)
    for lev in range(1, n_levels):
        if lev < n_levels - 1:
            both = _each(lambda p, t: _nn(p, jnp.concatenate([p, t], axis=1)), pw, tinv)
            pw = _each(lambda x: x[:, :L2], both)
            tinv = _each(lambda t, x: t + x[:, L2:], tinv, both)
        else:
            tinv = _each(lambda p, t: t + _nn(p, t), pw, tinv)
    akv = _each(_nn, a_k, vs)
    wu = _each(lambda t, p, q: _nn(t, jnp.concatenate([p, q], axis=1)), tinv, at, akv)
    zs = _each(lambda x, y: jnp.concatenate([x, jnp.concatenate([jnp.zeros_like(y), y], axis=1)], axis=0), wu, vs)
    qy = _each(_nn, r_bk, zs)
    md = _each(lambda x, y, e, z: _tn(jnp.concatenate([stack(x * e), stack(y * e)], axis=0), z),
               b, k, e_end, zs)
    dec = _each(lambda ce: jnp.sum(jnp.where(eye_w, jnp.broadcast_to(jnp.exp(ce), (W, W)), 0.0),
                                   axis=1, keepdims=True), cum_end)

    def head_mean(x):
        m0 = jnp.sum(jnp.where(head0, x, 0.0), axis=-1, keepdims=True)
        m1 = jnp.sum(jnp.where(head0, 0.0, x), axis=-1, keepdims=True)
        return jnp.where(head0, m0, m1) * (1.0 / D_HEAD_DIM)

    hs = [h_ref[p] for p in range(npair)]
    for i in range(nck):
        for p in range(npair):
            s = p * nck + i
            sl, lanes = where[s]
            ys = _nn(rt[s] + qy[s][:, :W], hs[p]) + qy[s][:, W:]
            y = ys[:L, :] + ys[L:, :]
            hs[p] = dec[s] * hs[p] + _nn(md[s][:, :W], hs[p]) + md[s][:, W:]
            if d == 0:
                y_ref[sl, lanes] = y
            else:
                par = par_ref[:, lanes]
                tot = y0_ref[sl, lanes] + y
                cen = tot - head_mean(tot)
                nrm = cen * lax.rsqrt(head_mean(cen * cen) + RWKV_EPS)
                bonus = head_mean(r[s] * k[s] * par[0:1, :]) * float(D_HEAD_DIM) * v[s]
                y_ref[sl, lanes] = ((nrm * par[1:2, :] + par[2:3, :] + bonus) * g_ref[sl, lanes]).astype(y_ref.dtype)
    for p in range(npair):
        h_ref[p] = hs[p]


def _rwkv_scan(r, k, v, a, b, lw, g, par, n_ctx):
    bsz, t, c = r.shape
    L = RWKV_CHUNK
    tb = SCAN_TIME_BLOCK
    nck, nb, ncb = tb // L, t // tb, n_ctx // tb
    y0 = None
    for d in (0, 1):
        tix = lambda j, d=d: _time_index(d, j, ncb, nb)
        seq = pl.BlockSpec((None, tb, c), lambda i, j: (i, tix(j), 0))
        in_specs = [seq] * 5 + [pl.BlockSpec((None, None, tb, c), lambda i, j, d=d: (d, i, tix(j), 0))]
        args = [r, k, v, a, b, lw]
        if d == 1:
            in_specs += [seq, seq, pl.BlockSpec(par.shape, lambda i, j: (0, 0))]
            args += [y0, g, par]
        y0 = pl.pallas_call(
            functools.partial(_rwkv_kernel, L=L, nck=nck, d=d),
            grid=(bsz, nb),
            in_specs=in_specs,
            out_specs=seq,
            out_shape=jax.ShapeDtypeStruct((bsz, t, c), BF16 if d else F32),
            scratch_shapes=[pltpu.VMEM((c // (2 * D_HEAD_DIM), 2 * D_HEAD_DIM, 2 * D_HEAD_DIM), F32)],
            compiler_params=_scan_params(2),
            name="rwkv7_scan",
        )(*args)
    return y0


def _softplus(x):
    return jnp.maximum(x, 0.0) + jnp.log(1.0 + jnp.exp(-jnp.abs(x)))


def _segment_neighbours(x, before_tile, after_tile):
    n = x.shape[0]
    row = lax.broadcasted_iota(jnp.int32, x.shape, 0)
    prev = jnp.where(row == 0, before_tile, pltpu.roll(x, 1, axis=0))
    nxt = jnp.where(row == n - 1, after_tile, pltpu.roll(x, n - 1, axis=0))
    return prev, nxt


def _even_prep_kernel(p_ref, prev_ref, next_ref, ca_ref, cb_ref, sm_ref,
                      xs_ref, bm_ref, cm_ref, dt_ref, la_ref, q_ref, k_ref, g_ref, *, tiles_per_seq, ctx_tiles):
    j = pl.program_id(0) % tiles_per_seq
    first = jnp.logical_or(j == 0, j == ctx_tiles)
    last = jnp.logical_or(j == ctx_tiles - 1, j == tiles_per_seq - 1)

    def conv_silu(lo, width, taps_ref):
        x = p_ref[:, lo:lo + width].astype(F32)
        prev, nxt = _segment_neighbours(
            x, jnp.where(first, 0.0, prev_ref[HALO_ROWS - 1:HALO_ROWS, lo:lo + width].astype(F32)),
            jnp.where(last, 0.0, next_ref[0:1, lo:lo + width].astype(F32)))
        taps = taps_ref[...]
        return _silu(prev * taps[0:1, :] + x * taps[1:2, :] + nxt * taps[2:3, :] + taps[3:4, :])

    xbc = conv_silu(A_INNER, A_XBC, ca_ref)
    xs_ref[...] = xbc[:, :A_INNER]
    bm_ref[...] = xbc[:, A_INNER:A_INNER + A_GROUPS * A_STATE]
    cm_ref[...] = xbc[:, A_INNER + A_GROUPS * A_STATE:]
    qk = conv_silu(EV_B0, 2 * B_QK, cb_ref)
    q_ref[...] = qk[:, :B_QK]
    k_ref[...] = qk[:, B_QK:]
    sm = sm_ref[...]
    nh2 = 2 * A_HEADS
    dt = _softplus(p_ref[:, A_INNER + A_XBC:A_INNER + A_XBC + nh2].astype(F32) + sm[0:1, :])
    la = dt * sm[1:2, :]
    g0 = EV_B0 + 2 * B_QK + 2 * B_INNER
    gx = p_ref[:, g0:g0 + 4 * B_HEADS].astype(F32)
    g = jnp.where(sm[4:5, :] > 0.5, -_softplus(-(gx + sm[3:4, :])), gx + sm[2:3, :])
    for d in range(2):
        dt_ref[d] = dt[:, d * A_HEADS:(d + 1) * A_HEADS]
        la_ref[d] = la[:, d * A_HEADS:(d + 1) * A_HEADS]
        g_ref[d] = g[:, d * 2 * B_HEADS:(d + 1) * 2 * B_HEADS]


def _gate_lane_order(a):
    h = B_HEADS
    return jnp.concatenate([a[..., :-4 * h], a[..., -4 * h:-3 * h], a[..., -2 * h:-h], a[..., -3 * h:-2 * h], a[..., -h:]],
                           axis=-1)


def _even_mixers(p, n_ctx, ssd_params, mlstm_params):
    conv_w, conv_b, dt_bias, a_log, d_skip, norm_a = ssd_params
    conv_bw, conv_bb, i_bias, f_bias, norm_b = mlstm_params
    b, t, n_pad = p.shape
    p2 = p.reshape(b * t, n_pad)
    tiles = b * t // ROW_TILE
    halo = ROW_TILE // HALO_ROWS
    zeros = jnp.zeros((B_HEADS,), F32)
    sm = jnp.stack([dt_bias.reshape(-1), -jnp.exp(a_log).reshape(-1),
                    jnp.concatenate([i_bias[0], zeros, i_bias[1], zeros]),
                    jnp.concatenate([zeros, f_bias[0], zeros, f_bias[1]]),
                    jnp.concatenate([zeros, zeros + 1.0, zeros, zeros + 1.0])])
    ca = jnp.concatenate([conv_w, conv_b[None, :]], axis=0)
    cb = jnp.concatenate([conv_bw, conv_bb[None, :]], axis=0)
    whole = lambda a: pl.BlockSpec(a.shape, lambda i: (0,) * a.ndim)
    rows = lambda c: pl.BlockSpec((ROW_TILE, c), lambda i: (i, 0))
    per_dir = lambda c: pl.BlockSpec((2, ROW_TILE, c), lambda i: (0, i, 0))
    f32 = lambda *s: jax.ShapeDtypeStruct(s, F32)
    n = b * t
    gn = A_GROUPS * A_STATE
    xs, bm, cm, dt, la, q, k, g = pl.pallas_call(
        functools.partial(_even_prep_kernel, tiles_per_seq=t // ROW_TILE, ctx_tiles=n_ctx // ROW_TILE),
        grid=(tiles,),
        in_specs=[pl.BlockSpec((ROW_TILE, n_pad), lambda i: (i, 0)),
                  pl.BlockSpec((HALO_ROWS, n_pad), lambda i: (jnp.maximum(i * halo - 1, 0), 0)),
                  pl.BlockSpec((HALO_ROWS, n_pad), lambda i: (jnp.minimum((i + 1) * halo, tiles * halo - 1), 0)),
                  whole(ca), whole(cb), whole(sm)],
        out_specs=[rows(A_INNER), rows(gn), rows(gn), per_dir(A_HEADS), per_dir(A_HEADS),
                   rows(B_QK), rows(B_QK), per_dir(2 * B_HEADS)],
        out_shape=[f32(n, A_INNER), f32(n, gn), f32(n, gn), f32(2, n, A_HEADS), f32(2, n, A_HEADS),
                   f32(n, B_QK), f32(n, B_QK), f32(2, n, 2 * B_HEADS)],
        compiler_params=pltpu.CompilerParams(dimension_semantics=("arbitrary",), vmem_limit_bytes=VMEM_LIMIT_BYTES),
        name="even_prep",
    )(p2, p2, p2, ca, cb, sm)
    seq = lambda a: a.reshape(b, t, a.shape[-1])
    seq_d = lambda a: a.reshape(2, b, t, a.shape[-1])
    par = jnp.stack([jnp.repeat(d_skip, A_HEAD_DIM), norm_a])
    fa = _ssd_scan(seq(xs), seq_d(dt), seq_d(la), seq(bm), seq(cm), p, par, n_ctx)
    fb = _mlstm_scan(seq(q), seq(k), seq_d(g), p, norm_b[None, :], n_ctx)
    return fa, fb


def _hgrn2_mixer(p, n_ctx, lb, params):
    f_bias, norm_w = params
    return _gla_scan(p, lb, f_bias, norm_w, n_ctx)


def _rwkv_prep_kernel(p_ref, prev_ref, next_ref, mu_ref, w2_ref, a2_ref, g2_ref, vec_ref,
                      r_ref, k_ref, v_ref, a_ref, b_ref, g_ref, lw_ref, *, tiles_per_seq, ctx_tiles):
    c = D_INNER
    j = pl.program_id(0) % tiles_per_seq
    x = p_ref[:, P_C:].astype(F32)
    n = x.shape[0]
    first = jnp.logical_or(j == 0, j == ctx_tiles)
    last = jnp.logical_or(j == ctx_tiles - 1, j == tiles_per_seq - 1)
    prev, nxt = _segment_neighbours(x, jnp.where(first, 0.0, prev_ref[HALO_ROWS - 1:HALO_ROWS, P_C:].astype(F32)),
                                    jnp.where(last, 0.0, next_ref[0:1, P_C:].astype(F32)))
    x = x + mu_ref[...] * (0.5 * (prev + nxt) - x)
    r, k, v = x[:, :c], x[:, c:2 * c], x[:, 2 * c:3 * c]
    o = 3 * c
    wl = jnp.tanh(x[:, o:o + 2 * D_W_LORA])
    gl = x[:, o + 2 * D_W_LORA:o + 2 * D_W_LORA + D_G_LORA]
    al = x[:, o + 2 * D_W_LORA + D_G_LORA:o + 2 * D_W_LORA + D_G_LORA + D_A_LORA]
    vec = vec_ref[...]
    for d in range(2):
        w = vec[d:d + 1, :] + _nn(wl, w2_ref[d])
        lw_ref[d] = -jnp.exp(-_softplus(-w) - 0.5)
    a = jax.nn.sigmoid(vec[2:3, :] + _nn(al, a2_ref[...]))
    g_ref[...] = _nn(jax.nn.sigmoid(gl), g2_ref[...])
    kx = k * vec[3:4, :]
    sq = kx * kx
    head0 = (lax.broadcasted_iota(jnp.int32, (n, 2 * D_HEAD_DIM), 1) < D_HEAD_DIM)
    sums = []
    for s in range(c // (2 * D_HEAD_DIM)):
        blk = sq[:, s * 2 * D_HEAD_DIM:(s + 1) * 2 * D_HEAD_DIM]
        s0 = jnp.sum(jnp.where(head0, blk, 0.0), axis=-1, keepdims=True)
        s1 = jnp.sum(jnp.where(head0, 0.0, blk), axis=-1, keepdims=True)
        sums.append(jnp.where(head0, s0, s1))
    kk = kx * lax.rsqrt(jnp.maximum(jnp.concatenate(sums, axis=1), 1e-12))
    r_ref[...] = r
    k_ref[...] = k * (1.0 + (a - 1.0) * vec[4:5, :])
    v_ref[...] = v
    a_ref[...] = -kk
    b_ref[...] = kk * a


def _rwkv7_mixer(p, n_ctx, params):
    mu, w0, w2, a0, a2, g2, k_k, k_a, r_k, ln_w, ln_b = params
    b, t, n_pad = p.shape
    c = D_INNER
    width = n_pad - P_C
    p2 = p.reshape(b * t, n_pad)
    tiles = b * t // ROW_TILE
    halo = ROW_TILE // HALO_ROWS
    w2z = jnp.zeros((2, 2 * D_W_LORA, c), F32)
    w2z = w2z.at[0, :D_W_LORA].set(w2[0]).at[1, D_W_LORA:].set(w2[1])
    vec = jnp.stack([w0[0], w0[1], a0, k_k, k_a])
    whole = lambda a: pl.BlockSpec(a.shape, lambda i: (0,) * a.ndim)
    rows = pl.BlockSpec((ROW_TILE, c), lambda i: (i, 0))
    mu_pad = jnp.pad(mu, (0, width - mu.shape[0]))[None, :]
    outs = pl.pallas_call(
        functools.partial(_rwkv_prep_kernel, tiles_per_seq=t // ROW_TILE, ctx_tiles=n_ctx // ROW_TILE),
        grid=(tiles,),
        in_specs=[pl.BlockSpec((ROW_TILE, n_pad), lambda i: (i, 0)),
                  pl.BlockSpec((HALO_ROWS, n_pad), lambda i: (jnp.maximum(i * halo - 1, 0), 0)),
                  pl.BlockSpec((HALO_ROWS, n_pad), lambda i: (jnp.minimum((i + 1) * halo, tiles * halo - 1), 0)),
                  whole(mu_pad), whole(w2z), whole(a2), whole(g2), whole(vec)],
        out_specs=[rows] * 6 + [pl.BlockSpec((2, ROW_TILE, c), lambda i: (0, i, 0))],
        out_shape=[jax.ShapeDtypeStruct((b * t, c), F32)] * 6 + [jax.ShapeDtypeStruct((2, b * t, c), F32)],
        compiler_params=pltpu.CompilerParams(dimension_semantics=("arbitrary",), vmem_limit_bytes=VMEM_LIMIT_BYTES),
        name="rwkv7_prep",
    )(p2, p2, p2, mu_pad, w2z, a2, g2, vec)
    r, k, v, a, bb, g = (u.reshape(b, t, c) for u in outs[:6])
    par = jnp.stack([r_k.reshape(D_INNER), ln_w, ln_b])
    return _rwkv_scan(r, k, v, a, bb, outs[6].reshape(2, b, t, c), g, par, n_ctx)


def _to_col_major(u, rows):
    b, s, d = u.shape
    return u.reshape(b, rows, GRID_W, d).transpose(0, 2, 1, 3).reshape(b, s, d)


def _from_col_major(u, rows):
    b, s, d = u.shape
    return u.reshape(b, GRID_W, rows, d).transpose(0, 2, 1, 3).reshape(b, s, d)


def _tile_specs(bsz, t, n_ctx):
    tiles_per_seq, ctx_tiles = t // ROW_TILE, n_ctx // ROW_TILE
    mod_row = lambda i: jnp.where(i % tiles_per_seq < ctx_tiles, bsz, i // tiles_per_seq)
    rows = lambda c: pl.BlockSpec((ROW_TILE, c), lambda i: (i, 0))
    whole = lambda a: pl.BlockSpec(a.shape, lambda i: (0,) * a.ndim)
    mod = pl.BlockSpec((None, 6, D_MODEL), lambda i: (mod_row(i), 0, 0))
    params = pltpu.CompilerParams(dimension_semantics=("arbitrary",), vmem_limit_bytes=VMEM_LIMIT_BYTES)
    return rows, whole, mod, params


def _in_proj_kernel(x_ref, mod_ref, w_ref, o_ref):
    m = mod_ref[...]
    o_ref[...] = _nn(x_ref[...] * (1.0 + m[1:2, :]) + m[0:1, :], w_ref[...]).astype(o_ref.dtype)


def _in_proj(xa, mods, w, bsz, n_ctx):
    n = w.shape[1]
    n_pad = -(-n // LANES) * LANES
    wb = jnp.pad(w.astype(BF16), ((0, 0), (0, n_pad - n)))
    rows, whole, mod, params = _tile_specs(bsz, xa.shape[0] // bsz, n_ctx)
    return pl.pallas_call(
        _in_proj_kernel,
        grid=(xa.shape[0] // ROW_TILE,),
        in_specs=[rows(D_MODEL), mod, whole(wb)],
        out_specs=rows(n_pad),
        out_shape=jax.ShapeDtypeStruct((xa.shape[0], n_pad), BF16),
        compiler_params=params,
        name="in_proj",
    )(xa, mods, wb)


def _norm_rows(z, ln):
    mu = jnp.mean(z, axis=-1, keepdims=True)
    zc = z - mu
    var = jnp.mean(zc * zc, axis=-1, keepdims=True)
    return zc * lax.rsqrt(var + LN_EPS) * ln[0:1, :] + ln[1:2, :]


def _out_proj_kernel(fa_ref, fb_ref, w_ref, x_ref, mod_ref, ln_ref, xo_ref, h_ref, hb_ref):
    ka = fa_ref.shape[1]
    m = mod_ref[...]
    y = _nn(fa_ref[...], w_ref[:ka, :]) + _nn(fb_ref[...], w_ref[ka:, :])
    xn = _norm_rows(DEEPNORM_ALPHA * x_ref[...] + m[2:3, :] * y, ln_ref[...])
    xo_ref[...] = xn
    h = xn * (1.0 + m[4:5, :]) + m[3:4, :]
    h_ref[...] = h
    hb_ref[...] = h.astype(BF16)


def _out_proj(fa, fb, w, xa, mods, ln, bsz, n_ctx):
    t_all, d = xa.shape
    rows, whole, mod, params = _tile_specs(bsz, t_all // bsz, n_ctx)
    wb = w.astype(BF16)
    return pl.pallas_call(
        _out_proj_kernel,
        grid=(t_all // ROW_TILE,),
        in_specs=[rows(fa.shape[1]), rows(fb.shape[1]), whole(wb), rows(d), mod, whole(ln)],
        out_specs=[rows(d), rows(d), rows(d)],
        out_shape=[jax.ShapeDtypeStruct((t_all, d), F32), jax.ShapeDtypeStruct((t_all, d), F32),
                   jax.ShapeDtypeStruct((t_all, d), BF16)],
        compiler_params=params,
        name="out_proj_norm",
    )(fa, fb, wb, xa, mods, ln)


def _ffn_norm_kernel(*refs):
    y_refs, (w_ref, x_ref, mod_ref, ln_ref, xo_ref) = refs[:TOP_K], refs[TOP_K:]
    m = mod_ref[...]
    w = w_ref[...]
    f = functools.reduce(jnp.add, [y_refs[kk][...].astype(F32) * w[:, kk:kk + 1] for kk in range(TOP_K)])
    xo_ref[...] = _norm_rows(DEEPNORM_ALPHA * x_ref[...] + m[5:6, :] * f, ln_ref[...])


def _ffn_norm(ys, wts, xa, mods, ln, bsz, n_ctx):
    t_all, d = xa.shape
    rows, whole, mod, params = _tile_specs(bsz, t_all // bsz, n_ctx)
    return pl.pallas_call(
        _ffn_norm_kernel,
        grid=(t_all // ROW_TILE,),
        in_specs=[rows(d)] * TOP_K + [rows(TOP_K), rows(d), mod, whole(ln)],
        out_specs=rows(d),
        out_shape=jax.ShapeDtypeStruct((t_all, d), F32),
        compiler_params=params,
        name="ffn_residual_norm",
    )(*ys, wts, xa, mods, ln)


def kernel(x, c, ctx, c_ctx, mod_w, mod_b, ln_g, ln_b, ev_w_in, ev_w_out, ssd_conv_w, ssd_conv_b, ssd_dt_bias, ssd_a_log, ssd_d, ssd_norm_w, mlstm_conv_w, mlstm_conv_b, mlstm_i_bias, mlstm_f_bias, mlstm_norm_w, od_w_in, od_w_out, hgrn_lb_logits, hgrn_f_bias, hgrn_norm_w, rwkv_mu, rwkv_w0, rwkv_w2, rwkv_a0, rwkv_a2, rwkv_g2, rwkv_k_k, rwkv_k_a, rwkv_r_k, rwkv_ln_w, rwkv_ln_b, router_w, router_bias, exp_w_gate, exp_w_up, exp_w_down):
    bsz, seq, _ = x.shape
    n_ctx = ctx.shape[1]
    rows = seq // GRID_W
    lb_all = jnp.cumsum(jax.nn.softmax(hgrn_lb_logits.astype(F32), axis=0), axis=0)
    lb_all = lb_all - lb_all[0]
    s_c = jax.nn.silu(c)
    s_cc = jax.nn.silu(c_ctx)
    t = n_ctx + seq
    xa = jnp.concatenate([ctx, x], axis=1).reshape(bsz * t, D_MODEL)
    seq3 = lambda a: a.reshape(bsz, t, a.shape[-1])
    flat = lambda a: a.reshape(bsz * t, a.shape[-1])
    lat_order = lambda a, f: flat(jnp.concatenate([seq3(a)[:, :n_ctx], f(seq3(a)[:, n_ctx:], rows)], axis=1))
    for layer in range(DEPTH):
        i = layer // 2
        mods = _matmul(jnp.concatenate([s_c, s_cc[None]], axis=0), mod_w[layer], tm=8, tn=512) + mod_b[layer]
        mods = mods.reshape(bsz + 1, 6, D_MODEL)
        ln = jnp.stack([ln_g[layer], ln_b[layer]], axis=1)
        if layer % 2 == 0:
            w_in = jnp.concatenate([ev_w_in[i][:, :P_A], jnp.zeros((D_MODEL, EV_B0 - P_A), F32),
                                    _gate_lane_order(ev_w_in[i][:, P_A:])], axis=1)
            p = seq3(_in_proj(xa, mods, w_in, bsz, n_ctx))
            fa, fb = _even_mixers(
                p, n_ctx,
                (ssd_conv_w[i], ssd_conv_b[i], ssd_dt_bias[i], ssd_a_log[i], ssd_d[i], ssd_norm_w[i]),
                (mlstm_conv_w[i], mlstm_conv_b[i], mlstm_i_bias[i], mlstm_f_bias[i], mlstm_norm_w[i]))
            fa, fb, w_out = flat(fa), flat(fb), ev_w_out[i]
        else:
            g0, g1 = 3 * D_INNER + 2 * D_W_LORA + D_A_LORA, P_D
            lora_last = lambda a: jnp.concatenate([a[..., :g0 - D_A_LORA], a[..., g0:g1], a[..., g0 - D_A_LORA:g0]], axis=-1)
            w_in = jnp.concatenate([od_w_in[i][:, :P_C], lora_last(od_w_in[i][:, P_C:])], axis=1)
            p = seq3(_in_proj(lat_order(xa, _to_col_major), mods, w_in, bsz, n_ctx))
            fa = _hgrn2_mixer(p, n_ctx, lb_all[layer], (hgrn_f_bias[i], hgrn_norm_w[i]))
            fb = _rwkv7_mixer(p, n_ctx,
                              (lora_last(rwkv_mu[i]), rwkv_w0[i], rwkv_w2[i], rwkv_a0[i], rwkv_a2[i], rwkv_g2[i],
                               rwkv_k_k[i], rwkv_k_a[i], rwkv_r_k[i], rwkv_ln_w[i], rwkv_ln_b[i]))
            fa, fb, w_out = lat_order(fa, _from_col_major), lat_order(fb, _from_col_major), od_w_out[i]
        xa, h, hb = _out_proj(fa, fb, w_out, xa, mods, ln[0], bsz, n_ctx)
        ys, wts = _moe_ffn(h, hb, router_w, router_bias, exp_w_gate, exp_w_up, exp_w_down, layer)
        xa = _ffn_norm(ys, wts, xa, mods, ln[1], bsz, n_ctx)
    return seq3(xa)[:, n_ctx:]
```

```python
import functools
import math

import jax
import jax.numpy as jnp
from jax import lax
from jax.experimental import pallas as pl
from jax.experimental.pallas import tpu as pltpu

F32 = jnp.float32
BF16 = jnp.bfloat16

D_MODEL = 1024
DEPTH = 4
GRID_W = 64
A_HEADS = 8
A_HEAD_DIM = 64
A_INNER = A_HEADS * A_HEAD_DIM
A_GROUPS = 2
A_STATE = 64
A_XBC = A_INNER + 2 * A_GROUPS * A_STATE
B_HEADS = 4
B_QK_DIM = 64
B_V_DIM = 128
B_QK = B_HEADS * B_QK_DIM
B_INNER = B_HEADS * B_V_DIM
MLSTM_EPS = 1e-6
C_HEADS = 4
C_HEAD_DIM = 128
C_INNER = C_HEADS * C_HEAD_DIM
D_HEADS = 8
D_HEAD_DIM = 64
D_INNER = D_HEADS * D_HEAD_DIM
D_W_LORA = 64
D_A_LORA = 64
D_G_LORA = 128
RWKV_EPS = 64e-5
P_A = A_INNER + A_XBC + 2 * A_HEADS
P_B = 2 * B_QK + 2 * B_INNER + 4 * B_HEADS
P_C = 5 * C_INNER
P_D = 3 * D_INNER + 2 * D_W_LORA + D_A_LORA + D_G_LORA
EV_B0 = 3 * A_INNER
N_EXPERTS = 32
N_EXPERT_GROUPS = 8
EXPERTS_PER_GROUP = N_EXPERTS // N_EXPERT_GROUPS
TOP_K = 2
D_EXPERT = 512
MOE_BLOCK = 512
RANK_BLOCK = 512
ROW_TILE = 256
HALO_ROWS = 16
LANES = 128
DEEPNORM_ALPHA = (2 * DEPTH) ** 0.25
LN_EPS = 1e-5
M_INIT = -1e30
NEG_BIG = -1e30

SSD_CHUNK = 128
MLSTM_CHUNK = 128
GLA_CHUNK = 16
GLA_BLOCK = 64
RWKV_CHUNK = 64
SCAN_TIME_BLOCK = 256

VMEM_LIMIT_BYTES = 48 * 1024 * 1024
HI = lax.Precision.HIGHEST


def _dot(a, b, dims, exact):
    if exact:
        return lax.dot_general(a.astype(F32), b.astype(F32), (dims, ((), ())),
                               precision=HI, preferred_element_type=F32)
    return lax.dot_general(a.astype(BF16), b.astype(BF16), (dims, ((), ())),
                           preferred_element_type=F32)


def _nn(a, b, exact=False):
    return _dot(a, b, ((1,), (0,)), exact)


def _nt(a, b, exact=False):
    return _dot(a, b, ((1,), (1,)), exact)


def _tn(a, b, exact=False):
    return _dot(a, b, ((0,), (0,)), exact)


def _iota2(n, m):
    return (lax.broadcasted_iota(jnp.int32, (n, m), 0),
            lax.broadcasted_iota(jnp.int32, (n, m), 1))


def _split3(x):
    x1 = x.astype(BF16)
    r1 = x - x1.astype(F32)
    x2 = r1.astype(BF16)
    x3 = (r1 - x2.astype(F32)).astype(BF16)
    return x1, x2, x3


def _mask_nn(mask, x):
    mb = mask.astype(BF16)
    x1, x2, x3 = _split3(x)
    return _nn(mb, x1) + _nn(mb, x2) + _nn(mb, x3)


def _nn_mask(x, mask):
    mb = mask.astype(BF16)
    x1, x2, x3 = _split3(x)
    return _nn(x1, mb) + _nn(x2, mb) + _nn(x3, mb)


def _each(f, *cols):
    return [f(*xs) for xs in zip(*cols)]


def _silu(x):
    return x * jax.nn.sigmoid(x)


def _mm_kernel(x_ref, w_ref, o_ref, *, exact):
    o_ref[...] = _nn(x_ref[...], w_ref[...], exact)


def _matmul(x, w, tm=512, tn=512, exact=False):
    m, k = x.shape
    n = w.shape[1]
    n_pad = -(-n // tn) * tn
    m_pad = -(-m // tm) * tm
    xb = x if exact else x.astype(BF16)
    wb = w if exact else w.astype(BF16)
    if n_pad != n:
        wb = jnp.pad(wb, ((0, 0), (0, n_pad - n)))
    if m_pad != m:
        xb = jnp.pad(xb, ((0, m_pad - m), (0, 0)))
    out = pl.pallas_call(
        functools.partial(_mm_kernel, exact=exact),
        grid=(n_pad // tn, m_pad // tm),
        in_specs=[pl.BlockSpec((tm, k), lambda j, i: (i, 0)),
                  pl.BlockSpec((k, tn), lambda j, i: (0, j))],
        out_specs=pl.BlockSpec((tm, tn), lambda j, i: (i, j)),
        out_shape=jax.ShapeDtypeStruct((m_pad, n_pad), F32),
        compiler_params=pltpu.CompilerParams(
            dimension_semantics=("arbitrary", "arbitrary"),
            vmem_limit_bytes=VMEM_LIMIT_BYTES),
        name="dense_matmul",
    )(xb, wb)
    return out[:m, :n]


def _moe_kernel(blk_e_ref, n_used_ref, x_ref, wg_ref, wu_ref, wd_ref, o_ref):
    i = pl.program_id(0)

    @pl.when(i < n_used_ref[0])
    def _():
        x = x_ref[...]
        g = _nn(x, wg_ref[...])
        u = _nn(x, wu_ref[...])
        o_ref[...] = _nn(g * jax.nn.sigmoid(g) * u, wd_ref[...]).astype(o_ref.dtype)

    @pl.when(i >= n_used_ref[0])
    def _():
        o_ref[...] = jnp.zeros_like(o_ref)


def _moe_experts(xp, blk_e, n_used, w_gate, w_up, w_down, layer):
    n_rows, d = xp.shape
    n_blocks = n_rows // MOE_BLOCK
    grid_spec = pltpu.PrefetchScalarGridSpec(
        num_scalar_prefetch=2,
        grid=(n_blocks,),
        in_specs=[
            pl.BlockSpec((MOE_BLOCK, d), lambda i, be, nu: (i, 0)),
            pl.BlockSpec((None, None, d, D_EXPERT), lambda i, be, nu: (layer, be[i], 0, 0)),
            pl.BlockSpec((None, None, d, D_EXPERT), lambda i, be, nu: (layer, be[i], 0, 0)),
            pl.BlockSpec((None, None, D_EXPERT, d), lambda i, be, nu: (layer, be[i], 0, 0)),
        ],
        out_specs=pl.BlockSpec((MOE_BLOCK, d), lambda i, be, nu: (i, 0)),
    )
    return pl.pallas_call(
        _moe_kernel,
        grid_spec=grid_spec,
        out_shape=jax.ShapeDtypeStruct((n_rows, d), BF16),
        compiler_params=pltpu.CompilerParams(
            dimension_semantics=("arbitrary",),
            vmem_limit_bytes=VMEM_LIMIT_BYTES),
        name="moe_experts",
    )(blk_e, n_used, xp, w_gate, w_up, w_down)


def _top2(vals):
    m = len(vals)
    m1 = functools.reduce(jnp.maximum, vals)
    i1 = jnp.full_like(m1, float(m - 1))
    for j in reversed(range(m - 1)):
        i1 = jnp.where(vals[j] == m1, float(j), i1)
    rest = [jnp.where(i1 == float(j), -jnp.inf, vals[j]) for j in range(m)]
    m2 = functools.reduce(jnp.maximum, rest)
    i2 = jnp.full_like(m1, float(m - 1))
    for j in reversed(range(m - 1)):
        i2 = jnp.where(rest[j] == m2, float(j), i2)
    return m1, i1, m2, i2


def _router_kernel(h_ref, wt_ref, bias_ref, e_ref, w_ref, rank_ref, cnt_ref, carry_ref):
    tm = h_ref.shape[0]
    ng, per = N_EXPERT_GROUPS, EXPERTS_PER_GROUP

    @pl.when(pl.program_id(0) == 0)
    def _():
        carry_ref[...] = jnp.zeros_like(carry_ref)

    s = jax.nn.sigmoid(_nt(wt_ref[...], h_ref[...], True))
    sb = s + bias_ref[...]
    biased = [sb[j * ng:(j + 1) * ng, :] for j in range(per)]
    plain = [s[j * ng:(j + 1) * ng, :] for j in range(per)]
    m1, _, m2, _ = _top2(biased)
    gsum = m1 + m2
    rows = lax.broadcasted_iota(jnp.int32, (ng, tm), 0).astype(F32)
    gmax = jnp.max(gsum, axis=0, keepdims=True)
    gi = jnp.min(jnp.where(gsum == gmax, rows, float(ng)), axis=0, keepdims=True)
    sel = rows == gi
    pick = lambda v: jnp.sum(jnp.where(sel, v, 0.0), axis=0, keepdims=True)
    in_b = [pick(v) for v in biased]
    in_s = [pick(v) for v in plain]
    _, l1, _, l2 = _top2(in_b)
    w1 = functools.reduce(jnp.add, [jnp.where(l1 == float(j), in_s[j], 0.0) for j in range(per)])
    w2 = functools.reduce(jnp.add, [jnp.where(l2 == float(j), in_s[j], 0.0) for j in range(per)])
    e1 = gi * float(per) + l1
    e2 = gi * float(per) + l2
    wsum = w1 + w2
    e_ref[0:1, :] = e1.astype(jnp.int32)
    e_ref[1:2, :] = e2.astype(jnp.int32)
    w_ref[0:1, :] = w1 / wsum
    w_ref[1:2, :] = w2 / wsum
    row = lax.broadcasted_iota(jnp.int32, (N_EXPERTS, tm), 0)
    experts = ((row % ng) * per + row // ng).astype(F32)
    oh1 = jnp.where(experts == e1, 1.0, 0.0)
    oh2 = jnp.where(experts == e2, 1.0, 0.0)
    oh = oh1 + oh2
    ri, ci = _iota2(tm, tm)
    seen = _nn(oh, jnp.where(ri < ci, 1.0, 0.0)) + carry_ref[...]
    rank_ref[0:1, :] = jnp.sum(oh1 * seen, axis=0, keepdims=True).astype(jnp.int32)
    rank_ref[1:2, :] = jnp.sum(oh2 * seen, axis=0, keepdims=True).astype(jnp.int32)
    carry = carry_ref[...] + jnp.sum(oh, axis=1, keepdims=True)
    carry_ref[...] = carry
    cnt_ref[...] = carry.astype(jnp.int32)


def _route(h, router_w, router_bias):
    t, d = h.shape
    tm = RANK_BLOCK
    kt = lambda dt: jax.ShapeDtypeStruct((TOP_K, t), dt)
    blk = pl.BlockSpec((TOP_K, tm), lambda i: (0, i))
    member_major = lambda a: a.reshape(N_EXPERT_GROUPS, EXPERTS_PER_GROUP, -1).transpose(1, 0, 2).reshape(N_EXPERTS, -1)
    e, w, rank, counts = pl.pallas_call(
        _router_kernel,
        grid=(t // tm,),
        in_specs=[pl.BlockSpec((tm, d), lambda i: (i, 0)),
                  pl.BlockSpec((N_EXPERTS, d), lambda i: (0, 0)),
                  pl.BlockSpec((N_EXPERTS, 1), lambda i: (0, 0))],
        out_specs=[blk, blk, blk, pl.BlockSpec((N_EXPERTS, 1), lambda i: (0, 0))],
        out_shape=[kt(jnp.int32), kt(F32), kt(jnp.int32), jax.ShapeDtypeStruct((N_EXPERTS, 1), jnp.int32)],
        scratch_shapes=[pltpu.VMEM((N_EXPERTS, 1), F32)],
        compiler_params=pltpu.CompilerParams(dimension_semantics=("arbitrary",),
                                             vmem_limit_bytes=VMEM_LIMIT_BYTES),
        name="moe_router",
    )(h, member_major(router_w.T), member_major(router_bias.astype(F32).reshape(N_EXPERTS, 1)))
    counts = counts.reshape(EXPERTS_PER_GROUP, N_EXPERT_GROUPS).T.reshape(N_EXPERTS)
    return e, w, rank, counts


def _moe_ffn(h, hb, router_w, router_bias, w_gate, w_up, w_down, layer):
    t, d = h.shape
    expert, wts, rank, counts = _route(h, router_w, router_bias)
    n_assign = t * TOP_K
    padded = (counts + MOE_BLOCK - 1) // MOE_BLOCK * MOE_BLOCK
    pends = jnp.cumsum(padded)
    pstarts = pends - padded
    start_of = jnp.sum(jnp.where(expert[..., None] == jnp.arange(N_EXPERTS, dtype=jnp.int32), pstarts, 0), axis=-1)
    dest = start_of + rank
    n_blocks = -(-n_assign // MOE_BLOCK) + N_EXPERTS
    slot_token = jnp.arange(n_blocks * MOE_BLOCK, dtype=jnp.int32) % t
    slot_token = slot_token.at[dest.reshape(-1)].set(jnp.tile(jnp.arange(t, dtype=jnp.int32), TOP_K),
                                                      unique_indices=True)
    xp = hb[slot_token]
    blk_start = jnp.arange(n_blocks, dtype=jnp.int32) * MOE_BLOCK
    blk_e = jnp.minimum(jnp.sum(pends[None, :] <= blk_start[:, None], axis=1), N_EXPERTS - 1).astype(jnp.int32)
    n_used = (pends[-1] // MOE_BLOCK).astype(jnp.int32).reshape(1)
    yp = _moe_experts(xp, blk_e, n_used, w_gate, w_up, w_down, layer)
    return [yp[dest[kk]] for kk in range(TOP_K)], wts.T


def _time_index(d, j, n_ctx_blocks, n_blocks):
    if d == 0:
        return j
    return jnp.where(j < n_ctx_blocks, n_ctx_blocks - 1 - j, n_blocks - 1 - j + n_ctx_blocks)


def _end_row(x, d):
    n = x.shape[0]
    return x[0:1, :] if d == 1 else x[n - 1:n, :]


def _before(n, d):
    ri, ci = _iota2(n, n)
    return ci >= ri if d == 1 else ci <= ri


def _chunk_slices(nck, L, d):
    order = range(nck - 1, -1, -1) if d == 1 else range(nck)
    return [slice(i * L, (i + 1) * L) for i in order]


def _scan_params(n_axes):
    return pltpu.CompilerParams(dimension_semantics=("arbitrary",) * n_axes, vmem_limit_bytes=VMEM_LIMIT_BYTES)


def _ssd_kernel(*refs, L, nck, d):
    if d == 0:
        x_ref, dt_ref, lac_ref, lar_ref, b_ref, c_ref, y_ref, st_ref = refs
    else:
        x_ref, dt_ref, lac_ref, lar_ref, b_ref, c_ref, y0_ref, z_ref, par_ref, y_ref, st_ref = refs
    hp, gn = A_INNER, A_GROUPS * A_STATE
    hpg = A_HEADS // A_GROUPS

    @pl.when(pl.program_id(1) == 0)
    def _():
        st_ref[...] = jnp.zeros_like(st_ref)

    before = _before(L, d)
    before_t = _before(L, 1 - d)
    head_of_lane = lax.broadcasted_iota(jnp.int32, (A_HEADS, hp), 1) // A_HEAD_DIM
    expand = head_of_lane == lax.broadcasted_iota(jnp.int32, (A_HEADS, hp), 0)
    group_of_lane = lax.broadcasted_iota(jnp.int32, (L, gn), 1) // A_STATE
    first_of_pair = (lax.broadcasted_iota(jnp.int32, (L, 2 * A_HEAD_DIM), 1) < A_HEAD_DIM)
    own_group = (lax.broadcasted_iota(jnp.int32, (gn, hp), 0) // A_STATE
                 == lax.broadcasted_iota(jnp.int32, (gn, hp), 1) // (A_HEAD_DIM * hpg))
    heads = list(range(A_HEADS))
    st = st_ref[...]
    for sl in _chunk_slices(nck, L, d):
        xs = x_ref[sl, :]
        bm = b_ref[sl, :]
        cm = c_ref[sl, :]
        ccol = _mask_nn(before, lac_ref[sl, :])
        crow = _nn_mask(lar_ref[:, sl], before_t)
        cum = _nn_mask(ccol, expand)
        x = xs * _nn_mask(dt_ref[sl, :], expand)
        end = _end_row(cum, d)
        cbs = [_nt(jnp.where(group_of_lane == g, cm, 0.0), bm) for g in range(A_GROUPS)]
        decay = [jnp.exp(jnp.where(before, ccol[:, h:h + 1] - crow[h:h + 1, :], NEG_BIG)) for h in heads]
        yh = [_nn(cbs[h // hpg] * decay[h], x[:, (h // 2) * 2 * A_HEAD_DIM:(h // 2 + 1) * 2 * A_HEAD_DIM])
              for h in heads]
        pairs = [jnp.where(first_of_pair, yh[2 * p], yh[2 * p + 1]) for p in range(A_HEADS // 2)]
        y = jnp.concatenate(pairs, axis=1) + jnp.exp(cum) * _nn(cm, st)
        st = jnp.exp(end) * st + jnp.where(own_group, _tn(bm, x * jnp.exp(end - cum)), 0.0)
        if d == 0:
            y_ref[sl, :] = y
        else:
            par = par_ref[...]
            u = (y0_ref[sl, :] + y + par[0:1, :] * xs) * _silu(z_ref[sl, :].astype(F32))
            y_ref[sl, :] = (u * lax.rsqrt(jnp.mean(u * u, axis=-1, keepdims=True) + 1e-6)
                            * par[1:2, :]).astype(y_ref.dtype)
    st_ref[...] = st


def _ssd_scan(xs, dt, la, bm, cm, p, par, n_ctx):
    b, t, hp = xs.shape
    L = SSD_CHUNK
    tb = SCAN_TIME_BLOCK
    nck, nb, ncb = tb // L, t // tb, n_ctx // tb
    lar = jnp.swapaxes(la, 2, 3)
    y0 = None
    for d in (0, 1):
        tix = lambda j, d=d: _time_index(d, j, ncb, nb)
        seq = lambda c: pl.BlockSpec((None, tb, c), lambda i, j: (i, tix(j), 0))
        in_specs = [seq(hp),
                    pl.BlockSpec((None, None, tb, A_HEADS), lambda i, j, d=d: (d, i, tix(j), 0)),
                    pl.BlockSpec((None, None, tb, A_HEADS), lambda i, j, d=d: (d, i, tix(j), 0)),
                    pl.BlockSpec((None, None, A_HEADS, tb), lambda i, j, d=d: (d, i, 0, tix(j))),
                    seq(bm.shape[-1]), seq(bm.shape[-1])]
        args = [xs, dt, la, lar, bm, cm]
        if d == 1:
            in_specs += [seq(hp), seq(hp), pl.BlockSpec(par.shape, lambda i, j: (0, 0))]
            args += [y0, p, par]
        y0 = pl.pallas_call(
            functools.partial(_ssd_kernel, L=L, nck=nck, d=d),
            grid=(b, nb),
            in_specs=in_specs,
            out_specs=seq(hp),
            out_shape=jax.ShapeDtypeStruct((b, t, hp), BF16 if d else F32),
            scratch_shapes=[pltpu.VMEM((bm.shape[-1], hp), F32)],
            compiler_params=_scan_params(2),
            name="ssd_scan",
        )(*args)
    return y0


def _mlstm_kernel(*refs, L, nck, d):
    if d == 0:
        q_ref, k_ref, v_ref, gc_ref, gr_ref, h_ref, c_ref, n_ref, m_ref = refs
    else:
        q_ref, k_ref, v_ref, gc_ref, gr_ref, h0_ref, o_ref, par_ref, h_ref, c_ref, n_ref, m_ref = refs
    nh, dk, dv = B_HEADS, B_QK_DIM, B_V_DIM

    @pl.when(pl.program_id(1) == 0)
    def _():
        c_ref[...] = jnp.zeros_like(c_ref)
        n_ref[...] = jnp.zeros_like(n_ref)
        m_ref[...] = jnp.full_like(m_ref, M_INIT)

    before = _before(L, d)
    before_t = _before(L, 1 - d)
    lane_head = lax.broadcasted_iota(jnp.int32, (L, 2 * dk), 1) // dk
    heads = list(range(nh))
    slab = [slice((h // 2) * 2 * dk, (h // 2 + 1) * 2 * dk) for h in heads]
    lanes = [slice(h * dv, (h + 1) * dv) for h in heads]
    state = ([c_ref[h] for h in heads], [n_ref[h] for h in heads], [m_ref[h] for h in heads])
    for sl in _chunk_slices(nck, L, d):
        state = _mlstm_chunk(sl, state, refs, before, before_t, lane_head, slab, lanes, d)
    for h in heads:
        c_ref[h], n_ref[h], m_ref[h] = state[0][h], state[1][h], state[2][h]


def _mlstm_chunk(sl, state, refs, before, before_t, lane_head, slab, lanes, d):
    if d == 0:
        q_ref, k_ref, v_ref, gc_ref, gr_ref, h_ref = refs[:6]
    else:
        q_ref, k_ref, v_ref, gc_ref, gr_ref, h0_ref, o_ref, par_ref, h_ref = refs[:9]
    nh, dk = B_HEADS, B_QK_DIM
    heads = list(range(nh))
    c_prev, n_prev, m_prev = state
    gc = gc_ref[sl, :]
    gr = gr_ref[:, sl]
    fcol = _mask_nn(before, gc[:, nh:])
    frow = _nn_mask(gr[nh:, :], before_t)
    q = [jnp.where(lane_head == h % 2, q_ref[sl, slab[h]], 0.0) * (dk ** -0.5) for h in heads]
    k = [k_ref[sl, slab[h]] for h in heads]
    v = [v_ref[sl, lanes[h]] for h in heads]
    li_c = [gc[:, h:h + 1] for h in heads]
    li_r = [gr[h:h + 1, :] for h in heads]
    f_c = [fcol[:, h:h + 1] for h in heads]
    f_r = [frow[h:h + 1, :] for h in heads]
    ftot = _each(lambda x: _end_row(x, d), f_c)
    w_end = _each(lambda ft, fc, lc: ft - fc + lc, ftot, f_c, li_c)
    m_loc = _each(lambda w: jnp.max(w, axis=0, keepdims=True), w_end)
    ke = _each(lambda x, w, m: x * jnp.exp(w - m), k, w_end, m_loc)
    c_loc = _each(_tn, ke, v)
    n_loc = _each(lambda x: jnp.sum(x, axis=0, keepdims=True), ke)
    log_d = _each(lambda fc, fr, lr: jnp.where(before, fc - fr + lr, NEG_BIG), f_c, f_r, li_r)
    log_inter = _each(jnp.add, f_c, m_prev)
    m_row = _each(lambda ld, lint: jnp.maximum(jnp.max(ld, axis=-1, keepdims=True), lint), log_d, log_inter)
    s = _each(lambda a, b, ld, mr: _nt(a, b) * jnp.exp(ld - mr), q, k, log_d, m_row)
    inter = _each(lambda lint, mr: jnp.exp(lint - mr), log_inter, m_row)
    num = _each(lambda ss, vv, it, qq, cp: _nn(ss, vv) + it * _nn(qq, cp), s, v, inter, q, c_prev)
    den = _each(lambda ss, it, qq, npv: jnp.sum(ss, axis=-1, keepdims=True)
                + it * jnp.sum(qq * npv, axis=-1, keepdims=True), s, inter, q, n_prev)
    out = _each(lambda nu, de, mr: nu / jnp.maximum(jnp.abs(de), jnp.exp(-mr)), num, den, m_row)
    m_new = _each(lambda ft, mp, ml: jnp.maximum(ft + mp, ml), ftot, m_prev, m_loc)
    sp = _each(lambda ft, mp, mn: jnp.exp(ft + mp - mn), ftot, m_prev, m_new)
    sc = _each(lambda ml, mn: jnp.exp(ml - mn), m_loc, m_new)
    for h in heads:
        if d == 0:
            h_ref[sl, lanes[h]] = out[h]
        else:
            tot = h0_ref[sl, lanes[h]] + out[h]
            cen = tot - jnp.mean(tot, axis=-1, keepdims=True)
            nrm = cen * lax.rsqrt(jnp.mean(cen * cen, axis=-1, keepdims=True) + MLSTM_EPS)
            h_ref[sl, lanes[h]] = (jax.nn.sigmoid(o_ref[sl, lanes[h]].astype(F32)) * nrm
                                   * par_ref[:, lanes[h]]).astype(h_ref.dtype)
    return (_each(lambda a, x, b, y: a * x + b * y, sp, c_prev, sc, c_loc),
            _each(lambda a, x, b, y: a * x + b * y, sp, n_prev, sc, n_loc), m_new)


def _mlstm_scan(q, k, gates, p, par, n_ctx):
    b, t, _ = q.shape
    L = MLSTM_CHUNK
    tb = SCAN_TIME_BLOCK
    nck, nb, ncb = tb // L, t // tb, n_ctx // tb
    v_blk, o_blk = (EV_B0 + 2 * B_QK) // B_INNER, (EV_B0 + 2 * B_QK + B_INNER) // B_INNER
    gates_r = jnp.swapaxes(gates, 2, 3)
    h0 = None
    for d in (0, 1):
        tix = lambda j, d=d: _time_index(d, j, ncb, nb)
        seq = lambda c, blk=0: pl.BlockSpec((None, tb, c), lambda i, j: (i, tix(j), blk))
        in_specs = [seq(B_QK), seq(B_QK), seq(B_INNER, v_blk),
                    pl.BlockSpec((None, None, tb, 2 * B_HEADS), lambda i, j, d=d: (d, i, tix(j), 0)),
                    pl.BlockSpec((None, None, 2 * B_HEADS, tb), lambda i, j, d=d: (d, i, 0, tix(j)))]
        args = [q, k, p, gates, gates_r]
        if d == 1:
            in_specs += [seq(B_INNER), seq(B_INNER, o_blk), pl.BlockSpec(par.shape, lambda i, j: (0, 0))]
            args += [h0, p, par]
        h0 = pl.pallas_call(
            functools.partial(_mlstm_kernel, L=L, nck=nck, d=d),
            grid=(b, nb),
            in_specs=in_specs,
            out_specs=seq(B_INNER),
            out_shape=jax.ShapeDtypeStruct((b, t, B_INNER), BF16 if d else F32),
            scratch_shapes=[pltpu.VMEM((B_HEADS, 2 * B_QK_DIM, B_V_DIM), F32),
                            pltpu.VMEM((B_HEADS, 1, 2 * B_QK_DIM), F32),
                            pltpu.VMEM((B_HEADS, 1, 1), F32)],
            compiler_params=_scan_params(2),
            name="mlstm_scan",
        )(*args)
    return h0


def _gla_kernel(*refs, L, sub, nck, d):
    if d == 0:
        q_ref, f_ref, v_ref, par_ref, y_ref, st_ref = refs
    else:
        q_ref, f_ref, v_ref, par_ref, y0_ref, g_ref, y_ref, st_ref = refs
    hd = C_HEAD_DIM
    nhead = st_ref.shape[0]

    @pl.when(pl.program_id(1) == 0)
    def _():
        st_ref[...] = jnp.zeros_like(st_ref)

    before = _before(L, d)
    rows = lax.broadcasted_iota(jnp.int32, (L, 1), 0)
    slices = _chunk_slices(nck, L, d)
    where = [(sl, slice(h * hd, (h + 1) * hd)) for h in range(nhead) for sl in slices]
    lb = [par_ref[0:1, lanes] for _, lanes in where]
    f_pre = [f_ref[sl, lanes].astype(F32) + par_ref[1:2, lanes] for sl, lanes in where]
    q = [_silu(q_ref[sl, lanes].astype(F32)) for sl, lanes in where]
    v = [v_ref[sl, lanes] for sl, lanes in where]
    lf = _each(lambda b, x: jnp.log(b + (1.0 - b) * jax.nn.sigmoid(x)), lb, f_pre)
    k = _each(lambda b, x: (1.0 - b) * jax.nn.sigmoid(-x), lb, f_pre)
    lam = _each(lambda x: _mask_nn(before, x), lf)
    lam_end = _each(lambda x: _end_row(x, d), lam)
    blocks = [[] for _ in where]
    for c in range(L // sub):
        lo, hi = c * sub, (c + 1) * sub
        upto = rows >= lo if d == 1 else rows < hi
        for s in range(len(where)):
            edge = (hi, hi + 1) if d == 1 else (lo - 1, lo)
            ref = lam[s][edge[0]:edge[1], :] if 0 <= edge[0] < L else jnp.zeros_like(lam_end[s])
            qc = q[s][lo:hi, :] * jnp.exp(lam[s][lo:hi, :] - ref)
            kc = k[s] * jnp.exp(jnp.where(upto, ref - lam[s], NEG_BIG))
            blocks[s].append(_nt(qc, kc))
    att = _each(lambda bl: jnp.where(before, jnp.concatenate(bl, axis=0), 0.0), blocks)
    y_intra = _each(_nn, att, v)
    q_in = _each(lambda x, l: x * jnp.exp(l), q, lam)
    kv = _each(lambda x, y, l, le: _tn(x, y * jnp.exp(le - l)), v, k, lam, lam_end)
    dec = _each(jnp.exp, lam_end)
    for h in range(nhead):
        st = st_ref[h]
        for i in range(nck):
            s = h * nck + i
            sl, lanes = where[s]
            y = y_intra[s] + _nt(q_in[s], st)
            st = st * dec[s] + kv[s]
            if d == 0:
                y_ref[sl, lanes] = y
            else:
                tot = y0_ref[sl, lanes] + y
                nrm = tot * lax.rsqrt(jnp.mean(tot * tot, axis=-1, keepdims=True) + 1e-6)
                y_ref[sl, lanes] = (nrm * par_ref[2:3, lanes] * _silu(g_ref[sl, lanes].astype(F32))).astype(y_ref.dtype)
        st_ref[h] = st


def _gla_scan(p, lb, f_bias, norm_w, n_ctx):
    b, t, _ = p.shape
    c = C_INNER
    L = GLA_BLOCK
    tb = SCAN_TIME_BLOCK
    nck, nb, ncb = tb // L, t // tb, n_ctx // tb
    y0 = None
    for d in (0, 1):
        tix = lambda j, d=d: _time_index(d, j, ncb, nb)
        seq = lambda blk=0: pl.BlockSpec((None, tb, c), lambda i, j: (i, tix(j), blk))
        par = jnp.stack([jnp.broadcast_to(lb, (c,)), f_bias[d], norm_w])
        in_specs = [seq(0), seq(1 + d), seq(3), pl.BlockSpec(par.shape, lambda i, j: (0, 0))]
        args = [p, p, p, par]
        if d == 1:
            in_specs += [seq(), seq(4)]
            args += [y0, p]
        y0 = pl.pallas_call(
            functools.partial(_gla_kernel, L=L, sub=GLA_CHUNK, nck=nck, d=d),
            grid=(b, nb),
            in_specs=in_specs,
            out_specs=seq(),
            out_shape=jax.ShapeDtypeStruct((b, t, c), BF16 if d else F32),
            scratch_shapes=[pltpu.VMEM((c // C_HEAD_DIM, C_HEAD_DIM, C_HEAD_DIM), F32)],
            compiler_params=_scan_params(2),
            name="gla_scan",
        )(*args)
    return y0


def _rwkv_kernel(*refs, L, nck, d):
    if d == 0:
        r_ref, k_ref, v_ref, a_ref, b_ref, lw_ref, y_ref, h_ref = refs
    else:
        r_ref, k_ref, v_ref, a_ref, b_ref, lw_ref, y0_ref, g_ref, par_ref, y_ref, h_ref = refs
    L2 = 2 * L
    W = 2 * D_HEAD_DIM
    sgn = 1 - 2 * d

    @pl.when(pl.program_id(1) == 0)
    def _():
        h_ref[...] = jnp.zeros_like(h_ref)

    before = _before(L, d)
    r2, c2 = _iota2(L2, L2)
    order2 = ((c2 & (L - 1)) - (r2 & (L - 1))) * sgn
    strict2 = order2 < 0
    incl2 = order2 <= 0
    eye2 = jnp.where(r2 == c2, 1.0, 0.0)
    rw, cw = _iota2(W, W)
    eye_w = rw == cw
    head0 = lax.broadcasted_iota(jnp.int32, (L, W), 1) < D_HEAD_DIM
    stack = lambda x: jnp.concatenate([jnp.where(head0, x, 0.0), jnp.where(head0, 0.0, x)], axis=0)
    n_levels = int(math.log2(L))

    npair = h_ref.shape[0]
    slices = _chunk_slices(nck, L, d)
    where = [(sl, slice(p * W, (p + 1) * W)) for p in range(npair) for sl in slices]
    r, k, v, a, b, lw = ([ref[sl, lanes] for sl, lanes in where]
                         for ref in (r_ref, k_ref, v_ref, a_ref, b_ref, lw_ref))
    cum = _each(lambda x: _mask_nn(before, x), lw)
    cum_end = _each(lambda c: _end_row(c, d), cum)
    e_neg = _each(lambda c: jnp.exp(-c), cum)
    e_end = _each(lambda ce, c: jnp.exp(ce - c), cum_end, cum)
    at = _each(lambda x, c, w: stack(x * jnp.exp(c - w)), a, cum, lw)
    rt = _each(lambda x, c: stack(x * jnp.exp(c)), r, cum)
    bt = _each(lambda x, e: stack(x * e), b, e_neg)
    kt = _each(lambda x, e: stack(x * e), k, e_neg)
    vs = _each(stack, v)
    gram = _each(lambda p, q, s, t: _nt(jnp.concatenate([p, q], axis=0), jnp.concatenate([s, t], axis=0)),
                 at, rt, bt, kt)
    nmat = _each(lambda g: jnp.where(strict2, g[:L2, :L2], 0.0), gram)
    a_k = _each(lambda g: jnp.where(strict2, g[:L2, L2:], 0.0), gram)
    r_bk = _each(lambda g: jnp.where(jnp.concatenate([incl2, incl2], axis=1), g[L2:, :], 0.0), gram)
    tinv = _each(lambda n: eye2 + n, nmat)
    pw = _each(lambda n: _nn(n, n), nmat)
    for lev in range(1, n_levels):
        if lev < n_levels - 1:
            both = _each(lambda p, t: _nn(p, jnp.concatenate([p, t], axis=1)), pw, tinv)
            pw = _each(lambda x: x[:, :L2], both)
            tinv = _each(lambda t, x: t + x[:, L2:], tinv, both)
        else:
            tinv = _each(lambda p, t: t + _nn(p, t), pw, tinv)
    akv = _each(_nn, a_k, vs)
    wu = _each(lambda t, p, q: _nn(t, jnp.concatenate([p, q], axis=1)), tinv, at, akv)
    zs = _each(lambda x, y: jnp.concatenate([x, jnp.concatenate([jnp.zeros_like(y), y], axis=1)], axis=0), wu, vs)
    qy = _each(_nn, r_bk, zs)
    md = _each(lambda x, y, e, z: _tn(jnp.concatenate([stack(x * e), stack(y * e)], axis=0), z),
               b, k, e_end, zs)
    dec = _each(lambda ce: jnp.sum(jnp.where(eye_w, jnp.broadcast_to(jnp.exp(ce), (W, W)), 0.0),
                                   axis=1, keepdims=True), cum_end)

    def head_mean(x):
        m0 = jnp.sum(jnp.where(head0, x, 0.0), axis=-1, keepdims=True)
        m1 = jnp.sum(jnp.where(head0, 0.0, x), axis=-1, keepdims=True)
        return jnp.where(head0, m0, m1) * (1.0 / D_HEAD_DIM)

    hs = [h_ref[p] for p in range(npair)]
    for i in range(nck):
        for p in range(npair):
            s = p * nck + i
            sl, lanes = where[s]
            ys = _nn(rt[s] + qy[s][:, :W], hs[p]) + qy[s][:, W:]
            y = ys[:L, :] + ys[L:, :]
            hs[p] = dec[s] * hs[p] + _nn(md[s][:, :W], hs[p]) + md[s][:, W:]
            if d == 0:
                y_ref[sl, lanes] = y
            else:
                par = par_ref[:, lanes]
                tot = y0_ref[sl, lanes] + y
                cen = tot - head_mean(tot)
                nrm = cen * lax.rsqrt(head_mean(cen * cen) + RWKV_EPS)
                bonus = head_mean(r[s] * k[s] * par[0:1, :]) * float(D_HEAD_DIM) * v[s]
                y_ref[sl, lanes] = ((nrm * par[1:2, :] + par[2:3, :] + bonus) * g_ref[sl, lanes]).astype(y_ref.dtype)
    for p in range(npair):
        h_ref[p] = hs[p]


def _rwkv_scan(r, k, v, a, b, lw, g, par, n_ctx):
    bsz, t, c = r.shape
    L = RWKV_CHUNK
    tb = SCAN_TIME_BLOCK
    nck, nb, ncb = tb // L, t // tb, n_ctx // tb
    y0 = None
    for d in (0, 1):
        tix = lambda j, d=d: _time_index(d, j, ncb, nb)
        seq = pl.BlockSpec((None, tb, c), lambda i, j: (i, tix(j), 0))
        in_specs = [seq] * 5 + [pl.BlockSpec((None, None, tb, c), lambda i, j, d=d: (d, i, tix(j), 0))]
        args = [r, k, v, a, b, lw]
        if d == 1:
            in_specs += [seq, seq, pl.BlockSpec(par.shape, lambda i, j: (0, 0))]
            args += [y0, g, par]
        y0 = pl.pallas_call(
            functools.partial(_rwkv_kernel, L=L, nck=nck, d=d),
            grid=(bsz, nb),
            in_specs=in_specs,
            out_specs=seq,
            out_shape=jax.ShapeDtypeStruct((bsz, t, c), BF16 if d else F32),
            scratch_shapes=[pltpu.VMEM((c // (2 * D_HEAD_DIM), 2 * D_HEAD_DIM, 2 * D_HEAD_DIM), F32)],
            compiler_params=_scan_params(2),
            name="rwkv7_scan",
        )(*args)
    return y0


def _softplus(x):
    return jnp.maximum(x, 0.0) + jnp.log(1.0 + jnp.exp(-jnp.abs(x)))


def _segment_neighbours(x, before_tile, after_tile):
    n = x.shape[0]
    row = lax.broadcasted_iota(jnp.int32, x.shape, 0)
    prev = jnp.where(row == 0, before_tile, pltpu.roll(x, 1, axis=0))
    nxt = jnp.where(row == n - 1, after_tile, pltpu.roll(x, n - 1, axis=0))
    return prev, nxt


def _even_prep_kernel(p_ref, prev_ref, next_ref, ca_ref, cb_ref, sm_ref,
                      xs_ref, bm_ref, cm_ref, dt_ref, la_ref, q_ref, k_ref, g_ref, *, tiles_per_seq, ctx_tiles):
    j = pl.program_id(0) % tiles_per_seq
    first = jnp.logical_or(j == 0, j == ctx_tiles)
    last = jnp.logical_or(j == ctx_tiles - 1, j == tiles_per_seq - 1)

    def conv_silu(lo, width, taps_ref):
        x = p_ref[:, lo:lo + width].astype(F32)
        prev, nxt = _segment_neighbours(
            x, jnp.where(first, 0.0, prev_ref[HALO_ROWS - 1:HALO_ROWS, lo:lo + width].astype(F32)),
            jnp.where(last, 0.0, next_ref[0:1, lo:lo + width].astype(F32)))
        taps = taps_ref[...]
        return _silu(prev * taps[0:1, :] + x * taps[1:2, :] + nxt * taps[2:3, :] + taps[3:4, :])

    xbc = conv_silu(A_INNER, A_XBC, ca_ref)
    xs_ref[...] = xbc[:, :A_INNER]
    bm_ref[...] = xbc[:, A_INNER:A_INNER + A_GROUPS * A_STATE]
    cm_ref[...] = xbc[:, A_INNER + A_GROUPS * A_STATE:]
    qk = conv_silu(EV_B0, 2 * B_QK, cb_ref)
    q_ref[...] = qk[:, :B_QK]
    k_ref[...] = qk[:, B_QK:]
    sm = sm_ref[...]
    nh2 = 2 * A_HEADS
    dt = _softplus(p_ref[:, A_INNER + A_XBC:A_INNER + A_XBC + nh2].astype(F32) + sm[0:1, :])
    la = dt * sm[1:2, :]
    g0 = EV_B0 + 2 * B_QK + 2 * B_INNER
    gx = p_ref[:, g0:g0 + 4 * B_HEADS].astype(F32)
    g = jnp.where(sm[4:5, :] > 0.5, -_softplus(-(gx + sm[3:4, :])), gx + sm[2:3, :])
    for d in range(2):
        dt_ref[d] = dt[:, d * A_HEADS:(d + 1) * A_HEADS]
        la_ref[d] = la[:, d * A_HEADS:(d + 1) * A_HEADS]
        g_ref[d] = g[:, d * 2 * B_HEADS:(d + 1) * 2 * B_HEADS]


def _gate_lane_order(a):
    h = B_HEADS
    return jnp.concatenate([a[..., :-4 * h], a[..., -4 * h:-3 * h], a[..., -2 * h:-h], a[..., -3 * h:-2 * h], a[..., -h:]],
                           axis=-1)


def _even_mixers(p, n_ctx, ssd_params, mlstm_params):
    conv_w, conv_b, dt_bias, a_log, d_skip, norm_a = ssd_params
    conv_bw, conv_bb, i_bias, f_bias, norm_b = mlstm_params
    b, t, n_pad = p.shape
    p2 = p.reshape(b * t, n_pad)
    tiles = b * t // ROW_TILE
    halo = ROW_TILE // HALO_ROWS
    zeros = jnp.zeros((B_HEADS,), F32)
    sm = jnp.stack([dt_bias.reshape(-1), -jnp.exp(a_log).reshape(-1),
                    jnp.concatenate([i_bias[0], zeros, i_bias[1], zeros]),
                    jnp.concatenate([zeros, f_bias[0], zeros, f_bias[1]]),
                    jnp.concatenate([zeros, zeros + 1.0, zeros, zeros + 1.0])])
    ca = jnp.concatenate([conv_w, conv_b[None, :]], axis=0)
    cb = jnp.concatenate([conv_bw, conv_bb[None, :]], axis=0)
    whole = lambda a: pl.BlockSpec(a.shape, lambda i: (0,) * a.ndim)
    rows = lambda c: pl.BlockSpec((ROW_TILE, c), lambda i: (i, 0))
    per_dir = lambda c: pl.BlockSpec((2, ROW_TILE, c), lambda i: (0, i, 0))
    f32 = lambda *s: jax.ShapeDtypeStruct(s, F32)
    n = b * t
    gn = A_GROUPS * A_STATE
    xs, bm, cm, dt, la, q, k, g = pl.pallas_call(
        functools.partial(_even_prep_kernel, tiles_per_seq=t // ROW_TILE, ctx_tiles=n_ctx // ROW_TILE),
        grid=(tiles,),
        in_specs=[pl.BlockSpec((ROW_TILE, n_pad), lambda i: (i, 0)),
                  pl.BlockSpec((HALO_ROWS, n_pad), lambda i: (jnp.maximum(i * halo - 1, 0), 0)),
                  pl.BlockSpec((HALO_ROWS, n_pad), lambda i: (jnp.minimum((i + 1) * halo, tiles * halo - 1), 0)),
                  whole(ca), whole(cb), whole(sm)],
        out_specs=[rows(A_INNER), rows(gn), rows(gn), per_dir(A_HEADS), per_dir(A_HEADS),
                   rows(B_QK), rows(B_QK), per_dir(2 * B_HEADS)],
        out_shape=[f32(n, A_INNER), f32(n, gn), f32(n, gn), f32(2, n, A_HEADS), f32(2, n, A_HEADS),
                   f32(n, B_QK), f32(n, B_QK), f32(2, n, 2 * B_HEADS)],
        compiler_params=pltpu.CompilerParams(dimension_semantics=("arbitrary",), vmem_limit_bytes=VMEM_LIMIT_BYTES),
        name="even_prep",
    )(p2, p2, p2, ca, cb, sm)
    seq = lambda a: a.reshape(b, t, a.shape[-1])
    seq_d = lambda a: a.reshape(2, b, t, a.shape[-1])
    par = jnp.stack([jnp.repeat(d_skip, A_HEAD_DIM), norm_a])
    fa = _ssd_scan(seq(xs), seq_d(dt), seq_d(la), seq(bm), seq(cm), p, par, n_ctx)
    fb = _mlstm_scan(seq(q), seq(k), seq_d(g), p, norm_b[None, :], n_ctx)
    return fa, fb


def _hgrn2_mixer(p, n_ctx, lb, params):
    f_bias, norm_w = params
    return _gla_scan(p, lb, f_bias, norm_w, n_ctx)


def _rwkv_prep_kernel(p_ref, prev_ref, next_ref, mu_ref, w2_ref, a2_ref, g2_ref, vec_ref,
                      r_ref, k_ref, v_ref, a_ref, b_ref, g_ref, lw_ref, *, tiles_per_seq, ctx_tiles):
    c = D_INNER
    j = pl.program_id(0) % tiles_per_seq
    x = p_ref[:, P_C:].astype(F32)
    n = x.shape[0]
    first = jnp.logical_or(j == 0, j == ctx_tiles)
    last = jnp.logical_or(j == ctx_tiles - 1, j == tiles_per_seq - 1)
    prev, nxt = _segment_neighbours(x, jnp.where(first, 0.0, prev_ref[HALO_ROWS - 1:HALO_ROWS, P_C:].astype(F32)),
                                    jnp.where(last, 0.0, next_ref[0:1, P_C:].astype(F32)))
    x = x + mu_ref[...] * (0.5 * (prev + nxt) - x)
    r, k, v = x[:, :c], x[:, c:2 * c], x[:, 2 * c:3 * c]
    o = 3 * c
    wl = jnp.tanh(x[:, o:o + 2 * D_W_LORA])
    gl = x[:, o + 2 * D_W_LORA:o + 2 * D_W_LORA + D_G_LORA]
    al = x[:, o + 2 * D_W_LORA + D_G_LORA:o + 2 * D_W_LORA + D_G_LORA + D_A_LORA]
    vec = vec_ref[...]
    for d in range(2):
        w = vec[d:d + 1, :] + _nn(wl, w2_ref[d])
        lw_ref[d] = -jnp.exp(-_softplus(-w) - 0.5)
    a = jax.nn.sigmoid(vec[2:3, :] + _nn(al, a2_ref[...]))
    g_ref[...] = _nn(jax.nn.sigmoid(gl), g2_ref[...])
    kx = k * vec[3:4, :]
    sq = kx * kx
    head0 = (lax.broadcasted_iota(jnp.int32, (n, 2 * D_HEAD_DIM), 1) < D_HEAD_DIM)
    sums = []
    for s in range(c // (2 * D_HEAD_DIM)):
        blk = sq[:, s * 2 * D_HEAD_DIM:(s + 1) * 2 * D_HEAD_DIM]
        s0 = jnp.sum(jnp.where(head0, blk, 0.0), axis=-1, keepdims=True)
        s1 = jnp.sum(jnp.where(head0, 0.0, blk), axis=-1, keepdims=True)
        sums.append(jnp.where(head0, s0, s1))
    kk = kx * lax.rsqrt(jnp.maximum(jnp.concatenate(sums, axis=1), 1e-12))
    r_ref[...] = r
    k_ref[...] = k * (1.0 + (a - 1.0) * vec[4:5, :])
    v_ref[...] = v
    a_ref[...] = -kk
    b_ref[...] = kk * a


def _rwkv7_mixer(p, n_ctx, params):
    mu, w0, w2, a0, a2, g2, k_k, k_a, r_k, ln_w, ln_b = params
    b, t, n_pad = p.shape
    c = D_INNER
    width = n_pad - P_C
    p2 = p.reshape(b * t, n_pad)
    tiles = b * t // ROW_TILE
    halo = ROW_TILE // HALO_ROWS
    w2z = jnp.zeros((2, 2 * D_W_LORA, c), F32)
    w2z = w2z.at[0, :D_W_LORA].set(w2[0]).at[1, D_W_LORA:].set(w2[1])
    vec = jnp.stack([w0[0], w0[1], a0, k_k, k_a])
    whole = lambda a: pl.BlockSpec(a.shape, lambda i: (0,) * a.ndim)
    rows = pl.BlockSpec((ROW_TILE, c), lambda i: (i, 0))
    mu_pad = jnp.pad(mu, (0, width - mu.shape[0]))[None, :]
    outs = pl.pallas_call(
        functools.partial(_rwkv_prep_kernel, tiles_per_seq=t // ROW_TILE, ctx_tiles=n_ctx // ROW_TILE),
        grid=(tiles,),
        in_specs=[pl.BlockSpec((ROW_TILE, n_pad), lambda i: (i, 0)),
                  pl.BlockSpec((HALO_ROWS, n_pad), lambda i: (jnp.maximum(i * halo - 1, 0), 0)),
                  pl.BlockSpec((HALO_ROWS, n_pad), lambda i: (jnp.minimum((i + 1) * halo, tiles * halo - 1), 0)),
                  whole(mu_pad), whole(w2z), whole(a2), whole(g2), whole(vec)],
        out_specs=[rows] * 6 + [pl.BlockSpec((2, ROW_TILE, c), lambda i: (0, i, 0))],
        out_shape=[jax.ShapeDtypeStruct((b * t, c), F32)] * 6 + [jax.ShapeDtypeStruct((2, b * t, c), F32)],
        compiler_params=pltpu.CompilerParams(dimension_semantics=("arbitrary",), vmem_limit_bytes=VMEM_LIMIT_BYTES),
        name="rwkv7_prep",
    )(p2, p2, p2, mu_pad, w2z, a2, g2, vec)
    r, k, v, a, bb, g = (u.reshape(b, t, c) for u in outs[:6])
    par = jnp.stack([r_k.reshape(D_INNER), ln_w, ln_b])
    return _rwkv_scan(r, k, v, a, bb, outs[6].reshape(2, b, t, c), g, par, n_ctx)


def _to_col_major(u, rows):
    b, s, d = u.shape
    return u.reshape(b, rows, GRID_W, d).transpose(0, 2, 1, 3).reshape(b, s, d)


def _from_col_major(u, rows):
    b, s, d = u.shape
    return u.reshape(b, GRID_W, rows, d).transpose(0, 2, 1, 3).reshape(b, s, d)


def _tile_specs(bsz, t, n_ctx):
    tiles_per_seq, ctx_tiles = t // ROW_TILE, n_ctx // ROW_TILE
    mod_row = lambda i: jnp.where(i % tiles_per_seq < ctx_tiles, bsz, i // tiles_per_seq)
    rows = lambda c: pl.BlockSpec((ROW_TILE, c), lambda i: (i, 0))
    whole = lambda a: pl.BlockSpec(a.shape, lambda i: (0,) * a.ndim)
    mod = pl.BlockSpec((None, 6, D_MODEL), lambda i: (mod_row(i), 0, 0))
    params = pltpu.CompilerParams(dimension_semantics=("arbitrary",), vmem_limit_bytes=VMEM_LIMIT_BYTES)
    return rows, whole, mod, params


def _in_proj_kernel(x_ref, mod_ref, w_ref, o_ref):
    m = mod_ref[...]
    o_ref[...] = _nn(x_ref[...] * (1.0 + m[1:2, :]) + m[0:1, :], w_ref[...]).astype(o_ref.dtype)


def _in_proj(xa, mods, w, bsz, n_ctx):
    n = w.shape[1]
    n_pad = -(-n // LANES) * LANES
    wb = jnp.pad(w.astype(BF16), ((0, 0), (0, n_pad - n)))
    rows, whole, mod, params = _tile_specs(bsz, xa.shape[0] // bsz, n_ctx)
    return pl.pallas_call(
        _in_proj_kernel,
        grid=(xa.shape[0] // ROW_TILE,),
        in_specs=[rows(D_MODEL), mod, whole(wb)],
        out_specs=rows(n_pad),
        out_shape=jax.ShapeDtypeStruct((xa.shape[0], n_pad), BF16),
        compiler_params=params,
        name="in_proj",
    )(xa, mods, wb)


def _norm_rows(z, ln):
    mu = jnp.mean(z, axis=-1, keepdims=True)
    zc = z - mu
    var = jnp.mean(zc * zc, axis=-1, keepdims=True)
    return zc * lax.rsqrt(var + LN_EPS) * ln[0:1, :] + ln[1:2, :]


def _out_proj_kernel(fa_ref, fb_ref, w_ref, x_ref, mod_ref, ln_ref, xo_ref, h_ref, hb_ref):
    ka = fa_ref.shape[1]
    m = mod_ref[...]
    y = _nn(fa_ref[...], w_ref[:ka, :]) + _nn(fb_ref[...], w_ref[ka:, :])
    xn = _norm_rows(DEEPNORM_ALPHA * x_ref[...] + m[2:3, :] * y, ln_ref[...])
    xo_ref[...] = xn
    h = xn * (1.0 + m[4:5, :]) + m[3:4, :]
    h_ref[...] = h
    hb_ref[...] = h.astype(BF16)


def _out_proj(fa, fb, w, xa, mods, ln, bsz, n_ctx):
    t_all, d = xa.shape
    rows, whole, mod, params = _tile_specs(bsz, t_all // bsz, n_ctx)
    wb = w.astype(BF16)
    return pl.pallas_call(
        _out_proj_kernel,
        grid=(t_all // ROW_TILE,),
        in_specs=[rows(fa.shape[1]), rows(fb.shape[1]), whole(wb), rows(d), mod, whole(ln)],
        out_specs=[rows(d), rows(d), rows(d)],
        out_shape=[jax.ShapeDtypeStruct((t_all, d), F32), jax.ShapeDtypeStruct((t_all, d), F32),
                   jax.ShapeDtypeStruct((t_all, d), BF16)],
        compiler_params=params,
        name="out_proj_norm",
    )(fa, fb, wb, xa, mods, ln)


def _ffn_norm_kernel(*refs):
    y_refs, (w_ref, x_ref, mod_ref, ln_ref, xo_ref) = refs[:TOP_K], refs[TOP_K:]
    m = mod_ref[...]
    w = w_ref[...]
    f = functools.reduce(jnp.add, [y_refs[kk][...].astype(F32) * w[:, kk:kk + 1] for kk in range(TOP_K)])
    xo_ref[...] = _norm_rows(DEEPNORM_ALPHA * x_ref[...] + m[5:6, :] * f, ln_ref[...])


def _ffn_norm(ys, wts, xa, mods, ln, bsz, n_ctx):
    t_all, d = xa.shape
    rows, whole, mod, params = _tile_specs(bsz, t_all // bsz, n_ctx)
    return pl.pallas_call(
        _ffn_norm_kernel,
        grid=(t_all // ROW_TILE,),
        in_specs=[rows(d)] * TOP_K + [rows(TOP_K), rows(d), mod, whole(ln)],
        out_specs=rows(d),
        out_shape=jax.ShapeDtypeStruct((t_all, d), F32),
        compiler_params=params,
        name="ffn_residual_norm",
    )(*ys, wts, xa, mods, ln)


def kernel(x, c, ctx, c_ctx, mod_w, mod_b, ln_g, ln_b, ev_w_in, ev_w_out, ssd_conv_w, ssd_conv_b, ssd_dt_bias, ssd_a_log, ssd_d, ssd_norm_w, mlstm_conv_w, mlstm_conv_b, mlstm_i_bias, mlstm_f_bias, mlstm_norm_w, od_w_in, od_w_out, hgrn_lb_logits, hgrn_f_bias, hgrn_norm_w, rwkv_mu, rwkv_w0, rwkv_w2, rwkv_a0, rwkv_a2, rwkv_g2, rwkv_k_k, rwkv_k_a, rwkv_r_k, rwkv_ln_w, rwkv_ln_b, router_w, router_bias, exp_w_gate, exp_w_up, exp_w_down):
    bsz, seq, _ = x.shape
    n_ctx = ctx.shape[1]
    rows = seq // GRID_W
    lb_all = jnp.cumsum(jax.nn.softmax(hgrn_lb_logits.astype(F32), axis=0), axis=0)
    lb_all = lb_all - lb_all[0]
    s_c = jax.nn.silu(c)
    s_cc = jax.nn.silu(c_ctx)
    t = n_ctx + seq
    xa = jnp.concatenate([ctx, x], axis=1).reshape(bsz * t, D_MODEL)
    seq3 = lambda a: a.reshape(bsz, t, a.shape[-1])
    flat = lambda a: a.reshape(bsz * t, a.shape[-1])
    lat_order = lambda a, f: flat(jnp.concatenate([seq3(a)[:, :n_ctx], f(seq3(a)[:, n_ctx:], rows)], axis=1))
    for layer in range(DEPTH):
        i = layer // 2
        mods = _matmul(jnp.concatenate([s_c, s_cc[None]], axis=0), mod_w[layer], tm=8, tn=512) + mod_b[layer]
        mods = mods.reshape(bsz + 1, 6, D_MODEL)
        ln = jnp.stack([ln_g[layer], ln_b[layer]], axis=1)
        if layer % 2 == 0:
            w_in = jnp.concatenate([ev_w_in[i][:, :P_A], jnp.zeros((D_MODEL, EV_B0 - P_A), F32),
                                    _gate_lane_order(ev_w_in[i][:, P_A:])], axis=1)
            p = seq3(_in_proj(xa, mods, w_in, bsz, n_ctx))
            fa, fb = _even_mixers(
                p, n_ctx,
                (ssd_conv_w[i], ssd_conv_b[i], ssd_dt_bias[i], ssd_a_log[i], ssd_d[i], ssd_norm_w[i]),
                (mlstm_conv_w[i], mlstm_conv_b[i], mlstm_i_bias[i], mlstm_f_bias[i], mlstm_norm_w[i]))
            fa, fb, w_out = flat(fa), flat(fb), ev_w_out[i]
        else:
            g0, g1 = 3 * D_INNER + 2 * D_W_LORA + D_A_LORA, P_D
            lora_last = lambda a: jnp.concatenate([a[..., :g0 - D_A_LORA], a[..., g0:g1], a[..., g0 - D_A_LORA:g0]], axis=-1)
            w_in = jnp.concatenate([od_w_in[i][:, :P_C], lora_last(od_w_in[i][:, P_C:])], axis=1)
            p = seq3(_in_proj(lat_order(xa, _to_col_major), mods, w_in, bsz, n_ctx))
            fa = _hgrn2_mixer(p, n_ctx, lb_all[layer], (hgrn_f_bias[i], hgrn_norm_w[i]))
            fb = _rwkv7_mixer(p, n_ctx,
                              (lora_last(rwkv_mu[i]), rwkv_w0[i], rwkv_w2[i], rwkv_a0[i], rwkv_a2[i], rwkv_g2[i],
                               rwkv_k_k[i], rwkv_k_a[i], rwkv_r_k[i], rwkv_ln_w[i], rwkv_ln_b[i]))
            fa, fb, w_out = lat_order(fa, _from_col_major), lat_order(fb, _from_col_major), od_w_out[i]
        xa, h, hb = _out_proj(fa, fb, w_out, xa, mods, ln[0], bsz, n_ctx)
        ys, wts = _moe_ffn(h, hb, router_w, router_bias, exp_w_gate, exp_w_up, exp_w_down, layer)
        xa = _ffn_norm(ys, wts, xa, mods, ln[1], bsz, n_ctx)
    return seq3(xa)[:, n_ctx:]
```
